```python
import jax, jax.numpy as jnp
from jax import lax
import numpy as np

D_MODEL = 1024
BATCH = 16
SEQ = 256
DEPTH = 2
DEC_BATCH = 2
DEC_SEQ = 2048
PAST_LEN = 512

GRID_W = 64
D_MIX = D_MODEL
HEAD_DIM = 64
N_HEADS = 8
N_KV_HEADS = 2
GQA_GROUP = N_HEADS // N_KV_HEADS
ATTN_W = N_HEADS * HEAD_DIM
KV_W = N_KV_HEADS * HEAD_DIM
WINDOW = 128
BLOCK = 128
ROPE_THETA = 10000.0
CONV_W = D_MIX // 4
CONV_K = 31
FNET_GROUPS = 4
FNET_W = D_MIX // 4
FNET_GW = FNET_W // FNET_GROUPS
IN_WIDTH = ATTN_W + 2 * KV_W + 2 * CONV_W + FNET_W
SPLITS = (ATTN_W, ATTN_W + KV_W, ATTN_W + 2 * KV_W, ATTN_W + 2 * KV_W + 2 * CONV_W)
D_FF = 256 * ((8 * D_MODEL // 3 + 255) // 256)
N_EXPERTS = 8
TOP_K = 2
D_FF_EXPERT = 7 * D_MODEL // 2
N_DENSE = (DEPTH + 1) // 2
N_MOE = DEPTH // 2
EPS = 1e-6
NEG = -1e30

kernel_name = 'hybrid_dit_prefix_step'


def rmsnorm(x, g):
    xf = x.astype(jnp.float32)
    y = xf * lax.rsqrt(jnp.mean(xf * xf, axis=-1, keepdims=True) + EPS)
    return (y * g.astype(jnp.float32)).astype(x.dtype)


def layernorm(x, g, b):
    xf = x.astype(jnp.float32)
    mu = jnp.mean(xf, axis=-1, keepdims=True)
    var = jnp.mean(jnp.square(xf - mu), axis=-1, keepdims=True)
    y = (xf - mu) * lax.rsqrt(var + EPS) * g.astype(jnp.float32) + b.astype(jnp.float32)
    return y.astype(x.dtype)


def axial_rope_angles(rows):
    n_freq = HEAD_DIM // 4
    inv = ROPE_THETA ** (-jnp.arange(n_freq, dtype=jnp.float32) / n_freq)
    gr, gc = jnp.meshgrid(jnp.arange(rows, dtype=jnp.float32),
                          jnp.arange(GRID_W, dtype=jnp.float32), indexing='ij')
    return gr.reshape(-1)[:, None] * inv, gc.reshape(-1)[:, None] * inv


def rope_half(x, ang):
    cos = jnp.cos(ang)[None, :, None, :].astype(x.dtype)
    sin = jnp.sin(ang)[None, :, None, :].astype(x.dtype)
    x1, x2 = jnp.split(x, 2, axis=-1)
    return jnp.concatenate([x1 * cos - x2 * sin, x1 * sin + x2 * cos], axis=-1)


def axial_rope(x, ang_row, ang_col):
    xr, xc = jnp.split(x, 2, axis=-1)
    return jnp.concatenate([rope_half(xr, ang_row), rope_half(xc, ang_col)], axis=-1)


def sink_softmax(s, sink):
    m = jnp.maximum(jnp.max(s, axis=-1, keepdims=True), sink)
    p = jnp.exp(s - m)
    return p / (jnp.sum(p, axis=-1, keepdims=True) + jnp.exp(sink - m))


def context_attention(q, k, v, sink):
    B, S = q.shape[:2]
    nb = S // BLOCK
    scale = HEAD_DIM ** -0.5
    qb = q.reshape(B, nb, BLOCK, N_KV_HEADS, GQA_GROUP, HEAD_DIM).transpose(1, 0, 2, 3, 4, 5)
    sink_b = sink.astype(jnp.float32).reshape(N_KV_HEADS, GQA_GROUP, 1, 1)

    def one_block(qblk):
        s = jnp.einsum('bqkgd,bskd->bkgqs', qblk, k, preferred_element_type=jnp.float32) * scale
        p = sink_softmax(s, sink_b).astype(v.dtype)
        return jnp.einsum('bkgqs,bskd->bqkgd', p, v)

    o = lax.map(one_block, qb)
    return o.transpose(1, 0, 2, 3, 4, 5).reshape(B, S, ATTN_W)


def latent_attention(q, k, v, ck, cv, sink):
    B, S = q.shape[:2]
    nb = S // BLOCK
    scale = HEAD_DIM ** -0.5
    qb = q.reshape(B, nb, BLOCK, N_KV_HEADS, GQA_GROUP, HEAD_DIM)
    pad = jnp.zeros((B, BLOCK, N_KV_HEADS, HEAD_DIM), k.dtype)

    def band(t):
        tp = jnp.concatenate([pad, t, pad], axis=1).reshape(B, nb + 2, BLOCK, N_KV_HEADS, HEAD_DIM)
        return jnp.concatenate([tp[:, :-2], tp[:, 1:-1], tp[:, 2:]], axis=2)

    kb, vb = band(k), band(v)
    blk = jnp.arange(nb)[:, None, None]
    qpos = blk * BLOCK + jnp.arange(BLOCK)[None, :, None]
    kpos = (blk - 1) * BLOCK + jnp.arange(3 * BLOCK)[None, None, :]
    valid = (jnp.abs(kpos - qpos) <= WINDOW) & (kpos >= 0) & (kpos < S)
    s_loc = jnp.einsum('bnqkgd,bnskd->bnkgqs', qb, kb, preferred_element_type=jnp.float32) * scale
    s_loc = jnp.where(valid[None, :, None, None], s_loc, NEG)
    s_ctx = jnp.einsum('bnqkgd,bpkd->bnkgqp', qb, ck, preferred_element_type=jnp.float32) * scale
    s = jnp.concatenate([s_loc, s_ctx], axis=-1)
    p = sink_softmax(s, sink.astype(jnp.float32).reshape(1, 1, N_KV_HEADS, GQA_GROUP, 1, 1)).astype(v.dtype)
    o = (jnp.einsum('bnkgqs,bnskd->bnqkgd', p[..., :3 * BLOCK], vb)
         + jnp.einsum('bnkgqp,bpkd->bnqkgd', p[..., 3 * BLOCK:], cv))
    return o.reshape(B, S, ATTN_W)


def conformer_conv(u, dw, dw_b, ln_g, ln_b, pw):
    a, b = jnp.split(u, 2, axis=-1)
    v = a * jax.nn.sigmoid(b)
    v = lax.conv_general_dilated(v, dw[:, None, :], window_strides=(1,),
                                 padding=[(CONV_K // 2, CONV_K // 2)],
                                 dimension_numbers=('NWC', 'WIO', 'NWC'),
                                 feature_group_count=CONV_W) + dw_b
    v = jax.nn.silu(layernorm(v, ln_g, ln_b))
    return v @ pw


def fourier_mix(u, w):
    B, S, _ = u.shape
    uf = u.astype(jnp.float32).reshape(B, S, FNET_GROUPS, FNET_GW)
    mixed = jnp.fft.fft2(uf, axes=(1, 3), norm='ortho').real.astype(u.dtype).reshape(B, S, FNET_W)
    return mixed @ w


def swiglu(h, wg, wu, wd):
    return (jax.nn.silu(h @ wg) * (h @ wu)) @ wd


def moe_ffn(h, router, wg, wu, wd):
    logits = (h @ router).astype(jnp.float32)
    top_v, top_i = lax.top_k(logits, TOP_K)
    w = jax.nn.softmax(top_v, axis=-1)
    gates = jnp.sum(jax.nn.one_hot(top_i, N_EXPERTS, dtype=jnp.float32) * w[..., None], axis=-2).astype(h.dtype)
    out = jnp.zeros_like(h)
    for e in range(N_EXPERTS):
        out = out + gates[..., e:e + 1] * swiglu(h, wg[e], wu[e], wd[e])
    return out


def setup_inputs(seed: int = 0) -> dict:
    key = jax.random.key(seed)
    ks = jax.random.split(key, 32)
    f32 = jnp.float32

    def nrm(k, shape, fan_in):
        return jax.random.normal(k, shape, f32) * fan_in ** -0.5

    def small(k, shape):
        return 0.02 * jax.random.normal(k, shape, f32)

    return {
        'x_prompt': jax.random.normal(ks[0], (BATCH, SEQ, D_MODEL), f32),
        'x_sample': jax.random.normal(ks[1], (DEC_BATCH, DEC_SEQ, D_MODEL), f32),
        'cache_k': jax.random.normal(ks[2], (DEC_BATCH, DEPTH, PAST_LEN, N_KV_HEADS, HEAD_DIM), f32),
        'cache_v': jax.random.normal(ks[3], (DEC_BATCH, DEPTH, PAST_LEN, N_KV_HEADS, HEAD_DIM), f32),
        'c': jax.random.normal(ks[4], (DEC_BATCH, D_MODEL), f32),
        'c_ctx': jax.random.normal(ks[5], (D_MODEL,), f32),
        'w_ada': 0.5 * nrm(ks[6], (DEPTH, D_MODEL, 6 * D_MODEL), D_MODEL),
        'b_ada': small(ks[7], (DEPTH, 6 * D_MODEL)),
        'g_norm_mix': 1.0 + small(ks[8], (DEPTH, D_MODEL)),
        'g_norm_ffn': 1.0 + small(ks[9], (DEPTH, D_MODEL)),
        'w_in': nrm(ks[10], (DEPTH, D_MODEL, IN_WIDTH), D_MODEL),
        'w_out': nrm(ks[11], (DEPTH, D_MIX, D_MODEL), D_MIX),
        'attn_sink': jax.random.normal(ks[12], (DEPTH, N_HEADS), f32),
        'conv_dw': nrm(ks[13], (DEPTH, CONV_K, CONV_W), CONV_K),
        'conv_dw_b': small(ks[14], (DEPTH, CONV_W)),
        'conv_ln_g': 1.0 + small(ks[15], (DEPTH, CONV_W)),
        'conv_ln_b': small(ks[16], (DEPTH, CONV_W)),
        'conv_pw': nrm(ks[17], (DEPTH, CONV_W, CONV_W), CONV_W),
        'fnet_w': nrm(ks[18], (DEPTH, FNET_W, FNET_W), FNET_W),
        'ffn_w_gate': nrm(ks[19], (N_DENSE, D_MODEL, D_FF), D_MODEL),
        'ffn_w_up': nrm(ks[20], (N_DENSE, D_MODEL, D_FF), D_MODEL),
        'ffn_w_down': nrm(ks[21], (N_DENSE, D_FF, D_MODEL), D_FF),
        'moe_router': nrm(ks[22], (N_MOE, D_MODEL, N_EXPERTS), D_MODEL),
        'moe_w_gate': nrm(ks[23], (N_MOE, N_EXPERTS, D_MODEL, D_FF_EXPERT), D_MODEL),
        'moe_w_up': nrm(ks[24], (N_MOE, N_EXPERTS, D_MODEL, D_FF_EXPERT), D_MODEL),
        'moe_w_down': nrm(ks[25], (N_MOE, N_EXPERTS, D_FF_EXPERT, D_MODEL), D_FF_EXPERT),
        'g_final': 1.0 + small(ks[26], (D_MODEL,)),
    }


def reference(x_prompt, x_sample, cache_k, cache_v, c, c_ctx, w_ada, b_ada, g_norm_mix, g_norm_ffn,
              w_in, w_out, attn_sink, conv_dw, conv_dw_b, conv_ln_g, conv_ln_b, conv_pw, fnet_w,
              ffn_w_gate, ffn_w_up, ffn_w_down, moe_router, moe_w_gate, moe_w_up, moe_w_down, g_final):
    rows = x_sample.shape[1] // GRID_W
    ang_row, ang_col = axial_rope_angles(rows)

    def modulation(cond, l):
        mod = jax.nn.silu(cond) @ w_ada[l] + b_ada[l]
        return jnp.split(mod, 6, axis=-1)

    def ctx_attend(q, k, v, l):
        return context_attention(q, k, v, attn_sink[l])

    def lat_attend(q, k, v, l):
        q = axial_rope(q, ang_row, ang_col)
        k = axial_rope(k, ang_row, ang_col)
        return latent_attention(q, k, v, cache_k[:, l], cache_v[:, l], attn_sink[l])

    def run_layer(x, l, mods, attend):
        sh_m, sc_m, g_m, sh_f, sc_f, g_f = mods
        B, S = x.shape[:2]
        h = rmsnorm(x, g_norm_mix[l]) * (1.0 + sc_m) + sh_m
        q, k, v, u_conv, u_fnet = jnp.split(h @ w_in[l], SPLITS, axis=-1)
        q = q.reshape(B, S, N_HEADS, HEAD_DIM)
        k = k.reshape(B, S, N_KV_HEADS, HEAD_DIM)
        v = v.reshape(B, S, N_KV_HEADS, HEAD_DIM)
        attn = attend(q, k, v, l)
        conv = conformer_conv(u_conv, conv_dw[l], conv_dw_b[l], conv_ln_g[l], conv_ln_b[l], conv_pw[l])
        four = fourier_mix(u_fnet, fnet_w[l])
        x = x + g_m * (jnp.concatenate([attn, conv, four], axis=-1) @ w_out[l])
        h = rmsnorm(x, g_norm_ffn[l]) * (1.0 + sc_f) + sh_f
        if l % 2 == 0:
            f = swiglu(h, ffn_w_gate[l // 2], ffn_w_up[l // 2], ffn_w_down[l // 2])
        else:
            f = moe_ffn(h, moe_router[l // 2], moe_w_gate[l // 2], moe_w_up[l // 2], moe_w_down[l // 2])
        return x + g_f * f, k, v

    y = x_prompt
    ks_list, vs_list = [], []
    for l in range(DEPTH):
        y, k_l, v_l = run_layer(y, l, modulation(c_ctx, l), ctx_attend)
        ks_list.append(k_l)
        vs_list.append(v_l)
    y_prompt = rmsnorm(y, g_final)
    state_k = jnp.stack(ks_list, axis=1)
    state_v = jnp.stack(vs_list, axis=1)

    z = x_sample
    for l in range(DEPTH):
        mods = [m[:, None, :] for m in modulation(c, l)]
        z, _, _ = run_layer(z, l, mods, lat_attend)
    y_sample = rmsnorm(z, g_final)

    return (y_prompt, y_sample, state_k, state_v)
```

```python
import functools

import numpy as np
import jax
import jax.numpy as jnp
from jax import lax
from jax.experimental import pallas as pl
from jax.experimental.pallas import tpu as pltpu

D_MODEL = 1024
BATCH = 16
SEQ = 256
DEPTH = 2
DEC_BATCH = 2
DEC_SEQ = 2048
PAST_LEN = 512
GRID_W = 64
HEAD_DIM = 64
N_HEADS = 8
N_KV_HEADS = 2
GQA_GROUP = N_HEADS // N_KV_HEADS
ATTN_W = N_HEADS * HEAD_DIM
KV_W = N_KV_HEADS * HEAD_DIM
WINDOW = 128
BLOCK = 128
ROPE_THETA = 10000.0
CONV_W = D_MODEL // 4
CONV_K = 31
FNET_GROUPS = 4
FNET_W = D_MODEL // 4
FNET_GW = FNET_W // FNET_GROUPS
IN_WIDTH = ATTN_W + 2 * KV_W + 2 * CONV_W + FNET_W
D_FF = 2816
N_EXPERTS = 8
TOP_K = 2
D_FF_EXPERT = 3584
EPS = 1e-6
NEG = -1e30

T_PROMPT = BATCH * SEQ
T_SAMPLE = DEC_BATCH * DEC_SEQ
T = T_PROMPT + T_SAMPLE
N_COND = 8

ROW_TILE = 512
SUB = 256
SUB_MAX = 8
TMAX = SUB * SUB_MAX
CONV_CHUNK = 128
CONV_PAD = 16
VMEM_LIMIT = 48 * 1024 * 1024

_BF = jnp.bfloat16
_F32 = jnp.float32


def _cond_of_tile(i, tile):
    r = i * tile
    return jnp.where(r < T_PROMPT, 0, 1 + (r - T_PROMPT) // DEC_SEQ)


def _params(sem, vmem=VMEM_LIMIT):
    return pltpu.CompilerParams(dimension_semantics=sem, vmem_limit_bytes=vmem)


def _sigmoid(x):
    return 1.0 / (1.0 + jnp.exp(-x))


def _mod_kernel(cond_ref, w_ref, b_ref, o_ref):
    cnd = cond_ref[...]
    s = (cnd * _sigmoid(cnd)).astype(_BF)
    o_ref[...] = jnp.dot(s, w_ref[...].astype(_BF), preferred_element_type=_F32) + b_ref[...]


def _modulation(cond8, w_ada, b_ada):
    nt = 1536
    return pl.pallas_call(
        _mod_kernel,
        grid=(DEPTH, 6 * D_MODEL // nt),
        in_specs=[
            pl.BlockSpec((N_COND, D_MODEL), lambda l, n: (0, 0)),
            pl.BlockSpec((None, D_MODEL, nt), lambda l, n: (l, 0, n)),
            pl.BlockSpec((None, 1, nt), lambda l, n: (l, 0, n)),
        ],
        out_specs=pl.BlockSpec((None, N_COND, nt), lambda l, n: (l, 0, n)),
        out_shape=jax.ShapeDtypeStruct((DEPTH, N_COND, 6 * D_MODEL), _F32),
        compiler_params=_params(("arbitrary", "arbitrary")),
        name="modulation",
    )(cond8, w_ada, b_ada.reshape(DEPTH, 1, 6 * D_MODEL))


def _norm_mod(x, g, shift, scale):
    ms = jnp.mean(x * x, axis=-1, keepdims=True)
    y = x * lax.rsqrt(ms + EPS) * g
    return y * (1.0 + scale) + shift


def _rope_tables():
    rows = DEC_SEQ // GRID_W
    n_freq = HEAD_DIM // 4
    inv = ROPE_THETA ** (-jnp.arange(n_freq, dtype=_F32) / n_freq)
    gr, gc = jnp.meshgrid(jnp.arange(rows, dtype=_F32), jnp.arange(GRID_W, dtype=_F32), indexing="ij")
    ang_r = gr.reshape(-1)[:, None] * inv
    ang_c = gc.reshape(-1)[:, None] * inv
    cr, sr, cc, sc = jnp.cos(ang_r), jnp.sin(ang_r), jnp.cos(ang_c), jnp.sin(ang_c)
    cos64 = jnp.concatenate([cr, cr, cc, cc], axis=-1)
    sin64 = jnp.concatenate([-sr, sr, -sc, sc], axis=-1)
    return jnp.tile(cos64, (1, 2)), jnp.tile(sin64, (1, 2))


def _rope128(x, cos, sin):
    lane = lax.broadcasted_iota(jnp.int32, x.shape, 1)
    first = (lane % 32) < 16
    partner = jnp.where(first, pltpu.roll(x, 128 - 16, 1), pltpu.roll(x, 16, 1))
    return x * cos + partner * sin


def _inproj_kernel(x_ref, mod_ref, g_ref, w_ref, cos_ref, sin_ref,
                   q_ref, k_ref, v_ref, uc_ref, uf_ref, wb_ref):
    i = pl.program_id(0)

    @pl.when(i == 0)
    def _():
        wb_ref[...] = w_ref[...].astype(_BF)

    h = _norm_mod(x_ref[...], g_ref[...], mod_ref[0:1, :], mod_ref[1:2, :]).astype(_BF)
    acc = jnp.dot(h, wb_ref[...], preferred_element_type=_F32)
    o = 0
    q_ref[...] = acc[:, o:o + ATTN_W]
    o += ATTN_W
    k_ref[...] = acc[:, o:o + KV_W]
    o += KV_W
    v_ref[...] = acc[:, o:o + KV_W]
    o += KV_W
    uc_ref[...] = acc[:, o:o + 2 * CONV_W]
    o += 2 * CONV_W
    uf_ref[...] = acc[:, o:o + FNET_W]

    @pl.when(i >= T_PROMPT // ROW_TILE)
    def _():
        cos = cos_ref[...]
        sin = sin_ref[...]
        for c in range(ATTN_W // 128):
            q_ref[:, c * 128:(c + 1) * 128] = _rope128(q_ref[:, c * 128:(c + 1) * 128], cos, sin)
        k_ref[...] = _rope128(k_ref[...], cos, sin)


def _inproj(x, mods_l, g, w, cos_t, sin_t):
    nt = T // ROW_TILE
    pt = T_PROMPT // ROW_TILE
    per_seq = DEC_SEQ // ROW_TILE

    def rope_idx(i):
        return (jnp.maximum(i - pt, 0) % per_seq, 0)

    row = lambda i: (i, 0)
    outs = [ATTN_W, KV_W, KV_W, 2 * CONV_W, FNET_W]
    return pl.pallas_call(
        _inproj_kernel,
        grid=(nt,),
        in_specs=[
            pl.BlockSpec((ROW_TILE, D_MODEL), row),
            pl.BlockSpec((None, 6, D_MODEL), lambda i: (_cond_of_tile(i, ROW_TILE), 0, 0)),
            pl.BlockSpec((1, D_MODEL), lambda i: (0, 0)),
            pl.BlockSpec((D_MODEL, IN_WIDTH), lambda i: (0, 0)),
            pl.BlockSpec((ROW_TILE, 128), rope_idx),
            pl.BlockSpec((ROW_TILE, 128), rope_idx),
        ],
        out_specs=[pl.BlockSpec((ROW_TILE, n), row) for n in outs],
        out_shape=[jax.ShapeDtypeStruct((T, n), _F32) for n in outs],
        scratch_shapes=[pltpu.VMEM((D_MODEL, IN_WIDTH), _BF)],
        compiler_params=_params(("arbitrary",)),
        name="inproj",
    )(x, mods_l, g.reshape(1, D_MODEL), w, cos_t, sin_t)


def _stack_groups(q, kh, rows):
    parts = [q[:, (kh * GQA_GROUP + g) * HEAD_DIM:(kh * GQA_GROUP + g + 1) * HEAD_DIM] for g in range(GQA_GROUP)]
    return jnp.concatenate(parts, axis=0)


def _sink_column(sink_ref, kh, rows):
    r = lax.broadcasted_iota(jnp.int32, (GQA_GROUP * rows, 1), 0)
    col = jnp.full((GQA_GROUP * rows, 1), sink_ref[kh * GQA_GROUP], _F32)
    for g in range(1, GQA_GROUP):
        col = jnp.where(r >= g * rows, sink_ref[kh * GQA_GROUP + g], col)
    return col


def _unstack_store(o_ref, o, kh, rows):
    for pair in range(GQA_GROUP // 2):
        a = o[(2 * pair) * rows:(2 * pair + 1) * rows]
        b = o[(2 * pair + 1) * rows:(2 * pair + 2) * rows]
        c0 = (kh * GQA_GROUP + 2 * pair) * HEAD_DIM
        o_ref[:, c0:c0 + 2 * HEAD_DIM] = jnp.concatenate([a, b], axis=-1).astype(o_ref.dtype)


_NT = (((1,), (1,)), ((), ()))


def _ctx_attn_kernel(sink_ref, q_ref, k_ref, v_ref, o_ref):
    scale = HEAD_DIM ** -0.5
    q = q_ref[...] * scale
    for kh in range(N_KV_HEADS):
        kk = k_ref[:, kh * HEAD_DIM:(kh + 1) * HEAD_DIM].astype(_BF)
        vv = v_ref[:, kh * HEAD_DIM:(kh + 1) * HEAD_DIM].astype(_BF)
        qs = _stack_groups(q, kh, SEQ).astype(_BF)
        s = lax.dot_general(qs, kk, _NT, preferred_element_type=_F32)
        sink = _sink_column(sink_ref, kh, SEQ)
        m = jnp.maximum(jnp.max(s, axis=-1, keepdims=True), sink)
        p = jnp.exp(s - m)
        den = jnp.sum(p, axis=-1, keepdims=True) + jnp.exp(sink - m)
        o = jnp.dot(p.astype(_BF), vv, preferred_element_type=_F32) / den
        _unstack_store(o_ref, o, kh, SEQ)


def _ctx_attention(sink_l, q, k, v):
    row = lambda b, s: (b, 0)
    return pl.pallas_call(
        _ctx_attn_kernel,
        grid_spec=pltpu.PrefetchScalarGridSpec(
            num_scalar_prefetch=1,
            grid=(BATCH,),
            in_specs=[
                pl.BlockSpec((SEQ, ATTN_W), row),
                pl.BlockSpec((SEQ, KV_W), row),
                pl.BlockSpec((SEQ, KV_W), row),
            ],
            out_specs=pl.BlockSpec((SEQ, ATTN_W), row),
        ),
        out_shape=jax.ShapeDtypeStruct((T_PROMPT, ATTN_W), _BF),
        compiler_params=_params(("arbitrary",)),
        name="ctx_attention",
    )(sink_l, q, k, v)


def _lat_attn_kernel(sink_ref, q_ref, k_ref, v_ref, ck_ref, cv_ref, o_ref):
    n = pl.program_id(1)
    scale = HEAD_DIM ** -0.5
    nb = DEC_SEQ // BLOCK
    band = 3 * BLOCK
    start = pl.multiple_of(jnp.clip(n - 1, 0, nb - 3) * BLOCK, BLOCK)
    q = q_ref[...] * scale
    kb = k_ref[pl.ds(start, band), :]
    vb = v_ref[pl.ds(start, band), :]
    rows = GQA_GROUP * BLOCK
    qpos = n * BLOCK + lax.broadcasted_iota(jnp.int32, (rows, band), 0) % BLOCK
    kpos = start + lax.broadcasted_iota(jnp.int32, (rows, band), 1)
    valid = jnp.abs(kpos - qpos) <= WINDOW
    for kh in range(N_KV_HEADS):
        hs = slice(kh * HEAD_DIM, (kh + 1) * HEAD_DIM)
        qs = _stack_groups(q, kh, BLOCK).astype(_BF)
        s_loc = lax.dot_general(qs, kb[:, hs].astype(_BF), _NT, preferred_element_type=_F32)
        s_loc = jnp.where(valid, s_loc, NEG)
        s_ctx = lax.dot_general(qs, ck_ref[:, hs].astype(_BF), _NT, preferred_element_type=_F32)
        sink = _sink_column(sink_ref, kh, BLOCK)
        m = jnp.maximum(jnp.maximum(jnp.max(s_loc, axis=-1, keepdims=True),
                                    jnp.max(s_ctx, axis=-1, keepdims=True)), sink)
        p_loc = jnp.exp(s_loc - m)
        p_ctx = jnp.exp(s_ctx - m)
        den = (jnp.sum(p_loc, axis=-1, keepdims=True) + jnp.sum(p_ctx, axis=-1, keepdims=True)
               + jnp.exp(sink - m))
        o = (jnp.dot(p_loc.astype(_BF), vb[:, hs].astype(_BF), preferred_element_type=_F32)
             + jnp.dot(p_ctx.astype(_BF), cv_ref[:, hs].astype(_BF), preferred_element_type=_F32)) / den
        _unstack_store(o_ref, o, kh, BLOCK)


def _lat_attention(sink_l, q, k, v, ck, cv):
    nb = DEC_SEQ // BLOCK
    q0 = T_PROMPT // BLOCK
    s0 = T_PROMPT // DEC_SEQ
    return pl.pallas_call(
        _lat_attn_kernel,
        grid_spec=pltpu.PrefetchScalarGridSpec(
            num_scalar_prefetch=1,
            grid=(DEC_BATCH, nb),
            in_specs=[
                pl.BlockSpec((BLOCK, ATTN_W), lambda b, n, s: (q0 + b * nb + n, 0)),
                pl.BlockSpec((DEC_SEQ, KV_W), lambda b, n, s: (s0 + b, 0)),
                pl.BlockSpec((DEC_SEQ, KV_W), lambda b, n, s: (s0 + b, 0)),
                pl.BlockSpec((None, PAST_LEN, KV_W), lambda b, n, s: (b, 0, 0)),
                pl.BlockSpec((None, PAST_LEN, KV_W), lambda b, n, s: (b, 0, 0)),
            ],
            out_specs=pl.BlockSpec((BLOCK, ATTN_W), lambda b, n, s: (b * nb + n, 0)),
        ),
        out_shape=jax.ShapeDtypeStruct((T_SAMPLE, ATTN_W), _BF),
        compiler_params=_params(("arbitrary", "arbitrary")),
        name="lat_attention",
    )(sink_l, q, k, v, ck, cv)


def _conv_kernel(seq, u_ref, dw_ref, dwb_ref, lg_ref, lb_ref, pw_ref, o_ref, pad_ref):
    u = u_ref[...]
    pad_ref[0:CONV_PAD, :] = jnp.zeros((CONV_PAD, CONV_W), _F32)
    pad_ref[CONV_PAD + seq:2 * CONV_PAD + seq, :] = jnp.zeros((CONV_PAD, CONV_W), _F32)
    pad_ref[CONV_PAD:CONV_PAD + seq, :] = u[:, :CONV_W] * _sigmoid(u[:, CONV_W:])
    pw = pw_ref[...].astype(_BF)
    off = CONV_PAD - CONV_K // 2

    def chunk(c, carry):
        row = pl.multiple_of(c * CONV_CHUNK, CONV_CHUNK)
        win = pad_ref[pl.ds(row, CONV_CHUNK + 2 * CONV_PAD), :]
        acc = jnp.zeros((CONV_CHUNK, CONV_W), _F32) + dwb_ref[...]
        for t in range(CONV_K):
            acc = acc + win[off + t:off + t + CONV_CHUNK, :] * dw_ref[t:t + 1, :]
        mu = jnp.mean(acc, axis=-1, keepdims=True)
        d = acc - mu
        var = jnp.mean(d * d, axis=-1, keepdims=True)
        y = d * lax.rsqrt(var + EPS) * lg_ref[...] + lb_ref[...]
        y = (y * _sigmoid(y)).astype(_BF)
        o_ref[pl.ds(row, CONV_CHUNK), :] = jnp.dot(y, pw, preferred_element_type=_F32).astype(o_ref.dtype)
        return carry

    lax.fori_loop(0, seq // CONV_CHUNK, chunk, 0)


def _conv_module(uc, seq, nbatch, block0, dw, dwb, lg, lb, pw):
    vec = lambda a: a.reshape(1, CONV_W)
    const = lambda b: (0, 0)
    return pl.pallas_call(
        functools.partial(_conv_kernel, seq),
        grid=(nbatch,),
        in_specs=[
            pl.BlockSpec((seq, 2 * CONV_W), lambda b: (block0 + b, 0)),
            pl.BlockSpec((CONV_K, CONV_W), const),
            pl.BlockSpec((1, CONV_W), const),
            pl.BlockSpec((1, CONV_W), const),
            pl.BlockSpec((1, CONV_W), const),
            pl.BlockSpec((CONV_W, CONV_W), const),
        ],
        out_specs=pl.BlockSpec((seq, CONV_W), lambda b: (b, 0)),
        out_shape=jax.ShapeDtypeStruct((nbatch * seq, CONV_W), _BF),
        scratch_shapes=[pltpu.VMEM((seq + 2 * CONV_PAD, CONV_W), _F32)],
        compiler_params=_params(("arbitrary",)),
        name="conv_module_%d" % seq,
    )(uc, dw, vec(dwb), vec(lg), vec(lb), pw)


def _dft_constants(seq):
    j = np.arange(seq, dtype=np.int64)
    ang = 2.0 * np.pi * ((j[:, None] * j[None, :]) % seq).astype(np.float64) / seq
    cs = np.concatenate([np.cos(ang), -np.sin(ang)], axis=1) / np.sqrt(seq)
    c = np.arange(FNET_GW, dtype=np.int64)
    angc = 2.0 * np.pi * ((c[:, None] * c[None, :]) % FNET_GW).astype(np.float64) / FNET_GW
    eye = np.eye(FNET_GROUPS)
    cc = np.kron(eye, np.cos(angc)) / np.sqrt(FNET_GW)
    sc = np.kron(eye, np.sin(angc)) / np.sqrt(FNET_GW)
    w1 = np.concatenate([cc, sc], axis=1)
    return jnp.asarray(cs, dtype=_F32), jnp.asarray(w1, dtype=_F32)


def _fnet_kernel(seq, u_ref, cs_ref, w1_ref, fw_ref, o_ref, xcs_ref):
    @pl.when(pl.program_id(1) == 0)
    def _():
        t = jnp.dot(u_ref[...].astype(_BF), w1_ref[...].astype(_BF), preferred_element_type=_F32)
        xcs_ref[0:seq, :] = t[:, :FNET_W].astype(_BF)
        xcs_ref[seq:2 * seq, :] = t[:, FNET_W:].astype(_BF)

    mixed = jnp.dot(cs_ref[...].astype(_BF), xcs_ref[...], preferred_element_type=_F32)
    o_ref[...] = jnp.dot(mixed.astype(_BF), fw_ref[...].astype(_BF),
                         preferred_element_type=_F32).astype(o_ref.dtype)


def _fourier_mix(uf, seq, nbatch, block0, fw):
    tr = min(seq, 512)
    nr = seq // tr
    cs, w1 = _dft_constants(seq)
    return pl.pallas_call(
        functools.partial(_fnet_kernel, seq),
        grid=(nbatch, nr),
        in_specs=[
            pl.BlockSpec((seq, FNET_W), lambda b, r: (block0 + b, 0)),
            pl.BlockSpec((tr, 2 * seq), lambda b, r: (r, 0)),
            pl.BlockSpec((FNET_W, 2 * FNET_W), lambda b, r: (0, 0)),
            pl.BlockSpec((FNET_W, FNET_W), lambda b, r: (0, 0)),
        ],
        out_specs=pl.BlockSpec((tr, FNET_W), lambda b, r: (b * nr + r, 0)),
        out_shape=jax.ShapeDtypeStruct((nbatch * seq, FNET_W), _BF),
        scratch_shapes=[pltpu.VMEM((2 * seq, FNET_W), _BF)],
        compiler_params=_params(("arbitrary", "arbitrary")),
        name="fourier_mix_%d" % seq,
    )(uf, cs, w1, fw)


def _outproj_kernel(a_ref, c_ref, f_ref, x_ref, mod_ref, w_ref, o_ref, wb_ref):
    @pl.when(pl.program_id(0) == 0)
    def _():
        wb_ref[...] = w_ref[...].astype(_BF)

    mix = jnp.concatenate([a_ref[...], c_ref[...], f_ref[...]], axis=-1)
    o_ref[...] = x_ref[...] + mod_ref[2:3, :] * jnp.dot(mix, wb_ref[...], preferred_element_type=_F32)


def _outproj(attn, conv, four, x, mods_l, w):
    row = lambda i: (i, 0)
    return pl.pallas_call(
        _outproj_kernel,
        grid=(T // ROW_TILE,),
        in_specs=[
            pl.BlockSpec((ROW_TILE, ATTN_W), row),
            pl.BlockSpec((ROW_TILE, CONV_W), row),
            pl.BlockSpec((ROW_TILE, FNET_W), row),
            pl.BlockSpec((ROW_TILE, D_MODEL), row),
            pl.BlockSpec((None, 6, D_MODEL), lambda i: (_cond_of_tile(i, ROW_TILE), 0, 0)),
            pl.BlockSpec((D_MODEL, D_MODEL), lambda i: (0, 0)),
        ],
        out_specs=pl.BlockSpec((ROW_TILE, D_MODEL), row),
        out_shape=jax.ShapeDtypeStruct((T, D_MODEL), _F32),
        scratch_shapes=[pltpu.VMEM((D_MODEL, D_MODEL), _BF)],
        compiler_params=_params(("arbitrary",)),
        name="outproj",
    )(attn, conv, four, x, mods_l, w)


def _ffn_in_kernel(route, x_ref, mod_ref, g_ref, *rest):
    if route:
        r_ref, h_ref, ei_ref, ew_ref = rest
    else:
        (h_ref,) = rest
    h = _norm_mod(x_ref[...], g_ref[...], mod_ref[3:4, :], mod_ref[4:5, :])
    h_ref[...] = h
    if route:
        lg = lax.dot_general(r_ref[...].astype(_BF), h.astype(_BF), _NT, preferred_element_type=_F32)
        eid = lax.broadcasted_iota(jnp.int32, lg.shape, 0)
        m1 = jnp.max(lg, axis=0, keepdims=True)
        i1 = jnp.min(jnp.where(lg == m1, eid, N_EXPERTS), axis=0, keepdims=True)
        lg2 = jnp.where(eid == i1, -jnp.inf, lg)
        m2 = jnp.max(lg2, axis=0, keepdims=True)
        i2 = jnp.min(jnp.where(lg2 == m2, eid, N_EXPERTS), axis=0, keepdims=True)
        e = jnp.exp(m2 - m1)
        ei_ref[0:1, :] = i1
        ei_ref[1:2, :] = i2
        ew_ref[0:1, :] = 1.0 / (1.0 + e)
        ew_ref[1:2, :] = e / (1.0 + e)


def _ffn_in(x, mods_l, g, router_t=None):
    route = router_t is not None
    row = lambda i: (i, 0)
    in_specs = [
        pl.BlockSpec((ROW_TILE, D_MODEL), row),
        pl.BlockSpec((None, 6, D_MODEL), lambda i: (_cond_of_tile(i, ROW_TILE), 0, 0)),
        pl.BlockSpec((1, D_MODEL), lambda i: (0, 0)),
    ]
    out_specs = [pl.BlockSpec((ROW_TILE, D_MODEL), row)]
    out_shape = [jax.ShapeDtypeStruct((T, D_MODEL), _F32)]
    args = [x, mods_l, g.reshape(1, D_MODEL)]
    if route:
        in_specs.append(pl.BlockSpec((N_EXPERTS, D_MODEL), lambda i: (0, 0)))
        out_specs += [pl.BlockSpec((TOP_K, ROW_TILE), lambda i: (0, i))] * 2
        out_shape += [jax.ShapeDtypeStruct((TOP_K, T), jnp.int32), jax.ShapeDtypeStruct((TOP_K, T), _F32)]
        args.append(router_t)
    return pl.pallas_call(
        functools.partial(_ffn_in_kernel, route),
        grid=(T // ROW_TILE,),
        in_specs=in_specs,
        out_specs=out_specs,
        out_shape=out_shape,
        compiler_params=_params(("arbitrary",)),
        name="ffn_in_route" if route else "ffn_in",
    )(*args)


def _ffn_kernel(gather, nj, vis_e, vis_start, vis_cnt, vis_rows, tok, dst,
                h_hbm, wg_ref, wu_ref, wd_ref, y_hbm,
                xbuf, acc, wgb, wub, wdb, sem_in, sem_out):
    del vis_e
    v = pl.program_id(0)
    j = pl.program_id(1)
    cnt = vis_cnt[v]
    row0 = vis_start[v] * SUB

    def sub_copy_in(s):
        r = pl.multiple_of(row0 + s * SUB, SUB)
        return pltpu.make_async_copy(h_hbm.at[pl.ds(r, SUB)], xbuf.at[pl.ds(s * SUB, SUB)], sem_in)

    def sub_copy_out(s):
        r = pl.multiple_of(row0 + s * SUB, SUB)
        return pltpu.make_async_copy(acc.at[pl.ds(s * SUB, SUB)], y_hbm.at[pl.ds(r, SUB)], sem_out)

    sub_wait_in = pltpu.make_async_copy(h_hbm.at[pl.ds(0, SUB)], xbuf.at[pl.ds(0, SUB)], sem_in)
    sub_wait_out = pltpu.make_async_copy(acc.at[pl.ds(0, SUB)], y_hbm.at[pl.ds(0, SUB)], sem_out)

    @pl.when(jnp.logical_and(j == 0, cnt > 0))
    def _():
        if gather:
            def issue(r, carry):
                t = tok[row0 + r]
                pltpu.make_async_copy(h_hbm.at[pl.ds(t, 1)], xbuf.at[pl.ds(r, 1)], sem_in).start()
                return carry
            lax.fori_loop(0, cnt * SUB, issue, 0)
        else:
            lax.fori_loop(0, cnt, lambda s, c: (sub_copy_in(s).start(), c)[1], 0)
        lax.fori_loop(0, cnt, lambda s, c: (sub_wait_in.wait(), c)[1], 0)

    @pl.when(cnt > 0)
    def _():
        wgb[...] = wg_ref[...].astype(_BF)
        wub[...] = wu_ref[...].astype(_BF)
        wdb[...] = wd_ref[...].astype(_BF)

        def sub(s, carry):
            rs = pl.ds(pl.multiple_of(s * SUB, SUB), SUB)
            x = xbuf[rs, :].astype(_BF)
            a = jnp.dot(x, wgb[...], preferred_element_type=_F32)
            b = jnp.dot(x, wub[...], preferred_element_type=_F32)
            p = (a * _sigmoid(a) * b).astype(_BF)
            contrib = jnp.dot(p, wdb[...], preferred_element_type=_F32)

            @pl.when(j == 0)
            def _():
                acc[rs, :] = contrib

            @pl.when(j > 0)
            def _():
                acc[rs, :] += contrib
            return carry
        lax.fori_loop(0, cnt, sub, 0)

    @pl.when(jnp.logical_and(j == nj - 1, cnt > 0))
    def _():
        if gather:
            def issue(r, carry):
                d = dst[row0 + r]
                pltpu.make_async_copy(acc.at[pl.ds(r, 1)], y_hbm.at[pl.ds(d, 1)], sem_out).start()
                return carry
            nrow = vis_rows[v]
            row_wait_out = pltpu.make_async_copy(acc.at[pl.ds(0, 1)], y_hbm.at[pl.ds(0, 1)], sem_out)
            lax.fori_loop(0, nrow, issue, 0)
            lax.fori_loop(0, nrow // SUB, lambda s, c: (sub_wait_out.wait(), c)[1], 0)
            lax.fori_loop(0, nrow % SUB, lambda s, c: (row_wait_out.wait(), c)[1], 0)
        else:
            lax.fori_loop(0, cnt, lambda s, c: (sub_copy_out(s).start(), c)[1], 0)
            lax.fori_loop(0, cnt, lambda s, c: (sub_wait_out.wait(), c)[1], 0)


def _grouped_ffn(h, wg, wu, wd, fc, vis_e, vis_start, vis_cnt, vis_rows, tok=None, dst=None, y_rows=T):
    gather = tok is not None
    if not gather:
        tok = jnp.zeros((1,), jnp.int32)
        dst = jnp.zeros((1,), jnp.int32)
    ff = wg.shape[-1]
    nj = ff // fc
    nv = vis_e.shape[0]
    return pl.pallas_call(
        functools.partial(_ffn_kernel, gather, nj),
        grid_spec=pltpu.PrefetchScalarGridSpec(
            num_scalar_prefetch=6,
            grid=(nv, nj),
            in_specs=[
                pl.BlockSpec(memory_space=pl.ANY),
                pl.BlockSpec((None, D_MODEL, fc), lambda v, j, ve, *_: (ve[v], 0, j)),
                pl.BlockSpec((None, D_MODEL, fc), lambda v, j, ve, *_: (ve[v], 0, j)),
                pl.BlockSpec((None, fc, D_MODEL), lambda v, j, ve, *_: (ve[v], j, 0)),
            ],
            out_specs=pl.BlockSpec(memory_space=pl.ANY),
            scratch_shapes=[
                pltpu.VMEM((TMAX, D_MODEL), _F32),
                pltpu.VMEM((TMAX, D_MODEL), _F32),
                pltpu.VMEM((D_MODEL, fc), _BF),
                pltpu.VMEM((D_MODEL, fc), _BF),
                pltpu.VMEM((fc, D_MODEL), _BF),
                pltpu.SemaphoreType.DMA(()),
                pltpu.SemaphoreType.DMA(()),
            ],
        ),
        out_shape=jax.ShapeDtypeStruct((y_rows, D_MODEL), _F32),
        compiler_params=_params(("arbitrary", "arbitrary")),
        name="ffn_routed" if gather else "ffn_dense",
    )(vis_e, vis_start, vis_cnt, vis_rows, tok, dst, h, wg, wu, wd)


def _dense_visits():
    nv = T // TMAX
    return (jnp.zeros((nv,), jnp.int32),
            jnp.arange(nv, dtype=jnp.int32) * SUB_MAX,
            jnp.full((nv,), SUB_MAX, jnp.int32),
            jnp.full((nv,), TMAX, jnp.int32))


N_SLOT_SUB = (TOP_K * T) // SUB + N_EXPERTS
N_SLOT = N_SLOT_SUB * SUB
N_VISIT = -(-N_SLOT_SUB // SUB_MAX) + N_EXPERTS


def _routing_plan(eidx):
    e_flat = eidx.reshape(-1)
    onehot = (e_flat[:, None] == jnp.arange(N_EXPERTS, dtype=jnp.int32)[None, :]).astype(jnp.int32)
    csum = jnp.cumsum(onehot, axis=0)
    counts = csum[-1]
    rank = jnp.take_along_axis(csum, e_flat[:, None], axis=1)[:, 0] - 1
    nsub = (counts + SUB - 1) // SUB
    sub_base = jnp.cumsum(nsub) - nsub
    slot = sub_base[e_flat] * SUB + rank
    pair = jnp.arange(TOP_K * T, dtype=jnp.int32)
    tok = jnp.zeros((N_SLOT,), jnp.int32).at[slot].set(pair % T)
    dst = jnp.zeros((N_SLOT,), jnp.int32).at[slot].set(pair)
    nvis = (nsub + SUB_MAX - 1) // SUB_MAX
    vend = jnp.cumsum(nvis)
    total = vend[-1]
    vid = jnp.arange(N_VISIT, dtype=jnp.int32)
    ve = jnp.minimum(jnp.sum((vid[:, None] >= vend[None, :]).astype(jnp.int32), axis=1), N_EXPERTS - 1)
    local = vid - (vend - nvis)[ve]
    nv_e = jnp.maximum(nvis[ve], 1)
    q, r = nsub[ve] // nv_e, nsub[ve] % nv_e
    cnt = q + (local < r).astype(jnp.int32)
    start = sub_base[ve] + local * q + jnp.minimum(local, r)
    used = vid < total
    last_e = ve[jnp.maximum(total - 1, 0)]
    vis_e = jnp.where(used, ve, last_e).astype(jnp.int32)
    vis_cnt = jnp.where(used, cnt, 0).astype(jnp.int32)
    vis_start = jnp.where(used, start, 0).astype(jnp.int32)
    rows = jnp.clip(counts[ve] - (start - sub_base[ve]) * SUB, 0, cnt * SUB)
    vis_rows = jnp.where(used, rows, 0).astype(jnp.int32)
    return vis_e, vis_start, vis_cnt, vis_rows, tok.astype(jnp.int32), dst.astype(jnp.int32)


def _combine_kernel(n_y, final, x_ref, mod_ref, *rest):
    y_refs = rest[:n_y]
    rest = rest[n_y:]
    if n_y == 2:
        w_ref, rest = rest[0], rest[1:]
        w = w_ref[...]
        f = w[:, 0:1] * y_refs[0][...] + w[:, 1:2] * y_refs[1][...]
    else:
        f = y_refs[0][...]
    out = x_ref[...] + mod_ref[5:6, :] * f
    if final:
        g_ref, o_ref = rest
        ms = jnp.mean(out * out, axis=-1, keepdims=True)
        out = out * lax.rsqrt(ms + EPS) * g_ref[...]
    else:
        (o_ref,) = rest
    o_ref[...] = out


def _combine(x, mods_l, y, w=None, g_final=None):
    n_y = 1 if w is None else 2
    final = g_final is not None
    row = lambda i: (i, 0)
    nt = T // ROW_TILE
    in_specs = [
        pl.BlockSpec((ROW_TILE, D_MODEL), row),
        pl.BlockSpec((None, 6, D_MODEL), lambda i: (_cond_of_tile(i, ROW_TILE), 0, 0)),
        pl.BlockSpec((ROW_TILE, D_MODEL), row),
    ]
    args = [x, mods_l, y]
    if n_y == 2:
        in_specs += [pl.BlockSpec((ROW_TILE, D_MODEL), lambda i: (nt + i, 0)),
                     pl.BlockSpec((ROW_TILE, TOP_K), row)]
        args += [y, w]
    if final:
        in_specs.append(pl.BlockSpec((1, D_MODEL), lambda i: (0, 0)))
        args.append(g_final.reshape(1, D_MODEL))
    return pl.pallas_call(
        functools.partial(_combine_kernel, n_y, final),
        grid=(nt,),
        in_specs=in_specs,
        out_specs=pl.BlockSpec((ROW_TILE, D_MODEL), row),
        out_shape=jax.ShapeDtypeStruct((T, D_MODEL), _F32),
        compiler_params=_params(("arbitrary",)),
        name="combine_%d%s" % (n_y, "_final" if final else ""),
    )(*args)


def kernel(x_prompt, x_sample, cache_k, cache_v, c, c_ctx, w_ada, b_ada, g_norm_mix, g_norm_ffn,
           w_in, w_out, attn_sink, conv_dw, conv_dw_b, conv_ln_g, conv_ln_b, conv_pw, fnet_w,
           ffn_w_gate, ffn_w_up, ffn_w_down, moe_router, moe_w_gate, moe_w_up, moe_w_down, g_final):
    x = jnp.concatenate([x_prompt.reshape(T_PROMPT, D_MODEL), x_sample.reshape(T_SAMPLE, D_MODEL)], axis=0)
    cond8 = jnp.concatenate([c_ctx[None, :], c, jnp.zeros((N_COND - 1 - DEC_BATCH, D_MODEL), _F32)], axis=0)
    mods = _modulation(cond8, w_ada, b_ada).reshape(DEPTH, N_COND, 6, D_MODEL)
    cos_t, sin_t = _rope_tables()
    ck_all = cache_k.reshape(DEC_BATCH, DEPTH, PAST_LEN, KV_W)
    cv_all = cache_v.reshape(DEC_BATCH, DEPTH, PAST_LEN, KV_W)
    p_blocks = T_PROMPT // DEC_SEQ

    ks, vs = [], []
    for l in range(DEPTH):
        q, k, v, uc, uf = _inproj(x, mods[l], g_norm_mix[l], w_in[l], cos_t, sin_t)
        ks.append(k[:T_PROMPT])
        vs.append(v[:T_PROMPT])
        attn = jnp.concatenate([
            _ctx_attention(attn_sink[l], q, k, v),
            _lat_attention(attn_sink[l], q, k, v, ck_all[:, l], cv_all[:, l]),
        ], axis=0)
        cargs = (conv_dw[l], conv_dw_b[l], conv_ln_g[l], conv_ln_b[l], conv_pw[l])
        conv = jnp.concatenate([
            _conv_module(uc, SEQ, BATCH, 0, *cargs),
            _conv_module(uc, DEC_SEQ, DEC_BATCH, p_blocks, *cargs),
        ], axis=0)
        four = jnp.concatenate([
            _fourier_mix(uf, SEQ, BATCH, 0, fnet_w[l]),
            _fourier_mix(uf, DEC_SEQ, DEC_BATCH, p_blocks, fnet_w[l]),
        ], axis=0)
        x = _outproj(attn, conv, four, x, mods[l], w_out[l])
        last = g_final if l == DEPTH - 1 else None
        if l % 2 == 0:
            (h,) = _ffn_in(x, mods[l], g_norm_ffn[l])
            i = l // 2
            y = _grouped_ffn(h, ffn_w_gate[i:i + 1], ffn_w_up[i:i + 1], ffn_w_down[i:i + 1], 256,
                             *_dense_visits())
            x = _combine(x, mods[l], y, g_final=last)
        else:
            i = l // 2
            h, eidx, ew = _ffn_in(x, mods[l], g_norm_ffn[l], moe_router[i].T)
            plan = _routing_plan(eidx)
            y = _grouped_ffn(h, moe_w_gate[i], moe_w_up[i], moe_w_down[i], 512, *plan, y_rows=TOP_K * T)
            x = _combine(x, mods[l], y, w=ew.T, g_final=last)

    y_prompt = x[:T_PROMPT].reshape(BATCH, SEQ, D_MODEL)
    y_sample = x[T_PROMPT:].reshape(DEC_BATCH, DEC_SEQ, D_MODEL)
    state_k = jnp.stack([a.reshape(BATCH, SEQ, N_KV_HEADS, HEAD_DIM) for a in ks], axis=1)
    state_v = jnp.stack([a.reshape(BATCH, SEQ, N_KV_HEADS, HEAD_DIM) for a in vs], axis=1)
    return (y_prompt, y_sample, state_k, state_v)
```

```python
import functools

import numpy as np
import jax
import jax.numpy as jnp
from jax import lax
from jax.experimental import pallas as pl
from jax.experimental.pallas import tpu as pltpu

D_MODEL = 1024
BATCH = 16
SEQ = 256
DEPTH = 2
DEC_BATCH = 2
DEC_SEQ = 2048
PAST_LEN = 512
GRID_W = 64
HEAD_DIM = 64
N_HEADS = 8
N_KV_HEADS = 2
GQA_GROUP = N_HEADS // N_KV_HEADS
ATTN_W = N_HEADS * HEAD_DIM
KV_W = N_KV_HEADS * HEAD_DIM
WINDOW = 128
BLOCK = 128
ROPE_THETA = 10000.0
CONV_W = D_MODEL // 4
CONV_K = 31
FNET_GROUPS = 4
FNET_W = D_MODEL // 4
FNET_GW = FNET_W // FNET_GROUPS
IN_WIDTH = ATTN_W + 2 * KV_W + 2 * CONV_W + FNET_W
D_FF = 2816
N_EXPERTS = 8
TOP_K = 2
D_FF_EXPERT = 3584
EPS = 1e-6
NEG = -1e30

T_PROMPT = BATCH * SEQ
T_SAMPLE = DEC_BATCH * DEC_SEQ
T = T_PROMPT + T_SAMPLE
N_COND = 8

ROW_TILE = 512
SUB = 256
SUB_MAX = 16
TMAX = SUB * SUB_MAX
BLK_SUBS = 4
COMB_TILE = 256
CONV_CHUNK = 128
CONV_PAD = 16
VMEM_LIMIT = 48 * 1024 * 1024
FFN_VMEM_LIMIT = 56 * 1024 * 1024

_BF = jnp.bfloat16
_F32 = jnp.float32


def _cond_of_tile(i, tile):
    r = i * tile
    return jnp.where(r < T_PROMPT, 0, 1 + (r - T_PROMPT) // DEC_SEQ)


def _params(sem, vmem=VMEM_LIMIT):
    return pltpu.CompilerParams(dimension_semantics=sem, vmem_limit_bytes=vmem)


def _sigmoid(x):
    return 1.0 / (1.0 + jnp.exp(-x))


def _mod_kernel(cond_ref, w_ref, b_ref, o_ref):
    cnd = cond_ref[...]
    s = (cnd * _sigmoid(cnd)).astype(_BF)
    o_ref[...] = jnp.dot(s, w_ref[...].astype(_BF), preferred_element_type=_F32) + b_ref[...]


def _modulation(cond8, w_ada, b_ada):
    nt = 1536
    return pl.pallas_call(
        _mod_kernel,
        grid=(DEPTH, 6 * D_MODEL // nt),
        in_specs=[
            pl.BlockSpec((N_COND, D_MODEL), lambda l, n: (0, 0)),
            pl.BlockSpec((None, D_MODEL, nt), lambda l, n: (l, 0, n)),
            pl.BlockSpec((None, 1, nt), lambda l, n: (l, 0, n)),
        ],
        out_specs=pl.BlockSpec((None, N_COND, nt), lambda l, n: (l, 0, n)),
        out_shape=jax.ShapeDtypeStruct((DEPTH, N_COND, 6 * D_MODEL), _F32),
        compiler_params=_params(("arbitrary", "arbitrary")),
        name="modulation",
    )(cond8, w_ada, b_ada.reshape(DEPTH, 1, 6 * D_MODEL))


def _norm_mod(x, g, shift, scale):
    ms = jnp.mean(x * x, axis=-1, keepdims=True)
    y = x * lax.rsqrt(ms + EPS) * g
    return y * (1.0 + scale) + shift


def _rope_tables():
    rows = DEC_SEQ // GRID_W
    n_freq = HEAD_DIM // 4
    inv = ROPE_THETA ** (-jnp.arange(n_freq, dtype=_F32) / n_freq)
    gr, gc = jnp.meshgrid(jnp.arange(rows, dtype=_F32), jnp.arange(GRID_W, dtype=_F32), indexing="ij")
    ang_r = gr.reshape(-1)[:, None] * inv
    ang_c = gc.reshape(-1)[:, None] * inv
    cr, sr, cc, sc = jnp.cos(ang_r), jnp.sin(ang_r), jnp.cos(ang_c), jnp.sin(ang_c)
    cos64 = jnp.concatenate([cr, cr, cc, cc], axis=-1)
    sin64 = jnp.concatenate([-sr, sr, -sc, sc], axis=-1)
    return jnp.tile(cos64, (1, 2)), jnp.tile(sin64, (1, 2))


def _rope128(x, cos, sin):
    lane = lax.broadcasted_iota(jnp.int32, x.shape, 1)
    first = (lane % 32) < 16
    partner = jnp.where(first, pltpu.roll(x, 128 - 16, 1), pltpu.roll(x, 16, 1))
    return x * cos + partner * sin


def _inproj_kernel(x_ref, mod_ref, g_ref, w_ref, cos_ref, sin_ref,
                   q_ref, k_ref, v_ref, uc_ref, uf_ref, wb_ref):
    i = pl.program_id(0)

    @pl.when(i == 0)
    def _():
        wb_ref[...] = w_ref[...].astype(_BF)

    h = _norm_mod(x_ref[...], g_ref[...], mod_ref[0:1, :], mod_ref[1:2, :]).astype(_BF)
    acc = jnp.dot(h, wb_ref[...], preferred_element_type=_F32)
    o = 0
    q_ref[...] = acc[:, o:o + ATTN_W]
    o += ATTN_W
    k_ref[...] = acc[:, o:o + KV_W]
    o += KV_W
    v_ref[...] = acc[:, o:o + KV_W]
    o += KV_W
    uc_ref[...] = acc[:, o:o + 2 * CONV_W]
    o += 2 * CONV_W
    uf_ref[...] = acc[:, o:o + FNET_W]

    @pl.when(i >= T_PROMPT // ROW_TILE)
    def _():
        cos = cos_ref[...]
        sin = sin_ref[...]
        for c in range(ATTN_W // 128):
            q_ref[:, c * 128:(c + 1) * 128] = _rope128(q_ref[:, c * 128:(c + 1) * 128], cos, sin)
        k_ref[...] = _rope128(k_ref[...], cos, sin)


def _inproj(x, mods_l, g, w, cos_t, sin_t):
    nt = T // ROW_TILE
    pt = T_PROMPT // ROW_TILE
    per_seq = DEC_SEQ // ROW_TILE

    def rope_idx(i):
        return (jnp.maximum(i - pt, 0) % per_seq, 0)

    row = lambda i: (i, 0)
    outs = [ATTN_W, KV_W, KV_W, 2 * CONV_W, FNET_W]
    return pl.pallas_call(
        _inproj_kernel,
        grid=(nt,),
        in_specs=[
            pl.BlockSpec((ROW_TILE, D_MODEL), row),
            pl.BlockSpec((None, 6, D_MODEL), lambda i: (_cond_of_tile(i, ROW_TILE), 0, 0)),
            pl.BlockSpec((1, D_MODEL), lambda i: (0, 0)),
            pl.BlockSpec((D_MODEL, IN_WIDTH), lambda i: (0, 0)),
            pl.BlockSpec((ROW_TILE, 128), rope_idx),
            pl.BlockSpec((ROW_TILE, 128), rope_idx),
        ],
        out_specs=[pl.BlockSpec((ROW_TILE, n), row) for n in outs],
        out_shape=[jax.ShapeDtypeStruct((T, n), _F32) for n in outs],
        scratch_shapes=[pltpu.VMEM((D_MODEL, IN_WIDTH), _BF)],
        compiler_params=_params(("arbitrary",)),
        name="inproj",
    )(x, mods_l, g.reshape(1, D_MODEL), w, cos_t, sin_t)


def _stack_groups(q, kh, rows):
    parts = [q[:, (kh * GQA_GROUP + g) * HEAD_DIM:(kh * GQA_GROUP + g + 1) * HEAD_DIM] for g in range(GQA_GROUP)]
    return jnp.concatenate(parts, axis=0)


def _sink_column(sink_ref, kh, rows):
    r = lax.broadcasted_iota(jnp.int32, (GQA_GROUP * rows, 1), 0)
    col = jnp.full((GQA_GROUP * rows, 1), sink_ref[kh * GQA_GROUP], _F32)
    for g in range(1, GQA_GROUP):
        col = jnp.where(r >= g * rows, sink_ref[kh * GQA_GROUP + g], col)
    return col


def _unstack_store(o_ref, o, kh, rows):
    for pair in range(GQA_GROUP // 2):
        a = o[(2 * pair) * rows:(2 * pair + 1) * rows]
        b = o[(2 * pair + 1) * rows:(2 * pair + 2) * rows]
        c0 = (kh * GQA_GROUP + 2 * pair) * HEAD_DIM
        o_ref[:, c0:c0 + 2 * HEAD_DIM] = jnp.concatenate([a, b], axis=-1).astype(o_ref.dtype)


_NT = (((1,), (1,)), ((), ()))


def _ctx_attn_kernel(sink_ref, q_ref, k_ref, v_ref, o_ref):
    scale = HEAD_DIM ** -0.5
    q = q_ref[...] * scale
    for kh in range(N_KV_HEADS):
        kk = k_ref[:, kh * HEAD_DIM:(kh + 1) * HEAD_DIM].astype(_BF)
        vv = v_ref[:, kh * HEAD_DIM:(kh + 1) * HEAD_DIM].astype(_BF)
        qs = _stack_groups(q, kh, SEQ).astype(_BF)
        s = lax.dot_general(qs, kk, _NT, preferred_element_type=_F32)
        sink = _sink_column(sink_ref, kh, SEQ)
        m = jnp.maximum(jnp.max(s, axis=-1, keepdims=True), sink)
        p = jnp.exp(s - m)
        den = jnp.sum(p, axis=-1, keepdims=True) + jnp.exp(sink - m)
        o = jnp.dot(p.astype(_BF), vv, preferred_element_type=_F32) / den
        _unstack_store(o_ref, o, kh, SEQ)


def _ctx_attention(sink_l, q, k, v):
    row = lambda b, s: (b, 0)
    return pl.pallas_call(
        _ctx_attn_kernel,
        grid_spec=pltpu.PrefetchScalarGridSpec(
            num_scalar_prefetch=1,
            grid=(BATCH,),
            in_specs=[
                pl.BlockSpec((SEQ, ATTN_W), row),
                pl.BlockSpec((SEQ, KV_W), row),
                pl.BlockSpec((SEQ, KV_W), row),
            ],
            out_specs=pl.BlockSpec((SEQ, ATTN_W), row),
        ),
        out_shape=jax.ShapeDtypeStruct((T_PROMPT, ATTN_W), _BF),
        compiler_params=_params(("arbitrary",)),
        name="ctx_attention",
    )(sink_l, q, k, v)


def _lat_attn_kernel(sink_ref, q_ref, k_ref, v_ref, ck_ref, cv_ref, o_ref):
    n = pl.program_id(1)
    scale = HEAD_DIM ** -0.5
    nb = DEC_SEQ // BLOCK
    band = 3 * BLOCK
    start = pl.multiple_of(jnp.clip(n - 1, 0, nb - 3) * BLOCK, BLOCK)
    q = q_ref[...] * scale
    kb = k_ref[pl.ds(start, band), :]
    vb = v_ref[pl.ds(start, band), :]
    rows = GQA_GROUP * BLOCK
    qpos = n * BLOCK + lax.broadcasted_iota(jnp.int32, (rows, band), 0) % BLOCK
    kpos = start + lax.broadcasted_iota(jnp.int32, (rows, band), 1)
    valid = jnp.abs(kpos - qpos) <= WINDOW
    for kh in range(N_KV_HEADS):
        hs = slice(kh * HEAD_DIM, (kh + 1) * HEAD_DIM)
        qs = _stack_groups(q, kh, BLOCK).astype(_BF)
        s_loc = lax.dot_general(qs, kb[:, hs].astype(_BF), _NT, preferred_element_type=_F32)
        s_loc = jnp.where(valid, s_loc, NEG)
        s_ctx = lax.dot_general(qs, ck_ref[:, hs].astype(_BF), _NT, preferred_element_type=_F32)
        sink = _sink_column(sink_ref, kh, BLOCK)
        m = jnp.maximum(jnp.maximum(jnp.max(s_loc, axis=-1, keepdims=True),
                                    jnp.max(s_ctx, axis=-1, keepdims=True)), sink)
        p_loc = jnp.exp(s_loc - m)
        p_ctx = jnp.exp(s_ctx - m)
        den = (jnp.sum(p_loc, axis=-1, keepdims=True) + jnp.sum(p_ctx, axis=-1, keepdims=True)
               + jnp.exp(sink - m))
        o = (jnp.dot(p_loc.astype(_BF), vb[:, hs].astype(_BF), preferred_element_type=_F32)
             + jnp.dot(p_ctx.astype(_BF), cv_ref[:, hs].astype(_BF), preferred_element_type=_F32)) / den
        _unstack_store(o_ref, o, kh, BLOCK)


def _lat_attention(sink_l, q, k, v, ck, cv):
    nb = DEC_SEQ // BLOCK
    q0 = T_PROMPT // BLOCK
    s0 = T_PROMPT // DEC_SEQ
    return pl.pallas_call(
        _lat_attn_kernel,
        grid_spec=pltpu.PrefetchScalarGridSpec(
            num_scalar_prefetch=1,
            grid=(DEC_BATCH, nb),
            in_specs=[
                pl.BlockSpec((BLOCK, ATTN_W), lambda b, n, s: (q0 + b * nb + n, 0)),
                pl.BlockSpec((DEC_SEQ, KV_W), lambda b, n, s: (s0 + b, 0)),
                pl.BlockSpec((DEC_SEQ, KV_W), lambda b, n, s: (s0 + b, 0)),
                pl.BlockSpec((None, PAST_LEN, KV_W), lambda b, n, s: (b, 0, 0)),
                pl.BlockSpec((None, PAST_LEN, KV_W), lambda b, n, s: (b, 0, 0)),
            ],
            out_specs=pl.BlockSpec((BLOCK, ATTN_W), lambda b, n, s: (b * nb + n, 0)),
        ),
        out_shape=jax.ShapeDtypeStruct((T_SAMPLE, ATTN_W), _BF),
        compiler_params=_params(("arbitrary", "arbitrary")),
        name="lat_attention",
    )(sink_l, q, k, v, ck, cv)


def _conv_kernel(seq, u_ref, dw_ref, dwb_ref, lg_ref, lb_ref, pw_ref, o_ref, pad_ref):
    u = u_ref[...]
    pad_ref[0:CONV_PAD, :] = jnp.zeros((CONV_PAD, CONV_W), _F32)
    pad_ref[CONV_PAD + seq:2 * CONV_PAD + seq, :] = jnp.zeros((CONV_PAD, CONV_W), _F32)
    pad_ref[CONV_PAD:CONV_PAD + seq, :] = u[:, :CONV_W] * _sigmoid(u[:, CONV_W:])
    pw = pw_ref[...].astype(_BF)
    off = CONV_PAD - CONV_K // 2

    def chunk(c, carry):
        row = pl.multiple_of(c * CONV_CHUNK, CONV_CHUNK)
        win = pad_ref[pl.ds(row, CONV_CHUNK + 2 * CONV_PAD), :]
        acc = jnp.zeros((CONV_CHUNK, CONV_W), _F32) + dwb_ref[...]
        for t in range(CONV_K):
            acc = acc + win[off + t:off + t + CONV_CHUNK, :] * dw_ref[t:t + 1, :]
        mu = jnp.mean(acc, axis=-1, keepdims=True)
        d = acc - mu
        var = jnp.mean(d * d, axis=-1, keepdims=True)
        y = d * lax.rsqrt(var + EPS) * lg_ref[...] + lb_ref[...]
        y = (y * _sigmoid(y)).astype(_BF)
        o_ref[pl.ds(row, CONV_CHUNK), :] = jnp.dot(y, pw, preferred_element_type=_F32).astype(o_ref.dtype)
        return carry

    lax.fori_loop(0, seq // CONV_CHUNK, chunk, 0)


def _conv_module(uc, seq, nbatch, block0, dw, dwb, lg, lb, pw):
    vec = lambda a: a.reshape(1, CONV_W)
    const = lambda b: (0, 0)
    return pl.pallas_call(
        functools.partial(_conv_kernel, seq),
        grid=(nbatch,),
        in_specs=[
            pl.BlockSpec((seq, 2 * CONV_W), lambda b: (block0 + b, 0)),
            pl.BlockSpec((CONV_K, CONV_W), const),
            pl.BlockSpec((1, CONV_W), const),
            pl.BlockSpec((1, CONV_W), const),
            pl.BlockSpec((1, CONV_W), const),
            pl.BlockSpec((CONV_W, CONV_W), const),
        ],
        out_specs=pl.BlockSpec((seq, CONV_W), lambda b: (b, 0)),
        out_shape=jax.ShapeDtypeStruct((nbatch * seq, CONV_W), _BF),
        scratch_shapes=[pltpu.VMEM((seq + 2 * CONV_PAD, CONV_W), _F32)],
        compiler_params=_params(("arbitrary",)),
        name="conv_module_%d" % seq,
    )(uc, dw, vec(dwb), vec(lg), vec(lb), pw)


def _dft_constants(seq):
    j = np.arange(seq, dtype=np.int64)
    ang = 2.0 * np.pi * ((j[:, None] * j[None, :]) % seq).astype(np.float64) / seq
    cs = np.concatenate([np.cos(ang), -np.sin(ang)], axis=1) / np.sqrt(seq)
    c = np.arange(FNET_GW, dtype=np.int64)
    angc = 2.0 * np.pi * ((c[:, None] * c[None, :]) % FNET_GW).astype(np.float64) / FNET_GW
    eye = np.eye(FNET_GROUPS)
    cc = np.kron(eye, np.cos(angc)) / np.sqrt(FNET_GW)
    sc = np.kron(eye, np.sin(angc)) / np.sqrt(FNET_GW)
    w1 = np.concatenate([cc, sc], axis=1)
    return jnp.asarray(cs, dtype=_F32), jnp.asarray(w1, dtype=_F32)


def _fnet_kernel(seq, u_ref, cs_ref, w1_ref, fw_ref, o_ref, xcs_ref):
    @pl.when(pl.program_id(1) == 0)
    def _():
        t = jnp.dot(u_ref[...].astype(_BF), w1_ref[...].astype(_BF), preferred_element_type=_F32)
        xcs_ref[0:seq, :] = t[:, :FNET_W].astype(_BF)
        xcs_ref[seq:2 * seq, :] = t[:, FNET_W:].astype(_BF)

    mixed = jnp.dot(cs_ref[...].astype(_BF), xcs_ref[...], preferred_element_type=_F32)
    o_ref[...] = jnp.dot(mixed.astype(_BF), fw_ref[...].astype(_BF),
                         preferred_element_type=_F32).astype(o_ref.dtype)


def _fourier_mix(uf, seq, nbatch, block0, fw):
    tr = min(seq, 512)
    nr = seq // tr
    cs, w1 = _dft_constants(seq)
    return pl.pallas_call(
        functools.partial(_fnet_kernel, seq),
        grid=(nbatch, nr),
        in_specs=[
            pl.BlockSpec((seq, FNET_W), lambda b, r: (block0 + b, 0)),
            pl.BlockSpec((tr, 2 * seq), lambda b, r: (r, 0)),
            pl.BlockSpec((FNET_W, 2 * FNET_W), lambda b, r: (0, 0)),
            pl.BlockSpec((FNET_W, FNET_W), lambda b, r: (0, 0)),
        ],
        out_specs=pl.BlockSpec((tr, FNET_W), lambda b, r: (b * nr + r, 0)),
        out_shape=jax.ShapeDtypeStruct((nbatch * seq, FNET_W), _BF),
        scratch_shapes=[pltpu.VMEM((2 * seq, FNET_W), _BF)],
        compiler_params=_params(("arbitrary", "arbitrary")),
        name="fourier_mix_%d" % seq,
    )(uf, cs, w1, fw)


def _outproj_kernel(a_ref, c_ref, f_ref, x_ref, mod_ref, w_ref, o_ref, wb_ref):
    @pl.when(pl.program_id(0) == 0)
    def _():
        wb_ref[...] = w_ref[...].astype(_BF)

    mix = jnp.concatenate([a_ref[...], c_ref[...], f_ref[...]], axis=-1)
    o_ref[...] = x_ref[...] + mod_ref[2:3, :] * jnp.dot(mix, wb_ref[...], preferred_element_type=_F32)


def _outproj(attn, conv, four, x, mods_l, w):
    row = lambda i: (i, 0)
    return pl.pallas_call(
        _outproj_kernel,
        grid=(T // ROW_TILE,),
        in_specs=[
            pl.BlockSpec((ROW_TILE, ATTN_W), row),
            pl.BlockSpec((ROW_TILE, CONV_W), row),
            pl.BlockSpec((ROW_TILE, FNET_W), row),
            pl.BlockSpec((ROW_TILE, D_MODEL), row),
            pl.BlockSpec((None, 6, D_MODEL), lambda i: (_cond_of_tile(i, ROW_TILE), 0, 0)),
            pl.BlockSpec((D_MODEL, D_MODEL), lambda i: (0, 0)),
        ],
        out_specs=pl.BlockSpec((ROW_TILE, D_MODEL), row),
        out_shape=jax.ShapeDtypeStruct((T, D_MODEL), _F32),
        scratch_shapes=[pltpu.VMEM((D_MODEL, D_MODEL), _BF)],
        compiler_params=_params(("arbitrary",)),
        name="outproj",
    )(attn, conv, four, x, mods_l, w)


def _ffn_in_kernel(route, x_ref, mod_ref, g_ref, *rest):
    if route:
        r_ref, h_ref, ei_ref, ew_ref = rest
    else:
        (h_ref,) = rest
    h = _norm_mod(x_ref[...], g_ref[...], mod_ref[3:4, :], mod_ref[4:5, :])
    h_ref[...] = h
    if route:
        lg = lax.dot_general(r_ref[...].astype(_BF), h.astype(_BF), _NT, preferred_element_type=_F32)
        eid = lax.broadcasted_iota(jnp.int32, lg.shape, 0)
        m1 = jnp.max(lg, axis=0, keepdims=True)
        i1 = jnp.min(jnp.where(lg == m1, eid, N_EXPERTS), axis=0, keepdims=True)
        lg2 = jnp.where(eid == i1, -jnp.inf, lg)
        m2 = jnp.max(lg2, axis=0, keepdims=True)
        i2 = jnp.min(jnp.where(lg2 == m2, eid, N_EXPERTS), axis=0, keepdims=True)
        e = jnp.exp(m2 - m1)
        ei_ref[0:1, :] = i1
        ei_ref[1:2, :] = i2
        ew_ref[0:1, :] = 1.0 / (1.0 + e)
        ew_ref[1:2, :] = e / (1.0 + e)


def _ffn_in(x, mods_l, g, router_t=None):
    route = router_t is not None
    row = lambda i: (i, 0)
    in_specs = [
        pl.BlockSpec((ROW_TILE, D_MODEL), row),
        pl.BlockSpec((None, 6, D_MODEL), lambda i: (_cond_of_tile(i, ROW_TILE), 0, 0)),
        pl.BlockSpec((1, D_MODEL), lambda i: (0, 0)),
    ]
    out_specs = [pl.BlockSpec((ROW_TILE, D_MODEL), row)]
    out_shape = [jax.ShapeDtypeStruct((T, D_MODEL), _F32)]
    args = [x, mods_l, g.reshape(1, D_MODEL)]
    if route:
        in_specs.append(pl.BlockSpec((N_EXPERTS, D_MODEL), lambda i: (0, 0)))
        out_specs += [pl.BlockSpec((TOP_K, ROW_TILE), lambda i: (0, i))] * 2
        out_shape += [jax.ShapeDtypeStruct((TOP_K, T), jnp.int32), jax.ShapeDtypeStruct((TOP_K, T), _F32)]
        args.append(router_t)
    return pl.pallas_call(
        functools.partial(_ffn_in_kernel, route),
        grid=(T // ROW_TILE,),
        in_specs=in_specs,
        out_specs=out_specs,
        out_shape=out_shape,
        compiler_params=_params(("arbitrary",)),
        name="ffn_in_route" if route else "ffn_in",
    )(*args)


def _ffn_kernel(nj, n_sub_rows, vis_e, vis_start, vis_cnt, used_sub,
                x_hbm, wg_ref, wu_ref, wd_ref, y_hbm,
                big, xb, wgb, wub, wdb, sem_in, sem_out):
    del vis_e
    v = pl.program_id(0)
    j = pl.program_id(1)
    cnt = vis_cnt[v]
    row0 = vis_start[v] * SUB

    def copy_in(s):
        r = pl.multiple_of(row0 + s * SUB, SUB)
        b = pl.multiple_of(s * SUB, SUB)
        return pltpu.make_async_copy(x_hbm.at[pl.ds(r, SUB)], big.at[pl.ds(b, SUB)], sem_in.at[s])

    def copy_out(s):
        r = pl.multiple_of(row0 + s * SUB, SUB)
        b = pl.multiple_of(s * SUB, SUB)
        return pltpu.make_async_copy(big.at[pl.ds(b, SUB)], y_hbm.at[pl.ds(r, SUB)], sem_out)

    def for_subs(fn):
        lax.fori_loop(0, cnt, lambda s, c: (fn(s), c)[1], 0)

    @pl.when(jnp.logical_and(j == 0, cnt > 0))
    def _():
        for_subs(lambda s: copy_in(s).start())

        def land(s):
            copy_in(s).wait()
            rs = pl.ds(pl.multiple_of(s * SUB, SUB), SUB)
            xb[rs, :] = big[rs, :].astype(_BF)
            big[rs, :] = jnp.zeros((SUB, D_MODEL), _F32)
        for_subs(land)

    @pl.when(cnt > 0)
    def _():
        wgb[...] = wg_ref[...].astype(_BF)
        wub[...] = wu_ref[...].astype(_BF)
        wdb[...] = wd_ref[...].astype(_BF)

        def block(r0, nrows):
            rs = pl.ds(r0, nrows)
            x = xb[rs, :]
            a = jnp.dot(x, wgb[...], preferred_element_type=_F32)
            b = jnp.dot(x, wub[...], preferred_element_type=_F32)
            p = (a * _sigmoid(a) * b).astype(_BF)
            big[rs, :] += jnp.dot(p, wdb[...], preferred_element_type=_F32)

        blk = BLK_SUBS * SUB
        nfull = cnt // BLK_SUBS
        lax.fori_loop(0, nfull, lambda i, c: (block(pl.multiple_of(i * blk, blk), blk), c)[1], 0)
        rem = cnt % BLK_SUBS
        base = nfull * blk
        bit = BLK_SUBS // 2
        while bit >= 1:
            take = (rem // bit) % 2 == 1
            pl.when(take)(functools.partial(block, pl.multiple_of(base, SUB), bit * SUB))
            base = base + jnp.where(take, bit * SUB, 0)
            bit //= 2

    @pl.when(jnp.logical_and(j == nj - 1, cnt > 0))
    def _():
        for_subs(lambda s: copy_out(s).start())
        for_subs(lambda s: copy_out(s).wait())

    @pl.when(jnp.logical_and(v == pl.num_programs(0) - 1, j == nj - 1))
    def _():
        def tail(s):
            r = pl.multiple_of(s * SUB, SUB)
            return pltpu.make_async_copy(x_hbm.at[pl.ds(r, SUB)], y_hbm.at[pl.ds(r, SUB)], sem_out)
        lax.fori_loop(used_sub[0], n_sub_rows, lambda s, c: (tail(s).start(), c)[1], 0)
        lax.fori_loop(used_sub[0], n_sub_rows, lambda s, c: (tail(s).wait(), c)[1], 0)


def _grouped_ffn(x, wg, wu, wd, fc, vis_e, vis_start, vis_cnt, used_sub):
    ff = wg.shape[-1]
    nj = ff // fc
    nv = vis_e.shape[0]
    n_rows = x.shape[0]
    return pl.pallas_call(
        functools.partial(_ffn_kernel, nj, n_rows // SUB),
        grid_spec=pltpu.PrefetchScalarGridSpec(
            num_scalar_prefetch=4,
            grid=(nv, nj),
            in_specs=[
                pl.BlockSpec(memory_space=pl.ANY),
                pl.BlockSpec((None, D_MODEL, fc), lambda v, j, ve, *_: (ve[v], 0, j)),
                pl.BlockSpec((None, D_MODEL, fc), lambda v, j, ve, *_: (ve[v], 0, j)),
                pl.BlockSpec((None, fc, D_MODEL), lambda v, j, ve, *_: (ve[v], j, 0)),
            ],
            out_specs=pl.BlockSpec(memory_space=pl.ANY),
            scratch_shapes=[
                pltpu.VMEM((TMAX, D_MODEL), _F32),
                pltpu.VMEM((TMAX, D_MODEL), _BF),
                pltpu.VMEM((D_MODEL, fc), _BF),
                pltpu.VMEM((D_MODEL, fc), _BF),
                pltpu.VMEM((fc, D_MODEL), _BF),
                pltpu.SemaphoreType.DMA((SUB_MAX,)),
                pltpu.SemaphoreType.DMA(()),
            ],
        ),
        out_shape=jax.ShapeDtypeStruct((n_rows, D_MODEL), _F32),
        compiler_params=_params(("arbitrary", "arbitrary"), FFN_VMEM_LIMIT),
        name="ffn_%d" % ff,
    )(vis_e, vis_start, vis_cnt, used_sub, x, wg, wu, wd)


def _dense_visits():
    nv = T // TMAX
    return (jnp.zeros((nv,), jnp.int32),
            jnp.arange(nv, dtype=jnp.int32) * SUB_MAX,
            jnp.full((nv,), SUB_MAX, jnp.int32),
            jnp.full((1,), T // SUB, jnp.int32))


N_SLOT_SUB = (TOP_K * T) // SUB + N_EXPERTS
N_SLOT = N_SLOT_SUB * SUB
N_VISIT = -(-N_SLOT_SUB // SUB_MAX) + N_EXPERTS


def _routing_plan(eidx):
    e_flat = eidx.reshape(-1)
    onehot = e_flat[:, None] == jnp.arange(N_EXPERTS, dtype=jnp.int32)[None, :]
    ch = 128
    oh = onehot.astype(_F32).reshape(-1, ch, N_EXPERTS)
    tri = (jnp.arange(ch)[:, None] >= jnp.arange(ch)[None, :]).astype(_F32)
    within = jnp.einsum("ij,cjk->cik", tri, oh)
    tot = within[:, -1, :]
    csum = (within + (jnp.cumsum(tot, axis=0) - tot)[:, None, :]).reshape(-1, N_EXPERTS).astype(jnp.int32)
    counts = csum[-1]
    nsub = (counts + SUB - 1) // SUB
    sub_base = jnp.cumsum(nsub) - nsub
    slot = jnp.sum(jnp.where(onehot, csum - 1 + (sub_base * SUB)[None, :], 0), axis=1).astype(jnp.int32)
    pad_start = (sub_base * SUB + counts).astype(jnp.int32)
    pad_cnt = (nsub * SUB - counts).astype(jnp.int32)
    used_sub = jnp.sum(nsub).reshape(1).astype(jnp.int32)
    nvis = (nsub + SUB_MAX - 1) // SUB_MAX
    vend = jnp.cumsum(nvis)
    total = vend[-1]
    vid = jnp.arange(N_VISIT, dtype=jnp.int32)
    ve = jnp.minimum(jnp.sum((vid[:, None] >= vend[None, :]).astype(jnp.int32), axis=1), N_EXPERTS - 1)
    local = vid - (vend - nvis)[ve]
    nv_e = jnp.maximum(nvis[ve], 1)
    q, r = nsub[ve] // nv_e, nsub[ve] % nv_e
    cnt = q + (local < r).astype(jnp.int32)
    start = sub_base[ve] + local * q + jnp.minimum(local, r)
    used = vid < total
    last_e = ve[jnp.maximum(total - 1, 0)]
    vis_e = jnp.where(used, ve, last_e).astype(jnp.int32)
    vis_cnt = jnp.where(used, cnt, 0).astype(jnp.int32)
    vis_start = jnp.where(used, start, 0).astype(jnp.int32)
    return slot, pad_start, pad_cnt, (vis_e, vis_start, vis_cnt, used_sub)


def _dispatch_kernel(slot, pad_start, pad_cnt, used_sub, h_hbm, xs_hbm, sem):
    def put(p, carry):
        t = jnp.where(p >= T, p - T, p)
        pltpu.make_async_copy(h_hbm.at[pl.ds(t, 1)], xs_hbm.at[pl.ds(slot[p], 1)], sem).start()
        return carry
    lax.fori_loop(0, TOP_K * T, put, 0, unroll=8)

    n_pad = 0
    for e in range(N_EXPERTS):
        s0 = pad_start[e]

        def fill(r, carry, s0=s0):
            pltpu.make_async_copy(h_hbm.at[pl.ds(0, 1)], xs_hbm.at[pl.ds(s0 + r, 1)], sem).start()
            return carry
        lax.fori_loop(0, pad_cnt[e], fill, 0)
        n_pad = n_pad + pad_cnt[e]

    def tail(s):
        r = pl.multiple_of(s * SUB, SUB)
        return pltpu.make_async_copy(h_hbm.at[pl.ds(0, SUB)], xs_hbm.at[pl.ds(r, SUB)], sem)
    lax.fori_loop(used_sub[0], N_SLOT_SUB, lambda s, c: (tail(s).start(), c)[1], 0)

    n_tail = N_SLOT_SUB - used_sub[0]
    lax.fori_loop(0, TOP_K * T // SUB + n_tail, lambda s, c: (tail(0).wait(), c)[1], 0)
    row_wait = pltpu.make_async_copy(h_hbm.at[pl.ds(0, 1)], xs_hbm.at[pl.ds(0, 1)], sem)
    lax.fori_loop(0, n_pad, lambda r, c: (row_wait.wait(), c)[1], 0)


def _dispatch(h, slot, pad_start, pad_cnt, used_sub):
    return pl.pallas_call(
        _dispatch_kernel,
        grid_spec=pltpu.PrefetchScalarGridSpec(
            num_scalar_prefetch=4,
            grid=(1,),
            in_specs=[pl.BlockSpec(memory_space=pl.ANY)],
            out_specs=pl.BlockSpec(memory_space=pl.ANY),
            scratch_shapes=[pltpu.SemaphoreType.DMA(())],
        ),
        out_shape=jax.ShapeDtypeStruct((N_SLOT, D_MODEL), _F32),
        compiler_params=_params(("arbitrary",)),
        name="dispatch",
    )(slot, pad_start, pad_cnt, used_sub, h)


def _residual_out(x, gate, f, g_ref, o_ref):
    out = x + gate * f
    if g_ref is not None:
        ms = jnp.mean(out * out, axis=-1, keepdims=True)
        out = out * lax.rsqrt(ms + EPS) * g_ref[...]
    o_ref[...] = out


def _combine_kernel(final, x_ref, mod_ref, y_ref, *rest):
    g_ref, o_ref = rest if final else (None, rest[0])
    _residual_out(x_ref[...], mod_ref[5:6, :], y_ref[...], g_ref, o_ref)


def _combine(x, mods_l, y, g_final=None):
    final = g_final is not None
    row = lambda i: (i, 0)
    in_specs = [
        pl.BlockSpec((ROW_TILE, D_MODEL), row),
        pl.BlockSpec((None, 6, D_MODEL), lambda i: (_cond_of_tile(i, ROW_TILE), 0, 0)),
        pl.BlockSpec((ROW_TILE, D_MODEL), row),
    ]
    args = [x, mods_l, y]
    if final:
        in_specs.append(pl.BlockSpec((1, D_MODEL), lambda i: (0, 0)))
        args.append(g_final.reshape(1, D_MODEL))
    return pl.pallas_call(
        functools.partial(_combine_kernel, final),
        grid=(T // ROW_TILE,),
        in_specs=in_specs,
        out_specs=pl.BlockSpec((ROW_TILE, D_MODEL), row),
        out_shape=jax.ShapeDtypeStruct((T, D_MODEL), _F32),
        compiler_params=_params(("arbitrary",)),
        name="combine%s" % ("_final" if final else ""),
    )(*args)


def _combine_top2_kernel(final, slot, x_ref, mod_ref, w_ref, *rest):
    if final:
        g_ref, ys_hbm, o_ref, ybuf, sems = rest
    else:
        g_ref = None
        ys_hbm, o_ref, ybuf, sems = rest
    i = pl.program_id(0)

    def fetch(tile, b):
        def one(r, carry):
            for k in range(TOP_K):
                s = slot[k * T + tile * COMB_TILE + r]
                pltpu.make_async_copy(ys_hbm.at[pl.ds(s, 1)], ybuf.at[b, k, pl.ds(r, 1)], sems.at[b]).start()
            return carry
        lax.fori_loop(0, COMB_TILE, one, 0, unroll=4)

    @pl.when(i == 0)
    def _():
        fetch(0, 0)

    @pl.when(i + 1 < pl.num_programs(0))
    def _():
        fetch(i + 1, (i + 1) % 2)

    b = i % 2
    for k in range(TOP_K):
        pltpu.make_async_copy(ys_hbm.at[pl.ds(0, COMB_TILE)], ybuf.at[b, k], sems.at[b]).wait()
    w = w_ref[...]
    f = w[:, 0:1] * ybuf[b, 0] + w[:, 1:2] * ybuf[b, 1]
    _residual_out(x_ref[...], mod_ref[5:6, :], f, g_ref, o_ref)


def _combine_top2(x, mods_l, ys, slot, w, g_final=None):
    final = g_final is not None
    row = lambda i, s: (i, 0)
    in_specs = [
        pl.BlockSpec((COMB_TILE, D_MODEL), row),
        pl.BlockSpec((None, 6, D_MODEL), lambda i, s: (_cond_of_tile(i, COMB_TILE), 0, 0)),
        pl.BlockSpec((COMB_TILE, TOP_K), row),
    ]
    args = [x, mods_l, w]
    if final:
        in_specs.append(pl.BlockSpec((1, D_MODEL), lambda i, s: (0, 0)))
        args.append(g_final.reshape(1, D_MODEL))
    in_specs.append(pl.BlockSpec(memory_space=pl.ANY))
    args.append(ys)
    return pl.pallas_call(
        functools.partial(_combine_top2_kernel, final),
        grid_spec=pltpu.PrefetchScalarGridSpec(
            num_scalar_prefetch=1,
            grid=(T // COMB_TILE,),
            in_specs=in_specs,
            out_specs=pl.BlockSpec((COMB_TILE, D_MODEL), row),
            scratch_shapes=[
                pltpu.VMEM((2, TOP_K, COMB_TILE, D_MODEL), _F32),
                pltpu.SemaphoreType.DMA((2,)),
            ],
        ),
        out_shape=jax.ShapeDtypeStruct((T, D_MODEL), _F32),
        compiler_params=_params(("arbitrary",)),
        name="combine_top2%s" % ("_final" if final else ""),
    )(slot, *args)


def kernel(x_prompt, x_sample, cache_k, cache_v, c, c_ctx, w_ada, b_ada, g_norm_mix, g_norm_ffn,
           w_in, w_out, attn_sink, conv_dw, conv_dw_b, conv_ln_g, conv_ln_b, conv_pw, fnet_w,
           ffn_w_gate, ffn_w_up, ffn_w_down, moe_router, moe_w_gate, moe_w_up, moe_w_down, g_final):
    x = jnp.concatenate([x_prompt.reshape(T_PROMPT, D_MODEL), x_sample.reshape(T_SAMPLE, D_MODEL)], axis=0)
    cond8 = jnp.concatenate([c_ctx[None, :], c, jnp.zeros((N_COND - 1 - DEC_BATCH, D_MODEL), _F32)], axis=0)
    mods = _modulation(cond8, w_ada, b_ada).reshape(DEPTH, N_COND, 6, D_MODEL)
    cos_t, sin_t = _rope_tables()
    ck_all = cache_k.reshape(DEC_BATCH, DEPTH, PAST_LEN, KV_W)
    cv_all = cache_v.reshape(DEC_BATCH, DEPTH, PAST_LEN, KV_W)
    p_blocks = T_PROMPT // DEC_SEQ

    ks, vs = [], []
    for l in range(DEPTH):
        q, k, v, uc, uf = _inproj(x, mods[l], g_norm_mix[l], w_in[l], cos_t, sin_t)
        ks.append(k[:T_PROMPT])
        vs.append(v[:T_PROMPT])
        attn = jnp.concatenate([
            _ctx_attention(attn_sink[l], q, k, v),
            _lat_attention(attn_sink[l], q, k, v, ck_all[:, l], cv_all[:, l]),
        ], axis=0)
        cargs = (conv_dw[l], conv_dw_b[l], conv_ln_g[l], conv_ln_b[l], conv_pw[l])
        conv = jnp.concatenate([
            _conv_module(uc, SEQ, BATCH, 0, *cargs),
            _conv_module(uc, DEC_SEQ, DEC_BATCH, p_blocks, *cargs),
        ], axis=0)
        four = jnp.concatenate([
            _fourier_mix(uf, SEQ, BATCH, 0, fnet_w[l]),
            _fourier_mix(uf, DEC_SEQ, DEC_BATCH, p_blocks, fnet_w[l]),
        ], axis=0)
        x = _outproj(attn, conv, four, x, mods[l], w_out[l])
        last = g_final if l == DEPTH - 1 else None
        if l % 2 == 0:
            (h,) = _ffn_in(x, mods[l], g_norm_ffn[l])
            i = l // 2
            y = _grouped_ffn(h, ffn_w_gate[i:i + 1], ffn_w_up[i:i + 1], ffn_w_down[i:i + 1], 256,
                             *_dense_visits())
            x = _combine(x, mods[l], y, g_final=last)
        else:
            i = l // 2
            h, eidx, ew = _ffn_in(x, mods[l], g_norm_ffn[l], moe_router[i].T)
            slot, pad_start, pad_cnt, visits = _routing_plan(eidx)
            xs = _dispatch(h, slot, pad_start, pad_cnt, visits[-1])
            ys = _grouped_ffn(xs, moe_w_gate[i], moe_w_up[i], moe_w_down[i], 512, *visits)
            x = _combine_top2(x, mods[l], ys, slot, ew.T, g_final=last)

    y_prompt = x[:T_PROMPT].reshape(BATCH, SEQ, D_MODEL)
    y_sample = x[T_PROMPT:].reshape(DEC_BATCH, DEC_SEQ, D_MODEL)
    state_k = jnp.stack([a.reshape(BATCH, SEQ, N_KV_HEADS, HEAD_DIM) for a in ks], axis=1)
    state_v = jnp.stack([a.reshape(BATCH, SEQ, N_KV_HEADS, HEAD_DIM) for a in vs], axis=1)
    return (y_prompt, y_sample, state_k, state_v)
```

```python
import functools

import numpy as np
import jax
import jax.numpy as jnp
from jax import lax
from jax.experimental import pallas as pl
from jax.experimental.pallas import tpu as pltpu

D_MODEL = 1024
BATCH = 16
SEQ = 256
DEPTH = 2
DEC_BATCH = 2
DEC_SEQ = 2048
PAST_LEN = 512
GRID_W = 64
HEAD_DIM = 64
N_HEADS = 8
N_KV_HEADS = 2
GQA_GROUP = N_HEADS // N_KV_HEADS
ATTN_W = N_HEADS * HEAD_DIM
KV_W = N_KV_HEADS * HEAD_DIM
WINDOW = 128
BLOCK = 128
ROPE_THETA = 10000.0
CONV_W = D_MODEL // 4
CONV_K = 31
FNET_GROUPS = 4
FNET_W = D_MODEL // 4
FNET_GW = FNET_W // FNET_GROUPS
IN_WIDTH = ATTN_W + 2 * KV_W + 2 * CONV_W + FNET_W
D_FF = 2816
N_EXPERTS = 8
TOP_K = 2
D_FF_EXPERT = 3584
EPS = 1e-6
NEG = -1e30

T_PROMPT = BATCH * SEQ
T_SAMPLE = DEC_BATCH * DEC_SEQ
T = T_PROMPT + T_SAMPLE
N_COND = 8

ROW_TILE = 512
SUB = 256
SUB_MAX = 16
TMAX = SUB * SUB_MAX
BLK_SUBS = 4
COMB_TILE = 256
CONV_CHUNK = 128
CONV_PAD = 16
VMEM_LIMIT = 48 * 1024 * 1024
FFN_VMEM_LIMIT = 56 * 1024 * 1024

_BF = jnp.bfloat16
_F32 = jnp.float32


def _cond_of_tile(i, tile):
    r = i * tile
    return jnp.where(r < T_PROMPT, 0, 1 + (r - T_PROMPT) // DEC_SEQ)


def _params(sem, vmem=VMEM_LIMIT):
    return pltpu.CompilerParams(dimension_semantics=sem, vmem_limit_bytes=vmem)


def _sigmoid(x):
    return 1.0 / (1.0 + jnp.exp(-x))


def _mod_kernel(cond_ref, w_ref, b_ref, o_ref):
    cnd = cond_ref[...]
    s = (cnd * _sigmoid(cnd)).astype(_BF)
    o_ref[...] = jnp.dot(s, w_ref[...].astype(_BF), preferred_element_type=_F32) + b_ref[...]


def _modulation(cond8, w_ada, b_ada):
    nt = 1536
    return pl.pallas_call(
        _mod_kernel,
        grid=(DEPTH, 6 * D_MODEL // nt),
        in_specs=[
            pl.BlockSpec((N_COND, D_MODEL), lambda l, n: (0, 0)),
            pl.BlockSpec((None, D_MODEL, nt), lambda l, n: (l, 0, n)),
            pl.BlockSpec((None, 1, nt), lambda l, n: (l, 0, n)),
        ],
        out_specs=pl.BlockSpec((None, N_COND, nt), lambda l, n: (l, 0, n)),
        out_shape=jax.ShapeDtypeStruct((DEPTH, N_COND, 6 * D_MODEL), _F32),
        compiler_params=_params(("arbitrary", "arbitrary")),
        name="modulation",
    )(cond8, w_ada, b_ada.reshape(DEPTH, 1, 6 * D_MODEL))


def _norm_mod(x, g, shift, scale):
    ms = jnp.mean(x * x, axis=-1, keepdims=True)
    y = x * lax.rsqrt(ms + EPS) * g
    return y * (1.0 + scale) + shift


def _rope_tables():
    rows = DEC_SEQ // GRID_W
    n_freq = HEAD_DIM // 4
    inv = ROPE_THETA ** (-jnp.arange(n_freq, dtype=_F32) / n_freq)
    gr, gc = jnp.meshgrid(jnp.arange(rows, dtype=_F32), jnp.arange(GRID_W, dtype=_F32), indexing="ij")
    ang_r = gr.reshape(-1)[:, None] * inv
    ang_c = gc.reshape(-1)[:, None] * inv
    cr, sr, cc, sc = jnp.cos(ang_r), jnp.sin(ang_r), jnp.cos(ang_c), jnp.sin(ang_c)
    cos64 = jnp.concatenate([cr, cr, cc, cc], axis=-1)
    sin64 = jnp.concatenate([-sr, sr, -sc, sc], axis=-1)
    return jnp.tile(cos64, (1, 2)), jnp.tile(sin64, (1, 2))


def _rope128(x, cos, sin):
    lane = lax.broadcasted_iota(jnp.int32, x.shape, 1)
    first = (lane % 32) < 16
    partner = jnp.where(first, pltpu.roll(x, 128 - 16, 1), pltpu.roll(x, 16, 1))
    return x * cos + partner * sin


def _inproj_kernel(x_ref, mod_ref, g_ref, w_ref, cos_ref, sin_ref,
                   q_ref, k_ref, v_ref, uc_ref, uf_ref, wb_ref):
    i = pl.program_id(0)

    @pl.when(i == 0)
    def _():
        wb_ref[...] = w_ref[...].astype(_BF)

    h = _norm_mod(x_ref[...], g_ref[...], mod_ref[0:1, :], mod_ref[1:2, :]).astype(_BF)
    acc = jnp.dot(h, wb_ref[...], preferred_element_type=_F32)
    o = 0
    q_ref[...] = acc[:, o:o + ATTN_W]
    o += ATTN_W
    k_ref[...] = acc[:, o:o + KV_W]
    o += KV_W
    v_ref[...] = acc[:, o:o + KV_W]
    o += KV_W
    uc_ref[...] = acc[:, o:o + 2 * CONV_W]
    o += 2 * CONV_W
    uf_ref[...] = acc[:, o:o + FNET_W]

    @pl.when(i >= T_PROMPT // ROW_TILE)
    def _():
        cos = cos_ref[...]
        sin = sin_ref[...]
        for c in range(ATTN_W // 128):
            q_ref[:, c * 128:(c + 1) * 128] = _rope128(q_ref[:, c * 128:(c + 1) * 128], cos, sin)
        k_ref[...] = _rope128(k_ref[...], cos, sin)


def _inproj(x, mods_l, g, w, cos_t, sin_t):
    nt = T // ROW_TILE
    pt = T_PROMPT // ROW_TILE
    per_seq = DEC_SEQ // ROW_TILE

    def rope_idx(i):
        return (jnp.maximum(i - pt, 0) % per_seq, 0)

    row = lambda i: (i, 0)
    outs = [ATTN_W, KV_W, KV_W, 2 * CONV_W, FNET_W]
    return pl.pallas_call(
        _inproj_kernel,
        grid=(nt,),
        in_specs=[
            pl.BlockSpec((ROW_TILE, D_MODEL), row),
            pl.BlockSpec((None, 6, D_MODEL), lambda i: (_cond_of_tile(i, ROW_TILE), 0, 0)),
            pl.BlockSpec((1, D_MODEL), lambda i: (0, 0)),
            pl.BlockSpec((D_MODEL, IN_WIDTH), lambda i: (0, 0)),
            pl.BlockSpec((ROW_TILE, 128), rope_idx),
            pl.BlockSpec((ROW_TILE, 128), rope_idx),
        ],
        out_specs=[pl.BlockSpec((ROW_TILE, n), row) for n in outs],
        out_shape=[jax.ShapeDtypeStruct((T, n), _F32) for n in outs],
        scratch_shapes=[pltpu.VMEM((D_MODEL, IN_WIDTH), _BF)],
        compiler_params=_params(("arbitrary",)),
        name="inproj",
    )(x, mods_l, g.reshape(1, D_MODEL), w, cos_t, sin_t)


def _stack_groups(q, kh, rows):
    parts = [q[:, (kh * GQA_GROUP + g) * HEAD_DIM:(kh * GQA_GROUP + g + 1) * HEAD_DIM] for g in range(GQA_GROUP)]
    return jnp.concatenate(parts, axis=0)


def _sink_column(sink_ref, kh, rows):
    r = lax.broadcasted_iota(jnp.int32, (GQA_GROUP * rows, 1), 0)
    col = jnp.full((GQA_GROUP * rows, 1), sink_ref[kh * GQA_GROUP], _F32)
    for g in range(1, GQA_GROUP):
        col = jnp.where(r >= g * rows, sink_ref[kh * GQA_GROUP + g], col)
    return col


def _unstack_store(o_ref, o, kh, rows):
    for pair in range(GQA_GROUP // 2):
        a = o[(2 * pair) * rows:(2 * pair + 1) * rows]
        b = o[(2 * pair + 1) * rows:(2 * pair + 2) * rows]
        c0 = (kh * GQA_GROUP + 2 * pair) * HEAD_DIM
        o_ref[:, c0:c0 + 2 * HEAD_DIM] = jnp.concatenate([a, b], axis=-1).astype(o_ref.dtype)


_NT = (((1,), (1,)), ((), ()))


def _ctx_attn_kernel(sink_ref, q_ref, k_ref, v_ref, o_ref):
    scale = HEAD_DIM ** -0.5
    q = q_ref[...] * scale
    for kh in range(N_KV_HEADS):
        kk = k_ref[:, kh * HEAD_DIM:(kh + 1) * HEAD_DIM].astype(_BF)
        vv = v_ref[:, kh * HEAD_DIM:(kh + 1) * HEAD_DIM].astype(_BF)
        qs = _stack_groups(q, kh, SEQ).astype(_BF)
        s = lax.dot_general(qs, kk, _NT, preferred_element_type=_F32)
        sink = _sink_column(sink_ref, kh, SEQ)
        m = jnp.maximum(jnp.max(s, axis=-1, keepdims=True), sink)
        p = jnp.exp(s - m)
        den = jnp.sum(p, axis=-1, keepdims=True) + jnp.exp(sink - m)
        o = jnp.dot(p.astype(_BF), vv, preferred_element_type=_F32) / den
        _unstack_store(o_ref, o, kh, SEQ)


def _ctx_attention(sink_l, q, k, v):
    row = lambda b, s: (b, 0)
    return pl.pallas_call(
        _ctx_attn_kernel,
        grid_spec=pltpu.PrefetchScalarGridSpec(
            num_scalar_prefetch=1,
            grid=(BATCH,),
            in_specs=[
                pl.BlockSpec((SEQ, ATTN_W), row),
                pl.BlockSpec((SEQ, KV_W), row),
                pl.BlockSpec((SEQ, KV_W), row),
            ],
            out_specs=pl.BlockSpec((SEQ, ATTN_W), row),
        ),
        out_shape=jax.ShapeDtypeStruct((T_PROMPT, ATTN_W), _BF),
        compiler_params=_params(("arbitrary",)),
        name="ctx_attention",
    )(sink_l, q, k, v)


def _lat_attn_kernel(sink_ref, q_ref, k_ref, v_ref, ck_ref, cv_ref, o_ref):
    n = pl.program_id(1)
    scale = HEAD_DIM ** -0.5
    nb = DEC_SEQ // BLOCK
    band = 3 * BLOCK
    start = pl.multiple_of(jnp.clip(n - 1, 0, nb - 3) * BLOCK, BLOCK)
    q = q_ref[...] * scale
    kb = k_ref[pl.ds(start, band), :]
    vb = v_ref[pl.ds(start, band), :]
    rows = GQA_GROUP * BLOCK
    qpos = n * BLOCK + lax.broadcasted_iota(jnp.int32, (rows, band), 0) % BLOCK
    kpos = start + lax.broadcasted_iota(jnp.int32, (rows, band), 1)
    valid = jnp.abs(kpos - qpos) <= WINDOW
    for kh in range(N_KV_HEADS):
        hs = slice(kh * HEAD_DIM, (kh + 1) * HEAD_DIM)
        qs = _stack_groups(q, kh, BLOCK).astype(_BF)
        s_loc = lax.dot_general(qs, kb[:, hs].astype(_BF), _NT, preferred_element_type=_F32)
        s_loc = jnp.where(valid, s_loc, NEG)
        s_ctx = lax.dot_general(qs, ck_ref[:, hs].astype(_BF), _NT, preferred_element_type=_F32)
        sink = _sink_column(sink_ref, kh, BLOCK)
        m = jnp.maximum(jnp.maximum(jnp.max(s_loc, axis=-1, keepdims=True),
                                    jnp.max(s_ctx, axis=-1, keepdims=True)), sink)
        p_loc = jnp.exp(s_loc - m)
        p_ctx = jnp.exp(s_ctx - m)
        den = (jnp.sum(p_loc, axis=-1, keepdims=True) + jnp.sum(p_ctx, axis=-1, keepdims=True)
               + jnp.exp(sink - m))
        o = (jnp.dot(p_loc.astype(_BF), vb[:, hs].astype(_BF), preferred_element_type=_F32)
             + jnp.dot(p_ctx.astype(_BF), cv_ref[:, hs].astype(_BF), preferred_element_type=_F32)) / den
        _unstack_store(o_ref, o, kh, BLOCK)


def _lat_attention(sink_l, q, k, v, ck, cv):
    nb = DEC_SEQ // BLOCK
    q0 = T_PROMPT // BLOCK
    s0 = T_PROMPT // DEC_SEQ
    return pl.pallas_call(
        _lat_attn_kernel,
        grid_spec=pltpu.PrefetchScalarGridSpec(
            num_scalar_prefetch=1,
            grid=(DEC_BATCH, nb),
            in_specs=[
                pl.BlockSpec((BLOCK, ATTN_W), lambda b, n, s: (q0 + b * nb + n, 0)),
                pl.BlockSpec((DEC_SEQ, KV_W), lambda b, n, s: (s0 + b, 0)),
                pl.BlockSpec((DEC_SEQ, KV_W), lambda b, n, s: (s0 + b, 0)),
                pl.BlockSpec((None, PAST_LEN, KV_W), lambda b, n, s: (b, 0, 0)),
                pl.BlockSpec((None, PAST_LEN, KV_W), lambda b, n, s: (b, 0, 0)),
            ],
            out_specs=pl.BlockSpec((BLOCK, ATTN_W), lambda b, n, s: (b * nb + n, 0)),
        ),
        out_shape=jax.ShapeDtypeStruct((T_SAMPLE, ATTN_W), _BF),
        compiler_params=_params(("arbitrary", "arbitrary")),
        name="lat_attention",
    )(sink_l, q, k, v, ck, cv)


def _conv_kernel(seq, u_ref, dw_ref, dwb_ref, lg_ref, lb_ref, pw_ref, o_ref, pad_ref):
    u = u_ref[...]
    pad_ref[0:CONV_PAD, :] = jnp.zeros((CONV_PAD, CONV_W), _F32)
    pad_ref[CONV_PAD + seq:2 * CONV_PAD + seq, :] = jnp.zeros((CONV_PAD, CONV_W), _F32)
    pad_ref[CONV_PAD:CONV_PAD + seq, :] = u[:, :CONV_W] * _sigmoid(u[:, CONV_W:])
    pw = pw_ref[...].astype(_BF)
    off = CONV_PAD - CONV_K // 2

    def chunk(c, carry):
        row = pl.multiple_of(c * CONV_CHUNK, CONV_CHUNK)
        win = pad_ref[pl.ds(row, CONV_CHUNK + 2 * CONV_PAD), :]
        acc = jnp.zeros((CONV_CHUNK, CONV_W), _F32) + dwb_ref[...]
        for t in range(CONV_K):
            acc = acc + win[off + t:off + t + CONV_CHUNK, :] * dw_ref[t:t + 1, :]
        mu = jnp.mean(acc, axis=-1, keepdims=True)
        d = acc - mu
        var = jnp.mean(d * d, axis=-1, keepdims=True)
        y = d * lax.rsqrt(var + EPS) * lg_ref[...] + lb_ref[...]
        y = (y * _sigmoid(y)).astype(_BF)
        o_ref[pl.ds(row, CONV_CHUNK), :] = jnp.dot(y, pw, preferred_element_type=_F32).astype(o_ref.dtype)
        return carry

    lax.fori_loop(0, seq // CONV_CHUNK, chunk, 0)


def _conv_module(uc, seq, nbatch, block0, dw, dwb, lg, lb, pw):
    vec = lambda a: a.reshape(1, CONV_W)
    const = lambda b: (0, 0)
    return pl.pallas_call(
        functools.partial(_conv_kernel, seq),
        grid=(nbatch,),
        in_specs=[
            pl.BlockSpec((seq, 2 * CONV_W), lambda b: (block0 + b, 0)),
            pl.BlockSpec((CONV_K, CONV_W), const),
            pl.BlockSpec((1, CONV_W), const),
            pl.BlockSpec((1, CONV_W), const),
            pl.BlockSpec((1, CONV_W), const),
            pl.BlockSpec((CONV_W, CONV_W), const),
        ],
        out_specs=pl.BlockSpec((seq, CONV_W), lambda b: (b, 0)),
        out_shape=jax.ShapeDtypeStruct((nbatch * seq, CONV_W), _BF),
        scratch_shapes=[pltpu.VMEM((seq + 2 * CONV_PAD, CONV_W), _F32)],
        compiler_params=_params(("arbitrary",)),
        name="conv_module_%d" % seq,
    )(uc, dw, vec(dwb), vec(lg), vec(lb), pw)


def _dft_constants(seq):
    j = np.arange(seq, dtype=np.int64)
    ang = 2.0 * np.pi * ((j[:, None] * j[None, :]) % seq).astype(np.float64) / seq
    cs = np.concatenate([np.cos(ang), -np.sin(ang)], axis=1) / np.sqrt(seq)
    c = np.arange(FNET_GW, dtype=np.int64)
    angc = 2.0 * np.pi * ((c[:, None] * c[None, :]) % FNET_GW).astype(np.float64) / FNET_GW
    eye = np.eye(FNET_GROUPS)
    cc = np.kron(eye, np.cos(angc)) / np.sqrt(FNET_GW)
    sc = np.kron(eye, np.sin(angc)) / np.sqrt(FNET_GW)
    w1 = np.concatenate([cc, sc], axis=1)
    return jnp.asarray(cs, dtype=_F32), jnp.asarray(w1, dtype=_F32)


def _fnet_kernel(seq, u_ref, cs_ref, w1_ref, fw_ref, o_ref, xcs_ref):
    @pl.when(pl.program_id(1) == 0)
    def _():
        t = jnp.dot(u_ref[...].astype(_BF), w1_ref[...].astype(_BF), preferred_element_type=_F32)
        xcs_ref[0:seq, :] = t[:, :FNET_W].astype(_BF)
        xcs_ref[seq:2 * seq, :] = t[:, FNET_W:].astype(_BF)

    mixed = jnp.dot(cs_ref[...].astype(_BF), xcs_ref[...], preferred_element_type=_F32)
    o_ref[...] = jnp.dot(mixed.astype(_BF), fw_ref[...].astype(_BF),
                         preferred_element_type=_F32).astype(o_ref.dtype)


def _fourier_mix(uf, seq, nbatch, block0, fw):
    tr = min(seq, 512)
    nr = seq // tr
    cs, w1 = _dft_constants(seq)
    return pl.pallas_call(
        functools.partial(_fnet_kernel, seq),
        grid=(nbatch, nr),
        in_specs=[
            pl.BlockSpec((seq, FNET_W), lambda b, r: (block0 + b, 0)),
            pl.BlockSpec((tr, 2 * seq), lambda b, r: (r, 0)),
            pl.BlockSpec((FNET_W, 2 * FNET_W), lambda b, r: (0, 0)),
            pl.BlockSpec((FNET_W, FNET_W), lambda b, r: (0, 0)),
        ],
        out_specs=pl.BlockSpec((tr, FNET_W), lambda b, r: (b * nr + r, 0)),
        out_shape=jax.ShapeDtypeStruct((nbatch * seq, FNET_W), _BF),
        scratch_shapes=[pltpu.VMEM((2 * seq, FNET_W), _BF)],
        compiler_params=_params(("arbitrary", "arbitrary")),
        name="fourier_mix_%d" % seq,
    )(uf, cs, w1, fw)


def _outproj_kernel(a_ref, c_ref, f_ref, x_ref, mod_ref, w_ref, o_ref, wb_ref):
    @pl.when(pl.program_id(0) == 0)
    def _():
        wb_ref[...] = w_ref[...].astype(_BF)

    mix = jnp.concatenate([a_ref[...], c_ref[...], f_ref[...]], axis=-1)
    o_ref[...] = x_ref[...] + mod_ref[2:3, :] * jnp.dot(mix, wb_ref[...], preferred_element_type=_F32)


def _outproj(attn, conv, four, x, mods_l, w):
    row = lambda i: (i, 0)
    return pl.pallas_call(
        _outproj_kernel,
        grid=(T // ROW_TILE,),
        in_specs=[
            pl.BlockSpec((ROW_TILE, ATTN_W), row),
            pl.BlockSpec((ROW_TILE, CONV_W), row),
            pl.BlockSpec((ROW_TILE, FNET_W), row),
            pl.BlockSpec((ROW_TILE, D_MODEL), row),
            pl.BlockSpec((None, 6, D_MODEL), lambda i: (_cond_of_tile(i, ROW_TILE), 0, 0)),
            pl.BlockSpec((D_MODEL, D_MODEL), lambda i: (0, 0)),
        ],
        out_specs=pl.BlockSpec((ROW_TILE, D_MODEL), row),
        out_shape=jax.ShapeDtypeStruct((T, D_MODEL), _F32),
        scratch_shapes=[pltpu.VMEM((D_MODEL, D_MODEL), _BF)],
        compiler_params=_params(("arbitrary",)),
        name="outproj",
    )(attn, conv, four, x, mods_l, w)


def _ffn_in_kernel(route, x_ref, mod_ref, g_ref, *rest):
    if route:
        r_ref, h_ref, ei_ref, ew_ref = rest
    else:
        (h_ref,) = rest
    h = _norm_mod(x_ref[...], g_ref[...], mod_ref[3:4, :], mod_ref[4:5, :])
    h_ref[...] = h
    if route:
        lg = lax.dot_general(r_ref[...].astype(_BF), h.astype(_BF), _NT, preferred_element_type=_F32)
        eid = lax.broadcasted_iota(jnp.int32, lg.shape, 0)
        m1 = jnp.max(lg, axis=0, keepdims=True)
        i1 = jnp.min(jnp.where(lg == m1, eid, N_EXPERTS), axis=0, keepdims=True)
        lg2 = jnp.where(eid == i1, -jnp.inf, lg)
        m2 = jnp.max(lg2, axis=0, keepdims=True)
        i2 = jnp.min(jnp.where(lg2 == m2, eid, N_EXPERTS), axis=0, keepdims=True)
        e = jnp.exp(m2 - m1)
        ei_ref[0:1, :] = i1
        ei_ref[1:2, :] = i2
        ew_ref[0:1, :] = 1.0 / (1.0 + e)
        ew_ref[1:2, :] = e / (1.0 + e)


def _ffn_in(x, mods_l, g, router_t=None):
    route = router_t is not None
    row = lambda i: (i, 0)
    in_specs = [
        pl.BlockSpec((ROW_TILE, D_MODEL), row),
        pl.BlockSpec((None, 6, D_MODEL), lambda i: (_cond_of_tile(i, ROW_TILE), 0, 0)),
        pl.BlockSpec((1, D_MODEL), lambda i: (0, 0)),
    ]
    out_specs = [pl.BlockSpec((ROW_TILE, D_MODEL), row)]
    out_shape = [jax.ShapeDtypeStruct((T, D_MODEL), _F32)]
    args = [x, mods_l, g.reshape(1, D_MODEL)]
    if route:
        in_specs.append(pl.BlockSpec((N_EXPERTS, D_MODEL), lambda i: (0, 0)))
        out_specs += [pl.BlockSpec((TOP_K, ROW_TILE), lambda i: (0, i))] * 2
        out_shape += [jax.ShapeDtypeStruct((TOP_K, T), jnp.int32), jax.ShapeDtypeStruct((TOP_K, T), _F32)]
        args.append(router_t)
    return pl.pallas_call(
        functools.partial(_ffn_in_kernel, route),
        grid=(T // ROW_TILE,),
        in_specs=in_specs,
        out_specs=out_specs,
        out_shape=out_shape,
        compiler_params=_params(("arbitrary",)),
        name="ffn_in_route" if route else "ffn_in",
    )(*args)


def _ffn_kernel(nj, n_sub_rows, vis_e, vis_start, vis_cnt, used_sub,
                x_hbm, wg_ref, wu_ref, wd_ref, y_hbm,
                big, xb, wgb, wub, wdb, sem_in, sem_out):
    del vis_e
    v = pl.program_id(0)
    j = pl.program_id(1)
    cnt = vis_cnt[v]
    row0 = vis_start[v] * SUB

    def copy_in(s):
        r = pl.multiple_of(row0 + s * SUB, SUB)
        b = pl.multiple_of(s * SUB, SUB)
        return pltpu.make_async_copy(x_hbm.at[pl.ds(r, SUB)], big.at[pl.ds(b, SUB)], sem_in.at[s])

    def copy_out(s):
        r = pl.multiple_of(row0 + s * SUB, SUB)
        b = pl.multiple_of(s * SUB, SUB)
        return pltpu.make_async_copy(big.at[pl.ds(b, SUB)], y_hbm.at[pl.ds(r, SUB)], sem_out)

    def for_subs(fn):
        lax.fori_loop(0, cnt, lambda s, c: (fn(s), c)[1], 0)

    @pl.when(jnp.logical_and(j == 0, cnt > 0))
    def _():
        for_subs(lambda s: copy_in(s).start())

        def land(s):
            copy_in(s).wait()
            rs = pl.ds(pl.multiple_of(s * SUB, SUB), SUB)
            xb[rs, :] = big[rs, :].astype(_BF)
            big[rs, :] = jnp.zeros((SUB, D_MODEL), _F32)
        for_subs(land)

    @pl.when(cnt > 0)
    def _():
        wgb[...] = wg_ref[...].astype(_BF)
        wub[...] = wu_ref[...].astype(_BF)
        wdb[...] = wd_ref[...].astype(_BF)

        def block(r0, nrows):
            rs = pl.ds(r0, nrows)
            x = xb[rs, :]
            a = jnp.dot(x, wgb[...], preferred_element_type=_F32)
            b = jnp.dot(x, wub[...], preferred_element_type=_F32)
            p = (a * _sigmoid(a) * b).astype(_BF)
            big[rs, :] += jnp.dot(p, wdb[...], preferred_element_type=_F32)

        blk = BLK_SUBS * SUB
        nfull = cnt // BLK_SUBS
        lax.fori_loop(0, nfull, lambda i, c: (block(pl.multiple_of(i * blk, blk), blk), c)[1], 0)
        rem = cnt % BLK_SUBS
        base = nfull * blk
        bit = BLK_SUBS // 2
        while bit >= 1:
            take = (rem // bit) % 2 == 1
            pl.when(take)(functools.partial(block, pl.multiple_of(base, SUB), bit * SUB))
            base = base + jnp.where(take, bit * SUB, 0)
            bit //= 2

    @pl.when(jnp.logical_and(j == nj - 1, cnt > 0))
    def _():
        for_subs(lambda s: copy_out(s).start())
        for_subs(lambda s: copy_out(s).wait())

    @pl.when(jnp.logical_and(v == pl.num_programs(0) - 1, j == nj - 1))
    def _():
        def tail(s):
            r = pl.multiple_of(s * SUB, SUB)
            return pltpu.make_async_copy(x_hbm.at[pl.ds(r, SUB)], y_hbm.at[pl.ds(r, SUB)], sem_out)
        lax.fori_loop(used_sub[0], n_sub_rows, lambda s, c: (tail(s).start(), c)[1], 0)
        lax.fori_loop(used_sub[0], n_sub_rows, lambda s, c: (tail(s).wait(), c)[1], 0)


def _grouped_ffn(x, wg, wu, wd, fc, vis_e, vis_start, vis_cnt, used_sub):
    ff = wg.shape[-1]
    nj = ff // fc
    nv = vis_e.shape[0]
    n_rows = x.shape[0]

    def chunk(v, j, vc):
        return jnp.where(vc[v] > 0, j, nj - 1)

    return pl.pallas_call(
        functools.partial(_ffn_kernel, nj, n_rows // SUB),
        grid_spec=pltpu.PrefetchScalarGridSpec(
            num_scalar_prefetch=4,
            grid=(nv, nj),
            in_specs=[
                pl.BlockSpec(memory_space=pl.ANY),
                pl.BlockSpec((None, D_MODEL, fc), lambda v, j, ve, vs, vc, us: (ve[v], 0, chunk(v, j, vc))),
                pl.BlockSpec((None, D_MODEL, fc), lambda v, j, ve, vs, vc, us: (ve[v], 0, chunk(v, j, vc))),
                pl.BlockSpec((None, fc, D_MODEL), lambda v, j, ve, vs, vc, us: (ve[v], chunk(v, j, vc), 0)),
            ],
            out_specs=pl.BlockSpec(memory_space=pl.ANY),
            scratch_shapes=[
                pltpu.VMEM((TMAX, D_MODEL), _F32),
                pltpu.VMEM((TMAX, D_MODEL), _BF),
                pltpu.VMEM((D_MODEL, fc), _BF),
                pltpu.VMEM((D_MODEL, fc), _BF),
                pltpu.VMEM((fc, D_MODEL), _BF),
                pltpu.SemaphoreType.DMA((SUB_MAX,)),
                pltpu.SemaphoreType.DMA(()),
            ],
        ),
        out_shape=jax.ShapeDtypeStruct((n_rows, D_MODEL), _F32),
        compiler_params=_params(("arbitrary", "arbitrary"), FFN_VMEM_LIMIT),
        name="ffn_%d" % ff,
    )(vis_e, vis_start, vis_cnt, used_sub, x, wg, wu, wd)


def _dense_visits():
    nv = T // TMAX
    return (jnp.zeros((nv,), jnp.int32),
            jnp.arange(nv, dtype=jnp.int32) * SUB_MAX,
            jnp.full((nv,), SUB_MAX, jnp.int32),
            jnp.full((1,), T // SUB, jnp.int32))


N_SLOT_SUB = (TOP_K * T) // SUB + N_EXPERTS
N_SLOT = N_SLOT_SUB * SUB
N_VISIT = -(-N_SLOT_SUB // SUB_MAX) + N_EXPERTS


def _routing_plan(eidx):
    e_flat = eidx.reshape(-1)
    onehot = e_flat[:, None] == jnp.arange(N_EXPERTS, dtype=jnp.int32)[None, :]
    ch = 128
    oh = onehot.astype(_F32).reshape(-1, ch, N_EXPERTS)
    tri = (jnp.arange(ch)[:, None] >= jnp.arange(ch)[None, :]).astype(_F32)
    within = jnp.einsum("ij,cjk->cik", tri, oh)
    tot = within[:, -1, :]
    csum = (within + (jnp.cumsum(tot, axis=0) - tot)[:, None, :]).reshape(-1, N_EXPERTS).astype(jnp.int32)
    counts = csum[-1]
    nsub = (counts + SUB - 1) // SUB
    sub_base = jnp.cumsum(nsub) - nsub
    slot = jnp.sum(jnp.where(onehot, csum - 1 + (sub_base * SUB)[None, :], 0), axis=1).astype(jnp.int32)
    pad_start = (sub_base * SUB + counts).astype(jnp.int32)
    pad_cnt = (nsub * SUB - counts).astype(jnp.int32)
    used_sub = jnp.sum(nsub).reshape(1).astype(jnp.int32)
    nvis = (nsub + SUB_MAX - 1) // SUB_MAX
    vend = jnp.cumsum(nvis)
    total = vend[-1]
    vid = jnp.arange(N_VISIT, dtype=jnp.int32)
    ve = jnp.minimum(jnp.sum((vid[:, None] >= vend[None, :]).astype(jnp.int32), axis=1), N_EXPERTS - 1)
    local = vid - (vend - nvis)[ve]
    nv_e = jnp.maximum(nvis[ve], 1)
    q, r = nsub[ve] // nv_e, nsub[ve] % nv_e
    cnt = q + (local < r).astype(jnp.int32)
    start = sub_base[ve] + local * q + jnp.minimum(local, r)
    used = vid < total
    last_e = ve[jnp.maximum(total - 1, 0)]
    vis_e = jnp.where(used, ve, last_e).astype(jnp.int32)
    vis_cnt = jnp.where(used, cnt, 0).astype(jnp.int32)
    vis_start = jnp.where(used, start, 0).astype(jnp.int32)
    return slot, pad_start, pad_cnt, (vis_e, vis_start, vis_cnt, used_sub)


def _dispatch_kernel(slot, pad_start, pad_cnt, used_sub, h_ref, xs_hbm, sem):
    i = pl.program_id(0)

    def put(r, carry):
        for k in range(TOP_K):
            s = slot[k * T + i * ROW_TILE + r]
            pltpu.make_async_copy(h_ref.at[pl.ds(r, 1)], xs_hbm.at[pl.ds(s, 1)], sem).start()
        return carry
    lax.fori_loop(0, ROW_TILE, put, 0, unroll=4)
    for _ in range(TOP_K):
        pltpu.make_async_copy(h_ref, xs_hbm.at[pl.ds(0, ROW_TILE)], sem).wait()

    @pl.when(i == pl.num_programs(0) - 1)
    def _():
        n_pad = 0
        for e in range(N_EXPERTS):
            s0 = pad_start[e]

            def fill(r, carry, s0=s0):
                pltpu.make_async_copy(h_ref.at[pl.ds(0, 1)], xs_hbm.at[pl.ds(s0 + r, 1)], sem).start()
                return carry
            lax.fori_loop(0, pad_cnt[e], fill, 0)
            n_pad = n_pad + pad_cnt[e]

        def tail(s):
            r = pl.multiple_of(s * SUB, SUB)
            return pltpu.make_async_copy(h_ref.at[pl.ds(0, SUB)], xs_hbm.at[pl.ds(r, SUB)], sem)
        lax.fori_loop(used_sub[0], N_SLOT_SUB, lambda s, c: (tail(s).start(), c)[1], 0)
        lax.fori_loop(used_sub[0], N_SLOT_SUB, lambda s, c: (tail(s).wait(), c)[1], 0)
        row_wait = pltpu.make_async_copy(h_ref.at[pl.ds(0, 1)], xs_hbm.at[pl.ds(0, 1)], sem)
        lax.fori_loop(0, n_pad, lambda r, c: (row_wait.wait(), c)[1], 0)


def _dispatch(h, slot, pad_start, pad_cnt, used_sub):
    return pl.pallas_call(
        _dispatch_kernel,
        grid_spec=pltpu.PrefetchScalarGridSpec(
            num_scalar_prefetch=4,
            grid=(T // ROW_TILE,),
            in_specs=[pl.BlockSpec((ROW_TILE, D_MODEL), lambda i, *_: (i, 0))],
            out_specs=pl.BlockSpec(memory_space=pl.ANY),
            scratch_shapes=[pltpu.SemaphoreType.DMA(())],
        ),
        out_shape=jax.ShapeDtypeStruct((N_SLOT, D_MODEL), _F32),
        compiler_params=_params(("arbitrary",)),
        name="dispatch",
    )(slot, pad_start, pad_cnt, used_sub, h)


def _residual_out(x, gate, f, g_ref, o_ref):
    out = x + gate * f
    if g_ref is not None:
        ms = jnp.mean(out * out, axis=-1, keepdims=True)
        out = out * lax.rsqrt(ms + EPS) * g_ref[...]
    o_ref[...] = out


def _combine_kernel(final, x_ref, mod_ref, y_ref, *rest):
    g_ref, o_ref = rest if final else (None, rest[0])
    _residual_out(x_ref[...], mod_ref[5:6, :], y_ref[...], g_ref, o_ref)


def _combine(x, mods_l, y, g_final=None):
    final = g_final is not None
    row = lambda i: (i, 0)
    in_specs = [
        pl.BlockSpec((ROW_TILE, D_MODEL), row),
        pl.BlockSpec((None, 6, D_MODEL), lambda i: (_cond_of_tile(i, ROW_TILE), 0, 0)),
        pl.BlockSpec((ROW_TILE, D_MODEL), row),
    ]
    args = [x, mods_l, y]
    if final:
        in_specs.append(pl.BlockSpec((1, D_MODEL), lambda i: (0, 0)))
        args.append(g_final.reshape(1, D_MODEL))
    return pl.pallas_call(
        functools.partial(_combine_kernel, final),
        grid=(T // ROW_TILE,),
        in_specs=in_specs,
        out_specs=pl.BlockSpec((ROW_TILE, D_MODEL), row),
        out_shape=jax.ShapeDtypeStruct((T, D_MODEL), _F32),
        compiler_params=_params(("arbitrary",)),
        name="combine%s" % ("_final" if final else ""),
    )(*args)


def _combine_top2_kernel(final, slot, x_ref, mod_ref, w_ref, *rest):
    if final:
        g_ref, ys_hbm, o_ref, ybuf, sems = rest
    else:
        g_ref = None
        ys_hbm, o_ref, ybuf, sems = rest
    i = pl.program_id(0)

    def fetch(tile, b):
        def one(r, carry):
            for k in range(TOP_K):
                s = slot[k * T + tile * COMB_TILE + r]
                pltpu.make_async_copy(ys_hbm.at[pl.ds(s, 1)], ybuf.at[b, k, pl.ds(r, 1)], sems.at[b]).start()
            return carry
        lax.fori_loop(0, COMB_TILE, one, 0, unroll=4)

    @pl.when(i == 0)
    def _():
        fetch(0, 0)

    @pl.when(i + 1 < pl.num_programs(0))
    def _():
        fetch(i + 1, (i + 1) % 2)

    b = i % 2
    for k in range(TOP_K):
        pltpu.make_async_copy(ys_hbm.at[pl.ds(0, COMB_TILE)], ybuf.at[b, k], sems.at[b]).wait()
    w = w_ref[...]
    f = w[:, 0:1] * ybuf[b, 0] + w[:, 1:2] * ybuf[b, 1]
    _residual_out(x_ref[...], mod_ref[5:6, :], f, g_ref, o_ref)


def _combine_top2(x, mods_l, ys, slot, w, g_final=None):
    final = g_final is not None
    row = lambda i, s: (i, 0)
    in_specs = [
        pl.BlockSpec((COMB_TILE, D_MODEL), row),
        pl.BlockSpec((None, 6, D_MODEL), lambda i, s: (_cond_of_tile(i, COMB_TILE), 0, 0)),
        pl.BlockSpec((COMB_TILE, TOP_K), row),
    ]
    args = [x, mods_l, w]
    if final:
        in_specs.append(pl.BlockSpec((1, D_MODEL), lambda i, s: (0, 0)))
        args.append(g_final.reshape(1, D_MODEL))
    in_specs.append(pl.BlockSpec(memory_space=pl.ANY))
    args.append(ys)
    return pl.pallas_call(
        functools.partial(_combine_top2_kernel, final),
        grid_spec=pltpu.PrefetchScalarGridSpec(
            num_scalar_prefetch=1,
            grid=(T // COMB_TILE,),
            in_specs=in_specs,
            out_specs=pl.BlockSpec((COMB_TILE, D_MODEL), row),
            scratch_shapes=[
                pltpu.VMEM((2, TOP_K, COMB_TILE, D_MODEL), _F32),
                pltpu.SemaphoreType.DMA((2,)),
            ],
        ),
        out_shape=jax.ShapeDtypeStruct((T, D_MODEL), _F32),
        compiler_params=_params(("arbitrary",)),
        name="combine_top2%s" % ("_final" if final else ""),
    )(slot, *args)


def kernel(x_prompt, x_sample, cache_k, cache_v, c, c_ctx, w_ada, b_ada, g_norm_mix, g_norm_ffn,
           w_in, w_out, attn_sink, conv_dw, conv_dw_b, conv_ln_g, conv_ln_b, conv_pw, fnet_w,
           ffn_w_gate, ffn_w_up, ffn_w_down, moe_router, moe_w_gate, moe_w_up, moe_w_down, g_final):
    x = jnp.concatenate([x_prompt.reshape(T_PROMPT, D_MODEL), x_sample.reshape(T_SAMPLE, D_MODEL)], axis=0)
    cond8 = jnp.concatenate([c_ctx[None, :], c, jnp.zeros((N_COND - 1 - DEC_BATCH, D_MODEL), _F32)], axis=0)
    mods = _modulation(cond8, w_ada, b_ada).reshape(DEPTH, N_COND, 6, D_MODEL)
    cos_t, sin_t = _rope_tables()
    ck_all = cache_k.reshape(DEC_BATCH, DEPTH, PAST_LEN, KV_W)
    cv_all = cache_v.reshape(DEC_BATCH, DEPTH, PAST_LEN, KV_W)
    p_blocks = T_PROMPT // DEC_SEQ

    ks, vs = [], []
    for l in range(DEPTH):
        q, k, v, uc, uf = _inproj(x, mods[l], g_norm_mix[l], w_in[l], cos_t, sin_t)
        ks.append(k[:T_PROMPT])
        vs.append(v[:T_PROMPT])
        attn = jnp.concatenate([
            _ctx_attention(attn_sink[l], q, k, v),
            _lat_attention(attn_sink[l], q, k, v, ck_all[:, l], cv_all[:, l]),
        ], axis=0)
        cargs = (conv_dw[l], conv_dw_b[l], conv_ln_g[l], conv_ln_b[l], conv_pw[l])
        conv = jnp.concatenate([
            _conv_module(uc, SEQ, BATCH, 0, *cargs),
            _conv_module(uc, DEC_SEQ, DEC_BATCH, p_blocks, *cargs),
        ], axis=0)
        four = jnp.concatenate([
            _fourier_mix(uf, SEQ, BATCH, 0, fnet_w[l]),
            _fourier_mix(uf, DEC_SEQ, DEC_BATCH, p_blocks, fnet_w[l]),
        ], axis=0)
        x = _outproj(attn, conv, four, x, mods[l], w_out[l])
        last = g_final if l == DEPTH - 1 else None
        if l % 2 == 0:
            (h,) = _ffn_in(x, mods[l], g_norm_ffn[l])
            i = l // 2
            y = _grouped_ffn(h, ffn_w_gate[i:i + 1], ffn_w_up[i:i + 1], ffn_w_down[i:i + 1], 256,
                             *_dense_visits())
            x = _combine(x, mods[l], y, g_final=last)
        else:
            i = l // 2
            h, eidx, ew = _ffn_in(x, mods[l], g_norm_ffn[l], moe_router[i].T)
            slot, pad_start, pad_cnt, visits = _routing_plan(eidx)
            xs = _dispatch(h, slot, pad_start, pad_cnt, visits[-1])
            ys = _grouped_ffn(xs, moe_w_gate[i], moe_w_up[i], moe_w_down[i], 512, *visits)
            x = _combine_top2(x, mods[l], ys, slot, ew.T, g_final=last)

    y_prompt = x[:T_PROMPT].reshape(BATCH, SEQ, D_MODEL)
    y_sample = x[T_PROMPT:].reshape(DEC_BATCH, DEC_SEQ, D_MODEL)
    state_k = jnp.stack([a.reshape(BATCH, SEQ, N_KV_HEADS, HEAD_DIM) for a in ks], axis=1)
    state_v = jnp.stack([a.reshape(BATCH, SEQ, N_KV_HEADS, HEAD_DIM) for a in vs], axis=1)
    return (y_prompt, y_sample, state_k, state_v)
```

```python
import functools

import numpy as np
import jax
import jax.numpy as jnp
from jax import lax
from jax.experimental import pallas as pl
from jax.experimental.pallas import tpu as pltpu

D_MODEL = 1024
BATCH = 16
SEQ = 256
DEPTH = 2
DEC_BATCH = 2
DEC_SEQ = 2048
PAST_LEN = 512
GRID_W = 64
HEAD_DIM = 64
N_HEADS = 8
N_KV_HEADS = 2
GQA_GROUP = N_HEADS // N_KV_HEADS
ATTN_W = N_HEADS * HEAD_DIM
KV_W = N_KV_HEADS * HEAD_DIM
WINDOW = 128
BLOCK = 128
ROPE_THETA = 10000.0
CONV_W = D_MODEL // 4
CONV_K = 31
FNET_GROUPS = 4
FNET_W = D_MODEL // 4
FNET_GW = FNET_W // FNET_GROUPS
IN_WIDTH = ATTN_W + 2 * KV_W + 2 * CONV_W + FNET_W
D_FF = 2816
N_EXPERTS = 8
TOP_K = 2
D_FF_EXPERT = 3584
EPS = 1e-6
NEG = -1e30

T_PROMPT = BATCH * SEQ
T_SAMPLE = DEC_BATCH * DEC_SEQ
T = T_PROMPT + T_SAMPLE
N_COND = 8

ROW_TILE = 512
SUB = 256
SUB_MAX = 16
TMAX = SUB * SUB_MAX
BLK_SUBS = 4
COMB_TILE = 256
CONV_CHUNK = 128
CONV_PAD = 16
VMEM_LIMIT = 48 * 1024 * 1024
FFN_VMEM_LIMIT = 56 * 1024 * 1024

_BF = jnp.bfloat16
_F32 = jnp.float32


def _cond_of_tile(i, tile):
    r = i * tile
    return jnp.where(r < T_PROMPT, 0, 1 + (r - T_PROMPT) // DEC_SEQ)


def _params(sem, vmem=VMEM_LIMIT):
    return pltpu.CompilerParams(dimension_semantics=sem, vmem_limit_bytes=vmem)


def _sigmoid(x):
    return 1.0 / (1.0 + jnp.exp(-x))


def _mod_kernel(cond_ref, w_ref, b_ref, o_ref):
    cnd = cond_ref[...]
    s = (cnd * _sigmoid(cnd)).astype(_BF)
    o_ref[...] = jnp.dot(s, w_ref[...].astype(_BF), preferred_element_type=_F32) + b_ref[...]


def _modulation(cond8, w_ada, b_ada):
    nt = 1536
    return pl.pallas_call(
        _mod_kernel,
        grid=(DEPTH, 6 * D_MODEL // nt),
        in_specs=[
            pl.BlockSpec((N_COND, D_MODEL), lambda l, n: (0, 0)),
            pl.BlockSpec((None, D_MODEL, nt), lambda l, n: (l, 0, n)),
            pl.BlockSpec((None, 1, nt), lambda l, n: (l, 0, n)),
        ],
        out_specs=pl.BlockSpec((None, N_COND, nt), lambda l, n: (l, 0, n)),
        out_shape=jax.ShapeDtypeStruct((DEPTH, N_COND, 6 * D_MODEL), _F32),
        compiler_params=_params(("arbitrary", "arbitrary")),
        name="modulation",
    )(cond8, w_ada, b_ada.reshape(DEPTH, 1, 6 * D_MODEL))


def _norm_mod(x, g, shift, scale):
    ms = jnp.mean(x * x, axis=-1, keepdims=True)
    y = x * lax.rsqrt(ms + EPS) * g
    return y * (1.0 + scale) + shift


def _rope_tables():
    rows = DEC_SEQ // GRID_W
    n_freq = HEAD_DIM // 4
    inv = ROPE_THETA ** (-jnp.arange(n_freq, dtype=_F32) / n_freq)
    gr, gc = jnp.meshgrid(jnp.arange(rows, dtype=_F32), jnp.arange(GRID_W, dtype=_F32), indexing="ij")
    ang_r = gr.reshape(-1)[:, None] * inv
    ang_c = gc.reshape(-1)[:, None] * inv
    cr, sr, cc, sc = jnp.cos(ang_r), jnp.sin(ang_r), jnp.cos(ang_c), jnp.sin(ang_c)
    cos64 = jnp.concatenate([cr, cr, cc, cc], axis=-1)
    sin64 = jnp.concatenate([-sr, sr, -sc, sc], axis=-1)
    return jnp.tile(cos64, (1, 2)), jnp.tile(sin64, (1, 2))


def _rope128(x, cos, sin):
    lane = lax.broadcasted_iota(jnp.int32, x.shape, 1)
    first = (lane % 32) < 16
    partner = jnp.where(first, pltpu.roll(x, 128 - 16, 1), pltpu.roll(x, 16, 1))
    return x * cos + partner * sin


def _half_specs(width, tile):
    pt = T_PROMPT // tile
    return [pl.BlockSpec((tile, width), lambda i, *_: (jnp.minimum(i, pt - 1), 0)),
            pl.BlockSpec((tile, width), lambda i, *_: (jnp.maximum(i - pt, 0), 0))]


def _pick_half(i, tile, a_ref, b_ref):
    return jnp.where(i < T_PROMPT // tile, a_ref[...], b_ref[...])


def _layer_specs(l, w_shape):
    return [
        pl.BlockSpec((None, None, 6, D_MODEL), lambda i, *_: (l, _cond_of_tile(i, ROW_TILE), 0, 0)),
        pl.BlockSpec((None, 1, D_MODEL), lambda i, *_: (l, 0, 0)),
        pl.BlockSpec((None,) + w_shape, lambda i, *_: (l, 0, 0)),
    ]


def _inproj_kernel(split, *refs):
    n_x = 2 if split else 1
    x_refs, refs = refs[:n_x], refs[n_x:]
    mod_ref, g_ref, w_ref, cos_ref, sin_ref, q_ref, k_ref, v_ref, uc_ref, uf_ref, wb_ref = refs
    i = pl.program_id(0)

    @pl.when(i == 0)
    def _():
        wb_ref[...] = w_ref[...].astype(_BF)

    x = _pick_half(i, ROW_TILE, *x_refs) if split else x_refs[0][...]
    h = _norm_mod(x, g_ref[...], mod_ref[0:1, :], mod_ref[1:2, :]).astype(_BF)
    acc = jnp.dot(h, wb_ref[...], preferred_element_type=_F32)
    o = 0
    q_ref[...] = acc[:, o:o + ATTN_W]
    o += ATTN_W
    k_ref[...] = acc[:, o:o + KV_W]
    o += KV_W
    v_ref[...] = acc[:, o:o + KV_W]
    o += KV_W
    uc_ref[...] = acc[:, o:o + 2 * CONV_W]
    o += 2 * CONV_W
    uf_ref[...] = acc[:, o:o + FNET_W]

    @pl.when(i >= T_PROMPT // ROW_TILE)
    def _():
        cos = cos_ref[...]
        sin = sin_ref[...]
        for c in range(ATTN_W // 128):
            q_ref[:, c * 128:(c + 1) * 128] = _rope128(q_ref[:, c * 128:(c + 1) * 128], cos, sin)
        k_ref[...] = _rope128(k_ref[...], cos, sin)


def _inproj(xs, l, mods, g_all, w_all, cos_t, sin_t):
    split = len(xs) == 2
    pt = T_PROMPT // ROW_TILE
    per_seq = DEC_SEQ // ROW_TILE

    def rope_idx(i):
        return (jnp.maximum(i - pt, 0) % per_seq, 0)

    row = lambda i: (i, 0)
    x_specs = _half_specs(D_MODEL, ROW_TILE) if split else [pl.BlockSpec((ROW_TILE, D_MODEL), row)]
    outs = [ATTN_W, KV_W, KV_W, 2 * CONV_W, FNET_W]
    return pl.pallas_call(
        functools.partial(_inproj_kernel, split),
        grid=(T // ROW_TILE,),
        in_specs=x_specs + _layer_specs(l, (D_MODEL, IN_WIDTH)) + [
            pl.BlockSpec((ROW_TILE, 128), rope_idx),
            pl.BlockSpec((ROW_TILE, 128), rope_idx),
        ],
        out_specs=[pl.BlockSpec((ROW_TILE, n), row) for n in outs],
        out_shape=[jax.ShapeDtypeStruct((T, n), _F32) for n in outs],
        scratch_shapes=[pltpu.VMEM((D_MODEL, IN_WIDTH), _BF)],
        compiler_params=_params(("arbitrary",)),
        name="inproj",
    )(*xs, mods, g_all.reshape(DEPTH, 1, D_MODEL), w_all, cos_t, sin_t)


def _stack_groups(q, kh, rows):
    parts = [q[:, (kh * GQA_GROUP + g) * HEAD_DIM:(kh * GQA_GROUP + g + 1) * HEAD_DIM] for g in range(GQA_GROUP)]
    return jnp.concatenate(parts, axis=0)


def _sink_column(sink_ref, kh, rows):
    r = lax.broadcasted_iota(jnp.int32, (GQA_GROUP * rows, 1), 0)
    col = jnp.full((GQA_GROUP * rows, 1), sink_ref[kh * GQA_GROUP], _F32)
    for g in range(1, GQA_GROUP):
        col = jnp.where(r >= g * rows, sink_ref[kh * GQA_GROUP + g], col)
    return col


def _unstack_store(o_ref, o, kh, rows):
    for pair in range(GQA_GROUP // 2):
        a = o[(2 * pair) * rows:(2 * pair + 1) * rows]
        b = o[(2 * pair + 1) * rows:(2 * pair + 2) * rows]
        c0 = (kh * GQA_GROUP + 2 * pair) * HEAD_DIM
        o_ref[:, c0:c0 + 2 * HEAD_DIM] = jnp.concatenate([a, b], axis=-1).astype(o_ref.dtype)


_NT = (((1,), (1,)), ((), ()))


def _ctx_attn_kernel(sink_ref, q_ref, k_ref, v_ref, o_ref):
    scale = HEAD_DIM ** -0.5
    q = q_ref[...] * scale
    for kh in range(N_KV_HEADS):
        kk = k_ref[:, kh * HEAD_DIM:(kh + 1) * HEAD_DIM].astype(_BF)
        vv = v_ref[:, kh * HEAD_DIM:(kh + 1) * HEAD_DIM].astype(_BF)
        qs = _stack_groups(q, kh, SEQ).astype(_BF)
        s = lax.dot_general(qs, kk, _NT, preferred_element_type=_F32)
        sink = _sink_column(sink_ref, kh, SEQ)
        m = jnp.maximum(jnp.max(s, axis=-1, keepdims=True), sink)
        p = jnp.exp(s - m)
        den = jnp.sum(p, axis=-1, keepdims=True) + jnp.exp(sink - m)
        o = jnp.dot(p.astype(_BF), vv, preferred_element_type=_F32) / den
        _unstack_store(o_ref, o, kh, SEQ)


def _ctx_attention(sink_l, q, k, v):
    row = lambda b, s: (b, 0)
    return pl.pallas_call(
        _ctx_attn_kernel,
        grid_spec=pltpu.PrefetchScalarGridSpec(
            num_scalar_prefetch=1,
            grid=(BATCH,),
            in_specs=[
                pl.BlockSpec((SEQ, ATTN_W), row),
                pl.BlockSpec((SEQ, KV_W), row),
                pl.BlockSpec((SEQ, KV_W), row),
            ],
            out_specs=pl.BlockSpec((SEQ, ATTN_W), row),
        ),
        out_shape=jax.ShapeDtypeStruct((T_PROMPT, ATTN_W), _BF),
        compiler_params=_params(("arbitrary",)),
        name="ctx_attention",
    )(sink_l, q, k, v)


def _lat_attn_kernel(sink_ref, q_ref, k_ref, v_ref, ck_ref, cv_ref, o_ref):
    n = pl.program_id(1)
    scale = HEAD_DIM ** -0.5
    nb = DEC_SEQ // BLOCK
    band = 3 * BLOCK
    start = pl.multiple_of(jnp.clip(n - 1, 0, nb - 3) * BLOCK, BLOCK)
    q = q_ref[...] * scale
    kb = k_ref[pl.ds(start, band), :]
    vb = v_ref[pl.ds(start, band), :]
    rows = GQA_GROUP * BLOCK
    qpos = n * BLOCK + lax.broadcasted_iota(jnp.int32, (rows, band), 0) % BLOCK
    kpos = start + lax.broadcasted_iota(jnp.int32, (rows, band), 1)
    valid = jnp.abs(kpos - qpos) <= WINDOW
    for kh in range(N_KV_HEADS):
        hs = slice(kh * HEAD_DIM, (kh + 1) * HEAD_DIM)
        qs = _stack_groups(q, kh, BLOCK).astype(_BF)
        s_loc = lax.dot_general(qs, kb[:, hs].astype(_BF), _NT, preferred_element_type=_F32)
        s_loc = jnp.where(valid, s_loc, NEG)
        s_ctx = lax.dot_general(qs, ck_ref[:, hs].astype(_BF), _NT, preferred_element_type=_F32)
        sink = _sink_column(sink_ref, kh, BLOCK)
        m = jnp.maximum(jnp.maximum(jnp.max(s_loc, axis=-1, keepdims=True),
                                    jnp.max(s_ctx, axis=-1, keepdims=True)), sink)
        p_loc = jnp.exp(s_loc - m)
        p_ctx = jnp.exp(s_ctx - m)
        den = (jnp.sum(p_loc, axis=-1, keepdims=True) + jnp.sum(p_ctx, axis=-1, keepdims=True)
               + jnp.exp(sink - m))
        o = (jnp.dot(p_loc.astype(_BF), vb[:, hs].astype(_BF), preferred_element_type=_F32)
             + jnp.dot(p_ctx.astype(_BF), cv_ref[:, hs].astype(_BF), preferred_element_type=_F32)) / den
        _unstack_store(o_ref, o, kh, BLOCK)


def _lat_attention(sink_l, q, k, v, ck, cv):
    nb = DEC_SEQ // BLOCK
    q0 = T_PROMPT // BLOCK
    s0 = T_PROMPT // DEC_SEQ
    return pl.pallas_call(
        _lat_attn_kernel,
        grid_spec=pltpu.PrefetchScalarGridSpec(
            num_scalar_prefetch=1,
            grid=(DEC_BATCH, nb),
            in_specs=[
                pl.BlockSpec((BLOCK, ATTN_W), lambda b, n, s: (q0 + b * nb + n, 0)),
                pl.BlockSpec((DEC_SEQ, KV_W), lambda b, n, s: (s0 + b, 0)),
                pl.BlockSpec((DEC_SEQ, KV_W), lambda b, n, s: (s0 + b, 0)),
                pl.BlockSpec((None, PAST_LEN, KV_W), lambda b, n, s: (b, 0, 0)),
                pl.BlockSpec((None, PAST_LEN, KV_W), lambda b, n, s: (b, 0, 0)),
            ],
            out_specs=pl.BlockSpec((BLOCK, ATTN_W), lambda b, n, s: (b * nb + n, 0)),
        ),
        out_shape=jax.ShapeDtypeStruct((T_SAMPLE, ATTN_W), _BF),
        compiler_params=_params(("arbitrary", "arbitrary")),
        name="lat_attention",
    )(sink_l, q, k, v, ck, cv)


def _conv_kernel(seq, u_ref, dw_ref, dwb_ref, lg_ref, lb_ref, pw_ref, o_ref, pad_ref):
    u = u_ref[...]
    pad_ref[0:CONV_PAD, :] = jnp.zeros((CONV_PAD, CONV_W), _F32)
    pad_ref[CONV_PAD + seq:2 * CONV_PAD + seq, :] = jnp.zeros((CONV_PAD, CONV_W), _F32)
    pad_ref[CONV_PAD:CONV_PAD + seq, :] = u[:, :CONV_W] * _sigmoid(u[:, CONV_W:])
    pw = pw_ref[...].astype(_BF)
    off = CONV_PAD - CONV_K // 2

    def chunk(c, carry):
        row = pl.multiple_of(c * CONV_CHUNK, CONV_CHUNK)
        win = pad_ref[pl.ds(row, CONV_CHUNK + 2 * CONV_PAD), :]
        acc = jnp.zeros((CONV_CHUNK, CONV_W), _F32) + dwb_ref[...]
        for t in range(CONV_K):
            acc = acc + win[off + t:off + t + CONV_CHUNK, :] * dw_ref[t:t + 1, :]
        mu = jnp.mean(acc, axis=-1, keepdims=True)
        d = acc - mu
        var = jnp.mean(d * d, axis=-1, keepdims=True)
        y = d * lax.rsqrt(var + EPS) * lg_ref[...] + lb_ref[...]
        y = (y * _sigmoid(y)).astype(_BF)
        o_ref[pl.ds(row, CONV_CHUNK), :] = jnp.dot(y, pw, preferred_element_type=_F32).astype(o_ref.dtype)
        return carry

    lax.fori_loop(0, seq // CONV_CHUNK, chunk, 0)


def _conv_module(uc, seq, nbatch, block0, dw, dwb, lg, lb, pw):
    vec = lambda a: a.reshape(1, CONV_W)
    const = lambda b: (0, 0)
    return pl.pallas_call(
        functools.partial(_conv_kernel, seq),
        grid=(nbatch,),
        in_specs=[
            pl.BlockSpec((seq, 2 * CONV_W), lambda b: (block0 + b, 0)),
            pl.BlockSpec((CONV_K, CONV_W), const),
            pl.BlockSpec((1, CONV_W), const),
            pl.BlockSpec((1, CONV_W), const),
            pl.BlockSpec((1, CONV_W), const),
            pl.BlockSpec((CONV_W, CONV_W), const),
        ],
        out_specs=pl.BlockSpec((seq, CONV_W), lambda b: (b, 0)),
        out_shape=jax.ShapeDtypeStruct((nbatch * seq, CONV_W), _BF),
        scratch_shapes=[pltpu.VMEM((seq + 2 * CONV_PAD, CONV_W), _F32)],
        compiler_params=_params(("arbitrary",)),
        name="conv_module_%d" % seq,
    )(uc, dw, vec(dwb), vec(lg), vec(lb), pw)


def _dft_constants(seq):
    j = np.arange(seq, dtype=np.int64)
    ang = 2.0 * np.pi * ((j[:, None] * j[None, :]) % seq).astype(np.float64) / seq
    cs = np.concatenate([np.cos(ang), -np.sin(ang)], axis=1) / np.sqrt(seq)
    c = np.arange(FNET_GW, dtype=np.int64)
    angc = 2.0 * np.pi * ((c[:, None] * c[None, :]) % FNET_GW).astype(np.float64) / FNET_GW
    eye = np.eye(FNET_GROUPS)
    cc = np.kron(eye, np.cos(angc)) / np.sqrt(FNET_GW)
    sc = np.kron(eye, np.sin(angc)) / np.sqrt(FNET_GW)
    w1 = np.concatenate([cc, sc], axis=1)
    return jnp.asarray(cs, dtype=_F32), jnp.asarray(w1, dtype=_F32)


def _fnet_kernel(seq, u_ref, cs_ref, w1_ref, fw_ref, o_ref, xcs_ref):
    @pl.when(pl.program_id(1) == 0)
    def _():
        t = jnp.dot(u_ref[...].astype(_BF), w1_ref[...].astype(_BF), preferred_element_type=_F32)
        xcs_ref[0:seq, :] = t[:, :FNET_W].astype(_BF)
        xcs_ref[seq:2 * seq, :] = t[:, FNET_W:].astype(_BF)

    mixed = jnp.dot(cs_ref[...].astype(_BF), xcs_ref[...], preferred_element_type=_F32)
    o_ref[...] = jnp.dot(mixed.astype(_BF), fw_ref[...].astype(_BF),
                         preferred_element_type=_F32).astype(o_ref.dtype)


def _fourier_mix(uf, seq, nbatch, block0, fw):
    tr = min(seq, 512)
    nr = seq // tr
    cs, w1 = _dft_constants(seq)
    return pl.pallas_call(
        functools.partial(_fnet_kernel, seq),
        grid=(nbatch, nr),
        in_specs=[
            pl.BlockSpec((seq, FNET_W), lambda b, r: (block0 + b, 0)),
            pl.BlockSpec((tr, 2 * seq), lambda b, r: (r, 0)),
            pl.BlockSpec((FNET_W, 2 * FNET_W), lambda b, r: (0, 0)),
            pl.BlockSpec((FNET_W, FNET_W), lambda b, r: (0, 0)),
        ],
        out_specs=pl.BlockSpec((tr, FNET_W), lambda b, r: (b * nr + r, 0)),
        out_shape=jax.ShapeDtypeStruct((nbatch * seq, FNET_W), _BF),
        scratch_shapes=[pltpu.VMEM((2 * seq, FNET_W), _BF)],
        compiler_params=_params(("arbitrary", "arbitrary")),
        name="fourier_mix_%d" % seq,
    )(uf, cs, w1, fw)


def _route_top2(r_ref, h, ei_ref, ew_ref):
    lg = lax.dot_general(r_ref[...].astype(_BF), h.astype(_BF), _NT, preferred_element_type=_F32)
    eid = lax.broadcasted_iota(jnp.int32, lg.shape, 0)
    m1 = jnp.max(lg, axis=0, keepdims=True)
    i1 = jnp.min(jnp.where(lg == m1, eid, N_EXPERTS), axis=0, keepdims=True)
    lg2 = jnp.where(eid == i1, -jnp.inf, lg)
    m2 = jnp.max(lg2, axis=0, keepdims=True)
    i2 = jnp.min(jnp.where(lg2 == m2, eid, N_EXPERTS), axis=0, keepdims=True)
    e = jnp.exp(m2 - m1)
    ei_ref[0:1, :] = i1
    ei_ref[1:2, :] = i2
    ew_ref[0:1, :] = 1.0 / (1.0 + e)
    ew_ref[1:2, :] = e / (1.0 + e)


def _outproj_kernel(split, route, *refs):
    branch_refs, refs = refs[:6], refs[6:]
    n_x = 2 if split else 1
    x_refs, refs = refs[:n_x], refs[n_x:]
    mod_ref, w_ref, gf_ref = refs[:3]
    refs = refs[3:]
    if route:
        r_ref, xo_ref, h_ref, ei_ref, ew_ref, wb_ref = refs
    else:
        xo_ref, h_ref, wb_ref = refs
    i = pl.program_id(0)

    @pl.when(i == 0)
    def _():
        wb_ref[...] = w_ref[...].astype(_BF)

    mix = jnp.concatenate([_pick_half(i, ROW_TILE, branch_refs[2 * n], branch_refs[2 * n + 1])
                           for n in range(3)], axis=-1)
    x = _pick_half(i, ROW_TILE, *x_refs) if split else x_refs[0][...]
    x_new = x + mod_ref[2:3, :] * jnp.dot(mix, wb_ref[...], preferred_element_type=_F32)
    xo_ref[...] = x_new
    h = _norm_mod(x_new, gf_ref[...], mod_ref[3:4, :], mod_ref[4:5, :])
    h_ref[...] = h.astype(h_ref.dtype)
    if route:
        _route_top2(r_ref, h, ei_ref, ew_ref)


def _outproj(branches, xs, l, mods, w_all, gf_all, h_dtype, router_t=None):
    split = len(xs) == 2
    route = router_t is not None
    row = lambda i: (i, 0)
    in_specs = []
    args = []
    for pair, width in zip(branches, (ATTN_W, CONV_W, FNET_W)):
        in_specs += _half_specs(width, ROW_TILE)
        args += list(pair)
    in_specs += _half_specs(D_MODEL, ROW_TILE) if split else [pl.BlockSpec((ROW_TILE, D_MODEL), row)]
    args += list(xs)
    mod_spec, gf_spec, w_spec = _layer_specs(l, (D_MODEL, D_MODEL))
    in_specs += [mod_spec, w_spec, gf_spec]
    args += [mods, w_all, gf_all.reshape(DEPTH, 1, D_MODEL)]
    out_specs = [pl.BlockSpec((ROW_TILE, D_MODEL), row)] * 2
    out_shape = [jax.ShapeDtypeStruct((T, D_MODEL), _F32), jax.ShapeDtypeStruct((T, D_MODEL), h_dtype)]
    if route:
        in_specs.append(pl.BlockSpec((N_EXPERTS, D_MODEL), lambda i: (0, 0)))
        args.append(router_t)
        out_specs += [pl.BlockSpec((TOP_K, ROW_TILE), lambda i: (0, i))] * 2
        out_shape += [jax.ShapeDtypeStruct((TOP_K, T), jnp.int32), jax.ShapeDtypeStruct((TOP_K, T), _F32)]
    return pl.pallas_call(
        functools.partial(_outproj_kernel, split, route),
        grid=(T // ROW_TILE,),
        in_specs=in_specs,
        out_specs=out_specs,
        out_shape=out_shape,
        scratch_shapes=[pltpu.VMEM((D_MODEL, D_MODEL), _BF)],
        compiler_params=_params(("arbitrary",)),
        name="outproj_route" if route else "outproj",
    )(*args)


def _ffn_kernel(nj, n_sub_rows, vis_e, vis_start, vis_cnt, used_sub,
                x_hbm, wg_ref, wu_ref, wd_ref, y_hbm,
                big, xb, wgb, wub, wdb, sem_in, sem_out):
    del vis_e
    v = pl.program_id(0)
    j = pl.program_id(1)
    cnt = vis_cnt[v]
    row0 = vis_start[v] * SUB

    x_is_bf16 = x_hbm.dtype == _BF
    landing = xb if x_is_bf16 else big

    def copy_in(s):
        r = pl.multiple_of(row0 + s * SUB, SUB)
        b = pl.multiple_of(s * SUB, SUB)
        return pltpu.make_async_copy(x_hbm.at[pl.ds(r, SUB)], landing.at[pl.ds(b, SUB)], sem_in.at[s])

    def copy_out(s):
        r = pl.multiple_of(row0 + s * SUB, SUB)
        b = pl.multiple_of(s * SUB, SUB)
        return pltpu.make_async_copy(big.at[pl.ds(b, SUB)], y_hbm.at[pl.ds(r, SUB)], sem_out)

    def for_subs(fn):
        lax.fori_loop(0, cnt, lambda s, c: (fn(s), c)[1], 0)

    @pl.when(jnp.logical_and(j == 0, cnt > 0))
    def _():
        for_subs(lambda s: copy_in(s).start())

        def land(s):
            copy_in(s).wait()
            rs = pl.ds(pl.multiple_of(s * SUB, SUB), SUB)
            if not x_is_bf16:
                xb[rs, :] = big[rs, :].astype(_BF)
            big[rs, :] = jnp.zeros((SUB, D_MODEL), _F32)
        for_subs(land)

    @pl.when(cnt > 0)
    def _():
        wgb[...] = wg_ref[...].astype(_BF)
        wub[...] = wu_ref[...].astype(_BF)
        wdb[...] = wd_ref[...].astype(_BF)

        def block(r0, nrows):
            rs = pl.ds(r0, nrows)
            x = xb[rs, :]
            a = jnp.dot(x, wgb[...], preferred_element_type=_F32)
            b = jnp.dot(x, wub[...], preferred_element_type=_F32)
            p = (a * _sigmoid(a) * b).astype(_BF)
            big[rs, :] += jnp.dot(p, wdb[...], preferred_element_type=_F32)

        blk = BLK_SUBS * SUB
        nfull = cnt // BLK_SUBS
        lax.fori_loop(0, nfull, lambda i, c: (block(pl.multiple_of(i * blk, blk), blk), c)[1], 0)
        rem = cnt % BLK_SUBS
        base = nfull * blk
        bit = BLK_SUBS // 2
        while bit >= 1:
            take = (rem // bit) % 2 == 1
            pl.when(take)(functools.partial(block, pl.multiple_of(base, SUB), bit * SUB))
            base = base + jnp.where(take, bit * SUB, 0)
            bit //= 2

    @pl.when(jnp.logical_and(j == nj - 1, cnt > 0))
    def _():
        for_subs(lambda s: copy_out(s).start())
        for_subs(lambda s: copy_out(s).wait())

    if x_hbm.dtype == y_hbm.dtype:
        @pl.when(jnp.logical_and(v == pl.num_programs(0) - 1, j == nj - 1))
        def _():
            def tail(s):
                r = pl.multiple_of(s * SUB, SUB)
                return pltpu.make_async_copy(x_hbm.at[pl.ds(r, SUB)], y_hbm.at[pl.ds(r, SUB)], sem_out)
            lax.fori_loop(used_sub[0], n_sub_rows, lambda s, c: (tail(s).start(), c)[1], 0)
            lax.fori_loop(used_sub[0], n_sub_rows, lambda s, c: (tail(s).wait(), c)[1], 0)


def _grouped_ffn(x, wg, wu, wd, fc, vis_e, vis_start, vis_cnt, used_sub):
    ff = wg.shape[-1]
    nj = ff // fc
    nv = vis_e.shape[0]
    n_rows = x.shape[0]

    def chunk(v, j, vc):
        return jnp.where(vc[v] > 0, j, nj - 1)

    return pl.pallas_call(
        functools.partial(_ffn_kernel, nj, n_rows // SUB),
        grid_spec=pltpu.PrefetchScalarGridSpec(
            num_scalar_prefetch=4,
            grid=(nv, nj),
            in_specs=[
                pl.BlockSpec(memory_space=pl.ANY),
                pl.BlockSpec((None, D_MODEL, fc), lambda v, j, ve, vs, vc, us: (ve[v], 0, chunk(v, j, vc))),
                pl.BlockSpec((None, D_MODEL, fc), lambda v, j, ve, vs, vc, us: (ve[v], 0, chunk(v, j, vc))),
                pl.BlockSpec((None, fc, D_MODEL), lambda v, j, ve, vs, vc, us: (ve[v], chunk(v, j, vc), 0)),
            ],
            out_specs=pl.BlockSpec(memory_space=pl.ANY),
            scratch_shapes=[
                pltpu.VMEM((TMAX, D_MODEL), _F32),
                pltpu.VMEM((TMAX, D_MODEL), _BF),
                pltpu.VMEM((D_MODEL, fc), _BF),
                pltpu.VMEM((D_MODEL, fc), _BF),
                pltpu.VMEM((fc, D_MODEL), _BF),
                pltpu.SemaphoreType.DMA((SUB_MAX,)),
                pltpu.SemaphoreType.DMA(()),
            ],
        ),
        out_shape=jax.ShapeDtypeStruct((n_rows, D_MODEL), _F32),
        compiler_params=_params(("arbitrary", "arbitrary"), FFN_VMEM_LIMIT),
        name="ffn_%d" % ff,
    )(vis_e, vis_start, vis_cnt, used_sub, x, wg, wu, wd)


def _dense_visits():
    nv = T // TMAX
    return (jnp.zeros((nv,), jnp.int32),
            jnp.arange(nv, dtype=jnp.int32) * SUB_MAX,
            jnp.full((nv,), SUB_MAX, jnp.int32),
            jnp.full((1,), T // SUB, jnp.int32))


N_SLOT_SUB = (TOP_K * T) // SUB + N_EXPERTS
N_SLOT = N_SLOT_SUB * SUB
N_VISIT = -(-N_SLOT_SUB // SUB_MAX) + N_EXPERTS


def _routing_plan(eidx):
    e_flat = eidx.reshape(-1)
    onehot = e_flat[:, None] == jnp.arange(N_EXPERTS, dtype=jnp.int32)[None, :]
    ch = 128
    oh = onehot.astype(_F32).reshape(-1, ch, N_EXPERTS)
    tri = (jnp.arange(ch)[:, None] >= jnp.arange(ch)[None, :]).astype(_F32)
    within = jnp.einsum("ij,cjk->cik", tri, oh)
    tot = within[:, -1, :]
    csum = (within + (jnp.cumsum(tot, axis=0) - tot)[:, None, :]).reshape(-1, N_EXPERTS).astype(jnp.int32)
    counts = csum[-1]
    nsub = (counts + SUB - 1) // SUB
    sub_base = jnp.cumsum(nsub) - nsub
    slot = jnp.sum(jnp.where(onehot, csum - 1 + (sub_base * SUB)[None, :], 0), axis=1).astype(jnp.int32)
    pad_start = (sub_base * SUB + counts).astype(jnp.int32)
    pad_cnt = (nsub * SUB - counts).astype(jnp.int32)
    used_sub = jnp.sum(nsub).reshape(1).astype(jnp.int32)
    nvis = (nsub + SUB_MAX - 1) // SUB_MAX
    vend = jnp.cumsum(nvis)
    total = vend[-1]
    vid = jnp.arange(N_VISIT, dtype=jnp.int32)
    ve = jnp.minimum(jnp.sum((vid[:, None] >= vend[None, :]).astype(jnp.int32), axis=1), N_EXPERTS - 1)
    local = vid - (vend - nvis)[ve]
    nv_e = jnp.maximum(nvis[ve], 1)
    q, r = nsub[ve] // nv_e, nsub[ve] % nv_e
    cnt = q + (local < r).astype(jnp.int32)
    start = sub_base[ve] + local * q + jnp.minimum(local, r)
    used = vid < total
    last_e = ve[jnp.maximum(total - 1, 0)]
    vis_e = jnp.where(used, ve, last_e).astype(jnp.int32)
    vis_cnt = jnp.where(used, cnt, 0).astype(jnp.int32)
    vis_start = jnp.where(used, start, 0).astype(jnp.int32)
    return slot, pad_start, pad_cnt, (vis_e, vis_start, vis_cnt, used_sub)


def _dispatch_kernel(slot, pad_start, pad_cnt, used_sub, h_ref, xs_hbm, sem):
    i = pl.program_id(0)

    def put(r, carry):
        for k in range(TOP_K):
            s = slot[k * T + i * ROW_TILE + r]
            pltpu.make_async_copy(h_ref.at[pl.ds(r, 1)], xs_hbm.at[pl.ds(s, 1)], sem).start()
        return carry
    lax.fori_loop(0, ROW_TILE, put, 0, unroll=4)
    for _ in range(TOP_K):
        pltpu.make_async_copy(h_ref, xs_hbm.at[pl.ds(0, ROW_TILE)], sem).wait()

    @pl.when(i == pl.num_programs(0) - 1)
    def _():
        n_pad = 0
        for e in range(N_EXPERTS):
            s0 = pad_start[e]

            def fill(r, carry, s0=s0):
                pltpu.make_async_copy(h_ref.at[pl.ds(0, 1)], xs_hbm.at[pl.ds(s0 + r, 1)], sem).start()
                return carry
            lax.fori_loop(0, pad_cnt[e], fill, 0)
            n_pad = n_pad + pad_cnt[e]

        def tail(s):
            r = pl.multiple_of(s * SUB, SUB)
            return pltpu.make_async_copy(h_ref.at[pl.ds(0, SUB)], xs_hbm.at[pl.ds(r, SUB)], sem)
        lax.fori_loop(used_sub[0], N_SLOT_SUB, lambda s, c: (tail(s).start(), c)[1], 0)
        lax.fori_loop(used_sub[0], N_SLOT_SUB, lambda s, c: (tail(s).wait(), c)[1], 0)
        row_wait = pltpu.make_async_copy(h_ref.at[pl.ds(0, 1)], xs_hbm.at[pl.ds(0, 1)], sem)
        lax.fori_loop(0, n_pad, lambda r, c: (row_wait.wait(), c)[1], 0)


def _dispatch(h, slot, pad_start, pad_cnt, used_sub):
    return pl.pallas_call(
        _dispatch_kernel,
        grid_spec=pltpu.PrefetchScalarGridSpec(
            num_scalar_prefetch=4,
            grid=(T // ROW_TILE,),
            in_specs=[pl.BlockSpec((ROW_TILE, D_MODEL), lambda i, *_: (i, 0))],
            out_specs=pl.BlockSpec(memory_space=pl.ANY),
            scratch_shapes=[pltpu.SemaphoreType.DMA(())],
        ),
        out_shape=jax.ShapeDtypeStruct((N_SLOT, D_MODEL), _F32),
        compiler_params=_params(("arbitrary",)),
        name="dispatch",
    )(slot, pad_start, pad_cnt, used_sub, h)


def _residual_out(x, gate, f, g_ref, o_refs, i, tile):
    out = x + gate * f
    if g_ref is not None:
        ms = jnp.mean(out * out, axis=-1, keepdims=True)
        out = out * lax.rsqrt(ms + EPS) * g_ref[...]
    if len(o_refs) == 1:
        o_refs[0][...] = out
    else:
        @pl.when(i < T_PROMPT // tile)
        def _():
            o_refs[0][...] = out

        @pl.when(i >= T_PROMPT // tile)
        def _():
            o_refs[1][...] = out


def _out_specs(final, tile):
    if not final:
        return ([pl.BlockSpec((tile, D_MODEL), lambda i, *_: (i, 0))],
                [jax.ShapeDtypeStruct((T, D_MODEL), _F32)])
    return (_half_specs(D_MODEL, tile),
            [jax.ShapeDtypeStruct((T_PROMPT, D_MODEL), _F32), jax.ShapeDtypeStruct((T_SAMPLE, D_MODEL), _F32)])


def _combine_kernel(final, x_ref, mod_ref, y_ref, *rest):
    g_ref, o_refs = (rest[0], rest[1:]) if final else (None, rest)
    _residual_out(x_ref[...], mod_ref[5:6, :], y_ref[...], g_ref, o_refs, pl.program_id(0), ROW_TILE)


def _combine(x, l, mods, y, g_final=None):
    final = g_final is not None
    row = lambda i: (i, 0)
    in_specs = [
        pl.BlockSpec((ROW_TILE, D_MODEL), row),
        _layer_specs(l, ())[0],
        pl.BlockSpec((ROW_TILE, D_MODEL), row),
    ]
    args = [x, mods, y]
    if final:
        in_specs.append(pl.BlockSpec((1, D_MODEL), lambda i: (0, 0)))
        args.append(g_final.reshape(1, D_MODEL))
    out_specs, out_shape = _out_specs(final, ROW_TILE)
    return pl.pallas_call(
        functools.partial(_combine_kernel, final),
        grid=(T // ROW_TILE,),
        in_specs=in_specs,
        out_specs=out_specs,
        out_shape=out_shape,
        compiler_params=_params(("arbitrary",)),
        name="combine%s" % ("_final" if final else ""),
    )(*args)


def _combine_top2_kernel(final, n_out, slot, x_ref, mod_ref, w_ref, *rest):
    g_ref, rest = (rest[0], rest[1:]) if final else (None, rest)
    ys_hbm, o_refs, (ybuf, sems) = rest[0], rest[1:1 + n_out], rest[1 + n_out:]
    i = pl.program_id(0)

    def fetch(tile, b):
        def one(r, carry):
            for k in range(TOP_K):
                s = slot[k * T + tile * COMB_TILE + r]
                pltpu.make_async_copy(ys_hbm.at[pl.ds(s, 1)], ybuf.at[b, k, pl.ds(r, 1)], sems.at[b]).start()
            return carry
        lax.fori_loop(0, COMB_TILE, one, 0, unroll=4)

    @pl.when(i == 0)
    def _():
        fetch(0, 0)

    @pl.when(i + 1 < pl.num_programs(0))
    def _():
        fetch(i + 1, (i + 1) % 2)

    b = i % 2
    for k in range(TOP_K):
        pltpu.make_async_copy(ys_hbm.at[pl.ds(0, COMB_TILE)], ybuf.at[b, k], sems.at[b]).wait()
    w = w_ref[...]
    f = w[:, 0:1] * ybuf[b, 0] + w[:, 1:2] * ybuf[b, 1]
    _residual_out(x_ref[...], mod_ref[5:6, :], f, g_ref, o_refs, i, COMB_TILE)


def _combine_top2(x, l, mods, ys, slot, w, g_final=None):
    final = g_final is not None
    row = lambda i, s: (i, 0)
    in_specs = [
        pl.BlockSpec((COMB_TILE, D_MODEL), row),
        pl.BlockSpec((None, None, 6, D_MODEL), lambda i, s: (l, _cond_of_tile(i, COMB_TILE), 0, 0)),
        pl.BlockSpec((COMB_TILE, TOP_K), row),
    ]
    args = [x, mods, w]
    if final:
        in_specs.append(pl.BlockSpec((1, D_MODEL), lambda i, s: (0, 0)))
        args.append(g_final.reshape(1, D_MODEL))
    in_specs.append(pl.BlockSpec(memory_space=pl.ANY))
    args.append(ys)
    out_specs, out_shape = _out_specs(final, COMB_TILE)
    return pl.pallas_call(
        functools.partial(_combine_top2_kernel, final, len(out_specs)),
        grid_spec=pltpu.PrefetchScalarGridSpec(
            num_scalar_prefetch=1,
            grid=(T // COMB_TILE,),
            in_specs=in_specs,
            out_specs=out_specs,
            scratch_shapes=[
                pltpu.VMEM((2, TOP_K, COMB_TILE, D_MODEL), _F32),
                pltpu.SemaphoreType.DMA((2,)),
            ],
        ),
        out_shape=out_shape,
        compiler_params=_params(("arbitrary",)),
        name="combine_top2%s" % ("_final" if final else ""),
    )(slot, *args)


def kernel(x_prompt, x_sample, cache_k, cache_v, c, c_ctx, w_ada, b_ada, g_norm_mix, g_norm_ffn,
           w_in, w_out, attn_sink, conv_dw, conv_dw_b, conv_ln_g, conv_ln_b, conv_pw, fnet_w,
           ffn_w_gate, ffn_w_up, ffn_w_down, moe_router, moe_w_gate, moe_w_up, moe_w_down, g_final):
    xs = (x_prompt.reshape(T_PROMPT, D_MODEL), x_sample.reshape(T_SAMPLE, D_MODEL))
    cond8 = jnp.concatenate([c_ctx[None, :], c, jnp.zeros((N_COND - 1 - DEC_BATCH, D_MODEL), _F32)], axis=0)
    mods = _modulation(cond8, w_ada, b_ada).reshape(DEPTH, N_COND, 6, D_MODEL)
    cos_t, sin_t = _rope_tables()
    ck_all = cache_k.reshape(DEC_BATCH, DEPTH, PAST_LEN, KV_W)
    cv_all = cache_v.reshape(DEC_BATCH, DEPTH, PAST_LEN, KV_W)
    p_blocks = T_PROMPT // DEC_SEQ

    ks, vs = [], []
    for l in range(DEPTH):
        q, k, v, uc, uf = _inproj(xs, l, mods, g_norm_mix, w_in, cos_t, sin_t)
        ks.append(k[:T_PROMPT])
        vs.append(v[:T_PROMPT])
        attn = (_ctx_attention(attn_sink[l], q, k, v),
                _lat_attention(attn_sink[l], q, k, v, ck_all[:, l], cv_all[:, l]))
        cargs = (conv_dw[l], conv_dw_b[l], conv_ln_g[l], conv_ln_b[l], conv_pw[l])
        conv = (_conv_module(uc, SEQ, BATCH, 0, *cargs),
                _conv_module(uc, DEC_SEQ, DEC_BATCH, p_blocks, *cargs))
        four = (_fourier_mix(uf, SEQ, BATCH, 0, fnet_w[l]),
                _fourier_mix(uf, DEC_SEQ, DEC_BATCH, p_blocks, fnet_w[l]))
        last = g_final if l == DEPTH - 1 else None
        i = l // 2
        if l % 2 == 0:
            x, h = _outproj((attn, conv, four), xs, l, mods, w_out, g_norm_ffn, _BF)
            y = _grouped_ffn(h, ffn_w_gate[i:i + 1], ffn_w_up[i:i + 1], ffn_w_down[i:i + 1], 256,
                             *_dense_visits())
            xs = tuple(_combine(x, l, mods, y, g_final=last))
        else:
            x, h, eidx, ew = _outproj((attn, conv, four), xs, l, mods, w_out, g_norm_ffn, _F32,
                                      router_t=moe_router[i].T)
            slot, pad_start, pad_cnt, visits = _routing_plan(eidx)
            xd = _dispatch(h, slot, pad_start, pad_cnt, visits[-1])
            ys = _grouped_ffn(xd, moe_w_gate[i], moe_w_up[i], moe_w_down[i], 512, *visits)
            xs = tuple(_combine_top2(x, l, mods, ys, slot, ew.T, g_final=last))

    y_prompt = xs[0].reshape(BATCH, SEQ, D_MODEL)
    y_sample = xs[1].reshape(DEC_BATCH, DEC_SEQ, D_MODEL)
    state_k = jnp.stack([a.reshape(BATCH, SEQ, N_KV_HEADS, HEAD_DIM) for a in ks], axis=1)
    state_v = jnp.stack([a.reshape(BATCH, SEQ, N_KV_HEADS, HEAD_DIM) for a in vs], axis=1)
    return (y_prompt, y_sample, state_k, state_v)
```

```python
import functools

import numpy as np
import jax
import jax.numpy as jnp
from jax import lax
from jax.experimental import pallas as pl
from jax.experimental.pallas import tpu as pltpu

D_MODEL = 1024
BATCH = 16
SEQ = 256
DEPTH = 2
DEC_BATCH = 2
DEC_SEQ = 2048
PAST_LEN = 512
GRID_W = 64
HEAD_DIM = 64
N_HEADS = 8
N_KV_HEADS = 2
GQA_GROUP = N_HEADS // N_KV_HEADS
ATTN_W = N_HEADS * HEAD_DIM
KV_W = N_KV_HEADS * HEAD_DIM
WINDOW = 128
BLOCK = 128
ROPE_THETA = 10000.0
CONV_W = D_MODEL // 4
CONV_K = 31
FNET_GROUPS = 4
FNET_W = D_MODEL // 4
FNET_GW = FNET_W // FNET_GROUPS
IN_WIDTH = ATTN_W + 2 * KV_W + 2 * CONV_W + FNET_W
D_FF = 2816
N_EXPERTS = 8
TOP_K = 2
D_FF_EXPERT = 3584
EPS = 1e-6
NEG = -1e30

T_PROMPT = BATCH * SEQ
T_SAMPLE = DEC_BATCH * DEC_SEQ
T = T_PROMPT + T_SAMPLE
N_COND = 8

ROW_TILE = 512
SUB = 256
SUB_MAX = 16
TMAX = SUB * SUB_MAX
BLK_SUBS = 4
COMB_TILE = 512
CONV_CHUNK = 64
CONV_PAD = 16
CONV_SPAN = CONV_CHUNK + 8 * ((CONV_PAD + CONV_K // 2) // 8)
VMEM_LIMIT = 48 * 1024 * 1024
FFN_VMEM_LIMIT = 56 * 1024 * 1024

_BF = jnp.bfloat16
_F32 = jnp.float32


def _cond_of_tile(i, tile):
    r = i * tile
    return jnp.where(r < T_PROMPT, 0, 1 + (r - T_PROMPT) // DEC_SEQ)


def _params(sem, vmem=VMEM_LIMIT):
    return pltpu.CompilerParams(dimension_semantics=sem, vmem_limit_bytes=vmem)


def _sigmoid(x):
    return 1.0 / (1.0 + jnp.exp(-x))


def _mod_kernel(cond_ref, w_ref, b_ref, o_ref):
    cnd = cond_ref[...]
    s = (cnd * _sigmoid(cnd)).astype(_BF)
    o_ref[...] = jnp.dot(s, w_ref[...].astype(_BF), preferred_element_type=_F32) + b_ref[...]


def _modulation(cond8, w_ada, b_ada):
    nt = 1536
    return pl.pallas_call(
        _mod_kernel,
        grid=(DEPTH, 6 * D_MODEL // nt),
        in_specs=[
            pl.BlockSpec((N_COND, D_MODEL), lambda l, n: (0, 0)),
            pl.BlockSpec((None, D_MODEL, nt), lambda l, n: (l, 0, n)),
            pl.BlockSpec((None, 1, nt), lambda l, n: (l, 0, n)),
        ],
        out_specs=pl.BlockSpec((None, N_COND, nt), lambda l, n: (l, 0, n)),
        out_shape=jax.ShapeDtypeStruct((DEPTH, N_COND, 6 * D_MODEL), _F32),
        compiler_params=_params(("arbitrary", "arbitrary")),
        name="modulation",
    )(cond8, w_ada, b_ada.reshape(DEPTH, 1, 6 * D_MODEL))


def _norm_mod(x, g, shift, scale):
    ms = jnp.mean(x * x, axis=-1, keepdims=True)
    y = x * lax.rsqrt(ms + EPS) * g
    return y * (1.0 + scale) + shift


def _rope_tables():
    rows = DEC_SEQ // GRID_W
    n_freq = HEAD_DIM // 4
    inv = ROPE_THETA ** (-jnp.arange(n_freq, dtype=_F32) / n_freq)
    gr, gc = jnp.meshgrid(jnp.arange(rows, dtype=_F32), jnp.arange(GRID_W, dtype=_F32), indexing="ij")
    ang_r = gr.reshape(-1)[:, None] * inv
    ang_c = gc.reshape(-1)[:, None] * inv
    cr, sr, cc, sc = jnp.cos(ang_r), jnp.sin(ang_r), jnp.cos(ang_c), jnp.sin(ang_c)
    cos64 = jnp.concatenate([cr, cr, cc, cc], axis=-1)
    sin64 = jnp.concatenate([-sr, sr, -sc, sc], axis=-1)
    return jnp.tile(cos64, (1, 2)), jnp.tile(sin64, (1, 2))


def _rope128(x, cos, sin):
    lane = lax.broadcasted_iota(jnp.int32, x.shape, 1)
    first = (lane % 32) < 16
    partner = jnp.where(first, pltpu.roll(x, 128 - 16, 1), pltpu.roll(x, 16, 1))
    return x * cos + partner * sin


def _half_specs(width, tile):
    pt = T_PROMPT // tile
    return [pl.BlockSpec((tile, width), lambda i, *_: (jnp.minimum(i, pt - 1), 0)),
            pl.BlockSpec((tile, width), lambda i, *_: (jnp.maximum(i - pt, 0), 0))]


def _pick_half(i, tile, a_ref, b_ref):
    return jnp.where(i < T_PROMPT // tile, a_ref[...], b_ref[...])


def _layer_specs(l, w_shape):
    return [
        pl.BlockSpec((None, None, 6, D_MODEL), lambda i, *_: (l, _cond_of_tile(i, ROW_TILE), 0, 0)),
        pl.BlockSpec((None, 1, D_MODEL), lambda i, *_: (l, 0, 0)),
        pl.BlockSpec((None,) + w_shape, lambda i, *_: (l, 0, 0)),
    ]


def _inproj_kernel(split, *refs):
    n_x = 2 if split else 1
    x_refs, refs = refs[:n_x], refs[n_x:]
    mod_ref, g_ref, w_ref, cos_ref, sin_ref, q_ref, k_ref, v_ref, uc_ref, uf_ref, wb_ref = refs
    i = pl.program_id(0)

    @pl.when(i == 0)
    def _():
        wb_ref[...] = w_ref[...].astype(_BF)

    x = _pick_half(i, ROW_TILE, *x_refs) if split else x_refs[0][...]
    h = _norm_mod(x, g_ref[...], mod_ref[0:1, :], mod_ref[1:2, :]).astype(_BF)
    latent = i >= T_PROMPT // ROW_TILE
    cos = jnp.where(latent, cos_ref[...], 1.0)
    sin = jnp.where(latent, sin_ref[...], 0.0)

    def proj(o, n):
        return jnp.dot(h, wb_ref[:, o:o + n], preferred_element_type=_F32)

    o = 0
    for c in range(ATTN_W // 256):
        qq = proj(o, 256)
        q_ref[:, o:o + 128] = _rope128(qq[:, :128], cos, sin)
        q_ref[:, o + 128:o + 256] = _rope128(qq[:, 128:], cos, sin)
        o += 256
    kv = proj(o, 2 * KV_W)
    k_ref[...] = _rope128(kv[:, :KV_W], cos, sin)
    v_ref[...] = kv[:, KV_W:]
    o += 2 * KV_W
    uc_ref[...] = proj(o, 2 * CONV_W)
    o += 2 * CONV_W
    uf_ref[...] = proj(o, FNET_W)


def _inproj(xs, l, mods, g_all, w_all, cos_t, sin_t):
    split = len(xs) == 2
    pt = T_PROMPT // ROW_TILE
    per_seq = DEC_SEQ // ROW_TILE

    def rope_idx(i):
        return (jnp.maximum(i - pt, 0) % per_seq, 0)

    row = lambda i: (i, 0)
    x_specs = _half_specs(D_MODEL, ROW_TILE) if split else [pl.BlockSpec((ROW_TILE, D_MODEL), row)]
    outs = [ATTN_W, KV_W, KV_W, 2 * CONV_W, FNET_W]
    return pl.pallas_call(
        functools.partial(_inproj_kernel, split),
        grid=(T // ROW_TILE,),
        in_specs=x_specs + _layer_specs(l, (D_MODEL, IN_WIDTH)) + [
            pl.BlockSpec((ROW_TILE, 128), rope_idx),
            pl.BlockSpec((ROW_TILE, 128), rope_idx),
        ],
        out_specs=[pl.BlockSpec((ROW_TILE, n), row) for n in outs],
        out_shape=[jax.ShapeDtypeStruct((T, n), _F32) for n in outs],
        scratch_shapes=[pltpu.VMEM((D_MODEL, IN_WIDTH), _BF)],
        compiler_params=_params(("arbitrary",)),
        name="inproj",
    )(*xs, mods, g_all.reshape(DEPTH, 1, D_MODEL), w_all, cos_t, sin_t)


def _stack_groups(q, kh, rows):
    parts = [q[:, (kh * GQA_GROUP + g) * HEAD_DIM:(kh * GQA_GROUP + g + 1) * HEAD_DIM] for g in range(GQA_GROUP)]
    return jnp.concatenate(parts, axis=0)


def _sink_column(sink_ref, kh, rows):
    r = lax.broadcasted_iota(jnp.int32, (GQA_GROUP * rows, 1), 0)
    col = jnp.full((GQA_GROUP * rows, 1), sink_ref[kh * GQA_GROUP], _F32)
    for g in range(1, GQA_GROUP):
        col = jnp.where(r >= g * rows, sink_ref[kh * GQA_GROUP + g], col)
    return col


def _unstack_store(o_ref, o, kh, rows):
    for pair in range(GQA_GROUP // 2):
        a = o[(2 * pair) * rows:(2 * pair + 1) * rows]
        b = o[(2 * pair + 1) * rows:(2 * pair + 2) * rows]
        c0 = (kh * GQA_GROUP + 2 * pair) * HEAD_DIM
        o_ref[:, c0:c0 + 2 * HEAD_DIM] = jnp.concatenate([a, b], axis=-1).astype(o_ref.dtype)


_NT = (((1,), (1,)), ((), ()))


def _ctx_attn_kernel(sink_ref, q_ref, k_ref, v_ref, o_ref):
    scale = HEAD_DIM ** -0.5
    q = q_ref[...] * scale
    for kh in range(N_KV_HEADS):
        kk = k_ref[:, kh * HEAD_DIM:(kh + 1) * HEAD_DIM].astype(_BF)
        vv = v_ref[:, kh * HEAD_DIM:(kh + 1) * HEAD_DIM].astype(_BF)
        qs = _stack_groups(q, kh, SEQ).astype(_BF)
        s = lax.dot_general(qs, kk, _NT, preferred_element_type=_F32)
        sink = _sink_column(sink_ref, kh, SEQ)
        m = jnp.maximum(jnp.max(s, axis=-1, keepdims=True), sink)
        p = jnp.exp(s - m)
        den = jnp.sum(p, axis=-1, keepdims=True) + jnp.exp(sink - m)
        o = jnp.dot(p.astype(_BF), vv, preferred_element_type=_F32) / den
        _unstack_store(o_ref, o, kh, SEQ)


def _ctx_attention(sink_l, q, k, v):
    row = lambda b, s: (b, 0)
    return pl.pallas_call(
        _ctx_attn_kernel,
        grid_spec=pltpu.PrefetchScalarGridSpec(
            num_scalar_prefetch=1,
            grid=(BATCH,),
            in_specs=[
                pl.BlockSpec((SEQ, ATTN_W), row),
                pl.BlockSpec((SEQ, KV_W), row),
                pl.BlockSpec((SEQ, KV_W), row),
            ],
            out_specs=pl.BlockSpec((SEQ, ATTN_W), row),
        ),
        out_shape=jax.ShapeDtypeStruct((T_PROMPT, ATTN_W), _BF),
        compiler_params=_params(("arbitrary",)),
        name="ctx_attention",
    )(sink_l, q, k, v)


def _lat_attn_kernel(sink_ref, q_ref, k_ref, v_ref, ck_ref, cv_ref, o_ref):
    n = pl.program_id(1)
    scale = HEAD_DIM ** -0.5
    nb = DEC_SEQ // BLOCK
    band = 3 * BLOCK
    start = pl.multiple_of(jnp.clip(n - 1, 0, nb - 3) * BLOCK, BLOCK)
    q = q_ref[...] * scale
    kb = k_ref[pl.ds(start, band), :]
    vb = v_ref[pl.ds(start, band), :]
    rows = GQA_GROUP * BLOCK
    qpos = n * BLOCK + lax.broadcasted_iota(jnp.int32, (rows, band), 0) % BLOCK
    kpos = start + lax.broadcasted_iota(jnp.int32, (rows, band), 1)
    valid = jnp.abs(kpos - qpos) <= WINDOW
    for kh in range(N_KV_HEADS):
        hs = slice(kh * HEAD_DIM, (kh + 1) * HEAD_DIM)
        qs = _stack_groups(q, kh, BLOCK).astype(_BF)
        s_loc = lax.dot_general(qs, kb[:, hs].astype(_BF), _NT, preferred_element_type=_F32)
        s_loc = jnp.where(valid, s_loc, NEG)
        s_ctx = lax.dot_general(qs, ck_ref[:, hs].astype(_BF), _NT, preferred_element_type=_F32)
        sink = _sink_column(sink_ref, kh, BLOCK)
        m = jnp.maximum(jnp.maximum(jnp.max(s_loc, axis=-1, keepdims=True),
                                    jnp.max(s_ctx, axis=-1, keepdims=True)), sink)
        p_loc = jnp.exp(s_loc - m)
        p_ctx = jnp.exp(s_ctx - m)
        den = (jnp.sum(p_loc, axis=-1, keepdims=True) + jnp.sum(p_ctx, axis=-1, keepdims=True)
               + jnp.exp(sink - m))
        o = (jnp.dot(p_loc.astype(_BF), vb[:, hs].astype(_BF), preferred_element_type=_F32)
             + jnp.dot(p_ctx.astype(_BF), cv_ref[:, hs].astype(_BF), preferred_element_type=_F32)) / den
        _unstack_store(o_ref, o, kh, BLOCK)


def _lat_attention(sink_l, q, k, v, ck, cv):
    nb = DEC_SEQ // BLOCK
    q0 = T_PROMPT // BLOCK
    s0 = T_PROMPT // DEC_SEQ
    return pl.pallas_call(
        _lat_attn_kernel,
        grid_spec=pltpu.PrefetchScalarGridSpec(
            num_scalar_prefetch=1,
            grid=(DEC_BATCH, nb),
            in_specs=[
                pl.BlockSpec((BLOCK, ATTN_W), lambda b, n, s: (q0 + b * nb + n, 0)),
                pl.BlockSpec((DEC_SEQ, KV_W), lambda b, n, s: (s0 + b, 0)),
                pl.BlockSpec((DEC_SEQ, KV_W), lambda b, n, s: (s0 + b, 0)),
                pl.BlockSpec((None, PAST_LEN, KV_W), lambda b, n, s: (b, 0, 0)),
                pl.BlockSpec((None, PAST_LEN, KV_W), lambda b, n, s: (b, 0, 0)),
            ],
            out_specs=pl.BlockSpec((BLOCK, ATTN_W), lambda b, n, s: (b * nb + n, 0)),
        ),
        out_shape=jax.ShapeDtypeStruct((T_SAMPLE, ATTN_W), _BF),
        compiler_params=_params(("arbitrary", "arbitrary")),
        name="lat_attention",
    )(sink_l, q, k, v, ck, cv)


def _conv_kernel(seq, u_ref, dw_ref, dwb_ref, lg_ref, lb_ref, pw_ref, o_ref, pad_ref, y_ref, sh_ref):
    u = u_ref[...]
    pad_ref[0:CONV_PAD, :] = jnp.zeros((CONV_PAD, CONV_W), _F32)
    pad_ref[CONV_PAD + seq:2 * CONV_PAD + seq, :] = jnp.zeros((CONV_PAD, CONV_W), _F32)
    pad_ref[CONV_PAD:CONV_PAD + seq, :] = u[:, :CONV_W] * _sigmoid(u[:, CONV_W:])
    off = CONV_PAD - CONV_K // 2
    span = CONV_SPAN

    def chunk(c, carry):
        row = pl.multiple_of(c * CONV_CHUNK, CONV_CHUNK)
        win = pad_ref[pl.ds(row, CONV_CHUNK + 2 * CONV_PAD), :]
        acc = jnp.zeros((CONV_CHUNK, CONV_W), _F32) + dwb_ref[...]
        for phase in range(8):
            sh_ref[phase] = win[phase:phase + span, :]
        for phase in range(8):
            for a in range(span // 8):
                t = 8 * a + phase - off
                if 0 <= t < CONV_K:
                    acc = acc + sh_ref[phase, 8 * a:8 * a + CONV_CHUNK, :] * dw_ref[t:t + 1, :]
        mu = jnp.mean(acc, axis=-1, keepdims=True)
        d = acc - mu
        var = jnp.mean(d * d, axis=-1, keepdims=True)
        y = d * lax.rsqrt(var + EPS) * lg_ref[...] + lb_ref[...]
        y_ref[pl.ds(row, CONV_CHUNK), :] = (y * _sigmoid(y)).astype(_BF)
        return carry

    lax.fori_loop(0, seq // CONV_CHUNK, chunk, 0)
    o_ref[...] = jnp.dot(y_ref[...], pw_ref[...].astype(_BF), preferred_element_type=_F32).astype(o_ref.dtype)


def _conv_module(uc, seq, nbatch, block0, dw, dwb, lg, lb, pw):
    vec = lambda a: a.reshape(1, CONV_W)
    const = lambda b: (0, 0)
    return pl.pallas_call(
        functools.partial(_conv_kernel, seq),
        grid=(nbatch,),
        in_specs=[
            pl.BlockSpec((seq, 2 * CONV_W), lambda b: (block0 + b, 0)),
            pl.BlockSpec((CONV_K, CONV_W), const),
            pl.BlockSpec((1, CONV_W), const),
            pl.BlockSpec((1, CONV_W), const),
            pl.BlockSpec((1, CONV_W), const),
            pl.BlockSpec((CONV_W, CONV_W), const),
        ],
        out_specs=pl.BlockSpec((seq, CONV_W), lambda b: (b, 0)),
        out_shape=jax.ShapeDtypeStruct((nbatch * seq, CONV_W), _BF),
        scratch_shapes=[pltpu.VMEM((seq + 2 * CONV_PAD, CONV_W), _F32), pltpu.VMEM((seq, CONV_W), _BF),
                        pltpu.VMEM((8, CONV_SPAN, CONV_W), _F32)],
        compiler_params=_params(("arbitrary",)),
        name="conv_module_%d" % seq,
    )(uc, dw, vec(dwb), vec(lg), vec(lb), pw)


def _dft_constants(seq):
    j = np.arange(seq, dtype=np.int64)
    ang = 2.0 * np.pi * ((j[:, None] * j[None, :]) % seq).astype(np.float64) / seq
    cs = np.concatenate([np.cos(ang), -np.sin(ang)], axis=1) / np.sqrt(seq)
    c = np.arange(FNET_GW, dtype=np.int64)
    angc = 2.0 * np.pi * ((c[:, None] * c[None, :]) % FNET_GW).astype(np.float64) / FNET_GW
    eye = np.eye(FNET_GROUPS)
    cc = np.kron(eye, np.cos(angc)) / np.sqrt(FNET_GW)
    sc = np.kron(eye, np.sin(angc)) / np.sqrt(FNET_GW)
    w1 = np.concatenate([cc, sc], axis=1)
    return jnp.asarray(cs, dtype=_F32), jnp.asarray(w1, dtype=_F32)


def _fnet_kernel(seq, u_ref, cs_ref, w1_ref, fw_ref, o_ref, xcs_ref):
    @pl.when(pl.program_id(1) == 0)
    def _():
        t = jnp.dot(u_ref[...].astype(_BF), w1_ref[...].astype(_BF), preferred_element_type=_F32)
        xcs_ref[0:seq, :] = t[:, :FNET_W].astype(_BF)
        xcs_ref[seq:2 * seq, :] = t[:, FNET_W:].astype(_BF)

    mixed = jnp.dot(cs_ref[...].astype(_BF), xcs_ref[...], preferred_element_type=_F32)
    o_ref[...] = jnp.dot(mixed.astype(_BF), fw_ref[...].astype(_BF),
                         preferred_element_type=_F32).astype(o_ref.dtype)


def _fourier_mix(uf, seq, nbatch, block0, fw):
    tr = min(seq, 512)
    nr = seq // tr
    cs, w1 = _dft_constants(seq)
    return pl.pallas_call(
        functools.partial(_fnet_kernel, seq),
        grid=(nbatch, nr),
        in_specs=[
            pl.BlockSpec((seq, FNET_W), lambda b, r: (block0 + b, 0)),
            pl.BlockSpec((tr, 2 * seq), lambda b, r: (r, 0)),
            pl.BlockSpec((FNET_W, 2 * FNET_W), lambda b, r: (0, 0)),
            pl.BlockSpec((FNET_W, FNET_W), lambda b, r: (0, 0)),
        ],
        out_specs=pl.BlockSpec((tr, FNET_W), lambda b, r: (b * nr + r, 0)),
        out_shape=jax.ShapeDtypeStruct((nbatch * seq, FNET_W), _BF),
        scratch_shapes=[pltpu.VMEM((2 * seq, FNET_W), _BF)],
        compiler_params=_params(("arbitrary", "arbitrary")),
        name="fourier_mix_%d" % seq,
    )(uf, cs, w1, fw)


def _route_top2(r_ref, h, ei_ref, ew_ref):
    lg = lax.dot_general(r_ref[...].astype(_BF), h.astype(_BF), _NT, preferred_element_type=_F32)
    eid = lax.broadcasted_iota(jnp.int32, lg.shape, 0)
    m1 = jnp.max(lg, axis=0, keepdims=True)
    i1 = jnp.min(jnp.where(lg == m1, eid, N_EXPERTS), axis=0, keepdims=True)
    lg2 = jnp.where(eid == i1, -jnp.inf, lg)
    m2 = jnp.max(lg2, axis=0, keepdims=True)
    i2 = jnp.min(jnp.where(lg2 == m2, eid, N_EXPERTS), axis=0, keepdims=True)
    e = jnp.exp(m2 - m1)
    ei_ref[0:1, :] = i1
    ei_ref[1:2, :] = i2
    ew_ref[0:1, :] = 1.0 / (1.0 + e)
    ew_ref[1:2, :] = e / (1.0 + e)


def _outproj_kernel(split, route, *refs):
    branch_refs, refs = refs[:6], refs[6:]
    n_x = 2 if split else 1
    x_refs, refs = refs[:n_x], refs[n_x:]
    mod_ref, w_ref, gf_ref = refs[:3]
    refs = refs[3:]
    if route:
        r_ref, xo_ref, h_ref, ei_ref, ew_ref, wb_ref = refs
    else:
        xo_ref, h_ref, wb_ref = refs
    i = pl.program_id(0)

    @pl.when(i == 0)
    def _():
        wb_ref[...] = w_ref[...].astype(_BF)

    mix = jnp.concatenate([_pick_half(i, ROW_TILE, branch_refs[2 * n], branch_refs[2 * n + 1])
                           for n in range(3)], axis=-1)
    x = _pick_half(i, ROW_TILE, *x_refs) if split else x_refs[0][...]
    x_new = x + mod_ref[2:3, :] * jnp.dot(mix, wb_ref[...], preferred_element_type=_F32)
    xo_ref[...] = x_new
    h = _norm_mod(x_new, gf_ref[...], mod_ref[3:4, :], mod_ref[4:5, :])
    h_ref[...] = h.astype(h_ref.dtype)
    if route:
        _route_top2(r_ref, h, ei_ref, ew_ref)


def _outproj(branches, xs, l, mods, w_all, gf_all, h_dtype, router_t=None):
    split = len(xs) == 2
    route = router_t is not None
    row = lambda i: (i, 0)
    in_specs = []
    args = []
    for pair, width in zip(branches, (ATTN_W, CONV_W, FNET_W)):
        in_specs += _half_specs(width, ROW_TILE)
        args += list(pair)
    in_specs += _half_specs(D_MODEL, ROW_TILE) if split else [pl.BlockSpec((ROW_TILE, D_MODEL), row)]
    args += list(xs)
    mod_spec, gf_spec, w_spec = _layer_specs(l, (D_MODEL, D_MODEL))
    in_specs += [mod_spec, w_spec, gf_spec]
    args += [mods, w_all, gf_all.reshape(DEPTH, 1, D_MODEL)]
    out_specs = [pl.BlockSpec((ROW_TILE, D_MODEL), row)] * 2
    out_shape = [jax.ShapeDtypeStruct((T, D_MODEL), _F32), jax.ShapeDtypeStruct((T, D_MODEL), h_dtype)]
    if route:
        in_specs.append(pl.BlockSpec((N_EXPERTS, D_MODEL), lambda i: (0, 0)))
        args.append(router_t)
        out_specs += [pl.BlockSpec((TOP_K, ROW_TILE), lambda i: (0, i))] * 2
        out_shape += [jax.ShapeDtypeStruct((TOP_K, T), jnp.int32), jax.ShapeDtypeStruct((TOP_K, T), _F32)]
    return pl.pallas_call(
        functools.partial(_outproj_kernel, split, route),
        grid=(T // ROW_TILE,),
        in_specs=in_specs,
        out_specs=out_specs,
        out_shape=out_shape,
        scratch_shapes=[pltpu.VMEM((D_MODEL, D_MODEL), _BF)],
        compiler_params=_params(("arbitrary",)),
        name="outproj_route" if route else "outproj",
    )(*args)


def _ffn_kernel(nj, n_sub_rows, vis_e, vis_start, vis_cnt, used_sub,
                x_hbm, wg_ref, wu_ref, wd_ref, y_hbm,
                big, xb, wgb, wub, wdb, sem_in, sem_out):
    del vis_e
    v = pl.program_id(0)
    j = pl.program_id(1)
    cnt = vis_cnt[v]
    row0 = vis_start[v] * SUB

    x_is_bf16 = x_hbm.dtype == _BF
    landing = xb if x_is_bf16 else big

    def copy_in(s):
        r = pl.multiple_of(row0 + s * SUB, SUB)
        b = pl.multiple_of(s * SUB, SUB)
        return pltpu.make_async_copy(x_hbm.at[pl.ds(r, SUB)], landing.at[pl.ds(b, SUB)], sem_in.at[s])

    def copy_out(s):
        r = pl.multiple_of(row0 + s * SUB, SUB)
        b = pl.multiple_of(s * SUB, SUB)
        return pltpu.make_async_copy(big.at[pl.ds(b, SUB)], y_hbm.at[pl.ds(r, SUB)], sem_out)

    def for_subs(fn):
        lax.fori_loop(0, cnt, lambda s, c: (fn(s), c)[1], 0)

    @pl.when(jnp.logical_and(j == 0, cnt > 0))
    def _():
        for_subs(lambda s: copy_in(s).start())

    @pl.when(cnt > 0)
    def _():
        wgb[...] = wg_ref[...].astype(_BF)
        wub[...] = wu_ref[...].astype(_BF)
        wdb[...] = wd_ref[...].astype(_BF)

        def block(r0, nrows):
            s0 = r0 // SUB

            @pl.when(j == 0)
            def _():
                for t in range(nrows // SUB):
                    copy_in(s0 + t).wait()
                    rs = pl.ds(pl.multiple_of(r0 + t * SUB, SUB), SUB)
                    if not x_is_bf16:
                        xb[rs, :] = big[rs, :].astype(_BF)
                    big[rs, :] = jnp.zeros((SUB, D_MODEL), _F32)

            rs = pl.ds(r0, nrows)
            x = xb[rs, :]
            a = jnp.dot(x, wgb[...], preferred_element_type=_F32)
            b = jnp.dot(x, wub[...], preferred_element_type=_F32)
            p = (a * _sigmoid(a) * b).astype(_BF)
            big[rs, :] += jnp.dot(p, wdb[...], preferred_element_type=_F32)

            @pl.when(j == nj - 1)
            def _():
                for t in range(nrows // SUB):
                    copy_out(s0 + t).start()

        blk = BLK_SUBS * SUB
        nfull = cnt // BLK_SUBS
        lax.fori_loop(0, nfull, lambda i, c: (block(pl.multiple_of(i * blk, blk), blk), c)[1], 0)
        rem = cnt % BLK_SUBS
        base = nfull * blk
        bit = BLK_SUBS // 2
        while bit >= 1:
            take = (rem // bit) % 2 == 1
            pl.when(take)(functools.partial(block, pl.multiple_of(base, SUB), bit * SUB))
            base = base + jnp.where(take, bit * SUB, 0)
            bit //= 2

    @pl.when(jnp.logical_and(j == nj - 1, cnt > 0))
    def _():
        for_subs(lambda s: copy_out(s).wait())

    if x_hbm.dtype == y_hbm.dtype:
        @pl.when(jnp.logical_and(v == pl.num_programs(0) - 1, j == nj - 1))
        def _():
            def tail(s):
                r = pl.multiple_of(s * SUB, SUB)
                return pltpu.make_async_copy(x_hbm.at[pl.ds(r, SUB)], y_hbm.at[pl.ds(r, SUB)], sem_out)
            lax.fori_loop(used_sub[0], n_sub_rows, lambda s, c: (tail(s).start(), c)[1], 0)
            lax.fori_loop(used_sub[0], n_sub_rows, lambda s, c: (tail(s).wait(), c)[1], 0)


def _grouped_ffn(x, wg, wu, wd, fc, vis_e, vis_start, vis_cnt, used_sub):
    ff = wg.shape[-1]
    nj = ff // fc
    nv = vis_e.shape[0]
    n_rows = x.shape[0]

    def chunk(v, j, vc):
        return jnp.where(vc[v] > 0, j, nj - 1)

    return pl.pallas_call(
        functools.partial(_ffn_kernel, nj, n_rows // SUB),
        grid_spec=pltpu.PrefetchScalarGridSpec(
            num_scalar_prefetch=4,
            grid=(nv, nj),
            in_specs=[
                pl.BlockSpec(memory_space=pl.ANY),
                pl.BlockSpec((None, D_MODEL, fc), lambda v, j, ve, vs, vc, us: (ve[v], 0, chunk(v, j, vc))),
                pl.BlockSpec((None, D_MODEL, fc), lambda v, j, ve, vs, vc, us: (ve[v], 0, chunk(v, j, vc))),
                pl.BlockSpec((None, fc, D_MODEL), lambda v, j, ve, vs, vc, us: (ve[v], chunk(v, j, vc), 0)),
            ],
            out_specs=pl.BlockSpec(memory_space=pl.ANY),
            scratch_shapes=[
                pltpu.VMEM((TMAX, D_MODEL), _F32),
                pltpu.VMEM((TMAX, D_MODEL), _BF),
                pltpu.VMEM((D_MODEL, fc), _BF),
                pltpu.VMEM((D_MODEL, fc), _BF),
                pltpu.VMEM((fc, D_MODEL), _BF),
                pltpu.SemaphoreType.DMA((SUB_MAX,)),
                pltpu.SemaphoreType.DMA(()),
            ],
        ),
        out_shape=jax.ShapeDtypeStruct((n_rows, D_MODEL), _F32),
        compiler_params=_params(("arbitrary", "arbitrary"), FFN_VMEM_LIMIT),
        name="ffn_%d" % ff,
    )(vis_e, vis_start, vis_cnt, used_sub, x, wg, wu, wd)


def _dense_visits():
    nv = T // TMAX
    return (jnp.zeros((nv,), jnp.int32),
            jnp.arange(nv, dtype=jnp.int32) * SUB_MAX,
            jnp.full((nv,), SUB_MAX, jnp.int32),
            jnp.full((1,), T // SUB, jnp.int32))


N_SLOT_SUB = (TOP_K * T) // SUB + N_EXPERTS
N_SLOT = N_SLOT_SUB * SUB
N_VISIT = -(-N_SLOT_SUB // SUB_MAX) + N_EXPERTS


def _routing_plan(eidx):
    e_flat = eidx.reshape(-1)
    onehot = e_flat[:, None] == jnp.arange(N_EXPERTS, dtype=jnp.int32)[None, :]
    ch = 128
    oh = onehot.astype(_F32).reshape(-1, ch, N_EXPERTS)
    tri = (jnp.arange(ch)[:, None] >= jnp.arange(ch)[None, :]).astype(_F32)
    within = jnp.einsum("ij,cjk->cik", tri, oh)
    tot = within[:, -1, :]
    csum = (within + (jnp.cumsum(tot, axis=0) - tot)[:, None, :]).reshape(-1, N_EXPERTS).astype(jnp.int32)
    counts = csum[-1]
    nsub = (counts + SUB - 1) // SUB
    sub_base = jnp.cumsum(nsub) - nsub
    slot = jnp.sum(jnp.where(onehot, csum - 1 + (sub_base * SUB)[None, :], 0), axis=1).astype(jnp.int32)
    pad_start = (sub_base * SUB + counts).astype(jnp.int32)
    pad_cnt = (nsub * SUB - counts).astype(jnp.int32)
    used_sub = jnp.sum(nsub).reshape(1).astype(jnp.int32)
    nvis = (nsub + SUB_MAX - 1) // SUB_MAX
    vend = jnp.cumsum(nvis)
    total = vend[-1]
    vid = jnp.arange(N_VISIT, dtype=jnp.int32)
    ve = jnp.minimum(jnp.sum((vid[:, None] >= vend[None, :]).astype(jnp.int32), axis=1), N_EXPERTS - 1)
    local = vid - (vend - nvis)[ve]
    nv_e = jnp.maximum(nvis[ve], 1)
    q, r = nsub[ve] // nv_e, nsub[ve] % nv_e
    cnt = q + (local < r).astype(jnp.int32)
    start = sub_base[ve] + local * q + jnp.minimum(local, r)
    used = vid < total
    last_e = ve[jnp.maximum(total - 1, 0)]
    vis_e = jnp.where(used, ve, last_e).astype(jnp.int32)
    vis_cnt = jnp.where(used, cnt, 0).astype(jnp.int32)
    vis_start = jnp.where(used, start, 0).astype(jnp.int32)
    return slot, pad_start, pad_cnt, (vis_e, vis_start, vis_cnt, used_sub)


def _dispatch_kernel(slot, pad_start, pad_cnt, used_sub, h_ref, xs_hbm, ring, tile_sems, sem):
    i = pl.program_id(0)
    last = pl.num_programs(0) - 1
    b = i % 2

    def drain(slot_id):
        for _ in range(TOP_K):
            pltpu.make_async_copy(ring.at[slot_id], xs_hbm.at[pl.ds(0, ROW_TILE)], tile_sems.at[slot_id]).wait()

    pl.when(i >= 2)(lambda: drain(b))
    ring[b] = h_ref[...]

    def put(r, carry):
        for k in range(TOP_K):
            s = slot[k * T + i * ROW_TILE + r]
            pltpu.make_async_copy(ring.at[b, pl.ds(r, 1)], xs_hbm.at[pl.ds(s, 1)], tile_sems.at[b]).start()
        return carry
    lax.fori_loop(0, ROW_TILE, put, 0, unroll=4)

    @pl.when(i == last)
    def _():
        drain(1 - b)
        drain(b)

    @pl.when(i == last)
    def _():
        n_pad = 0
        for e in range(N_EXPERTS):
            s0 = pad_start[e]

            def fill(r, carry, s0=s0):
                pltpu.make_async_copy(h_ref.at[pl.ds(0, 1)], xs_hbm.at[pl.ds(s0 + r, 1)], sem).start()
                return carry
            lax.fori_loop(0, pad_cnt[e], fill, 0)
            n_pad = n_pad + pad_cnt[e]

        def tail(s):
            r = pl.multiple_of(s * SUB, SUB)
            return pltpu.make_async_copy(h_ref.at[pl.ds(0, SUB)], xs_hbm.at[pl.ds(r, SUB)], sem)
        lax.fori_loop(used_sub[0], N_SLOT_SUB, lambda s, c: (tail(s).start(), c)[1], 0)
        lax.fori_loop(used_sub[0], N_SLOT_SUB, lambda s, c: (tail(s).wait(), c)[1], 0)
        row_wait = pltpu.make_async_copy(h_ref.at[pl.ds(0, 1)], xs_hbm.at[pl.ds(0, 1)], sem)
        lax.fori_loop(0, n_pad, lambda r, c: (row_wait.wait(), c)[1], 0)


def _dispatch(h, slot, pad_start, pad_cnt, used_sub):
    return pl.pallas_call(
        _dispatch_kernel,
        grid_spec=pltpu.PrefetchScalarGridSpec(
            num_scalar_prefetch=4,
            grid=(T // ROW_TILE,),
            in_specs=[pl.BlockSpec((ROW_TILE, D_MODEL), lambda i, *_: (i, 0))],
            out_specs=pl.BlockSpec(memory_space=pl.ANY),
            scratch_shapes=[pltpu.VMEM((2, ROW_TILE, D_MODEL), _F32),
                            pltpu.SemaphoreType.DMA((2,)), pltpu.SemaphoreType.DMA(())],
        ),
        out_shape=jax.ShapeDtypeStruct((N_SLOT, D_MODEL), _F32),
        compiler_params=_params(("arbitrary",)),
        name="dispatch",
    )(slot, pad_start, pad_cnt, used_sub, h)


def _residual_out(x, gate, f, g_ref, o_refs, i, tile):
    out = x + gate * f
    if g_ref is not None:
        ms = jnp.mean(out * out, axis=-1, keepdims=True)
        out = out * lax.rsqrt(ms + EPS) * g_ref[...]
    if len(o_refs) == 1:
        o_refs[0][...] = out
    else:
        @pl.when(i < T_PROMPT // tile)
        def _():
            o_refs[0][...] = out

        @pl.when(i >= T_PROMPT // tile)
        def _():
            o_refs[1][...] = out


def _out_specs(final, tile):
    if not final:
        return ([pl.BlockSpec((tile, D_MODEL), lambda i, *_: (i, 0))],
                [jax.ShapeDtypeStruct((T, D_MODEL), _F32)])
    return (_half_specs(D_MODEL, tile),
            [jax.ShapeDtypeStruct((T_PROMPT, D_MODEL), _F32), jax.ShapeDtypeStruct((T_SAMPLE, D_MODEL), _F32)])


def _combine_kernel(final, x_ref, mod_ref, y_ref, *rest):
    g_ref, o_refs = (rest[0], rest[1:]) if final else (None, rest)
    _residual_out(x_ref[...], mod_ref[5:6, :], y_ref[...], g_ref, o_refs, pl.program_id(0), ROW_TILE)


def _combine(x, l, mods, y, g_final=None):
    final = g_final is not None
    row = lambda i: (i, 0)
    in_specs = [
        pl.BlockSpec((ROW_TILE, D_MODEL), row),
        _layer_specs(l, ())[0],
        pl.BlockSpec((ROW_TILE, D_MODEL), row),
    ]
    args = [x, mods, y]
    if final:
        in_specs.append(pl.BlockSpec((1, D_MODEL), lambda i: (0, 0)))
        args.append(g_final.reshape(1, D_MODEL))
    out_specs, out_shape = _out_specs(final, ROW_TILE)
    return pl.pallas_call(
        functools.partial(_combine_kernel, final),
        grid=(T // ROW_TILE,),
        in_specs=in_specs,
        out_specs=out_specs,
        out_shape=out_shape,
        compiler_params=_params(("arbitrary",)),
        name="combine%s" % ("_final" if final else ""),
    )(*args)


def _combine_top2_kernel(final, n_out, slot, x_ref, mod_ref, w_ref, *rest):
    g_ref, rest = (rest[0], rest[1:]) if final else (None, rest)
    ys_hbm, o_refs, (ybuf, sems) = rest[0], rest[1:1 + n_out], rest[1 + n_out:]
    i = pl.program_id(0)

    def fetch(tile, b):
        def one(r, carry):
            for k in range(TOP_K):
                s = slot[k * T + tile * COMB_TILE + r]
                pltpu.make_async_copy(ys_hbm.at[pl.ds(s, 1)], ybuf.at[b, k, pl.ds(r, 1)], sems.at[b]).start()
            return carry
        lax.fori_loop(0, COMB_TILE, one, 0, unroll=4)

    @pl.when(i == 0)
    def _():
        fetch(0, 0)

    @pl.when(i + 1 < pl.num_programs(0))
    def _():
        fetch(i + 1, (i + 1) % 2)

    b = i % 2
    for k in range(TOP_K):
        pltpu.make_async_copy(ys_hbm.at[pl.ds(0, COMB_TILE)], ybuf.at[b, k], sems.at[b]).wait()
    w = w_ref[...]
    f = w[:, 0:1] * ybuf[b, 0] + w[:, 1:2] * ybuf[b, 1]
    _residual_out(x_ref[...], mod_ref[5:6, :], f, g_ref, o_refs, i, COMB_TILE)


def _combine_top2(x, l, mods, ys, slot, w, g_final=None):
    final = g_final is not None
    row = lambda i, s: (i, 0)
    in_specs = [
        pl.BlockSpec((COMB_TILE, D_MODEL), row),
        pl.BlockSpec((None, None, 6, D_MODEL), lambda i, s: (l, _cond_of_tile(i, COMB_TILE), 0, 0)),
        pl.BlockSpec((COMB_TILE, TOP_K), row),
    ]
    args = [x, mods, w]
    if final:
        in_specs.append(pl.BlockSpec((1, D_MODEL), lambda i, s: (0, 0)))
        args.append(g_final.reshape(1, D_MODEL))
    in_specs.append(pl.BlockSpec(memory_space=pl.ANY))
    args.append(ys)
    out_specs, out_shape = _out_specs(final, COMB_TILE)
    return pl.pallas_call(
        functools.partial(_combine_top2_kernel, final, len(out_specs)),
        grid_spec=pltpu.PrefetchScalarGridSpec(
            num_scalar_prefetch=1,
            grid=(T // COMB_TILE,),
            in_specs=in_specs,
            out_specs=out_specs,
            scratch_shapes=[
                pltpu.VMEM((2, TOP_K, COMB_TILE, D_MODEL), _F32),
                pltpu.SemaphoreType.DMA((2,)),
            ],
        ),
        out_shape=out_shape,
        compiler_params=_params(("arbitrary",)),
        name="combine_top2%s" % ("_final" if final else ""),
    )(slot, *args)


def kernel(x_prompt, x_sample, cache_k, cache_v, c, c_ctx, w_ada, b_ada, g_norm_mix, g_norm_ffn,
           w_in, w_out, attn_sink, conv_dw, conv_dw_b, conv_ln_g, conv_ln_b, conv_pw, fnet_w,
           ffn_w_gate, ffn_w_up, ffn_w_down, moe_router, moe_w_gate, moe_w_up, moe_w_down, g_final):
    xs = (x_prompt.reshape(T_PROMPT, D_MODEL), x_sample.reshape(T_SAMPLE, D_MODEL))
    cond8 = jnp.concatenate([c_ctx[None, :], c, jnp.zeros((N_COND - 1 - DEC_BATCH, D_MODEL), _F32)], axis=0)
    mods = _modulation(cond8, w_ada, b_ada).reshape(DEPTH, N_COND, 6, D_MODEL)
    cos_t, sin_t = _rope_tables()
    ck_all = cache_k.reshape(DEC_BATCH, DEPTH, PAST_LEN, KV_W)
    cv_all = cache_v.reshape(DEC_BATCH, DEPTH, PAST_LEN, KV_W)
    p_blocks = T_PROMPT // DEC_SEQ

    ks, vs = [], []
    for l in range(DEPTH):
        q, k, v, uc, uf = _inproj(xs, l, mods, g_norm_mix, w_in, cos_t, sin_t)
        ks.append(k[:T_PROMPT])
        vs.append(v[:T_PROMPT])
        attn = (_ctx_attention(attn_sink[l], q, k, v),
                _lat_attention(attn_sink[l], q, k, v, ck_all[:, l], cv_all[:, l]))
        cargs = (conv_dw[l], conv_dw_b[l], conv_ln_g[l], conv_ln_b[l], conv_pw[l])
        conv = (_conv_module(uc, SEQ, BATCH, 0, *cargs),
                _conv_module(uc, DEC_SEQ, DEC_BATCH, p_blocks, *cargs))
        four = (_fourier_mix(uf, SEQ, BATCH, 0, fnet_w[l]),
                _fourier_mix(uf, DEC_SEQ, DEC_BATCH, p_blocks, fnet_w[l]))
        last = g_final if l == DEPTH - 1 else None
        i = l // 2
        if l % 2 == 0:
            x, h = _outproj((attn, conv, four), xs, l, mods, w_out, g_norm_ffn, _BF)
            y = _grouped_ffn(h, ffn_w_gate[i:i + 1], ffn_w_up[i:i + 1], ffn_w_down[i:i + 1], 256,
                             *_dense_visits())
            xs = tuple(_combine(x, l, mods, y, g_final=last))
        else:
            x, h, eidx, ew = _outproj((attn, conv, four), xs, l, mods, w_out, g_norm_ffn, _F32,
                                      router_t=moe_router[i].T)
            slot, pad_start, pad_cnt, visits = _routing_plan(eidx)
            xd = _dispatch(h, slot, pad_start, pad_cnt, visits[-1])
            ys = _grouped_ffn(xd, moe_w_gate[i], moe_w_up[i], moe_w_down[i], 512, *visits)
            xs = tuple(_combine_top2(x, l, mods, ys, slot, ew.T, g_final=last))

    y_prompt = xs[0].reshape(BATCH, SEQ, D_MODEL)
    y_sample = xs[1].reshape(DEC_BATCH, DEC_SEQ, D_MODEL)
    state_k = jnp.stack([a.reshape(BATCH, SEQ, N_KV_HEADS, HEAD_DIM) for a in ks], axis=1)
    state_v = jnp.stack([a.reshape(BATCH, SEQ, N_KV_HEADS, HEAD_DIM) for a in vs], axis=1)
    return (y_prompt, y_sample, state_k, state_v)
```

```python
import functools

import numpy as np
import jax
import jax.numpy as jnp
from jax import lax
from jax.experimental import pallas as pl
from jax.experimental.pallas import tpu as pltpu

D_MODEL = 1024
BATCH = 16
SEQ = 256
DEPTH = 2
DEC_BATCH = 2
DEC_SEQ = 2048
PAST_LEN = 512
GRID_W = 64
HEAD_DIM = 64
N_HEADS = 8
N_KV_HEADS = 2
GQA_GROUP = N_HEADS // N_KV_HEADS
ATTN_W = N_HEADS * HEAD_DIM
KV_W = N_KV_HEADS * HEAD_DIM
WINDOW = 128
BLOCK = 128
ROPE_THETA = 10000.0
CONV_W = D_MODEL // 4
CONV_K = 31
FNET_GROUPS = 4
FNET_W = D_MODEL // 4
FNET_GW = FNET_W // FNET_GROUPS
IN_WIDTH = ATTN_W + 2 * KV_W + 2 * CONV_W + FNET_W
D_FF = 2816
N_EXPERTS = 8
TOP_K = 2
D_FF_EXPERT = 3584
EPS = 1e-6
NEG = -1e30

T_PROMPT = BATCH * SEQ
T_SAMPLE = DEC_BATCH * DEC_SEQ
T = T_PROMPT + T_SAMPLE
N_COND = 8

ROW_TILE = 512
SUB = 256
SUB_MAX = 16
TMAX = SUB * SUB_MAX
BLK_SUBS = 4
COMB_TILE = 512
CONV_CHUNK = 64
CONV_PAD = 16
CONV_SPAN = CONV_CHUNK + 8 * ((CONV_PAD + CONV_K // 2) // 8)
VMEM_LIMIT = 48 * 1024 * 1024
FFN_VMEM_LIMIT = 56 * 1024 * 1024

_BF = jnp.bfloat16
_F32 = jnp.float32


def _cond_of_tile(i, tile):
    r = i * tile
    return jnp.where(r < T_PROMPT, 0, 1 + (r - T_PROMPT) // DEC_SEQ)


def _params(sem, vmem=VMEM_LIMIT):
    return pltpu.CompilerParams(dimension_semantics=sem, vmem_limit_bytes=vmem)


def _sigmoid(x):
    return 1.0 / (1.0 + jnp.exp(-x))


def _mod_kernel(cond_ref, w_ref, b_ref, o_ref):
    cnd = cond_ref[...]
    s = (cnd * _sigmoid(cnd)).astype(_BF)
    o_ref[...] = jnp.dot(s, w_ref[...].astype(_BF), preferred_element_type=_F32) + b_ref[...]


def _modulation(cond8, w_ada, b_ada):
    nt = 1536
    return pl.pallas_call(
        _mod_kernel,
        grid=(DEPTH, 6 * D_MODEL // nt),
        in_specs=[
            pl.BlockSpec((N_COND, D_MODEL), lambda l, n: (0, 0)),
            pl.BlockSpec((None, D_MODEL, nt), lambda l, n: (l, 0, n)),
            pl.BlockSpec((None, 1, nt), lambda l, n: (l, 0, n)),
        ],
        out_specs=pl.BlockSpec((None, N_COND, nt), lambda l, n: (l, 0, n)),
        out_shape=jax.ShapeDtypeStruct((DEPTH, N_COND, 6 * D_MODEL), _F32),
        compiler_params=_params(("arbitrary", "arbitrary")),
        name="modulation",
    )(cond8, w_ada, b_ada.reshape(DEPTH, 1, 6 * D_MODEL))


def _norm_mod(x, g, shift, scale):
    ms = jnp.mean(x * x, axis=-1, keepdims=True)
    y = x * lax.rsqrt(ms + EPS) * g
    return y * (1.0 + scale) + shift


def _rope_tables():
    rows = DEC_SEQ // GRID_W
    n_freq = HEAD_DIM // 4
    inv = ROPE_THETA ** (-jnp.arange(n_freq, dtype=_F32) / n_freq)
    gr, gc = jnp.meshgrid(jnp.arange(rows, dtype=_F32), jnp.arange(GRID_W, dtype=_F32), indexing="ij")
    ang_r = gr.reshape(-1)[:, None] * inv
    ang_c = gc.reshape(-1)[:, None] * inv
    cr, sr, cc, sc = jnp.cos(ang_r), jnp.sin(ang_r), jnp.cos(ang_c), jnp.sin(ang_c)
    cos64 = jnp.concatenate([cr, cr, cc, cc], axis=-1)
    sin64 = jnp.concatenate([-sr, sr, -sc, sc], axis=-1)
    return jnp.tile(cos64, (1, 2)), jnp.tile(sin64, (1, 2))


def _rope128(x, cos, sin):
    lane = lax.broadcasted_iota(jnp.int32, x.shape, 1)
    first = (lane % 32) < 16
    partner = jnp.where(first, pltpu.roll(x, 128 - 16, 1), pltpu.roll(x, 16, 1))
    return x * cos + partner * sin


def _half_specs(width, tile):
    pt = T_PROMPT // tile
    return [pl.BlockSpec((tile, width), lambda i, *_: (jnp.minimum(i, pt - 1), 0)),
            pl.BlockSpec((tile, width), lambda i, *_: (jnp.maximum(i - pt, 0), 0))]


def _pick_half(i, tile, a_ref, b_ref):
    return jnp.where(i < T_PROMPT // tile, a_ref[...], b_ref[...])


def _layer_specs(l, w_shape):
    return [
        pl.BlockSpec((None, None, 6, D_MODEL), lambda i, *_: (l, _cond_of_tile(i, ROW_TILE), 0, 0)),
        pl.BlockSpec((None, 1, D_MODEL), lambda i, *_: (l, 0, 0)),
        pl.BlockSpec((None,) + w_shape, lambda i, *_: (l, 0, 0)),
    ]


def _inproj_kernel(split, resid, *refs):
    n_x = 2 if split else 1
    x_refs, refs = refs[:n_x], refs[n_x:]
    if resid:
        y_ref, modp_ref = refs[:2]
        refs = refs[2:]
    mod_ref, g_ref, w_ref, cos_ref, sin_ref, q_ref, k_ref, v_ref, uc_ref, uf_ref = refs[:10]
    wb_ref = refs[-1]
    i = pl.program_id(0)

    @pl.when(i == 0)
    def _():
        wb_ref[...] = w_ref[...].astype(_BF)

    x = _pick_half(i, ROW_TILE, *x_refs) if split else x_refs[0][...]
    if resid:
        x = x + modp_ref[5:6, :] * y_ref[...]
        refs[10][...] = x
    h = _norm_mod(x, g_ref[...], mod_ref[0:1, :], mod_ref[1:2, :]).astype(_BF)
    latent = i >= T_PROMPT // ROW_TILE
    cos = jnp.where(latent, cos_ref[...], 1.0)
    sin = jnp.where(latent, sin_ref[...], 0.0)

    def proj(o, n):
        return jnp.dot(h, wb_ref[:, o:o + n], preferred_element_type=_F32)

    o = 0
    for c in range(ATTN_W // 256):
        qq = proj(o, 256)
        q_ref[:, o:o + 128] = _rope128(qq[:, :128], cos, sin)
        q_ref[:, o + 128:o + 256] = _rope128(qq[:, 128:], cos, sin)
        o += 256
    kv = proj(o, 2 * KV_W)
    k_ref[...] = _rope128(kv[:, :KV_W], cos, sin)
    v_ref[...] = kv[:, KV_W:]
    o += 2 * KV_W
    uc_ref[...] = proj(o, 2 * CONV_W)
    o += 2 * CONV_W
    uf_ref[...] = proj(o, FNET_W)


def _inproj(xs, l, mods, g_all, w_all, cos_t, sin_t, resid=None):
    split = len(xs) == 2
    pt = T_PROMPT // ROW_TILE
    per_seq = DEC_SEQ // ROW_TILE

    def rope_idx(i):
        return (jnp.maximum(i - pt, 0) % per_seq, 0)

    row = lambda i: (i, 0)
    in_specs = _half_specs(D_MODEL, ROW_TILE) if split else [pl.BlockSpec((ROW_TILE, D_MODEL), row)]
    args = list(xs)
    outs = [ATTN_W, KV_W, KV_W, 2 * CONV_W, FNET_W]
    if resid is not None:
        y, l_prev = resid
        in_specs += [pl.BlockSpec((ROW_TILE, D_MODEL), row), _layer_specs(l_prev, ())[0]]
        args += [y, mods]
        outs.append(D_MODEL)
    return pl.pallas_call(
        functools.partial(_inproj_kernel, split, resid is not None),
        grid=(T // ROW_TILE,),
        in_specs=in_specs + _layer_specs(l, (D_MODEL, IN_WIDTH)) + [
            pl.BlockSpec((ROW_TILE, 128), rope_idx),
            pl.BlockSpec((ROW_TILE, 128), rope_idx),
        ],
        out_specs=[pl.BlockSpec((ROW_TILE, n), row) for n in outs],
        out_shape=[jax.ShapeDtypeStruct((T, n), _F32) for n in outs],
        scratch_shapes=[pltpu.VMEM((D_MODEL, IN_WIDTH), _BF)],
        compiler_params=_params(("arbitrary",)),
        name="inproj_resid" if resid is not None else "inproj",
    )(*args, mods, g_all.reshape(DEPTH, 1, D_MODEL), w_all, cos_t, sin_t)


def _stack_groups(q, kh, rows):
    parts = [q[:, (kh * GQA_GROUP + g) * HEAD_DIM:(kh * GQA_GROUP + g + 1) * HEAD_DIM] for g in range(GQA_GROUP)]
    return jnp.concatenate(parts, axis=0)


def _sink_column(sink_ref, head0, rows):
    r = lax.broadcasted_iota(jnp.int32, (GQA_GROUP * rows, 1), 0)
    col = jnp.full((GQA_GROUP * rows, 1), sink_ref[head0], _F32)
    for g in range(1, GQA_GROUP):
        col = jnp.where(r >= g * rows, sink_ref[head0 + g], col)
    return col


def _unstack_store(o_ref, o, kh, rows):
    for pair in range(GQA_GROUP // 2):
        a = o[(2 * pair) * rows:(2 * pair + 1) * rows]
        b = o[(2 * pair + 1) * rows:(2 * pair + 2) * rows]
        c0 = (kh * GQA_GROUP + 2 * pair) * HEAD_DIM
        o_ref[:, c0:c0 + 2 * HEAD_DIM] = jnp.concatenate([a, b], axis=-1).astype(o_ref.dtype)


_NT = (((1,), (1,)), ((), ()))


def _ctx_attn_kernel(l, sink_ref, q_ref, k_ref, v_ref, o_ref):
    scale = HEAD_DIM ** -0.5
    q = q_ref[...] * scale
    for kh in range(N_KV_HEADS):
        kk = k_ref[:, kh * HEAD_DIM:(kh + 1) * HEAD_DIM].astype(_BF)
        vv = v_ref[:, kh * HEAD_DIM:(kh + 1) * HEAD_DIM].astype(_BF)
        qs = _stack_groups(q, kh, SEQ).astype(_BF)
        s = lax.dot_general(qs, kk, _NT, preferred_element_type=_F32)
        sink = _sink_column(sink_ref, l * N_HEADS + kh * GQA_GROUP, SEQ)
        m = jnp.maximum(jnp.max(s, axis=-1, keepdims=True), sink)
        p = jnp.exp(s - m)
        den = jnp.sum(p, axis=-1, keepdims=True) + jnp.exp(sink - m)
        o = jnp.dot(p.astype(_BF), vv, preferred_element_type=_F32) / den
        _unstack_store(o_ref, o, kh, SEQ)


def _ctx_attention(l, sinks, q, k, v):
    row = lambda b, s: (b, 0)
    return pl.pallas_call(
        functools.partial(_ctx_attn_kernel, l),
        grid_spec=pltpu.PrefetchScalarGridSpec(
            num_scalar_prefetch=1,
            grid=(BATCH,),
            in_specs=[
                pl.BlockSpec((SEQ, ATTN_W), row),
                pl.BlockSpec((SEQ, KV_W), row),
                pl.BlockSpec((SEQ, KV_W), row),
            ],
            out_specs=pl.BlockSpec((SEQ, ATTN_W), row),
        ),
        out_shape=jax.ShapeDtypeStruct((T_PROMPT, ATTN_W), _BF),
        compiler_params=_params(("arbitrary",)),
        name="ctx_attention",
    )(sinks, q, k, v)


def _lat_attn_kernel(l, sink_ref, q_ref, k_ref, v_ref, ck_ref, cv_ref, o_ref):
    n = pl.program_id(1)
    scale = HEAD_DIM ** -0.5
    nb = DEC_SEQ // BLOCK
    band = 3 * BLOCK
    start = pl.multiple_of(jnp.clip(n - 1, 0, nb - 3) * BLOCK, BLOCK)
    q = q_ref[...] * scale
    kb = k_ref[pl.ds(start, band), :]
    vb = v_ref[pl.ds(start, band), :]
    rows = GQA_GROUP * BLOCK
    qpos = n * BLOCK + lax.broadcasted_iota(jnp.int32, (rows, band), 0) % BLOCK
    kpos = start + lax.broadcasted_iota(jnp.int32, (rows, band), 1)
    valid = jnp.abs(kpos - qpos) <= WINDOW
    for kh in range(N_KV_HEADS):
        hs = slice(kh * HEAD_DIM, (kh + 1) * HEAD_DIM)
        qs = _stack_groups(q, kh, BLOCK).astype(_BF)
        s_loc = lax.dot_general(qs, kb[:, hs].astype(_BF), _NT, preferred_element_type=_F32)
        s_loc = jnp.where(valid, s_loc, NEG)
        s_ctx = lax.dot_general(qs, ck_ref[:, hs].astype(_BF), _NT, preferred_element_type=_F32)
        sink = _sink_column(sink_ref, l * N_HEADS + kh * GQA_GROUP, BLOCK)
        m = jnp.maximum(jnp.maximum(jnp.max(s_loc, axis=-1, keepdims=True),
                                    jnp.max(s_ctx, axis=-1, keepdims=True)), sink)
        p_loc = jnp.exp(s_loc - m)
        p_ctx = jnp.exp(s_ctx - m)
        den = (jnp.sum(p_loc, axis=-1, keepdims=True) + jnp.sum(p_ctx, axis=-1, keepdims=True)
               + jnp.exp(sink - m))
        o = (jnp.dot(p_loc.astype(_BF), vb[:, hs].astype(_BF), preferred_element_type=_F32)
             + jnp.dot(p_ctx.astype(_BF), cv_ref[:, hs].astype(_BF), preferred_element_type=_F32)) / den
        _unstack_store(o_ref, o, kh, BLOCK)


def _lat_attention(l, sinks, q, k, v, ck, cv):
    nb = DEC_SEQ // BLOCK
    q0 = T_PROMPT // BLOCK
    s0 = T_PROMPT // DEC_SEQ
    return pl.pallas_call(
        functools.partial(_lat_attn_kernel, l),
        grid_spec=pltpu.PrefetchScalarGridSpec(
            num_scalar_prefetch=1,
            grid=(DEC_BATCH, nb),
            in_specs=[
                pl.BlockSpec((BLOCK, ATTN_W), lambda b, n, s: (q0 + b * nb + n, 0)),
                pl.BlockSpec((DEC_SEQ, KV_W), lambda b, n, s: (s0 + b, 0)),
                pl.BlockSpec((DEC_SEQ, KV_W), lambda b, n, s: (s0 + b, 0)),
                pl.BlockSpec((None, None, PAST_LEN, KV_W), lambda b, n, s: (b, l, 0, 0)),
                pl.BlockSpec((None, None, PAST_LEN, KV_W), lambda b, n, s: (b, l, 0, 0)),
            ],
            out_specs=pl.BlockSpec((BLOCK, ATTN_W), lambda b, n, s: (b * nb + n, 0)),
        ),
        out_shape=jax.ShapeDtypeStruct((T_SAMPLE, ATTN_W), _BF),
        compiler_params=_params(("arbitrary", "arbitrary")),
        name="lat_attention",
    )(sinks, q, k, v, ck, cv)


def _conv_kernel(seq, u_ref, dw_ref, dwb_ref, lg_ref, lb_ref, pw_ref, o_ref, pad_ref, y_ref, sh_ref):
    u = u_ref[...]
    pad_ref[0:CONV_PAD, :] = jnp.zeros((CONV_PAD, CONV_W), _F32)
    pad_ref[CONV_PAD + seq:2 * CONV_PAD + seq, :] = jnp.zeros((CONV_PAD, CONV_W), _F32)
    pad_ref[CONV_PAD:CONV_PAD + seq, :] = u[:, :CONV_W] * _sigmoid(u[:, CONV_W:])
    off = CONV_PAD - CONV_K // 2
    span = CONV_SPAN

    def chunk(c, carry):
        row = pl.multiple_of(c * CONV_CHUNK, CONV_CHUNK)
        win = pad_ref[pl.ds(row, CONV_CHUNK + 2 * CONV_PAD), :]
        acc = jnp.zeros((CONV_CHUNK, CONV_W), _F32) + dwb_ref[...]
        for phase in range(8):
            sh_ref[phase] = win[phase:phase + span, :]
        for phase in range(8):
            for a in range(span // 8):
                t = 8 * a + phase - off
                if 0 <= t < CONV_K:
                    acc = acc + sh_ref[phase, 8 * a:8 * a + CONV_CHUNK, :] * dw_ref[t:t + 1, :]
        mu = jnp.mean(acc, axis=-1, keepdims=True)
        d = acc - mu
        var = jnp.mean(d * d, axis=-1, keepdims=True)
        y = d * lax.rsqrt(var + EPS) * lg_ref[...] + lb_ref[...]
        y_ref[pl.ds(row, CONV_CHUNK), :] = (y * _sigmoid(y)).astype(_BF)
        return carry

    lax.fori_loop(0, seq // CONV_CHUNK, chunk, 0)
    o_ref[...] = jnp.dot(y_ref[...], pw_ref[...].astype(_BF), preferred_element_type=_F32).astype(o_ref.dtype)


def _conv_module(uc, seq, nbatch, block0, l, dw, dwb, lg, lb, pw):
    vec = lambda a: a.reshape(DEPTH, 1, CONV_W)
    layer = lambda b: (l, 0, 0)
    return pl.pallas_call(
        functools.partial(_conv_kernel, seq),
        grid=(nbatch,),
        in_specs=[
            pl.BlockSpec((seq, 2 * CONV_W), lambda b: (block0 + b, 0)),
            pl.BlockSpec((None, CONV_K, CONV_W), layer),
            pl.BlockSpec((None, 1, CONV_W), layer),
            pl.BlockSpec((None, 1, CONV_W), layer),
            pl.BlockSpec((None, 1, CONV_W), layer),
            pl.BlockSpec((None, CONV_W, CONV_W), layer),
        ],
        out_specs=pl.BlockSpec((seq, CONV_W), lambda b: (b, 0)),
        out_shape=jax.ShapeDtypeStruct((nbatch * seq, CONV_W), _BF),
        scratch_shapes=[pltpu.VMEM((seq + 2 * CONV_PAD, CONV_W), _F32), pltpu.VMEM((seq, CONV_W), _BF),
                        pltpu.VMEM((8, CONV_SPAN, CONV_W), _F32)],
        compiler_params=_params(("arbitrary",)),
        name="conv_module_%d" % seq,
    )(uc, dw, vec(dwb), vec(lg), vec(lb), pw)


def _dft_constants(seq):
    j = np.arange(seq, dtype=np.int64)
    ang = 2.0 * np.pi * ((j[:, None] * j[None, :]) % seq).astype(np.float64) / seq
    cs = np.concatenate([np.cos(ang), -np.sin(ang)], axis=1) / np.sqrt(seq)
    c = np.arange(FNET_GW, dtype=np.int64)
    angc = 2.0 * np.pi * ((c[:, None] * c[None, :]) % FNET_GW).astype(np.float64) / FNET_GW
    eye = np.eye(FNET_GROUPS)
    cc = np.kron(eye, np.cos(angc)) / np.sqrt(FNET_GW)
    sc = np.kron(eye, np.sin(angc)) / np.sqrt(FNET_GW)
    w1 = np.concatenate([cc, sc], axis=1)
    return jnp.asarray(cs, dtype=_F32), jnp.asarray(w1, dtype=_F32)


def _fnet_kernel(seq, u_ref, cs_ref, w1_ref, fw_ref, o_ref, xcs_ref):
    @pl.when(pl.program_id(1) == 0)
    def _():
        t = jnp.dot(u_ref[...].astype(_BF), w1_ref[...].astype(_BF), preferred_element_type=_F32)
        xcs_ref[0:seq, :] = t[:, :FNET_W].astype(_BF)
        xcs_ref[seq:2 * seq, :] = t[:, FNET_W:].astype(_BF)

    mixed = jnp.dot(cs_ref[...].astype(_BF), xcs_ref[...], preferred_element_type=_F32)
    o_ref[...] = jnp.dot(mixed.astype(_BF), fw_ref[...].astype(_BF),
                         preferred_element_type=_F32).astype(o_ref.dtype)


def _fourier_mix(uf, seq, nbatch, block0, l, fw):
    tr = min(seq, 512)
    nr = seq // tr
    cs, w1 = _dft_constants(seq)
    return pl.pallas_call(
        functools.partial(_fnet_kernel, seq),
        grid=(nbatch, nr),
        in_specs=[
            pl.BlockSpec((seq, FNET_W), lambda b, r: (block0 + b, 0)),
            pl.BlockSpec((tr, 2 * seq), lambda b, r: (r, 0)),
            pl.BlockSpec((FNET_W, 2 * FNET_W), lambda b, r: (0, 0)),
            pl.BlockSpec((None, FNET_W, FNET_W), lambda b, r: (l, 0, 0)),
        ],
        out_specs=pl.BlockSpec((tr, FNET_W), lambda b, r: (b * nr + r, 0)),
        out_shape=jax.ShapeDtypeStruct((nbatch * seq, FNET_W), _BF),
        scratch_shapes=[pltpu.VMEM((2 * seq, FNET_W), _BF)],
        compiler_params=_params(("arbitrary", "arbitrary")),
        name="fourier_mix_%d" % seq,
    )(uf, cs, w1, fw)


def _route_top2(r_ref, h, ei_ref, ew_ref):
    lg = lax.dot_general(r_ref[...].astype(_BF), h.astype(_BF), _NT, preferred_element_type=_F32)
    eid = lax.broadcasted_iota(jnp.int32, lg.shape, 0)
    m1 = jnp.max(lg, axis=0, keepdims=True)
    i1 = jnp.min(jnp.where(lg == m1, eid, N_EXPERTS), axis=0, keepdims=True)
    lg2 = jnp.where(eid == i1, -jnp.inf, lg)
    m2 = jnp.max(lg2, axis=0, keepdims=True)
    i2 = jnp.min(jnp.where(lg2 == m2, eid, N_EXPERTS), axis=0, keepdims=True)
    e = jnp.exp(m2 - m1)
    ei_ref[0:1, :] = i1
    ei_ref[1:2, :] = i2
    ew_ref[0:1, :] = 1.0 / (1.0 + e)
    ew_ref[1:2, :] = e / (1.0 + e)


def _outproj_kernel(split, route, *refs):
    branch_refs, refs = refs[:6], refs[6:]
    n_x = 2 if split else 1
    x_refs, refs = refs[:n_x], refs[n_x:]
    mod_ref, w_ref, gf_ref = refs[:3]
    refs = refs[3:]
    if route:
        r_ref, xo_ref, h_ref, ei_ref, ew_ref, wb_ref = refs
    else:
        xo_ref, h_ref, wb_ref = refs
    i = pl.program_id(0)

    @pl.when(i == 0)
    def _():
        wb_ref[...] = w_ref[...].astype(_BF)

    mix = jnp.concatenate([_pick_half(i, ROW_TILE, branch_refs[2 * n], branch_refs[2 * n + 1])
                           for n in range(3)], axis=-1)
    x = _pick_half(i, ROW_TILE, *x_refs) if split else x_refs[0][...]
    x_new = x + mod_ref[2:3, :] * jnp.dot(mix, wb_ref[...], preferred_element_type=_F32)
    xo_ref[...] = x_new
    h = _norm_mod(x_new, gf_ref[...], mod_ref[3:4, :], mod_ref[4:5, :])
    h_ref[...] = h.astype(h_ref.dtype)
    if route:
        _route_top2(r_ref, h, ei_ref, ew_ref)


def _outproj(branches, xs, l, mods, w_all, gf_all, h_dtype, router_t=None):
    split = len(xs) == 2
    route = router_t is not None
    row = lambda i: (i, 0)
    in_specs = []
    args = []
    for pair, width in zip(branches, (ATTN_W, CONV_W, FNET_W)):
        in_specs += _half_specs(width, ROW_TILE)
        args += list(pair)
    in_specs += _half_specs(D_MODEL, ROW_TILE) if split else [pl.BlockSpec((ROW_TILE, D_MODEL), row)]
    args += list(xs)
    mod_spec, gf_spec, w_spec = _layer_specs(l, (D_MODEL, D_MODEL))
    in_specs += [mod_spec, w_spec, gf_spec]
    args += [mods, w_all, gf_all.reshape(DEPTH, 1, D_MODEL)]
    out_specs = [pl.BlockSpec((ROW_TILE, D_MODEL), row)] * 2
    out_shape = [jax.ShapeDtypeStruct((T, D_MODEL), _F32), jax.ShapeDtypeStruct((T, D_MODEL), h_dtype)]
    if route:
        in_specs.append(pl.BlockSpec((N_EXPERTS, D_MODEL), lambda i: (0, 0)))
        args.append(router_t)
        out_specs += [pl.BlockSpec((TOP_K, ROW_TILE), lambda i: (0, i))] * 2
        out_shape += [jax.ShapeDtypeStruct((TOP_K, T), jnp.int32), jax.ShapeDtypeStruct((TOP_K, T), _F32)]
    return pl.pallas_call(
        functools.partial(_outproj_kernel, split, route),
        grid=(T // ROW_TILE,),
        in_specs=in_specs,
        out_specs=out_specs,
        out_shape=out_shape,
        scratch_shapes=[pltpu.VMEM((D_MODEL, D_MODEL), _BF)],
        compiler_params=_params(("arbitrary",)),
        name="outproj_route" if route else "outproj",
    )(*args)


def _ffn_kernel(nj, n_sub_rows, vis_e, vis_start, vis_cnt, used_sub,
                x_hbm, wg_ref, wu_ref, wd_ref, y_hbm,
                big, xb, wgb, wub, wdb, sem_in, sem_out):
    del vis_e
    v = pl.program_id(0)
    j = pl.program_id(1)
    cnt = vis_cnt[v]
    row0 = vis_start[v] * SUB

    x_is_bf16 = x_hbm.dtype == _BF
    landing = xb if x_is_bf16 else big

    def copy_in(s):
        r = pl.multiple_of(row0 + s * SUB, SUB)
        b = pl.multiple_of(s * SUB, SUB)
        return pltpu.make_async_copy(x_hbm.at[pl.ds(r, SUB)], landing.at[pl.ds(b, SUB)], sem_in.at[s])

    def copy_out(s):
        r = pl.multiple_of(row0 + s * SUB, SUB)
        b = pl.multiple_of(s * SUB, SUB)
        return pltpu.make_async_copy(big.at[pl.ds(b, SUB)], y_hbm.at[pl.ds(r, SUB)], sem_out)

    def for_subs(fn):
        lax.fori_loop(0, cnt, lambda s, c: (fn(s), c)[1], 0)

    @pl.when(jnp.logical_and(j == 0, cnt > 0))
    def _():
        for_subs(lambda s: copy_in(s).start())

    @pl.when(cnt > 0)
    def _():
        wgb[...] = wg_ref[...].astype(_BF)
        wub[...] = wu_ref[...].astype(_BF)
        wdb[...] = wd_ref[...].astype(_BF)

        def block(r0, nrows):
            s0 = r0 // SUB

            @pl.when(j == 0)
            def _():
                for t in range(nrows // SUB):
                    copy_in(s0 + t).wait()
                    rs = pl.ds(pl.multiple_of(r0 + t * SUB, SUB), SUB)
                    if not x_is_bf16:
                        xb[rs, :] = big[rs, :].astype(_BF)
                    big[rs, :] = jnp.zeros((SUB, D_MODEL), _F32)

            rs = pl.ds(r0, nrows)
            x = xb[rs, :]
            a = jnp.dot(x, wgb[...], preferred_element_type=_F32)
            b = jnp.dot(x, wub[...], preferred_element_type=_F32)
            p = (a * _sigmoid(a) * b).astype(_BF)
            big[rs, :] += jnp.dot(p, wdb[...], preferred_element_type=_F32)

            @pl.when(j == nj - 1)
            def _():
                for t in range(nrows // SUB):
                    copy_out(s0 + t).start()

        blk = BLK_SUBS * SUB
        nfull = cnt // BLK_SUBS
        lax.fori_loop(0, nfull, lambda i, c: (block(pl.multiple_of(i * blk, blk), blk), c)[1], 0)
        rem = cnt % BLK_SUBS
        base = nfull * blk
        bit = BLK_SUBS // 2
        while bit >= 1:
            take = (rem // bit) % 2 == 1
            pl.when(take)(functools.partial(block, pl.multiple_of(base, SUB), bit * SUB))
            base = base + jnp.where(take, bit * SUB, 0)
            bit //= 2

    @pl.when(jnp.logical_and(j == nj - 1, cnt > 0))
    def _():
        for_subs(lambda s: copy_out(s).wait())

    if x_hbm.dtype == y_hbm.dtype:
        @pl.when(jnp.logical_and(v == pl.num_programs(0) - 1, j == nj - 1))
        def _():
            def tail(s):
                r = pl.multiple_of(s * SUB, SUB)
                return pltpu.make_async_copy(x_hbm.at[pl.ds(r, SUB)], y_hbm.at[pl.ds(r, SUB)], sem_out)
            lax.fori_loop(used_sub[0], n_sub_rows, lambda s, c: (tail(s).start(), c)[1], 0)
            lax.fori_loop(used_sub[0], n_sub_rows, lambda s, c: (tail(s).wait(), c)[1], 0)


def _grouped_ffn(x, wg, wu, wd, fc, vis_e, vis_start, vis_cnt, used_sub):
    ff = wg.shape[-1]
    nj = ff // fc
    nv = vis_e.shape[0]
    n_rows = x.shape[0]

    def chunk(v, j, vc):
        return jnp.where(vc[v] > 0, j, nj - 1)

    return pl.pallas_call(
        functools.partial(_ffn_kernel, nj, n_rows // SUB),
        grid_spec=pltpu.PrefetchScalarGridSpec(
            num_scalar_prefetch=4,
            grid=(nv, nj),
            in_specs=[
                pl.BlockSpec(memory_space=pl.ANY),
                pl.BlockSpec((None, D_MODEL, fc), lambda v, j, ve, vs, vc, us: (ve[v], 0, chunk(v, j, vc))),
                pl.BlockSpec((None, D_MODEL, fc), lambda v, j, ve, vs, vc, us: (ve[v], 0, chunk(v, j, vc))),
                pl.BlockSpec((None, fc, D_MODEL), lambda v, j, ve, vs, vc, us: (ve[v], chunk(v, j, vc), 0)),
            ],
            out_specs=pl.BlockSpec(memory_space=pl.ANY),
            scratch_shapes=[
                pltpu.VMEM((TMAX, D_MODEL), _F32),
                pltpu.VMEM((TMAX, D_MODEL), _BF),
                pltpu.VMEM((D_MODEL, fc), _BF),
                pltpu.VMEM((D_MODEL, fc), _BF),
                pltpu.VMEM((fc, D_MODEL), _BF),
                pltpu.SemaphoreType.DMA((SUB_MAX,)),
                pltpu.SemaphoreType.DMA(()),
            ],
        ),
        out_shape=jax.ShapeDtypeStruct((n_rows, D_MODEL), _F32),
        compiler_params=_params(("arbitrary", "arbitrary"), FFN_VMEM_LIMIT),
        name="ffn_%d" % ff,
    )(vis_e, vis_start, vis_cnt, used_sub, x, wg, wu, wd)


def _dense_visits():
    nv = T // TMAX
    return (jnp.zeros((nv,), jnp.int32),
            jnp.arange(nv, dtype=jnp.int32) * SUB_MAX,
            jnp.full((nv,), SUB_MAX, jnp.int32),
            jnp.full((1,), T // SUB, jnp.int32))


N_SLOT_SUB = (TOP_K * T) // SUB + N_EXPERTS
N_SLOT = N_SLOT_SUB * SUB
N_VISIT = -(-N_SLOT_SUB // SUB_MAX) + N_EXPERTS


def _routing_plan(eidx):
    e_flat = eidx.reshape(-1)
    onehot = e_flat[:, None] == jnp.arange(N_EXPERTS, dtype=jnp.int32)[None, :]
    ch = 128
    oh = onehot.astype(_F32).reshape(-1, ch, N_EXPERTS)
    tri = (jnp.arange(ch)[:, None] >= jnp.arange(ch)[None, :]).astype(_F32)
    within = jnp.einsum("ij,cjk->cik", tri, oh)
    tot = within[:, -1, :]
    csum = (within + (jnp.cumsum(tot, axis=0) - tot)[:, None, :]).reshape(-1, N_EXPERTS).astype(jnp.int32)
    counts = csum[-1]
    nsub = (counts + SUB - 1) // SUB
    sub_base = jnp.cumsum(nsub) - nsub
    slot = jnp.sum(jnp.where(onehot, csum - 1 + (sub_base * SUB)[None, :], 0), axis=1).astype(jnp.int32)
    pad_start = (sub_base * SUB + counts).astype(jnp.int32)
    pad_cnt = (nsub * SUB - counts).astype(jnp.int32)
    used_sub = jnp.sum(nsub).reshape(1).astype(jnp.int32)
    nvis = (nsub + SUB_MAX - 1) // SUB_MAX
    vend = jnp.cumsum(nvis)
    total = vend[-1]
    vid = jnp.arange(N_VISIT, dtype=jnp.int32)
    ve = jnp.minimum(jnp.sum((vid[:, None] >= vend[None, :]).astype(jnp.int32), axis=1), N_EXPERTS - 1)
    local = vid - (vend - nvis)[ve]
    nv_e = jnp.maximum(nvis[ve], 1)
    q, r = nsub[ve] // nv_e, nsub[ve] % nv_e
    cnt = q + (local < r).astype(jnp.int32)
    start = sub_base[ve] + local * q + jnp.minimum(local, r)
    used = vid < total
    last_e = ve[jnp.maximum(total - 1, 0)]
    vis_e = jnp.where(used, ve, last_e).astype(jnp.int32)
    vis_cnt = jnp.where(used, cnt, 0).astype(jnp.int32)
    vis_start = jnp.where(used, start, 0).astype(jnp.int32)
    return slot, pad_start, pad_cnt, (vis_e, vis_start, vis_cnt, used_sub)


def _dispatch_kernel(slot, pad_start, pad_cnt, used_sub, h_ref, xs_hbm, ring, tile_sems, sem):
    i = pl.program_id(0)
    last = pl.num_programs(0) - 1
    b = i % 2

    def drain(slot_id):
        for _ in range(TOP_K):
            pltpu.make_async_copy(ring.at[slot_id], xs_hbm.at[pl.ds(0, ROW_TILE)], tile_sems.at[slot_id]).wait()

    pl.when(i >= 2)(lambda: drain(b))
    ring[b] = h_ref[...]

    def put(r, carry):
        for k in range(TOP_K):
            s = slot[k * T + i * ROW_TILE + r]
            pltpu.make_async_copy(ring.at[b, pl.ds(r, 1)], xs_hbm.at[pl.ds(s, 1)], tile_sems.at[b]).start()
        return carry
    lax.fori_loop(0, ROW_TILE, put, 0, unroll=4)

    @pl.when(i == last)
    def _():
        drain(1 - b)
        drain(b)

    @pl.when(i == last)
    def _():
        n_pad = 0
        for e in range(N_EXPERTS):
            s0 = pad_start[e]

            def fill(r, carry, s0=s0):
                pltpu.make_async_copy(h_ref.at[pl.ds(0, 1)], xs_hbm.at[pl.ds(s0 + r, 1)], sem).start()
                return carry
            lax.fori_loop(0, pad_cnt[e], fill, 0)
            n_pad = n_pad + pad_cnt[e]

        def tail(s):
            r = pl.multiple_of(s * SUB, SUB)
            return pltpu.make_async_copy(h_ref.at[pl.ds(0, SUB)], xs_hbm.at[pl.ds(r, SUB)], sem)
        lax.fori_loop(used_sub[0], N_SLOT_SUB, lambda s, c: (tail(s).start(), c)[1], 0)
        lax.fori_loop(used_sub[0], N_SLOT_SUB, lambda s, c: (tail(s).wait(), c)[1], 0)
        row_wait = pltpu.make_async_copy(h_ref.at[pl.ds(0, 1)], xs_hbm.at[pl.ds(0, 1)], sem)
        lax.fori_loop(0, n_pad, lambda r, c: (row_wait.wait(), c)[1], 0)


def _dispatch(h, slot, pad_start, pad_cnt, used_sub):
    return pl.pallas_call(
        _dispatch_kernel,
        grid_spec=pltpu.PrefetchScalarGridSpec(
            num_scalar_prefetch=4,
            grid=(T // ROW_TILE,),
            in_specs=[pl.BlockSpec((ROW_TILE, D_MODEL), lambda i, *_: (i, 0))],
            out_specs=pl.BlockSpec(memory_space=pl.ANY),
            scratch_shapes=[pltpu.VMEM((2, ROW_TILE, D_MODEL), _F32),
                            pltpu.SemaphoreType.DMA((2,)), pltpu.SemaphoreType.DMA(())],
        ),
        out_shape=jax.ShapeDtypeStruct((N_SLOT, D_MODEL), _F32),
        compiler_params=_params(("arbitrary",)),
        name="dispatch",
    )(slot, pad_start, pad_cnt, used_sub, h)


def _residual_out(x, gate, f, g_ref, o_refs, i, tile):
    out = x + gate * f
    if g_ref is not None:
        ms = jnp.mean(out * out, axis=-1, keepdims=True)
        out = out * lax.rsqrt(ms + EPS) * g_ref[...]
    if len(o_refs) == 1:
        o_refs[0][...] = out
    else:
        @pl.when(i < T_PROMPT // tile)
        def _():
            o_refs[0][...] = out

        @pl.when(i >= T_PROMPT // tile)
        def _():
            o_refs[1][...] = out


def _out_specs(final, tile):
    if not final:
        return ([pl.BlockSpec((tile, D_MODEL), lambda i, *_: (i, 0))],
                [jax.ShapeDtypeStruct((T, D_MODEL), _F32)])
    return (_half_specs(D_MODEL, tile),
            [jax.ShapeDtypeStruct((T_PROMPT, D_MODEL), _F32), jax.ShapeDtypeStruct((T_SAMPLE, D_MODEL), _F32)])


def _combine_kernel(final, x_ref, mod_ref, y_ref, *rest):
    g_ref, o_refs = (rest[0], rest[1:]) if final else (None, rest)
    _residual_out(x_ref[...], mod_ref[5:6, :], y_ref[...], g_ref, o_refs, pl.program_id(0), ROW_TILE)


def _combine(x, l, mods, y, g_final=None):
    final = g_final is not None
    row = lambda i: (i, 0)
    in_specs = [
        pl.BlockSpec((ROW_TILE, D_MODEL), row),
        _layer_specs(l, ())[0],
        pl.BlockSpec((ROW_TILE, D_MODEL), row),
    ]
    args = [x, mods, y]
    if final:
        in_specs.append(pl.BlockSpec((1, D_MODEL), lambda i: (0, 0)))
        args.append(g_final.reshape(1, D_MODEL))
    out_specs, out_shape = _out_specs(final, ROW_TILE)
    return pl.pallas_call(
        functools.partial(_combine_kernel, final),
        grid=(T // ROW_TILE,),
        in_specs=in_specs,
        out_specs=out_specs,
        out_shape=out_shape,
        compiler_params=_params(("arbitrary",)),
        name="combine%s" % ("_final" if final else ""),
    )(*args)


def _combine_top2_kernel(final, n_out, slot, x_ref, mod_ref, w_ref, *rest):
    g_ref, rest = (rest[0], rest[1:]) if final else (None, rest)
    ys_hbm, o_refs, (ybuf, sems) = rest[0], rest[1:1 + n_out], rest[1 + n_out:]
    i = pl.program_id(0)

    def fetch(tile, b):
        def one(r, carry):
            for k in range(TOP_K):
                s = slot[k * T + tile * COMB_TILE + r]
                pltpu.make_async_copy(ys_hbm.at[pl.ds(s, 1)], ybuf.at[b, k, pl.ds(r, 1)], sems.at[b]).start()
            return carry
        lax.fori_loop(0, COMB_TILE, one, 0, unroll=4)

    @pl.when(i == 0)
    def _():
        fetch(0, 0)

    @pl.when(i + 1 < pl.num_programs(0))
    def _():
        fetch(i + 1, (i + 1) % 2)

    b = i % 2
    for k in range(TOP_K):
        pltpu.make_async_copy(ys_hbm.at[pl.ds(0, COMB_TILE)], ybuf.at[b, k], sems.at[b]).wait()
    w = w_ref[...]
    f = w[:, 0:1] * ybuf[b, 0] + w[:, 1:2] * ybuf[b, 1]
    _residual_out(x_ref[...], mod_ref[5:6, :], f, g_ref, o_refs, i, COMB_TILE)


def _combine_top2(x, l, mods, ys, slot, w, g_final=None):
    final = g_final is not None
    row = lambda i, s: (i, 0)
    in_specs = [
        pl.BlockSpec((COMB_TILE, D_MODEL), row),
        pl.BlockSpec((None, None, 6, D_MODEL), lambda i, s: (l, _cond_of_tile(i, COMB_TILE), 0, 0)),
        pl.BlockSpec((COMB_TILE, TOP_K), row),
    ]
    args = [x, mods, w]
    if final:
        in_specs.append(pl.BlockSpec((1, D_MODEL), lambda i, s: (0, 0)))
        args.append(g_final.reshape(1, D_MODEL))
    in_specs.append(pl.BlockSpec(memory_space=pl.ANY))
    args.append(ys)
    out_specs, out_shape = _out_specs(final, COMB_TILE)
    return pl.pallas_call(
        functools.partial(_combine_top2_kernel, final, len(out_specs)),
        grid_spec=pltpu.PrefetchScalarGridSpec(
            num_scalar_prefetch=1,
            grid=(T // COMB_TILE,),
            in_specs=in_specs,
            out_specs=out_specs,
            scratch_shapes=[
                pltpu.VMEM((2, TOP_K, COMB_TILE, D_MODEL), _F32),
                pltpu.SemaphoreType.DMA((2,)),
            ],
        ),
        out_shape=out_shape,
        compiler_params=_params(("arbitrary",)),
        name="combine_top2%s" % ("_final" if final else ""),
    )(slot, *args)


def kernel(x_prompt, x_sample, cache_k, cache_v, c, c_ctx, w_ada, b_ada, g_norm_mix, g_norm_ffn,
           w_in, w_out, attn_sink, conv_dw, conv_dw_b, conv_ln_g, conv_ln_b, conv_pw, fnet_w,
           ffn_w_gate, ffn_w_up, ffn_w_down, moe_router, moe_w_gate, moe_w_up, moe_w_down, g_final):
    xs = (x_prompt.reshape(T_PROMPT, D_MODEL), x_sample.reshape(T_SAMPLE, D_MODEL))
    cond8 = jnp.concatenate([c_ctx[None, :], c, jnp.zeros((N_COND - 1 - DEC_BATCH, D_MODEL), _F32)], axis=0)
    mods = _modulation(cond8, w_ada, b_ada).reshape(DEPTH, N_COND, 6, D_MODEL)
    cos_t, sin_t = _rope_tables()
    ck_all = cache_k.reshape(DEC_BATCH, DEPTH, PAST_LEN, KV_W)
    cv_all = cache_v.reshape(DEC_BATCH, DEPTH, PAST_LEN, KV_W)
    p_blocks = T_PROMPT // DEC_SEQ
    sinks = attn_sink.reshape(DEPTH * N_HEADS)

    ks, vs = [], []
    resid = None
    for l in range(DEPTH):
        q, k, v, uc, uf, *x_new = _inproj(xs, l, mods, g_norm_mix, w_in, cos_t, sin_t, resid=resid)
        if resid is not None:
            xs, resid = tuple(x_new), None
        ks.append(k[:T_PROMPT])
        vs.append(v[:T_PROMPT])
        attn = (_ctx_attention(l, sinks, q, k, v),
                _lat_attention(l, sinks, q, k, v, ck_all, cv_all))
        cargs = (l, conv_dw, conv_dw_b, conv_ln_g, conv_ln_b, conv_pw)
        conv = (_conv_module(uc, SEQ, BATCH, 0, *cargs),
                _conv_module(uc, DEC_SEQ, DEC_BATCH, p_blocks, *cargs))
        four = (_fourier_mix(uf, SEQ, BATCH, 0, l, fnet_w),
                _fourier_mix(uf, DEC_SEQ, DEC_BATCH, p_blocks, l, fnet_w))
        last = g_final if l == DEPTH - 1 else None
        i = l // 2
        if l % 2 == 0:
            x, h = _outproj((attn, conv, four), xs, l, mods, w_out, g_norm_ffn, _BF)
            y = _grouped_ffn(h, ffn_w_gate[i:i + 1], ffn_w_up[i:i + 1], ffn_w_down[i:i + 1], 256,
                             *_dense_visits())
            if last is None:
                xs, resid = (x,), (y, l)
            else:
                xs = tuple(_combine(x, l, mods, y, g_final=last))
        else:
            x, h, eidx, ew = _outproj((attn, conv, four), xs, l, mods, w_out, g_norm_ffn, _F32,
                                      router_t=moe_router[i].T)
            slot, pad_start, pad_cnt, visits = _routing_plan(eidx)
            xd = _dispatch(h, slot, pad_start, pad_cnt, visits[-1])
            ys = _grouped_ffn(xd, moe_w_gate[i], moe_w_up[i], moe_w_down[i], 256, *visits)
            xs = tuple(_combine_top2(x, l, mods, ys, slot, ew.T, g_final=last))

    y_prompt = xs[0].reshape(BATCH, SEQ, D_MODEL)
    y_sample = xs[1].reshape(DEC_BATCH, DEC_SEQ, D_MODEL)
    state_k = jnp.stack([a.reshape(BATCH, SEQ, N_KV_HEADS, HEAD_DIM) for a in ks], axis=1)
    state_v = jnp.stack([a.reshape(BATCH, SEQ, N_KV_HEADS, HEAD_DIM) for a in vs], axis=1)
    return (y_prompt, y_sample, state_k, state_v)
```

```python
import functools

import numpy as np
import jax
import jax.numpy as jnp
from jax import lax
from jax.experimental import pallas as pl
from jax.experimental.pallas import tpu as pltpu

D_MODEL = 1024
BATCH = 16
SEQ = 256
DEPTH = 2
DEC_BATCH = 2
DEC_SEQ = 2048
PAST_LEN = 512
GRID_W = 64
HEAD_DIM = 64
N_HEADS = 8
N_KV_HEADS = 2
GQA_GROUP = N_HEADS // N_KV_HEADS
ATTN_W = N_HEADS * HEAD_DIM
KV_W = N_KV_HEADS * HEAD_DIM
WINDOW = 128
BLOCK = 128
ROPE_THETA = 10000.0
CONV_W = D_MODEL // 4
CONV_K = 31
FNET_GROUPS = 4
FNET_W = D_MODEL // 4
FNET_GW = FNET_W // FNET_GROUPS
IN_WIDTH = ATTN_W + 2 * KV_W + 2 * CONV_W + FNET_W
D_FF = 2816
N_EXPERTS = 8
TOP_K = 2
D_FF_EXPERT = 3584
EPS = 1e-6
NEG = -1e30

T_PROMPT = BATCH * SEQ
T_SAMPLE = DEC_BATCH * DEC_SEQ
T = T_PROMPT + T_SAMPLE
N_COND = 8

ROW_TILE = 512
SUB = 256
SUB_MAX = 16
TMAX = SUB * SUB_MAX
BLK_SUBS = 4
DISP_TILE = 512
N_TILES = T // DISP_TILE
PAIRS = TOP_K * DISP_TILE
SEG_ALIGN = 8
SORT_ROWS = 1152
CONV_CHUNK = 64
CONV_PAD = 16
CONV_SPAN = CONV_CHUNK + 8 * ((CONV_PAD + CONV_K // 2) // 8)
VMEM_LIMIT = 48 * 1024 * 1024
FFN_VMEM_LIMIT = 56 * 1024 * 1024

_BF = jnp.bfloat16
_F32 = jnp.float32


def _cond_of_tile(i, tile):
    r = i * tile
    return jnp.where(r < T_PROMPT, 0, 1 + (r - T_PROMPT) // DEC_SEQ)


def _params(sem, vmem=VMEM_LIMIT):
    return pltpu.CompilerParams(dimension_semantics=sem, vmem_limit_bytes=vmem)


def _sigmoid(x):
    return 1.0 / (1.0 + jnp.exp(-x))


def _mod_kernel(cond_ref, w_ref, b_ref, o_ref):
    cnd = cond_ref[...]
    s = (cnd * _sigmoid(cnd)).astype(_BF)
    o_ref[...] = jnp.dot(s, w_ref[...].astype(_BF), preferred_element_type=_F32) + b_ref[...]


def _modulation(cond8, w_ada, b_ada):
    nt = 1536
    return pl.pallas_call(
        _mod_kernel,
        grid=(DEPTH, 6 * D_MODEL // nt),
        in_specs=[
            pl.BlockSpec((N_COND, D_MODEL), lambda l, n: (0, 0)),
            pl.BlockSpec((None, D_MODEL, nt), lambda l, n: (l, 0, n)),
            pl.BlockSpec((None, 1, nt), lambda l, n: (l, 0, n)),
        ],
        out_specs=pl.BlockSpec((None, N_COND, nt), lambda l, n: (l, 0, n)),
        out_shape=jax.ShapeDtypeStruct((DEPTH, N_COND, 6 * D_MODEL), _F32),
        compiler_params=_params(("arbitrary", "arbitrary")),
        name="modulation",
    )(cond8, w_ada, b_ada.reshape(DEPTH, 1, 6 * D_MODEL))


def _norm_mod(x, g, shift, scale):
    ms = jnp.mean(x * x, axis=-1, keepdims=True)
    y = x * lax.rsqrt(ms + EPS) * g
    return y * (1.0 + scale) + shift


def _rope_tables():
    rows = DEC_SEQ // GRID_W
    n_freq = HEAD_DIM // 4
    inv = ROPE_THETA ** (-jnp.arange(n_freq, dtype=_F32) / n_freq)
    gr, gc = jnp.meshgrid(jnp.arange(rows, dtype=_F32), jnp.arange(GRID_W, dtype=_F32), indexing="ij")
    ang_r = gr.reshape(-1)[:, None] * inv
    ang_c = gc.reshape(-1)[:, None] * inv
    cr, sr, cc, sc = jnp.cos(ang_r), jnp.sin(ang_r), jnp.cos(ang_c), jnp.sin(ang_c)
    cos64 = jnp.concatenate([cr, cr, cc, cc], axis=-1)
    sin64 = jnp.concatenate([-sr, sr, -sc, sc], axis=-1)
    return jnp.tile(cos64, (1, 2)), jnp.tile(sin64, (1, 2))


def _rope128(x, cos, sin):
    lane = lax.broadcasted_iota(jnp.int32, x.shape, 1)
    first = (lane % 32) < 16
    partner = jnp.where(first, pltpu.roll(x, 128 - 16, 1), pltpu.roll(x, 16, 1))
    return x * cos + partner * sin


def _half_specs(width, tile):
    pt = T_PROMPT // tile
    return [pl.BlockSpec((tile, width), lambda i, *_: (jnp.minimum(i, pt - 1), 0)),
            pl.BlockSpec((tile, width), lambda i, *_: (jnp.maximum(i - pt, 0), 0))]


def _pick_half(i, tile, a_ref, b_ref):
    return jnp.where(i < T_PROMPT // tile, a_ref[...], b_ref[...])


def _layer_specs(l, w_shape):
    return [
        pl.BlockSpec((None, None, 6, D_MODEL), lambda i, *_: (l, _cond_of_tile(i, ROW_TILE), 0, 0)),
        pl.BlockSpec((None, 1, D_MODEL), lambda i, *_: (l, 0, 0)),
        pl.BlockSpec((None,) + w_shape, lambda i, *_: (l, 0, 0)),
    ]


def _inproj_kernel(split, resid, *refs):
    n_x = 2 if split else 1
    x_refs, refs = refs[:n_x], refs[n_x:]
    if resid:
        y_ref, modp_ref = refs[:2]
        refs = refs[2:]
    mod_ref, g_ref, w_ref, cos_ref, sin_ref, q_ref, k_ref, v_ref, uc_ref, uf_ref = refs[:10]
    wb_ref = refs[-1]
    i = pl.program_id(0)

    @pl.when(i == 0)
    def _():
        wb_ref[...] = w_ref[...].astype(_BF)

    x = _pick_half(i, ROW_TILE, *x_refs) if split else x_refs[0][...]
    if resid:
        x = x + modp_ref[5:6, :] * y_ref[...]
        refs[10][...] = x
    h = _norm_mod(x, g_ref[...], mod_ref[0:1, :], mod_ref[1:2, :]).astype(_BF)
    latent = i >= T_PROMPT // ROW_TILE
    cos = jnp.where(latent, cos_ref[...], 1.0)
    sin = jnp.where(latent, sin_ref[...], 0.0)

    def proj(o, n):
        return jnp.dot(h, wb_ref[:, o:o + n], preferred_element_type=_F32)

    o = 0
    for c in range(ATTN_W // 256):
        qq = proj(o, 256)
        q_ref[:, o:o + 128] = _rope128(qq[:, :128], cos, sin)
        q_ref[:, o + 128:o + 256] = _rope128(qq[:, 128:], cos, sin)
        o += 256
    kv = proj(o, 2 * KV_W)
    k_ref[...] = _rope128(kv[:, :KV_W], cos, sin)
    v_ref[...] = kv[:, KV_W:]
    o += 2 * KV_W
    uc_ref[...] = proj(o, 2 * CONV_W)
    o += 2 * CONV_W
    uf_ref[...] = proj(o, FNET_W)


def _inproj(xs, l, mods, g_all, w_all, cos_t, sin_t, resid=None):
    split = len(xs) == 2
    pt = T_PROMPT // ROW_TILE
    per_seq = DEC_SEQ // ROW_TILE

    def rope_idx(i):
        return (jnp.maximum(i - pt, 0) % per_seq, 0)

    row = lambda i: (i, 0)
    in_specs = _half_specs(D_MODEL, ROW_TILE) if split else [pl.BlockSpec((ROW_TILE, D_MODEL), row)]
    args = list(xs)
    outs = [ATTN_W, KV_W, KV_W, 2 * CONV_W, FNET_W]
    if resid is not None:
        y, l_prev = resid
        in_specs += [pl.BlockSpec((ROW_TILE, D_MODEL), row), _layer_specs(l_prev, ())[0]]
        args += [y, mods]
        outs.append(D_MODEL)
    return pl.pallas_call(
        functools.partial(_inproj_kernel, split, resid is not None),
        grid=(T // ROW_TILE,),
        in_specs=in_specs + _layer_specs(l, (D_MODEL, IN_WIDTH)) + [
            pl.BlockSpec((ROW_TILE, 128), rope_idx),
            pl.BlockSpec((ROW_TILE, 128), rope_idx),
        ],
        out_specs=[pl.BlockSpec((ROW_TILE, n), row) for n in outs],
        out_shape=[jax.ShapeDtypeStruct((T, n), _F32) for n in outs],
        scratch_shapes=[pltpu.VMEM((D_MODEL, IN_WIDTH), _BF)],
        compiler_params=_params(("arbitrary",)),
        name="inproj_resid" if resid is not None else "inproj",
    )(*args, mods, g_all.reshape(DEPTH, 1, D_MODEL), w_all, cos_t, sin_t)


def _stack_groups(q, kh, rows):
    parts = [q[:, (kh * GQA_GROUP + g) * HEAD_DIM:(kh * GQA_GROUP + g + 1) * HEAD_DIM] for g in range(GQA_GROUP)]
    return jnp.concatenate(parts, axis=0)


def _sink_column(sink_ref, head0, rows):
    r = lax.broadcasted_iota(jnp.int32, (GQA_GROUP * rows, 1), 0)
    col = jnp.full((GQA_GROUP * rows, 1), sink_ref[head0], _F32)
    for g in range(1, GQA_GROUP):
        col = jnp.where(r >= g * rows, sink_ref[head0 + g], col)
    return col


def _unstack_store(o_ref, o, kh, rows):
    for pair in range(GQA_GROUP // 2):
        a = o[(2 * pair) * rows:(2 * pair + 1) * rows]
        b = o[(2 * pair + 1) * rows:(2 * pair + 2) * rows]
        c0 = (kh * GQA_GROUP + 2 * pair) * HEAD_DIM
        o_ref[:, c0:c0 + 2 * HEAD_DIM] = jnp.concatenate([a, b], axis=-1).astype(o_ref.dtype)


_NT = (((1,), (1,)), ((), ()))


def _ctx_attn_kernel(l, sink_ref, q_ref, k_ref, v_ref, o_ref):
    scale = HEAD_DIM ** -0.5
    q = q_ref[...] * scale
    for kh in range(N_KV_HEADS):
        kk = k_ref[:, kh * HEAD_DIM:(kh + 1) * HEAD_DIM].astype(_BF)
        vv = v_ref[:, kh * HEAD_DIM:(kh + 1) * HEAD_DIM].astype(_BF)
        qs = _stack_groups(q, kh, SEQ).astype(_BF)
        s = lax.dot_general(qs, kk, _NT, preferred_element_type=_F32)
        sink = _sink_column(sink_ref, l * N_HEADS + kh * GQA_GROUP, SEQ)
        m = jnp.maximum(jnp.max(s, axis=-1, keepdims=True), sink)
        p = jnp.exp(s - m)
        den = jnp.sum(p, axis=-1, keepdims=True) + jnp.exp(sink - m)
        o = jnp.dot(p.astype(_BF), vv, preferred_element_type=_F32) / den
        _unstack_store(o_ref, o, kh, SEQ)


def _ctx_attention(l, sinks, q, k, v):
    row = lambda b, s: (b, 0)
    return pl.pallas_call(
        functools.partial(_ctx_attn_kernel, l),
        grid_spec=pltpu.PrefetchScalarGridSpec(
            num_scalar_prefetch=1,
            grid=(BATCH,),
            in_specs=[
                pl.BlockSpec((SEQ, ATTN_W), row),
                pl.BlockSpec((SEQ, KV_W), row),
                pl.BlockSpec((SEQ, KV_W), row),
            ],
            out_specs=pl.BlockSpec((SEQ, ATTN_W), row),
        ),
        out_shape=jax.ShapeDtypeStruct((T_PROMPT, ATTN_W), _BF),
        compiler_params=_params(("arbitrary",)),
        name="ctx_attention",
    )(sinks, q, k, v)


def _lat_attn_kernel(l, sink_ref, q_ref, k_ref, v_ref, ck_ref, cv_ref, o_ref):
    n = pl.program_id(1)
    scale = HEAD_DIM ** -0.5
    nb = DEC_SEQ // BLOCK
    band = 3 * BLOCK
    start = pl.multiple_of(jnp.clip(n - 1, 0, nb - 3) * BLOCK, BLOCK)
    q = q_ref[...] * scale
    kb = k_ref[pl.ds(start, band), :]
    vb = v_ref[pl.ds(start, band), :]
    rows = GQA_GROUP * BLOCK
    qpos = n * BLOCK + lax.broadcasted_iota(jnp.int32, (rows, band), 0) % BLOCK
    kpos = start + lax.broadcasted_iota(jnp.int32, (rows, band), 1)
    valid = jnp.abs(kpos - qpos) <= WINDOW
    for kh in range(N_KV_HEADS):
        hs = slice(kh * HEAD_DIM, (kh + 1) * HEAD_DIM)
        qs = _stack_groups(q, kh, BLOCK).astype(_BF)
        s_loc = lax.dot_general(qs, kb[:, hs].astype(_BF), _NT, preferred_element_type=_F32)
        s_loc = jnp.where(valid, s_loc, NEG)
        s_ctx = lax.dot_general(qs, ck_ref[:, hs].astype(_BF), _NT, preferred_element_type=_F32)
        sink = _sink_column(sink_ref, l * N_HEADS + kh * GQA_GROUP, BLOCK)
        m = jnp.maximum(jnp.maximum(jnp.max(s_loc, axis=-1, keepdims=True),
                                    jnp.max(s_ctx, axis=-1, keepdims=True)), sink)
        p_loc = jnp.exp(s_loc - m)
        p_ctx = jnp.exp(s_ctx - m)
        den = (jnp.sum(p_loc, axis=-1, keepdims=True) + jnp.sum(p_ctx, axis=-1, keepdims=True)
               + jnp.exp(sink - m))
        o = (jnp.dot(p_loc.astype(_BF), vb[:, hs].astype(_BF), preferred_element_type=_F32)
             + jnp.dot(p_ctx.astype(_BF), cv_ref[:, hs].astype(_BF), preferred_element_type=_F32)) / den
        _unstack_store(o_ref, o, kh, BLOCK)


def _lat_attention(l, sinks, q, k, v, ck, cv):
    nb = DEC_SEQ // BLOCK
    q0 = T_PROMPT // BLOCK
    s0 = T_PROMPT // DEC_SEQ
    return pl.pallas_call(
        functools.partial(_lat_attn_kernel, l),
        grid_spec=pltpu.PrefetchScalarGridSpec(
            num_scalar_prefetch=1,
            grid=(DEC_BATCH, nb),
            in_specs=[
                pl.BlockSpec((BLOCK, ATTN_W), lambda b, n, s: (q0 + b * nb + n, 0)),
                pl.BlockSpec((DEC_SEQ, KV_W), lambda b, n, s: (s0 + b, 0)),
                pl.BlockSpec((DEC_SEQ, KV_W), lambda b, n, s: (s0 + b, 0)),
                pl.BlockSpec((None, None, PAST_LEN, KV_W), lambda b, n, s: (b, l, 0, 0)),
                pl.BlockSpec((None, None, PAST_LEN, KV_W), lambda b, n, s: (b, l, 0, 0)),
            ],
            out_specs=pl.BlockSpec((BLOCK, ATTN_W), lambda b, n, s: (b * nb + n, 0)),
        ),
        out_shape=jax.ShapeDtypeStruct((T_SAMPLE, ATTN_W), _BF),
        compiler_params=_params(("arbitrary", "arbitrary")),
        name="lat_attention",
    )(sinks, q, k, v, ck, cv)


def _conv_kernel(seq, u_ref, dw_ref, dwb_ref, lg_ref, lb_ref, pw_ref, o_ref, pad_ref, y_ref, sh_ref):
    u = u_ref[...]
    pad_ref[0:CONV_PAD, :] = jnp.zeros((CONV_PAD, CONV_W), _F32)
    pad_ref[CONV_PAD + seq:2 * CONV_PAD + seq, :] = jnp.zeros((CONV_PAD, CONV_W), _F32)
    pad_ref[CONV_PAD:CONV_PAD + seq, :] = u[:, :CONV_W] * _sigmoid(u[:, CONV_W:])
    off = CONV_PAD - CONV_K // 2
    span = CONV_SPAN

    def chunk(c, carry):
        row = pl.multiple_of(c * CONV_CHUNK, CONV_CHUNK)
        win = pad_ref[pl.ds(row, CONV_CHUNK + 2 * CONV_PAD), :]
        acc = jnp.zeros((CONV_CHUNK, CONV_W), _F32) + dwb_ref[...]
        for phase in range(8):
            sh_ref[phase] = win[phase:phase + span, :]
        for phase in range(8):
            for a in range(span // 8):
                t = 8 * a + phase - off
                if 0 <= t < CONV_K:
                    acc = acc + sh_ref[phase, 8 * a:8 * a + CONV_CHUNK, :] * dw_ref[t:t + 1, :]
        mu = jnp.mean(acc, axis=-1, keepdims=True)
        d = acc - mu
        var = jnp.mean(d * d, axis=-1, keepdims=True)
        y = d * lax.rsqrt(var + EPS) * lg_ref[...] + lb_ref[...]
        y_ref[pl.ds(row, CONV_CHUNK), :] = (y * _sigmoid(y)).astype(_BF)
        return carry

    lax.fori_loop(0, seq // CONV_CHUNK, chunk, 0)
    o_ref[...] = jnp.dot(y_ref[...], pw_ref[...].astype(_BF), preferred_element_type=_F32).astype(o_ref.dtype)


def _conv_module(uc, seq, nbatch, block0, l, dw, dwb, lg, lb, pw):
    vec = lambda a: a.reshape(DEPTH, 1, CONV_W)
    layer = lambda b: (l, 0, 0)
    return pl.pallas_call(
        functools.partial(_conv_kernel, seq),
        grid=(nbatch,),
        in_specs=[
            pl.BlockSpec((seq, 2 * CONV_W), lambda b: (block0 + b, 0)),
            pl.BlockSpec((None, CONV_K, CONV_W), layer),
            pl.BlockSpec((None, 1, CONV_W), layer),
            pl.BlockSpec((None, 1, CONV_W), layer),
            pl.BlockSpec((None, 1, CONV_W), layer),
            pl.BlockSpec((None, CONV_W, CONV_W), layer),
        ],
        out_specs=pl.BlockSpec((seq, CONV_W), lambda b: (b, 0)),
        out_shape=jax.ShapeDtypeStruct((nbatch * seq, CONV_W), _BF),
        scratch_shapes=[pltpu.VMEM((seq + 2 * CONV_PAD, CONV_W), _F32), pltpu.VMEM((seq, CONV_W), _BF),
                        pltpu.VMEM((8, CONV_SPAN, CONV_W), _F32)],
        compiler_params=_params(("arbitrary",)),
        name="conv_module_%d" % seq,
    )(uc, dw, vec(dwb), vec(lg), vec(lb), pw)


def _dft_constants(seq):
    j = np.arange(seq, dtype=np.int64)
    ang = 2.0 * np.pi * ((j[:, None] * j[None, :]) % seq).astype(np.float64) / seq
    cs = np.concatenate([np.cos(ang), -np.sin(ang)], axis=1) / np.sqrt(seq)
    c = np.arange(FNET_GW, dtype=np.int64)
    angc = 2.0 * np.pi * ((c[:, None] * c[None, :]) % FNET_GW).astype(np.float64) / FNET_GW
    eye = np.eye(FNET_GROUPS)
    cc = np.kron(eye, np.cos(angc)) / np.sqrt(FNET_GW)
    sc = np.kron(eye, np.sin(angc)) / np.sqrt(FNET_GW)
    w1 = np.concatenate([cc, sc], axis=1)
    return jnp.asarray(cs, dtype=_F32), jnp.asarray(w1, dtype=_F32)


def _fnet_kernel(seq, u_ref, cs_ref, w1_ref, fw_ref, o_ref, xcs_ref):
    @pl.when(pl.program_id(1) == 0)
    def _():
        t = jnp.dot(u_ref[...].astype(_BF), w1_ref[...].astype(_BF), preferred_element_type=_F32)
        xcs_ref[0:seq, :] = t[:, :FNET_W].astype(_BF)
        xcs_ref[seq:2 * seq, :] = t[:, FNET_W:].astype(_BF)

    mixed = jnp.dot(cs_ref[...].astype(_BF), xcs_ref[...], preferred_element_type=_F32)
    o_ref[...] = jnp.dot(mixed.astype(_BF), fw_ref[...].astype(_BF),
                         preferred_element_type=_F32).astype(o_ref.dtype)


def _fourier_mix(uf, seq, nbatch, block0, l, fw):
    tr = min(seq, 512)
    nr = seq // tr
    cs, w1 = _dft_constants(seq)
    return pl.pallas_call(
        functools.partial(_fnet_kernel, seq),
        grid=(nbatch, nr),
        in_specs=[
            pl.BlockSpec((seq, FNET_W), lambda b, r: (block0 + b, 0)),
            pl.BlockSpec((tr, 2 * seq), lambda b, r: (r, 0)),
            pl.BlockSpec((FNET_W, 2 * FNET_W), lambda b, r: (0, 0)),
            pl.BlockSpec((None, FNET_W, FNET_W), lambda b, r: (l, 0, 0)),
        ],
        out_specs=pl.BlockSpec((tr, FNET_W), lambda b, r: (b * nr + r, 0)),
        out_shape=jax.ShapeDtypeStruct((nbatch * seq, FNET_W), _BF),
        scratch_shapes=[pltpu.VMEM((2 * seq, FNET_W), _BF)],
        compiler_params=_params(("arbitrary", "arbitrary")),
        name="fourier_mix_%d" % seq,
    )(uf, cs, w1, fw)


def _route_top2(r_ref, h, ei_ref, ew_ref):
    lg = lax.dot_general(r_ref[...].astype(_BF), h.astype(_BF), _NT, preferred_element_type=_F32)
    eid = lax.broadcasted_iota(jnp.int32, lg.shape, 0)
    m1 = jnp.max(lg, axis=0, keepdims=True)
    i1 = jnp.min(jnp.where(lg == m1, eid, N_EXPERTS), axis=0, keepdims=True)
    lg2 = jnp.where(eid == i1, -jnp.inf, lg)
    m2 = jnp.max(lg2, axis=0, keepdims=True)
    i2 = jnp.min(jnp.where(lg2 == m2, eid, N_EXPERTS), axis=0, keepdims=True)
    e = jnp.exp(m2 - m1)
    ei_ref[0:1, :] = i1
    ei_ref[1:2, :] = i2
    ew_ref[0:1, :] = 1.0 / (1.0 + e)
    ew_ref[1:2, :] = e / (1.0 + e)


def _outproj_kernel(split, route, *refs):
    branch_refs, refs = refs[:6], refs[6:]
    n_x = 2 if split else 1
    x_refs, refs = refs[:n_x], refs[n_x:]
    mod_ref, w_ref, gf_ref = refs[:3]
    refs = refs[3:]
    if route:
        r_ref, xo_ref, h_ref, ei_ref, ew_ref, wb_ref = refs
    else:
        xo_ref, h_ref, wb_ref = refs
    i = pl.program_id(0)

    @pl.when(i == 0)
    def _():
        wb_ref[...] = w_ref[...].astype(_BF)

    mix = jnp.concatenate([_pick_half(i, ROW_TILE, branch_refs[2 * n], branch_refs[2 * n + 1])
                           for n in range(3)], axis=-1)
    x = _pick_half(i, ROW_TILE, *x_refs) if split else x_refs[0][...]
    x_new = x + mod_ref[2:3, :] * jnp.dot(mix, wb_ref[...], preferred_element_type=_F32)
    xo_ref[...] = x_new
    h = _norm_mod(x_new, gf_ref[...], mod_ref[3:4, :], mod_ref[4:5, :])
    h_ref[...] = h.astype(h_ref.dtype)
    if route:
        _route_top2(r_ref, h, ei_ref, ew_ref)


def _outproj(branches, xs, l, mods, w_all, gf_all, h_dtype, router_t=None):
    split = len(xs) == 2
    route = router_t is not None
    row = lambda i: (i, 0)
    in_specs = []
    args = []
    for pair, width in zip(branches, (ATTN_W, CONV_W, FNET_W)):
        in_specs += _half_specs(width, ROW_TILE)
        args += list(pair)
    in_specs += _half_specs(D_MODEL, ROW_TILE) if split else [pl.BlockSpec((ROW_TILE, D_MODEL), row)]
    args += list(xs)
    mod_spec, gf_spec, w_spec = _layer_specs(l, (D_MODEL, D_MODEL))
    in_specs += [mod_spec, w_spec, gf_spec]
    args += [mods, w_all, gf_all.reshape(DEPTH, 1, D_MODEL)]
    out_specs = [pl.BlockSpec((ROW_TILE, D_MODEL), row)] * 2
    out_shape = [jax.ShapeDtypeStruct((T, D_MODEL), _F32), jax.ShapeDtypeStruct((T, D_MODEL), h_dtype)]
    if route:
        in_specs.append(pl.BlockSpec((N_EXPERTS, D_MODEL), lambda i: (0, 0)))
        args.append(router_t)
        out_specs += [pl.BlockSpec((TOP_K, ROW_TILE), lambda i: (0, i))] * 2
        out_shape += [jax.ShapeDtypeStruct((TOP_K, T), jnp.int32), jax.ShapeDtypeStruct((TOP_K, T), _F32)]
    return pl.pallas_call(
        functools.partial(_outproj_kernel, split, route),
        grid=(T // ROW_TILE,),
        in_specs=in_specs,
        out_specs=out_specs,
        out_shape=out_shape,
        scratch_shapes=[pltpu.VMEM((D_MODEL, D_MODEL), _BF)],
        compiler_params=_params(("arbitrary",)),
        name="outproj_route" if route else "outproj",
    )(*args)


def _ffn_kernel(nj, n_sub_rows, vis_e, vis_start, vis_cnt, used_sub,
                x_hbm, wg_ref, wu_ref, wd_ref, y_hbm,
                big, xb, wgb, wub, wdb, sem_in, sem_out):
    del vis_e
    v = pl.program_id(0)
    j = pl.program_id(1)
    cnt = vis_cnt[v]
    row0 = vis_start[v] * SUB

    x_is_bf16 = x_hbm.dtype == _BF
    landing = xb if x_is_bf16 else big

    def copy_in(s):
        r = pl.multiple_of(row0 + s * SUB, SUB)
        b = pl.multiple_of(s * SUB, SUB)
        return pltpu.make_async_copy(x_hbm.at[pl.ds(r, SUB)], landing.at[pl.ds(b, SUB)], sem_in.at[s])

    def copy_out(s):
        r = pl.multiple_of(row0 + s * SUB, SUB)
        b = pl.multiple_of(s * SUB, SUB)
        return pltpu.make_async_copy(big.at[pl.ds(b, SUB)], y_hbm.at[pl.ds(r, SUB)], sem_out)

    def for_subs(fn):
        lax.fori_loop(0, cnt, lambda s, c: (fn(s), c)[1], 0)

    @pl.when(jnp.logical_and(j == 0, cnt > 0))
    def _():
        for_subs(lambda s: copy_in(s).start())

    @pl.when(cnt > 0)
    def _():
        wgb[...] = wg_ref[...].astype(_BF)
        wub[...] = wu_ref[...].astype(_BF)
        wdb[...] = wd_ref[...].astype(_BF)

        def block(r0, nrows):
            s0 = r0 // SUB

            @pl.when(j == 0)
            def _():
                for t in range(nrows // SUB):
                    copy_in(s0 + t).wait()
                    rs = pl.ds(pl.multiple_of(r0 + t * SUB, SUB), SUB)
                    if not x_is_bf16:
                        xb[rs, :] = big[rs, :].astype(_BF)
                    big[rs, :] = jnp.zeros((SUB, D_MODEL), _F32)

            rs = pl.ds(r0, nrows)
            x = xb[rs, :]
            a = jnp.dot(x, wgb[...], preferred_element_type=_F32)
            b = jnp.dot(x, wub[...], preferred_element_type=_F32)
            p = (a * _sigmoid(a) * b).astype(_BF)
            big[rs, :] += jnp.dot(p, wdb[...], preferred_element_type=_F32)

            @pl.when(j == nj - 1)
            def _():
                for t in range(nrows // SUB):
                    copy_out(s0 + t).start()

        blk = BLK_SUBS * SUB
        nfull = cnt // BLK_SUBS
        lax.fori_loop(0, nfull, lambda i, c: (block(pl.multiple_of(i * blk, blk), blk), c)[1], 0)
        rem = cnt % BLK_SUBS
        base = nfull * blk
        bit = BLK_SUBS // 2
        while bit >= 1:
            take = (rem // bit) % 2 == 1
            pl.when(take)(functools.partial(block, pl.multiple_of(base, SUB), bit * SUB))
            base = base + jnp.where(take, bit * SUB, 0)
            bit //= 2

    @pl.when(jnp.logical_and(j == nj - 1, cnt > 0))
    def _():
        for_subs(lambda s: copy_out(s).wait())

    if x_hbm.dtype == y_hbm.dtype:
        @pl.when(jnp.logical_and(v == pl.num_programs(0) - 1, j == nj - 1))
        def _():
            def tail(s):
                r = pl.multiple_of(s * SUB, SUB)
                return pltpu.make_async_copy(x_hbm.at[pl.ds(r, SUB)], y_hbm.at[pl.ds(r, SUB)], sem_out)
            lax.fori_loop(used_sub[0], n_sub_rows, lambda s, c: (tail(s).start(), c)[1], 0)
            lax.fori_loop(used_sub[0], n_sub_rows, lambda s, c: (tail(s).wait(), c)[1], 0)


def _grouped_ffn(x, wg, wu, wd, fc, vis_e, vis_start, vis_cnt, used_sub):
    ff = wg.shape[-1]
    nj = ff // fc
    nv = vis_e.shape[0]
    n_rows = x.shape[0]

    def chunk(v, j, vc):
        return jnp.where(vc[v] > 0, j, nj - 1)

    return pl.pallas_call(
        functools.partial(_ffn_kernel, nj, n_rows // SUB),
        grid_spec=pltpu.PrefetchScalarGridSpec(
            num_scalar_prefetch=4,
            grid=(nv, nj),
            in_specs=[
                pl.BlockSpec(memory_space=pl.ANY),
                pl.BlockSpec((None, D_MODEL, fc), lambda v, j, ve, vs, vc, us: (ve[v], 0, chunk(v, j, vc))),
                pl.BlockSpec((None, D_MODEL, fc), lambda v, j, ve, vs, vc, us: (ve[v], 0, chunk(v, j, vc))),
                pl.BlockSpec((None, fc, D_MODEL), lambda v, j, ve, vs, vc, us: (ve[v], chunk(v, j, vc), 0)),
            ],
            out_specs=pl.BlockSpec(memory_space=pl.ANY),
            scratch_shapes=[
                pltpu.VMEM((TMAX, D_MODEL), _F32),
                pltpu.VMEM((TMAX, D_MODEL), _BF),
                pltpu.VMEM((D_MODEL, fc), _BF),
                pltpu.VMEM((D_MODEL, fc), _BF),
                pltpu.VMEM((fc, D_MODEL), _BF),
                pltpu.SemaphoreType.DMA((SUB_MAX,)),
                pltpu.SemaphoreType.DMA(()),
            ],
        ),
        out_shape=jax.ShapeDtypeStruct((n_rows, D_MODEL), _F32),
        compiler_params=_params(("arbitrary", "arbitrary"), FFN_VMEM_LIMIT),
        name="ffn_%d" % ff,
    )(vis_e, vis_start, vis_cnt, used_sub, x, wg, wu, wd)


def _dense_visits():
    nv = T // TMAX
    return (jnp.zeros((nv,), jnp.int32),
            jnp.arange(nv, dtype=jnp.int32) * SUB_MAX,
            jnp.full((nv,), SUB_MAX, jnp.int32),
            jnp.full((1,), T // SUB, jnp.int32))


N_SLOT_SUB = -(-(TOP_K * T + N_TILES * N_EXPERTS * (SEG_ALIGN - 1)) // SUB) + N_EXPERTS
N_SLOT = N_SLOT_SUB * SUB
N_VISIT = -(-N_SLOT_SUB // SUB_MAX) + N_EXPERTS


def _routing_plan(eidx):
    e_loc = eidx.reshape(TOP_K, N_TILES, DISP_TILE).transpose(1, 0, 2).reshape(N_TILES, PAIRS)
    onehot = e_loc[:, :, None] == jnp.arange(N_EXPERTS, dtype=jnp.int32)[None, None, :]
    ch = 128
    oh = onehot.astype(_F32).reshape(N_TILES, PAIRS // ch, ch, N_EXPERTS)
    tri = (jnp.arange(ch)[:, None] >= jnp.arange(ch)[None, :]).astype(_F32)
    within = jnp.einsum("ij,tcjk->tcik", tri, oh)
    tot = within[:, :, -1, :]
    csum = (within + (jnp.cumsum(tot, axis=1) - tot)[:, :, None, :]).reshape(N_TILES, PAIRS, N_EXPERTS)
    csum = csum.astype(jnp.int32)
    n_te = (csum[:, -1, :] + SEG_ALIGN - 1) // SEG_ALIGN * SEG_ALIGN
    src = jnp.cumsum(n_te, axis=1) - n_te
    lpos = jnp.sum(jnp.where(onehot, csum - 1 + src[:, None, :], 0), axis=2).astype(jnp.int32)
    counts = jnp.sum(n_te, axis=0)
    nsub = (counts + SUB - 1) // SUB
    sub_base = jnp.cumsum(nsub) - nsub
    dst = (sub_base * SUB)[None, :] + jnp.cumsum(n_te, axis=0) - n_te
    seg = tuple(a.reshape(-1).astype(jnp.int32) for a in (n_te, src, dst))
    pads = ((sub_base * SUB + counts).astype(jnp.int32), (nsub * SUB - counts).astype(jnp.int32))
    used_sub = jnp.sum(nsub).reshape(1).astype(jnp.int32)
    nvis = (nsub + SUB_MAX - 1) // SUB_MAX
    vend = jnp.cumsum(nvis)
    total = vend[-1]
    vid = jnp.arange(N_VISIT, dtype=jnp.int32)
    ve = jnp.minimum(jnp.sum((vid[:, None] >= vend[None, :]).astype(jnp.int32), axis=1), N_EXPERTS - 1)
    local = vid - (vend - nvis)[ve]
    nv_e = jnp.maximum(nvis[ve], 1)
    q, r = nsub[ve] // nv_e, nsub[ve] % nv_e
    cnt = q + (local < r).astype(jnp.int32)
    start = sub_base[ve] + local * q + jnp.minimum(local, r)
    used = vid < total
    last_e = ve[jnp.maximum(total - 1, 0)]
    vis_e = jnp.where(used, ve, last_e).astype(jnp.int32)
    vis_cnt = jnp.where(used, cnt, 0).astype(jnp.int32)
    vis_start = jnp.where(used, start, 0).astype(jnp.int32)
    return lpos.reshape(N_TILES, TOP_K, DISP_TILE), seg, pads, (vis_e, vis_start, vis_cnt, used_sub)


def _for_pow2_pieces(n, max_piece, fn):
    off = 0
    bit = max_piece
    while bit >= SEG_ALIGN:
        take = (n // bit) % 2 == 1
        pl.when(take)(functools.partial(fn, off, bit))
        off = off + jnp.where(take, bit, 0)
        bit //= 2


def _segment_copies(tile, seg, make_copy, wait=False):
    n_te, src, dst = seg
    for e in range(N_EXPERTS):
        j = tile * N_EXPERTS + e
        s0, d0 = src[j], dst[j]

        def piece(off, size, s0=s0, d0=d0):
            cp = make_copy(pl.multiple_of(s0 + off, SEG_ALIGN), pl.multiple_of(d0 + off, SEG_ALIGN), size)
            cp.wait() if wait else cp.start()
        _for_pow2_pieces(n_te[j], DISP_TILE, piece)


def _dispatch_kernel(n_te, src, dst, pad_start, pad_cnt, used_sub, h_ref, lpos_ref, xs_hbm, ring, ring_sems, sem):
    i = pl.program_id(0)
    last = pl.num_programs(0) - 1
    b = i % 2

    def copies(tile, slot_id, wait):
        _segment_copies(tile, (n_te, src, dst), lambda s, d, size: pltpu.make_async_copy(
            ring.at[slot_id, pl.ds(s, size)], xs_hbm.at[pl.ds(d, size)], ring_sems.at[slot_id]), wait)

    pl.when(i >= 2)(lambda: copies(i - 2, b, True))
    lp = lpos_ref[...]
    srow = lax.broadcasted_iota(jnp.int32, (SORT_ROWS, DISP_TILE), 0)
    perm = jnp.where(lp[0:1, :] == srow, 1.0, jnp.where(lp[1:2, :] == srow, 1.0, 0.0)).astype(_BF)
    ring[b] = jnp.dot(perm, h_ref[...].astype(_BF), preferred_element_type=_F32)
    copies(i, b, False)

    @pl.when(i == last)
    def _():
        copies(i - 1, 1 - b, True)
        copies(i, b, True)
        for e in range(N_EXPERTS):
            p0 = pad_start[e]

            def fill(off, size, p0=p0):
                cp = pltpu.make_async_copy(ring.at[b, pl.ds(0, size)],
                                           xs_hbm.at[pl.ds(pl.multiple_of(p0 + off, SEG_ALIGN), size)], sem)
                cp.start()
                cp.wait()
            _for_pow2_pieces(pad_cnt[e], SUB // 2, fill)

        def tail(s):
            r = pl.multiple_of(s * SUB, SUB)
            return pltpu.make_async_copy(ring.at[b, pl.ds(0, SUB)], xs_hbm.at[pl.ds(r, SUB)], sem)
        lax.fori_loop(used_sub[0], N_SLOT_SUB, lambda s, c: (tail(s).start(), c)[1], 0)
        lax.fori_loop(used_sub[0], N_SLOT_SUB, lambda s, c: (tail(s).wait(), c)[1], 0)


def _dispatch(h, lpos, seg, pads, used_sub):
    return pl.pallas_call(
        _dispatch_kernel,
        grid_spec=pltpu.PrefetchScalarGridSpec(
            num_scalar_prefetch=6,
            grid=(N_TILES,),
            in_specs=[pl.BlockSpec((DISP_TILE, D_MODEL), lambda i, *_: (i, 0)),
                      pl.BlockSpec((None, TOP_K, DISP_TILE), lambda i, *_: (i, 0, 0))],
            out_specs=pl.BlockSpec(memory_space=pl.ANY),
            scratch_shapes=[pltpu.VMEM((2, SORT_ROWS, D_MODEL), _F32),
                            pltpu.SemaphoreType.DMA((2,)), pltpu.SemaphoreType.DMA(())],
        ),
        out_shape=jax.ShapeDtypeStruct((N_SLOT, D_MODEL), _F32),
        compiler_params=_params(("arbitrary",)),
        name="dispatch",
    )(*seg, *pads, used_sub, h, lpos)


def _residual_out(x, gate, f, g_ref, o_refs, i, tile):
    out = x + gate * f
    if g_ref is not None:
        ms = jnp.mean(out * out, axis=-1, keepdims=True)
        out = out * lax.rsqrt(ms + EPS) * g_ref[...]
    if len(o_refs) == 1:
        o_refs[0][...] = out
    else:
        @pl.when(i < T_PROMPT // tile)
        def _():
            o_refs[0][...] = out

        @pl.when(i >= T_PROMPT // tile)
        def _():
            o_refs[1][...] = out


def _out_specs(final, tile):
    if not final:
        return ([pl.BlockSpec((tile, D_MODEL), lambda i, *_: (i, 0))],
                [jax.ShapeDtypeStruct((T, D_MODEL), _F32)])
    return (_half_specs(D_MODEL, tile),
            [jax.ShapeDtypeStruct((T_PROMPT, D_MODEL), _F32), jax.ShapeDtypeStruct((T_SAMPLE, D_MODEL), _F32)])


def _combine_kernel(final, x_ref, mod_ref, y_ref, *rest):
    g_ref, o_refs = (rest[0], rest[1:]) if final else (None, rest)
    _residual_out(x_ref[...], mod_ref[5:6, :], y_ref[...], g_ref, o_refs, pl.program_id(0), ROW_TILE)


def _combine(x, l, mods, y, g_final=None):
    final = g_final is not None
    row = lambda i: (i, 0)
    in_specs = [
        pl.BlockSpec((ROW_TILE, D_MODEL), row),
        _layer_specs(l, ())[0],
        pl.BlockSpec((ROW_TILE, D_MODEL), row),
    ]
    args = [x, mods, y]
    if final:
        in_specs.append(pl.BlockSpec((1, D_MODEL), lambda i: (0, 0)))
        args.append(g_final.reshape(1, D_MODEL))
    out_specs, out_shape = _out_specs(final, ROW_TILE)
    return pl.pallas_call(
        functools.partial(_combine_kernel, final),
        grid=(T // ROW_TILE,),
        in_specs=in_specs,
        out_specs=out_specs,
        out_shape=out_shape,
        compiler_params=_params(("arbitrary",)),
        name="combine%s" % ("_final" if final else ""),
    )(*args)


def _combine_top2_kernel(final, n_out, n_te, src, dst, x_ref, mod_ref, w_ref, lpos_ref, *rest):
    g_ref, rest = (rest[0], rest[1:]) if final else (None, rest)
    ys_hbm, o_refs, (ybuf, sems) = rest[0], rest[1:1 + n_out], rest[1 + n_out:]
    i = pl.program_id(0)

    def fetch(tile, b, wait=False):
        _segment_copies(tile, (n_te, src, dst), lambda s, d, size: pltpu.make_async_copy(
            ys_hbm.at[pl.ds(d, size)], ybuf.at[b, pl.ds(s, size)], sems.at[b]), wait)

    @pl.when(i == 0)
    def _():
        for slot_id in range(2):
            ybuf[slot_id, PAIRS:SORT_ROWS, :] = jnp.zeros((SORT_ROWS - PAIRS, D_MODEL), _F32)
        fetch(0, 0)

    @pl.when(i + 1 < pl.num_programs(0))
    def _():
        fetch(i + 1, (i + 1) % 2)

    b = i % 2
    fetch(i, b, wait=True)
    y = ybuf[b].astype(_BF)
    lp = lpos_ref[...]
    scol = lax.broadcasted_iota(jnp.int32, (DISP_TILE, SORT_ROWS), 1)
    w = w_ref[...]
    f = None
    for k in range(TOP_K):
        pick = jnp.where(lp[:, k:k + 1] == scol, 1.0, 0.0).astype(_BF)
        term = w[:, k:k + 1] * jnp.dot(pick, y, preferred_element_type=_F32)
        f = term if f is None else f + term
    _residual_out(x_ref[...], mod_ref[5:6, :], f, g_ref, o_refs, i, DISP_TILE)


def _combine_top2(x, l, mods, ys, lpos_t, seg, w, g_final=None):
    final = g_final is not None
    row = lambda i, *_: (i, 0)
    in_specs = [
        pl.BlockSpec((DISP_TILE, D_MODEL), row),
        pl.BlockSpec((None, None, 6, D_MODEL), lambda i, *_: (l, _cond_of_tile(i, DISP_TILE), 0, 0)),
        pl.BlockSpec((DISP_TILE, TOP_K), row),
        pl.BlockSpec((None, DISP_TILE, TOP_K), lambda i, *_: (i, 0, 0)),
    ]
    args = [x, mods, w, lpos_t]
    if final:
        in_specs.append(pl.BlockSpec((1, D_MODEL), lambda i, *_: (0, 0)))
        args.append(g_final.reshape(1, D_MODEL))
    in_specs.append(pl.BlockSpec(memory_space=pl.ANY))
    args.append(ys)
    out_specs, out_shape = _out_specs(final, DISP_TILE)
    return pl.pallas_call(
        functools.partial(_combine_top2_kernel, final, len(out_specs)),
        grid_spec=pltpu.PrefetchScalarGridSpec(
            num_scalar_prefetch=3,
            grid=(N_TILES,),
            in_specs=in_specs,
            out_specs=out_specs,
            scratch_shapes=[
                pltpu.VMEM((2, SORT_ROWS, D_MODEL), _F32),
                pltpu.SemaphoreType.DMA((2,)),
            ],
        ),
        out_shape=out_shape,
        compiler_params=_params(("arbitrary",)),
        name="combine_top2%s" % ("_final" if final else ""),
    )(*seg, *args)


def kernel(x_prompt, x_sample, cache_k, cache_v, c, c_ctx, w_ada, b_ada, g_norm_mix, g_norm_ffn,
           w_in, w_out, attn_sink, conv_dw, conv_dw_b, conv_ln_g, conv_ln_b, conv_pw, fnet_w,
           ffn_w_gate, ffn_w_up, ffn_w_down, moe_router, moe_w_gate, moe_w_up, moe_w_down, g_final):
    xs = (x_prompt.reshape(T_PROMPT, D_MODEL), x_sample.reshape(T_SAMPLE, D_MODEL))
    cond8 = jnp.concatenate([c_ctx[None, :], c, jnp.zeros((N_COND - 1 - DEC_BATCH, D_MODEL), _F32)], axis=0)
    mods = _modulation(cond8, w_ada, b_ada).reshape(DEPTH, N_COND, 6, D_MODEL)
    cos_t, sin_t = _rope_tables()
    ck_all = cache_k.reshape(DEC_BATCH, DEPTH, PAST_LEN, KV_W)
    cv_all = cache_v.reshape(DEC_BATCH, DEPTH, PAST_LEN, KV_W)
    p_blocks = T_PROMPT // DEC_SEQ
    sinks = attn_sink.reshape(DEPTH * N_HEADS)

    ks, vs = [], []
    resid = None
    for l in range(DEPTH):
        q, k, v, uc, uf, *x_new = _inproj(xs, l, mods, g_norm_mix, w_in, cos_t, sin_t, resid=resid)
        if resid is not None:
            xs, resid = tuple(x_new), None
        ks.append(k[:T_PROMPT])
        vs.append(v[:T_PROMPT])
        attn = (_ctx_attention(l, sinks, q, k, v),
                _lat_attention(l, sinks, q, k, v, ck_all, cv_all))
        cargs = (l, conv_dw, conv_dw_b, conv_ln_g, conv_ln_b, conv_pw)
        conv = (_conv_module(uc, SEQ, BATCH, 0, *cargs),
                _conv_module(uc, DEC_SEQ, DEC_BATCH, p_blocks, *cargs))
        four = (_fourier_mix(uf, SEQ, BATCH, 0, l, fnet_w),
                _fourier_mix(uf, DEC_SEQ, DEC_BATCH, p_blocks, l, fnet_w))
        last = g_final if l == DEPTH - 1 else None
        i = l // 2
        if l % 2 == 0:
            x, h = _outproj((attn, conv, four), xs, l, mods, w_out, g_norm_ffn, _BF)
            y = _grouped_ffn(h, ffn_w_gate[i:i + 1], ffn_w_up[i:i + 1], ffn_w_down[i:i + 1], 256,
                             *_dense_visits())
            if last is None:
                xs, resid = (x,), (y, l)
            else:
                xs = tuple(_combine(x, l, mods, y, g_final=last))
        else:
            x, h, eidx, ew = _outproj((attn, conv, four), xs, l, mods, w_out, g_norm_ffn, _F32,
                                      router_t=moe_router[i].T)
            lpos, seg, pads, visits = _routing_plan(eidx)
            xd = _dispatch(h, lpos, seg, pads, visits[-1])
            ys = _grouped_ffn(xd, moe_w_gate[i], moe_w_up[i], moe_w_down[i], 512, *visits)
            xs = tuple(_combine_top2(x, l, mods, ys, lpos.transpose(0, 2, 1), seg, ew.T, g_final=last))

    y_prompt = xs[0].reshape(BATCH, SEQ, D_MODEL)
    y_sample = xs[1].reshape(DEC_BATCH, DEC_SEQ, D_MODEL)
    state_k = jnp.stack([a.reshape(BATCH, SEQ, N_KV_HEADS, HEAD_DIM) for a in ks], axis=1)
    state_v = jnp.stack([a.reshape(BATCH, SEQ, N_KV_HEADS, HEAD_DIM) for a in vs], axis=1)
    return (y_prompt, y_sample, state_k, state_v)
```

```python
import functools

import numpy as np
import jax
import jax.numpy as jnp
from jax import lax
from jax.experimental import pallas as pl
from jax.experimental.pallas import tpu as pltpu

D_MODEL = 1024
BATCH = 16
SEQ = 256
DEPTH = 2
DEC_BATCH = 2
DEC_SEQ = 2048
PAST_LEN = 512
GRID_W = 64
HEAD_DIM = 64
N_HEADS = 8
N_KV_HEADS = 2
GQA_GROUP = N_HEADS // N_KV_HEADS
ATTN_W = N_HEADS * HEAD_DIM
KV_W = N_KV_HEADS * HEAD_DIM
WINDOW = 128
BLOCK = 128
ROPE_THETA = 10000.0
CONV_W = D_MODEL // 4
CONV_K = 31
FNET_GROUPS = 4
FNET_W = D_MODEL // 4
FNET_GW = FNET_W // FNET_GROUPS
IN_WIDTH = ATTN_W + 2 * KV_W + 2 * CONV_W + FNET_W
D_FF = 2816
N_EXPERTS = 8
TOP_K = 2
D_FF_EXPERT = 3584
EPS = 1e-6
NEG = -1e30

T_PROMPT = BATCH * SEQ
T_SAMPLE = DEC_BATCH * DEC_SEQ
T = T_PROMPT + T_SAMPLE
N_COND = 8

ROW_TILE = 512
SUB = 256
SUB_MAX = 16
TMAX = SUB * SUB_MAX
BLK_SUBS = 5
DISP_TILE = 512
N_TILES = T // DISP_TILE
PAIRS = TOP_K * DISP_TILE
SEG_ALIGN = 8
SORT_ROWS = 1152
CONV_CHUNK = 64
CONV_PAD = 16
CONV_SPAN = CONV_CHUNK + 8 * ((CONV_PAD + CONV_K // 2) // 8)
VMEM_LIMIT = 48 * 1024 * 1024
FFN_VMEM_LIMIT = 56 * 1024 * 1024

_BF = jnp.bfloat16
_F32 = jnp.float32


def _cond_of_tile(i, tile):
    r = i * tile
    return jnp.where(r < T_PROMPT, 0, 1 + (r - T_PROMPT) // DEC_SEQ)


def _params(sem, vmem=VMEM_LIMIT):
    return pltpu.CompilerParams(dimension_semantics=sem, vmem_limit_bytes=vmem)


def _sigmoid(x):
    return 1.0 / (1.0 + jnp.exp(-x))


def _mod_kernel(cond_ref, w_ref, b_ref, o_ref):
    cnd = cond_ref[...]
    s = (cnd * _sigmoid(cnd)).astype(_BF)
    o_ref[...] = jnp.dot(s, w_ref[...].astype(_BF), preferred_element_type=_F32) + b_ref[...]


def _modulation(cond8, w_ada, b_ada):
    nt = 1536
    return pl.pallas_call(
        _mod_kernel,
        grid=(DEPTH, 6 * D_MODEL // nt),
        in_specs=[
            pl.BlockSpec((N_COND, D_MODEL), lambda l, n: (0, 0)),
            pl.BlockSpec((None, D_MODEL, nt), lambda l, n: (l, 0, n)),
            pl.BlockSpec((None, 1, nt), lambda l, n: (l, 0, n)),
        ],
        out_specs=pl.BlockSpec((None, N_COND, nt), lambda l, n: (l, 0, n)),
        out_shape=jax.ShapeDtypeStruct((DEPTH, N_COND, 6 * D_MODEL), _F32),
        compiler_params=_params(("arbitrary", "arbitrary")),
        name="modulation",
    )(cond8, w_ada, b_ada.reshape(DEPTH, 1, 6 * D_MODEL))


def _norm_mod(x, g, shift, scale):
    ms = jnp.mean(x * x, axis=-1, keepdims=True)
    y = x * lax.rsqrt(ms + EPS) * g
    return y * (1.0 + scale) + shift


def _rope_tables():
    rows = DEC_SEQ // GRID_W
    n_freq = HEAD_DIM // 4
    inv = ROPE_THETA ** (-jnp.arange(n_freq, dtype=_F32) / n_freq)
    gr, gc = jnp.meshgrid(jnp.arange(rows, dtype=_F32), jnp.arange(GRID_W, dtype=_F32), indexing="ij")
    ang_r = gr.reshape(-1)[:, None] * inv
    ang_c = gc.reshape(-1)[:, None] * inv
    cr, sr, cc, sc = jnp.cos(ang_r), jnp.sin(ang_r), jnp.cos(ang_c), jnp.sin(ang_c)
    cos64 = jnp.concatenate([cr, cr, cc, cc], axis=-1)
    sin64 = jnp.concatenate([-sr, sr, -sc, sc], axis=-1)
    return jnp.tile(cos64, (1, 2)), jnp.tile(sin64, (1, 2))


def _rope128(x, cos, sin):
    lane = lax.broadcasted_iota(jnp.int32, x.shape, 1)
    first = (lane % 32) < 16
    partner = jnp.where(first, pltpu.roll(x, 128 - 16, 1), pltpu.roll(x, 16, 1))
    return x * cos + partner * sin


def _half_specs(width, tile):
    pt = T_PROMPT // tile
    return [pl.BlockSpec((tile, width), lambda i, *_: (jnp.minimum(i, pt - 1), 0)),
            pl.BlockSpec((tile, width), lambda i, *_: (jnp.maximum(i - pt, 0), 0))]


def _pick_half(i, tile, a_ref, b_ref):
    return jnp.where(i < T_PROMPT // tile, a_ref[...], b_ref[...])


def _layer_specs(l, w_shape):
    return [
        pl.BlockSpec((None, None, 6, D_MODEL), lambda i, *_: (l, _cond_of_tile(i, ROW_TILE), 0, 0)),
        pl.BlockSpec((None, 1, D_MODEL), lambda i, *_: (l, 0, 0)),
        pl.BlockSpec((None,) + w_shape, lambda i, *_: (l, 0, 0)),
    ]


def _inproj_kernel(split, resid, *refs):
    n_x = 2 if split else 1
    x_refs, refs = refs[:n_x], refs[n_x:]
    if resid:
        y_ref, modp_ref = refs[:2]
        refs = refs[2:]
    mod_ref, g_ref, w_ref, cos_ref, sin_ref, q_ref, k_ref, v_ref, uc_ref, uf_ref = refs[:10]
    wb_ref = refs[-1]
    i = pl.program_id(0)

    @pl.when(i == 0)
    def _():
        wb_ref[...] = w_ref[...].astype(_BF)

    x = _pick_half(i, ROW_TILE, *x_refs) if split else x_refs[0][...]
    if resid:
        x = x + modp_ref[5:6, :] * y_ref[...]
        refs[10][...] = x
    h = _norm_mod(x, g_ref[...], mod_ref[0:1, :], mod_ref[1:2, :]).astype(_BF)
    latent = i >= T_PROMPT // ROW_TILE
    cos = jnp.where(latent, cos_ref[...], 1.0)
    sin = jnp.where(latent, sin_ref[...], 0.0)

    def proj(o, n):
        return jnp.dot(h, wb_ref[:, o:o + n], preferred_element_type=_F32)

    o = 0
    for c in range(ATTN_W // 256):
        qq = proj(o, 256)
        q_ref[:, o:o + 128] = _rope128(qq[:, :128], cos, sin)
        q_ref[:, o + 128:o + 256] = _rope128(qq[:, 128:], cos, sin)
        o += 256
    kv = proj(o, 2 * KV_W)
    k_ref[...] = _rope128(kv[:, :KV_W], cos, sin)
    v_ref[...] = kv[:, KV_W:]
    o += 2 * KV_W
    uc_ref[...] = proj(o, 2 * CONV_W)
    o += 2 * CONV_W
    uf_ref[...] = proj(o, FNET_W)


def _inproj(xs, l, mods, g_all, w_all, cos_t, sin_t, resid=None):
    split = len(xs) == 2
    pt = T_PROMPT // ROW_TILE
    per_seq = DEC_SEQ // ROW_TILE

    def rope_idx(i):
        return (jnp.maximum(i - pt, 0) % per_seq, 0)

    row = lambda i: (i, 0)
    in_specs = _half_specs(D_MODEL, ROW_TILE) if split else [pl.BlockSpec((ROW_TILE, D_MODEL), row)]
    args = list(xs)
    outs = [ATTN_W, KV_W, KV_W, 2 * CONV_W, FNET_W]
    if resid is not None:
        y, l_prev = resid
        in_specs += [pl.BlockSpec((ROW_TILE, D_MODEL), row), _layer_specs(l_prev, ())[0]]
        args += [y, mods]
        outs.append(D_MODEL)
    return pl.pallas_call(
        functools.partial(_inproj_kernel, split, resid is not None),
        grid=(T // ROW_TILE,),
        in_specs=in_specs + _layer_specs(l, (D_MODEL, IN_WIDTH)) + [
            pl.BlockSpec((ROW_TILE, 128), rope_idx),
            pl.BlockSpec((ROW_TILE, 128), rope_idx),
        ],
        out_specs=[pl.BlockSpec((ROW_TILE, n), row) for n in outs],
        out_shape=[jax.ShapeDtypeStruct((T, n), _F32) for n in outs],
        scratch_shapes=[pltpu.VMEM((D_MODEL, IN_WIDTH), _BF)],
        compiler_params=_params(("arbitrary",)),
        name="inproj_resid" if resid is not None else "inproj",
    )(*args, mods, g_all.reshape(DEPTH, 1, D_MODEL), w_all, cos_t, sin_t)


def _stack_groups(q, kh, rows):
    parts = [q[:, (kh * GQA_GROUP + g) * HEAD_DIM:(kh * GQA_GROUP + g + 1) * HEAD_DIM] for g in range(GQA_GROUP)]
    return jnp.concatenate(parts, axis=0)


def _sink_column(sink_ref, head0, rows):
    r = lax.broadcasted_iota(jnp.int32, (GQA_GROUP * rows, 1), 0)
    col = jnp.full((GQA_GROUP * rows, 1), sink_ref[head0], _F32)
    for g in range(1, GQA_GROUP):
        col = jnp.where(r >= g * rows, sink_ref[head0 + g], col)
    return col


def _unstack_store(o_ref, o, kh, rows):
    for pair in range(GQA_GROUP // 2):
        a = o[(2 * pair) * rows:(2 * pair + 1) * rows]
        b = o[(2 * pair + 1) * rows:(2 * pair + 2) * rows]
        c0 = (kh * GQA_GROUP + 2 * pair) * HEAD_DIM
        o_ref[:, c0:c0 + 2 * HEAD_DIM] = jnp.concatenate([a, b], axis=-1).astype(o_ref.dtype)


_NT = (((1,), (1,)), ((), ()))


def _ctx_attn_kernel(l, sink_ref, q_ref, k_ref, v_ref, o_ref):
    scale = HEAD_DIM ** -0.5
    q = q_ref[...] * scale
    for kh in range(N_KV_HEADS):
        kk = k_ref[:, kh * HEAD_DIM:(kh + 1) * HEAD_DIM].astype(_BF)
        vv = v_ref[:, kh * HEAD_DIM:(kh + 1) * HEAD_DIM].astype(_BF)
        qs = _stack_groups(q, kh, SEQ).astype(_BF)
        s = lax.dot_general(qs, kk, _NT, preferred_element_type=_F32)
        sink = _sink_column(sink_ref, l * N_HEADS + kh * GQA_GROUP, SEQ)
        m = jnp.maximum(jnp.max(s, axis=-1, keepdims=True), sink)
        p = jnp.exp(s - m)
        den = jnp.sum(p, axis=-1, keepdims=True) + jnp.exp(sink - m)
        o = jnp.dot(p.astype(_BF), vv, preferred_element_type=_F32) / den
        _unstack_store(o_ref, o, kh, SEQ)


def _ctx_attention(l, sinks, q, k, v):
    row = lambda b, s: (b, 0)
    return pl.pallas_call(
        functools.partial(_ctx_attn_kernel, l),
        grid_spec=pltpu.PrefetchScalarGridSpec(
            num_scalar_prefetch=1,
            grid=(BATCH,),
            in_specs=[
                pl.BlockSpec((SEQ, ATTN_W), row),
                pl.BlockSpec((SEQ, KV_W), row),
                pl.BlockSpec((SEQ, KV_W), row),
            ],
            out_specs=pl.BlockSpec((SEQ, ATTN_W), row),
        ),
        out_shape=jax.ShapeDtypeStruct((T_PROMPT, ATTN_W), _BF),
        compiler_params=_params(("arbitrary",)),
        name="ctx_attention",
    )(sinks, q, k, v)


def _lat_attn_kernel(l, sink_ref, q_ref, k_ref, v_ref, ck_ref, cv_ref, o_ref):
    n = pl.program_id(1)
    scale = HEAD_DIM ** -0.5
    nb = DEC_SEQ // BLOCK
    band = 3 * BLOCK
    start = pl.multiple_of(jnp.clip(n - 1, 0, nb - 3) * BLOCK, BLOCK)
    q = q_ref[...] * scale
    kb = k_ref[pl.ds(start, band), :]
    vb = v_ref[pl.ds(start, band), :]
    rows = GQA_GROUP * BLOCK
    qpos = n * BLOCK + lax.broadcasted_iota(jnp.int32, (rows, band), 0) % BLOCK
    kpos = start + lax.broadcasted_iota(jnp.int32, (rows, band), 1)
    valid = jnp.abs(kpos - qpos) <= WINDOW
    for kh in range(N_KV_HEADS):
        hs = slice(kh * HEAD_DIM, (kh + 1) * HEAD_DIM)
        qs = _stack_groups(q, kh, BLOCK).astype(_BF)
        s_loc = lax.dot_general(qs, kb[:, hs].astype(_BF), _NT, preferred_element_type=_F32)
        s_loc = jnp.where(valid, s_loc, NEG)
        s_ctx = lax.dot_general(qs, ck_ref[:, hs].astype(_BF), _NT, preferred_element_type=_F32)
        sink = _sink_column(sink_ref, l * N_HEADS + kh * GQA_GROUP, BLOCK)
        m = jnp.maximum(jnp.maximum(jnp.max(s_loc, axis=-1, keepdims=True),
                                    jnp.max(s_ctx, axis=-1, keepdims=True)), sink)
        p_loc = jnp.exp(s_loc - m)
        p_ctx = jnp.exp(s_ctx - m)
        den = (jnp.sum(p_loc, axis=-1, keepdims=True) + jnp.sum(p_ctx, axis=-1, keepdims=True)
               + jnp.exp(sink - m))
        o = (jnp.dot(p_loc.astype(_BF), vb[:, hs].astype(_BF), preferred_element_type=_F32)
             + jnp.dot(p_ctx.astype(_BF), cv_ref[:, hs].astype(_BF), preferred_element_type=_F32)) / den
        _unstack_store(o_ref, o, kh, BLOCK)


def _lat_attention(l, sinks, q, k, v, ck, cv):
    nb = DEC_SEQ // BLOCK
    q0 = T_PROMPT // BLOCK
    s0 = T_PROMPT // DEC_SEQ
    return pl.pallas_call(
        functools.partial(_lat_attn_kernel, l),
        grid_spec=pltpu.PrefetchScalarGridSpec(
            num_scalar_prefetch=1,
            grid=(DEC_BATCH, nb),
            in_specs=[
                pl.BlockSpec((BLOCK, ATTN_W), lambda b, n, s: (q0 + b * nb + n, 0)),
                pl.BlockSpec((DEC_SEQ, KV_W), lambda b, n, s: (s0 + b, 0)),
                pl.BlockSpec((DEC_SEQ, KV_W), lambda b, n, s: (s0 + b, 0)),
                pl.BlockSpec((None, None, PAST_LEN, KV_W), lambda b, n, s: (b, l, 0, 0)),
                pl.BlockSpec((None, None, PAST_LEN, KV_W), lambda b, n, s: (b, l, 0, 0)),
            ],
            out_specs=pl.BlockSpec((BLOCK, ATTN_W), lambda b, n, s: (b * nb + n, 0)),
        ),
        out_shape=jax.ShapeDtypeStruct((T_SAMPLE, ATTN_W), _BF),
        compiler_params=_params(("arbitrary", "arbitrary")),
        name="lat_attention",
    )(sinks, q, k, v, ck, cv)


def _conv_kernel(seq, u_ref, dw_ref, dwb_ref, lg_ref, lb_ref, pw_ref, o_ref, pad_ref, y_ref, sh_ref):
    u = u_ref[...]
    pad_ref[0:CONV_PAD, :] = jnp.zeros((CONV_PAD, CONV_W), _F32)
    pad_ref[CONV_PAD + seq:2 * CONV_PAD + seq, :] = jnp.zeros((CONV_PAD, CONV_W), _F32)
    pad_ref[CONV_PAD:CONV_PAD + seq, :] = u[:, :CONV_W] * _sigmoid(u[:, CONV_W:])
    off = CONV_PAD - CONV_K // 2
    span = CONV_SPAN

    def chunk(c, carry):
        row = pl.multiple_of(c * CONV_CHUNK, CONV_CHUNK)
        win = pad_ref[pl.ds(row, CONV_CHUNK + 2 * CONV_PAD), :]
        acc = jnp.zeros((CONV_CHUNK, CONV_W), _F32) + dwb_ref[...]
        for phase in range(8):
            sh_ref[phase] = win[phase:phase + span, :]
        for phase in range(8):
            for a in range(span // 8):
                t = 8 * a + phase - off
                if 0 <= t < CONV_K:
                    acc = acc + sh_ref[phase, 8 * a:8 * a + CONV_CHUNK, :] * dw_ref[t:t + 1, :]
        mu = jnp.mean(acc, axis=-1, keepdims=True)
        d = acc - mu
        var = jnp.mean(d * d, axis=-1, keepdims=True)
        y = d * lax.rsqrt(var + EPS) * lg_ref[...] + lb_ref[...]
        y_ref[pl.ds(row, CONV_CHUNK), :] = (y * _sigmoid(y)).astype(_BF)
        return carry

    lax.fori_loop(0, seq // CONV_CHUNK, chunk, 0)
    o_ref[...] = jnp.dot(y_ref[...], pw_ref[...].astype(_BF), preferred_element_type=_F32).astype(o_ref.dtype)


def _conv_module(uc, seq, nbatch, block0, l, dw, dwb, lg, lb, pw):
    vec = lambda a: a.reshape(DEPTH, 1, CONV_W)
    layer = lambda b: (l, 0, 0)
    return pl.pallas_call(
        functools.partial(_conv_kernel, seq),
        grid=(nbatch,),
        in_specs=[
            pl.BlockSpec((seq, 2 * CONV_W), lambda b: (block0 + b, 0)),
            pl.BlockSpec((None, CONV_K, CONV_W), layer),
            pl.BlockSpec((None, 1, CONV_W), layer),
            pl.BlockSpec((None, 1, CONV_W), layer),
            pl.BlockSpec((None, 1, CONV_W), layer),
            pl.BlockSpec((None, CONV_W, CONV_W), layer),
        ],
        out_specs=pl.BlockSpec((seq, CONV_W), lambda b: (b, 0)),
        out_shape=jax.ShapeDtypeStruct((nbatch * seq, CONV_W), _BF),
        scratch_shapes=[pltpu.VMEM((seq + 2 * CONV_PAD, CONV_W), _F32), pltpu.VMEM((seq, CONV_W), _BF),
                        pltpu.VMEM((8, CONV_SPAN, CONV_W), _F32)],
        compiler_params=_params(("arbitrary",)),
        name="conv_module_%d" % seq,
    )(uc, dw, vec(dwb), vec(lg), vec(lb), pw)


def _dft_constants(seq):
    j = np.arange(seq, dtype=np.int64)
    ang = 2.0 * np.pi * ((j[:, None] * j[None, :]) % seq).astype(np.float64) / seq
    cs = np.concatenate([np.cos(ang), -np.sin(ang)], axis=1) / np.sqrt(seq)
    c = np.arange(FNET_GW, dtype=np.int64)
    angc = 2.0 * np.pi * ((c[:, None] * c[None, :]) % FNET_GW).astype(np.float64) / FNET_GW
    eye = np.eye(FNET_GROUPS)
    cc = np.kron(eye, np.cos(angc)) / np.sqrt(FNET_GW)
    sc = np.kron(eye, np.sin(angc)) / np.sqrt(FNET_GW)
    w1 = np.concatenate([cc, sc], axis=1)
    return jnp.asarray(cs, dtype=_F32), jnp.asarray(w1, dtype=_F32)


def _fnet_kernel(seq, u_ref, cs_ref, w1_ref, fw_ref, o_ref, xcs_ref):
    @pl.when(pl.program_id(1) == 0)
    def _():
        t = jnp.dot(u_ref[...].astype(_BF), w1_ref[...].astype(_BF), preferred_element_type=_F32)
        xcs_ref[0:seq, :] = t[:, :FNET_W].astype(_BF)
        xcs_ref[seq:2 * seq, :] = t[:, FNET_W:].astype(_BF)

    mixed = jnp.dot(cs_ref[...].astype(_BF), xcs_ref[...], preferred_element_type=_F32)
    o_ref[...] = jnp.dot(mixed.astype(_BF), fw_ref[...].astype(_BF),
                         preferred_element_type=_F32).astype(o_ref.dtype)


def _fourier_mix(uf, seq, nbatch, block0, l, fw):
    tr = min(seq, 512)
    nr = seq // tr
    cs, w1 = _dft_constants(seq)
    return pl.pallas_call(
        functools.partial(_fnet_kernel, seq),
        grid=(nbatch, nr),
        in_specs=[
            pl.BlockSpec((seq, FNET_W), lambda b, r: (block0 + b, 0)),
            pl.BlockSpec((tr, 2 * seq), lambda b, r: (r, 0)),
            pl.BlockSpec((FNET_W, 2 * FNET_W), lambda b, r: (0, 0)),
            pl.BlockSpec((None, FNET_W, FNET_W), lambda b, r: (l, 0, 0)),
        ],
        out_specs=pl.BlockSpec((tr, FNET_W), lambda b, r: (b * nr + r, 0)),
        out_shape=jax.ShapeDtypeStruct((nbatch * seq, FNET_W), _BF),
        scratch_shapes=[pltpu.VMEM((2 * seq, FNET_W), _BF)],
        compiler_params=_params(("arbitrary", "arbitrary")),
        name="fourier_mix_%d" % seq,
    )(uf, cs, w1, fw)


def _route_top2(r_ref, h, ei_ref, ew_ref):
    lg = lax.dot_general(r_ref[...].astype(_BF), h.astype(_BF), _NT, preferred_element_type=_F32)
    eid = lax.broadcasted_iota(jnp.int32, lg.shape, 0)
    m1 = jnp.max(lg, axis=0, keepdims=True)
    i1 = jnp.min(jnp.where(lg == m1, eid, N_EXPERTS), axis=0, keepdims=True)
    lg2 = jnp.where(eid == i1, -jnp.inf, lg)
    m2 = jnp.max(lg2, axis=0, keepdims=True)
    i2 = jnp.min(jnp.where(lg2 == m2, eid, N_EXPERTS), axis=0, keepdims=True)
    e = jnp.exp(m2 - m1)
    ei_ref[0:1, :] = i1
    ei_ref[1:2, :] = i2
    ew_ref[0:1, :] = 1.0 / (1.0 + e)
    ew_ref[1:2, :] = e / (1.0 + e)


def _outproj_kernel(split, route, *refs):
    branch_refs, refs = refs[:6], refs[6:]
    n_x = 2 if split else 1
    x_refs, refs = refs[:n_x], refs[n_x:]
    mod_ref, w_ref, gf_ref = refs[:3]
    refs = refs[3:]
    if route:
        r_ref, xo_ref, h_ref, ei_ref, ew_ref, wb_ref = refs
    else:
        xo_ref, h_ref, wb_ref = refs
    i = pl.program_id(0)

    @pl.when(i == 0)
    def _():
        wb_ref[...] = w_ref[...].astype(_BF)

    mix = jnp.concatenate([_pick_half(i, ROW_TILE, branch_refs[2 * n], branch_refs[2 * n + 1])
                           for n in range(3)], axis=-1)
    x = _pick_half(i, ROW_TILE, *x_refs) if split else x_refs[0][...]
    x_new = x + mod_ref[2:3, :] * jnp.dot(mix, wb_ref[...], preferred_element_type=_F32)
    xo_ref[...] = x_new
    h = _norm_mod(x_new, gf_ref[...], mod_ref[3:4, :], mod_ref[4:5, :])
    h_ref[...] = h.astype(h_ref.dtype)
    if route:
        _route_top2(r_ref, h, ei_ref, ew_ref)


def _outproj(branches, xs, l, mods, w_all, gf_all, h_dtype, router_t=None):
    split = len(xs) == 2
    route = router_t is not None
    row = lambda i: (i, 0)
    in_specs = []
    args = []
    for pair, width in zip(branches, (ATTN_W, CONV_W, FNET_W)):
        in_specs += _half_specs(width, ROW_TILE)
        args += list(pair)
    in_specs += _half_specs(D_MODEL, ROW_TILE) if split else [pl.BlockSpec((ROW_TILE, D_MODEL), row)]
    args += list(xs)
    mod_spec, gf_spec, w_spec = _layer_specs(l, (D_MODEL, D_MODEL))
    in_specs += [mod_spec, w_spec, gf_spec]
    args += [mods, w_all, gf_all.reshape(DEPTH, 1, D_MODEL)]
    out_specs = [pl.BlockSpec((ROW_TILE, D_MODEL), row)] * 2
    out_shape = [jax.ShapeDtypeStruct((T, D_MODEL), _F32), jax.ShapeDtypeStruct((T, D_MODEL), h_dtype)]
    if route:
        in_specs.append(pl.BlockSpec((N_EXPERTS, D_MODEL), lambda i: (0, 0)))
        args.append(router_t)
        out_specs += [pl.BlockSpec((TOP_K, ROW_TILE), lambda i: (0, i))] * 2
        out_shape += [jax.ShapeDtypeStruct((TOP_K, T), jnp.int32), jax.ShapeDtypeStruct((TOP_K, T), _F32)]
    return pl.pallas_call(
        functools.partial(_outproj_kernel, split, route),
        grid=(T // ROW_TILE,),
        in_specs=in_specs,
        out_specs=out_specs,
        out_shape=out_shape,
        scratch_shapes=[pltpu.VMEM((D_MODEL, D_MODEL), _BF)],
        compiler_params=_params(("arbitrary",)),
        name="outproj_route" if route else "outproj",
    )(*args)


def _ffn_kernel(nj, n_sub_rows, vis_e, vis_start, vis_cnt, used_sub,
                x_hbm, wg_ref, wu_ref, wd_ref, y_hbm,
                big, xb, wgb, wub, wdb, sem_in, sem_out):
    del vis_e
    v = pl.program_id(0)
    j = pl.program_id(1)
    cnt = vis_cnt[v]
    row0 = vis_start[v] * SUB

    x_is_bf16 = x_hbm.dtype == _BF
    landing = xb if x_is_bf16 else big

    def copy_in(s):
        r = pl.multiple_of(row0 + s * SUB, SUB)
        b = pl.multiple_of(s * SUB, SUB)
        return pltpu.make_async_copy(x_hbm.at[pl.ds(r, SUB)], landing.at[pl.ds(b, SUB)], sem_in.at[s])

    def copy_out(s):
        r = pl.multiple_of(row0 + s * SUB, SUB)
        b = pl.multiple_of(s * SUB, SUB)
        return pltpu.make_async_copy(big.at[pl.ds(b, SUB)], y_hbm.at[pl.ds(r, SUB)], sem_out)

    def for_subs(fn):
        lax.fori_loop(0, cnt, lambda s, c: (fn(s), c)[1], 0)

    @pl.when(jnp.logical_and(j == 0, cnt > 0))
    def _():
        for_subs(lambda s: copy_in(s).start())

    @pl.when(cnt > 0)
    def _():
        wgb[...] = wg_ref[...].astype(_BF)
        wub[...] = wu_ref[...].astype(_BF)
        wdb[...] = wd_ref[...].astype(_BF)

        def block(r0, nrows):
            s0 = r0 // SUB

            @pl.when(j == 0)
            def _():
                for t in range(nrows // SUB):
                    copy_in(s0 + t).wait()
                    rs = pl.ds(pl.multiple_of(r0 + t * SUB, SUB), SUB)
                    if not x_is_bf16:
                        xb[rs, :] = big[rs, :].astype(_BF)
                    big[rs, :] = jnp.zeros((SUB, D_MODEL), _F32)

            rs = pl.ds(r0, nrows)
            x = xb[rs, :]
            a = jnp.dot(x, wgb[...], preferred_element_type=_F32)
            b = jnp.dot(x, wub[...], preferred_element_type=_F32)
            p = (a * _sigmoid(a) * b).astype(_BF)
            big[rs, :] += jnp.dot(p, wdb[...], preferred_element_type=_F32)

            @pl.when(j == nj - 1)
            def _():
                for t in range(nrows // SUB):
                    copy_out(s0 + t).start()

        nblk = (cnt + BLK_SUBS - 1) // BLK_SUBS
        small = cnt // nblk
        n_big = cnt % nblk

        def one_block(i, r0):
            size = small + (i < n_big).astype(jnp.int32)
            for sz in range(1, BLK_SUBS + 1):
                pl.when(size == sz)(functools.partial(block, pl.multiple_of(r0, SUB), sz * SUB))
            return r0 + size * SUB
        lax.fori_loop(0, nblk, one_block, 0)

    @pl.when(jnp.logical_and(j == nj - 1, cnt > 0))
    def _():
        for_subs(lambda s: copy_out(s).wait())

    if x_hbm.dtype == y_hbm.dtype:
        @pl.when(jnp.logical_and(v == pl.num_programs(0) - 1, j == nj - 1))
        def _():
            def tail(s):
                r = pl.multiple_of(s * SUB, SUB)
                return pltpu.make_async_copy(x_hbm.at[pl.ds(r, SUB)], y_hbm.at[pl.ds(r, SUB)], sem_out)
            lax.fori_loop(used_sub[0], n_sub_rows, lambda s, c: (tail(s).start(), c)[1], 0)
            lax.fori_loop(used_sub[0], n_sub_rows, lambda s, c: (tail(s).wait(), c)[1], 0)


def _grouped_ffn(x, wg, wu, wd, fc, vis_e, vis_start, vis_cnt, used_sub):
    ff = wg.shape[-1]
    nj = ff // fc
    nv = vis_e.shape[0]
    n_rows = x.shape[0]

    def chunk(v, j, vc):
        return jnp.where(vc[v] > 0, j, nj - 1)

    return pl.pallas_call(
        functools.partial(_ffn_kernel, nj, n_rows // SUB),
        grid_spec=pltpu.PrefetchScalarGridSpec(
            num_scalar_prefetch=4,
            grid=(nv, nj),
            in_specs=[
                pl.BlockSpec(memory_space=pl.ANY),
                pl.BlockSpec((None, D_MODEL, fc), lambda v, j, ve, vs, vc, us: (ve[v], 0, chunk(v, j, vc))),
                pl.BlockSpec((None, D_MODEL, fc), lambda v, j, ve, vs, vc, us: (ve[v], 0, chunk(v, j, vc))),
                pl.BlockSpec((None, fc, D_MODEL), lambda v, j, ve, vs, vc, us: (ve[v], chunk(v, j, vc), 0)),
            ],
            out_specs=pl.BlockSpec(memory_space=pl.ANY),
            scratch_shapes=[
                pltpu.VMEM((TMAX, D_MODEL), _F32),
                pltpu.VMEM((TMAX, D_MODEL), _BF),
                pltpu.VMEM((D_MODEL, fc), _BF),
                pltpu.VMEM((D_MODEL, fc), _BF),
                pltpu.VMEM((fc, D_MODEL), _BF),
                pltpu.SemaphoreType.DMA((SUB_MAX,)),
                pltpu.SemaphoreType.DMA(()),
            ],
        ),
        out_shape=jax.ShapeDtypeStruct((n_rows, D_MODEL), _F32),
        compiler_params=_params(("arbitrary", "arbitrary"), FFN_VMEM_LIMIT),
        name="ffn_%d" % ff,
    )(vis_e, vis_start, vis_cnt, used_sub, x, wg, wu, wd)


def _dense_visits():
    nv = T // TMAX
    return (jnp.zeros((nv,), jnp.int32),
            jnp.arange(nv, dtype=jnp.int32) * SUB_MAX,
            jnp.full((nv,), SUB_MAX, jnp.int32),
            jnp.full((1,), T // SUB, jnp.int32))


N_SLOT_SUB = -(-(TOP_K * T + N_TILES * N_EXPERTS * (SEG_ALIGN - 1)) // SUB) + N_EXPERTS
N_SLOT = N_SLOT_SUB * SUB
N_VISIT = -(-N_SLOT_SUB // SUB_MAX) + N_EXPERTS


def _routing_plan(eidx):
    e_loc = eidx.reshape(TOP_K, N_TILES, DISP_TILE).transpose(1, 0, 2).reshape(N_TILES, PAIRS)
    onehot = e_loc[:, :, None] == jnp.arange(N_EXPERTS, dtype=jnp.int32)[None, None, :]
    ch = 128
    oh = onehot.astype(_F32).reshape(N_TILES, PAIRS // ch, ch, N_EXPERTS)
    tri = (jnp.arange(ch)[:, None] >= jnp.arange(ch)[None, :]).astype(_F32)
    within = jnp.einsum("ij,tcjk->tcik", tri, oh)
    tot = within[:, :, -1, :]
    csum = (within + (jnp.cumsum(tot, axis=1) - tot)[:, :, None, :]).reshape(N_TILES, PAIRS, N_EXPERTS)
    csum = csum.astype(jnp.int32)
    n_te = (csum[:, -1, :] + SEG_ALIGN - 1) // SEG_ALIGN * SEG_ALIGN
    src = jnp.cumsum(n_te, axis=1) - n_te
    lpos = jnp.sum(jnp.where(onehot, csum - 1 + src[:, None, :], 0), axis=2).astype(jnp.int32)
    counts = jnp.sum(n_te, axis=0)
    nsub = (counts + SUB - 1) // SUB
    sub_base = jnp.cumsum(nsub) - nsub
    dst = (sub_base * SUB)[None, :] + jnp.cumsum(n_te, axis=0) - n_te
    seg = tuple(a.reshape(-1).astype(jnp.int32) for a in (n_te, src, dst))
    pads = ((sub_base * SUB + counts).astype(jnp.int32), (nsub * SUB - counts).astype(jnp.int32))
    used_sub = jnp.sum(nsub).reshape(1).astype(jnp.int32)
    nvis = (nsub + SUB_MAX - 1) // SUB_MAX
    vend = jnp.cumsum(nvis)
    total = vend[-1]
    vid = jnp.arange(N_VISIT, dtype=jnp.int32)
    ve = jnp.minimum(jnp.sum((vid[:, None] >= vend[None, :]).astype(jnp.int32), axis=1), N_EXPERTS - 1)
    local = vid - (vend - nvis)[ve]
    nv_e = jnp.maximum(nvis[ve], 1)
    q, r = nsub[ve] // nv_e, nsub[ve] % nv_e
    cnt = q + (local < r).astype(jnp.int32)
    start = sub_base[ve] + local * q + jnp.minimum(local, r)
    used = vid < total
    last_e = ve[jnp.maximum(total - 1, 0)]
    vis_e = jnp.where(used, ve, last_e).astype(jnp.int32)
    vis_cnt = jnp.where(used, cnt, 0).astype(jnp.int32)
    vis_start = jnp.where(used, start, 0).astype(jnp.int32)
    return lpos.reshape(N_TILES, TOP_K, DISP_TILE), seg, pads, (vis_e, vis_start, vis_cnt, used_sub)


def _for_pow2_pieces(n, max_piece, fn):
    off = 0
    bit = max_piece
    while bit >= SEG_ALIGN:
        take = (n // bit) % 2 == 1
        pl.when(take)(functools.partial(fn, off, bit))
        off = off + jnp.where(take, bit, 0)
        bit //= 2


def _segment_copies(tile, seg, make_copy, wait=False):
    n_te, src, dst = seg
    for e in range(N_EXPERTS):
        j = tile * N_EXPERTS + e
        s0, d0 = src[j], dst[j]

        def piece(off, size, s0=s0, d0=d0):
            cp = make_copy(pl.multiple_of(s0 + off, SEG_ALIGN), pl.multiple_of(d0 + off, SEG_ALIGN), size)
            cp.wait() if wait else cp.start()
        _for_pow2_pieces(n_te[j], DISP_TILE, piece)


def _dispatch_kernel(n_te, src, dst, pad_start, pad_cnt, used_sub, h_ref, lpos_ref, xs_hbm, ring, ring_sems, sem):
    i = pl.program_id(0)
    last = pl.num_programs(0) - 1
    b = i % 2

    def copies(tile, slot_id, wait):
        _segment_copies(tile, (n_te, src, dst), lambda s, d, size: pltpu.make_async_copy(
            ring.at[slot_id, pl.ds(s, size)], xs_hbm.at[pl.ds(d, size)], ring_sems.at[slot_id]), wait)

    pl.when(i >= 2)(lambda: copies(i - 2, b, True))
    lp = lpos_ref[...]
    srow = lax.broadcasted_iota(jnp.int32, (SORT_ROWS, DISP_TILE), 0)
    perm = jnp.where(lp[0:1, :] == srow, 1.0, jnp.where(lp[1:2, :] == srow, 1.0, 0.0)).astype(_BF)
    ring[b] = jnp.dot(perm, h_ref[...].astype(_BF), preferred_element_type=_F32)
    copies(i, b, False)

    @pl.when(i == last)
    def _():
        copies(i - 1, 1 - b, True)
        copies(i, b, True)
        for e in range(N_EXPERTS):
            p0 = pad_start[e]

            def fill(off, size, p0=p0):
                cp = pltpu.make_async_copy(ring.at[b, pl.ds(0, size)],
                                           xs_hbm.at[pl.ds(pl.multiple_of(p0 + off, SEG_ALIGN), size)], sem)
                cp.start()
                cp.wait()
            _for_pow2_pieces(pad_cnt[e], SUB // 2, fill)

        def tail(s):
            r = pl.multiple_of(s * SUB, SUB)
            return pltpu.make_async_copy(ring.at[b, pl.ds(0, SUB)], xs_hbm.at[pl.ds(r, SUB)], sem)
        lax.fori_loop(used_sub[0], N_SLOT_SUB, lambda s, c: (tail(s).start(), c)[1], 0)
        lax.fori_loop(used_sub[0], N_SLOT_SUB, lambda s, c: (tail(s).wait(), c)[1], 0)


def _dispatch(h, lpos, seg, pads, used_sub):
    return pl.pallas_call(
        _dispatch_kernel,
        grid_spec=pltpu.PrefetchScalarGridSpec(
            num_scalar_prefetch=6,
            grid=(N_TILES,),
            in_specs=[pl.BlockSpec((DISP_TILE, D_MODEL), lambda i, *_: (i, 0)),
                      pl.BlockSpec((None, TOP_K, DISP_TILE), lambda i, *_: (i, 0, 0))],
            out_specs=pl.BlockSpec(memory_space=pl.ANY),
            scratch_shapes=[pltpu.VMEM((2, SORT_ROWS, D_MODEL), _F32),
                            pltpu.SemaphoreType.DMA((2,)), pltpu.SemaphoreType.DMA(())],
        ),
        out_shape=jax.ShapeDtypeStruct((N_SLOT, D_MODEL), _F32),
        compiler_params=_params(("arbitrary",)),
        name="dispatch",
    )(*seg, *pads, used_sub, h, lpos)


def _residual_out(x, gate, f, g_ref, o_refs, i, tile):
    out = x + gate * f
    if g_ref is not None:
        ms = jnp.mean(out * out, axis=-1, keepdims=True)
        out = out * lax.rsqrt(ms + EPS) * g_ref[...]
    if len(o_refs) == 1:
        o_refs[0][...] = out
    else:
        @pl.when(i < T_PROMPT // tile)
        def _():
            o_refs[0][...] = out

        @pl.when(i >= T_PROMPT // tile)
        def _():
            o_refs[1][...] = out


def _out_specs(final, tile):
    if not final:
        return ([pl.BlockSpec((tile, D_MODEL), lambda i, *_: (i, 0))],
                [jax.ShapeDtypeStruct((T, D_MODEL), _F32)])
    return (_half_specs(D_MODEL, tile),
            [jax.ShapeDtypeStruct((T_PROMPT, D_MODEL), _F32), jax.ShapeDtypeStruct((T_SAMPLE, D_MODEL), _F32)])


def _combine_kernel(final, x_ref, mod_ref, y_ref, *rest):
    g_ref, o_refs = (rest[0], rest[1:]) if final else (None, rest)
    _residual_out(x_ref[...], mod_ref[5:6, :], y_ref[...], g_ref, o_refs, pl.program_id(0), ROW_TILE)


def _combine(x, l, mods, y, g_final=None):
    final = g_final is not None
    row = lambda i: (i, 0)
    in_specs = [
        pl.BlockSpec((ROW_TILE, D_MODEL), row),
        _layer_specs(l, ())[0],
        pl.BlockSpec((ROW_TILE, D_MODEL), row),
    ]
    args = [x, mods, y]
    if final:
        in_specs.append(pl.BlockSpec((1, D_MODEL), lambda i: (0, 0)))
        args.append(g_final.reshape(1, D_MODEL))
    out_specs, out_shape = _out_specs(final, ROW_TILE)
    return pl.pallas_call(
        functools.partial(_combine_kernel, final),
        grid=(T // ROW_TILE,),
        in_specs=in_specs,
        out_specs=out_specs,
        out_shape=out_shape,
        compiler_params=_params(("arbitrary",)),
        name="combine%s" % ("_final" if final else ""),
    )(*args)


def _combine_top2_kernel(final, n_out, n_te, src, dst, x_ref, mod_ref, w_ref, lpos_ref, *rest):
    g_ref, rest = (rest[0], rest[1:]) if final else (None, rest)
    ys_hbm, o_refs, (ybuf, sems) = rest[0], rest[1:1 + n_out], rest[1 + n_out:]
    i = pl.program_id(0)

    def fetch(tile, b, wait=False):
        _segment_copies(tile, (n_te, src, dst), lambda s, d, size: pltpu.make_async_copy(
            ys_hbm.at[pl.ds(d, size)], ybuf.at[b, pl.ds(s, size)], sems.at[b]), wait)

    @pl.when(i == 0)
    def _():
        for slot_id in range(2):
            ybuf[slot_id, PAIRS:SORT_ROWS, :] = jnp.zeros((SORT_ROWS - PAIRS, D_MODEL), _F32)
        fetch(0, 0)

    @pl.when(i + 1 < pl.num_programs(0))
    def _():
        fetch(i + 1, (i + 1) % 2)

    b = i % 2
    fetch(i, b, wait=True)
    y = ybuf[b].astype(_BF)
    lp = lpos_ref[...]
    scol = lax.broadcasted_iota(jnp.int32, (DISP_TILE, SORT_ROWS), 1)
    w = w_ref[...]
    pick = jnp.zeros((DISP_TILE, SORT_ROWS), _F32)
    for k in range(TOP_K):
        pick = jnp.where(lp[:, k:k + 1] == scol, w[:, k:k + 1], pick)
    f = jnp.dot(pick.astype(_BF), y, preferred_element_type=_F32)
    _residual_out(x_ref[...], mod_ref[5:6, :], f, g_ref, o_refs, i, DISP_TILE)


def _combine_top2(x, l, mods, ys, lpos_t, seg, w, g_final=None):
    final = g_final is not None
    row = lambda i, *_: (i, 0)
    in_specs = [
        pl.BlockSpec((DISP_TILE, D_MODEL), row),
        pl.BlockSpec((None, None, 6, D_MODEL), lambda i, *_: (l, _cond_of_tile(i, DISP_TILE), 0, 0)),
        pl.BlockSpec((DISP_TILE, TOP_K), row),
        pl.BlockSpec((None, DISP_TILE, TOP_K), lambda i, *_: (i, 0, 0)),
    ]
    args = [x, mods, w, lpos_t]
    if final:
        in_specs.append(pl.BlockSpec((1, D_MODEL), lambda i, *_: (0, 0)))
        args.append(g_final.reshape(1, D_MODEL))
    in_specs.append(pl.BlockSpec(memory_space=pl.ANY))
    args.append(ys)
    out_specs, out_shape = _out_specs(final, DISP_TILE)
    return pl.pallas_call(
        functools.partial(_combine_top2_kernel, final, len(out_specs)),
        grid_spec=pltpu.PrefetchScalarGridSpec(
            num_scalar_prefetch=3,
            grid=(N_TILES,),
            in_specs=in_specs,
            out_specs=out_specs,
            scratch_shapes=[
                pltpu.VMEM((2, SORT_ROWS, D_MODEL), _F32),
                pltpu.SemaphoreType.DMA((2,)),
            ],
        ),
        out_shape=out_shape,
        compiler_params=_params(("arbitrary",)),
        name="combine_top2%s" % ("_final" if final else ""),
    )(*seg, *args)


def kernel(x_prompt, x_sample, cache_k, cache_v, c, c_ctx, w_ada, b_ada, g_norm_mix, g_norm_ffn,
           w_in, w_out, attn_sink, conv_dw, conv_dw_b, conv_ln_g, conv_ln_b, conv_pw, fnet_w,
           ffn_w_gate, ffn_w_up, ffn_w_down, moe_router, moe_w_gate, moe_w_up, moe_w_down, g_final):
    xs = (x_prompt.reshape(T_PROMPT, D_MODEL), x_sample.reshape(T_SAMPLE, D_MODEL))
    cond8 = jnp.concatenate([c_ctx[None, :], c, jnp.zeros((N_COND - 1 - DEC_BATCH, D_MODEL), _F32)], axis=0)
    mods = _modulation(cond8, w_ada, b_ada).reshape(DEPTH, N_COND, 6, D_MODEL)
    cos_t, sin_t = _rope_tables()
    ck_all = cache_k.reshape(DEC_BATCH, DEPTH, PAST_LEN, KV_W)
    cv_all = cache_v.reshape(DEC_BATCH, DEPTH, PAST_LEN, KV_W)
    p_blocks = T_PROMPT // DEC_SEQ
    sinks = attn_sink.reshape(DEPTH * N_HEADS)

    ks, vs = [], []
    resid = None
    for l in range(DEPTH):
        q, k, v, uc, uf, *x_new = _inproj(xs, l, mods, g_norm_mix, w_in, cos_t, sin_t, resid=resid)
        if resid is not None:
            xs, resid = tuple(x_new), None
        ks.append(k[:T_PROMPT])
        vs.append(v[:T_PROMPT])
        attn = (_ctx_attention(l, sinks, q, k, v),
                _lat_attention(l, sinks, q, k, v, ck_all, cv_all))
        cargs = (l, conv_dw, conv_dw_b, conv_ln_g, conv_ln_b, conv_pw)
        conv = (_conv_module(uc, SEQ, BATCH, 0, *cargs),
                _conv_module(uc, DEC_SEQ, DEC_BATCH, p_blocks, *cargs))
        four = (_fourier_mix(uf, SEQ, BATCH, 0, l, fnet_w),
                _fourier_mix(uf, DEC_SEQ, DEC_BATCH, p_blocks, l, fnet_w))
        last = g_final if l == DEPTH - 1 else None
        i = l // 2
        if l % 2 == 0:
            x, h = _outproj((attn, conv, four), xs, l, mods, w_out, g_norm_ffn, _BF)
            y = _grouped_ffn(h, ffn_w_gate[i:i + 1], ffn_w_up[i:i + 1], ffn_w_down[i:i + 1], 256,
                             *_dense_visits())
            if last is None:
                xs, resid = (x,), (y, l)
            else:
                xs = tuple(_combine(x, l, mods, y, g_final=last))
        else:
            x, h, eidx, ew = _outproj((attn, conv, four), xs, l, mods, w_out, g_norm_ffn, _F32,
                                      router_t=moe_router[i].T)
            lpos, seg, pads, visits = _routing_plan(eidx)
            xd = _dispatch(h, lpos, seg, pads, visits[-1])
            ys = _grouped_ffn(xd, moe_w_gate[i], moe_w_up[i], moe_w_down[i], 512, *visits)
            xs = tuple(_combine_top2(x, l, mods, ys, lpos.transpose(0, 2, 1), seg, ew.T, g_final=last))

    y_prompt = xs[0].reshape(BATCH, SEQ, D_MODEL)
    y_sample = xs[1].reshape(DEC_BATCH, DEC_SEQ, D_MODEL)
    state_k = jnp.stack([a.reshape(BATCH, SEQ, N_KV_HEADS, HEAD_DIM) for a in ks], axis=1)
    state_v = jnp.stack([a.reshape(BATCH, SEQ, N_KV_HEADS, HEAD_DIM) for a in vs], axis=1)
    return (y_prompt, y_sample, state_k, state_v)
```

```python
import functools

import numpy as np
import jax
import jax.numpy as jnp
from jax import lax
from jax.experimental import pallas as pl
from jax.experimental.pallas import tpu as pltpu

D_MODEL = 1024
BATCH = 16
SEQ = 256
DEPTH = 2
DEC_BATCH = 2
DEC_SEQ = 2048
PAST_LEN = 512
GRID_W = 64
HEAD_DIM = 64
N_HEADS = 8
N_KV_HEADS = 2
GQA_GROUP = N_HEADS // N_KV_HEADS
ATTN_W = N_HEADS * HEAD_DIM
KV_W = N_KV_HEADS * HEAD_DIM
WINDOW = 128
BLOCK = 128
ROPE_THETA = 10000.0
CONV_W = D_MODEL // 4
CONV_K = 31
FNET_GROUPS = 4
FNET_W = D_MODEL // 4
FNET_GW = FNET_W // FNET_GROUPS
IN_WIDTH = ATTN_W + 2 * KV_W + 2 * CONV_W + FNET_W
D_FF = 2816
N_EXPERTS = 8
TOP_K = 2
D_FF_EXPERT = 3584
EPS = 1e-6
NEG = -1e30

T_PROMPT = BATCH * SEQ
T_SAMPLE = DEC_BATCH * DEC_SEQ
T = T_PROMPT + T_SAMPLE
N_COND = 8

ROW_TILE = 512
SUB = 128
SUB_MAX = 32
TMAX = SUB * SUB_MAX
BLK_MIN = 8
BLK_SUBS = 10
DISP_TILE = 512
N_TILES = T // DISP_TILE
PAIRS = TOP_K * DISP_TILE
SEG_ALIGN = 8
SORT_ROWS = 1152
CONV_CHUNK = 64
CONV_PAD = 16
CONV_SPAN = CONV_CHUNK + 8 * ((CONV_PAD + CONV_K // 2) // 8)
VMEM_LIMIT = 48 * 1024 * 1024
FFN_VMEM_LIMIT = 56 * 1024 * 1024

_BF = jnp.bfloat16
_F32 = jnp.float32


def _cond_of_tile(i, tile):
    r = i * tile
    return jnp.where(r < T_PROMPT, 0, 1 + (r - T_PROMPT) // DEC_SEQ)


def _params(sem, vmem=VMEM_LIMIT):
    return pltpu.CompilerParams(dimension_semantics=sem, vmem_limit_bytes=vmem)


def _sigmoid(x):
    return 1.0 / (1.0 + jnp.exp(-x))


def _mod_kernel(cond_ref, w_ref, b_ref, o_ref):
    cnd = cond_ref[...]
    s = (cnd * _sigmoid(cnd)).astype(_BF)
    o_ref[...] = jnp.dot(s, w_ref[...].astype(_BF), preferred_element_type=_F32) + b_ref[...]


def _modulation(cond8, w_ada, b_ada):
    nt = 1536
    return pl.pallas_call(
        _mod_kernel,
        grid=(DEPTH, 6 * D_MODEL // nt),
        in_specs=[
            pl.BlockSpec((N_COND, D_MODEL), lambda l, n: (0, 0)),
            pl.BlockSpec((None, D_MODEL, nt), lambda l, n: (l, 0, n)),
            pl.BlockSpec((None, 1, nt), lambda l, n: (l, 0, n)),
        ],
        out_specs=pl.BlockSpec((None, N_COND, nt), lambda l, n: (l, 0, n)),
        out_shape=jax.ShapeDtypeStruct((DEPTH, N_COND, 6 * D_MODEL), _F32),
        compiler_params=_params(("arbitrary", "arbitrary")),
        name="modulation",
    )(cond8, w_ada, b_ada.reshape(DEPTH, 1, 6 * D_MODEL))


def _norm_mod(x, g, shift, scale):
    ms = jnp.mean(x * x, axis=-1, keepdims=True)
    y = x * lax.rsqrt(ms + EPS) * g
    return y * (1.0 + scale) + shift


def _rope_tables():
    rows = DEC_SEQ // GRID_W
    n_freq = HEAD_DIM // 4
    inv = ROPE_THETA ** (-jnp.arange(n_freq, dtype=_F32) / n_freq)
    gr, gc = jnp.meshgrid(jnp.arange(rows, dtype=_F32), jnp.arange(GRID_W, dtype=_F32), indexing="ij")
    ang_r = gr.reshape(-1)[:, None] * inv
    ang_c = gc.reshape(-1)[:, None] * inv
    cr, sr, cc, sc = jnp.cos(ang_r), jnp.sin(ang_r), jnp.cos(ang_c), jnp.sin(ang_c)
    cos64 = jnp.concatenate([cr, cr, cc, cc], axis=-1)
    sin64 = jnp.concatenate([-sr, sr, -sc, sc], axis=-1)
    return jnp.tile(cos64, (1, 2)), jnp.tile(sin64, (1, 2))


def _rope128(x, cos, sin):
    lane = lax.broadcasted_iota(jnp.int32, x.shape, 1)
    first = (lane % 32) < 16
    partner = jnp.where(first, pltpu.roll(x, 128 - 16, 1), pltpu.roll(x, 16, 1))
    return x * cos + partner * sin


def _half_specs(width, tile):
    pt = T_PROMPT // tile
    return [pl.BlockSpec((tile, width), lambda i, *_: (jnp.minimum(i, pt - 1), 0)),
            pl.BlockSpec((tile, width), lambda i, *_: (jnp.maximum(i - pt, 0), 0))]


def _pick_half(i, tile, a_ref, b_ref):
    return jnp.where(i < T_PROMPT // tile, a_ref[...], b_ref[...])


def _layer_specs(l, w_shape):
    return [
        pl.BlockSpec((None, None, 6, D_MODEL), lambda i, *_: (l, _cond_of_tile(i, ROW_TILE), 0, 0)),
        pl.BlockSpec((None, 1, D_MODEL), lambda i, *_: (l, 0, 0)),
        pl.BlockSpec((None,) + w_shape, lambda i, *_: (l, 0, 0)),
    ]


def _inproj_kernel(split, resid, *refs):
    n_x = 2 if split else 1
    x_refs, refs = refs[:n_x], refs[n_x:]
    if resid:
        y_ref, modp_ref = refs[:2]
        refs = refs[2:]
    mod_ref, g_ref, w_ref, cos_ref, sin_ref, q_ref, k_ref, v_ref, uc_ref, uf_ref = refs[:10]
    wb_ref = refs[-1]
    i = pl.program_id(0)

    @pl.when(i == 0)
    def _():
        wb_ref[...] = w_ref[...].astype(_BF)

    x = _pick_half(i, ROW_TILE, *x_refs) if split else x_refs[0][...]
    if resid:
        x = x + modp_ref[5:6, :] * y_ref[...]
        refs[10][...] = x
    h = _norm_mod(x, g_ref[...], mod_ref[0:1, :], mod_ref[1:2, :]).astype(_BF)
    latent = i >= T_PROMPT // ROW_TILE
    cos = jnp.where(latent, cos_ref[...], 1.0)
    sin = jnp.where(latent, sin_ref[...], 0.0)

    def proj(o, n):
        return jnp.dot(h, wb_ref[:, o:o + n], preferred_element_type=_F32)

    o = 0
    for c in range(ATTN_W // 256):
        qq = proj(o, 256)
        q_ref[:, o:o + 128] = _rope128(qq[:, :128], cos, sin).astype(q_ref.dtype)
        q_ref[:, o + 128:o + 256] = _rope128(qq[:, 128:], cos, sin).astype(q_ref.dtype)
        o += 256
    kv = proj(o, 2 * KV_W)
    k_ref[...] = _rope128(kv[:, :KV_W], cos, sin)
    v_ref[...] = kv[:, KV_W:]
    o += 2 * KV_W
    uc_ref[...] = proj(o, 2 * CONV_W)
    o += 2 * CONV_W
    uf_ref[...] = proj(o, FNET_W).astype(uf_ref.dtype)


def _inproj(xs, l, mods, g_all, w_all, cos_t, sin_t, resid=None):
    split = len(xs) == 2
    pt = T_PROMPT // ROW_TILE
    per_seq = DEC_SEQ // ROW_TILE

    def rope_idx(i):
        return (jnp.maximum(i - pt, 0) % per_seq, 0)

    row = lambda i: (i, 0)
    in_specs = _half_specs(D_MODEL, ROW_TILE) if split else [pl.BlockSpec((ROW_TILE, D_MODEL), row)]
    args = list(xs)
    outs = [ATTN_W, KV_W, KV_W, 2 * CONV_W, FNET_W]
    dtypes = [_BF, _F32, _F32, _F32, _BF]
    if resid is not None:
        y, l_prev = resid
        in_specs += [pl.BlockSpec((ROW_TILE, D_MODEL), row), _layer_specs(l_prev, ())[0]]
        args += [y, mods]
        outs.append(D_MODEL)
        dtypes.append(_F32)
    return pl.pallas_call(
        functools.partial(_inproj_kernel, split, resid is not None),
        grid=(T // ROW_TILE,),
        in_specs=in_specs + _layer_specs(l, (D_MODEL, IN_WIDTH)) + [
            pl.BlockSpec((ROW_TILE, 128), rope_idx),
            pl.BlockSpec((ROW_TILE, 128), rope_idx),
        ],
        out_specs=[pl.BlockSpec((ROW_TILE, n), row) for n in outs],
        out_shape=[jax.ShapeDtypeStruct((T, n), dt) for n, dt in zip(outs, dtypes)],
        scratch_shapes=[pltpu.VMEM((D_MODEL, IN_WIDTH), _BF)],
        compiler_params=_params(("arbitrary",)),
        name="inproj_resid" if resid is not None else "inproj",
    )(*args, mods, g_all.reshape(DEPTH, 1, D_MODEL), w_all, cos_t, sin_t)


def _stack_groups(q, kh, rows):
    parts = [q[:, (kh * GQA_GROUP + g) * HEAD_DIM:(kh * GQA_GROUP + g + 1) * HEAD_DIM] for g in range(GQA_GROUP)]
    return jnp.concatenate(parts, axis=0)


def _sink_column(sink_ref, head0, rows):
    r = lax.broadcasted_iota(jnp.int32, (GQA_GROUP * rows, 1), 0)
    col = jnp.full((GQA_GROUP * rows, 1), sink_ref[head0], _F32)
    for g in range(1, GQA_GROUP):
        col = jnp.where(r >= g * rows, sink_ref[head0 + g], col)
    return col


def _unstack_store(o_ref, o, kh, rows):
    for pair in range(GQA_GROUP // 2):
        a = o[(2 * pair) * rows:(2 * pair + 1) * rows]
        b = o[(2 * pair + 1) * rows:(2 * pair + 2) * rows]
        c0 = (kh * GQA_GROUP + 2 * pair) * HEAD_DIM
        o_ref[:, c0:c0 + 2 * HEAD_DIM] = jnp.concatenate([a, b], axis=-1).astype(o_ref.dtype)


_NT = (((1,), (1,)), ((), ()))


def _ctx_attn_kernel(l, sink_ref, q_ref, k_ref, v_ref, o_ref):
    scale = HEAD_DIM ** -0.5
    q = q_ref[...] * scale
    for kh in range(N_KV_HEADS):
        kk = k_ref[:, kh * HEAD_DIM:(kh + 1) * HEAD_DIM].astype(_BF)
        vv = v_ref[:, kh * HEAD_DIM:(kh + 1) * HEAD_DIM].astype(_BF)
        qs = _stack_groups(q, kh, SEQ).astype(_BF)
        s = lax.dot_general(qs, kk, _NT, preferred_element_type=_F32)
        sink = _sink_column(sink_ref, l * N_HEADS + kh * GQA_GROUP, SEQ)
        m = jnp.maximum(jnp.max(s, axis=-1, keepdims=True), sink)
        p = jnp.exp(s - m)
        den = jnp.sum(p, axis=-1, keepdims=True) + jnp.exp(sink - m)
        o = jnp.dot(p.astype(_BF), vv, preferred_element_type=_F32) / den
        _unstack_store(o_ref, o, kh, SEQ)


def _ctx_attention(l, sinks, q, k, v):
    row = lambda b, s: (b, 0)
    return pl.pallas_call(
        functools.partial(_ctx_attn_kernel, l),
        grid_spec=pltpu.PrefetchScalarGridSpec(
            num_scalar_prefetch=1,
            grid=(BATCH,),
            in_specs=[
                pl.BlockSpec((SEQ, ATTN_W), row),
                pl.BlockSpec((SEQ, KV_W), row),
                pl.BlockSpec((SEQ, KV_W), row),
            ],
            out_specs=pl.BlockSpec((SEQ, ATTN_W), row),
        ),
        out_shape=jax.ShapeDtypeStruct((T_PROMPT, ATTN_W), _BF),
        compiler_params=_params(("arbitrary",)),
        name="ctx_attention",
    )(sinks, q, k, v)


def _lat_attn_kernel(l, sink_ref, q_ref, k_ref, v_ref, ck_ref, cv_ref, o_ref):
    n = pl.program_id(1)
    scale = HEAD_DIM ** -0.5
    nb = DEC_SEQ // BLOCK
    band = 3 * BLOCK
    start = pl.multiple_of(jnp.clip(n - 1, 0, nb - 3) * BLOCK, BLOCK)
    q = q_ref[...] * scale
    kb = k_ref[pl.ds(start, band), :]
    vb = v_ref[pl.ds(start, band), :]
    rows = GQA_GROUP * BLOCK
    qpos = n * BLOCK + lax.broadcasted_iota(jnp.int32, (rows, band), 0) % BLOCK
    kpos = start + lax.broadcasted_iota(jnp.int32, (rows, band), 1)
    valid = jnp.abs(kpos - qpos) <= WINDOW
    for kh in range(N_KV_HEADS):
        hs = slice(kh * HEAD_DIM, (kh + 1) * HEAD_DIM)
        qs = _stack_groups(q, kh, BLOCK).astype(_BF)
        s_loc = lax.dot_general(qs, kb[:, hs].astype(_BF), _NT, preferred_element_type=_F32)
        s_loc = jnp.where(valid, s_loc, NEG)
        s_ctx = lax.dot_general(qs, ck_ref[:, hs].astype(_BF), _NT, preferred_element_type=_F32)
        sink = _sink_column(sink_ref, l * N_HEADS + kh * GQA_GROUP, BLOCK)
        m = jnp.maximum(jnp.maximum(jnp.max(s_loc, axis=-1, keepdims=True),
                                    jnp.max(s_ctx, axis=-1, keepdims=True)), sink)
        p_loc = jnp.exp(s_loc - m)
        p_ctx = jnp.exp(s_ctx - m)
        den = (jnp.sum(p_loc, axis=-1, keepdims=True) + jnp.sum(p_ctx, axis=-1, keepdims=True)
               + jnp.exp(sink - m))
        o = (jnp.dot(p_loc.astype(_BF), vb[:, hs].astype(_BF), preferred_element_type=_F32)
             + jnp.dot(p_ctx.astype(_BF), cv_ref[:, hs].astype(_BF), preferred_element_type=_F32)) / den
        _unstack_store(o_ref, o, kh, BLOCK)


def _lat_attention(l, sinks, q, k, v, ck, cv):
    nb = DEC_SEQ // BLOCK
    q0 = T_PROMPT // BLOCK
    s0 = T_PROMPT // DEC_SEQ
    return pl.pallas_call(
        functools.partial(_lat_attn_kernel, l),
        grid_spec=pltpu.PrefetchScalarGridSpec(
            num_scalar_prefetch=1,
            grid=(DEC_BATCH, nb),
            in_specs=[
                pl.BlockSpec((BLOCK, ATTN_W), lambda b, n, s: (q0 + b * nb + n, 0)),
                pl.BlockSpec((DEC_SEQ, KV_W), lambda b, n, s: (s0 + b, 0)),
                pl.BlockSpec((DEC_SEQ, KV_W), lambda b, n, s: (s0 + b, 0)),
                pl.BlockSpec((None, None, PAST_LEN, KV_W), lambda b, n, s: (b, l, 0, 0)),
                pl.BlockSpec((None, None, PAST_LEN, KV_W), lambda b, n, s: (b, l, 0, 0)),
            ],
            out_specs=pl.BlockSpec((BLOCK, ATTN_W), lambda b, n, s: (b * nb + n, 0)),
        ),
        out_shape=jax.ShapeDtypeStruct((T_SAMPLE, ATTN_W), _BF),
        compiler_params=_params(("arbitrary", "arbitrary")),
        name="lat_attention",
    )(sinks, q, k, v, ck, cv)


def _conv_kernel(seq, u_ref, dw_ref, dwb_ref, lg_ref, lb_ref, pw_ref, o_ref, pad_ref, y_ref, sh_ref):
    u = u_ref[...]
    pad_ref[0:CONV_PAD, :] = jnp.zeros((CONV_PAD, CONV_W), _F32)
    pad_ref[CONV_PAD + seq:2 * CONV_PAD + seq, :] = jnp.zeros((CONV_PAD, CONV_W), _F32)
    pad_ref[CONV_PAD:CONV_PAD + seq, :] = u[:, :CONV_W] * _sigmoid(u[:, CONV_W:])
    off = CONV_PAD - CONV_K // 2
    span = CONV_SPAN

    def chunk(c, carry):
        row = pl.multiple_of(c * CONV_CHUNK, CONV_CHUNK)
        win = pad_ref[pl.ds(row, CONV_CHUNK + 2 * CONV_PAD), :]
        acc = jnp.zeros((CONV_CHUNK, CONV_W), _F32) + dwb_ref[...]
        for phase in range(8):
            sh_ref[phase] = win[phase:phase + span, :]
        for phase in range(8):
            for a in range(span // 8):
                t = 8 * a + phase - off
                if 0 <= t < CONV_K:
                    acc = acc + sh_ref[phase, 8 * a:8 * a + CONV_CHUNK, :] * dw_ref[t:t + 1, :]
        mu = jnp.mean(acc, axis=-1, keepdims=True)
        d = acc - mu
        var = jnp.mean(d * d, axis=-1, keepdims=True)
        y = d * lax.rsqrt(var + EPS) * lg_ref[...] + lb_ref[...]
        y_ref[pl.ds(row, CONV_CHUNK), :] = (y * _sigmoid(y)).astype(_BF)
        return carry

    lax.fori_loop(0, seq // CONV_CHUNK, chunk, 0)
    o_ref[...] = jnp.dot(y_ref[...], pw_ref[...].astype(_BF), preferred_element_type=_F32).astype(o_ref.dtype)


def _conv_module(uc, seq, nbatch, block0, l, dw, dwb, lg, lb, pw):
    vec = lambda a: a.reshape(DEPTH, 1, CONV_W)
    layer = lambda b: (l, 0, 0)
    return pl.pallas_call(
        functools.partial(_conv_kernel, seq),
        grid=(nbatch,),
        in_specs=[
            pl.BlockSpec((seq, 2 * CONV_W), lambda b: (block0 + b, 0)),
            pl.BlockSpec((None, CONV_K, CONV_W), layer),
            pl.BlockSpec((None, 1, CONV_W), layer),
            pl.BlockSpec((None, 1, CONV_W), layer),
            pl.BlockSpec((None, 1, CONV_W), layer),
            pl.BlockSpec((None, CONV_W, CONV_W), layer),
        ],
        out_specs=pl.BlockSpec((seq, CONV_W), lambda b: (b, 0)),
        out_shape=jax.ShapeDtypeStruct((nbatch * seq, CONV_W), _BF),
        scratch_shapes=[pltpu.VMEM((seq + 2 * CONV_PAD, CONV_W), _F32), pltpu.VMEM((seq, CONV_W), _BF),
                        pltpu.VMEM((8, CONV_SPAN, CONV_W), _F32)],
        compiler_params=_params(("arbitrary",)),
        name="conv_module_%d" % seq,
    )(uc, dw, vec(dwb), vec(lg), vec(lb), pw)


def _dft_constants(seq):
    j = np.arange(seq, dtype=np.int64)
    ang = 2.0 * np.pi * ((j[:, None] * j[None, :]) % seq).astype(np.float64) / seq
    cs = np.concatenate([np.cos(ang), -np.sin(ang)], axis=1) / np.sqrt(seq)
    c = np.arange(FNET_GW, dtype=np.int64)
    angc = 2.0 * np.pi * ((c[:, None] * c[None, :]) % FNET_GW).astype(np.float64) / FNET_GW
    eye = np.eye(FNET_GROUPS)
    cc = np.kron(eye, np.cos(angc)) / np.sqrt(FNET_GW)
    sc = np.kron(eye, np.sin(angc)) / np.sqrt(FNET_GW)
    w1 = np.concatenate([cc, sc], axis=1)
    return jnp.asarray(cs, dtype=_F32), jnp.asarray(w1, dtype=_F32)


def _fnet_kernel(seq, u_ref, cs_ref, w1_ref, fw_ref, o_ref, xcs_ref):
    @pl.when(pl.program_id(1) == 0)
    def _():
        t = jnp.dot(u_ref[...].astype(_BF), w1_ref[...].astype(_BF), preferred_element_type=_F32)
        xcs_ref[0:seq, :] = t[:, :FNET_W].astype(_BF)
        xcs_ref[seq:2 * seq, :] = t[:, FNET_W:].astype(_BF)

    mixed = jnp.dot(cs_ref[...].astype(_BF), xcs_ref[...], preferred_element_type=_F32)
    o_ref[...] = jnp.dot(mixed.astype(_BF), fw_ref[...].astype(_BF),
                         preferred_element_type=_F32).astype(o_ref.dtype)


def _fourier_mix(uf, seq, nbatch, block0, l, fw):
    tr = min(seq, 512)
    nr = seq // tr
    cs, w1 = _dft_constants(seq)
    return pl.pallas_call(
        functools.partial(_fnet_kernel, seq),
        grid=(nbatch, nr),
        in_specs=[
            pl.BlockSpec((seq, FNET_W), lambda b, r: (block0 + b, 0)),
            pl.BlockSpec((tr, 2 * seq), lambda b, r: (r, 0)),
            pl.BlockSpec((FNET_W, 2 * FNET_W), lambda b, r: (0, 0)),
            pl.BlockSpec((None, FNET_W, FNET_W), lambda b, r: (l, 0, 0)),
        ],
        out_specs=pl.BlockSpec((tr, FNET_W), lambda b, r: (b * nr + r, 0)),
        out_shape=jax.ShapeDtypeStruct((nbatch * seq, FNET_W), _BF),
        scratch_shapes=[pltpu.VMEM((2 * seq, FNET_W), _BF)],
        compiler_params=_params(("arbitrary", "arbitrary")),
        name="fourier_mix_%d" % seq,
    )(uf, cs, w1, fw)


def _route_top2(r_ref, h, ei_ref, ew_ref):
    lg = lax.dot_general(r_ref[...].astype(_BF), h.astype(_BF), _NT, preferred_element_type=_F32)
    eid = lax.broadcasted_iota(jnp.int32, lg.shape, 0)
    m1 = jnp.max(lg, axis=0, keepdims=True)
    i1 = jnp.min(jnp.where(lg == m1, eid, N_EXPERTS), axis=0, keepdims=True)
    lg2 = jnp.where(eid == i1, -jnp.inf, lg)
    m2 = jnp.max(lg2, axis=0, keepdims=True)
    i2 = jnp.min(jnp.where(lg2 == m2, eid, N_EXPERTS), axis=0, keepdims=True)
    e = jnp.exp(m2 - m1)
    ei_ref[0:1, :] = i1
    ei_ref[1:2, :] = i2
    ew_ref[0:1, :] = 1.0 / (1.0 + e)
    ew_ref[1:2, :] = e / (1.0 + e)


def _outproj_kernel(split, route, *refs):
    branch_refs, refs = refs[:6], refs[6:]
    n_x = 2 if split else 1
    x_refs, refs = refs[:n_x], refs[n_x:]
    mod_ref, w_ref, gf_ref = refs[:3]
    refs = refs[3:]
    if route:
        r_ref, xo_ref, h_ref, ei_ref, ew_ref, wb_ref = refs
    else:
        xo_ref, h_ref, wb_ref = refs
    i = pl.program_id(0)

    @pl.when(i == 0)
    def _():
        wb_ref[...] = w_ref[...].astype(_BF)

    mix = jnp.concatenate([_pick_half(i, ROW_TILE, branch_refs[2 * n], branch_refs[2 * n + 1])
                           for n in range(3)], axis=-1)
    x = _pick_half(i, ROW_TILE, *x_refs) if split else x_refs[0][...]
    x_new = x + mod_ref[2:3, :] * jnp.dot(mix, wb_ref[...], preferred_element_type=_F32)
    xo_ref[...] = x_new
    h = _norm_mod(x_new, gf_ref[...], mod_ref[3:4, :], mod_ref[4:5, :])
    h_ref[...] = h.astype(h_ref.dtype)
    if route:
        _route_top2(r_ref, h, ei_ref, ew_ref)


def _outproj(branches, xs, l, mods, w_all, gf_all, h_dtype, router_t=None):
    split = len(xs) == 2
    route = router_t is not None
    row = lambda i: (i, 0)
    in_specs = []
    args = []
    for pair, width in zip(branches, (ATTN_W, CONV_W, FNET_W)):
        in_specs += _half_specs(width, ROW_TILE)
        args += list(pair)
    in_specs += _half_specs(D_MODEL, ROW_TILE) if split else [pl.BlockSpec((ROW_TILE, D_MODEL), row)]
    args += list(xs)
    mod_spec, gf_spec, w_spec = _layer_specs(l, (D_MODEL, D_MODEL))
    in_specs += [mod_spec, w_spec, gf_spec]
    args += [mods, w_all, gf_all.reshape(DEPTH, 1, D_MODEL)]
    out_specs = [pl.BlockSpec((ROW_TILE, D_MODEL), row)] * 2
    out_shape = [jax.ShapeDtypeStruct((T, D_MODEL), _F32), jax.ShapeDtypeStruct((T, D_MODEL), h_dtype)]
    if route:
        in_specs.append(pl.BlockSpec((N_EXPERTS, D_MODEL), lambda i: (0, 0)))
        args.append(router_t)
        out_specs += [pl.BlockSpec((TOP_K, ROW_TILE), lambda i: (0, i))] * 2
        out_shape += [jax.ShapeDtypeStruct((TOP_K, T), jnp.int32), jax.ShapeDtypeStruct((TOP_K, T), _F32)]
    return pl.pallas_call(
        functools.partial(_outproj_kernel, split, route),
        grid=(T // ROW_TILE,),
        in_specs=in_specs,
        out_specs=out_specs,
        out_shape=out_shape,
        scratch_shapes=[pltpu.VMEM((D_MODEL, D_MODEL), _BF)],
        compiler_params=_params(("arbitrary",)),
        name="outproj_route" if route else "outproj",
    )(*args)


def _ffn_kernel(nj, n_sub_rows, vis_e, vis_start, vis_cnt, used_sub,
                x_hbm, wg_ref, wu_ref, wd_ref, y_hbm,
                big, xb, wgb, wub, wdb, sem_in, sem_out):
    del vis_e
    v = pl.program_id(0)
    j = pl.program_id(1)
    cnt = vis_cnt[v]
    row0 = vis_start[v] * SUB

    x_is_bf16 = x_hbm.dtype == _BF
    landing = xb if x_is_bf16 else big

    def copy_in(s):
        r = pl.multiple_of(row0 + s * SUB, SUB)
        b = pl.multiple_of(s * SUB, SUB)
        return pltpu.make_async_copy(x_hbm.at[pl.ds(r, SUB)], landing.at[pl.ds(b, SUB)], sem_in.at[s])

    def copy_out(s):
        r = pl.multiple_of(row0 + s * SUB, SUB)
        b = pl.multiple_of(s * SUB, SUB)
        return pltpu.make_async_copy(big.at[pl.ds(b, SUB)], y_hbm.at[pl.ds(r, SUB)], sem_out)

    def for_subs(fn):
        lax.fori_loop(0, cnt, lambda s, c: (fn(s), c)[1], 0)

    @pl.when(jnp.logical_and(j == 0, cnt > 0))
    def _():
        for_subs(lambda s: copy_in(s).start())

    @pl.when(cnt > 0)
    def _():
        wgb[...] = wg_ref[...].astype(_BF)
        wub[...] = wu_ref[...].astype(_BF)
        wdb[...] = wd_ref[...].astype(_BF)

        def block(r0, nrows):
            s0 = r0 // SUB

            @pl.when(j == 0)
            def _():
                for t in range(nrows // SUB):
                    copy_in(s0 + t).wait()
                    rs = pl.ds(pl.multiple_of(r0 + t * SUB, SUB), SUB)
                    if not x_is_bf16:
                        xb[rs, :] = big[rs, :].astype(_BF)
                    big[rs, :] = jnp.zeros((SUB, D_MODEL), _F32)

            rs = pl.ds(r0, nrows)
            x = xb[rs, :]
            a = jnp.dot(x, wgb[...], preferred_element_type=_F32)
            b = jnp.dot(x, wub[...], preferred_element_type=_F32)
            p = (a * _sigmoid(a) * b).astype(_BF)
            big[rs, :] += jnp.dot(p, wdb[...], preferred_element_type=_F32)

            @pl.when(j == nj - 1)
            def _():
                for t in range(nrows // SUB):
                    copy_out(s0 + t).start()

        n_big = cnt // BLK_MIN
        rem = cnt - n_big * BLK_MIN
        room = BLK_SUBS - BLK_MIN
        grow = jnp.minimum(rem, n_big * room)
        left = rem - grow

        def one_block(i, r0):
            size = BLK_MIN + jnp.clip(grow - i * room, 0, room)
            for sz in range(BLK_MIN, BLK_SUBS + 1):
                pl.when(size == sz)(functools.partial(block, pl.multiple_of(r0, SUB), sz * SUB))
            return r0 + size * SUB
        r0 = lax.fori_loop(0, n_big, one_block, 0)
        bit = BLK_MIN // 2
        while bit >= 1:
            take = (left // bit) % 2 == 1
            pl.when(take)(functools.partial(block, pl.multiple_of(r0, SUB), bit * SUB))
            r0 = r0 + jnp.where(take, bit * SUB, 0)
            bit //= 2

    @pl.when(jnp.logical_and(j == nj - 1, cnt > 0))
    def _():
        for_subs(lambda s: copy_out(s).wait())

    if x_hbm.dtype == y_hbm.dtype:
        @pl.when(jnp.logical_and(v == pl.num_programs(0) - 1, j == nj - 1))
        def _():
            def tail(s):
                r = pl.multiple_of(s * SUB, SUB)
                return pltpu.make_async_copy(x_hbm.at[pl.ds(r, SUB)], y_hbm.at[pl.ds(r, SUB)], sem_out)
            lax.fori_loop(used_sub[0], n_sub_rows, lambda s, c: (tail(s).start(), c)[1], 0)
            lax.fori_loop(used_sub[0], n_sub_rows, lambda s, c: (tail(s).wait(), c)[1], 0)


def _grouped_ffn(x, wg, wu, wd, fc, vis_e, vis_start, vis_cnt, used_sub):
    ff = wg.shape[-1]
    nj = ff // fc
    nv = vis_e.shape[0]
    n_rows = x.shape[0]

    def chunk(v, j, vc):
        return jnp.where(vc[v] > 0, j, nj - 1)

    return pl.pallas_call(
        functools.partial(_ffn_kernel, nj, n_rows // SUB),
        grid_spec=pltpu.PrefetchScalarGridSpec(
            num_scalar_prefetch=4,
            grid=(nv, nj),
            in_specs=[
                pl.BlockSpec(memory_space=pl.ANY),
                pl.BlockSpec((None, D_MODEL, fc), lambda v, j, ve, vs, vc, us: (ve[v], 0, chunk(v, j, vc))),
                pl.BlockSpec((None, D_MODEL, fc), lambda v, j, ve, vs, vc, us: (ve[v], 0, chunk(v, j, vc))),
                pl.BlockSpec((None, fc, D_MODEL), lambda v, j, ve, vs, vc, us: (ve[v], chunk(v, j, vc), 0)),
            ],
            out_specs=pl.BlockSpec(memory_space=pl.ANY),
            scratch_shapes=[
                pltpu.VMEM((TMAX, D_MODEL), _F32),
                pltpu.VMEM((TMAX, D_MODEL), _BF),
                pltpu.VMEM((D_MODEL, fc), _BF),
                pltpu.VMEM((D_MODEL, fc), _BF),
                pltpu.VMEM((fc, D_MODEL), _BF),
                pltpu.SemaphoreType.DMA((SUB_MAX,)),
                pltpu.SemaphoreType.DMA(()),
            ],
        ),
        out_shape=jax.ShapeDtypeStruct((n_rows, D_MODEL), _F32),
        compiler_params=_params(("arbitrary", "arbitrary"), FFN_VMEM_LIMIT),
        name="ffn_%d" % ff,
    )(vis_e, vis_start, vis_cnt, used_sub, x, wg, wu, wd)


def _dense_visits():
    nv = T // TMAX
    return (jnp.zeros((nv,), jnp.int32),
            jnp.arange(nv, dtype=jnp.int32) * SUB_MAX,
            jnp.full((nv,), SUB_MAX, jnp.int32),
            jnp.full((1,), T // SUB, jnp.int32))


N_SLOT_SUB = -(-(TOP_K * T + N_TILES * N_EXPERTS * (SEG_ALIGN - 1)) // SUB) + N_EXPERTS
N_SLOT = N_SLOT_SUB * SUB
N_VISIT = -(-N_SLOT_SUB // SUB_MAX) + N_EXPERTS


def _routing_plan(eidx):
    e_loc = eidx.reshape(TOP_K, N_TILES, DISP_TILE).transpose(1, 0, 2).reshape(N_TILES, PAIRS)
    onehot = e_loc[:, :, None] == jnp.arange(N_EXPERTS, dtype=jnp.int32)[None, None, :]
    ch = 128
    oh = onehot.astype(_F32).reshape(N_TILES, PAIRS // ch, ch, N_EXPERTS)
    tri = (jnp.arange(ch)[:, None] >= jnp.arange(ch)[None, :]).astype(_F32)
    within = jnp.einsum("ij,tcjk->tcik", tri, oh)
    tot = within[:, :, -1, :]
    csum = (within + (jnp.cumsum(tot, axis=1) - tot)[:, :, None, :]).reshape(N_TILES, PAIRS, N_EXPERTS)
    csum = csum.astype(jnp.int32)
    n_te = (csum[:, -1, :] + SEG_ALIGN - 1) // SEG_ALIGN * SEG_ALIGN
    src = jnp.cumsum(n_te, axis=1) - n_te
    lpos = jnp.sum(jnp.where(onehot, csum - 1 + src[:, None, :], 0), axis=2).astype(jnp.int32)
    counts = jnp.sum(n_te, axis=0)
    nsub = (counts + SUB - 1) // SUB
    sub_base = jnp.cumsum(nsub) - nsub
    dst = (sub_base * SUB)[None, :] + jnp.cumsum(n_te, axis=0) - n_te
    seg = tuple(a.reshape(-1).astype(jnp.int32) for a in (n_te, src, dst))
    pads = ((sub_base * SUB + counts).astype(jnp.int32), (nsub * SUB - counts).astype(jnp.int32))
    used_sub = jnp.sum(nsub).reshape(1).astype(jnp.int32)
    nvis = (nsub + SUB_MAX - 1) // SUB_MAX
    vend = jnp.cumsum(nvis)
    total = vend[-1]
    vid = jnp.arange(N_VISIT, dtype=jnp.int32)
    ve = jnp.minimum(jnp.sum((vid[:, None] >= vend[None, :]).astype(jnp.int32), axis=1), N_EXPERTS - 1)
    local = vid - (vend - nvis)[ve]
    nv_e = jnp.maximum(nvis[ve], 1)
    q, r = nsub[ve] // nv_e, nsub[ve] % nv_e
    cnt = q + (local < r).astype(jnp.int32)
    start = sub_base[ve] + local * q + jnp.minimum(local, r)
    used = vid < total
    last_e = ve[jnp.maximum(total - 1, 0)]
    vis_e = jnp.where(used, ve, last_e).astype(jnp.int32)
    vis_cnt = jnp.where(used, cnt, 0).astype(jnp.int32)
    vis_start = jnp.where(used, start, 0).astype(jnp.int32)
    return lpos.reshape(N_TILES, TOP_K, DISP_TILE), seg, pads, (vis_e, vis_start, vis_cnt, used_sub)


def _for_pow2_pieces(n, max_piece, fn):
    off = 0
    bit = max_piece
    while bit >= SEG_ALIGN:
        take = (n // bit) % 2 == 1
        pl.when(take)(functools.partial(fn, off, bit))
        off = off + jnp.where(take, bit, 0)
        bit //= 2


def _segment_copies(tile, seg, make_copy, wait=False):
    n_te, src, dst = seg
    for e in range(N_EXPERTS):
        j = tile * N_EXPERTS + e
        s0, d0 = src[j], dst[j]

        def piece(off, size, s0=s0, d0=d0):
            cp = make_copy(pl.multiple_of(s0 + off, SEG_ALIGN), pl.multiple_of(d0 + off, SEG_ALIGN), size)
            cp.wait() if wait else cp.start()
        _for_pow2_pieces(n_te[j], DISP_TILE, piece)


def _dispatch_kernel(n_te, src, dst, pad_start, pad_cnt, used_sub, h_ref, lpos_ref, xs_hbm, ring, ring_sems, sem):
    i = pl.program_id(0)
    last = pl.num_programs(0) - 1
    b = i % 2

    def copies(tile, slot_id, wait):
        _segment_copies(tile, (n_te, src, dst), lambda s, d, size: pltpu.make_async_copy(
            ring.at[slot_id, pl.ds(s, size)], xs_hbm.at[pl.ds(d, size)], ring_sems.at[slot_id]), wait)

    pl.when(i >= 2)(lambda: copies(i - 2, b, True))
    lp = lpos_ref[...]
    srow = lax.broadcasted_iota(jnp.int32, (SORT_ROWS, DISP_TILE), 0)
    perm = jnp.where(lp[0:1, :] == srow, 1.0, jnp.where(lp[1:2, :] == srow, 1.0, 0.0)).astype(_BF)
    ring[b] = jnp.dot(perm, h_ref[...].astype(_BF), preferred_element_type=_F32)
    copies(i, b, False)

    @pl.when(i == last)
    def _():
        copies(i - 1, 1 - b, True)
        copies(i, b, True)
        for e in range(N_EXPERTS):
            p0 = pad_start[e]

            def fill(off, size, p0=p0):
                cp = pltpu.make_async_copy(ring.at[b, pl.ds(0, size)],
                                           xs_hbm.at[pl.ds(pl.multiple_of(p0 + off, SEG_ALIGN), size)], sem)
                cp.start()
                cp.wait()
            _for_pow2_pieces(pad_cnt[e], SUB // 2, fill)

        def tail(s):
            r = pl.multiple_of(s * SUB, SUB)
            return pltpu.make_async_copy(ring.at[b, pl.ds(0, SUB)], xs_hbm.at[pl.ds(r, SUB)], sem)
        lax.fori_loop(used_sub[0], N_SLOT_SUB, lambda s, c: (tail(s).start(), c)[1], 0)
        lax.fori_loop(used_sub[0], N_SLOT_SUB, lambda s, c: (tail(s).wait(), c)[1], 0)


def _dispatch(h, lpos, seg, pads, used_sub):
    return pl.pallas_call(
        _dispatch_kernel,
        grid_spec=pltpu.PrefetchScalarGridSpec(
            num_scalar_prefetch=6,
            grid=(N_TILES,),
            in_specs=[pl.BlockSpec((DISP_TILE, D_MODEL), lambda i, *_: (i, 0)),
                      pl.BlockSpec((None, TOP_K, DISP_TILE), lambda i, *_: (i, 0, 0))],
            out_specs=pl.BlockSpec(memory_space=pl.ANY),
            scratch_shapes=[pltpu.VMEM((2, SORT_ROWS, D_MODEL), _F32),
                            pltpu.SemaphoreType.DMA((2,)), pltpu.SemaphoreType.DMA(())],
        ),
        out_shape=jax.ShapeDtypeStruct((N_SLOT, D_MODEL), _F32),
        compiler_params=_params(("arbitrary",)),
        name="dispatch",
    )(*seg, *pads, used_sub, h, lpos)


def _residual_out(x, gate, f, g_ref, o_refs, i, tile):
    out = x + gate * f
    if g_ref is not None:
        ms = jnp.mean(out * out, axis=-1, keepdims=True)
        out = out * lax.rsqrt(ms + EPS) * g_ref[...]
    if len(o_refs) == 1:
        o_refs[0][...] = out
    else:
        @pl.when(i < T_PROMPT // tile)
        def _():
            o_refs[0][...] = out

        @pl.when(i >= T_PROMPT // tile)
        def _():
            o_refs[1][...] = out


def _out_specs(final, tile):
    if not final:
        return ([pl.BlockSpec((tile, D_MODEL), lambda i, *_: (i, 0))],
                [jax.ShapeDtypeStruct((T, D_MODEL), _F32)])
    return (_half_specs(D_MODEL, tile),
            [jax.ShapeDtypeStruct((T_PROMPT, D_MODEL), _F32), jax.ShapeDtypeStruct((T_SAMPLE, D_MODEL), _F32)])


def _combine_kernel(final, x_ref, mod_ref, y_ref, *rest):
    g_ref, o_refs = (rest[0], rest[1:]) if final else (None, rest)
    _residual_out(x_ref[...], mod_ref[5:6, :], y_ref[...], g_ref, o_refs, pl.program_id(0), ROW_TILE)


def _combine(x, l, mods, y, g_final=None):
    final = g_final is not None
    row = lambda i: (i, 0)
    in_specs = [
        pl.BlockSpec((ROW_TILE, D_MODEL), row),
        _layer_specs(l, ())[0],
        pl.BlockSpec((ROW_TILE, D_MODEL), row),
    ]
    args = [x, mods, y]
    if final:
        in_specs.append(pl.BlockSpec((1, D_MODEL), lambda i: (0, 0)))
        args.append(g_final.reshape(1, D_MODEL))
    out_specs, out_shape = _out_specs(final, ROW_TILE)
    return pl.pallas_call(
        functools.partial(_combine_kernel, final),
        grid=(T // ROW_TILE,),
        in_specs=in_specs,
        out_specs=out_specs,
        out_shape=out_shape,
        compiler_params=_params(("arbitrary",)),
        name="combine%s" % ("_final" if final else ""),
    )(*args)


def _combine_top2_kernel(final, n_out, n_te, src, dst, x_ref, mod_ref, w_ref, lpos_ref, *rest):
    g_ref, rest = (rest[0], rest[1:]) if final else (None, rest)
    ys_hbm, o_refs, (ybuf, sems) = rest[0], rest[1:1 + n_out], rest[1 + n_out:]
    i = pl.program_id(0)

    def fetch(tile, b, wait=False):
        _segment_copies(tile, (n_te, src, dst), lambda s, d, size: pltpu.make_async_copy(
            ys_hbm.at[pl.ds(d, size)], ybuf.at[b, pl.ds(s, size)], sems.at[b]), wait)

    @pl.when(i == 0)
    def _():
        for slot_id in range(2):
            ybuf[slot_id, PAIRS:SORT_ROWS, :] = jnp.zeros((SORT_ROWS - PAIRS, D_MODEL), _F32)
        fetch(0, 0)

    @pl.when(i + 1 < pl.num_programs(0))
    def _():
        fetch(i + 1, (i + 1) % 2)

    b = i % 2
    fetch(i, b, wait=True)
    y = ybuf[b].astype(_BF)
    lp = lpos_ref[...]
    scol = lax.broadcasted_iota(jnp.int32, (DISP_TILE, SORT_ROWS), 1)
    w = w_ref[...]
    pick = jnp.zeros((DISP_TILE, SORT_ROWS), _F32)
    for k in range(TOP_K):
        pick = jnp.where(lp[:, k:k + 1] == scol, w[:, k:k + 1], pick)
    f = jnp.dot(pick.astype(_BF), y, preferred_element_type=_F32)
    _residual_out(x_ref[...], mod_ref[5:6, :], f, g_ref, o_refs, i, DISP_TILE)


def _combine_top2(x, l, mods, ys, lpos_t, seg, w, g_final=None):
    final = g_final is not None
    row = lambda i, *_: (i, 0)
    in_specs = [
        pl.BlockSpec((DISP_TILE, D_MODEL), row),
        pl.BlockSpec((None, None, 6, D_MODEL), lambda i, *_: (l, _cond_of_tile(i, DISP_TILE), 0, 0)),
        pl.BlockSpec((DISP_TILE, TOP_K), row),
        pl.BlockSpec((None, DISP_TILE, TOP_K), lambda i, *_: (i, 0, 0)),
    ]
    args = [x, mods, w, lpos_t]
    if final:
        in_specs.append(pl.BlockSpec((1, D_MODEL), lambda i, *_: (0, 0)))
        args.append(g_final.reshape(1, D_MODEL))
    in_specs.append(pl.BlockSpec(memory_space=pl.ANY))
    args.append(ys)
    out_specs, out_shape = _out_specs(final, DISP_TILE)
    return pl.pallas_call(
        functools.partial(_combine_top2_kernel, final, len(out_specs)),
        grid_spec=pltpu.PrefetchScalarGridSpec(
            num_scalar_prefetch=3,
            grid=(N_TILES,),
            in_specs=in_specs,
            out_specs=out_specs,
            scratch_shapes=[
                pltpu.VMEM((2, SORT_ROWS, D_MODEL), _F32),
                pltpu.SemaphoreType.DMA((2,)),
            ],
        ),
        out_shape=out_shape,
        compiler_params=_params(("arbitrary",)),
        name="combine_top2%s" % ("_final" if final else ""),
    )(*seg, *args)


def kernel(x_prompt, x_sample, cache_k, cache_v, c, c_ctx, w_ada, b_ada, g_norm_mix, g_norm_ffn,
           w_in, w_out, attn_sink, conv_dw, conv_dw_b, conv_ln_g, conv_ln_b, conv_pw, fnet_w,
           ffn_w_gate, ffn_w_up, ffn_w_down, moe_router, moe_w_gate, moe_w_up, moe_w_down, g_final):
    xs = (x_prompt.reshape(T_PROMPT, D_MODEL), x_sample.reshape(T_SAMPLE, D_MODEL))
    cond8 = jnp.concatenate([c_ctx[None, :], c, jnp.zeros((N_COND - 1 - DEC_BATCH, D_MODEL), _F32)], axis=0)
    mods = _modulation(cond8, w_ada, b_ada).reshape(DEPTH, N_COND, 6, D_MODEL)
    cos_t, sin_t = _rope_tables()
    ck_all = cache_k.reshape(DEC_BATCH, DEPTH, PAST_LEN, KV_W)
    cv_all = cache_v.reshape(DEC_BATCH, DEPTH, PAST_LEN, KV_W)
    p_blocks = T_PROMPT // DEC_SEQ
    sinks = attn_sink.reshape(DEPTH * N_HEADS)

    ks, vs = [], []
    resid = None
    for l in range(DEPTH):
        q, k, v, uc, uf, *x_new = _inproj(xs, l, mods, g_norm_mix, w_in, cos_t, sin_t, resid=resid)
        if resid is not None:
            xs, resid = tuple(x_new), None
        ks.append(k[:T_PROMPT])
        vs.append(v[:T_PROMPT])
        attn = (_ctx_attention(l, sinks, q, k, v),
                _lat_attention(l, sinks, q, k, v, ck_all, cv_all))
        cargs = (l, conv_dw, conv_dw_b, conv_ln_g, conv_ln_b, conv_pw)
        conv = (_conv_module(uc, SEQ, BATCH, 0, *cargs),
                _conv_module(uc, DEC_SEQ, DEC_BATCH, p_blocks, *cargs))
        four = (_fourier_mix(uf, SEQ, BATCH, 0, l, fnet_w),
                _fourier_mix(uf, DEC_SEQ, DEC_BATCH, p_blocks, l, fnet_w))
        last = g_final if l == DEPTH - 1 else None
        i = l // 2
        if l % 2 == 0:
            x, h = _outproj((attn, conv, four), xs, l, mods, w_out, g_norm_ffn, _BF)
            y = _grouped_ffn(h, ffn_w_gate[i:i + 1], ffn_w_up[i:i + 1], ffn_w_down[i:i + 1], 256,
                             *_dense_visits())
            if last is None:
                xs, resid = (x,), (y, l)
            else:
                xs = tuple(_combine(x, l, mods, y, g_final=last))
        else:
            x, h, eidx, ew = _outproj((attn, conv, four), xs, l, mods, w_out, g_norm_ffn, _F32,
                                      router_t=moe_router[i].T)
            lpos, seg, pads, visits = _routing_plan(eidx)
            xd = _dispatch(h, lpos, seg, pads, visits[-1])
            ys = _grouped_ffn(xd, moe_w_gate[i], moe_w_up[i], moe_w_down[i], 512, *visits)
            xs = tuple(_combine_top2(x, l, mods, ys, lpos.transpose(0, 2, 1), seg, ew.T, g_final=last))

    y_prompt = xs[0].reshape(BATCH, SEQ, D_MODEL)
    y_sample = xs[1].reshape(DEC_BATCH, DEC_SEQ, D_MODEL)
    state_k = jnp.stack([a.reshape(BATCH, SEQ, N_KV_HEADS, HEAD_DIM) for a in ks], axis=1)
    state_v = jnp.stack([a.reshape(BATCH, SEQ, N_KV_HEADS, HEAD_DIM) for a in vs], axis=1)
    return (y_prompt, y_sample, state_k, state_v)
```

```python
import functools

import numpy as np
import jax
import jax.numpy as jnp
from jax import lax
from jax.experimental import pallas as pl
from jax.experimental.pallas import tpu as pltpu

D_MODEL = 1024
BATCH = 16
SEQ = 256
DEPTH = 2
DEC_BATCH = 2
DEC_SEQ = 2048
PAST_LEN = 512
GRID_W = 64
HEAD_DIM = 64
N_HEADS = 8
N_KV_HEADS = 2
GQA_GROUP = N_HEADS // N_KV_HEADS
ATTN_W = N_HEADS * HEAD_DIM
KV_W = N_KV_HEADS * HEAD_DIM
WINDOW = 128
BLOCK = 128
ROPE_THETA = 10000.0
CONV_W = D_MODEL // 4
CONV_K = 31
FNET_GROUPS = 4
FNET_W = D_MODEL // 4
FNET_GW = FNET_W // FNET_GROUPS
IN_WIDTH = ATTN_W + 2 * KV_W + 2 * CONV_W + FNET_W
D_FF = 2816
N_EXPERTS = 8
TOP_K = 2
D_FF_EXPERT = 3584
EPS = 1e-6
NEG = -1e30

T_PROMPT = BATCH * SEQ
T_SAMPLE = DEC_BATCH * DEC_SEQ
T = T_PROMPT + T_SAMPLE
N_COND = 8

ROW_TILE = 512
SUB = 128
SUB_MAX = 32
TMAX = SUB * SUB_MAX
BLK_MIN = 8
BLK_SUBS = 10
DISP_TILE = 512
N_TILES = T // DISP_TILE
PAIRS = TOP_K * DISP_TILE
SEG_ALIGN = 8
SORT_ROWS = 1152
CONV_CHUNK = 64
CONV_PAD = 16
CONV_SPAN = CONV_CHUNK + 8 * ((CONV_PAD + CONV_K // 2) // 8)
VMEM_LIMIT = 48 * 1024 * 1024
FFN_VMEM_LIMIT = 56 * 1024 * 1024

_BF = jnp.bfloat16
_F32 = jnp.float32


def _cond_of_tile(i, tile):
    r = i * tile
    return jnp.where(r < T_PROMPT, 0, 1 + (r - T_PROMPT) // DEC_SEQ)


def _params(sem, vmem=VMEM_LIMIT):
    return pltpu.CompilerParams(dimension_semantics=sem, vmem_limit_bytes=vmem)


def _sigmoid(x):
    return 1.0 / (1.0 + jnp.exp(-x))


def _mod_kernel(cond_ref, w_ref, b_ref, o_ref):
    cnd = cond_ref[...]
    s = (cnd * _sigmoid(cnd)).astype(_BF)
    o_ref[...] = jnp.dot(s, w_ref[...].astype(_BF), preferred_element_type=_F32) + b_ref[...]


def _modulation(cond8, w_ada, b_ada):
    nt = 1536
    return pl.pallas_call(
        _mod_kernel,
        grid=(DEPTH, 6 * D_MODEL // nt),
        in_specs=[
            pl.BlockSpec((N_COND, D_MODEL), lambda l, n: (0, 0)),
            pl.BlockSpec((None, D_MODEL, nt), lambda l, n: (l, 0, n)),
            pl.BlockSpec((None, 1, nt), lambda l, n: (l, 0, n)),
        ],
        out_specs=pl.BlockSpec((None, N_COND, nt), lambda l, n: (l, 0, n)),
        out_shape=jax.ShapeDtypeStruct((DEPTH, N_COND, 6 * D_MODEL), _F32),
        compiler_params=_params(("arbitrary", "arbitrary")),
        name="modulation",
    )(cond8, w_ada, b_ada.reshape(DEPTH, 1, 6 * D_MODEL))


def _norm_mod(x, g, shift, scale):
    ms = jnp.mean(x * x, axis=-1, keepdims=True)
    y = x * lax.rsqrt(ms + EPS) * g
    return y * (1.0 + scale) + shift


def _rope_tables():
    rows = DEC_SEQ // GRID_W
    n_freq = HEAD_DIM // 4
    inv = ROPE_THETA ** (-jnp.arange(n_freq, dtype=_F32) / n_freq)
    gr, gc = jnp.meshgrid(jnp.arange(rows, dtype=_F32), jnp.arange(GRID_W, dtype=_F32), indexing="ij")
    ang_r = gr.reshape(-1)[:, None] * inv
    ang_c = gc.reshape(-1)[:, None] * inv
    cr, sr, cc, sc = jnp.cos(ang_r), jnp.sin(ang_r), jnp.cos(ang_c), jnp.sin(ang_c)
    cos64 = jnp.concatenate([cr, cr, cc, cc], axis=-1)
    sin64 = jnp.concatenate([-sr, sr, -sc, sc], axis=-1)
    return jnp.tile(cos64, (1, 2)), jnp.tile(sin64, (1, 2))


def _rope128(x, cos, sin):
    lane = lax.broadcasted_iota(jnp.int32, x.shape, 1)
    first = (lane % 32) < 16
    partner = jnp.where(first, pltpu.roll(x, 128 - 16, 1), pltpu.roll(x, 16, 1))
    return x * cos + partner * sin


def _half_specs(width, tile):
    pt = T_PROMPT // tile
    return [pl.BlockSpec((tile, width), lambda i, *_: (jnp.minimum(i, pt - 1), 0)),
            pl.BlockSpec((tile, width), lambda i, *_: (jnp.maximum(i - pt, 0), 0))]


def _pick_half(i, tile, a_ref, b_ref):
    return jnp.where(i < T_PROMPT // tile, a_ref[...], b_ref[...])


def _layer_specs(l, w_shape):
    return [
        pl.BlockSpec((None, None, 6, D_MODEL), lambda i, *_: (l, _cond_of_tile(i, ROW_TILE), 0, 0)),
        pl.BlockSpec((None, 1, D_MODEL), lambda i, *_: (l, 0, 0)),
        pl.BlockSpec((None,) + w_shape, lambda i, *_: (l, 0, 0)),
    ]


def _inproj_kernel(split, resid, *refs):
    n_x = 2 if split else 1
    x_refs, refs = refs[:n_x], refs[n_x:]
    if resid:
        y_ref, modp_ref = refs[:2]
        refs = refs[2:]
    mod_ref, g_ref, w_ref, cos_ref, sin_ref, q_ref, k_ref, v_ref, uc_ref, uf_ref = refs[:10]
    wb_ref = refs[-1]
    i = pl.program_id(0)

    @pl.when(i == 0)
    def _():
        wb_ref[...] = w_ref[...].astype(_BF)

    x = _pick_half(i, ROW_TILE, *x_refs) if split else x_refs[0][...]
    if resid:
        x = x + modp_ref[5:6, :] * y_ref[...]
        refs[10][...] = x
    h = _norm_mod(x, g_ref[...], mod_ref[0:1, :], mod_ref[1:2, :]).astype(_BF)
    latent = i >= T_PROMPT // ROW_TILE
    cos = jnp.where(latent, cos_ref[...], 1.0)
    sin = jnp.where(latent, sin_ref[...], 0.0)

    def proj(o, n):
        return jnp.dot(h, wb_ref[:, o:o + n], preferred_element_type=_F32)

    o = 0
    for c in range(ATTN_W // 256):
        qq = proj(o, 256)
        q_ref[:, o:o + 128] = _rope128(qq[:, :128], cos, sin).astype(q_ref.dtype)
        q_ref[:, o + 128:o + 256] = _rope128(qq[:, 128:], cos, sin).astype(q_ref.dtype)
        o += 256
    kv = proj(o, 2 * KV_W)
    k_ref[...] = _rope128(kv[:, :KV_W], cos, sin)
    v_ref[...] = kv[:, KV_W:]
    o += 2 * KV_W
    uc_ref[...] = proj(o, 2 * CONV_W)
    o += 2 * CONV_W
    uf_ref[...] = proj(o, FNET_W).astype(uf_ref.dtype)


def _inproj(xs, l, mods, g_all, w_all, cos_t, sin_t, resid=None):
    split = len(xs) == 2
    pt = T_PROMPT // ROW_TILE
    per_seq = DEC_SEQ // ROW_TILE

    def rope_idx(i):
        return (jnp.maximum(i - pt, 0) % per_seq, 0)

    row = lambda i: (i, 0)
    in_specs = _half_specs(D_MODEL, ROW_TILE) if split else [pl.BlockSpec((ROW_TILE, D_MODEL), row)]
    args = list(xs)
    outs = [ATTN_W, KV_W, KV_W, 2 * CONV_W, FNET_W]
    dtypes = [_BF, _F32, _F32, _F32, _BF]
    if resid is not None:
        y, l_prev = resid
        in_specs += [pl.BlockSpec((ROW_TILE, D_MODEL), row), _layer_specs(l_prev, ())[0]]
        args += [y, mods]
        outs.append(D_MODEL)
        dtypes.append(_F32)
    return pl.pallas_call(
        functools.partial(_inproj_kernel, split, resid is not None),
        grid=(T // ROW_TILE,),
        in_specs=in_specs + _layer_specs(l, (D_MODEL, IN_WIDTH)) + [
            pl.BlockSpec((ROW_TILE, 128), rope_idx),
            pl.BlockSpec((ROW_TILE, 128), rope_idx),
        ],
        out_specs=[pl.BlockSpec((ROW_TILE, n), row) for n in outs],
        out_shape=[jax.ShapeDtypeStruct((T, n), dt) for n, dt in zip(outs, dtypes)],
        scratch_shapes=[pltpu.VMEM((D_MODEL, IN_WIDTH), _BF)],
        compiler_params=_params(("arbitrary",)),
        name="inproj_resid" if resid is not None else "inproj",
    )(*args, mods, g_all.reshape(DEPTH, 1, D_MODEL), w_all, cos_t, sin_t)


def _stack_groups(q, kh, rows):
    parts = [q[:, (kh * GQA_GROUP + g) * HEAD_DIM:(kh * GQA_GROUP + g + 1) * HEAD_DIM] for g in range(GQA_GROUP)]
    return jnp.concatenate(parts, axis=0)


def _sink_column(sink_ref, head0, rows):
    r = lax.broadcasted_iota(jnp.int32, (GQA_GROUP * rows, 1), 0)
    col = jnp.full((GQA_GROUP * rows, 1), sink_ref[head0], _F32)
    for g in range(1, GQA_GROUP):
        col = jnp.where(r >= g * rows, sink_ref[head0 + g], col)
    return col


def _unstack_store(o_ref, o, kh, rows):
    for pair in range(GQA_GROUP // 2):
        a = o[(2 * pair) * rows:(2 * pair + 1) * rows]
        b = o[(2 * pair + 1) * rows:(2 * pair + 2) * rows]
        c0 = (kh * GQA_GROUP + 2 * pair) * HEAD_DIM
        o_ref[:, c0:c0 + 2 * HEAD_DIM] = jnp.concatenate([a, b], axis=-1).astype(o_ref.dtype)


_NT = (((1,), (1,)), ((), ()))


def _ctx_attn_kernel(l, sink_ref, q_ref, k_ref, v_ref, o_ref):
    scale = HEAD_DIM ** -0.5
    q = q_ref[...] * scale
    for kh in range(N_KV_HEADS):
        kk = k_ref[:, kh * HEAD_DIM:(kh + 1) * HEAD_DIM].astype(_BF)
        vv = v_ref[:, kh * HEAD_DIM:(kh + 1) * HEAD_DIM].astype(_BF)
        qs = _stack_groups(q, kh, SEQ).astype(_BF)
        s = lax.dot_general(qs, kk, _NT, preferred_element_type=_F32)
        sink = _sink_column(sink_ref, l * N_HEADS + kh * GQA_GROUP, SEQ)
        m = jnp.maximum(jnp.max(s, axis=-1, keepdims=True), sink)
        p = jnp.exp(s - m)
        den = jnp.sum(p, axis=-1, keepdims=True) + jnp.exp(sink - m)
        o = jnp.dot(p.astype(_BF), vv, preferred_element_type=_F32) / den
        _unstack_store(o_ref, o, kh, SEQ)


def _ctx_attention(l, sinks, q, k, v):
    row = lambda b, s: (b, 0)
    return pl.pallas_call(
        functools.partial(_ctx_attn_kernel, l),
        grid_spec=pltpu.PrefetchScalarGridSpec(
            num_scalar_prefetch=1,
            grid=(BATCH,),
            in_specs=[
                pl.BlockSpec((SEQ, ATTN_W), row),
                pl.BlockSpec((SEQ, KV_W), row),
                pl.BlockSpec((SEQ, KV_W), row),
            ],
            out_specs=pl.BlockSpec((SEQ, ATTN_W), row),
        ),
        out_shape=jax.ShapeDtypeStruct((T_PROMPT, ATTN_W), _BF),
        compiler_params=_params(("arbitrary",)),
        name="ctx_attention",
    )(sinks, q, k, v)


def _lat_attn_kernel(l, sink_ref, q_ref, k_ref, v_ref, ck_ref, cv_ref, o_ref):
    n = pl.program_id(1)
    scale = HEAD_DIM ** -0.5
    nb = DEC_SEQ // BLOCK
    band = 3 * BLOCK
    start = pl.multiple_of(jnp.clip(n - 1, 0, nb - 3) * BLOCK, BLOCK)
    q = q_ref[...] * scale
    kb = k_ref[pl.ds(start, band), :]
    vb = v_ref[pl.ds(start, band), :]
    dpos = (lax.broadcasted_iota(jnp.int32, (BLOCK, band), 1) - lax.broadcasted_iota(jnp.int32, (BLOCK, band), 0)
            + (start - n * BLOCK))
    in_window = jnp.where(jnp.abs(dpos) <= WINDOW, 0.0, NEG)
    bias = jnp.concatenate([in_window] * GQA_GROUP, axis=0)
    for kh in range(N_KV_HEADS):
        hs = slice(kh * HEAD_DIM, (kh + 1) * HEAD_DIM)
        qs = _stack_groups(q, kh, BLOCK).astype(_BF)
        s_loc = lax.dot_general(qs, kb[:, hs].astype(_BF), _NT, preferred_element_type=_F32)
        s_loc = jnp.where(bias < 0.0, NEG, s_loc)
        s_ctx = lax.dot_general(qs, ck_ref[:, hs].astype(_BF), _NT, preferred_element_type=_F32)
        sink = _sink_column(sink_ref, l * N_HEADS + kh * GQA_GROUP, BLOCK)
        m = jnp.maximum(jnp.maximum(jnp.max(s_loc, axis=-1, keepdims=True),
                                    jnp.max(s_ctx, axis=-1, keepdims=True)), sink)
        p_loc = jnp.exp(s_loc - m)
        p_ctx = jnp.exp(s_ctx - m)
        den = (jnp.sum(p_loc, axis=-1, keepdims=True) + jnp.sum(p_ctx, axis=-1, keepdims=True)
               + jnp.exp(sink - m))
        o = (jnp.dot(p_loc.astype(_BF), vb[:, hs].astype(_BF), preferred_element_type=_F32)
             + jnp.dot(p_ctx.astype(_BF), cv_ref[:, hs].astype(_BF), preferred_element_type=_F32)) / den
        _unstack_store(o_ref, o, kh, BLOCK)


def _lat_attention(l, sinks, q, k, v, ck, cv):
    nb = DEC_SEQ // BLOCK
    q0 = T_PROMPT // BLOCK
    s0 = T_PROMPT // DEC_SEQ
    return pl.pallas_call(
        functools.partial(_lat_attn_kernel, l),
        grid_spec=pltpu.PrefetchScalarGridSpec(
            num_scalar_prefetch=1,
            grid=(DEC_BATCH, nb),
            in_specs=[
                pl.BlockSpec((BLOCK, ATTN_W), lambda b, n, s: (q0 + b * nb + n, 0)),
                pl.BlockSpec((DEC_SEQ, KV_W), lambda b, n, s: (s0 + b, 0)),
                pl.BlockSpec((DEC_SEQ, KV_W), lambda b, n, s: (s0 + b, 0)),
                pl.BlockSpec((None, None, PAST_LEN, KV_W), lambda b, n, s: (b, l, 0, 0)),
                pl.BlockSpec((None, None, PAST_LEN, KV_W), lambda b, n, s: (b, l, 0, 0)),
            ],
            out_specs=pl.BlockSpec((BLOCK, ATTN_W), lambda b, n, s: (b * nb + n, 0)),
        ),
        out_shape=jax.ShapeDtypeStruct((T_SAMPLE, ATTN_W), _BF),
        compiler_params=_params(("arbitrary", "arbitrary")),
        name="lat_attention",
    )(sinks, q, k, v, ck, cv)


def _conv_kernel(seq, u_ref, dw_ref, dwb_ref, lg_ref, lb_ref, pw_ref, o_ref, pad_ref, y_ref, sh_ref):
    u = u_ref[...]
    pad_ref[0:CONV_PAD, :] = jnp.zeros((CONV_PAD, CONV_W), _F32)
    pad_ref[CONV_PAD + seq:2 * CONV_PAD + seq, :] = jnp.zeros((CONV_PAD, CONV_W), _F32)
    pad_ref[CONV_PAD:CONV_PAD + seq, :] = u[:, :CONV_W] * _sigmoid(u[:, CONV_W:])
    off = CONV_PAD - CONV_K // 2
    span = CONV_SPAN

    def chunk(c, carry):
        row = pl.multiple_of(c * CONV_CHUNK, CONV_CHUNK)
        win = pad_ref[pl.ds(row, CONV_CHUNK + 2 * CONV_PAD), :]
        acc = jnp.zeros((CONV_CHUNK, CONV_W), _F32) + dwb_ref[...]
        for phase in range(8):
            sh_ref[phase] = win[phase:phase + span, :]
        for phase in range(8):
            for a in range(span // 8):
                t = 8 * a + phase - off
                if 0 <= t < CONV_K:
                    acc = acc + sh_ref[phase, 8 * a:8 * a + CONV_CHUNK, :] * dw_ref[t:t + 1, :]
        mu = jnp.mean(acc, axis=-1, keepdims=True)
        d = acc - mu
        var = jnp.mean(d * d, axis=-1, keepdims=True)
        y = d * lax.rsqrt(var + EPS) * lg_ref[...] + lb_ref[...]
        y_ref[pl.ds(row, CONV_CHUNK), :] = (y * _sigmoid(y)).astype(_BF)
        return carry

    lax.fori_loop(0, seq // CONV_CHUNK, chunk, 0)
    o_ref[...] = jnp.dot(y_ref[...], pw_ref[...].astype(_BF), preferred_element_type=_F32).astype(o_ref.dtype)


def _conv_module(uc, seq, nbatch, block0, l, dw, dwb, lg, lb, pw):
    vec = lambda a: a.reshape(DEPTH, 1, CONV_W)
    layer = lambda b: (l, 0, 0)
    return pl.pallas_call(
        functools.partial(_conv_kernel, seq),
        grid=(nbatch,),
        in_specs=[
            pl.BlockSpec((seq, 2 * CONV_W), lambda b: (block0 + b, 0)),
            pl.BlockSpec((None, CONV_K, CONV_W), layer),
            pl.BlockSpec((None, 1, CONV_W), layer),
            pl.BlockSpec((None, 1, CONV_W), layer),
            pl.BlockSpec((None, 1, CONV_W), layer),
            pl.BlockSpec((None, CONV_W, CONV_W), layer),
        ],
        out_specs=pl.BlockSpec((seq, CONV_W), lambda b: (b, 0)),
        out_shape=jax.ShapeDtypeStruct((nbatch * seq, CONV_W), _BF),
        scratch_shapes=[pltpu.VMEM((seq + 2 * CONV_PAD, CONV_W), _F32), pltpu.VMEM((seq, CONV_W), _BF),
                        pltpu.VMEM((8, CONV_SPAN, CONV_W), _F32)],
        compiler_params=_params(("arbitrary",)),
        name="conv_module_%d" % seq,
    )(uc, dw, vec(dwb), vec(lg), vec(lb), pw)


def _dft_constants(seq):
    j = np.arange(seq, dtype=np.int64)
    ang = 2.0 * np.pi * ((j[:, None] * j[None, :]) % seq).astype(np.float64) / seq
    cs = np.concatenate([np.cos(ang), -np.sin(ang)], axis=1) / np.sqrt(seq)
    c = np.arange(FNET_GW, dtype=np.int64)
    angc = 2.0 * np.pi * ((c[:, None] * c[None, :]) % FNET_GW).astype(np.float64) / FNET_GW
    eye = np.eye(FNET_GROUPS)
    cc = np.kron(eye, np.cos(angc)) / np.sqrt(FNET_GW)
    sc = np.kron(eye, np.sin(angc)) / np.sqrt(FNET_GW)
    w1 = np.concatenate([cc, sc], axis=1)
    return jnp.asarray(cs, dtype=_F32), jnp.asarray(w1, dtype=_F32)


def _fnet_kernel(seq, u_ref, cs_ref, w1_ref, fw_ref, o_ref, xcs_ref):
    @pl.when(pl.program_id(1) == 0)
    def _():
        t = jnp.dot(u_ref[...].astype(_BF), w1_ref[...].astype(_BF), preferred_element_type=_F32)
        xcs_ref[0:seq, :] = t[:, :FNET_W].astype(_BF)
        xcs_ref[seq:2 * seq, :] = t[:, FNET_W:].astype(_BF)

    mixed = jnp.dot(cs_ref[...].astype(_BF), xcs_ref[...], preferred_element_type=_F32)
    o_ref[...] = jnp.dot(mixed.astype(_BF), fw_ref[...].astype(_BF),
                         preferred_element_type=_F32).astype(o_ref.dtype)


def _fourier_mix(uf, seq, nbatch, block0, l, fw):
    tr = min(seq, 512)
    nr = seq // tr
    cs, w1 = _dft_constants(seq)
    return pl.pallas_call(
        functools.partial(_fnet_kernel, seq),
        grid=(nbatch, nr),
        in_specs=[
            pl.BlockSpec((seq, FNET_W), lambda b, r: (block0 + b, 0)),
            pl.BlockSpec((tr, 2 * seq), lambda b, r: (r, 0)),
            pl.BlockSpec((FNET_W, 2 * FNET_W), lambda b, r: (0, 0)),
            pl.BlockSpec((None, FNET_W, FNET_W), lambda b, r: (l, 0, 0)),
        ],
        out_specs=pl.BlockSpec((tr, FNET_W), lambda b, r: (b * nr + r, 0)),
        out_shape=jax.ShapeDtypeStruct((nbatch * seq, FNET_W), _BF),
        scratch_shapes=[pltpu.VMEM((2 * seq, FNET_W), _BF)],
        compiler_params=_params(("arbitrary", "arbitrary")),
        name="fourier_mix_%d" % seq,
    )(uf, cs, w1, fw)


def _route_top2(r_ref, h, ei_ref, ew_ref):
    lg = lax.dot_general(r_ref[...].astype(_BF), h.astype(_BF), _NT, preferred_element_type=_F32)
    eid = lax.broadcasted_iota(jnp.int32, lg.shape, 0)
    m1 = jnp.max(lg, axis=0, keepdims=True)
    i1 = jnp.min(jnp.where(lg == m1, eid, N_EXPERTS), axis=0, keepdims=True)
    lg2 = jnp.where(eid == i1, -jnp.inf, lg)
    m2 = jnp.max(lg2, axis=0, keepdims=True)
    i2 = jnp.min(jnp.where(lg2 == m2, eid, N_EXPERTS), axis=0, keepdims=True)
    e = jnp.exp(m2 - m1)
    ei_ref[0:1, :] = i1
    ei_ref[1:2, :] = i2
    ew_ref[0:1, :] = 1.0 / (1.0 + e)
    ew_ref[1:2, :] = e / (1.0 + e)


def _outproj_kernel(split, route, *refs):
    branch_refs, refs = refs[:6], refs[6:]
    n_x = 2 if split else 1
    x_refs, refs = refs[:n_x], refs[n_x:]
    mod_ref, w_ref, gf_ref = refs[:3]
    refs = refs[3:]
    if route:
        r_ref, xo_ref, h_ref, ei_ref, ew_ref, wb_ref = refs
    else:
        xo_ref, h_ref, wb_ref = refs
    i = pl.program_id(0)

    @pl.when(i == 0)
    def _():
        wb_ref[...] = w_ref[...].astype(_BF)

    mix = jnp.concatenate([_pick_half(i, ROW_TILE, branch_refs[2 * n], branch_refs[2 * n + 1])
                           for n in range(3)], axis=-1)
    x = _pick_half(i, ROW_TILE, *x_refs) if split else x_refs[0][...]
    x_new = x + mod_ref[2:3, :] * jnp.dot(mix, wb_ref[...], preferred_element_type=_F32)
    xo_ref[...] = x_new
    h = _norm_mod(x_new, gf_ref[...], mod_ref[3:4, :], mod_ref[4:5, :])
    h_ref[...] = h.astype(h_ref.dtype)
    if route:
        _route_top2(r_ref, h, ei_ref, ew_ref)


def _outproj(branches, xs, l, mods, w_all, gf_all, router_t=None):
    split = len(xs) == 2
    route = router_t is not None
    row = lambda i: (i, 0)
    in_specs = []
    args = []
    for pair, width in zip(branches, (ATTN_W, CONV_W, FNET_W)):
        in_specs += _half_specs(width, ROW_TILE)
        args += list(pair)
    in_specs += _half_specs(D_MODEL, ROW_TILE) if split else [pl.BlockSpec((ROW_TILE, D_MODEL), row)]
    args += list(xs)
    mod_spec, gf_spec, w_spec = _layer_specs(l, (D_MODEL, D_MODEL))
    in_specs += [mod_spec, w_spec, gf_spec]
    args += [mods, w_all, gf_all.reshape(DEPTH, 1, D_MODEL)]
    out_specs = [pl.BlockSpec((ROW_TILE, D_MODEL), row)] * 2
    out_shape = [jax.ShapeDtypeStruct((T, D_MODEL), _F32), jax.ShapeDtypeStruct((T, D_MODEL), _BF)]
    if route:
        in_specs.append(pl.BlockSpec((N_EXPERTS, D_MODEL), lambda i: (0, 0)))
        args.append(router_t)
        out_specs += [pl.BlockSpec((TOP_K, ROW_TILE), lambda i: (0, i))] * 2
        out_shape += [jax.ShapeDtypeStruct((TOP_K, T), jnp.int32), jax.ShapeDtypeStruct((TOP_K, T), _F32)]
    return pl.pallas_call(
        functools.partial(_outproj_kernel, split, route),
        grid=(T // ROW_TILE,),
        in_specs=in_specs,
        out_specs=out_specs,
        out_shape=out_shape,
        scratch_shapes=[pltpu.VMEM((D_MODEL, D_MODEL), _BF)],
        compiler_params=_params(("arbitrary",)),
        name="outproj_route" if route else "outproj",
    )(*args)


def _ffn_kernel(nj, n_sub_rows, vis_e, vis_start, vis_cnt, used_sub,
                x_hbm, wg_ref, wu_ref, wd_ref, y_hbm,
                big, xb, wgb, wub, wdb, sem_in, sem_out):
    del vis_e
    v = pl.program_id(0)
    j = pl.program_id(1)
    cnt = vis_cnt[v]
    row0 = vis_start[v] * SUB

    x_is_bf16 = x_hbm.dtype == _BF
    landing = xb if x_is_bf16 else big

    def copy_in(s):
        r = pl.multiple_of(row0 + s * SUB, SUB)
        b = pl.multiple_of(s * SUB, SUB)
        return pltpu.make_async_copy(x_hbm.at[pl.ds(r, SUB)], landing.at[pl.ds(b, SUB)], sem_in.at[s])

    def copy_out(s):
        r = pl.multiple_of(row0 + s * SUB, SUB)
        b = pl.multiple_of(s * SUB, SUB)
        return pltpu.make_async_copy(big.at[pl.ds(b, SUB)], y_hbm.at[pl.ds(r, SUB)], sem_out)

    def for_subs(fn):
        lax.fori_loop(0, cnt, lambda s, c: (fn(s), c)[1], 0)

    @pl.when(jnp.logical_and(j == 0, cnt > 0))
    def _():
        for_subs(lambda s: copy_in(s).start())

    @pl.when(cnt > 0)
    def _():
        wgb[...] = wg_ref[...].astype(_BF)
        wub[...] = wu_ref[...].astype(_BF)
        wdb[...] = wd_ref[...].astype(_BF)

        def block(r0, nrows):
            s0 = r0 // SUB

            @pl.when(j == 0)
            def _():
                for t in range(nrows // SUB):
                    copy_in(s0 + t).wait()
                    rs = pl.ds(pl.multiple_of(r0 + t * SUB, SUB), SUB)
                    if not x_is_bf16:
                        xb[rs, :] = big[rs, :].astype(_BF)
                    big[rs, :] = jnp.zeros((SUB, D_MODEL), _F32)

            rs = pl.ds(r0, nrows)
            x = xb[rs, :]
            a = jnp.dot(x, wgb[...], preferred_element_type=_F32)
            b = jnp.dot(x, wub[...], preferred_element_type=_F32)
            p = (a * _sigmoid(a) * b).astype(_BF)
            big[rs, :] += jnp.dot(p, wdb[...], preferred_element_type=_F32)

            @pl.when(j == nj - 1)
            def _():
                for t in range(nrows // SUB):
                    copy_out(s0 + t).start()

        n_big = cnt // BLK_MIN
        rem = cnt - n_big * BLK_MIN
        room = BLK_SUBS - BLK_MIN
        grow = jnp.minimum(rem, n_big * room)
        left = rem - grow

        def one_block(i, r0):
            size = BLK_MIN + jnp.clip(grow - i * room, 0, room)
            for sz in range(BLK_MIN, BLK_SUBS + 1):
                pl.when(size == sz)(functools.partial(block, pl.multiple_of(r0, SUB), sz * SUB))
            return r0 + size * SUB
        r0 = lax.fori_loop(0, n_big, one_block, 0)
        bit = BLK_MIN // 2
        while bit >= 1:
            take = (left // bit) % 2 == 1
            pl.when(take)(functools.partial(block, pl.multiple_of(r0, SUB), bit * SUB))
            r0 = r0 + jnp.where(take, bit * SUB, 0)
            bit //= 2

    @pl.when(jnp.logical_and(j == nj - 1, cnt > 0))
    def _():
        for_subs(lambda s: copy_out(s).wait())

    if x_hbm.dtype == y_hbm.dtype:
        @pl.when(jnp.logical_and(v == pl.num_programs(0) - 1, j == nj - 1))
        def _():
            def tail(s):
                r = pl.multiple_of(s * SUB, SUB)
                return pltpu.make_async_copy(x_hbm.at[pl.ds(r, SUB)], y_hbm.at[pl.ds(r, SUB)], sem_out)
            lax.fori_loop(used_sub[0], n_sub_rows, lambda s, c: (tail(s).start(), c)[1], 0)
            lax.fori_loop(used_sub[0], n_sub_rows, lambda s, c: (tail(s).wait(), c)[1], 0)


def _grouped_ffn(x, wg, wu, wd, fc, vis_e, vis_start, vis_cnt, used_sub):
    ff = wg.shape[-1]
    nj = ff // fc
    nv = vis_e.shape[0]
    n_rows = x.shape[0]

    def chunk(v, j, vc):
        return jnp.where(vc[v] > 0, j, nj - 1)

    return pl.pallas_call(
        functools.partial(_ffn_kernel, nj, n_rows // SUB),
        grid_spec=pltpu.PrefetchScalarGridSpec(
            num_scalar_prefetch=4,
            grid=(nv, nj),
            in_specs=[
                pl.BlockSpec(memory_space=pl.ANY),
                pl.BlockSpec((None, D_MODEL, fc), lambda v, j, ve, vs, vc, us: (ve[v], 0, chunk(v, j, vc))),
                pl.BlockSpec((None, D_MODEL, fc), lambda v, j, ve, vs, vc, us: (ve[v], 0, chunk(v, j, vc))),
                pl.BlockSpec((None, fc, D_MODEL), lambda v, j, ve, vs, vc, us: (ve[v], chunk(v, j, vc), 0)),
            ],
            out_specs=pl.BlockSpec(memory_space=pl.ANY),
            scratch_shapes=[
                pltpu.VMEM((TMAX, D_MODEL), _F32),
                pltpu.VMEM((TMAX, D_MODEL), _BF),
                pltpu.VMEM((D_MODEL, fc), _BF),
                pltpu.VMEM((D_MODEL, fc), _BF),
                pltpu.VMEM((fc, D_MODEL), _BF),
                pltpu.SemaphoreType.DMA((SUB_MAX,)),
                pltpu.SemaphoreType.DMA(()),
            ],
        ),
        out_shape=jax.ShapeDtypeStruct((n_rows, D_MODEL), _F32),
        compiler_params=_params(("arbitrary", "arbitrary"), FFN_VMEM_LIMIT),
        name="ffn_%d" % ff,
    )(vis_e, vis_start, vis_cnt, used_sub, x, wg, wu, wd)


def _dense_visits():
    nv = T // TMAX
    return (jnp.zeros((nv,), jnp.int32),
            jnp.arange(nv, dtype=jnp.int32) * SUB_MAX,
            jnp.full((nv,), SUB_MAX, jnp.int32),
            jnp.full((1,), T // SUB, jnp.int32))


N_SLOT_SUB = -(-(TOP_K * T + N_TILES * N_EXPERTS * (SEG_ALIGN - 1)) // SUB) + N_EXPERTS
N_SLOT = N_SLOT_SUB * SUB
N_VISIT = -(-N_SLOT_SUB // SUB_MAX) + N_EXPERTS


def _routing_plan(eidx):
    e_loc = eidx.reshape(TOP_K, N_TILES, DISP_TILE).transpose(1, 0, 2).reshape(N_TILES, PAIRS)
    onehot = e_loc[:, :, None] == jnp.arange(N_EXPERTS, dtype=jnp.int32)[None, None, :]
    ch = 128
    oh = onehot.astype(_F32).reshape(N_TILES, PAIRS // ch, ch, N_EXPERTS)
    tri = (jnp.arange(ch)[:, None] >= jnp.arange(ch)[None, :]).astype(_F32)
    within = jnp.einsum("ij,tcjk->tcik", tri, oh)
    tot = within[:, :, -1, :]
    csum = (within + (jnp.cumsum(tot, axis=1) - tot)[:, :, None, :]).reshape(N_TILES, PAIRS, N_EXPERTS)
    csum = csum.astype(jnp.int32)
    n_te = (csum[:, -1, :] + SEG_ALIGN - 1) // SEG_ALIGN * SEG_ALIGN
    src = jnp.cumsum(n_te, axis=1) - n_te
    lpos = jnp.sum(jnp.where(onehot, csum - 1 + src[:, None, :], 0), axis=2).astype(jnp.int32)
    counts = jnp.sum(n_te, axis=0)
    nsub = (counts + SUB - 1) // SUB
    sub_base = jnp.cumsum(nsub) - nsub
    dst = (sub_base * SUB)[None, :] + jnp.cumsum(n_te, axis=0) - n_te
    seg = tuple(a.reshape(-1).astype(jnp.int32) for a in (n_te, src, dst))
    pads = ((sub_base * SUB + counts).astype(jnp.int32), (nsub * SUB - counts).astype(jnp.int32))
    used_sub = jnp.sum(nsub).reshape(1).astype(jnp.int32)
    nvis = (nsub + SUB_MAX - 1) // SUB_MAX
    vend = jnp.cumsum(nvis)
    total = vend[-1]
    vid = jnp.arange(N_VISIT, dtype=jnp.int32)
    ve = jnp.minimum(jnp.sum((vid[:, None] >= vend[None, :]).astype(jnp.int32), axis=1), N_EXPERTS - 1)
    local = vid - (vend - nvis)[ve]
    nv_e = jnp.maximum(nvis[ve], 1)
    q, r = nsub[ve] // nv_e, nsub[ve] % nv_e
    cnt = q + (local < r).astype(jnp.int32)
    start = sub_base[ve] + local * q + jnp.minimum(local, r)
    used = vid < total
    last_e = ve[jnp.maximum(total - 1, 0)]
    vis_e = jnp.where(used, ve, last_e).astype(jnp.int32)
    vis_cnt = jnp.where(used, cnt, 0).astype(jnp.int32)
    vis_start = jnp.where(used, start, 0).astype(jnp.int32)
    return lpos.reshape(N_TILES, TOP_K, DISP_TILE), seg, pads, (vis_e, vis_start, vis_cnt, used_sub)


def _for_pow2_pieces(n, max_piece, fn):
    off = 0
    bit = max_piece
    while bit >= SEG_ALIGN:
        take = (n // bit) % 2 == 1
        pl.when(take)(functools.partial(fn, off, bit))
        off = off + jnp.where(take, bit, 0)
        bit //= 2


def _segment_copies(tile, seg, make_copy, wait=False):
    n_te, src, dst = seg
    for e in range(N_EXPERTS):
        j = tile * N_EXPERTS + e
        s0, d0 = src[j], dst[j]

        def piece(off, size, s0=s0, d0=d0):
            cp = make_copy(pl.multiple_of(s0 + off, SEG_ALIGN), pl.multiple_of(d0 + off, SEG_ALIGN), size)
            cp.wait() if wait else cp.start()
        _for_pow2_pieces(n_te[j], DISP_TILE, piece)


def _dispatch_kernel(n_te, src, dst, pad_start, pad_cnt, used_sub, h_ref, lpos_ref, xs_hbm, ring, ring_sems, sem):
    i = pl.program_id(0)
    last = pl.num_programs(0) - 1
    b = i % 2

    def copies(tile, slot_id, wait):
        _segment_copies(tile, (n_te, src, dst), lambda s, d, size: pltpu.make_async_copy(
            ring.at[slot_id, pl.ds(s, size)], xs_hbm.at[pl.ds(d, size)], ring_sems.at[slot_id]), wait)

    pl.when(i >= 2)(lambda: copies(i - 2, b, True))
    lp = lpos_ref[...]
    srow = lax.broadcasted_iota(jnp.int32, (SORT_ROWS, DISP_TILE), 0)
    perm = jnp.where(lp[0:1, :] == srow, 1.0, jnp.where(lp[1:2, :] == srow, 1.0, 0.0)).astype(_BF)
    ring[b] = jnp.dot(perm, h_ref[...].astype(_BF), preferred_element_type=_F32)
    copies(i, b, False)

    @pl.when(i == last)
    def _():
        copies(i - 1, 1 - b, True)
        copies(i, b, True)
        for e in range(N_EXPERTS):
            p0 = pad_start[e]

            def fill(off, size, p0=p0):
                cp = pltpu.make_async_copy(ring.at[b, pl.ds(0, size)],
                                           xs_hbm.at[pl.ds(pl.multiple_of(p0 + off, SEG_ALIGN), size)], sem)
                cp.start()
                cp.wait()
            _for_pow2_pieces(pad_cnt[e], SUB // 2, fill)

        def tail(s):
            r = pl.multiple_of(s * SUB, SUB)
            return pltpu.make_async_copy(ring.at[b, pl.ds(0, SUB)], xs_hbm.at[pl.ds(r, SUB)], sem)
        lax.fori_loop(used_sub[0], N_SLOT_SUB, lambda s, c: (tail(s).start(), c)[1], 0)
        lax.fori_loop(used_sub[0], N_SLOT_SUB, lambda s, c: (tail(s).wait(), c)[1], 0)


def _dispatch(h, lpos, seg, pads, used_sub):
    return pl.pallas_call(
        _dispatch_kernel,
        grid_spec=pltpu.PrefetchScalarGridSpec(
            num_scalar_prefetch=6,
            grid=(N_TILES,),
            in_specs=[pl.BlockSpec((DISP_TILE, D_MODEL), lambda i, *_: (i, 0)),
                      pl.BlockSpec((None, TOP_K, DISP_TILE), lambda i, *_: (i, 0, 0))],
            out_specs=pl.BlockSpec(memory_space=pl.ANY),
            scratch_shapes=[pltpu.VMEM((2, SORT_ROWS, D_MODEL), _F32),
                            pltpu.SemaphoreType.DMA((2,)), pltpu.SemaphoreType.DMA(())],
        ),
        out_shape=jax.ShapeDtypeStruct((N_SLOT, D_MODEL), _F32),
        compiler_params=_params(("arbitrary",)),
        name="dispatch",
    )(*seg, *pads, used_sub, h, lpos)


def _residual_out(x, gate, f, g_ref, o_refs, i, tile):
    out = x + gate * f
    if g_ref is not None:
        ms = jnp.mean(out * out, axis=-1, keepdims=True)
        out = out * lax.rsqrt(ms + EPS) * g_ref[...]
    if len(o_refs) == 1:
        o_refs[0][...] = out
    else:
        @pl.when(i < T_PROMPT // tile)
        def _():
            o_refs[0][...] = out

        @pl.when(i >= T_PROMPT // tile)
        def _():
            o_refs[1][...] = out


def _out_specs(final, tile):
    if not final:
        return ([pl.BlockSpec((tile, D_MODEL), lambda i, *_: (i, 0))],
                [jax.ShapeDtypeStruct((T, D_MODEL), _F32)])
    return (_half_specs(D_MODEL, tile),
            [jax.ShapeDtypeStruct((T_PROMPT, D_MODEL), _F32), jax.ShapeDtypeStruct((T_SAMPLE, D_MODEL), _F32)])


def _combine_kernel(final, x_ref, mod_ref, y_ref, *rest):
    g_ref, o_refs = (rest[0], rest[1:]) if final else (None, rest)
    _residual_out(x_ref[...], mod_ref[5:6, :], y_ref[...], g_ref, o_refs, pl.program_id(0), ROW_TILE)


def _combine(x, l, mods, y, g_final=None):
    final = g_final is not None
    row = lambda i: (i, 0)
    in_specs = [
        pl.BlockSpec((ROW_TILE, D_MODEL), row),
        _layer_specs(l, ())[0],
        pl.BlockSpec((ROW_TILE, D_MODEL), row),
    ]
    args = [x, mods, y]
    if final:
        in_specs.append(pl.BlockSpec((1, D_MODEL), lambda i: (0, 0)))
        args.append(g_final.reshape(1, D_MODEL))
    out_specs, out_shape = _out_specs(final, ROW_TILE)
    return pl.pallas_call(
        functools.partial(_combine_kernel, final),
        grid=(T // ROW_TILE,),
        in_specs=in_specs,
        out_specs=out_specs,
        out_shape=out_shape,
        compiler_params=_params(("arbitrary",)),
        name="combine%s" % ("_final" if final else ""),
    )(*args)


def _combine_top2_kernel(final, n_out, n_te, src, dst, x_ref, mod_ref, w_ref, lpos_ref, *rest):
    g_ref, rest = (rest[0], rest[1:]) if final else (None, rest)
    ys_hbm, o_refs, (ybuf, sems) = rest[0], rest[1:1 + n_out], rest[1 + n_out:]
    i = pl.program_id(0)

    def fetch(tile, b, wait=False):
        _segment_copies(tile, (n_te, src, dst), lambda s, d, size: pltpu.make_async_copy(
            ys_hbm.at[pl.ds(d, size)], ybuf.at[b, pl.ds(s, size)], sems.at[b]), wait)

    @pl.when(i == 0)
    def _():
        for slot_id in range(2):
            ybuf[slot_id, PAIRS:SORT_ROWS, :] = jnp.zeros((SORT_ROWS - PAIRS, D_MODEL), _F32)
        fetch(0, 0)

    @pl.when(i + 1 < pl.num_programs(0))
    def _():
        fetch(i + 1, (i + 1) % 2)

    b = i % 2
    fetch(i, b, wait=True)
    y = ybuf[b].astype(_BF)
    lp = lpos_ref[...]
    scol = lax.broadcasted_iota(jnp.int32, (DISP_TILE, SORT_ROWS), 1)
    w = w_ref[...]
    pick = jnp.zeros((DISP_TILE, SORT_ROWS), _F32)
    for k in range(TOP_K):
        pick = jnp.where(lp[:, k:k + 1] == scol, w[:, k:k + 1], pick)
    f = jnp.dot(pick.astype(_BF), y, preferred_element_type=_F32)
    _residual_out(x_ref[...], mod_ref[5:6, :], f, g_ref, o_refs, i, DISP_TILE)


def _combine_top2(x, l, mods, ys, lpos_t, seg, w, g_final=None):
    final = g_final is not None
    row = lambda i, *_: (i, 0)
    in_specs = [
        pl.BlockSpec((DISP_TILE, D_MODEL), row),
        pl.BlockSpec((None, None, 6, D_MODEL), lambda i, *_: (l, _cond_of_tile(i, DISP_TILE), 0, 0)),
        pl.BlockSpec((DISP_TILE, TOP_K), row),
        pl.BlockSpec((None, DISP_TILE, TOP_K), lambda i, *_: (i, 0, 0)),
    ]
    args = [x, mods, w, lpos_t]
    if final:
        in_specs.append(pl.BlockSpec((1, D_MODEL), lambda i, *_: (0, 0)))
        args.append(g_final.reshape(1, D_MODEL))
    in_specs.append(pl.BlockSpec(memory_space=pl.ANY))
    args.append(ys)
    out_specs, out_shape = _out_specs(final, DISP_TILE)
    return pl.pallas_call(
        functools.partial(_combine_top2_kernel, final, len(out_specs)),
        grid_spec=pltpu.PrefetchScalarGridSpec(
            num_scalar_prefetch=3,
            grid=(N_TILES,),
            in_specs=in_specs,
            out_specs=out_specs,
            scratch_shapes=[
                pltpu.VMEM((2, SORT_ROWS, D_MODEL), _F32),
                pltpu.SemaphoreType.DMA((2,)),
            ],
        ),
        out_shape=out_shape,
        compiler_params=_params(("arbitrary",)),
        name="combine_top2%s" % ("_final" if final else ""),
    )(*seg, *args)


def kernel(x_prompt, x_sample, cache_k, cache_v, c, c_ctx, w_ada, b_ada, g_norm_mix, g_norm_ffn,
           w_in, w_out, attn_sink, conv_dw, conv_dw_b, conv_ln_g, conv_ln_b, conv_pw, fnet_w,
           ffn_w_gate, ffn_w_up, ffn_w_down, moe_router, moe_w_gate, moe_w_up, moe_w_down, g_final):
    xs = (x_prompt.reshape(T_PROMPT, D_MODEL), x_sample.reshape(T_SAMPLE, D_MODEL))
    cond8 = jnp.concatenate([c_ctx[None, :], c, jnp.zeros((N_COND - 1 - DEC_BATCH, D_MODEL), _F32)], axis=0)
    mods = _modulation(cond8, w_ada, b_ada).reshape(DEPTH, N_COND, 6, D_MODEL)
    cos_t, sin_t = _rope_tables()
    ck_all = cache_k.reshape(DEC_BATCH, DEPTH, PAST_LEN, KV_W)
    cv_all = cache_v.reshape(DEC_BATCH, DEPTH, PAST_LEN, KV_W)
    p_blocks = T_PROMPT // DEC_SEQ
    sinks = attn_sink.reshape(DEPTH * N_HEADS)

    ks, vs = [], []
    resid = None
    for l in range(DEPTH):
        q, k, v, uc, uf, *x_new = _inproj(xs, l, mods, g_norm_mix, w_in, cos_t, sin_t, resid=resid)
        if resid is not None:
            xs, resid = tuple(x_new), None
        ks.append(k[:T_PROMPT])
        vs.append(v[:T_PROMPT])
        attn = (_ctx_attention(l, sinks, q, k, v),
                _lat_attention(l, sinks, q, k, v, ck_all, cv_all))
        cargs = (l, conv_dw, conv_dw_b, conv_ln_g, conv_ln_b, conv_pw)
        conv = (_conv_module(uc, SEQ, BATCH, 0, *cargs),
                _conv_module(uc, DEC_SEQ, DEC_BATCH, p_blocks, *cargs))
        four = (_fourier_mix(uf, SEQ, BATCH, 0, l, fnet_w),
                _fourier_mix(uf, DEC_SEQ, DEC_BATCH, p_blocks, l, fnet_w))
        last = g_final if l == DEPTH - 1 else None
        i = l // 2
        if l % 2 == 0:
            x, h = _outproj((attn, conv, four), xs, l, mods, w_out, g_norm_ffn)
            y = _grouped_ffn(h, ffn_w_gate[i:i + 1], ffn_w_up[i:i + 1], ffn_w_down[i:i + 1], 256,
                             *_dense_visits())
            if last is None:
                xs, resid = (x,), (y, l)
            else:
                xs = tuple(_combine(x, l, mods, y, g_final=last))
        else:
            x, h, eidx, ew = _outproj((attn, conv, four), xs, l, mods, w_out, g_norm_ffn,
                                      router_t=moe_router[i].T)
            lpos, seg, pads, visits = _routing_plan(eidx)
            xd = _dispatch(h, lpos, seg, pads, visits[-1])
            ys = _grouped_ffn(xd, moe_w_gate[i], moe_w_up[i], moe_w_down[i], 512, *visits)
            xs = tuple(_combine_top2(x, l, mods, ys, lpos.transpose(0, 2, 1), seg, ew.T, g_final=last))

    y_prompt = xs[0].reshape(BATCH, SEQ, D_MODEL)
    y_sample = xs[1].reshape(DEC_BATCH, DEC_SEQ, D_MODEL)
    state_k = jnp.stack([a.reshape(BATCH, SEQ, N_KV_HEADS, HEAD_DIM) for a in ks], axis=1)
    state_v = jnp.stack([a.reshape(BATCH, SEQ, N_KV_HEADS, HEAD_DIM) for a in vs], axis=1)
    return (y_prompt, y_sample, state_k, state_v)
```

```python
import functools

import numpy as np
import jax
import jax.numpy as jnp
from jax import lax
from jax.experimental import pallas as pl
from jax.experimental.pallas import tpu as pltpu

D_MODEL = 1024
BATCH = 16
SEQ = 256
DEPTH = 2
DEC_BATCH = 2
DEC_SEQ = 2048
PAST_LEN = 512
GRID_W = 64
HEAD_DIM = 64
N_HEADS = 8
N_KV_HEADS = 2
GQA_GROUP = N_HEADS // N_KV_HEADS
ATTN_W = N_HEADS * HEAD_DIM
KV_W = N_KV_HEADS * HEAD_DIM
WINDOW = 128
BLOCK = 128
ROPE_THETA = 10000.0
CONV_W = D_MODEL // 4
CONV_K = 31
FNET_GROUPS = 4
FNET_W = D_MODEL // 4
FNET_GW = FNET_W // FNET_GROUPS
IN_WIDTH = ATTN_W + 2 * KV_W + 2 * CONV_W + FNET_W
D_FF = 2816
N_EXPERTS = 8
TOP_K = 2
D_FF_EXPERT = 3584
EPS = 1e-6
NEG = -1e30

T_PROMPT = BATCH * SEQ
T_SAMPLE = DEC_BATCH * DEC_SEQ
T = T_PROMPT + T_SAMPLE
N_COND = 8

ROW_TILE = 512
SUB = 128
SUB_MAX = 32
TMAX = SUB * SUB_MAX
BLK_MIN = 8
BLK_SUBS = 10
DISP_TILE = 512
N_TILES = T // DISP_TILE
PAIRS = TOP_K * DISP_TILE
SEG_ALIGN = 8
SORT_ROWS = 1152
CONV_CHUNK = 64
CONV_PAD = 16
CONV_SPAN = CONV_CHUNK + 8 * ((CONV_PAD + CONV_K // 2) // 8)
VMEM_LIMIT = 48 * 1024 * 1024
FFN_VMEM_LIMIT = 56 * 1024 * 1024

_BF = jnp.bfloat16
_F32 = jnp.float32


def _cond_of_tile(i, tile):
    r = i * tile
    return jnp.where(r < T_PROMPT, 0, 1 + (r - T_PROMPT) // DEC_SEQ)


def _params(sem, vmem=VMEM_LIMIT):
    return pltpu.CompilerParams(dimension_semantics=sem, vmem_limit_bytes=vmem)


def _sigmoid(x):
    return 1.0 / (1.0 + jnp.exp(-x))


def _mod_kernel(cond_ref, w_ref, b_ref, o_ref):
    cnd = cond_ref[...]
    s = (cnd * _sigmoid(cnd)).astype(_BF)
    o_ref[...] = jnp.dot(s, w_ref[...].astype(_BF), preferred_element_type=_F32) + b_ref[...]


def _modulation(cond8, w_ada, b_ada):
    nt = 1536
    return pl.pallas_call(
        _mod_kernel,
        grid=(DEPTH, 6 * D_MODEL // nt),
        in_specs=[
            pl.BlockSpec((N_COND, D_MODEL), lambda l, n: (0, 0)),
            pl.BlockSpec((None, D_MODEL, nt), lambda l, n: (l, 0, n)),
            pl.BlockSpec((None, 1, nt), lambda l, n: (l, 0, n)),
        ],
        out_specs=pl.BlockSpec((None, N_COND, nt), lambda l, n: (l, 0, n)),
        out_shape=jax.ShapeDtypeStruct((DEPTH, N_COND, 6 * D_MODEL), _F32),
        compiler_params=_params(("arbitrary", "arbitrary")),
        name="modulation",
    )(cond8, w_ada, b_ada.reshape(DEPTH, 1, 6 * D_MODEL))


def _norm_mod(x, g, shift, scale):
    ms = jnp.mean(x * x, axis=-1, keepdims=True)
    y = x * lax.rsqrt(ms + EPS) * g
    return y * (1.0 + scale) + shift


def _rope_tables():
    rows = DEC_SEQ // GRID_W
    n_freq = HEAD_DIM // 4
    inv = ROPE_THETA ** (-jnp.arange(n_freq, dtype=_F32) / n_freq)
    gr, gc = jnp.meshgrid(jnp.arange(rows, dtype=_F32), jnp.arange(GRID_W, dtype=_F32), indexing="ij")
    ang_r = gr.reshape(-1)[:, None] * inv
    ang_c = gc.reshape(-1)[:, None] * inv
    cr, sr, cc, sc = jnp.cos(ang_r), jnp.sin(ang_r), jnp.cos(ang_c), jnp.sin(ang_c)
    cos64 = jnp.concatenate([cr, cr, cc, cc], axis=-1)
    sin64 = jnp.concatenate([-sr, sr, -sc, sc], axis=-1)
    return jnp.tile(cos64, (1, 2)), jnp.tile(sin64, (1, 2))


def _rope128(x, cos, sin):
    lane = lax.broadcasted_iota(jnp.int32, x.shape, 1)
    first = (lane % 32) < 16
    partner = jnp.where(first, pltpu.roll(x, 128 - 16, 1), pltpu.roll(x, 16, 1))
    return x * cos + partner * sin


def _half_specs(width, tile):
    pt = T_PROMPT // tile
    return [pl.BlockSpec((tile, width), lambda i, *_: (jnp.minimum(i, pt - 1), 0)),
            pl.BlockSpec((tile, width), lambda i, *_: (jnp.maximum(i - pt, 0), 0))]


def _pick_half(i, tile, a_ref, b_ref):
    return jnp.where(i < T_PROMPT // tile, a_ref[...], b_ref[...])


def _layer_specs(l, w_shape):
    return [
        pl.BlockSpec((None, None, 6, D_MODEL), lambda i, *_: (l, _cond_of_tile(i, ROW_TILE), 0, 0)),
        pl.BlockSpec((None, 1, D_MODEL), lambda i, *_: (l, 0, 0)),
        pl.BlockSpec((None,) + w_shape, lambda i, *_: (l, 0, 0)),
    ]


def _inproj_kernel(split, resid, *refs):
    n_x = 2 if split else 1
    x_refs, refs = refs[:n_x], refs[n_x:]
    if resid:
        y_ref, modp_ref = refs[:2]
        refs = refs[2:]
    mod_ref, g_ref, w_ref, cos_ref, sin_ref, q_ref, k_ref, v_ref, uc_ref, uf_ref = refs[:10]
    wb_ref = refs[-1]
    i = pl.program_id(0)

    @pl.when(i == 0)
    def _():
        wb_ref[...] = w_ref[...].astype(_BF)

    x = _pick_half(i, ROW_TILE, *x_refs) if split else x_refs[0][...]
    if resid:
        x = x + modp_ref[5:6, :] * y_ref[...]
        refs[10][...] = x
    h = _norm_mod(x, g_ref[...], mod_ref[0:1, :], mod_ref[1:2, :]).astype(_BF)
    latent = i >= T_PROMPT // ROW_TILE
    cos = jnp.where(latent, cos_ref[...], 1.0)
    sin = jnp.where(latent, sin_ref[...], 0.0)

    def proj(o, n):
        return jnp.dot(h, wb_ref[:, o:o + n], preferred_element_type=_F32)

    o = 0
    for c in range(ATTN_W // 256):
        qq = proj(o, 256)
        q_ref[:, o:o + 128] = _rope128(qq[:, :128], cos, sin).astype(q_ref.dtype)
        q_ref[:, o + 128:o + 256] = _rope128(qq[:, 128:], cos, sin).astype(q_ref.dtype)
        o += 256
    kv = proj(o, 2 * KV_W)
    k_ref[...] = _rope128(kv[:, :KV_W], cos, sin)
    v_ref[...] = kv[:, KV_W:]
    o += 2 * KV_W
    uc_ref[...] = proj(o, 2 * CONV_W)
    o += 2 * CONV_W
    uf_ref[...] = proj(o, FNET_W).astype(uf_ref.dtype)


def _inproj(xs, l, mods, g_all, w_all, cos_t, sin_t, resid=None):
    split = len(xs) == 2
    pt = T_PROMPT // ROW_TILE
    per_seq = DEC_SEQ // ROW_TILE

    def rope_idx(i):
        return (jnp.maximum(i - pt, 0) % per_seq, 0)

    row = lambda i: (i, 0)
    in_specs = _half_specs(D_MODEL, ROW_TILE) if split else [pl.BlockSpec((ROW_TILE, D_MODEL), row)]
    args = list(xs)
    outs = [ATTN_W, KV_W, KV_W, 2 * CONV_W, FNET_W]
    dtypes = [_BF, _F32, _F32, _F32, _BF]
    if resid is not None:
        y, l_prev = resid
        in_specs += [pl.BlockSpec((ROW_TILE, D_MODEL), row), _layer_specs(l_prev, ())[0]]
        args += [y, mods]
        outs.append(D_MODEL)
        dtypes.append(_F32)
    return pl.pallas_call(
        functools.partial(_inproj_kernel, split, resid is not None),
        grid=(T // ROW_TILE,),
        in_specs=in_specs + _layer_specs(l, (D_MODEL, IN_WIDTH)) + [
            pl.BlockSpec((ROW_TILE, 128), rope_idx),
            pl.BlockSpec((ROW_TILE, 128), rope_idx),
        ],
        out_specs=[pl.BlockSpec((ROW_TILE, n), row) for n in outs],
        out_shape=[jax.ShapeDtypeStruct((T, n), dt) for n, dt in zip(outs, dtypes)],
        scratch_shapes=[pltpu.VMEM((D_MODEL, IN_WIDTH), _BF)],
        compiler_params=_params(("arbitrary",)),
        name="inproj_resid" if resid is not None else "inproj",
    )(*args, mods, g_all.reshape(DEPTH, 1, D_MODEL), w_all, cos_t, sin_t)


def _stack_groups(q, kh, rows):
    parts = [q[:, (kh * GQA_GROUP + g) * HEAD_DIM:(kh * GQA_GROUP + g + 1) * HEAD_DIM] for g in range(GQA_GROUP)]
    return jnp.concatenate(parts, axis=0)


def _sink_column(sink_ref, head0, rows):
    r = lax.broadcasted_iota(jnp.int32, (GQA_GROUP * rows, 1), 0)
    col = jnp.full((GQA_GROUP * rows, 1), sink_ref[head0], _F32)
    for g in range(1, GQA_GROUP):
        col = jnp.where(r >= g * rows, sink_ref[head0 + g], col)
    return col


def _unstack_store(o_ref, o, kh, rows):
    for pair in range(GQA_GROUP // 2):
        a = o[(2 * pair) * rows:(2 * pair + 1) * rows]
        b = o[(2 * pair + 1) * rows:(2 * pair + 2) * rows]
        c0 = (kh * GQA_GROUP + 2 * pair) * HEAD_DIM
        o_ref[:, c0:c0 + 2 * HEAD_DIM] = jnp.concatenate([a, b], axis=-1).astype(o_ref.dtype)


_NT = (((1,), (1,)), ((), ()))


def _ctx_attn_kernel(l, sink_ref, q_ref, k_ref, v_ref, o_ref):
    scale = HEAD_DIM ** -0.5
    q = q_ref[...] * scale
    for kh in range(N_KV_HEADS):
        kk = k_ref[:, kh * HEAD_DIM:(kh + 1) * HEAD_DIM].astype(_BF)
        vv = v_ref[:, kh * HEAD_DIM:(kh + 1) * HEAD_DIM].astype(_BF)
        qs = _stack_groups(q, kh, SEQ).astype(_BF)
        s = lax.dot_general(qs, kk, _NT, preferred_element_type=_F32)
        sink = _sink_column(sink_ref, l * N_HEADS + kh * GQA_GROUP, SEQ)
        m = jnp.maximum(jnp.max(s, axis=-1, keepdims=True), sink)
        p = jnp.exp(s - m)
        den = jnp.sum(p, axis=-1, keepdims=True) + jnp.exp(sink - m)
        o = jnp.dot(p.astype(_BF), vv, preferred_element_type=_F32) / den
        _unstack_store(o_ref, o, kh, SEQ)


def _ctx_attention(l, sinks, q, k, v):
    row = lambda b, s: (b, 0)
    return pl.pallas_call(
        functools.partial(_ctx_attn_kernel, l),
        grid_spec=pltpu.PrefetchScalarGridSpec(
            num_scalar_prefetch=1,
            grid=(BATCH,),
            in_specs=[
                pl.BlockSpec((SEQ, ATTN_W), row),
                pl.BlockSpec((SEQ, KV_W), row),
                pl.BlockSpec((SEQ, KV_W), row),
            ],
            out_specs=pl.BlockSpec((SEQ, ATTN_W), row),
        ),
        out_shape=jax.ShapeDtypeStruct((T_PROMPT, ATTN_W), _BF),
        compiler_params=_params(("arbitrary",)),
        name="ctx_attention",
    )(sinks, q, k, v)


def _lat_attn_kernel(l, sink_ref, q_ref, k_ref, v_ref, ck_ref, cv_ref, o_ref):
    n = pl.program_id(1)
    scale = HEAD_DIM ** -0.5
    nb = DEC_SEQ // BLOCK
    band = 3 * BLOCK
    start = pl.multiple_of(jnp.clip(n - 1, 0, nb - 3) * BLOCK, BLOCK)
    q = q_ref[...] * scale
    kb = k_ref[pl.ds(start, band), :]
    vb = v_ref[pl.ds(start, band), :]
    dpos = (lax.broadcasted_iota(jnp.int32, (BLOCK, band), 1) - lax.broadcasted_iota(jnp.int32, (BLOCK, band), 0)
            + (start - n * BLOCK))
    in_window = jnp.where(jnp.abs(dpos) <= WINDOW, 0.0, NEG)
    bias = jnp.concatenate([in_window] * GQA_GROUP, axis=0)
    for kh in range(N_KV_HEADS):
        hs = slice(kh * HEAD_DIM, (kh + 1) * HEAD_DIM)
        qs = _stack_groups(q, kh, BLOCK).astype(_BF)
        s_loc = lax.dot_general(qs, kb[:, hs].astype(_BF), _NT, preferred_element_type=_F32)
        s_loc = jnp.where(bias < 0.0, NEG, s_loc)
        s_ctx = lax.dot_general(qs, ck_ref[:, hs].astype(_BF), _NT, preferred_element_type=_F32)
        sink = _sink_column(sink_ref, l * N_HEADS + kh * GQA_GROUP, BLOCK)
        m = jnp.maximum(jnp.maximum(jnp.max(s_loc, axis=-1, keepdims=True),
                                    jnp.max(s_ctx, axis=-1, keepdims=True)), sink)
        p_loc = jnp.exp(s_loc - m)
        p_ctx = jnp.exp(s_ctx - m)
        den = (jnp.sum(p_loc, axis=-1, keepdims=True) + jnp.sum(p_ctx, axis=-1, keepdims=True)
               + jnp.exp(sink - m))
        o = (jnp.dot(p_loc.astype(_BF), vb[:, hs].astype(_BF), preferred_element_type=_F32)
             + jnp.dot(p_ctx.astype(_BF), cv_ref[:, hs].astype(_BF), preferred_element_type=_F32)) / den
        _unstack_store(o_ref, o, kh, BLOCK)


def _lat_attention(l, sinks, q, k, v, ck, cv):
    nb = DEC_SEQ // BLOCK
    q0 = T_PROMPT // BLOCK
    s0 = T_PROMPT // DEC_SEQ
    return pl.pallas_call(
        functools.partial(_lat_attn_kernel, l),
        grid_spec=pltpu.PrefetchScalarGridSpec(
            num_scalar_prefetch=1,
            grid=(DEC_BATCH, nb),
            in_specs=[
                pl.BlockSpec((BLOCK, ATTN_W), lambda b, n, s: (q0 + b * nb + n, 0)),
                pl.BlockSpec((DEC_SEQ, KV_W), lambda b, n, s: (s0 + b, 0)),
                pl.BlockSpec((DEC_SEQ, KV_W), lambda b, n, s: (s0 + b, 0)),
                pl.BlockSpec((None, None, PAST_LEN, KV_W), lambda b, n, s: (b, l, 0, 0)),
                pl.BlockSpec((None, None, PAST_LEN, KV_W), lambda b, n, s: (b, l, 0, 0)),
            ],
            out_specs=pl.BlockSpec((BLOCK, ATTN_W), lambda b, n, s: (b * nb + n, 0)),
        ),
        out_shape=jax.ShapeDtypeStruct((T_SAMPLE, ATTN_W), _BF),
        compiler_params=_params(("arbitrary", "arbitrary")),
        name="lat_attention",
    )(sinks, q, k, v, ck, cv)


def _conv_kernel(seq, u_ref, dw_ref, dwb_ref, lg_ref, lb_ref, pw_ref, o_ref, pad_ref, y_ref, sh_ref):
    u = u_ref[...]
    pad_ref[0:CONV_PAD, :] = jnp.zeros((CONV_PAD, CONV_W), _F32)
    pad_ref[CONV_PAD + seq:2 * CONV_PAD + seq, :] = jnp.zeros((CONV_PAD, CONV_W), _F32)
    pad_ref[CONV_PAD:CONV_PAD + seq, :] = u[:, :CONV_W] * _sigmoid(u[:, CONV_W:])
    off = CONV_PAD - CONV_K // 2
    span = CONV_SPAN

    def chunk(c, carry):
        row = pl.multiple_of(c * CONV_CHUNK, CONV_CHUNK)
        win = pad_ref[pl.ds(row, CONV_CHUNK + 2 * CONV_PAD), :]
        acc = jnp.zeros((CONV_CHUNK, CONV_W), _F32) + dwb_ref[...]
        for phase in range(8):
            sh_ref[phase] = win[phase:phase + span, :]
        for phase in range(8):
            for a in range(span // 8):
                t = 8 * a + phase - off
                if 0 <= t < CONV_K:
                    acc = acc + sh_ref[phase, 8 * a:8 * a + CONV_CHUNK, :] * dw_ref[t:t + 1, :]
        mu = jnp.mean(acc, axis=-1, keepdims=True)
        d = acc - mu
        var = jnp.mean(d * d, axis=-1, keepdims=True)
        y = d * lax.rsqrt(var + EPS) * lg_ref[...] + lb_ref[...]
        y_ref[pl.ds(row, CONV_CHUNK), :] = (y * _sigmoid(y)).astype(_BF)
        return carry

    lax.fori_loop(0, seq // CONV_CHUNK, chunk, 0)
    o_ref[...] = jnp.dot(y_ref[...], pw_ref[...].astype(_BF), preferred_element_type=_F32).astype(o_ref.dtype)


def _conv_module(uc, seq, nbatch, block0, l, dw, dwb, lg, lb, pw):
    vec = lambda a: a.reshape(DEPTH, 1, CONV_W)
    layer = lambda b: (l, 0, 0)
    return pl.pallas_call(
        functools.partial(_conv_kernel, seq),
        grid=(nbatch,),
        in_specs=[
            pl.BlockSpec((seq, 2 * CONV_W), lambda b: (block0 + b, 0)),
            pl.BlockSpec((None, CONV_K, CONV_W), layer),
            pl.BlockSpec((None, 1, CONV_W), layer),
            pl.BlockSpec((None, 1, CONV_W), layer),
            pl.BlockSpec((None, 1, CONV_W), layer),
            pl.BlockSpec((None, CONV_W, CONV_W), layer),
        ],
        out_specs=pl.BlockSpec((seq, CONV_W), lambda b: (b, 0)),
        out_shape=jax.ShapeDtypeStruct((nbatch * seq, CONV_W), _BF),
        scratch_shapes=[pltpu.VMEM((seq + 2 * CONV_PAD, CONV_W), _F32), pltpu.VMEM((seq, CONV_W), _BF),
                        pltpu.VMEM((8, CONV_SPAN, CONV_W), _F32)],
        compiler_params=_params(("arbitrary",)),
        name="conv_module_%d" % seq,
    )(uc, dw, vec(dwb), vec(lg), vec(lb), pw)


def _dft_constants(seq):
    half = seq // 2
    j = np.arange(half, dtype=np.int64)
    ang = 2.0 * np.pi * ((j[:, None] * j[None, :]) % half).astype(np.float64) / half
    m = np.concatenate([np.cos(ang), -np.sin(ang)], axis=1) / np.sqrt(seq)
    tw = np.pi * j.astype(np.float64) / half
    cb = np.repeat(np.cos(tw)[:, None], FNET_W, axis=1)
    sb = np.repeat(np.sin(tw)[:, None], FNET_W, axis=1)
    c = np.arange(FNET_GW, dtype=np.int64)
    angc = 2.0 * np.pi * ((c[:, None] * c[None, :]) % FNET_GW).astype(np.float64) / FNET_GW
    eye = np.eye(FNET_GROUPS)
    w1 = np.concatenate([np.kron(eye, np.cos(angc)), np.kron(eye, np.sin(angc))], axis=1) / np.sqrt(FNET_GW)
    return tuple(jnp.asarray(a, dtype=_F32) for a in (m, cb, sb, w1))


def _fnet_kernel(half, u_ref, m_ref, cb_ref, sb_ref, w1_ref, fw_ref, o_ref, rhs_ref):
    @pl.when(pl.program_id(1) == 0)
    def _():
        w1 = w1_ref[...].astype(_BF)
        te = jnp.dot(u_ref[:, :FNET_W], w1, preferred_element_type=_F32)
        to = jnp.dot(u_ref[:, FNET_W:], w1, preferred_element_type=_F32)
        ec, es = te[:, :FNET_W], te[:, FNET_W:]
        oc, os_ = to[:, :FNET_W], to[:, FNET_W:]
        for col, (top, bot) in enumerate(((ec, es), (oc, os_), (os_, -oc))):
            rhs_ref[0:half, col * FNET_W:(col + 1) * FNET_W] = top.astype(_BF)
            rhs_ref[half:2 * half, col * FNET_W:(col + 1) * FNET_W] = bot.astype(_BF)

    pqr = jnp.dot(m_ref[...].astype(_BF), rhs_ref[...], preferred_element_type=_F32)
    p = pqr[:, :FNET_W]
    rot = cb_ref[...] * pqr[:, FNET_W:2 * FNET_W] - sb_ref[...] * pqr[:, 2 * FNET_W:]
    fw = fw_ref[...].astype(_BF)
    o_ref[0] = jnp.dot((p + rot).astype(_BF), fw, preferred_element_type=_F32).astype(o_ref.dtype)
    o_ref[1] = jnp.dot((p - rot).astype(_BF), fw, preferred_element_type=_F32).astype(o_ref.dtype)


def _fourier_mix(u_pairs, l, fw):
    nbatch, half, _ = u_pairs.shape
    tr = min(half, 512)
    nr = half // tr
    m, cb, sb, w1 = _dft_constants(2 * half)
    return pl.pallas_call(
        functools.partial(_fnet_kernel, half),
        grid=(nbatch, nr),
        in_specs=[
            pl.BlockSpec((None, half, 2 * FNET_W), lambda b, r: (b, 0, 0)),
            pl.BlockSpec((tr, 2 * half), lambda b, r: (r, 0)),
            pl.BlockSpec((tr, FNET_W), lambda b, r: (r, 0)),
            pl.BlockSpec((tr, FNET_W), lambda b, r: (r, 0)),
            pl.BlockSpec((FNET_W, 2 * FNET_W), lambda b, r: (0, 0)),
            pl.BlockSpec((None, FNET_W, FNET_W), lambda b, r: (l, 0, 0)),
        ],
        out_specs=pl.BlockSpec((None, 2, tr, FNET_W), lambda b, r: (b, 0, r, 0)),
        out_shape=jax.ShapeDtypeStruct((nbatch, 2, half, FNET_W), _BF),
        scratch_shapes=[pltpu.VMEM((2 * half, 3 * FNET_W), _BF)],
        compiler_params=_params(("arbitrary", "arbitrary")),
        name="fourier_mix_%d" % (2 * half),
    )(u_pairs, m, cb, sb, w1, fw).reshape(nbatch * 2 * half, FNET_W)


def _route_top2(r_ref, h, ei_ref, ew_ref):
    lg = lax.dot_general(r_ref[...].astype(_BF), h.astype(_BF), _NT, preferred_element_type=_F32)
    eid = lax.broadcasted_iota(jnp.int32, lg.shape, 0)
    m1 = jnp.max(lg, axis=0, keepdims=True)
    i1 = jnp.min(jnp.where(lg == m1, eid, N_EXPERTS), axis=0, keepdims=True)
    lg2 = jnp.where(eid == i1, -jnp.inf, lg)
    m2 = jnp.max(lg2, axis=0, keepdims=True)
    i2 = jnp.min(jnp.where(lg2 == m2, eid, N_EXPERTS), axis=0, keepdims=True)
    e = jnp.exp(m2 - m1)
    ei_ref[0:1, :] = i1
    ei_ref[1:2, :] = i2
    ew_ref[0:1, :] = 1.0 / (1.0 + e)
    ew_ref[1:2, :] = e / (1.0 + e)


def _outproj_kernel(split, route, *refs):
    branch_refs, refs = refs[:6], refs[6:]
    n_x = 2 if split else 1
    x_refs, refs = refs[:n_x], refs[n_x:]
    mod_ref, w_ref, gf_ref = refs[:3]
    refs = refs[3:]
    if route:
        r_ref, xo_ref, h_ref, ei_ref, ew_ref, wb_ref = refs
    else:
        xo_ref, h_ref, wb_ref = refs
    i = pl.program_id(0)

    @pl.when(i == 0)
    def _():
        wb_ref[...] = w_ref[...].astype(_BF)

    mix = jnp.concatenate([_pick_half(i, ROW_TILE, branch_refs[2 * n], branch_refs[2 * n + 1])
                           for n in range(3)], axis=-1)
    x = _pick_half(i, ROW_TILE, *x_refs) if split else x_refs[0][...]
    x_new = x + mod_ref[2:3, :] * jnp.dot(mix, wb_ref[...], preferred_element_type=_F32)
    xo_ref[...] = x_new
    h = _norm_mod(x_new, gf_ref[...], mod_ref[3:4, :], mod_ref[4:5, :])
    h_ref[...] = h.astype(h_ref.dtype)
    if route:
        _route_top2(r_ref, h, ei_ref, ew_ref)


def _outproj(branches, xs, l, mods, w_all, gf_all, router_t=None):
    split = len(xs) == 2
    route = router_t is not None
    row = lambda i: (i, 0)
    in_specs = []
    args = []
    for pair, width in zip(branches, (ATTN_W, CONV_W, FNET_W)):
        in_specs += _half_specs(width, ROW_TILE)
        args += list(pair)
    in_specs += _half_specs(D_MODEL, ROW_TILE) if split else [pl.BlockSpec((ROW_TILE, D_MODEL), row)]
    args += list(xs)
    mod_spec, gf_spec, w_spec = _layer_specs(l, (D_MODEL, D_MODEL))
    in_specs += [mod_spec, w_spec, gf_spec]
    args += [mods, w_all, gf_all.reshape(DEPTH, 1, D_MODEL)]
    out_specs = [pl.BlockSpec((ROW_TILE, D_MODEL), row)] * 2
    out_shape = [jax.ShapeDtypeStruct((T, D_MODEL), _F32), jax.ShapeDtypeStruct((T, D_MODEL), _BF)]
    if route:
        in_specs.append(pl.BlockSpec((N_EXPERTS, D_MODEL), lambda i: (0, 0)))
        args.append(router_t)
        out_specs += [pl.BlockSpec((TOP_K, ROW_TILE), lambda i: (0, i))] * 2
        out_shape += [jax.ShapeDtypeStruct((TOP_K, T), jnp.int32), jax.ShapeDtypeStruct((TOP_K, T), _F32)]
    return pl.pallas_call(
        functools.partial(_outproj_kernel, split, route),
        grid=(T // ROW_TILE,),
        in_specs=in_specs,
        out_specs=out_specs,
        out_shape=out_shape,
        scratch_shapes=[pltpu.VMEM((D_MODEL, D_MODEL), _BF)],
        compiler_params=_params(("arbitrary",)),
        name="outproj_route" if route else "outproj",
    )(*args)


def _ffn_kernel(nj, n_sub_rows, vis_e, vis_start, vis_cnt, used_sub,
                x_hbm, wg_ref, wu_ref, wd_ref, y_hbm,
                big, xb, wgb, wub, wdb, sem_in, sem_out):
    del vis_e
    v = pl.program_id(0)
    j = pl.program_id(1)
    cnt = vis_cnt[v]
    row0 = vis_start[v] * SUB

    x_is_bf16 = x_hbm.dtype == _BF
    landing = xb if x_is_bf16 else big

    def copy_in(s):
        r = pl.multiple_of(row0 + s * SUB, SUB)
        b = pl.multiple_of(s * SUB, SUB)
        return pltpu.make_async_copy(x_hbm.at[pl.ds(r, SUB)], landing.at[pl.ds(b, SUB)], sem_in.at[s])

    def copy_out(s):
        r = pl.multiple_of(row0 + s * SUB, SUB)
        b = pl.multiple_of(s * SUB, SUB)
        return pltpu.make_async_copy(big.at[pl.ds(b, SUB)], y_hbm.at[pl.ds(r, SUB)], sem_out)

    def for_subs(fn):
        lax.fori_loop(0, cnt, lambda s, c: (fn(s), c)[1], 0)

    @pl.when(jnp.logical_and(j == 0, cnt > 0))
    def _():
        for_subs(lambda s: copy_in(s).start())

    @pl.when(cnt > 0)
    def _():
        wgb[...] = wg_ref[...].astype(_BF)
        wub[...] = wu_ref[...].astype(_BF)
        wdb[...] = wd_ref[...].astype(_BF)

        def block(r0, nrows):
            s0 = r0 // SUB

            @pl.when(j == 0)
            def _():
                for t in range(nrows // SUB):
                    copy_in(s0 + t).wait()
                    rs = pl.ds(pl.multiple_of(r0 + t * SUB, SUB), SUB)
                    if not x_is_bf16:
                        xb[rs, :] = big[rs, :].astype(_BF)
                    big[rs, :] = jnp.zeros((SUB, D_MODEL), _F32)

            rs = pl.ds(r0, nrows)
            x = xb[rs, :]
            a = jnp.dot(x, wgb[...], preferred_element_type=_F32)
            b = jnp.dot(x, wub[...], preferred_element_type=_F32)
            p = (a * _sigmoid(a) * b).astype(_BF)
            big[rs, :] += jnp.dot(p, wdb[...], preferred_element_type=_F32)

            @pl.when(j == nj - 1)
            def _():
                for t in range(nrows // SUB):
                    copy_out(s0 + t).start()

        n_big = cnt // BLK_MIN
        rem = cnt - n_big * BLK_MIN
        room = BLK_SUBS - BLK_MIN
        grow = jnp.minimum(rem, n_big * room)
        left = rem - grow

        def one_block(i, r0):
            size = BLK_MIN + jnp.clip(grow - i * room, 0, room)
            for sz in range(BLK_MIN, BLK_SUBS + 1):
                pl.when(size == sz)(functools.partial(block, pl.multiple_of(r0, SUB), sz * SUB))
            return r0 + size * SUB
        r0 = lax.fori_loop(0, n_big, one_block, 0)
        bit = BLK_MIN // 2
        while bit >= 1:
            take = (left // bit) % 2 == 1
            pl.when(take)(functools.partial(block, pl.multiple_of(r0, SUB), bit * SUB))
            r0 = r0 + jnp.where(take, bit * SUB, 0)
            bit //= 2

    @pl.when(jnp.logical_and(j == nj - 1, cnt > 0))
    def _():
        for_subs(lambda s: copy_out(s).wait())

    if x_hbm.dtype == y_hbm.dtype:
        @pl.when(jnp.logical_and(v == pl.num_programs(0) - 1, j == nj - 1))
        def _():
            def tail(s):
                r = pl.multiple_of(s * SUB, SUB)
                return pltpu.make_async_copy(x_hbm.at[pl.ds(r, SUB)], y_hbm.at[pl.ds(r, SUB)], sem_out)
            lax.fori_loop(used_sub[0], n_sub_rows, lambda s, c: (tail(s).start(), c)[1], 0)
            lax.fori_loop(used_sub[0], n_sub_rows, lambda s, c: (tail(s).wait(), c)[1], 0)


def _grouped_ffn(x, wg, wu, wd, fc, vis_e, vis_start, vis_cnt, used_sub):
    ff = wg.shape[-1]
    nj = ff // fc
    nv = vis_e.shape[0]
    n_rows = x.shape[0]

    def chunk(v, j, vc):
        return jnp.where(vc[v] > 0, j, nj - 1)

    return pl.pallas_call(
        functools.partial(_ffn_kernel, nj, n_rows // SUB),
        grid_spec=pltpu.PrefetchScalarGridSpec(
            num_scalar_prefetch=4,
            grid=(nv, nj),
            in_specs=[
                pl.BlockSpec(memory_space=pl.ANY),
                pl.BlockSpec((None, D_MODEL, fc), lambda v, j, ve, vs, vc, us: (ve[v], 0, chunk(v, j, vc))),
                pl.BlockSpec((None, D_MODEL, fc), lambda v, j, ve, vs, vc, us: (ve[v], 0, chunk(v, j, vc))),
                pl.BlockSpec((None, fc, D_MODEL), lambda v, j, ve, vs, vc, us: (ve[v], chunk(v, j, vc), 0)),
            ],
            out_specs=pl.BlockSpec(memory_space=pl.ANY),
            scratch_shapes=[
                pltpu.VMEM((TMAX, D_MODEL), _F32),
                pltpu.VMEM((TMAX, D_MODEL), _BF),
                pltpu.VMEM((D_MODEL, fc), _BF),
                pltpu.VMEM((D_MODEL, fc), _BF),
                pltpu.VMEM((fc, D_MODEL), _BF),
                pltpu.SemaphoreType.DMA((SUB_MAX,)),
                pltpu.SemaphoreType.DMA(()),
            ],
        ),
        out_shape=jax.ShapeDtypeStruct((n_rows, D_MODEL), _F32),
        compiler_params=_params(("arbitrary", "arbitrary"), FFN_VMEM_LIMIT),
        name="ffn_%d" % ff,
    )(vis_e, vis_start, vis_cnt, used_sub, x, wg, wu, wd)


def _dense_visits():
    nv = T // TMAX
    return (jnp.zeros((nv,), jnp.int32),
            jnp.arange(nv, dtype=jnp.int32) * SUB_MAX,
            jnp.full((nv,), SUB_MAX, jnp.int32),
            jnp.full((1,), T // SUB, jnp.int32))


N_SLOT_SUB = -(-(TOP_K * T + N_TILES * N_EXPERTS * (SEG_ALIGN - 1)) // SUB) + N_EXPERTS
N_SLOT = N_SLOT_SUB * SUB
N_VISIT = -(-N_SLOT_SUB // SUB_MAX) + N_EXPERTS


def _routing_plan(eidx):
    e_loc = eidx.reshape(TOP_K, N_TILES, DISP_TILE).transpose(1, 0, 2).reshape(N_TILES, PAIRS)
    onehot = e_loc[:, :, None] == jnp.arange(N_EXPERTS, dtype=jnp.int32)[None, None, :]
    ch = 128
    oh = onehot.astype(_F32).reshape(N_TILES, PAIRS // ch, ch, N_EXPERTS)
    tri = (jnp.arange(ch)[:, None] >= jnp.arange(ch)[None, :]).astype(_F32)
    within = jnp.einsum("ij,tcjk->tcik", tri, oh)
    tot = within[:, :, -1, :]
    csum = (within + (jnp.cumsum(tot, axis=1) - tot)[:, :, None, :]).reshape(N_TILES, PAIRS, N_EXPERTS)
    csum = csum.astype(jnp.int32)
    n_te = (csum[:, -1, :] + SEG_ALIGN - 1) // SEG_ALIGN * SEG_ALIGN
    src = jnp.cumsum(n_te, axis=1) - n_te
    lpos = jnp.sum(jnp.where(onehot, csum - 1 + src[:, None, :], 0), axis=2).astype(jnp.int32)
    counts = jnp.sum(n_te, axis=0)
    nsub = (counts + SUB - 1) // SUB
    sub_base = jnp.cumsum(nsub) - nsub
    dst = (sub_base * SUB)[None, :] + jnp.cumsum(n_te, axis=0) - n_te
    seg = tuple(a.reshape(-1).astype(jnp.int32) for a in (n_te, src, dst))
    pads = ((sub_base * SUB + counts).astype(jnp.int32), (nsub * SUB - counts).astype(jnp.int32))
    used_sub = jnp.sum(nsub).reshape(1).astype(jnp.int32)
    nvis = (nsub + SUB_MAX - 1) // SUB_MAX
    vend = jnp.cumsum(nvis)
    total = vend[-1]
    vid = jnp.arange(N_VISIT, dtype=jnp.int32)
    ve = jnp.minimum(jnp.sum((vid[:, None] >= vend[None, :]).astype(jnp.int32), axis=1), N_EXPERTS - 1)
    local = vid - (vend - nvis)[ve]
    nv_e = jnp.maximum(nvis[ve], 1)
    q, r = nsub[ve] // nv_e, nsub[ve] % nv_e
    cnt = q + (local < r).astype(jnp.int32)
    start = sub_base[ve] + local * q + jnp.minimum(local, r)
    used = vid < total
    last_e = ve[jnp.maximum(total - 1, 0)]
    vis_e = jnp.where(used, ve, last_e).astype(jnp.int32)
    vis_cnt = jnp.where(used, cnt, 0).astype(jnp.int32)
    vis_start = jnp.where(used, start, 0).astype(jnp.int32)
    return lpos.reshape(N_TILES, TOP_K, DISP_TILE), seg, pads, (vis_e, vis_start, vis_cnt, used_sub)


def _for_pow2_pieces(n, max_piece, fn):
    off = 0
    bit = max_piece
    while bit >= SEG_ALIGN:
        take = (n // bit) % 2 == 1
        pl.when(take)(functools.partial(fn, off, bit))
        off = off + jnp.where(take, bit, 0)
        bit //= 2


def _segment_copies(tile, seg, make_copy, wait=False):
    n_te, src, dst = seg
    for e in range(N_EXPERTS):
        j = tile * N_EXPERTS + e
        s0, d0 = src[j], dst[j]

        def piece(off, size, s0=s0, d0=d0):
            cp = make_copy(pl.multiple_of(s0 + off, SEG_ALIGN), pl.multiple_of(d0 + off, SEG_ALIGN), size)
            cp.wait() if wait else cp.start()
        _for_pow2_pieces(n_te[j], DISP_TILE, piece)


def _dispatch_kernel(n_te, src, dst, pad_start, pad_cnt, used_sub, h_ref, lpos_ref, xs_hbm, ring, ring_sems, sem):
    i = pl.program_id(0)
    last = pl.num_programs(0) - 1
    b = i % 2

    def copies(tile, slot_id, wait):
        _segment_copies(tile, (n_te, src, dst), lambda s, d, size: pltpu.make_async_copy(
            ring.at[slot_id, pl.ds(s, size)], xs_hbm.at[pl.ds(d, size)], ring_sems.at[slot_id]), wait)

    pl.when(i >= 2)(lambda: copies(i - 2, b, True))
    lp = lpos_ref[...]
    srow = lax.broadcasted_iota(jnp.int32, (SORT_ROWS, DISP_TILE), 0)
    perm = jnp.where(lp[0:1, :] == srow, 1.0, jnp.where(lp[1:2, :] == srow, 1.0, 0.0)).astype(_BF)
    ring[b] = jnp.dot(perm, h_ref[...].astype(_BF), preferred_element_type=_F32)
    copies(i, b, False)

    @pl.when(i == last)
    def _():
        copies(i - 1, 1 - b, True)
        copies(i, b, True)
        for e in range(N_EXPERTS):
            p0 = pad_start[e]

            def fill(off, size, p0=p0):
                cp = pltpu.make_async_copy(ring.at[b, pl.ds(0, size)],
                                           xs_hbm.at[pl.ds(pl.multiple_of(p0 + off, SEG_ALIGN), size)], sem)
                cp.start()
                cp.wait()
            _for_pow2_pieces(pad_cnt[e], SUB // 2, fill)

        def tail(s):
            r = pl.multiple_of(s * SUB, SUB)
            return pltpu.make_async_copy(ring.at[b, pl.ds(0, SUB)], xs_hbm.at[pl.ds(r, SUB)], sem)
        lax.fori_loop(used_sub[0], N_SLOT_SUB, lambda s, c: (tail(s).start(), c)[1], 0)
        lax.fori_loop(used_sub[0], N_SLOT_SUB, lambda s, c: (tail(s).wait(), c)[1], 0)


def _dispatch(h, lpos, seg, pads, used_sub):
    return pl.pallas_call(
        _dispatch_kernel,
        grid_spec=pltpu.PrefetchScalarGridSpec(
            num_scalar_prefetch=6,
            grid=(N_TILES,),
            in_specs=[pl.BlockSpec((DISP_TILE, D_MODEL), lambda i, *_: (i, 0)),
                      pl.BlockSpec((None, TOP_K, DISP_TILE), lambda i, *_: (i, 0, 0))],
            out_specs=pl.BlockSpec(memory_space=pl.ANY),
            scratch_shapes=[pltpu.VMEM((2, SORT_ROWS, D_MODEL), _F32),
                            pltpu.SemaphoreType.DMA((2,)), pltpu.SemaphoreType.DMA(())],
        ),
        out_shape=jax.ShapeDtypeStruct((N_SLOT, D_MODEL), _F32),
        compiler_params=_params(("arbitrary",)),
        name="dispatch",
    )(*seg, *pads, used_sub, h, lpos)


def _residual_out(x, gate, f, g_ref, o_refs, i, tile):
    out = x + gate * f
    if g_ref is not None:
        ms = jnp.mean(out * out, axis=-1, keepdims=True)
        out = out * lax.rsqrt(ms + EPS) * g_ref[...]
    if len(o_refs) == 1:
        o_refs[0][...] = out
    else:
        @pl.when(i < T_PROMPT // tile)
        def _():
            o_refs[0][...] = out

        @pl.when(i >= T_PROMPT // tile)
        def _():
            o_refs[1][...] = out


def _out_specs(final, tile):
    if not final:
        return ([pl.BlockSpec((tile, D_MODEL), lambda i, *_: (i, 0))],
                [jax.ShapeDtypeStruct((T, D_MODEL), _F32)])
    return (_half_specs(D_MODEL, tile),
            [jax.ShapeDtypeStruct((T_PROMPT, D_MODEL), _F32), jax.ShapeDtypeStruct((T_SAMPLE, D_MODEL), _F32)])


def _combine_kernel(final, x_ref, mod_ref, y_ref, *rest):
    g_ref, o_refs = (rest[0], rest[1:]) if final else (None, rest)
    _residual_out(x_ref[...], mod_ref[5:6, :], y_ref[...], g_ref, o_refs, pl.program_id(0), ROW_TILE)


def _combine(x, l, mods, y, g_final=None):
    final = g_final is not None
    row = lambda i: (i, 0)
    in_specs = [
        pl.BlockSpec((ROW_TILE, D_MODEL), row),
        _layer_specs(l, ())[0],
        pl.BlockSpec((ROW_TILE, D_MODEL), row),
    ]
    args = [x, mods, y]
    if final:
        in_specs.append(pl.BlockSpec((1, D_MODEL), lambda i: (0, 0)))
        args.append(g_final.reshape(1, D_MODEL))
    out_specs, out_shape = _out_specs(final, ROW_TILE)
    return pl.pallas_call(
        functools.partial(_combine_kernel, final),
        grid=(T // ROW_TILE,),
        in_specs=in_specs,
        out_specs=out_specs,
        out_shape=out_shape,
        compiler_params=_params(("arbitrary",)),
        name="combine%s" % ("_final" if final else ""),
    )(*args)


def _combine_top2_kernel(final, n_out, n_te, src, dst, x_ref, mod_ref, w_ref, lpos_ref, *rest):
    g_ref, rest = (rest[0], rest[1:]) if final else (None, rest)
    ys_hbm, o_refs, (ybuf, sems) = rest[0], rest[1:1 + n_out], rest[1 + n_out:]
    i = pl.program_id(0)

    def fetch(tile, b, wait=False):
        _segment_copies(tile, (n_te, src, dst), lambda s, d, size: pltpu.make_async_copy(
            ys_hbm.at[pl.ds(d, size)], ybuf.at[b, pl.ds(s, size)], sems.at[b]), wait)

    @pl.when(i == 0)
    def _():
        for slot_id in range(2):
            ybuf[slot_id, PAIRS:SORT_ROWS, :] = jnp.zeros((SORT_ROWS - PAIRS, D_MODEL), _F32)
        fetch(0, 0)

    @pl.when(i + 1 < pl.num_programs(0))
    def _():
        fetch(i + 1, (i + 1) % 2)

    b = i % 2
    fetch(i, b, wait=True)
    y = ybuf[b].astype(_BF)
    lp = lpos_ref[...]
    scol = lax.broadcasted_iota(jnp.int32, (DISP_TILE, SORT_ROWS), 1)
    w = w_ref[...]
    pick = jnp.zeros((DISP_TILE, SORT_ROWS), _F32)
    for k in range(TOP_K):
        pick = jnp.where(lp[:, k:k + 1] == scol, w[:, k:k + 1], pick)
    f = jnp.dot(pick.astype(_BF), y, preferred_element_type=_F32)
    _residual_out(x_ref[...], mod_ref[5:6, :], f, g_ref, o_refs, i, DISP_TILE)


def _combine_top2(x, l, mods, ys, lpos_t, seg, w, g_final=None):
    final = g_final is not None
    row = lambda i, *_: (i, 0)
    in_specs = [
        pl.BlockSpec((DISP_TILE, D_MODEL), row),
        pl.BlockSpec((None, None, 6, D_MODEL), lambda i, *_: (l, _cond_of_tile(i, DISP_TILE), 0, 0)),
        pl.BlockSpec((DISP_TILE, TOP_K), row),
        pl.BlockSpec((None, DISP_TILE, TOP_K), lambda i, *_: (i, 0, 0)),
    ]
    args = [x, mods, w, lpos_t]
    if final:
        in_specs.append(pl.BlockSpec((1, D_MODEL), lambda i, *_: (0, 0)))
        args.append(g_final.reshape(1, D_MODEL))
    in_specs.append(pl.BlockSpec(memory_space=pl.ANY))
    args.append(ys)
    out_specs, out_shape = _out_specs(final, DISP_TILE)
    return pl.pallas_call(
        functools.partial(_combine_top2_kernel, final, len(out_specs)),
        grid_spec=pltpu.PrefetchScalarGridSpec(
            num_scalar_prefetch=3,
            grid=(N_TILES,),
            in_specs=in_specs,
            out_specs=out_specs,
            scratch_shapes=[
                pltpu.VMEM((2, SORT_ROWS, D_MODEL), _F32),
                pltpu.SemaphoreType.DMA((2,)),
            ],
        ),
        out_shape=out_shape,
        compiler_params=_params(("arbitrary",)),
        name="combine_top2%s" % ("_final" if final else ""),
    )(*seg, *args)


def kernel(x_prompt, x_sample, cache_k, cache_v, c, c_ctx, w_ada, b_ada, g_norm_mix, g_norm_ffn,
           w_in, w_out, attn_sink, conv_dw, conv_dw_b, conv_ln_g, conv_ln_b, conv_pw, fnet_w,
           ffn_w_gate, ffn_w_up, ffn_w_down, moe_router, moe_w_gate, moe_w_up, moe_w_down, g_final):
    xs = (x_prompt.reshape(T_PROMPT, D_MODEL), x_sample.reshape(T_SAMPLE, D_MODEL))
    cond8 = jnp.concatenate([c_ctx[None, :], c, jnp.zeros((N_COND - 1 - DEC_BATCH, D_MODEL), _F32)], axis=0)
    mods = _modulation(cond8, w_ada, b_ada).reshape(DEPTH, N_COND, 6, D_MODEL)
    cos_t, sin_t = _rope_tables()
    ck_all = cache_k.reshape(DEC_BATCH, DEPTH, PAST_LEN, KV_W)
    cv_all = cache_v.reshape(DEC_BATCH, DEPTH, PAST_LEN, KV_W)
    p_blocks = T_PROMPT // DEC_SEQ
    sinks = attn_sink.reshape(DEPTH * N_HEADS)

    ks, vs = [], []
    resid = None
    for l in range(DEPTH):
        q, k, v, uc, uf, *x_new = _inproj(xs, l, mods, g_norm_mix, w_in, cos_t, sin_t, resid=resid)
        if resid is not None:
            xs, resid = tuple(x_new), None
        ks.append(k[:T_PROMPT])
        vs.append(v[:T_PROMPT])
        attn = (_ctx_attention(l, sinks, q, k, v),
                _lat_attention(l, sinks, q, k, v, ck_all, cv_all))
        cargs = (l, conv_dw, conv_dw_b, conv_ln_g, conv_ln_b, conv_pw)
        conv = (_conv_module(uc, SEQ, BATCH, 0, *cargs),
                _conv_module(uc, DEC_SEQ, DEC_BATCH, p_blocks, *cargs))
        four = (_fourier_mix(uf[:T_PROMPT].reshape(BATCH, SEQ // 2, 2 * FNET_W), l, fnet_w),
                _fourier_mix(uf[T_PROMPT:].reshape(DEC_BATCH, DEC_SEQ // 2, 2 * FNET_W), l, fnet_w))
        last = g_final if l == DEPTH - 1 else None
        i = l // 2
        if l % 2 == 0:
            x, h = _outproj((attn, conv, four), xs, l, mods, w_out, g_norm_ffn)
            y = _grouped_ffn(h, ffn_w_gate[i:i + 1], ffn_w_up[i:i + 1], ffn_w_down[i:i + 1], 256,
                             *_dense_visits())
            if last is None:
                xs, resid = (x,), (y, l)
            else:
                xs = tuple(_combine(x, l, mods, y, g_final=last))
        else:
            x, h, eidx, ew = _outproj((attn, conv, four), xs, l, mods, w_out, g_norm_ffn,
                                      router_t=moe_router[i].T)
            lpos, seg, pads, visits = _routing_plan(eidx)
            xd = _dispatch(h, lpos, seg, pads, visits[-1])
            ys = _grouped_ffn(xd, moe_w_gate[i], moe_w_up[i], moe_w_down[i], 512, *visits)
            xs = tuple(_combine_top2(x, l, mods, ys, lpos.transpose(0, 2, 1), seg, ew.T, g_final=last))

    y_prompt = xs[0].reshape(BATCH, SEQ, D_MODEL)
    y_sample = xs[1].reshape(DEC_BATCH, DEC_SEQ, D_MODEL)
    state_k = jnp.stack([a.reshape(BATCH, SEQ, N_KV_HEADS, HEAD_DIM) for a in ks], axis=1)
    state_v = jnp.stack([a.reshape(BATCH, SEQ, N_KV_HEADS, HEAD_DIM) for a in vs], axis=1)
    return (y_prompt, y_sample, state_k, state_v)
```

```python
import functools

import numpy as np
import jax
import jax.numpy as jnp
from jax import lax
from jax.experimental import pallas as pl
from jax.experimental.pallas import tpu as pltpu

D_MODEL = 1024
BATCH = 16
SEQ = 256
DEPTH = 2
DEC_BATCH = 2
DEC_SEQ = 2048
PAST_LEN = 512
GRID_W = 64
HEAD_DIM = 64
N_HEADS = 8
N_KV_HEADS = 2
GQA_GROUP = N_HEADS // N_KV_HEADS
ATTN_W = N_HEADS * HEAD_DIM
KV_W = N_KV_HEADS * HEAD_DIM
WINDOW = 128
BLOCK = 128
ROPE_THETA = 10000.0
CONV_W = D_MODEL // 4
CONV_K = 31
FNET_GROUPS = 4
FNET_W = D_MODEL // 4
FNET_GW = FNET_W // FNET_GROUPS
IN_WIDTH = ATTN_W + 2 * KV_W + 2 * CONV_W + FNET_W
D_FF = 2816
N_EXPERTS = 8
TOP_K = 2
D_FF_EXPERT = 3584
EPS = 1e-6
NEG = -1e30

T_PROMPT = BATCH * SEQ
T_SAMPLE = DEC_BATCH * DEC_SEQ
T = T_PROMPT + T_SAMPLE
N_COND = 8

ROW_TILE = 512
SUB = 128
SUB_MAX = 32
TMAX = SUB * SUB_MAX
BLK_MIN = 8
BLK_SUBS = 10
DISP_TILE = 512
N_TILES = T // DISP_TILE
PAIRS = TOP_K * DISP_TILE
SEG_ALIGN = 8
SORT_ROWS = 1152
CONV_CHUNK = 64
CONV_PAD = 16
CONV_SPAN = CONV_CHUNK + 8 * ((CONV_PAD + CONV_K // 2) // 8)
VMEM_LIMIT = 48 * 1024 * 1024
FFN_VMEM_LIMIT = 56 * 1024 * 1024

_BF = jnp.bfloat16
_F32 = jnp.float32


def _cond_of_tile(i, tile):
    r = i * tile
    return jnp.where(r < T_PROMPT, 0, 1 + (r - T_PROMPT) // DEC_SEQ)


def _params(sem, vmem=VMEM_LIMIT):
    return pltpu.CompilerParams(dimension_semantics=sem, vmem_limit_bytes=vmem)


def _sigmoid(x):
    return 1.0 / (1.0 + jnp.exp(-x))


def _mod_kernel(cond_ref, w_ref, b_ref, o_ref):
    cnd = cond_ref[...]
    s = (cnd * _sigmoid(cnd)).astype(_BF)
    o_ref[...] = jnp.dot(s, w_ref[...].astype(_BF), preferred_element_type=_F32) + b_ref[...]


def _modulation(cond8, w_ada, b_ada):
    nt = 1536
    return pl.pallas_call(
        _mod_kernel,
        grid=(DEPTH, 6 * D_MODEL // nt),
        in_specs=[
            pl.BlockSpec((N_COND, D_MODEL), lambda l, n: (0, 0)),
            pl.BlockSpec((None, D_MODEL, nt), lambda l, n: (l, 0, n)),
            pl.BlockSpec((None, 1, nt), lambda l, n: (l, 0, n)),
        ],
        out_specs=pl.BlockSpec((None, N_COND, nt), lambda l, n: (l, 0, n)),
        out_shape=jax.ShapeDtypeStruct((DEPTH, N_COND, 6 * D_MODEL), _F32),
        compiler_params=_params(("arbitrary", "arbitrary")),
        name="modulation",
    )(cond8, w_ada, b_ada.reshape(DEPTH, 1, 6 * D_MODEL))


def _norm_mod(x, g, shift, scale):
    ms = jnp.mean(x * x, axis=-1, keepdims=True)
    y = x * lax.rsqrt(ms + EPS) * g
    return y * (1.0 + scale) + shift


def _rope_tables():
    rows = DEC_SEQ // GRID_W
    n_freq = HEAD_DIM // 4
    inv = ROPE_THETA ** (-jnp.arange(n_freq, dtype=_F32) / n_freq)
    gr, gc = jnp.meshgrid(jnp.arange(rows, dtype=_F32), jnp.arange(GRID_W, dtype=_F32), indexing="ij")
    ang_r = gr.reshape(-1)[:, None] * inv
    ang_c = gc.reshape(-1)[:, None] * inv
    cr, sr, cc, sc = jnp.cos(ang_r), jnp.sin(ang_r), jnp.cos(ang_c), jnp.sin(ang_c)
    cos64 = jnp.concatenate([cr, cr, cc, cc], axis=-1)
    sin64 = jnp.concatenate([-sr, sr, -sc, sc], axis=-1)
    return jnp.tile(cos64, (1, 2)), jnp.tile(sin64, (1, 2))


def _rope128(x, cos, sin):
    lane = lax.broadcasted_iota(jnp.int32, x.shape, 1)
    first = (lane % 32) < 16
    partner = jnp.where(first, pltpu.roll(x, 128 - 16, 1), pltpu.roll(x, 16, 1))
    return x * cos + partner * sin


def _half_specs(width, tile):
    pt = T_PROMPT // tile
    return [pl.BlockSpec((tile, width), lambda i, *_: (jnp.minimum(i, pt - 1), 0)),
            pl.BlockSpec((tile, width), lambda i, *_: (jnp.maximum(i - pt, 0), 0))]


def _pick_half(i, tile, a_ref, b_ref):
    return jnp.where(i < T_PROMPT // tile, a_ref[...], b_ref[...])


def _layer_specs(l, w_shape):
    return [
        pl.BlockSpec((None, None, 6, D_MODEL), lambda i, *_: (l, _cond_of_tile(i, ROW_TILE), 0, 0)),
        pl.BlockSpec((None, 1, D_MODEL), lambda i, *_: (l, 0, 0)),
        pl.BlockSpec((None,) + w_shape, lambda i, *_: (l, 0, 0)),
    ]


def _inproj_kernel(split, resid, *refs):
    n_x = 2 if split else 1
    x_refs, refs = refs[:n_x], refs[n_x:]
    if resid:
        y_ref, modp_ref = refs[:2]
        refs = refs[2:]
    mod_ref, g_ref, w_ref, cos_ref, sin_ref, q_ref, k_ref, v_ref, uc_ref, uf_ref = refs[:10]
    wb_ref, uf_scr = refs[-2:]
    i = pl.program_id(0)

    @pl.when(i == 0)
    def _():
        wb_ref[...] = w_ref[...].astype(_BF)

    x = _pick_half(i, ROW_TILE, *x_refs) if split else x_refs[0][...]
    if resid:
        x = x + modp_ref[5:6, :] * y_ref[...]
        refs[10][...] = x
    h = _norm_mod(x, g_ref[...], mod_ref[0:1, :], mod_ref[1:2, :]).astype(_BF)
    latent = i >= T_PROMPT // ROW_TILE
    cos = jnp.where(latent, cos_ref[...], 1.0)
    sin = jnp.where(latent, sin_ref[...], 0.0)

    def proj(o, n):
        return jnp.dot(h, wb_ref[:, o:o + n], preferred_element_type=_F32)

    o = 0
    for c in range(ATTN_W // 256):
        qq = proj(o, 256)
        q_ref[:, o:o + 128] = _rope128(qq[:, :128], cos, sin).astype(q_ref.dtype)
        q_ref[:, o + 128:o + 256] = _rope128(qq[:, 128:], cos, sin).astype(q_ref.dtype)
        o += 256
    kv = proj(o, 2 * KV_W)
    k_ref[...] = _rope128(kv[:, :KV_W], cos, sin)
    v_ref[...] = kv[:, KV_W:]
    o += 2 * KV_W
    uc_ref[...] = proj(o, 2 * CONV_W)
    o += 2 * CONV_W
    uf = proj(o, FNET_W)
    n_lane = FNET_W // 128
    for c in range(n_lane):
        uf_scr[c] = uf[:, c * 128:(c + 1) * 128]
    half = ROW_TILE // 2
    uf_ref[...] = jnp.concatenate([uf_scr[c, pl.ds(par, half, stride=2), :]
                                   for par in range(2) for c in range(n_lane)], axis=-1).astype(uf_ref.dtype)


def _inproj(xs, l, mods, g_all, w_all, cos_t, sin_t, resid=None):
    split = len(xs) == 2
    pt = T_PROMPT // ROW_TILE
    per_seq = DEC_SEQ // ROW_TILE

    def rope_idx(i):
        return (jnp.maximum(i - pt, 0) % per_seq, 0)

    row = lambda i: (i, 0)
    in_specs = _half_specs(D_MODEL, ROW_TILE) if split else [pl.BlockSpec((ROW_TILE, D_MODEL), row)]
    args = list(xs)
    outs = [ATTN_W, KV_W, KV_W, 2 * CONV_W, FNET_W]
    dtypes = [_BF, _F32, _F32, _F32, _BF]
    fold = [1, 1, 1, 1, 2]
    if resid is not None:
        y, l_prev = resid
        in_specs += [pl.BlockSpec((ROW_TILE, D_MODEL), row), _layer_specs(l_prev, ())[0]]
        args += [y, mods]
        outs.append(D_MODEL)
        dtypes.append(_F32)
        fold.append(1)
    return pl.pallas_call(
        functools.partial(_inproj_kernel, split, resid is not None),
        grid=(T // ROW_TILE,),
        in_specs=in_specs + _layer_specs(l, (D_MODEL, IN_WIDTH)) + [
            pl.BlockSpec((ROW_TILE, 128), rope_idx),
            pl.BlockSpec((ROW_TILE, 128), rope_idx),
        ],
        out_specs=[pl.BlockSpec((ROW_TILE // f, n * f), row) for n, f in zip(outs, fold)],
        out_shape=[jax.ShapeDtypeStruct((T // f, n * f), dt) for n, dt, f in zip(outs, dtypes, fold)],
        scratch_shapes=[pltpu.VMEM((D_MODEL, IN_WIDTH), _BF), pltpu.VMEM((FNET_W // 128, ROW_TILE, 128), _F32)],
        compiler_params=_params(("arbitrary",)),
        name="inproj_resid" if resid is not None else "inproj",
    )(*args, mods, g_all.reshape(DEPTH, 1, D_MODEL), w_all, cos_t, sin_t)


def _stack_groups(q, kh, rows):
    parts = [q[:, (kh * GQA_GROUP + g) * HEAD_DIM:(kh * GQA_GROUP + g + 1) * HEAD_DIM] for g in range(GQA_GROUP)]
    return jnp.concatenate(parts, axis=0)


def _sink_column(sink_ref, head0, rows):
    r = lax.broadcasted_iota(jnp.int32, (GQA_GROUP * rows, 1), 0)
    col = jnp.full((GQA_GROUP * rows, 1), sink_ref[head0], _F32)
    for g in range(1, GQA_GROUP):
        col = jnp.where(r >= g * rows, sink_ref[head0 + g], col)
    return col


def _unstack_store(o_ref, o, kh, rows):
    for pair in range(GQA_GROUP // 2):
        a = o[(2 * pair) * rows:(2 * pair + 1) * rows]
        b = o[(2 * pair + 1) * rows:(2 * pair + 2) * rows]
        c0 = (kh * GQA_GROUP + 2 * pair) * HEAD_DIM
        o_ref[:, c0:c0 + 2 * HEAD_DIM] = jnp.concatenate([a, b], axis=-1).astype(o_ref.dtype)


_NT = (((1,), (1,)), ((), ()))


def _ctx_attn_kernel(l, sink_ref, q_ref, k_ref, v_ref, o_ref):
    scale = HEAD_DIM ** -0.5
    q = q_ref[...] * scale
    for kh in range(N_KV_HEADS):
        kk = k_ref[:, kh * HEAD_DIM:(kh + 1) * HEAD_DIM].astype(_BF)
        vv = v_ref[:, kh * HEAD_DIM:(kh + 1) * HEAD_DIM].astype(_BF)
        qs = _stack_groups(q, kh, SEQ).astype(_BF)
        s = lax.dot_general(qs, kk, _NT, preferred_element_type=_F32)
        sink = _sink_column(sink_ref, l * N_HEADS + kh * GQA_GROUP, SEQ)
        m = jnp.maximum(jnp.max(s, axis=-1, keepdims=True), sink)
        p = jnp.exp(s - m)
        den = jnp.sum(p, axis=-1, keepdims=True) + jnp.exp(sink - m)
        o = jnp.dot(p.astype(_BF), vv, preferred_element_type=_F32) / den
        _unstack_store(o_ref, o, kh, SEQ)


def _ctx_attention(l, sinks, q, k, v):
    row = lambda b, s: (b, 0)
    return pl.pallas_call(
        functools.partial(_ctx_attn_kernel, l),
        grid_spec=pltpu.PrefetchScalarGridSpec(
            num_scalar_prefetch=1,
            grid=(BATCH,),
            in_specs=[
                pl.BlockSpec((SEQ, ATTN_W), row),
                pl.BlockSpec((SEQ, KV_W), row),
                pl.BlockSpec((SEQ, KV_W), row),
            ],
            out_specs=pl.BlockSpec((SEQ, ATTN_W), row),
        ),
        out_shape=jax.ShapeDtypeStruct((T_PROMPT, ATTN_W), _BF),
        compiler_params=_params(("arbitrary",)),
        name="ctx_attention",
    )(sinks, q, k, v)


def _lat_attn_kernel(l, sink_ref, q_ref, k_ref, v_ref, ck_ref, cv_ref, o_ref):
    n = pl.program_id(1)
    scale = HEAD_DIM ** -0.5
    nb = DEC_SEQ // BLOCK
    band = 3 * BLOCK
    start = pl.multiple_of(jnp.clip(n - 1, 0, nb - 3) * BLOCK, BLOCK)
    q = q_ref[...] * scale
    kb = k_ref[pl.ds(start, band), :]
    vb = v_ref[pl.ds(start, band), :]
    dpos = (lax.broadcasted_iota(jnp.int32, (BLOCK, band), 1) - lax.broadcasted_iota(jnp.int32, (BLOCK, band), 0)
            + (start - n * BLOCK))
    in_window = jnp.where(jnp.abs(dpos) <= WINDOW, 0.0, NEG)
    bias = jnp.concatenate([in_window] * GQA_GROUP, axis=0)
    for kh in range(N_KV_HEADS):
        hs = slice(kh * HEAD_DIM, (kh + 1) * HEAD_DIM)
        qs = _stack_groups(q, kh, BLOCK).astype(_BF)
        s_loc = lax.dot_general(qs, kb[:, hs].astype(_BF), _NT, preferred_element_type=_F32)
        s_loc = jnp.where(bias < 0.0, NEG, s_loc)
        s_ctx = lax.dot_general(qs, ck_ref[:, hs].astype(_BF), _NT, preferred_element_type=_F32)
        sink = _sink_column(sink_ref, l * N_HEADS + kh * GQA_GROUP, BLOCK)
        m = jnp.maximum(jnp.maximum(jnp.max(s_loc, axis=-1, keepdims=True),
                                    jnp.max(s_ctx, axis=-1, keepdims=True)), sink)
        p_loc = jnp.exp(s_loc - m)
        p_ctx = jnp.exp(s_ctx - m)
        den = (jnp.sum(p_loc, axis=-1, keepdims=True) + jnp.sum(p_ctx, axis=-1, keepdims=True)
               + jnp.exp(sink - m))
        o = (jnp.dot(p_loc.astype(_BF), vb[:, hs].astype(_BF), preferred_element_type=_F32)
             + jnp.dot(p_ctx.astype(_BF), cv_ref[:, hs].astype(_BF), preferred_element_type=_F32)) / den
        _unstack_store(o_ref, o, kh, BLOCK)


def _lat_attention(l, sinks, q, k, v, ck, cv):
    nb = DEC_SEQ // BLOCK
    q0 = T_PROMPT // BLOCK
    s0 = T_PROMPT // DEC_SEQ
    return pl.pallas_call(
        functools.partial(_lat_attn_kernel, l),
        grid_spec=pltpu.PrefetchScalarGridSpec(
            num_scalar_prefetch=1,
            grid=(DEC_BATCH, nb),
            in_specs=[
                pl.BlockSpec((BLOCK, ATTN_W), lambda b, n, s: (q0 + b * nb + n, 0)),
                pl.BlockSpec((DEC_SEQ, KV_W), lambda b, n, s: (s0 + b, 0)),
                pl.BlockSpec((DEC_SEQ, KV_W), lambda b, n, s: (s0 + b, 0)),
                pl.BlockSpec((None, None, PAST_LEN, KV_W), lambda b, n, s: (b, l, 0, 0)),
                pl.BlockSpec((None, None, PAST_LEN, KV_W), lambda b, n, s: (b, l, 0, 0)),
            ],
            out_specs=pl.BlockSpec((BLOCK, ATTN_W), lambda b, n, s: (b * nb + n, 0)),
        ),
        out_shape=jax.ShapeDtypeStruct((T_SAMPLE, ATTN_W), _BF),
        compiler_params=_params(("arbitrary", "arbitrary")),
        name="lat_attention",
    )(sinks, q, k, v, ck, cv)


def _conv_kernel(seq, u_ref, dw_ref, dwb_ref, lg_ref, lb_ref, pw_ref, o_ref, pad_ref, y_ref, sh_ref):
    u = u_ref[...]
    pad_ref[0:CONV_PAD, :] = jnp.zeros((CONV_PAD, CONV_W), _F32)
    pad_ref[CONV_PAD + seq:2 * CONV_PAD + seq, :] = jnp.zeros((CONV_PAD, CONV_W), _F32)
    pad_ref[CONV_PAD:CONV_PAD + seq, :] = u[:, :CONV_W] * _sigmoid(u[:, CONV_W:])
    off = CONV_PAD - CONV_K // 2
    span = CONV_SPAN

    def chunk(c, carry):
        row = pl.multiple_of(c * CONV_CHUNK, CONV_CHUNK)
        win = pad_ref[pl.ds(row, CONV_CHUNK + 2 * CONV_PAD), :]
        acc = jnp.zeros((CONV_CHUNK, CONV_W), _F32) + dwb_ref[...]
        for phase in range(8):
            sh_ref[phase] = win[phase:phase + span, :]
        for phase in range(8):
            for a in range(span // 8):
                t = 8 * a + phase - off
                if 0 <= t < CONV_K:
                    acc = acc + sh_ref[phase, 8 * a:8 * a + CONV_CHUNK, :] * dw_ref[t:t + 1, :]
        mu = jnp.mean(acc, axis=-1, keepdims=True)
        d = acc - mu
        var = jnp.mean(d * d, axis=-1, keepdims=True)
        y = d * lax.rsqrt(var + EPS) * lg_ref[...] + lb_ref[...]
        y_ref[pl.ds(row, CONV_CHUNK), :] = (y * _sigmoid(y)).astype(_BF)
        return carry

    lax.fori_loop(0, seq // CONV_CHUNK, chunk, 0)
    o_ref[...] = jnp.dot(y_ref[...], pw_ref[...].astype(_BF), preferred_element_type=_F32).astype(o_ref.dtype)


def _conv_module(uc, seq, nbatch, block0, l, dw, dwb, lg, lb, pw):
    vec = lambda a: a.reshape(DEPTH, 1, CONV_W)
    layer = lambda b: (l, 0, 0)
    return pl.pallas_call(
        functools.partial(_conv_kernel, seq),
        grid=(nbatch,),
        in_specs=[
            pl.BlockSpec((seq, 2 * CONV_W), lambda b: (block0 + b, 0)),
            pl.BlockSpec((None, CONV_K, CONV_W), layer),
            pl.BlockSpec((None, 1, CONV_W), layer),
            pl.BlockSpec((None, 1, CONV_W), layer),
            pl.BlockSpec((None, 1, CONV_W), layer),
            pl.BlockSpec((None, CONV_W, CONV_W), layer),
        ],
        out_specs=pl.BlockSpec((seq, CONV_W), lambda b: (b, 0)),
        out_shape=jax.ShapeDtypeStruct((nbatch * seq, CONV_W), _BF),
        scratch_shapes=[pltpu.VMEM((seq + 2 * CONV_PAD, CONV_W), _F32), pltpu.VMEM((seq, CONV_W), _BF),
                        pltpu.VMEM((8, CONV_SPAN, CONV_W), _F32)],
        compiler_params=_params(("arbitrary",)),
        name="conv_module_%d" % seq,
    )(uc, dw, vec(dwb), vec(lg), vec(lb), pw)


def _dft_constants(seq):
    half = seq // 2
    j = np.arange(half, dtype=np.int64)
    ang = 2.0 * np.pi * ((j[:, None] * j[None, :]) % half).astype(np.float64) / half
    m = np.concatenate([np.cos(ang), -np.sin(ang)], axis=1) / np.sqrt(seq)
    tw = np.pi * j.astype(np.float64) / half
    cb = np.repeat(np.cos(tw)[:, None], FNET_W, axis=1)
    sb = np.repeat(np.sin(tw)[:, None], FNET_W, axis=1)
    c = np.arange(FNET_GW, dtype=np.int64)
    angc = 2.0 * np.pi * ((c[:, None] * c[None, :]) % FNET_GW).astype(np.float64) / FNET_GW
    eye = np.eye(FNET_GROUPS)
    w1 = np.concatenate([np.kron(eye, np.cos(angc)), np.kron(eye, np.sin(angc))], axis=1) / np.sqrt(FNET_GW)
    return tuple(jnp.asarray(a, dtype=_F32) for a in (m, cb, sb, w1))


def _fnet_kernel(half, u_ref, m_ref, cb_ref, sb_ref, w1_ref, fw_ref, o_ref, rhs_ref, ybot_ref):
    first = jnp.logical_and(pl.program_id(1) == 0, pl.program_id(2) == 0)

    @pl.when(first)
    def _():
        w1 = w1_ref[...].astype(_BF)
        te = jnp.dot(u_ref[:, :FNET_W], w1, preferred_element_type=_F32)
        to = jnp.dot(u_ref[:, FNET_W:], w1, preferred_element_type=_F32)
        ec, es = te[:, :FNET_W], te[:, FNET_W:]
        oc, os_ = to[:, :FNET_W], to[:, FNET_W:]
        for col, (top, bot) in enumerate(((ec, es), (oc, os_), (os_, -oc))):
            rhs_ref[0:half, col * FNET_W:(col + 1) * FNET_W] = top.astype(_BF)
            rhs_ref[half:2 * half, col * FNET_W:(col + 1) * FNET_W] = bot.astype(_BF)

    fw = fw_ref[...].astype(_BF)

    @pl.when(pl.program_id(2) == 0)
    def _():
        pqr = jnp.dot(m_ref[...].astype(_BF), rhs_ref[...], preferred_element_type=_F32)
        p = pqr[:, :FNET_W]
        rot = cb_ref[...] * pqr[:, FNET_W:2 * FNET_W] - sb_ref[...] * pqr[:, 2 * FNET_W:]
        ybot_ref[...] = (p - rot).astype(_BF)
        o_ref[...] = jnp.dot((p + rot).astype(_BF), fw, preferred_element_type=_F32).astype(o_ref.dtype)

    @pl.when(pl.program_id(2) == 1)
    def _():
        o_ref[...] = jnp.dot(ybot_ref[...], fw, preferred_element_type=_F32).astype(o_ref.dtype)


def _fourier_mix(u_pairs, seq, nbatch, block0, l, fw):
    half = seq // 2
    tr = min(half, 512)
    nr = half // tr
    m, cb, sb, w1 = _dft_constants(seq)
    return pl.pallas_call(
        functools.partial(_fnet_kernel, half),
        grid=(nbatch, nr, 2),
        in_specs=[
            pl.BlockSpec((half, 2 * FNET_W), lambda b, r, t: (block0 + b, 0)),
            pl.BlockSpec((tr, 2 * half), lambda b, r, t: (r, 0)),
            pl.BlockSpec((tr, FNET_W), lambda b, r, t: (r, 0)),
            pl.BlockSpec((tr, FNET_W), lambda b, r, t: (r, 0)),
            pl.BlockSpec((FNET_W, 2 * FNET_W), lambda b, r, t: (0, 0)),
            pl.BlockSpec((None, FNET_W, FNET_W), lambda b, r, t: (l, 0, 0)),
        ],
        out_specs=pl.BlockSpec((tr, FNET_W), lambda b, r, t: ((b * 2 + t) * nr + r, 0)),
        out_shape=jax.ShapeDtypeStruct((nbatch * seq, FNET_W), _BF),
        scratch_shapes=[pltpu.VMEM((2 * half, 3 * FNET_W), _BF), pltpu.VMEM((tr, FNET_W), _BF)],
        compiler_params=_params(("arbitrary", "arbitrary", "arbitrary")),
        name="fourier_mix_%d" % seq,
    )(u_pairs, m, cb, sb, w1, fw)


def _route_top2(r_ref, h, ei_ref, ew_ref):
    lg = lax.dot_general(r_ref[...].astype(_BF), h.astype(_BF), _NT, preferred_element_type=_F32)
    eid = lax.broadcasted_iota(jnp.int32, lg.shape, 0)
    m1 = jnp.max(lg, axis=0, keepdims=True)
    i1 = jnp.min(jnp.where(lg == m1, eid, N_EXPERTS), axis=0, keepdims=True)
    lg2 = jnp.where(eid == i1, -jnp.inf, lg)
    m2 = jnp.max(lg2, axis=0, keepdims=True)
    i2 = jnp.min(jnp.where(lg2 == m2, eid, N_EXPERTS), axis=0, keepdims=True)
    e = jnp.exp(m2 - m1)
    ei_ref[0:1, :] = i1
    ei_ref[1:2, :] = i2
    ew_ref[0:1, :] = 1.0 / (1.0 + e)
    ew_ref[1:2, :] = e / (1.0 + e)


def _outproj_kernel(split, route, *refs):
    branch_refs, refs = refs[:6], refs[6:]
    n_x = 2 if split else 1
    x_refs, refs = refs[:n_x], refs[n_x:]
    mod_ref, w_ref, gf_ref = refs[:3]
    refs = refs[3:]
    if route:
        r_ref, xo_ref, h_ref, ei_ref, ew_ref, wb_ref = refs
    else:
        xo_ref, h_ref, wb_ref = refs
    i = pl.program_id(0)

    @pl.when(i == 0)
    def _():
        wb_ref[...] = w_ref[...].astype(_BF)

    mix = jnp.concatenate([_pick_half(i, ROW_TILE, branch_refs[2 * n], branch_refs[2 * n + 1])
                           for n in range(3)], axis=-1)
    x = _pick_half(i, ROW_TILE, *x_refs) if split else x_refs[0][...]
    x_new = x + mod_ref[2:3, :] * jnp.dot(mix, wb_ref[...], preferred_element_type=_F32)
    xo_ref[...] = x_new
    h = _norm_mod(x_new, gf_ref[...], mod_ref[3:4, :], mod_ref[4:5, :])
    h_ref[...] = h.astype(h_ref.dtype)
    if route:
        _route_top2(r_ref, h, ei_ref, ew_ref)


def _outproj(branches, xs, l, mods, w_all, gf_all, router_t=None):
    split = len(xs) == 2
    route = router_t is not None
    row = lambda i: (i, 0)
    in_specs = []
    args = []
    for pair, width in zip(branches, (ATTN_W, CONV_W, FNET_W)):
        in_specs += _half_specs(width, ROW_TILE)
        args += list(pair)
    in_specs += _half_specs(D_MODEL, ROW_TILE) if split else [pl.BlockSpec((ROW_TILE, D_MODEL), row)]
    args += list(xs)
    mod_spec, gf_spec, w_spec = _layer_specs(l, (D_MODEL, D_MODEL))
    in_specs += [mod_spec, w_spec, gf_spec]
    args += [mods, w_all, gf_all.reshape(DEPTH, 1, D_MODEL)]
    out_specs = [pl.BlockSpec((ROW_TILE, D_MODEL), row)] * 2
    out_shape = [jax.ShapeDtypeStruct((T, D_MODEL), _F32), jax.ShapeDtypeStruct((T, D_MODEL), _BF)]
    if route:
        in_specs.append(pl.BlockSpec((N_EXPERTS, D_MODEL), lambda i: (0, 0)))
        args.append(router_t)
        out_specs += [pl.BlockSpec((TOP_K, ROW_TILE), lambda i: (0, i))] * 2
        out_shape += [jax.ShapeDtypeStruct((TOP_K, T), jnp.int32), jax.ShapeDtypeStruct((TOP_K, T), _F32)]
    return pl.pallas_call(
        functools.partial(_outproj_kernel, split, route),
        grid=(T // ROW_TILE,),
        in_specs=in_specs,
        out_specs=out_specs,
        out_shape=out_shape,
        scratch_shapes=[pltpu.VMEM((D_MODEL, D_MODEL), _BF)],
        compiler_params=_params(("arbitrary",)),
        name="outproj_route" if route else "outproj",
    )(*args)


def _ffn_kernel(nj, n_sub_rows, vis_e, vis_start, vis_cnt, used_sub,
                x_hbm, wg_ref, wu_ref, wd_ref, y_hbm,
                big, xb, wgb, wub, wdb, sem_in, sem_out):
    del vis_e
    v = pl.program_id(0)
    j = pl.program_id(1)
    cnt = vis_cnt[v]
    row0 = vis_start[v] * SUB

    x_is_bf16 = x_hbm.dtype == _BF
    landing = xb if x_is_bf16 else big

    def copy_in(s):
        r = pl.multiple_of(row0 + s * SUB, SUB)
        b = pl.multiple_of(s * SUB, SUB)
        return pltpu.make_async_copy(x_hbm.at[pl.ds(r, SUB)], landing.at[pl.ds(b, SUB)], sem_in.at[s])

    def copy_out(s):
        r = pl.multiple_of(row0 + s * SUB, SUB)
        b = pl.multiple_of(s * SUB, SUB)
        return pltpu.make_async_copy(big.at[pl.ds(b, SUB)], y_hbm.at[pl.ds(r, SUB)], sem_out)

    def for_subs(fn):
        lax.fori_loop(0, cnt, lambda s, c: (fn(s), c)[1], 0)

    @pl.when(jnp.logical_and(j == 0, cnt > 0))
    def _():
        for_subs(lambda s: copy_in(s).start())

    @pl.when(cnt > 0)
    def _():
        wgb[...] = wg_ref[...].astype(_BF)
        wub[...] = wu_ref[...].astype(_BF)
        wdb[...] = wd_ref[...].astype(_BF)

        def block(r0, nrows):
            s0 = r0 // SUB

            @pl.when(j == 0)
            def _():
                for t in range(nrows // SUB):
                    copy_in(s0 + t).wait()
                    rs = pl.ds(pl.multiple_of(r0 + t * SUB, SUB), SUB)
                    if not x_is_bf16:
                        xb[rs, :] = big[rs, :].astype(_BF)
                    big[rs, :] = jnp.zeros((SUB, D_MODEL), _F32)

            rs = pl.ds(r0, nrows)
            x = xb[rs, :]
            a = jnp.dot(x, wgb[...], preferred_element_type=_F32)
            b = jnp.dot(x, wub[...], preferred_element_type=_F32)
            p = (a * _sigmoid(a) * b).astype(_BF)
            big[rs, :] += jnp.dot(p, wdb[...], preferred_element_type=_F32)

            @pl.when(j == nj - 1)
            def _():
                for t in range(nrows // SUB):
                    copy_out(s0 + t).start()

        n_big = cnt // BLK_MIN
        rem = cnt - n_big * BLK_MIN
        room = BLK_SUBS - BLK_MIN
        grow = jnp.minimum(rem, n_big * room)
        left = rem - grow

        def one_block(i, r0):
            size = BLK_MIN + jnp.clip(grow - i * room, 0, room)
            for sz in range(BLK_MIN, BLK_SUBS + 1):
                pl.when(size == sz)(functools.partial(block, pl.multiple_of(r0, SUB), sz * SUB))
            return r0 + size * SUB
        r0 = lax.fori_loop(0, n_big, one_block, 0)
        bit = BLK_MIN // 2
        while bit >= 1:
            take = (left // bit) % 2 == 1
            pl.when(take)(functools.partial(block, pl.multiple_of(r0, SUB), bit * SUB))
            r0 = r0 + jnp.where(take, bit * SUB, 0)
            bit //= 2

    @pl.when(jnp.logical_and(j == nj - 1, cnt > 0))
    def _():
        for_subs(lambda s: copy_out(s).wait())

    if x_hbm.dtype == y_hbm.dtype:
        @pl.when(jnp.logical_and(v == pl.num_programs(0) - 1, j == nj - 1))
        def _():
            def tail(s):
                r = pl.multiple_of(s * SUB, SUB)
                return pltpu.make_async_copy(x_hbm.at[pl.ds(r, SUB)], y_hbm.at[pl.ds(r, SUB)], sem_out)
            lax.fori_loop(used_sub[0], n_sub_rows, lambda s, c: (tail(s).start(), c)[1], 0)
            lax.fori_loop(used_sub[0], n_sub_rows, lambda s, c: (tail(s).wait(), c)[1], 0)


def _grouped_ffn(x, wg, wu, wd, fc, vis_e, vis_start, vis_cnt, used_sub):
    ff = wg.shape[-1]
    nj = ff // fc
    nv = vis_e.shape[0]
    n_rows = x.shape[0]

    def chunk(v, j, vc):
        return jnp.where(vc[v] > 0, j, nj - 1)

    return pl.pallas_call(
        functools.partial(_ffn_kernel, nj, n_rows // SUB),
        grid_spec=pltpu.PrefetchScalarGridSpec(
            num_scalar_prefetch=4,
            grid=(nv, nj),
            in_specs=[
                pl.BlockSpec(memory_space=pl.ANY),
                pl.BlockSpec((None, D_MODEL, fc), lambda v, j, ve, vs, vc, us: (ve[v], 0, chunk(v, j, vc))),
                pl.BlockSpec((None, D_MODEL, fc), lambda v, j, ve, vs, vc, us: (ve[v], 0, chunk(v, j, vc))),
                pl.BlockSpec((None, fc, D_MODEL), lambda v, j, ve, vs, vc, us: (ve[v], chunk(v, j, vc), 0)),
            ],
            out_specs=pl.BlockSpec(memory_space=pl.ANY),
            scratch_shapes=[
                pltpu.VMEM((TMAX, D_MODEL), _F32),
                pltpu.VMEM((TMAX, D_MODEL), _BF),
                pltpu.VMEM((D_MODEL, fc), _BF),
                pltpu.VMEM((D_MODEL, fc), _BF),
                pltpu.VMEM((fc, D_MODEL), _BF),
                pltpu.SemaphoreType.DMA((SUB_MAX,)),
                pltpu.SemaphoreType.DMA(()),
            ],
        ),
        out_shape=jax.ShapeDtypeStruct((n_rows, D_MODEL), _F32),
        compiler_params=_params(("arbitrary", "arbitrary"), FFN_VMEM_LIMIT),
        name="ffn_%d" % ff,
    )(vis_e, vis_start, vis_cnt, used_sub, x, wg, wu, wd)


def _dense_visits():
    nv = T // TMAX
    return (jnp.zeros((nv,), jnp.int32),
            jnp.arange(nv, dtype=jnp.int32) * SUB_MAX,
            jnp.full((nv,), SUB_MAX, jnp.int32),
            jnp.full((1,), T // SUB, jnp.int32))


N_SLOT_SUB = -(-(TOP_K * T + N_TILES * N_EXPERTS * (SEG_ALIGN - 1)) // SUB) + N_EXPERTS
N_SLOT = N_SLOT_SUB * SUB
N_VISIT = -(-N_SLOT_SUB // SUB_MAX) + N_EXPERTS


def _routing_plan(eidx):
    e_loc = eidx.reshape(TOP_K, N_TILES, DISP_TILE).transpose(1, 0, 2).reshape(N_TILES, PAIRS)
    onehot = e_loc[:, :, None] == jnp.arange(N_EXPERTS, dtype=jnp.int32)[None, None, :]
    ch = 128
    oh = onehot.astype(_F32).reshape(N_TILES, PAIRS // ch, ch, N_EXPERTS)
    tri = (jnp.arange(ch)[:, None] >= jnp.arange(ch)[None, :]).astype(_F32)
    within = jnp.einsum("ij,tcjk->tcik", tri, oh)
    tot = within[:, :, -1, :]
    csum = (within + (jnp.cumsum(tot, axis=1) - tot)[:, :, None, :]).reshape(N_TILES, PAIRS, N_EXPERTS)
    csum = csum.astype(jnp.int32)
    n_te = (csum[:, -1, :] + SEG_ALIGN - 1) // SEG_ALIGN * SEG_ALIGN
    src = jnp.cumsum(n_te, axis=1) - n_te
    lpos = jnp.sum(jnp.where(onehot, csum - 1 + src[:, None, :], 0), axis=2).astype(jnp.int32)
    counts = jnp.sum(n_te, axis=0)
    nsub = (counts + SUB - 1) // SUB
    sub_base = jnp.cumsum(nsub) - nsub
    dst = (sub_base * SUB)[None, :] + jnp.cumsum(n_te, axis=0) - n_te
    seg = tuple(a.reshape(-1).astype(jnp.int32) for a in (n_te, src, dst))
    pads = ((sub_base * SUB + counts).astype(jnp.int32), (nsub * SUB - counts).astype(jnp.int32))
    used_sub = jnp.sum(nsub).reshape(1).astype(jnp.int32)
    nvis = (nsub + SUB_MAX - 1) // SUB_MAX
    vend = jnp.cumsum(nvis)
    total = vend[-1]
    vid = jnp.arange(N_VISIT, dtype=jnp.int32)
    ve = jnp.minimum(jnp.sum((vid[:, None] >= vend[None, :]).astype(jnp.int32), axis=1), N_EXPERTS - 1)
    local = vid - (vend - nvis)[ve]
    nv_e = jnp.maximum(nvis[ve], 1)
    q, r = nsub[ve] // nv_e, nsub[ve] % nv_e
    cnt = q + (local < r).astype(jnp.int32)
    start = sub_base[ve] + local * q + jnp.minimum(local, r)
    used = vid < total
    last_e = ve[jnp.maximum(total - 1, 0)]
    vis_e = jnp.where(used, ve, last_e).astype(jnp.int32)
    vis_cnt = jnp.where(used, cnt, 0).astype(jnp.int32)
    vis_start = jnp.where(used, start, 0).astype(jnp.int32)
    return lpos.reshape(N_TILES, TOP_K, DISP_TILE), seg, pads, (vis_e, vis_start, vis_cnt, used_sub)


def _for_pow2_pieces(n, max_piece, fn):
    off = 0
    bit = max_piece
    while bit >= SEG_ALIGN:
        take = (n // bit) % 2 == 1
        pl.when(take)(functools.partial(fn, off, bit))
        off = off + jnp.where(take, bit, 0)
        bit //= 2


def _segment_copies(tile, seg, make_copy, wait=False):
    n_te, src, dst = seg
    for e in range(N_EXPERTS):
        j = tile * N_EXPERTS + e
        s0, d0 = src[j], dst[j]

        def piece(off, size, s0=s0, d0=d0):
            cp = make_copy(pl.multiple_of(s0 + off, SEG_ALIGN), pl.multiple_of(d0 + off, SEG_ALIGN), size)
            cp.wait() if wait else cp.start()
        _for_pow2_pieces(n_te[j], DISP_TILE, piece)


def _dispatch_kernel(n_te, src, dst, pad_start, pad_cnt, used_sub, h_ref, lpos_ref, xs_hbm, ring, ring_sems, sem):
    i = pl.program_id(0)
    last = pl.num_programs(0) - 1
    b = i % 2

    def copies(tile, slot_id, wait):
        _segment_copies(tile, (n_te, src, dst), lambda s, d, size: pltpu.make_async_copy(
            ring.at[slot_id, pl.ds(s, size)], xs_hbm.at[pl.ds(d, size)], ring_sems.at[slot_id]), wait)

    pl.when(i >= 2)(lambda: copies(i - 2, b, True))
    lp = lpos_ref[...]
    srow = lax.broadcasted_iota(jnp.int32, (SORT_ROWS, DISP_TILE), 0)
    perm = jnp.where(lp[0:1, :] == srow, 1.0, jnp.where(lp[1:2, :] == srow, 1.0, 0.0)).astype(_BF)
    ring[b] = jnp.dot(perm, h_ref[...].astype(_BF), preferred_element_type=_F32)
    copies(i, b, False)

    @pl.when(i == last)
    def _():
        copies(i - 1, 1 - b, True)
        copies(i, b, True)
        for e in range(N_EXPERTS):
            p0 = pad_start[e]

            def fill(off, size, p0=p0):
                cp = pltpu.make_async_copy(ring.at[b, pl.ds(0, size)],
                                           xs_hbm.at[pl.ds(pl.multiple_of(p0 + off, SEG_ALIGN), size)], sem)
                cp.start()
                cp.wait()
            _for_pow2_pieces(pad_cnt[e], SUB // 2, fill)

        def tail(s):
            r = pl.multiple_of(s * SUB, SUB)
            return pltpu.make_async_copy(ring.at[b, pl.ds(0, SUB)], xs_hbm.at[pl.ds(r, SUB)], sem)
        lax.fori_loop(used_sub[0], N_SLOT_SUB, lambda s, c: (tail(s).start(), c)[1], 0)
        lax.fori_loop(used_sub[0], N_SLOT_SUB, lambda s, c: (tail(s).wait(), c)[1], 0)


def _dispatch(h, lpos, seg, pads, used_sub):
    return pl.pallas_call(
        _dispatch_kernel,
        grid_spec=pltpu.PrefetchScalarGridSpec(
            num_scalar_prefetch=6,
            grid=(N_TILES,),
            in_specs=[pl.BlockSpec((DISP_TILE, D_MODEL), lambda i, *_: (i, 0)),
                      pl.BlockSpec((None, TOP_K, DISP_TILE), lambda i, *_: (i, 0, 0))],
            out_specs=pl.BlockSpec(memory_space=pl.ANY),
            scratch_shapes=[pltpu.VMEM((2, SORT_ROWS, D_MODEL), _F32),
                            pltpu.SemaphoreType.DMA((2,)), pltpu.SemaphoreType.DMA(())],
        ),
        out_shape=jax.ShapeDtypeStruct((N_SLOT, D_MODEL), _F32),
        compiler_params=_params(("arbitrary",)),
        name="dispatch",
    )(*seg, *pads, used_sub, h, lpos)


def _residual_out(x, gate, f, g_ref, o_refs, i, tile):
    out = x + gate * f
    if g_ref is not None:
        ms = jnp.mean(out * out, axis=-1, keepdims=True)
        out = out * lax.rsqrt(ms + EPS) * g_ref[...]
    if len(o_refs) == 1:
        o_refs[0][...] = out
    else:
        @pl.when(i < T_PROMPT // tile)
        def _():
            o_refs[0][...] = out

        @pl.when(i >= T_PROMPT // tile)
        def _():
            o_refs[1][...] = out


def _out_specs(final, tile):
    if not final:
        return ([pl.BlockSpec((tile, D_MODEL), lambda i, *_: (i, 0))],
                [jax.ShapeDtypeStruct((T, D_MODEL), _F32)])
    return (_half_specs(D_MODEL, tile),
            [jax.ShapeDtypeStruct((T_PROMPT, D_MODEL), _F32), jax.ShapeDtypeStruct((T_SAMPLE, D_MODEL), _F32)])


def _combine_kernel(final, x_ref, mod_ref, y_ref, *rest):
    g_ref, o_refs = (rest[0], rest[1:]) if final else (None, rest)
    _residual_out(x_ref[...], mod_ref[5:6, :], y_ref[...], g_ref, o_refs, pl.program_id(0), ROW_TILE)


def _combine(x, l, mods, y, g_final=None):
    final = g_final is not None
    row = lambda i: (i, 0)
    in_specs = [
        pl.BlockSpec((ROW_TILE, D_MODEL), row),
        _layer_specs(l, ())[0],
        pl.BlockSpec((ROW_TILE, D_MODEL), row),
    ]
    args = [x, mods, y]
    if final:
        in_specs.append(pl.BlockSpec((1, D_MODEL), lambda i: (0, 0)))
        args.append(g_final.reshape(1, D_MODEL))
    out_specs, out_shape = _out_specs(final, ROW_TILE)
    return pl.pallas_call(
        functools.partial(_combine_kernel, final),
        grid=(T // ROW_TILE,),
        in_specs=in_specs,
        out_specs=out_specs,
        out_shape=out_shape,
        compiler_params=_params(("arbitrary",)),
        name="combine%s" % ("_final" if final else ""),
    )(*args)


def _combine_top2_kernel(final, n_out, n_te, src, dst, x_ref, mod_ref, w_ref, lpos_ref, *rest):
    g_ref, rest = (rest[0], rest[1:]) if final else (None, rest)
    ys_hbm, o_refs, (ybuf, sems) = rest[0], rest[1:1 + n_out], rest[1 + n_out:]
    i = pl.program_id(0)

    def fetch(tile, b, wait=False):
        _segment_copies(tile, (n_te, src, dst), lambda s, d, size: pltpu.make_async_copy(
            ys_hbm.at[pl.ds(d, size)], ybuf.at[b, pl.ds(s, size)], sems.at[b]), wait)

    @pl.when(i == 0)
    def _():
        for slot_id in range(2):
            ybuf[slot_id, PAIRS:SORT_ROWS, :] = jnp.zeros((SORT_ROWS - PAIRS, D_MODEL), _F32)
        fetch(0, 0)

    @pl.when(i + 1 < pl.num_programs(0))
    def _():
        fetch(i + 1, (i + 1) % 2)

    b = i % 2
    fetch(i, b, wait=True)
    y = ybuf[b].astype(_BF)
    lp = lpos_ref[...]
    scol = lax.broadcasted_iota(jnp.int32, (DISP_TILE, SORT_ROWS), 1)
    w = w_ref[...]
    pick = jnp.zeros((DISP_TILE, SORT_ROWS), _F32)
    for k in range(TOP_K):
        pick = jnp.where(lp[:, k:k + 1] == scol, w[:, k:k + 1], pick)
    f = jnp.dot(pick.astype(_BF), y, preferred_element_type=_F32)
    _residual_out(x_ref[...], mod_ref[5:6, :], f, g_ref, o_refs, i, DISP_TILE)


def _combine_top2(x, l, mods, ys, lpos_t, seg, w, g_final=None):
    final = g_final is not None
    row = lambda i, *_: (i, 0)
    in_specs = [
        pl.BlockSpec((DISP_TILE, D_MODEL), row),
        pl.BlockSpec((None, None, 6, D_MODEL), lambda i, *_: (l, _cond_of_tile(i, DISP_TILE), 0, 0)),
        pl.BlockSpec((DISP_TILE, TOP_K), row),
        pl.BlockSpec((None, DISP_TILE, TOP_K), lambda i, *_: (i, 0, 0)),
    ]
    args = [x, mods, w, lpos_t]
    if final:
        in_specs.append(pl.BlockSpec((1, D_MODEL), lambda i, *_: (0, 0)))
        args.append(g_final.reshape(1, D_MODEL))
    in_specs.append(pl.BlockSpec(memory_space=pl.ANY))
    args.append(ys)
    out_specs, out_shape = _out_specs(final, DISP_TILE)
    return pl.pallas_call(
        functools.partial(_combine_top2_kernel, final, len(out_specs)),
        grid_spec=pltpu.PrefetchScalarGridSpec(
            num_scalar_prefetch=3,
            grid=(N_TILES,),
            in_specs=in_specs,
            out_specs=out_specs,
            scratch_shapes=[
                pltpu.VMEM((2, SORT_ROWS, D_MODEL), _F32),
                pltpu.SemaphoreType.DMA((2,)),
            ],
        ),
        out_shape=out_shape,
        compiler_params=_params(("arbitrary",)),
        name="combine_top2%s" % ("_final" if final else ""),
    )(*seg, *args)


def kernel(x_prompt, x_sample, cache_k, cache_v, c, c_ctx, w_ada, b_ada, g_norm_mix, g_norm_ffn,
           w_in, w_out, attn_sink, conv_dw, conv_dw_b, conv_ln_g, conv_ln_b, conv_pw, fnet_w,
           ffn_w_gate, ffn_w_up, ffn_w_down, moe_router, moe_w_gate, moe_w_up, moe_w_down, g_final):
    xs = (x_prompt.reshape(T_PROMPT, D_MODEL), x_sample.reshape(T_SAMPLE, D_MODEL))
    cond8 = jnp.concatenate([c_ctx[None, :], c, jnp.zeros((N_COND - 1 - DEC_BATCH, D_MODEL), _F32)], axis=0)
    mods = _modulation(cond8, w_ada, b_ada).reshape(DEPTH, N_COND, 6, D_MODEL)
    cos_t, sin_t = _rope_tables()
    ck_all = cache_k.reshape(DEC_BATCH, DEPTH, PAST_LEN, KV_W)
    cv_all = cache_v.reshape(DEC_BATCH, DEPTH, PAST_LEN, KV_W)
    p_blocks = T_PROMPT // DEC_SEQ
    sinks = attn_sink.reshape(DEPTH * N_HEADS)

    ks, vs = [], []
    resid = None
    for l in range(DEPTH):
        q, k, v, uc, uf, *x_new = _inproj(xs, l, mods, g_norm_mix, w_in, cos_t, sin_t, resid=resid)
        if resid is not None:
            xs, resid = tuple(x_new), None
        ks.append(k[:T_PROMPT])
        vs.append(v[:T_PROMPT])
        attn = (_ctx_attention(l, sinks, q, k, v),
                _lat_attention(l, sinks, q, k, v, ck_all, cv_all))
        cargs = (l, conv_dw, conv_dw_b, conv_ln_g, conv_ln_b, conv_pw)
        conv = (_conv_module(uc, SEQ, BATCH, 0, *cargs),
                _conv_module(uc, DEC_SEQ, DEC_BATCH, p_blocks, *cargs))
        four = (_fourier_mix(uf, SEQ, BATCH, 0, l, fnet_w),
                _fourier_mix(uf, DEC_SEQ, DEC_BATCH, p_blocks, l, fnet_w))
        last = g_final if l == DEPTH - 1 else None
        i = l // 2
        if l % 2 == 0:
            x, h = _outproj((attn, conv, four), xs, l, mods, w_out, g_norm_ffn)
            y = _grouped_ffn(h, ffn_w_gate[i:i + 1], ffn_w_up[i:i + 1], ffn_w_down[i:i + 1], 256,
                             *_dense_visits())
            if last is None:
                xs, resid = (x,), (y, l)
            else:
                xs = tuple(_combine(x, l, mods, y, g_final=last))
        else:
            x, h, eidx, ew = _outproj((attn, conv, four), xs, l, mods, w_out, g_norm_ffn,
                                      router_t=moe_router[i].T)
            lpos, seg, pads, visits = _routing_plan(eidx)
            xd = _dispatch(h, lpos, seg, pads, visits[-1])
            ys = _grouped_ffn(xd, moe_w_gate[i], moe_w_up[i], moe_w_down[i], 512, *visits)
            xs = tuple(_combine_top2(x, l, mods, ys, lpos.transpose(0, 2, 1), seg, ew.T, g_final=last))

    y_prompt = xs[0].reshape(BATCH, SEQ, D_MODEL)
    y_sample = xs[1].reshape(DEC_BATCH, DEC_SEQ, D_MODEL)
    state_k = jnp.stack([a.reshape(BATCH, SEQ, N_KV_HEADS, HEAD_DIM) for a in ks], axis=1)
    state_v = jnp.stack([a.reshape(BATCH, SEQ, N_KV_HEADS, HEAD_DIM) for a in vs], axis=1)
    return (y_prompt, y_sample, state_k, state_v)
```

```python
import functools

import numpy as np
import jax
import jax.numpy as jnp
from jax import lax
from jax.experimental import pallas as pl
from jax.experimental.pallas import tpu as pltpu

D_MODEL = 1024
BATCH = 16
SEQ = 256
DEPTH = 2
DEC_BATCH = 2
DEC_SEQ = 2048
PAST_LEN = 512
GRID_W = 64
HEAD_DIM = 64
N_HEADS = 8
N_KV_HEADS = 2
GQA_GROUP = N_HEADS // N_KV_HEADS
ATTN_W = N_HEADS * HEAD_DIM
KV_W = N_KV_HEADS * HEAD_DIM
WINDOW = 128
BLOCK = 128
ROPE_THETA = 10000.0
CONV_W = D_MODEL // 4
CONV_K = 31
FNET_GROUPS = 4
FNET_W = D_MODEL // 4
FNET_GW = FNET_W // FNET_GROUPS
IN_WIDTH = ATTN_W + 2 * KV_W + 2 * CONV_W + FNET_W
D_FF = 2816
N_EXPERTS = 8
TOP_K = 2
D_FF_EXPERT = 3584
EPS = 1e-6
NEG = -1e30

T_PROMPT = BATCH * SEQ
T_SAMPLE = DEC_BATCH * DEC_SEQ
T = T_PROMPT + T_SAMPLE
N_COND = 8

ROW_TILE = 512
SUB = 128
SUB_MAX = 32
TMAX = SUB * SUB_MAX
BLK_MIN = 8
BLK_SUBS = 10
DISP_TILE = 512
N_TILES = T // DISP_TILE
PAIRS = TOP_K * DISP_TILE
SEG_ALIGN = 8
SORT_ROWS = 1152
CONV_CHUNK = 64
CONV_PAD = 16
CONV_SPAN = CONV_CHUNK + 8 * ((CONV_PAD + CONV_K // 2) // 8)
VMEM_LIMIT = 48 * 1024 * 1024
FFN_VMEM_LIMIT = 56 * 1024 * 1024

_BF = jnp.bfloat16
_F32 = jnp.float32


def _cond_of_tile(i, tile):
    r = i * tile
    return jnp.where(r < T_PROMPT, 0, 1 + (r - T_PROMPT) // DEC_SEQ)


def _params(sem, vmem=VMEM_LIMIT):
    return pltpu.CompilerParams(dimension_semantics=sem, vmem_limit_bytes=vmem)


def _sigmoid(x):
    return 1.0 / (1.0 + jnp.exp(-x))


def _mod_kernel(cond_ref, w_ref, b_ref, o_ref):
    cnd = cond_ref[...]
    s = (cnd * _sigmoid(cnd)).astype(_BF)
    o_ref[...] = jnp.dot(s, w_ref[...].astype(_BF), preferred_element_type=_F32) + b_ref[...]


def _modulation(cond8, w_ada, b_ada):
    nt = 1536
    return pl.pallas_call(
        _mod_kernel,
        grid=(DEPTH, 6 * D_MODEL // nt),
        in_specs=[
            pl.BlockSpec((N_COND, D_MODEL), lambda l, n: (0, 0)),
            pl.BlockSpec((None, D_MODEL, nt), lambda l, n: (l, 0, n)),
            pl.BlockSpec((None, 1, nt), lambda l, n: (l, 0, n)),
        ],
        out_specs=pl.BlockSpec((None, N_COND, nt), lambda l, n: (l, 0, n)),
        out_shape=jax.ShapeDtypeStruct((DEPTH, N_COND, 6 * D_MODEL), _F32),
        compiler_params=_params(("arbitrary", "arbitrary")),
        name="modulation",
    )(cond8, w_ada, b_ada.reshape(DEPTH, 1, 6 * D_MODEL))


def _norm_mod(x, g, shift, scale):
    ms = jnp.mean(x * x, axis=-1, keepdims=True)
    y = x * lax.rsqrt(ms + EPS) * g
    return y * (1.0 + scale) + shift


def _rope_tables():
    rows = DEC_SEQ // GRID_W
    n_freq = HEAD_DIM // 4
    inv = ROPE_THETA ** (-jnp.arange(n_freq, dtype=_F32) / n_freq)
    gr, gc = jnp.meshgrid(jnp.arange(rows, dtype=_F32), jnp.arange(GRID_W, dtype=_F32), indexing="ij")
    ang_r = gr.reshape(-1)[:, None] * inv
    ang_c = gc.reshape(-1)[:, None] * inv
    cr, sr, cc, sc = jnp.cos(ang_r), jnp.sin(ang_r), jnp.cos(ang_c), jnp.sin(ang_c)
    cos64 = jnp.concatenate([cr, cr, cc, cc], axis=-1)
    sin64 = jnp.concatenate([-sr, sr, -sc, sc], axis=-1)
    return jnp.tile(cos64, (1, 2)), jnp.tile(sin64, (1, 2))


def _rope128(x, cos, sin):
    lane = lax.broadcasted_iota(jnp.int32, x.shape, 1)
    first = (lane % 32) < 16
    partner = jnp.where(first, pltpu.roll(x, 128 - 16, 1), pltpu.roll(x, 16, 1))
    return x * cos + partner * sin


def _half_specs(width, tile):
    pt = T_PROMPT // tile
    return [pl.BlockSpec((tile, width), lambda i, *_: (jnp.minimum(i, pt - 1), 0)),
            pl.BlockSpec((tile, width), lambda i, *_: (jnp.maximum(i - pt, 0), 0))]


def _pick_half(i, tile, a_ref, b_ref):
    return jnp.where(i < T_PROMPT // tile, a_ref[...], b_ref[...])


def _layer_specs(l, w_shape):
    return [
        pl.BlockSpec((None, None, 6, D_MODEL), lambda i, *_: (l, _cond_of_tile(i, ROW_TILE), 0, 0)),
        pl.BlockSpec((None, 1, D_MODEL), lambda i, *_: (l, 0, 0)),
        pl.BlockSpec((None,) + w_shape, lambda i, *_: (l, 0, 0)),
    ]


def _inproj_kernel(split, resid, *refs):
    n_x = 2 if split else 1
    x_refs, refs = refs[:n_x], refs[n_x:]
    if resid:
        y_ref, modp_ref = refs[:2]
        refs = refs[2:]
    mod_ref, g_ref, w_ref, cos_ref, sin_ref, q_ref, k_ref, v_ref, uc_ref, uf_ref = refs[:10]
    wb_ref, uf_scr = refs[-2:]
    i = pl.program_id(0)

    @pl.when(i == 0)
    def _():
        wb_ref[...] = w_ref[...].astype(_BF)

    x = _pick_half(i, ROW_TILE, *x_refs) if split else x_refs[0][...]
    if resid:
        x = x + modp_ref[5:6, :] * y_ref[...]
        refs[10][...] = x
    h = _norm_mod(x, g_ref[...], mod_ref[0:1, :], mod_ref[1:2, :]).astype(_BF)
    latent = i >= T_PROMPT // ROW_TILE
    cos = jnp.where(latent, cos_ref[...], 1.0)
    sin = jnp.where(latent, sin_ref[...], 0.0)

    def proj(o, n):
        return jnp.dot(h, wb_ref[:, o:o + n], preferred_element_type=_F32)

    o = 0
    for c in range(ATTN_W // 256):
        qq = proj(o, 256)
        q_ref[:, o:o + 128] = _rope128(qq[:, :128], cos, sin).astype(q_ref.dtype)
        q_ref[:, o + 128:o + 256] = _rope128(qq[:, 128:], cos, sin).astype(q_ref.dtype)
        o += 256
    kv = proj(o, 2 * KV_W)
    k_ref[...] = _rope128(kv[:, :KV_W], cos, sin)
    v_ref[...] = kv[:, KV_W:]
    o += 2 * KV_W
    uc_ref[...] = proj(o, 2 * CONV_W)
    o += 2 * CONV_W
    uf = proj(o, FNET_W)
    n_lane = FNET_W // 128
    for c in range(n_lane):
        uf_scr[c] = uf[:, c * 128:(c + 1) * 128]
    half = ROW_TILE // 2
    uf_ref[...] = jnp.concatenate([uf_scr[c, pl.ds(par, half, stride=2), :]
                                   for par in range(2) for c in range(n_lane)], axis=-1).astype(uf_ref.dtype)


def _inproj(xs, l, mods, g_all, w_all, cos_t, sin_t, resid=None):
    split = len(xs) == 2
    pt = T_PROMPT // ROW_TILE
    per_seq = DEC_SEQ // ROW_TILE

    def rope_idx(i):
        return (jnp.maximum(i - pt, 0) % per_seq, 0)

    row = lambda i: (i, 0)
    in_specs = _half_specs(D_MODEL, ROW_TILE) if split else [pl.BlockSpec((ROW_TILE, D_MODEL), row)]
    args = list(xs)
    outs = [ATTN_W, KV_W, KV_W, 2 * CONV_W, FNET_W]
    dtypes = [_BF, _F32, _F32, _F32, _BF]
    fold = [1, 1, 1, 1, 2]
    if resid is not None:
        y, l_prev = resid
        in_specs += [pl.BlockSpec((ROW_TILE, D_MODEL), row), _layer_specs(l_prev, ())[0]]
        args += [y, mods]
        outs.append(D_MODEL)
        dtypes.append(_F32)
        fold.append(1)
    return pl.pallas_call(
        functools.partial(_inproj_kernel, split, resid is not None),
        grid=(T // ROW_TILE,),
        in_specs=in_specs + _layer_specs(l, (D_MODEL, IN_WIDTH)) + [
            pl.BlockSpec((ROW_TILE, 128), rope_idx),
            pl.BlockSpec((ROW_TILE, 128), rope_idx),
        ],
        out_specs=[pl.BlockSpec((ROW_TILE // f, n * f), row) for n, f in zip(outs, fold)],
        out_shape=[jax.ShapeDtypeStruct((T // f, n * f), dt) for n, dt, f in zip(outs, dtypes, fold)],
        scratch_shapes=[pltpu.VMEM((D_MODEL, IN_WIDTH), _BF), pltpu.VMEM((FNET_W // 128, ROW_TILE, 128), _F32)],
        compiler_params=_params(("arbitrary",)),
        name="inproj_resid" if resid is not None else "inproj",
    )(*args, mods, g_all.reshape(DEPTH, 1, D_MODEL), w_all, cos_t, sin_t)


def _stack_groups(q, kh, rows):
    parts = [q[:, (kh * GQA_GROUP + g) * HEAD_DIM:(kh * GQA_GROUP + g + 1) * HEAD_DIM] for g in range(GQA_GROUP)]
    return jnp.concatenate(parts, axis=0)


def _sink_column(sink_ref, head0, rows):
    r = lax.broadcasted_iota(jnp.int32, (GQA_GROUP * rows, 1), 0)
    col = jnp.full((GQA_GROUP * rows, 1), sink_ref[head0], _F32)
    for g in range(1, GQA_GROUP):
        col = jnp.where(r >= g * rows, sink_ref[head0 + g], col)
    return col


def _unstack_store(o_ref, o, kh, rows):
    for pair in range(GQA_GROUP // 2):
        a = o[(2 * pair) * rows:(2 * pair + 1) * rows]
        b = o[(2 * pair + 1) * rows:(2 * pair + 2) * rows]
        c0 = (kh * GQA_GROUP + 2 * pair) * HEAD_DIM
        o_ref[:, c0:c0 + 2 * HEAD_DIM] = jnp.concatenate([a, b], axis=-1).astype(o_ref.dtype)


_NT = (((1,), (1,)), ((), ()))


def _ctx_attn_kernel(l, sink_ref, q_ref, k_ref, v_ref, o_ref):
    scale = HEAD_DIM ** -0.5
    q = q_ref[...] * scale
    for kh in range(N_KV_HEADS):
        kk = k_ref[:, kh * HEAD_DIM:(kh + 1) * HEAD_DIM].astype(_BF)
        vv = v_ref[:, kh * HEAD_DIM:(kh + 1) * HEAD_DIM].astype(_BF)
        qs = _stack_groups(q, kh, SEQ).astype(_BF)
        s = lax.dot_general(qs, kk, _NT, preferred_element_type=_F32)
        sink = _sink_column(sink_ref, l * N_HEADS + kh * GQA_GROUP, SEQ)
        m = jnp.maximum(jnp.max(s, axis=-1, keepdims=True), sink)
        p = jnp.exp(s - m)
        den = jnp.sum(p, axis=-1, keepdims=True) + jnp.exp(sink - m)
        o = jnp.dot(p.astype(_BF), vv, preferred_element_type=_F32) / den
        _unstack_store(o_ref, o, kh, SEQ)


def _ctx_attention(l, sinks, q, k, v):
    row = lambda b, s: (b, 0)
    return pl.pallas_call(
        functools.partial(_ctx_attn_kernel, l),
        grid_spec=pltpu.PrefetchScalarGridSpec(
            num_scalar_prefetch=1,
            grid=(BATCH,),
            in_specs=[
                pl.BlockSpec((SEQ, ATTN_W), row),
                pl.BlockSpec((SEQ, KV_W), row),
                pl.BlockSpec((SEQ, KV_W), row),
            ],
            out_specs=pl.BlockSpec((SEQ, ATTN_W), row),
        ),
        out_shape=jax.ShapeDtypeStruct((T_PROMPT, ATTN_W), _BF),
        compiler_params=_params(("arbitrary",)),
        name="ctx_attention",
    )(sinks, q, k, v)


def _lat_attn_kernel(l, sink_ref, q_ref, k_ref, v_ref, ck_ref, cv_ref, o_ref):
    n = pl.program_id(1)
    scale = HEAD_DIM ** -0.5
    nb = DEC_SEQ // BLOCK
    band = 3 * BLOCK
    start = pl.multiple_of(jnp.clip(n - 1, 0, nb - 3) * BLOCK, BLOCK)
    q = q_ref[...] * scale
    kb = k_ref[pl.ds(start, band), :]
    vb = v_ref[pl.ds(start, band), :]
    dpos = (lax.broadcasted_iota(jnp.int32, (BLOCK, band), 1) - lax.broadcasted_iota(jnp.int32, (BLOCK, band), 0)
            + (start - n * BLOCK))
    in_window = jnp.where(jnp.abs(dpos) <= WINDOW, 0.0, NEG)
    bias = jnp.concatenate([in_window] * GQA_GROUP, axis=0)
    for kh in range(N_KV_HEADS):
        hs = slice(kh * HEAD_DIM, (kh + 1) * HEAD_DIM)
        qs = _stack_groups(q, kh, BLOCK).astype(_BF)
        s_loc = lax.dot_general(qs, kb[:, hs].astype(_BF), _NT, preferred_element_type=_F32)
        s_loc = jnp.where(bias < 0.0, NEG, s_loc)
        s_ctx = lax.dot_general(qs, ck_ref[:, hs].astype(_BF), _NT, preferred_element_type=_F32)
        sink = _sink_column(sink_ref, l * N_HEADS + kh * GQA_GROUP, BLOCK)
        m = jnp.maximum(jnp.maximum(jnp.max(s_loc, axis=-1, keepdims=True),
                                    jnp.max(s_ctx, axis=-1, keepdims=True)), sink)
        p_loc = jnp.exp(s_loc - m)
        p_ctx = jnp.exp(s_ctx - m)
        den = (jnp.sum(p_loc, axis=-1, keepdims=True) + jnp.sum(p_ctx, axis=-1, keepdims=True)
               + jnp.exp(sink - m))
        o = (jnp.dot(p_loc.astype(_BF), vb[:, hs].astype(_BF), preferred_element_type=_F32)
             + jnp.dot(p_ctx.astype(_BF), cv_ref[:, hs].astype(_BF), preferred_element_type=_F32)) / den
        _unstack_store(o_ref, o, kh, BLOCK)


def _lat_attention(l, sinks, q, k, v, ck, cv):
    nb = DEC_SEQ // BLOCK
    q0 = T_PROMPT // BLOCK
    s0 = T_PROMPT // DEC_SEQ
    return pl.pallas_call(
        functools.partial(_lat_attn_kernel, l),
        grid_spec=pltpu.PrefetchScalarGridSpec(
            num_scalar_prefetch=1,
            grid=(DEC_BATCH, nb),
            in_specs=[
                pl.BlockSpec((BLOCK, ATTN_W), lambda b, n, s: (q0 + b * nb + n, 0)),
                pl.BlockSpec((DEC_SEQ, KV_W), lambda b, n, s: (s0 + b, 0)),
                pl.BlockSpec((DEC_SEQ, KV_W), lambda b, n, s: (s0 + b, 0)),
                pl.BlockSpec((None, None, PAST_LEN, KV_W), lambda b, n, s: (b, l, 0, 0)),
                pl.BlockSpec((None, None, PAST_LEN, KV_W), lambda b, n, s: (b, l, 0, 0)),
            ],
            out_specs=pl.BlockSpec((BLOCK, ATTN_W), lambda b, n, s: (b * nb + n, 0)),
        ),
        out_shape=jax.ShapeDtypeStruct((T_SAMPLE, ATTN_W), _BF),
        compiler_params=_params(("arbitrary", "arbitrary")),
        name="lat_attention",
    )(sinks, q, k, v, ck, cv)


def _conv_kernel(seq, u_ref, dw_ref, dwb_ref, lg_ref, lb_ref, pw_ref, o_ref, pad_ref, y_ref, sh_ref):
    u = u_ref[...]
    pad_ref[0:CONV_PAD, :] = jnp.zeros((CONV_PAD, CONV_W), _F32)
    pad_ref[CONV_PAD + seq:2 * CONV_PAD + seq, :] = jnp.zeros((CONV_PAD, CONV_W), _F32)
    pad_ref[CONV_PAD:CONV_PAD + seq, :] = u[:, :CONV_W] * _sigmoid(u[:, CONV_W:])
    off = CONV_PAD - CONV_K // 2
    span = CONV_SPAN

    def chunk(c, carry):
        row = pl.multiple_of(c * CONV_CHUNK, CONV_CHUNK)
        win = pad_ref[pl.ds(row, CONV_CHUNK + 2 * CONV_PAD), :]
        acc = jnp.zeros((CONV_CHUNK, CONV_W), _F32) + dwb_ref[...]
        for phase in range(8):
            sh_ref[phase] = win[phase:phase + span, :]
        for phase in range(8):
            for a in range(span // 8):
                t = 8 * a + phase - off
                if 0 <= t < CONV_K:
                    acc = acc + sh_ref[phase, 8 * a:8 * a + CONV_CHUNK, :] * dw_ref[t:t + 1, :]
        mu = jnp.mean(acc, axis=-1, keepdims=True)
        d = acc - mu
        var = jnp.mean(d * d, axis=-1, keepdims=True)
        y = d * lax.rsqrt(var + EPS) * lg_ref[...] + lb_ref[...]
        y_ref[pl.ds(row, CONV_CHUNK), :] = (y * _sigmoid(y)).astype(_BF)
        return carry

    lax.fori_loop(0, seq // CONV_CHUNK, chunk, 0)
    o_ref[...] = jnp.dot(y_ref[...], pw_ref[...].astype(_BF), preferred_element_type=_F32).astype(o_ref.dtype)


def _conv_module(uc, seq, nbatch, block0, l, dw, dwb, lg, lb, pw):
    vec = lambda a: a.reshape(DEPTH, 1, CONV_W)
    layer = lambda b: (l, 0, 0)
    return pl.pallas_call(
        functools.partial(_conv_kernel, seq),
        grid=(nbatch,),
        in_specs=[
            pl.BlockSpec((seq, 2 * CONV_W), lambda b: (block0 + b, 0)),
            pl.BlockSpec((None, CONV_K, CONV_W), layer),
            pl.BlockSpec((None, 1, CONV_W), layer),
            pl.BlockSpec((None, 1, CONV_W), layer),
            pl.BlockSpec((None, 1, CONV_W), layer),
            pl.BlockSpec((None, CONV_W, CONV_W), layer),
        ],
        out_specs=pl.BlockSpec((seq, CONV_W), lambda b: (b, 0)),
        out_shape=jax.ShapeDtypeStruct((nbatch * seq, CONV_W), _BF),
        scratch_shapes=[pltpu.VMEM((seq + 2 * CONV_PAD, CONV_W), _F32), pltpu.VMEM((seq, CONV_W), _BF),
                        pltpu.VMEM((8, CONV_SPAN, CONV_W), _F32)],
        compiler_params=_params(("arbitrary",)),
        name="conv_module_%d" % seq,
    )(uc, dw, vec(dwb), vec(lg), vec(lb), pw)


def _dft_constants(seq):
    half = seq // 2
    j = np.arange(half, dtype=np.int64)
    ang = 2.0 * np.pi * ((j[:, None] * j[None, :]) % half).astype(np.float64) / half
    m = np.concatenate([np.cos(ang), -np.sin(ang)], axis=1) / np.sqrt(seq)
    tw = np.pi * j.astype(np.float64) / half
    cb = np.repeat(np.cos(tw)[:, None], FNET_W, axis=1)
    sb = np.repeat(np.sin(tw)[:, None], FNET_W, axis=1)
    c = np.arange(FNET_GW, dtype=np.int64)
    angc = 2.0 * np.pi * ((c[:, None] * c[None, :]) % FNET_GW).astype(np.float64) / FNET_GW
    eye = np.eye(FNET_GROUPS)
    w1 = np.concatenate([np.kron(eye, np.cos(angc)), np.kron(eye, np.sin(angc))], axis=1) / np.sqrt(FNET_GW)
    return tuple(jnp.asarray(a, dtype=_F32) for a in (m, cb, sb, w1))


def _fnet_kernel(half, group, u_ref, m_ref, cb_ref, sb_ref, w1_ref, fw_ref, o_ref, rhs_ref):
    w1 = w1_ref[...].astype(_BF)
    te = jnp.dot(u_ref[:, :FNET_W], w1, preferred_element_type=_F32)
    to = jnp.dot(u_ref[:, FNET_W:], w1, preferred_element_type=_F32)
    for g in range(group):
        rows = slice(g * half, (g + 1) * half)
        ec, es = te[rows, :FNET_W], te[rows, FNET_W:]
        oc, os_ = to[rows, :FNET_W], to[rows, FNET_W:]
        for col, (top, bot) in enumerate(((ec, es), (oc, os_), (os_, -oc))):
            lanes = slice((3 * g + col) * FNET_W, (3 * g + col + 1) * FNET_W)
            rhs_ref[0:half, lanes] = top.astype(_BF)
            rhs_ref[half:2 * half, lanes] = bot.astype(_BF)
    pqr = jnp.dot(m_ref[...].astype(_BF), rhs_ref[...], preferred_element_type=_F32)
    cb, sb = cb_ref[...], sb_ref[...]
    ys = []
    for g in range(group):
        p, q, r = (pqr[:, (3 * g + col) * FNET_W:(3 * g + col + 1) * FNET_W] for col in range(3))
        rot = cb * q - sb * r
        ys += [(p + rot).astype(_BF), (p - rot).astype(_BF)]
    o_ref[...] = jnp.dot(jnp.concatenate(ys, axis=0), fw_ref[...].astype(_BF),
                         preferred_element_type=_F32).astype(o_ref.dtype)


def _fourier_mix(u_pairs, seq, nbatch, row0, l, fw):
    half = seq // 2
    group = max(1, 512 // half)
    rows = group * half
    m, cb, sb, w1 = _dft_constants(seq)
    const = lambda b: (0, 0)
    return pl.pallas_call(
        functools.partial(_fnet_kernel, half, group),
        grid=(nbatch // group,),
        in_specs=[
            pl.BlockSpec((rows, 2 * FNET_W), lambda b: (row0 // rows + b, 0)),
            pl.BlockSpec((half, 2 * half), const),
            pl.BlockSpec((half, FNET_W), const),
            pl.BlockSpec((half, FNET_W), const),
            pl.BlockSpec((FNET_W, 2 * FNET_W), const),
            pl.BlockSpec((None, FNET_W, FNET_W), lambda b: (l, 0, 0)),
        ],
        out_specs=pl.BlockSpec((2 * rows, FNET_W), lambda b: (b, 0)),
        out_shape=jax.ShapeDtypeStruct((nbatch * seq, FNET_W), _BF),
        scratch_shapes=[pltpu.VMEM((2 * half, 3 * group * FNET_W), _BF)],
        compiler_params=_params(("arbitrary",)),
        name="fourier_mix_%d" % seq,
    )(u_pairs, m, cb, sb, w1, fw)


def _route_top2(r_ref, h, ei_ref, ew_ref):
    lg = lax.dot_general(r_ref[...].astype(_BF), h.astype(_BF), _NT, preferred_element_type=_F32)
    eid = lax.broadcasted_iota(jnp.int32, lg.shape, 0)
    m1 = jnp.max(lg, axis=0, keepdims=True)
    i1 = jnp.min(jnp.where(lg == m1, eid, N_EXPERTS), axis=0, keepdims=True)
    lg2 = jnp.where(eid == i1, -jnp.inf, lg)
    m2 = jnp.max(lg2, axis=0, keepdims=True)
    i2 = jnp.min(jnp.where(lg2 == m2, eid, N_EXPERTS), axis=0, keepdims=True)
    e = jnp.exp(m2 - m1)
    ei_ref[0:1, :] = i1
    ei_ref[1:2, :] = i2
    ew_ref[0:1, :] = 1.0 / (1.0 + e)
    ew_ref[1:2, :] = e / (1.0 + e)


def _outproj_kernel(split, route, *refs):
    branch_refs, refs = refs[:6], refs[6:]
    n_x = 2 if split else 1
    x_refs, refs = refs[:n_x], refs[n_x:]
    mod_ref, w_ref, gf_ref = refs[:3]
    refs = refs[3:]
    if route:
        r_ref, xo_ref, h_ref, ei_ref, ew_ref, wb_ref = refs
    else:
        xo_ref, h_ref, wb_ref = refs
    i = pl.program_id(0)

    @pl.when(i == 0)
    def _():
        wb_ref[...] = w_ref[...].astype(_BF)

    mix = jnp.concatenate([_pick_half(i, ROW_TILE, branch_refs[2 * n], branch_refs[2 * n + 1])
                           for n in range(3)], axis=-1)
    x = _pick_half(i, ROW_TILE, *x_refs) if split else x_refs[0][...]
    x_new = x + mod_ref[2:3, :] * jnp.dot(mix, wb_ref[...], preferred_element_type=_F32)
    xo_ref[...] = x_new
    h = _norm_mod(x_new, gf_ref[...], mod_ref[3:4, :], mod_ref[4:5, :])
    h_ref[...] = h.astype(h_ref.dtype)
    if route:
        _route_top2(r_ref, h, ei_ref, ew_ref)


def _outproj(branches, xs, l, mods, w_all, gf_all, router_t=None):
    split = len(xs) == 2
    route = router_t is not None
    row = lambda i: (i, 0)
    in_specs = []
    args = []
    for pair, width in zip(branches, (ATTN_W, CONV_W, FNET_W)):
        in_specs += _half_specs(width, ROW_TILE)
        args += list(pair)
    in_specs += _half_specs(D_MODEL, ROW_TILE) if split else [pl.BlockSpec((ROW_TILE, D_MODEL), row)]
    args += list(xs)
    mod_spec, gf_spec, w_spec = _layer_specs(l, (D_MODEL, D_MODEL))
    in_specs += [mod_spec, w_spec, gf_spec]
    args += [mods, w_all, gf_all.reshape(DEPTH, 1, D_MODEL)]
    out_specs = [pl.BlockSpec((ROW_TILE, D_MODEL), row)] * 2
    out_shape = [jax.ShapeDtypeStruct((T, D_MODEL), _F32), jax.ShapeDtypeStruct((T, D_MODEL), _BF)]
    if route:
        in_specs.append(pl.BlockSpec((N_EXPERTS, D_MODEL), lambda i: (0, 0)))
        args.append(router_t)
        out_specs += [pl.BlockSpec((TOP_K, ROW_TILE), lambda i: (0, i))] * 2
        out_shape += [jax.ShapeDtypeStruct((TOP_K, T), jnp.int32), jax.ShapeDtypeStruct((TOP_K, T), _F32)]
    return pl.pallas_call(
        functools.partial(_outproj_kernel, split, route),
        grid=(T // ROW_TILE,),
        in_specs=in_specs,
        out_specs=out_specs,
        out_shape=out_shape,
        scratch_shapes=[pltpu.VMEM((D_MODEL, D_MODEL), _BF)],
        compiler_params=_params(("arbitrary",)),
        name="outproj_route" if route else "outproj",
    )(*args)


def _ffn_kernel(nj, n_sub_rows, vis_e, vis_start, vis_cnt, used_sub,
                x_hbm, wg_ref, wu_ref, wd_ref, y_hbm,
                big, xb, wgb, wub, wdb, sem_in, sem_out):
    del vis_e
    v = pl.program_id(0)
    j = pl.program_id(1)
    cnt = vis_cnt[v]
    row0 = vis_start[v] * SUB

    x_is_bf16 = x_hbm.dtype == _BF
    landing = xb if x_is_bf16 else big

    def copy_in(s):
        r = pl.multiple_of(row0 + s * SUB, SUB)
        b = pl.multiple_of(s * SUB, SUB)
        return pltpu.make_async_copy(x_hbm.at[pl.ds(r, SUB)], landing.at[pl.ds(b, SUB)], sem_in.at[s])

    def copy_out(s):
        r = pl.multiple_of(row0 + s * SUB, SUB)
        b = pl.multiple_of(s * SUB, SUB)
        return pltpu.make_async_copy(big.at[pl.ds(b, SUB)], y_hbm.at[pl.ds(r, SUB)], sem_out)

    def for_subs(fn):
        lax.fori_loop(0, cnt, lambda s, c: (fn(s), c)[1], 0)

    @pl.when(jnp.logical_and(j == 0, cnt > 0))
    def _():
        for_subs(lambda s: copy_in(s).start())

    @pl.when(cnt > 0)
    def _():
        wgb[...] = wg_ref[...].astype(_BF)
        wub[...] = wu_ref[...].astype(_BF)
        wdb[...] = wd_ref[...].astype(_BF)

        def block(r0, nrows):
            s0 = r0 // SUB

            @pl.when(j == 0)
            def _():
                for t in range(nrows // SUB):
                    copy_in(s0 + t).wait()
                    rs = pl.ds(pl.multiple_of(r0 + t * SUB, SUB), SUB)
                    if not x_is_bf16:
                        xb[rs, :] = big[rs, :].astype(_BF)
                    big[rs, :] = jnp.zeros((SUB, D_MODEL), _F32)

            rs = pl.ds(r0, nrows)
            x = xb[rs, :]
            a = jnp.dot(x, wgb[...], preferred_element_type=_F32)
            b = jnp.dot(x, wub[...], preferred_element_type=_F32)
            p = (a * _sigmoid(a) * b).astype(_BF)
            big[rs, :] += jnp.dot(p, wdb[...], preferred_element_type=_F32)

            @pl.when(j == nj - 1)
            def _():
                for t in range(nrows // SUB):
                    copy_out(s0 + t).start()

        n_big = cnt // BLK_MIN
        rem = cnt - n_big * BLK_MIN
        room = BLK_SUBS - BLK_MIN
        grow = jnp.minimum(rem, n_big * room)
        left = rem - grow

        def one_block(i, r0):
            size = BLK_MIN + jnp.clip(grow - i * room, 0, room)
            for sz in range(BLK_MIN, BLK_SUBS + 1):
                pl.when(size == sz)(functools.partial(block, pl.multiple_of(r0, SUB), sz * SUB))
            return r0 + size * SUB
        r0 = lax.fori_loop(0, n_big, one_block, 0)
        bit = BLK_MIN // 2
        while bit >= 1:
            take = (left // bit) % 2 == 1
            pl.when(take)(functools.partial(block, pl.multiple_of(r0, SUB), bit * SUB))
            r0 = r0 + jnp.where(take, bit * SUB, 0)
            bit //= 2

    @pl.when(jnp.logical_and(j == nj - 1, cnt > 0))
    def _():
        for_subs(lambda s: copy_out(s).wait())

    if x_hbm.dtype == y_hbm.dtype:
        @pl.when(jnp.logical_and(v == pl.num_programs(0) - 1, j == nj - 1))
        def _():
            def tail(s):
                r = pl.multiple_of(s * SUB, SUB)
                return pltpu.make_async_copy(x_hbm.at[pl.ds(r, SUB)], y_hbm.at[pl.ds(r, SUB)], sem_out)
            lax.fori_loop(used_sub[0], n_sub_rows, lambda s, c: (tail(s).start(), c)[1], 0)
            lax.fori_loop(used_sub[0], n_sub_rows, lambda s, c: (tail(s).wait(), c)[1], 0)


def _grouped_ffn(x, wg, wu, wd, fc, vis_e, vis_start, vis_cnt, used_sub):
    ff = wg.shape[-1]
    nj = ff // fc
    nv = vis_e.shape[0]
    n_rows = x.shape[0]

    def chunk(v, j, vc):
        return jnp.where(vc[v] > 0, j, nj - 1)

    return pl.pallas_call(
        functools.partial(_ffn_kernel, nj, n_rows // SUB),
        grid_spec=pltpu.PrefetchScalarGridSpec(
            num_scalar_prefetch=4,
            grid=(nv, nj),
            in_specs=[
                pl.BlockSpec(memory_space=pl.ANY),
                pl.BlockSpec((None, D_MODEL, fc), lambda v, j, ve, vs, vc, us: (ve[v], 0, chunk(v, j, vc))),
                pl.BlockSpec((None, D_MODEL, fc), lambda v, j, ve, vs, vc, us: (ve[v], 0, chunk(v, j, vc))),
                pl.BlockSpec((None, fc, D_MODEL), lambda v, j, ve, vs, vc, us: (ve[v], chunk(v, j, vc), 0)),
            ],
            out_specs=pl.BlockSpec(memory_space=pl.ANY),
            scratch_shapes=[
                pltpu.VMEM((TMAX, D_MODEL), _F32),
                pltpu.VMEM((TMAX, D_MODEL), _BF),
                pltpu.VMEM((D_MODEL, fc), _BF),
                pltpu.VMEM((D_MODEL, fc), _BF),
                pltpu.VMEM((fc, D_MODEL), _BF),
                pltpu.SemaphoreType.DMA((SUB_MAX,)),
                pltpu.SemaphoreType.DMA(()),
            ],
        ),
        out_shape=jax.ShapeDtypeStruct((n_rows, D_MODEL), _F32),
        compiler_params=_params(("arbitrary", "arbitrary"), FFN_VMEM_LIMIT),
        name="ffn_%d" % ff,
    )(vis_e, vis_start, vis_cnt, used_sub, x, wg, wu, wd)


def _dense_visits():
    nv = T // TMAX
    return (jnp.zeros((nv,), jnp.int32),
            jnp.arange(nv, dtype=jnp.int32) * SUB_MAX,
            jnp.full((nv,), SUB_MAX, jnp.int32),
            jnp.full((1,), T // SUB, jnp.int32))


N_SLOT_SUB = -(-(TOP_K * T + N_TILES * N_EXPERTS * (SEG_ALIGN - 1)) // SUB) + N_EXPERTS
N_SLOT = N_SLOT_SUB * SUB
N_VISIT = -(-N_SLOT_SUB // SUB_MAX) + N_EXPERTS


def _routing_plan(eidx):
    e_loc = eidx.reshape(TOP_K, N_TILES, DISP_TILE).transpose(1, 0, 2).reshape(N_TILES, PAIRS)
    onehot = e_loc[:, :, None] == jnp.arange(N_EXPERTS, dtype=jnp.int32)[None, None, :]
    ch = 128
    oh = onehot.astype(_F32).reshape(N_TILES, PAIRS // ch, ch, N_EXPERTS)
    tri = (jnp.arange(ch)[:, None] >= jnp.arange(ch)[None, :]).astype(_F32)
    within = jnp.einsum("ij,tcjk->tcik", tri, oh)
    tot = within[:, :, -1, :]
    csum = (within + (jnp.cumsum(tot, axis=1) - tot)[:, :, None, :]).reshape(N_TILES, PAIRS, N_EXPERTS)
    csum = csum.astype(jnp.int32)
    n_te = (csum[:, -1, :] + SEG_ALIGN - 1) // SEG_ALIGN * SEG_ALIGN
    src = jnp.cumsum(n_te, axis=1) - n_te
    lpos = jnp.sum(jnp.where(onehot, csum - 1 + src[:, None, :], 0), axis=2).astype(jnp.int32)
    counts = jnp.sum(n_te, axis=0)
    nsub = (counts + SUB - 1) // SUB
    sub_base = jnp.cumsum(nsub) - nsub
    dst = (sub_base * SUB)[None, :] + jnp.cumsum(n_te, axis=0) - n_te
    seg = tuple(a.reshape(-1).astype(jnp.int32) for a in (n_te, src, dst))
    pads = ((sub_base * SUB + counts).astype(jnp.int32), (nsub * SUB - counts).astype(jnp.int32))
    used_sub = jnp.sum(nsub).reshape(1).astype(jnp.int32)
    nvis = (nsub + SUB_MAX - 1) // SUB_MAX
    vend = jnp.cumsum(nvis)
    total = vend[-1]
    vid = jnp.arange(N_VISIT, dtype=jnp.int32)
    ve = jnp.minimum(jnp.sum((vid[:, None] >= vend[None, :]).astype(jnp.int32), axis=1), N_EXPERTS - 1)
    local = vid - (vend - nvis)[ve]
    nv_e = jnp.maximum(nvis[ve], 1)
    q, r = nsub[ve] // nv_e, nsub[ve] % nv_e
    cnt = q + (local < r).astype(jnp.int32)
    start = sub_base[ve] + local * q + jnp.minimum(local, r)
    used = vid < total
    last_e = ve[jnp.maximum(total - 1, 0)]
    vis_e = jnp.where(used, ve, last_e).astype(jnp.int32)
    vis_cnt = jnp.where(used, cnt, 0).astype(jnp.int32)
    vis_start = jnp.where(used, start, 0).astype(jnp.int32)
    return lpos.reshape(N_TILES, TOP_K, DISP_TILE), seg, pads, (vis_e, vis_start, vis_cnt, used_sub)


def _for_pow2_pieces(n, max_piece, fn):
    off = 0
    bit = max_piece
    while bit >= SEG_ALIGN:
        take = (n // bit) % 2 == 1
        pl.when(take)(functools.partial(fn, off, bit))
        off = off + jnp.where(take, bit, 0)
        bit //= 2


def _segment_copies(tile, seg, make_copy, wait=False):
    n_te, src, dst = seg
    for e in range(N_EXPERTS):
        j = tile * N_EXPERTS + e
        s0, d0 = src[j], dst[j]

        def piece(off, size, s0=s0, d0=d0):
            cp = make_copy(pl.multiple_of(s0 + off, SEG_ALIGN), pl.multiple_of(d0 + off, SEG_ALIGN), size)
            cp.wait() if wait else cp.start()
        _for_pow2_pieces(n_te[j], DISP_TILE, piece)


def _dispatch_kernel(n_te, src, dst, pad_start, pad_cnt, used_sub, h_ref, lpos_ref, xs_hbm, ring, ring_sems, sem):
    i = pl.program_id(0)
    last = pl.num_programs(0) - 1
    b = i % 2

    def copies(tile, slot_id, wait):
        _segment_copies(tile, (n_te, src, dst), lambda s, d, size: pltpu.make_async_copy(
            ring.at[slot_id, pl.ds(s, size)], xs_hbm.at[pl.ds(d, size)], ring_sems.at[slot_id]), wait)

    pl.when(i >= 2)(lambda: copies(i - 2, b, True))
    lp = lpos_ref[...]
    srow = lax.broadcasted_iota(jnp.int32, (SORT_ROWS, DISP_TILE), 0)
    perm = jnp.where(lp[0:1, :] == srow, 1.0, jnp.where(lp[1:2, :] == srow, 1.0, 0.0)).astype(_BF)
    ring[b] = jnp.dot(perm, h_ref[...].astype(_BF), preferred_element_type=_F32)
    copies(i, b, False)

    @pl.when(i == last)
    def _():
        copies(i - 1, 1 - b, True)
        copies(i, b, True)
        for e in range(N_EXPERTS):
            p0 = pad_start[e]

            def fill(off, size, p0=p0):
                cp = pltpu.make_async_copy(ring.at[b, pl.ds(0, size)],
                                           xs_hbm.at[pl.ds(pl.multiple_of(p0 + off, SEG_ALIGN), size)], sem)
                cp.start()
                cp.wait()
            _for_pow2_pieces(pad_cnt[e], SUB // 2, fill)

        def tail(s):
            r = pl.multiple_of(s * SUB, SUB)
            return pltpu.make_async_copy(ring.at[b, pl.ds(0, SUB)], xs_hbm.at[pl.ds(r, SUB)], sem)
        lax.fori_loop(used_sub[0], N_SLOT_SUB, lambda s, c: (tail(s).start(), c)[1], 0)
        lax.fori_loop(used_sub[0], N_SLOT_SUB, lambda s, c: (tail(s).wait(), c)[1], 0)


def _dispatch(h, lpos, seg, pads, used_sub):
    return pl.pallas_call(
        _dispatch_kernel,
        grid_spec=pltpu.PrefetchScalarGridSpec(
            num_scalar_prefetch=6,
            grid=(N_TILES,),
            in_specs=[pl.BlockSpec((DISP_TILE, D_MODEL), lambda i, *_: (i, 0)),
                      pl.BlockSpec((None, TOP_K, DISP_TILE), lambda i, *_: (i, 0, 0))],
            out_specs=pl.BlockSpec(memory_space=pl.ANY),
            scratch_shapes=[pltpu.VMEM((2, SORT_ROWS, D_MODEL), _F32),
                            pltpu.SemaphoreType.DMA((2,)), pltpu.SemaphoreType.DMA(())],
        ),
        out_shape=jax.ShapeDtypeStruct((N_SLOT, D_MODEL), _F32),
        compiler_params=_params(("arbitrary",)),
        name="dispatch",
    )(*seg, *pads, used_sub, h, lpos)


def _residual_out(x, gate, f, g_ref, o_refs, i, tile):
    out = x + gate * f
    if g_ref is not None:
        ms = jnp.mean(out * out, axis=-1, keepdims=True)
        out = out * lax.rsqrt(ms + EPS) * g_ref[...]
    if len(o_refs) == 1:
        o_refs[0][...] = out
    else:
        @pl.when(i < T_PROMPT // tile)
        def _():
            o_refs[0][...] = out

        @pl.when(i >= T_PROMPT // tile)
        def _():
            o_refs[1][...] = out


def _out_specs(final, tile):
    if not final:
        return ([pl.BlockSpec((tile, D_MODEL), lambda i, *_: (i, 0))],
                [jax.ShapeDtypeStruct((T, D_MODEL), _F32)])
    return (_half_specs(D_MODEL, tile),
            [jax.ShapeDtypeStruct((T_PROMPT, D_MODEL), _F32), jax.ShapeDtypeStruct((T_SAMPLE, D_MODEL), _F32)])


def _combine_kernel(final, x_ref, mod_ref, y_ref, *rest):
    g_ref, o_refs = (rest[0], rest[1:]) if final else (None, rest)
    _residual_out(x_ref[...], mod_ref[5:6, :], y_ref[...], g_ref, o_refs, pl.program_id(0), ROW_TILE)


def _combine(x, l, mods, y, g_final=None):
    final = g_final is not None
    row = lambda i: (i, 0)
    in_specs = [
        pl.BlockSpec((ROW_TILE, D_MODEL), row),
        _layer_specs(l, ())[0],
        pl.BlockSpec((ROW_TILE, D_MODEL), row),
    ]
    args = [x, mods, y]
    if final:
        in_specs.append(pl.BlockSpec((1, D_MODEL), lambda i: (0, 0)))
        args.append(g_final.reshape(1, D_MODEL))
    out_specs, out_shape = _out_specs(final, ROW_TILE)
    return pl.pallas_call(
        functools.partial(_combine_kernel, final),
        grid=(T // ROW_TILE,),
        in_specs=in_specs,
        out_specs=out_specs,
        out_shape=out_shape,
        compiler_params=_params(("arbitrary",)),
        name="combine%s" % ("_final" if final else ""),
    )(*args)


def _combine_top2_kernel(final, n_out, n_te, src, dst, x_ref, mod_ref, w_ref, lpos_ref, *rest):
    g_ref, rest = (rest[0], rest[1:]) if final else (None, rest)
    ys_hbm, o_refs, (ybuf, sems) = rest[0], rest[1:1 + n_out], rest[1 + n_out:]
    i = pl.program_id(0)

    def fetch(tile, b, wait=False):
        _segment_copies(tile, (n_te, src, dst), lambda s, d, size: pltpu.make_async_copy(
            ys_hbm.at[pl.ds(d, size)], ybuf.at[b, pl.ds(s, size)], sems.at[b]), wait)

    @pl.when(i == 0)
    def _():
        for slot_id in range(2):
            ybuf[slot_id, PAIRS:SORT_ROWS, :] = jnp.zeros((SORT_ROWS - PAIRS, D_MODEL), _F32)
        fetch(0, 0)

    @pl.when(i + 1 < pl.num_programs(0))
    def _():
        fetch(i + 1, (i + 1) % 2)

    b = i % 2
    fetch(i, b, wait=True)
    y = ybuf[b].astype(_BF)
    lp = lpos_ref[...]
    scol = lax.broadcasted_iota(jnp.int32, (DISP_TILE, SORT_ROWS), 1)
    w = w_ref[...]
    pick = jnp.zeros((DISP_TILE, SORT_ROWS), _F32)
    for k in range(TOP_K):
        pick = jnp.where(lp[:, k:k + 1] == scol, w[:, k:k + 1], pick)
    f = jnp.dot(pick.astype(_BF), y, preferred_element_type=_F32)
    _residual_out(x_ref[...], mod_ref[5:6, :], f, g_ref, o_refs, i, DISP_TILE)


def _combine_top2(x, l, mods, ys, lpos_t, seg, w, g_final=None):
    final = g_final is not None
    row = lambda i, *_: (i, 0)
    in_specs = [
        pl.BlockSpec((DISP_TILE, D_MODEL), row),
        pl.BlockSpec((None, None, 6, D_MODEL), lambda i, *_: (l, _cond_of_tile(i, DISP_TILE), 0, 0)),
        pl.BlockSpec((DISP_TILE, TOP_K), row),
        pl.BlockSpec((None, DISP_TILE, TOP_K), lambda i, *_: (i, 0, 0)),
    ]
    args = [x, mods, w, lpos_t]
    if final:
        in_specs.append(pl.BlockSpec((1, D_MODEL), lambda i, *_: (0, 0)))
        args.append(g_final.reshape(1, D_MODEL))
    in_specs.append(pl.BlockSpec(memory_space=pl.ANY))
    args.append(ys)
    out_specs, out_shape = _out_specs(final, DISP_TILE)
    return pl.pallas_call(
        functools.partial(_combine_top2_kernel, final, len(out_specs)),
        grid_spec=pltpu.PrefetchScalarGridSpec(
            num_scalar_prefetch=3,
            grid=(N_TILES,),
            in_specs=in_specs,
            out_specs=out_specs,
            scratch_shapes=[
                pltpu.VMEM((2, SORT_ROWS, D_MODEL), _F32),
                pltpu.SemaphoreType.DMA((2,)),
            ],
        ),
        out_shape=out_shape,
        compiler_params=_params(("arbitrary",)),
        name="combine_top2%s" % ("_final" if final else ""),
    )(*seg, *args)


def kernel(x_prompt, x_sample, cache_k, cache_v, c, c_ctx, w_ada, b_ada, g_norm_mix, g_norm_ffn,
           w_in, w_out, attn_sink, conv_dw, conv_dw_b, conv_ln_g, conv_ln_b, conv_pw, fnet_w,
           ffn_w_gate, ffn_w_up, ffn_w_down, moe_router, moe_w_gate, moe_w_up, moe_w_down, g_final):
    xs = (x_prompt.reshape(T_PROMPT, D_MODEL), x_sample.reshape(T_SAMPLE, D_MODEL))
    cond8 = jnp.concatenate([c_ctx[None, :], c, jnp.zeros((N_COND - 1 - DEC_BATCH, D_MODEL), _F32)], axis=0)
    mods = _modulation(cond8, w_ada, b_ada).reshape(DEPTH, N_COND, 6, D_MODEL)
    cos_t, sin_t = _rope_tables()
    ck_all = cache_k.reshape(DEC_BATCH, DEPTH, PAST_LEN, KV_W)
    cv_all = cache_v.reshape(DEC_BATCH, DEPTH, PAST_LEN, KV_W)
    p_blocks = T_PROMPT // DEC_SEQ
    sinks = attn_sink.reshape(DEPTH * N_HEADS)

    ks, vs = [], []
    resid = None
    for l in range(DEPTH):
        q, k, v, uc, uf, *x_new = _inproj(xs, l, mods, g_norm_mix, w_in, cos_t, sin_t, resid=resid)
        if resid is not None:
            xs, resid = tuple(x_new), None
        ks.append(k[:T_PROMPT])
        vs.append(v[:T_PROMPT])
        attn = (_ctx_attention(l, sinks, q, k, v),
                _lat_attention(l, sinks, q, k, v, ck_all, cv_all))
        cargs = (l, conv_dw, conv_dw_b, conv_ln_g, conv_ln_b, conv_pw)
        conv = (_conv_module(uc, SEQ, BATCH, 0, *cargs),
                _conv_module(uc, DEC_SEQ, DEC_BATCH, p_blocks, *cargs))
        four = (_fourier_mix(uf, SEQ, BATCH, 0, l, fnet_w),
                _fourier_mix(uf, DEC_SEQ, DEC_BATCH, T_PROMPT // 2, l, fnet_w))
        last = g_final if l == DEPTH - 1 else None
        i = l // 2
        if l % 2 == 0:
            x, h = _outproj((attn, conv, four), xs, l, mods, w_out, g_norm_ffn)
            y = _grouped_ffn(h, ffn_w_gate[i:i + 1], ffn_w_up[i:i + 1], ffn_w_down[i:i + 1], 256,
                             *_dense_visits())
            if last is None:
                xs, resid = (x,), (y, l)
            else:
                xs = tuple(_combine(x, l, mods, y, g_final=last))
        else:
            x, h, eidx, ew = _outproj((attn, conv, four), xs, l, mods, w_out, g_norm_ffn,
                                      router_t=moe_router[i].T)
            lpos, seg, pads, visits = _routing_plan(eidx)
            xd = _dispatch(h, lpos, seg, pads, visits[-1])
            ys = _grouped_ffn(xd, moe_w_gate[i], moe_w_up[i], moe_w_down[i], 512, *visits)
            xs = tuple(_combine_top2(x, l, mods, ys, lpos.transpose(0, 2, 1), seg, ew.T, g_final=last))

    y_prompt = xs[0].reshape(BATCH, SEQ, D_MODEL)
    y_sample = xs[1].reshape(DEC_BATCH, DEC_SEQ, D_MODEL)
    state_k = jnp.stack([a.reshape(BATCH, SEQ, N_KV_HEADS, HEAD_DIM) for a in ks], axis=1)
    state_v = jnp.stack([a.reshape(BATCH, SEQ, N_KV_HEADS, HEAD_DIM) for a in vs], axis=1)
    return (y_prompt, y_sample, state_k, state_v)
```

```python
import functools

import numpy as np
import jax
import jax.numpy as jnp
from jax import lax
from jax.experimental import pallas as pl
from jax.experimental.pallas import tpu as pltpu

D_MODEL = 1024
BATCH = 16
SEQ = 256
DEPTH = 2
DEC_BATCH = 2
DEC_SEQ = 2048
PAST_LEN = 512
GRID_W = 64
HEAD_DIM = 64
N_HEADS = 8
N_KV_HEADS = 2
GQA_GROUP = N_HEADS // N_KV_HEADS
ATTN_W = N_HEADS * HEAD_DIM
KV_W = N_KV_HEADS * HEAD_DIM
WINDOW = 128
BLOCK = 128
ROPE_THETA = 10000.0
CONV_W = D_MODEL // 4
CONV_K = 31
FNET_GROUPS = 4
FNET_W = D_MODEL // 4
FNET_GW = FNET_W // FNET_GROUPS
IN_WIDTH = ATTN_W + 2 * KV_W + 2 * CONV_W + FNET_W
D_FF = 2816
N_EXPERTS = 8
TOP_K = 2
D_FF_EXPERT = 3584
EPS = 1e-6
NEG = -1e30

T_PROMPT = BATCH * SEQ
T_SAMPLE = DEC_BATCH * DEC_SEQ
T = T_PROMPT + T_SAMPLE
N_COND = 8

ROW_TILE = 512
SUB = 64
SUB_MAX = 64
TMAX = SUB * SUB_MAX
BLK_MIN = 16
BLK_SUBS = 20
DISP_TILE = 512
N_TILES = T // DISP_TILE
PAIRS = TOP_K * DISP_TILE
SEG_ALIGN = 8
SORT_ROWS = 1152
CONV_CHUNK = 64
CONV_PAD = 16
CONV_SPAN = CONV_CHUNK + 8 * ((CONV_PAD + CONV_K // 2) // 8)
VMEM_LIMIT = 48 * 1024 * 1024
FFN_VMEM_LIMIT = 56 * 1024 * 1024

_BF = jnp.bfloat16
_F32 = jnp.float32


def _cond_of_tile(i, tile):
    r = i * tile
    return jnp.where(r < T_PROMPT, 0, 1 + (r - T_PROMPT) // DEC_SEQ)


def _params(sem, vmem=VMEM_LIMIT):
    return pltpu.CompilerParams(dimension_semantics=sem, vmem_limit_bytes=vmem)


def _sigmoid(x):
    return 1.0 / (1.0 + jnp.exp(-x))


def _mod_kernel(cond_ref, w_ref, b_ref, o_ref):
    cnd = cond_ref[...]
    s = (cnd * _sigmoid(cnd)).astype(_BF)
    o_ref[...] = jnp.dot(s, w_ref[...].astype(_BF), preferred_element_type=_F32) + b_ref[...]


def _modulation(cond8, w_ada, b_ada):
    nt = 1536
    return pl.pallas_call(
        _mod_kernel,
        grid=(DEPTH, 6 * D_MODEL // nt),
        in_specs=[
            pl.BlockSpec((N_COND, D_MODEL), lambda l, n: (0, 0)),
            pl.BlockSpec((None, D_MODEL, nt), lambda l, n: (l, 0, n)),
            pl.BlockSpec((None, 1, nt), lambda l, n: (l, 0, n)),
        ],
        out_specs=pl.BlockSpec((None, N_COND, nt), lambda l, n: (l, 0, n)),
        out_shape=jax.ShapeDtypeStruct((DEPTH, N_COND, 6 * D_MODEL), _F32),
        compiler_params=_params(("arbitrary", "arbitrary")),
        name="modulation",
    )(cond8, w_ada, b_ada.reshape(DEPTH, 1, 6 * D_MODEL))


def _norm_mod(x, g, shift, scale):
    ms = jnp.mean(x * x, axis=-1, keepdims=True)
    y = x * lax.rsqrt(ms + EPS) * g
    return y * (1.0 + scale) + shift


def _rope_tables():
    rows = DEC_SEQ // GRID_W
    n_freq = HEAD_DIM // 4
    inv = ROPE_THETA ** (-jnp.arange(n_freq, dtype=_F32) / n_freq)
    gr, gc = jnp.meshgrid(jnp.arange(rows, dtype=_F32), jnp.arange(GRID_W, dtype=_F32), indexing="ij")
    ang_r = gr.reshape(-1)[:, None] * inv
    ang_c = gc.reshape(-1)[:, None] * inv
    cr, sr, cc, sc = jnp.cos(ang_r), jnp.sin(ang_r), jnp.cos(ang_c), jnp.sin(ang_c)
    cos64 = jnp.concatenate([cr, cr, cc, cc], axis=-1)
    sin64 = jnp.concatenate([-sr, sr, -sc, sc], axis=-1)
    return jnp.tile(cos64, (1, 2)), jnp.tile(sin64, (1, 2))


def _rope128(x, cos, sin):
    lane = lax.broadcasted_iota(jnp.int32, x.shape, 1)
    first = (lane % 32) < 16
    partner = jnp.where(first, pltpu.roll(x, 128 - 16, 1), pltpu.roll(x, 16, 1))
    return x * cos + partner * sin


def _half_specs(width, tile):
    pt = T_PROMPT // tile
    return [pl.BlockSpec((tile, width), lambda i, *_: (jnp.minimum(i, pt - 1), 0)),
            pl.BlockSpec((tile, width), lambda i, *_: (jnp.maximum(i - pt, 0), 0))]


def _pick_half(i, tile, a_ref, b_ref):
    return jnp.where(i < T_PROMPT // tile, a_ref[...], b_ref[...])


def _layer_specs(l, w_shape):
    return [
        pl.BlockSpec((None, None, 6, D_MODEL), lambda i, *_: (l, _cond_of_tile(i, ROW_TILE), 0, 0)),
        pl.BlockSpec((None, 1, D_MODEL), lambda i, *_: (l, 0, 0)),
        pl.BlockSpec((None,) + w_shape, lambda i, *_: (l, 0, 0)),
    ]


def _inproj_kernel(split, resid, *refs):
    n_x = 2 if split else 1
    x_refs, refs = refs[:n_x], refs[n_x:]
    if resid:
        y_ref, modp_ref = refs[:2]
        refs = refs[2:]
    mod_ref, g_ref, w_ref, cos_ref, sin_ref, q_ref, k_ref, v_ref, uc_ref, uf_ref = refs[:10]
    wb_ref, uf_scr = refs[-2:]
    i = pl.program_id(0)

    @pl.when(i == 0)
    def _():
        wb_ref[...] = w_ref[...].astype(_BF)

    x = _pick_half(i, ROW_TILE, *x_refs) if split else x_refs[0][...]
    if resid:
        x = x + modp_ref[5:6, :] * y_ref[...]
        refs[10][...] = x
    h = _norm_mod(x, g_ref[...], mod_ref[0:1, :], mod_ref[1:2, :]).astype(_BF)
    latent = i >= T_PROMPT // ROW_TILE
    cos = jnp.where(latent, cos_ref[...], 1.0)
    sin = jnp.where(latent, sin_ref[...], 0.0)

    def proj(o, n):
        return jnp.dot(h, wb_ref[:, o:o + n], preferred_element_type=_F32)

    o = 0
    for c in range(ATTN_W // 256):
        qq = proj(o, 256)
        q_ref[:, o:o + 128] = _rope128(qq[:, :128], cos, sin).astype(q_ref.dtype)
        q_ref[:, o + 128:o + 256] = _rope128(qq[:, 128:], cos, sin).astype(q_ref.dtype)
        o += 256
    kv = proj(o, 2 * KV_W)
    k_ref[...] = _rope128(kv[:, :KV_W], cos, sin)
    v_ref[...] = kv[:, KV_W:]
    o += 2 * KV_W
    uc_ref[...] = proj(o, 2 * CONV_W)
    o += 2 * CONV_W
    uf = proj(o, FNET_W)
    n_lane = FNET_W // 128
    for c in range(n_lane):
        uf_scr[c] = uf[:, c * 128:(c + 1) * 128]
    half = ROW_TILE // 2
    uf_ref[...] = jnp.concatenate([uf_scr[c, pl.ds(par, half, stride=2), :]
                                   for par in range(2) for c in range(n_lane)], axis=-1).astype(uf_ref.dtype)


def _inproj(xs, l, mods, g_all, w_all, cos_t, sin_t, resid=None):
    split = len(xs) == 2
    pt = T_PROMPT // ROW_TILE
    per_seq = DEC_SEQ // ROW_TILE

    def rope_idx(i):
        return (jnp.maximum(i - pt, 0) % per_seq, 0)

    row = lambda i: (i, 0)
    in_specs = _half_specs(D_MODEL, ROW_TILE) if split else [pl.BlockSpec((ROW_TILE, D_MODEL), row)]
    args = list(xs)
    outs = [ATTN_W, KV_W, KV_W, 2 * CONV_W, FNET_W]
    dtypes = [_BF, _F32, _F32, _F32, _BF]
    fold = [1, 1, 1, 1, 2]
    if resid is not None:
        y, l_prev = resid
        in_specs += [pl.BlockSpec((ROW_TILE, D_MODEL), row), _layer_specs(l_prev, ())[0]]
        args += [y, mods]
        outs.append(D_MODEL)
        dtypes.append(_F32)
        fold.append(1)
    return pl.pallas_call(
        functools.partial(_inproj_kernel, split, resid is not None),
        grid=(T // ROW_TILE,),
        in_specs=in_specs + _layer_specs(l, (D_MODEL, IN_WIDTH)) + [
            pl.BlockSpec((ROW_TILE, 128), rope_idx),
            pl.BlockSpec((ROW_TILE, 128), rope_idx),
        ],
        out_specs=[pl.BlockSpec((ROW_TILE // f, n * f), row) for n, f in zip(outs, fold)],
        out_shape=[jax.ShapeDtypeStruct((T // f, n * f), dt) for n, dt, f in zip(outs, dtypes, fold)],
        scratch_shapes=[pltpu.VMEM((D_MODEL, IN_WIDTH), _BF), pltpu.VMEM((FNET_W // 128, ROW_TILE, 128), _F32)],
        compiler_params=_params(("arbitrary",)),
        name="inproj_resid" if resid is not None else "inproj",
    )(*args, mods, g_all.reshape(DEPTH, 1, D_MODEL), w_all, cos_t, sin_t)


def _stack_groups(q, kh, rows):
    parts = [q[:, (kh * GQA_GROUP + g) * HEAD_DIM:(kh * GQA_GROUP + g + 1) * HEAD_DIM] for g in range(GQA_GROUP)]
    return jnp.concatenate(parts, axis=0)


def _sink_column(sink_ref, head0, rows):
    r = lax.broadcasted_iota(jnp.int32, (GQA_GROUP * rows, 1), 0)
    col = jnp.full((GQA_GROUP * rows, 1), sink_ref[head0], _F32)
    for g in range(1, GQA_GROUP):
        col = jnp.where(r >= g * rows, sink_ref[head0 + g], col)
    return col


def _unstack_store(o_ref, o, kh, rows):
    for pair in range(GQA_GROUP // 2):
        a = o[(2 * pair) * rows:(2 * pair + 1) * rows]
        b = o[(2 * pair + 1) * rows:(2 * pair + 2) * rows]
        c0 = (kh * GQA_GROUP + 2 * pair) * HEAD_DIM
        o_ref[:, c0:c0 + 2 * HEAD_DIM] = jnp.concatenate([a, b], axis=-1).astype(o_ref.dtype)


_NT = (((1,), (1,)), ((), ()))


def _ctx_attn_kernel(l, sink_ref, q_ref, k_ref, v_ref, o_ref):
    scale = HEAD_DIM ** -0.5
    q = q_ref[...] * scale
    for kh in range(N_KV_HEADS):
        kk = k_ref[:, kh * HEAD_DIM:(kh + 1) * HEAD_DIM].astype(_BF)
        vv = v_ref[:, kh * HEAD_DIM:(kh + 1) * HEAD_DIM].astype(_BF)
        qs = _stack_groups(q, kh, SEQ).astype(_BF)
        s = lax.dot_general(qs, kk, _NT, preferred_element_type=_F32)
        sink = _sink_column(sink_ref, l * N_HEADS + kh * GQA_GROUP, SEQ)
        m = jnp.maximum(jnp.max(s, axis=-1, keepdims=True), sink)
        p = jnp.exp(s - m)
        den = jnp.sum(p, axis=-1, keepdims=True) + jnp.exp(sink - m)
        o = jnp.dot(p.astype(_BF), vv, preferred_element_type=_F32) / den
        _unstack_store(o_ref, o, kh, SEQ)


def _ctx_attention(l, sinks, q, k, v):
    row = lambda b, s: (b, 0)
    return pl.pallas_call(
        functools.partial(_ctx_attn_kernel, l),
        grid_spec=pltpu.PrefetchScalarGridSpec(
            num_scalar_prefetch=1,
            grid=(BATCH,),
            in_specs=[
                pl.BlockSpec((SEQ, ATTN_W), row),
                pl.BlockSpec((SEQ, KV_W), row),
                pl.BlockSpec((SEQ, KV_W), row),
            ],
            out_specs=pl.BlockSpec((SEQ, ATTN_W), row),
        ),
        out_shape=jax.ShapeDtypeStruct((T_PROMPT, ATTN_W), _BF),
        compiler_params=_params(("arbitrary",)),
        name="ctx_attention",
    )(sinks, q, k, v)


def _lat_attn_kernel(l, sink_ref, q_ref, k_ref, v_ref, ck_ref, cv_ref, o_ref):
    n = pl.program_id(1)
    scale = HEAD_DIM ** -0.5
    nb = DEC_SEQ // BLOCK
    band = 3 * BLOCK
    start = pl.multiple_of(jnp.clip(n - 1, 0, nb - 3) * BLOCK, BLOCK)
    q = q_ref[...] * scale
    kb = k_ref[pl.ds(start, band), :]
    vb = v_ref[pl.ds(start, band), :]
    dpos = (lax.broadcasted_iota(jnp.int32, (BLOCK, band), 1) - lax.broadcasted_iota(jnp.int32, (BLOCK, band), 0)
            + (start - n * BLOCK))
    in_window = jnp.where(jnp.abs(dpos) <= WINDOW, 0.0, NEG)
    bias = jnp.concatenate([in_window] * GQA_GROUP, axis=0)
    for kh in range(N_KV_HEADS):
        hs = slice(kh * HEAD_DIM, (kh + 1) * HEAD_DIM)
        qs = _stack_groups(q, kh, BLOCK).astype(_BF)
        s_loc = lax.dot_general(qs, kb[:, hs].astype(_BF), _NT, preferred_element_type=_F32)
        s_loc = jnp.where(bias < 0.0, NEG, s_loc)
        s_ctx = lax.dot_general(qs, ck_ref[:, hs].astype(_BF), _NT, preferred_element_type=_F32)
        sink = _sink_column(sink_ref, l * N_HEADS + kh * GQA_GROUP, BLOCK)
        m = jnp.maximum(jnp.maximum(jnp.max(s_loc, axis=-1, keepdims=True),
                                    jnp.max(s_ctx, axis=-1, keepdims=True)), sink)
        p_loc = jnp.exp(s_loc - m)
        p_ctx = jnp.exp(s_ctx - m)
        den = (jnp.sum(p_loc, axis=-1, keepdims=True) + jnp.sum(p_ctx, axis=-1, keepdims=True)
               + jnp.exp(sink - m))
        o = (jnp.dot(p_loc.astype(_BF), vb[:, hs].astype(_BF), preferred_element_type=_F32)
             + jnp.dot(p_ctx.astype(_BF), cv_ref[:, hs].astype(_BF), preferred_element_type=_F32)) / den
        _unstack_store(o_ref, o, kh, BLOCK)


def _lat_attention(l, sinks, q, k, v, ck, cv):
    nb = DEC_SEQ // BLOCK
    q0 = T_PROMPT // BLOCK
    s0 = T_PROMPT // DEC_SEQ
    return pl.pallas_call(
        functools.partial(_lat_attn_kernel, l),
        grid_spec=pltpu.PrefetchScalarGridSpec(
            num_scalar_prefetch=1,
            grid=(DEC_BATCH, nb),
            in_specs=[
                pl.BlockSpec((BLOCK, ATTN_W), lambda b, n, s: (q0 + b * nb + n, 0)),
                pl.BlockSpec((DEC_SEQ, KV_W), lambda b, n, s: (s0 + b, 0)),
                pl.BlockSpec((DEC_SEQ, KV_W), lambda b, n, s: (s0 + b, 0)),
                pl.BlockSpec((None, None, PAST_LEN, KV_W), lambda b, n, s: (b, l, 0, 0)),
                pl.BlockSpec((None, None, PAST_LEN, KV_W), lambda b, n, s: (b, l, 0, 0)),
            ],
            out_specs=pl.BlockSpec((BLOCK, ATTN_W), lambda b, n, s: (b * nb + n, 0)),
        ),
        out_shape=jax.ShapeDtypeStruct((T_SAMPLE, ATTN_W), _BF),
        compiler_params=_params(("arbitrary", "arbitrary")),
        name="lat_attention",
    )(sinks, q, k, v, ck, cv)


def _conv_kernel(seq, u_ref, dw_ref, dwb_ref, lg_ref, lb_ref, pw_ref, o_ref, pad_ref, y_ref, sh_ref):
    u = u_ref[...]
    pad_ref[0:CONV_PAD, :] = jnp.zeros((CONV_PAD, CONV_W), _F32)
    pad_ref[CONV_PAD + seq:2 * CONV_PAD + seq, :] = jnp.zeros((CONV_PAD, CONV_W), _F32)
    pad_ref[CONV_PAD:CONV_PAD + seq, :] = u[:, :CONV_W] * _sigmoid(u[:, CONV_W:])
    off = CONV_PAD - CONV_K // 2
    span = CONV_SPAN

    def chunk(c, carry):
        row = pl.multiple_of(c * CONV_CHUNK, CONV_CHUNK)
        win = pad_ref[pl.ds(row, CONV_CHUNK + 2 * CONV_PAD), :]
        acc = jnp.zeros((CONV_CHUNK, CONV_W), _F32) + dwb_ref[...]
        for phase in range(8):
            sh_ref[phase] = win[phase:phase + span, :]
        for phase in range(8):
            for a in range(span // 8):
                t = 8 * a + phase - off
                if 0 <= t < CONV_K:
                    acc = acc + sh_ref[phase, 8 * a:8 * a + CONV_CHUNK, :] * dw_ref[t:t + 1, :]
        mu = jnp.mean(acc, axis=-1, keepdims=True)
        d = acc - mu
        var = jnp.mean(d * d, axis=-1, keepdims=True)
        y = d * lax.rsqrt(var + EPS) * lg_ref[...] + lb_ref[...]
        y_ref[pl.ds(row, CONV_CHUNK), :] = (y * _sigmoid(y)).astype(_BF)
        return carry

    lax.fori_loop(0, seq // CONV_CHUNK, chunk, 0)
    o_ref[...] = jnp.dot(y_ref[...], pw_ref[...].astype(_BF), preferred_element_type=_F32).astype(o_ref.dtype)


def _conv_module(uc, seq, nbatch, block0, l, dw, dwb, lg, lb, pw):
    vec = lambda a: a.reshape(DEPTH, 1, CONV_W)
    layer = lambda b: (l, 0, 0)
    return pl.pallas_call(
        functools.partial(_conv_kernel, seq),
        grid=(nbatch,),
        in_specs=[
            pl.BlockSpec((seq, 2 * CONV_W), lambda b: (block0 + b, 0)),
            pl.BlockSpec((None, CONV_K, CONV_W), layer),
            pl.BlockSpec((None, 1, CONV_W), layer),
            pl.BlockSpec((None, 1, CONV_W), layer),
            pl.BlockSpec((None, 1, CONV_W), layer),
            pl.BlockSpec((None, CONV_W, CONV_W), layer),
        ],
        out_specs=pl.BlockSpec((seq, CONV_W), lambda b: (b, 0)),
        out_shape=jax.ShapeDtypeStruct((nbatch * seq, CONV_W), _BF),
        scratch_shapes=[pltpu.VMEM((seq + 2 * CONV_PAD, CONV_W), _F32), pltpu.VMEM((seq, CONV_W), _BF),
                        pltpu.VMEM((8, CONV_SPAN, CONV_W), _F32)],
        compiler_params=_params(("arbitrary",)),
        name="conv_module_%d" % seq,
    )(uc, dw, vec(dwb), vec(lg), vec(lb), pw)


def _dft_constants(seq):
    half = seq // 2
    j = np.arange(half, dtype=np.int64)
    ang = 2.0 * np.pi * ((j[:, None] * j[None, :]) % half).astype(np.float64) / half
    m = np.concatenate([np.cos(ang), -np.sin(ang)], axis=1) / np.sqrt(seq)
    tw = np.pi * j.astype(np.float64) / half
    cb = np.repeat(np.cos(tw)[:, None], FNET_W, axis=1)
    sb = np.repeat(np.sin(tw)[:, None], FNET_W, axis=1)
    c = np.arange(FNET_GW, dtype=np.int64)
    angc = 2.0 * np.pi * ((c[:, None] * c[None, :]) % FNET_GW).astype(np.float64) / FNET_GW
    eye = np.eye(FNET_GROUPS)
    w1 = np.concatenate([np.kron(eye, np.cos(angc)), np.kron(eye, np.sin(angc))], axis=1) / np.sqrt(FNET_GW)
    return tuple(jnp.asarray(a, dtype=_F32) for a in (m, cb, sb, w1))


def _fnet_kernel(half, group, u_ref, m_ref, cb_ref, sb_ref, w1_ref, fw_ref, o_ref, rhs_ref):
    w1 = w1_ref[...].astype(_BF)
    te = jnp.dot(u_ref[:, :FNET_W], w1, preferred_element_type=_F32)
    to = jnp.dot(u_ref[:, FNET_W:], w1, preferred_element_type=_F32)
    for g in range(group):
        rows = slice(g * half, (g + 1) * half)
        ec, es = te[rows, :FNET_W], te[rows, FNET_W:]
        oc, os_ = to[rows, :FNET_W], to[rows, FNET_W:]
        for col, (top, bot) in enumerate(((ec, es), (oc, os_), (os_, -oc))):
            lanes = slice((3 * g + col) * FNET_W, (3 * g + col + 1) * FNET_W)
            rhs_ref[0:half, lanes] = top.astype(_BF)
            rhs_ref[half:2 * half, lanes] = bot.astype(_BF)
    pqr = jnp.dot(m_ref[...].astype(_BF), rhs_ref[...], preferred_element_type=_F32)
    cb, sb = cb_ref[...], sb_ref[...]
    ys = []
    for g in range(group):
        p, q, r = (pqr[:, (3 * g + col) * FNET_W:(3 * g + col + 1) * FNET_W] for col in range(3))
        rot = cb * q - sb * r
        ys += [(p + rot).astype(_BF), (p - rot).astype(_BF)]
    o_ref[...] = jnp.dot(jnp.concatenate(ys, axis=0), fw_ref[...].astype(_BF),
                         preferred_element_type=_F32).astype(o_ref.dtype)


def _fourier_mix(u_pairs, seq, nbatch, row0, l, fw):
    half = seq // 2
    group = max(1, 512 // half)
    rows = group * half
    m, cb, sb, w1 = _dft_constants(seq)
    const = lambda b: (0, 0)
    return pl.pallas_call(
        functools.partial(_fnet_kernel, half, group),
        grid=(nbatch // group,),
        in_specs=[
            pl.BlockSpec((rows, 2 * FNET_W), lambda b: (row0 // rows + b, 0)),
            pl.BlockSpec((half, 2 * half), const),
            pl.BlockSpec((half, FNET_W), const),
            pl.BlockSpec((half, FNET_W), const),
            pl.BlockSpec((FNET_W, 2 * FNET_W), const),
            pl.BlockSpec((None, FNET_W, FNET_W), lambda b: (l, 0, 0)),
        ],
        out_specs=pl.BlockSpec((2 * rows, FNET_W), lambda b: (b, 0)),
        out_shape=jax.ShapeDtypeStruct((nbatch * seq, FNET_W), _BF),
        scratch_shapes=[pltpu.VMEM((2 * half, 3 * group * FNET_W), _BF)],
        compiler_params=_params(("arbitrary",)),
        name="fourier_mix_%d" % seq,
    )(u_pairs, m, cb, sb, w1, fw)


def _route_top2(r_ref, h, ei_ref, ew_ref):
    lg = lax.dot_general(r_ref[...].astype(_BF), h.astype(_BF), _NT, preferred_element_type=_F32)
    eid = lax.broadcasted_iota(jnp.int32, lg.shape, 0)
    m1 = jnp.max(lg, axis=0, keepdims=True)
    i1 = jnp.min(jnp.where(lg == m1, eid, N_EXPERTS), axis=0, keepdims=True)
    lg2 = jnp.where(eid == i1, -jnp.inf, lg)
    m2 = jnp.max(lg2, axis=0, keepdims=True)
    i2 = jnp.min(jnp.where(lg2 == m2, eid, N_EXPERTS), axis=0, keepdims=True)
    e = jnp.exp(m2 - m1)
    ei_ref[0:1, :] = i1
    ei_ref[1:2, :] = i2
    ew_ref[0:1, :] = 1.0 / (1.0 + e)
    ew_ref[1:2, :] = e / (1.0 + e)


def _outproj_kernel(split, route, *refs):
    branch_refs, refs = refs[:6], refs[6:]
    n_x = 2 if split else 1
    x_refs, refs = refs[:n_x], refs[n_x:]
    mod_ref, w_ref, gf_ref = refs[:3]
    refs = refs[3:]
    if route:
        r_ref, xo_ref, h_ref, ei_ref, ew_ref, wb_ref = refs
    else:
        xo_ref, h_ref, wb_ref = refs
    i = pl.program_id(0)

    @pl.when(i == 0)
    def _():
        wb_ref[...] = w_ref[...].astype(_BF)

    mix = jnp.concatenate([_pick_half(i, ROW_TILE, branch_refs[2 * n], branch_refs[2 * n + 1])
                           for n in range(3)], axis=-1)
    x = _pick_half(i, ROW_TILE, *x_refs) if split else x_refs[0][...]
    x_new = x + mod_ref[2:3, :] * jnp.dot(mix, wb_ref[...], preferred_element_type=_F32)
    xo_ref[...] = x_new
    h = _norm_mod(x_new, gf_ref[...], mod_ref[3:4, :], mod_ref[4:5, :])
    h_ref[...] = h.astype(h_ref.dtype)
    if route:
        _route_top2(r_ref, h, ei_ref, ew_ref)


def _outproj(branches, xs, l, mods, w_all, gf_all, router_t=None):
    split = len(xs) == 2
    route = router_t is not None
    row = lambda i: (i, 0)
    in_specs = []
    args = []
    for pair, width in zip(branches, (ATTN_W, CONV_W, FNET_W)):
        in_specs += _half_specs(width, ROW_TILE)
        args += list(pair)
    in_specs += _half_specs(D_MODEL, ROW_TILE) if split else [pl.BlockSpec((ROW_TILE, D_MODEL), row)]
    args += list(xs)
    mod_spec, gf_spec, w_spec = _layer_specs(l, (D_MODEL, D_MODEL))
    in_specs += [mod_spec, w_spec, gf_spec]
    args += [mods, w_all, gf_all.reshape(DEPTH, 1, D_MODEL)]
    out_specs = [pl.BlockSpec((ROW_TILE, D_MODEL), row)] * 2
    out_shape = [jax.ShapeDtypeStruct((T, D_MODEL), _F32), jax.ShapeDtypeStruct((T, D_MODEL), _BF)]
    if route:
        in_specs.append(pl.BlockSpec((N_EXPERTS, D_MODEL), lambda i: (0, 0)))
        args.append(router_t)
        out_specs += [pl.BlockSpec((TOP_K, ROW_TILE), lambda i: (0, i))] * 2
        out_shape += [jax.ShapeDtypeStruct((TOP_K, T), jnp.int32), jax.ShapeDtypeStruct((TOP_K, T), _F32)]
    return pl.pallas_call(
        functools.partial(_outproj_kernel, split, route),
        grid=(T // ROW_TILE,),
        in_specs=in_specs,
        out_specs=out_specs,
        out_shape=out_shape,
        scratch_shapes=[pltpu.VMEM((D_MODEL, D_MODEL), _BF)],
        compiler_params=_params(("arbitrary",)),
        name="outproj_route" if route else "outproj",
    )(*args)


def _ffn_kernel(nj, n_sub_rows, vis_e, vis_start, vis_cnt, used_sub,
                x_hbm, wg_ref, wu_ref, wd_ref, y_hbm,
                big, xb, wgb, wub, wdb, sem_in, sem_out):
    del vis_e
    v = pl.program_id(0)
    j = pl.program_id(1)
    cnt = vis_cnt[v]
    row0 = vis_start[v] * SUB

    x_is_bf16 = x_hbm.dtype == _BF
    landing = xb if x_is_bf16 else big

    def copy_in(s):
        r = pl.multiple_of(row0 + s * SUB, SUB)
        b = pl.multiple_of(s * SUB, SUB)
        return pltpu.make_async_copy(x_hbm.at[pl.ds(r, SUB)], landing.at[pl.ds(b, SUB)], sem_in.at[s])

    def copy_out(s):
        r = pl.multiple_of(row0 + s * SUB, SUB)
        b = pl.multiple_of(s * SUB, SUB)
        return pltpu.make_async_copy(big.at[pl.ds(b, SUB)], y_hbm.at[pl.ds(r, SUB)], sem_out)

    def for_subs(fn):
        lax.fori_loop(0, cnt, lambda s, c: (fn(s), c)[1], 0)

    @pl.when(jnp.logical_and(j == 0, cnt > 0))
    def _():
        for_subs(lambda s: copy_in(s).start())

    @pl.when(cnt > 0)
    def _():
        wgb[...] = wg_ref[...].astype(_BF)
        wub[...] = wu_ref[...].astype(_BF)
        wdb[...] = wd_ref[...].astype(_BF)

        def block(r0, nrows):
            s0 = r0 // SUB

            @pl.when(j == 0)
            def _():
                for t in range(nrows // SUB):
                    copy_in(s0 + t).wait()
                    rs = pl.ds(pl.multiple_of(r0 + t * SUB, SUB), SUB)
                    if not x_is_bf16:
                        xb[rs, :] = big[rs, :].astype(_BF)
                    big[rs, :] = jnp.zeros((SUB, D_MODEL), _F32)

            rs = pl.ds(r0, nrows)
            x = xb[rs, :]
            a = jnp.dot(x, wgb[...], preferred_element_type=_F32)
            b = jnp.dot(x, wub[...], preferred_element_type=_F32)
            p = (a * _sigmoid(a) * b).astype(_BF)
            big[rs, :] += jnp.dot(p, wdb[...], preferred_element_type=_F32)

            @pl.when(j == nj - 1)
            def _():
                for t in range(nrows // SUB):
                    copy_out(s0 + t).start()

        n_big = cnt // BLK_MIN
        rem = cnt - n_big * BLK_MIN
        room = BLK_SUBS - BLK_MIN
        grow = jnp.minimum(rem, n_big * room)
        left = rem - grow

        def one_block(i, r0):
            size = BLK_MIN + jnp.clip(grow - i * room, 0, room)
            for sz in range(BLK_MIN, BLK_SUBS + 1):
                pl.when(size == sz)(functools.partial(block, pl.multiple_of(r0, SUB), sz * SUB))
            return r0 + size * SUB
        r0 = lax.fori_loop(0, n_big, one_block, 0)
        bit = BLK_MIN // 2
        while bit >= 1:
            take = (left // bit) % 2 == 1
            pl.when(take)(functools.partial(block, pl.multiple_of(r0, SUB), bit * SUB))
            r0 = r0 + jnp.where(take, bit * SUB, 0)
            bit //= 2

    @pl.when(jnp.logical_and(j == nj - 1, cnt > 0))
    def _():
        for_subs(lambda s: copy_out(s).wait())

    if x_hbm.dtype == y_hbm.dtype:
        @pl.when(jnp.logical_and(v == pl.num_programs(0) - 1, j == nj - 1))
        def _():
            def tail(s):
                r = pl.multiple_of(s * SUB, SUB)
                return pltpu.make_async_copy(x_hbm.at[pl.ds(r, SUB)], y_hbm.at[pl.ds(r, SUB)], sem_out)
            lax.fori_loop(used_sub[0], n_sub_rows, lambda s, c: (tail(s).start(), c)[1], 0)
            lax.fori_loop(used_sub[0], n_sub_rows, lambda s, c: (tail(s).wait(), c)[1], 0)


def _grouped_ffn(x, wg, wu, wd, fc, vis_e, vis_start, vis_cnt, used_sub):
    ff = wg.shape[-1]
    nj = ff // fc
    nv = vis_e.shape[0]
    n_rows = x.shape[0]

    def chunk(v, j, vc):
        return jnp.where(vc[v] > 0, j, nj - 1)

    return pl.pallas_call(
        functools.partial(_ffn_kernel, nj, n_rows // SUB),
        grid_spec=pltpu.PrefetchScalarGridSpec(
            num_scalar_prefetch=4,
            grid=(nv, nj),
            in_specs=[
                pl.BlockSpec(memory_space=pl.ANY),
                pl.BlockSpec((None, D_MODEL, fc), lambda v, j, ve, vs, vc, us: (ve[v], 0, chunk(v, j, vc))),
                pl.BlockSpec((None, D_MODEL, fc), lambda v, j, ve, vs, vc, us: (ve[v], 0, chunk(v, j, vc))),
                pl.BlockSpec((None, fc, D_MODEL), lambda v, j, ve, vs, vc, us: (ve[v], chunk(v, j, vc), 0)),
            ],
            out_specs=pl.BlockSpec(memory_space=pl.ANY),
            scratch_shapes=[
                pltpu.VMEM((TMAX, D_MODEL), _F32),
                pltpu.VMEM((TMAX, D_MODEL), _BF),
                pltpu.VMEM((D_MODEL, fc), _BF),
                pltpu.VMEM((D_MODEL, fc), _BF),
                pltpu.VMEM((fc, D_MODEL), _BF),
                pltpu.SemaphoreType.DMA((SUB_MAX,)),
                pltpu.SemaphoreType.DMA(()),
            ],
        ),
        out_shape=jax.ShapeDtypeStruct((n_rows, D_MODEL), _F32),
        compiler_params=_params(("arbitrary", "arbitrary"), FFN_VMEM_LIMIT),
        name="ffn_%d" % ff,
    )(vis_e, vis_start, vis_cnt, used_sub, x, wg, wu, wd)


def _dense_visits():
    nv = T // TMAX
    return (jnp.zeros((nv,), jnp.int32),
            jnp.arange(nv, dtype=jnp.int32) * SUB_MAX,
            jnp.full((nv,), SUB_MAX, jnp.int32),
            jnp.full((1,), T // SUB, jnp.int32))


N_SLOT_SUB = -(-(TOP_K * T + N_TILES * N_EXPERTS * (SEG_ALIGN - 1)) // SUB) + N_EXPERTS
N_SLOT = N_SLOT_SUB * SUB
N_VISIT = -(-N_SLOT_SUB // SUB_MAX) + N_EXPERTS


def _routing_plan(eidx):
    e_loc = eidx.reshape(TOP_K, N_TILES, DISP_TILE).transpose(1, 0, 2).reshape(N_TILES, PAIRS)
    onehot = e_loc[:, :, None] == jnp.arange(N_EXPERTS, dtype=jnp.int32)[None, None, :]
    ch = 128
    oh = onehot.astype(_F32).reshape(N_TILES, PAIRS // ch, ch, N_EXPERTS)
    tri = (jnp.arange(ch)[:, None] >= jnp.arange(ch)[None, :]).astype(_F32)
    within = jnp.einsum("ij,tcjk->tcik", tri, oh)
    tot = within[:, :, -1, :]
    csum = (within + (jnp.cumsum(tot, axis=1) - tot)[:, :, None, :]).reshape(N_TILES, PAIRS, N_EXPERTS)
    csum = csum.astype(jnp.int32)
    n_te = (csum[:, -1, :] + SEG_ALIGN - 1) // SEG_ALIGN * SEG_ALIGN
    src = jnp.cumsum(n_te, axis=1) - n_te
    lpos = jnp.sum(jnp.where(onehot, csum - 1 + src[:, None, :], 0), axis=2).astype(jnp.int32)
    counts = jnp.sum(n_te, axis=0)
    nsub = (counts + SUB - 1) // SUB
    sub_base = jnp.cumsum(nsub) - nsub
    dst = (sub_base * SUB)[None, :] + jnp.cumsum(n_te, axis=0) - n_te
    seg = tuple(a.reshape(-1).astype(jnp.int32) for a in (n_te, src, dst))
    pads = ((sub_base * SUB + counts).astype(jnp.int32), (nsub * SUB - counts).astype(jnp.int32))
    used_sub = jnp.sum(nsub).reshape(1).astype(jnp.int32)
    nvis = (nsub + SUB_MAX - 1) // SUB_MAX
    vend = jnp.cumsum(nvis)
    total = vend[-1]
    vid = jnp.arange(N_VISIT, dtype=jnp.int32)
    ve = jnp.minimum(jnp.sum((vid[:, None] >= vend[None, :]).astype(jnp.int32), axis=1), N_EXPERTS - 1)
    local = vid - (vend - nvis)[ve]
    nv_e = jnp.maximum(nvis[ve], 1)
    q, r = nsub[ve] // nv_e, nsub[ve] % nv_e
    cnt = q + (local < r).astype(jnp.int32)
    start = sub_base[ve] + local * q + jnp.minimum(local, r)
    used = vid < total
    last_e = ve[jnp.maximum(total - 1, 0)]
    vis_e = jnp.where(used, ve, last_e).astype(jnp.int32)
    vis_cnt = jnp.where(used, cnt, 0).astype(jnp.int32)
    vis_start = jnp.where(used, start, 0).astype(jnp.int32)
    return lpos.reshape(N_TILES, TOP_K, DISP_TILE), seg, pads, (vis_e, vis_start, vis_cnt, used_sub)


def _for_pow2_pieces(n, max_piece, fn):
    off = 0
    bit = max_piece
    while bit >= SEG_ALIGN:
        take = (n // bit) % 2 == 1
        pl.when(take)(functools.partial(fn, off, bit))
        off = off + jnp.where(take, bit, 0)
        bit //= 2


def _segment_copies(tile, seg, make_copy, wait=False):
    n_te, src, dst = seg
    for e in range(N_EXPERTS):
        j = tile * N_EXPERTS + e
        s0, d0 = src[j], dst[j]

        def piece(off, size, s0=s0, d0=d0):
            cp = make_copy(pl.multiple_of(s0 + off, SEG_ALIGN), pl.multiple_of(d0 + off, SEG_ALIGN), size)
            cp.wait() if wait else cp.start()
        _for_pow2_pieces(n_te[j], DISP_TILE, piece)


def _dispatch_kernel(n_te, src, dst, pad_start, pad_cnt, used_sub, h_ref, lpos_ref, xs_hbm, ring, ring_sems, sem):
    i = pl.program_id(0)
    last = pl.num_programs(0) - 1
    b = i % 2

    def copies(tile, slot_id, wait):
        _segment_copies(tile, (n_te, src, dst), lambda s, d, size: pltpu.make_async_copy(
            ring.at[slot_id, pl.ds(s, size)], xs_hbm.at[pl.ds(d, size)], ring_sems.at[slot_id]), wait)

    pl.when(i >= 2)(lambda: copies(i - 2, b, True))
    lp = lpos_ref[...]
    srow = lax.broadcasted_iota(jnp.int32, (SORT_ROWS, DISP_TILE), 0)
    perm = jnp.where(lp[0:1, :] == srow, 1.0, jnp.where(lp[1:2, :] == srow, 1.0, 0.0)).astype(_BF)
    ring[b] = jnp.dot(perm, h_ref[...].astype(_BF), preferred_element_type=_F32)
    copies(i, b, False)

    @pl.when(i == last)
    def _():
        copies(i - 1, 1 - b, True)
        copies(i, b, True)
        for e in range(N_EXPERTS):
            p0 = pad_start[e]

            def fill(off, size, p0=p0):
                cp = pltpu.make_async_copy(ring.at[b, pl.ds(0, size)],
                                           xs_hbm.at[pl.ds(pl.multiple_of(p0 + off, SEG_ALIGN), size)], sem)
                cp.start()
                cp.wait()
            _for_pow2_pieces(pad_cnt[e], SUB // 2, fill)

        def tail(s):
            r = pl.multiple_of(s * SUB, SUB)
            return pltpu.make_async_copy(ring.at[b, pl.ds(0, SUB)], xs_hbm.at[pl.ds(r, SUB)], sem)
        lax.fori_loop(used_sub[0], N_SLOT_SUB, lambda s, c: (tail(s).start(), c)[1], 0)
        lax.fori_loop(used_sub[0], N_SLOT_SUB, lambda s, c: (tail(s).wait(), c)[1], 0)


def _dispatch(h, lpos, seg, pads, used_sub):
    return pl.pallas_call(
        _dispatch_kernel,
        grid_spec=pltpu.PrefetchScalarGridSpec(
            num_scalar_prefetch=6,
            grid=(N_TILES,),
            in_specs=[pl.BlockSpec((DISP_TILE, D_MODEL), lambda i, *_: (i, 0)),
                      pl.BlockSpec((None, TOP_K, DISP_TILE), lambda i, *_: (i, 0, 0))],
            out_specs=pl.BlockSpec(memory_space=pl.ANY),
            scratch_shapes=[pltpu.VMEM((2, SORT_ROWS, D_MODEL), _F32),
                            pltpu.SemaphoreType.DMA((2,)), pltpu.SemaphoreType.DMA(())],
        ),
        out_shape=jax.ShapeDtypeStruct((N_SLOT, D_MODEL), _F32),
        compiler_params=_params(("arbitrary",)),
        name="dispatch",
    )(*seg, *pads, used_sub, h, lpos)


def _residual_out(x, gate, f, g_ref, o_refs, i, tile):
    out = x + gate * f
    if g_ref is not None:
        ms = jnp.mean(out * out, axis=-1, keepdims=True)
        out = out * lax.rsqrt(ms + EPS) * g_ref[...]
    if len(o_refs) == 1:
        o_refs[0][...] = out
    else:
        @pl.when(i < T_PROMPT // tile)
        def _():
            o_refs[0][...] = out

        @pl.when(i >= T_PROMPT // tile)
        def _():
            o_refs[1][...] = out


def _out_specs(final, tile):
    if not final:
        return ([pl.BlockSpec((tile, D_MODEL), lambda i, *_: (i, 0))],
                [jax.ShapeDtypeStruct((T, D_MODEL), _F32)])
    return (_half_specs(D_MODEL, tile),
            [jax.ShapeDtypeStruct((T_PROMPT, D_MODEL), _F32), jax.ShapeDtypeStruct((T_SAMPLE, D_MODEL), _F32)])


def _combine_kernel(final, x_ref, mod_ref, y_ref, *rest):
    g_ref, o_refs = (rest[0], rest[1:]) if final else (None, rest)
    _residual_out(x_ref[...], mod_ref[5:6, :], y_ref[...], g_ref, o_refs, pl.program_id(0), ROW_TILE)


def _combine(x, l, mods, y, g_final=None):
    final = g_final is not None
    row = lambda i: (i, 0)
    in_specs = [
        pl.BlockSpec((ROW_TILE, D_MODEL), row),
        _layer_specs(l, ())[0],
        pl.BlockSpec((ROW_TILE, D_MODEL), row),
    ]
    args = [x, mods, y]
    if final:
        in_specs.append(pl.BlockSpec((1, D_MODEL), lambda i: (0, 0)))
        args.append(g_final.reshape(1, D_MODEL))
    out_specs, out_shape = _out_specs(final, ROW_TILE)
    return pl.pallas_call(
        functools.partial(_combine_kernel, final),
        grid=(T // ROW_TILE,),
        in_specs=in_specs,
        out_specs=out_specs,
        out_shape=out_shape,
        compiler_params=_params(("arbitrary",)),
        name="combine%s" % ("_final" if final else ""),
    )(*args)


def _combine_top2_kernel(final, n_out, n_te, src, dst, x_ref, mod_ref, w_ref, lpos_ref, *rest):
    g_ref, rest = (rest[0], rest[1:]) if final else (None, rest)
    ys_hbm, o_refs, (ybuf, sems) = rest[0], rest[1:1 + n_out], rest[1 + n_out:]
    i = pl.program_id(0)

    def fetch(tile, b, wait=False):
        _segment_copies(tile, (n_te, src, dst), lambda s, d, size: pltpu.make_async_copy(
            ys_hbm.at[pl.ds(d, size)], ybuf.at[b, pl.ds(s, size)], sems.at[b]), wait)

    @pl.when(i == 0)
    def _():
        for slot_id in range(2):
            ybuf[slot_id, PAIRS:SORT_ROWS, :] = jnp.zeros((SORT_ROWS - PAIRS, D_MODEL), _F32)
        fetch(0, 0)

    @pl.when(i + 1 < pl.num_programs(0))
    def _():
        fetch(i + 1, (i + 1) % 2)

    b = i % 2
    fetch(i, b, wait=True)
    y = ybuf[b].astype(_BF)
    lp = lpos_ref[...]
    scol = lax.broadcasted_iota(jnp.int32, (DISP_TILE, SORT_ROWS), 1)
    w = w_ref[...]
    pick = jnp.zeros((DISP_TILE, SORT_ROWS), _F32)
    for k in range(TOP_K):
        pick = jnp.where(lp[:, k:k + 1] == scol, w[:, k:k + 1], pick)
    f = jnp.dot(pick.astype(_BF), y, preferred_element_type=_F32)
    _residual_out(x_ref[...], mod_ref[5:6, :], f, g_ref, o_refs, i, DISP_TILE)


def _combine_top2(x, l, mods, ys, lpos_t, seg, w, g_final=None):
    final = g_final is not None
    row = lambda i, *_: (i, 0)
    in_specs = [
        pl.BlockSpec((DISP_TILE, D_MODEL), row),
        pl.BlockSpec((None, None, 6, D_MODEL), lambda i, *_: (l, _cond_of_tile(i, DISP_TILE), 0, 0)),
        pl.BlockSpec((DISP_TILE, TOP_K), row),
        pl.BlockSpec((None, DISP_TILE, TOP_K), lambda i, *_: (i, 0, 0)),
    ]
    args = [x, mods, w, lpos_t]
    if final:
        in_specs.append(pl.BlockSpec((1, D_MODEL), lambda i, *_: (0, 0)))
        args.append(g_final.reshape(1, D_MODEL))
    in_specs.append(pl.BlockSpec(memory_space=pl.ANY))
    args.append(ys)
    out_specs, out_shape = _out_specs(final, DISP_TILE)
    return pl.pallas_call(
        functools.partial(_combine_top2_kernel, final, len(out_specs)),
        grid_spec=pltpu.PrefetchScalarGridSpec(
            num_scalar_prefetch=3,
            grid=(N_TILES,),
            in_specs=in_specs,
            out_specs=out_specs,
            scratch_shapes=[
                pltpu.VMEM((2, SORT_ROWS, D_MODEL), _F32),
                pltpu.SemaphoreType.DMA((2,)),
            ],
        ),
        out_shape=out_shape,
        compiler_params=_params(("arbitrary",)),
        name="combine_top2%s" % ("_final" if final else ""),
    )(*seg, *args)


def kernel(x_prompt, x_sample, cache_k, cache_v, c, c_ctx, w_ada, b_ada, g_norm_mix, g_norm_ffn,
           w_in, w_out, attn_sink, conv_dw, conv_dw_b, conv_ln_g, conv_ln_b, conv_pw, fnet_w,
           ffn_w_gate, ffn_w_up, ffn_w_down, moe_router, moe_w_gate, moe_w_up, moe_w_down, g_final):
    xs = (x_prompt.reshape(T_PROMPT, D_MODEL), x_sample.reshape(T_SAMPLE, D_MODEL))
    cond8 = jnp.concatenate([c_ctx[None, :], c, jnp.zeros((N_COND - 1 - DEC_BATCH, D_MODEL), _F32)], axis=0)
    mods = _modulation(cond8, w_ada, b_ada).reshape(DEPTH, N_COND, 6, D_MODEL)
    cos_t, sin_t = _rope_tables()
    ck_all = cache_k.reshape(DEC_BATCH, DEPTH, PAST_LEN, KV_W)
    cv_all = cache_v.reshape(DEC_BATCH, DEPTH, PAST_LEN, KV_W)
    p_blocks = T_PROMPT // DEC_SEQ
    sinks = attn_sink.reshape(DEPTH * N_HEADS)

    ks, vs = [], []
    resid = None
    for l in range(DEPTH):
        q, k, v, uc, uf, *x_new = _inproj(xs, l, mods, g_norm_mix, w_in, cos_t, sin_t, resid=resid)
        if resid is not None:
            xs, resid = tuple(x_new), None
        ks.append(k[:T_PROMPT])
        vs.append(v[:T_PROMPT])
        attn = (_ctx_attention(l, sinks, q, k, v),
                _lat_attention(l, sinks, q, k, v, ck_all, cv_all))
        cargs = (l, conv_dw, conv_dw_b, conv_ln_g, conv_ln_b, conv_pw)
        conv = (_conv_module(uc, SEQ, BATCH, 0, *cargs),
                _conv_module(uc, DEC_SEQ, DEC_BATCH, p_blocks, *cargs))
        four = (_fourier_mix(uf, SEQ, BATCH, 0, l, fnet_w),
                _fourier_mix(uf, DEC_SEQ, DEC_BATCH, T_PROMPT // 2, l, fnet_w))
        last = g_final if l == DEPTH - 1 else None
        i = l // 2
        if l % 2 == 0:
            x, h = _outproj((attn, conv, four), xs, l, mods, w_out, g_norm_ffn)
            y = _grouped_ffn(h, ffn_w_gate[i:i + 1], ffn_w_up[i:i + 1], ffn_w_down[i:i + 1], 256,
                             *_dense_visits())
            if last is None:
                xs, resid = (x,), (y, l)
            else:
                xs = tuple(_combine(x, l, mods, y, g_final=last))
        else:
            x, h, eidx, ew = _outproj((attn, conv, four), xs, l, mods, w_out, g_norm_ffn,
                                      router_t=moe_router[i].T)
            lpos, seg, pads, visits = _routing_plan(eidx)
            xd = _dispatch(h, lpos, seg, pads, visits[-1])
            ys = _grouped_ffn(xd, moe_w_gate[i], moe_w_up[i], moe_w_down[i], 512, *visits)
            xs = tuple(_combine_top2(x, l, mods, ys, lpos.transpose(0, 2, 1), seg, ew.T, g_final=last))

    y_prompt = xs[0].reshape(BATCH, SEQ, D_MODEL)
    y_sample = xs[1].reshape(DEC_BATCH, DEC_SEQ, D_MODEL)
    state_k = jnp.stack([a.reshape(BATCH, SEQ, N_KV_HEADS, HEAD_DIM) for a in ks], axis=1)
    state_v = jnp.stack([a.reshape(BATCH, SEQ, N_KV_HEADS, HEAD_DIM) for a in vs], axis=1)
    return (y_prompt, y_sample, state_k, state_v)
```

```python
import functools

import numpy as np
import jax
import jax.numpy as jnp
from jax import lax
from jax.experimental import pallas as pl
from jax.experimental.pallas import tpu as pltpu

D_MODEL = 1024
BATCH = 16
SEQ = 256
DEPTH = 2
DEC_BATCH = 2
DEC_SEQ = 2048
PAST_LEN = 512
GRID_W = 64
HEAD_DIM = 64
N_HEADS = 8
N_KV_HEADS = 2
GQA_GROUP = N_HEADS // N_KV_HEADS
ATTN_W = N_HEADS * HEAD_DIM
KV_W = N_KV_HEADS * HEAD_DIM
WINDOW = 128
BLOCK = 128
ROPE_THETA = 10000.0
CONV_W = D_MODEL // 4
CONV_K = 31
FNET_GROUPS = 4
FNET_W = D_MODEL // 4
FNET_GW = FNET_W // FNET_GROUPS
IN_WIDTH = ATTN_W + 2 * KV_W + 2 * CONV_W + FNET_W
D_FF = 2816
N_EXPERTS = 8
TOP_K = 2
D_FF_EXPERT = 3584
EPS = 1e-6
NEG = -1e30

T_PROMPT = BATCH * SEQ
T_SAMPLE = DEC_BATCH * DEC_SEQ
T = T_PROMPT + T_SAMPLE
N_COND = 8

ROW_TILE = 512
SUB = 64
SUB_MAX = 64
TMAX = SUB * SUB_MAX
BLK_MIN = 16
BLK_SUBS = 20
DISP_TILE = 512
N_TILES = T // DISP_TILE
PAIRS = TOP_K * DISP_TILE
SEG_ALIGN = 8
SORT_ROWS = 1152
CONV_CHUNK = 64
CONV_PAD = 16
CONV_SPAN = CONV_CHUNK + 8 * ((CONV_PAD + CONV_K // 2) // 8)
VMEM_LIMIT = 48 * 1024 * 1024
FFN_VMEM_LIMIT = 56 * 1024 * 1024

_BF = jnp.bfloat16
_F32 = jnp.float32


def _cond_of_tile(i, tile):
    r = i * tile
    return jnp.where(r < T_PROMPT, 0, 1 + (r - T_PROMPT) // DEC_SEQ)


def _params(sem, vmem=VMEM_LIMIT):
    return pltpu.CompilerParams(dimension_semantics=sem, vmem_limit_bytes=vmem)


def _sigmoid(x):
    return 1.0 / (1.0 + jnp.exp(-x))


def _mod_kernel(cond_ref, w_ref, b_ref, o_ref):
    cnd = cond_ref[...]
    s = (cnd * _sigmoid(cnd)).astype(_BF)
    o_ref[...] = jnp.dot(s, w_ref[...].astype(_BF), preferred_element_type=_F32) + b_ref[...]


def _modulation(cond8, w_ada, b_ada):
    nt = 1536
    return pl.pallas_call(
        _mod_kernel,
        grid=(DEPTH, 6 * D_MODEL // nt),
        in_specs=[
            pl.BlockSpec((N_COND, D_MODEL), lambda l, n: (0, 0)),
            pl.BlockSpec((None, D_MODEL, nt), lambda l, n: (l, 0, n)),
            pl.BlockSpec((None, 1, nt), lambda l, n: (l, 0, n)),
        ],
        out_specs=pl.BlockSpec((None, N_COND, nt), lambda l, n: (l, 0, n)),
        out_shape=jax.ShapeDtypeStruct((DEPTH, N_COND, 6 * D_MODEL), _F32),
        compiler_params=_params(("arbitrary", "arbitrary")),
        name="modulation",
    )(cond8, w_ada, b_ada.reshape(DEPTH, 1, 6 * D_MODEL))


def _norm_mod(x, g, shift, scale):
    ms = jnp.mean(x * x, axis=-1, keepdims=True)
    y = x * lax.rsqrt(ms + EPS) * g
    return y * (1.0 + scale) + shift


def _rope_tables():
    rows = DEC_SEQ // GRID_W
    n_freq = HEAD_DIM // 4
    inv = ROPE_THETA ** (-jnp.arange(n_freq, dtype=_F32) / n_freq)
    gr, gc = jnp.meshgrid(jnp.arange(rows, dtype=_F32), jnp.arange(GRID_W, dtype=_F32), indexing="ij")
    ang_r = gr.reshape(-1)[:, None] * inv
    ang_c = gc.reshape(-1)[:, None] * inv
    cr, sr, cc, sc = jnp.cos(ang_r), jnp.sin(ang_r), jnp.cos(ang_c), jnp.sin(ang_c)
    cos64 = jnp.concatenate([cr, cr, cc, cc], axis=-1)
    sin64 = jnp.concatenate([-sr, sr, -sc, sc], axis=-1)
    return jnp.tile(cos64, (1, 2)), jnp.tile(sin64, (1, 2))


def _rope128(x, cos, sin):
    lane = lax.broadcasted_iota(jnp.int32, x.shape, 1)
    first = (lane % 32) < 16
    partner = jnp.where(first, pltpu.roll(x, 128 - 16, 1), pltpu.roll(x, 16, 1))
    return x * cos + partner * sin


def _half_specs(width, tile):
    pt = T_PROMPT // tile
    return [pl.BlockSpec((tile, width), lambda i, *_: (jnp.minimum(i, pt - 1), 0)),
            pl.BlockSpec((tile, width), lambda i, *_: (jnp.maximum(i - pt, 0), 0))]


def _pick_half(i, tile, a_ref, b_ref):
    return jnp.where(i < T_PROMPT // tile, a_ref[...], b_ref[...])


def _layer_specs(l, w_shape):
    return [
        pl.BlockSpec((None, None, 6, D_MODEL), lambda i, *_: (l, _cond_of_tile(i, ROW_TILE), 0, 0)),
        pl.BlockSpec((None, 1, D_MODEL), lambda i, *_: (l, 0, 0)),
        pl.BlockSpec((None,) + w_shape, lambda i, *_: (l, 0, 0)),
    ]


def _inproj_kernel(split, resid, *refs):
    n_x = 2 if split else 1
    x_refs, refs = refs[:n_x], refs[n_x:]
    if resid:
        y_ref, modp_ref = refs[:2]
        refs = refs[2:]
    mod_ref, g_ref, w_ref, cos_ref, sin_ref = refs[:5]
    q_ref, kp_ref, vp_ref, ks_ref, vs_ref, uc_ref, uf_ref = refs[5:12]
    wb_ref, uf_scr = refs[-2:]
    i = pl.program_id(0)

    @pl.when(i == 0)
    def _():
        wb_ref[...] = w_ref[...].astype(_BF)

    x = _pick_half(i, ROW_TILE, *x_refs) if split else x_refs[0][...]
    if resid:
        x = x + modp_ref[5:6, :] * y_ref[...]
        refs[12][...] = x
    h = _norm_mod(x, g_ref[...], mod_ref[0:1, :], mod_ref[1:2, :]).astype(_BF)
    latent = i >= T_PROMPT // ROW_TILE
    cos = jnp.where(latent, cos_ref[...], 1.0)
    sin = jnp.where(latent, sin_ref[...], 0.0)

    def proj(o, n):
        return jnp.dot(h, wb_ref[:, o:o + n], preferred_element_type=_F32)

    o = 0
    for c in range(ATTN_W // 256):
        qq = proj(o, 256)
        q_ref[:, o:o + 128] = _rope128(qq[:, :128], cos, sin).astype(q_ref.dtype)
        q_ref[:, o + 128:o + 256] = _rope128(qq[:, 128:], cos, sin).astype(q_ref.dtype)
        o += 256
    kv = proj(o, 2 * KV_W)
    k = _rope128(kv[:, :KV_W], cos, sin)

    @pl.when(jnp.logical_not(latent))
    def _():
        kp_ref[...] = k
        vp_ref[...] = kv[:, KV_W:]

    @pl.when(latent)
    def _():
        ks_ref[...] = k
        vs_ref[...] = kv[:, KV_W:]
    o += 2 * KV_W
    uc_ref[...] = proj(o, 2 * CONV_W)
    o += 2 * CONV_W
    uf = proj(o, FNET_W)
    n_lane = FNET_W // 128
    for c in range(n_lane):
        uf_scr[c] = uf[:, c * 128:(c + 1) * 128]
    half = ROW_TILE // 2
    uf_ref[...] = jnp.concatenate([uf_scr[c, pl.ds(par, half, stride=2), :]
                                   for par in range(2) for c in range(n_lane)], axis=-1).astype(uf_ref.dtype)


def _inproj(xs, l, mods, g_all, w_all, cos_t, sin_t, resid=None):
    split = len(xs) == 2
    pt = T_PROMPT // ROW_TILE
    per_seq = DEC_SEQ // ROW_TILE

    def rope_idx(i):
        return (jnp.maximum(i - pt, 0) % per_seq, 0)

    row = lambda i: (i, 0)
    in_specs = _half_specs(D_MODEL, ROW_TILE) if split else [pl.BlockSpec((ROW_TILE, D_MODEL), row)]
    args = list(xs)
    kv_ctx, kv_lat = _half_specs(KV_W, ROW_TILE)
    out_specs = [pl.BlockSpec((ROW_TILE, ATTN_W), row), kv_ctx, kv_ctx, kv_lat, kv_lat,
                 pl.BlockSpec((ROW_TILE, 2 * CONV_W), row), pl.BlockSpec((ROW_TILE // 2, 2 * FNET_W), row)]
    out_shape = [jax.ShapeDtypeStruct((T, ATTN_W), _BF),
                 jax.ShapeDtypeStruct((T_PROMPT, KV_W), _F32), jax.ShapeDtypeStruct((T_PROMPT, KV_W), _F32),
                 jax.ShapeDtypeStruct((T_SAMPLE, KV_W), _F32), jax.ShapeDtypeStruct((T_SAMPLE, KV_W), _F32),
                 jax.ShapeDtypeStruct((T, 2 * CONV_W), _F32), jax.ShapeDtypeStruct((T // 2, 2 * FNET_W), _BF)]
    if resid is not None:
        y, l_prev = resid
        in_specs += [pl.BlockSpec((ROW_TILE, D_MODEL), row), _layer_specs(l_prev, ())[0]]
        args += [y, mods]
        out_specs.append(pl.BlockSpec((ROW_TILE, D_MODEL), row))
        out_shape.append(jax.ShapeDtypeStruct((T, D_MODEL), _F32))
    return pl.pallas_call(
        functools.partial(_inproj_kernel, split, resid is not None),
        grid=(T // ROW_TILE,),
        in_specs=in_specs + _layer_specs(l, (D_MODEL, IN_WIDTH)) + [
            pl.BlockSpec((ROW_TILE, 128), rope_idx),
            pl.BlockSpec((ROW_TILE, 128), rope_idx),
        ],
        out_specs=out_specs,
        out_shape=out_shape,
        scratch_shapes=[pltpu.VMEM((D_MODEL, IN_WIDTH), _BF), pltpu.VMEM((FNET_W // 128, ROW_TILE, 128), _F32)],
        compiler_params=_params(("arbitrary",)),
        name="inproj_resid" if resid is not None else "inproj",
    )(*args, mods, g_all.reshape(DEPTH, 1, D_MODEL), w_all, cos_t, sin_t)


assert 2 * HEAD_DIM == 128 and KV_W == 128 and GQA_GROUP == 4


def _kv_operands(x):
    return x.astype(_BF), pltpu.roll(x, HEAD_DIM, 1).astype(_BF)


def _half_mask(shape, par):
    return (lax.broadcasted_iota(jnp.int32, shape, 1) >= HEAD_DIM) == (par == 1)


def _stack_chunks(q, kh, rows):
    c0 = 2 * kh
    return jnp.concatenate([q[:, c0 * 128:(c0 + 1) * 128], q[:, (c0 + 1) * 128:(c0 + 2) * 128]], axis=0)


def _sink_column(sink_ref, head0, rows):
    r = lax.broadcasted_iota(jnp.int32, (2 * rows, 1), 0)
    return jnp.where(r >= rows, sink_ref[head0 + 2], sink_ref[head0])


def _store_chunks(o_ref, o, kh, rows):
    c0 = 2 * kh
    o_ref[:, c0 * 128:(c0 + 1) * 128] = o[:rows].astype(o_ref.dtype)
    o_ref[:, (c0 + 1) * 128:(c0 + 2) * 128] = o[rows:].astype(o_ref.dtype)


_NT = (((1,), (1,)), ((), ()))


def _ctx_attn_kernel(l, sink_ref, q_ref, k_ref, v_ref, o_ref):
    scale = HEAD_DIM ** -0.5
    q = q_ref[...] * scale
    k_ops = _kv_operands(k_ref[...])
    v_ops = _kv_operands(v_ref[...])
    for kh in range(N_KV_HEADS):
        qs = _stack_chunks(q, kh, SEQ).astype(_BF)
        o = None
        for par in range(2):
            own = _half_mask(qs.shape, par)
            which = 0 if kh == par else 1
            s = lax.dot_general(jnp.where(own, qs, 0), k_ops[which], _NT, preferred_element_type=_F32)
            sink = _sink_column(sink_ref, l * N_HEADS + kh * GQA_GROUP + par, SEQ)
            m = jnp.maximum(jnp.max(s, axis=-1, keepdims=True), sink)
            p = jnp.exp(s - m)
            den = jnp.sum(p, axis=-1, keepdims=True) + jnp.exp(sink - m)
            t = jnp.where(own, jnp.dot(p.astype(_BF), v_ops[which], preferred_element_type=_F32) / den, 0.0)
            o = t if o is None else o + t
        _store_chunks(o_ref, o, kh, SEQ)


def _ctx_attention(l, sinks, q, k, v):
    row = lambda b, s: (b, 0)
    return pl.pallas_call(
        functools.partial(_ctx_attn_kernel, l),
        grid_spec=pltpu.PrefetchScalarGridSpec(
            num_scalar_prefetch=1,
            grid=(BATCH,),
            in_specs=[
                pl.BlockSpec((SEQ, ATTN_W), row),
                pl.BlockSpec((SEQ, KV_W), row),
                pl.BlockSpec((SEQ, KV_W), row),
            ],
            out_specs=pl.BlockSpec((SEQ, ATTN_W), row),
        ),
        out_shape=jax.ShapeDtypeStruct((T_PROMPT, ATTN_W), _BF),
        compiler_params=_params(("arbitrary",)),
        name="ctx_attention",
    )(sinks, q, k, v)


def _lat_attn_kernel(l, sink_ref, q_ref, k_ref, v_ref, ck_ref, cv_ref, o_ref):
    n = pl.program_id(1)
    scale = HEAD_DIM ** -0.5
    nb = DEC_SEQ // BLOCK
    band = 3 * BLOCK
    start = pl.multiple_of(jnp.clip(n - 1, 0, nb - 3) * BLOCK, BLOCK)
    q = q_ref[...] * scale
    kb_ops = _kv_operands(k_ref[pl.ds(start, band), :])
    vb_ops = _kv_operands(v_ref[pl.ds(start, band), :])
    kc_ops = _kv_operands(ck_ref[...])
    vc_ops = _kv_operands(cv_ref[...])
    dpos = (lax.broadcasted_iota(jnp.int32, (BLOCK, band), 1) - lax.broadcasted_iota(jnp.int32, (BLOCK, band), 0)
            + (start - n * BLOCK))
    in_window = jnp.where(jnp.abs(dpos) <= WINDOW, 0.0, NEG)
    bias = jnp.concatenate([in_window] * 2, axis=0)
    for kh in range(N_KV_HEADS):
        qs = _stack_chunks(q, kh, BLOCK).astype(_BF)
        o = None
        for par in range(2):
            own = _half_mask(qs.shape, par)
            which = 0 if kh == par else 1
            qm = jnp.where(own, qs, 0)
            s_loc = lax.dot_general(qm, kb_ops[which], _NT, preferred_element_type=_F32)
            s_loc = jnp.where(bias < 0.0, NEG, s_loc)
            s_ctx = lax.dot_general(qm, kc_ops[which], _NT, preferred_element_type=_F32)
            sink = _sink_column(sink_ref, l * N_HEADS + kh * GQA_GROUP + par, BLOCK)
            m = jnp.maximum(jnp.maximum(jnp.max(s_loc, axis=-1, keepdims=True),
                                        jnp.max(s_ctx, axis=-1, keepdims=True)), sink)
            p_loc = jnp.exp(s_loc - m)
            p_ctx = jnp.exp(s_ctx - m)
            den = (jnp.sum(p_loc, axis=-1, keepdims=True) + jnp.sum(p_ctx, axis=-1, keepdims=True)
                   + jnp.exp(sink - m))
            t = (jnp.dot(p_loc.astype(_BF), vb_ops[which], preferred_element_type=_F32)
                 + jnp.dot(p_ctx.astype(_BF), vc_ops[which], preferred_element_type=_F32)) / den
            t = jnp.where(own, t, 0.0)
            o = t if o is None else o + t
        _store_chunks(o_ref, o, kh, BLOCK)


def _lat_attention(l, sinks, q, k, v, ck, cv):
    nb = DEC_SEQ // BLOCK
    q0 = T_PROMPT // BLOCK
    return pl.pallas_call(
        functools.partial(_lat_attn_kernel, l),
        grid_spec=pltpu.PrefetchScalarGridSpec(
            num_scalar_prefetch=1,
            grid=(DEC_BATCH, nb),
            in_specs=[
                pl.BlockSpec((BLOCK, ATTN_W), lambda b, n, s: (q0 + b * nb + n, 0)),
                pl.BlockSpec((DEC_SEQ, KV_W), lambda b, n, s: (b, 0)),
                pl.BlockSpec((DEC_SEQ, KV_W), lambda b, n, s: (b, 0)),
                pl.BlockSpec((None, None, PAST_LEN, KV_W), lambda b, n, s: (b, l, 0, 0)),
                pl.BlockSpec((None, None, PAST_LEN, KV_W), lambda b, n, s: (b, l, 0, 0)),
            ],
            out_specs=pl.BlockSpec((BLOCK, ATTN_W), lambda b, n, s: (b * nb + n, 0)),
        ),
        out_shape=jax.ShapeDtypeStruct((T_SAMPLE, ATTN_W), _BF),
        compiler_params=_params(("arbitrary", "arbitrary")),
        name="lat_attention",
    )(sinks, q, k, v, ck, cv)


def _conv_kernel(seq, u_ref, dw_ref, dwb_ref, lg_ref, lb_ref, pw_ref, o_ref, pad_ref, y_ref, sh_ref):
    u = u_ref[...]
    pad_ref[0:CONV_PAD, :] = jnp.zeros((CONV_PAD, CONV_W), _F32)
    pad_ref[CONV_PAD + seq:2 * CONV_PAD + seq, :] = jnp.zeros((CONV_PAD, CONV_W), _F32)
    pad_ref[CONV_PAD:CONV_PAD + seq, :] = u[:, :CONV_W] * _sigmoid(u[:, CONV_W:])
    off = CONV_PAD - CONV_K // 2
    span = CONV_SPAN

    def chunk(c, carry):
        row = pl.multiple_of(c * CONV_CHUNK, CONV_CHUNK)
        win = pad_ref[pl.ds(row, CONV_CHUNK + 2 * CONV_PAD), :]
        acc = jnp.zeros((CONV_CHUNK, CONV_W), _F32) + dwb_ref[...]
        for phase in range(8):
            sh_ref[phase] = win[phase:phase + span, :]
        for phase in range(8):
            for a in range(span // 8):
                t = 8 * a + phase - off
                if 0 <= t < CONV_K:
                    acc = acc + sh_ref[phase, 8 * a:8 * a + CONV_CHUNK, :] * dw_ref[t:t + 1, :]
        mu = jnp.mean(acc, axis=-1, keepdims=True)
        d = acc - mu
        var = jnp.mean(d * d, axis=-1, keepdims=True)
        y = d * lax.rsqrt(var + EPS) * lg_ref[...] + lb_ref[...]
        y_ref[pl.ds(row, CONV_CHUNK), :] = (y * _sigmoid(y)).astype(_BF)
        return carry

    lax.fori_loop(0, seq // CONV_CHUNK, chunk, 0)
    o_ref[...] = jnp.dot(y_ref[...], pw_ref[...].astype(_BF), preferred_element_type=_F32).astype(o_ref.dtype)


def _conv_module(uc, seq, nbatch, block0, l, dw, dwb, lg, lb, pw):
    vec = lambda a: a.reshape(DEPTH, 1, CONV_W)
    layer = lambda b: (l, 0, 0)
    return pl.pallas_call(
        functools.partial(_conv_kernel, seq),
        grid=(nbatch,),
        in_specs=[
            pl.BlockSpec((seq, 2 * CONV_W), lambda b: (block0 + b, 0)),
            pl.BlockSpec((None, CONV_K, CONV_W), layer),
            pl.BlockSpec((None, 1, CONV_W), layer),
            pl.BlockSpec((None, 1, CONV_W), layer),
            pl.BlockSpec((None, 1, CONV_W), layer),
            pl.BlockSpec((None, CONV_W, CONV_W), layer),
        ],
        out_specs=pl.BlockSpec((seq, CONV_W), lambda b: (b, 0)),
        out_shape=jax.ShapeDtypeStruct((nbatch * seq, CONV_W), _BF),
        scratch_shapes=[pltpu.VMEM((seq + 2 * CONV_PAD, CONV_W), _F32), pltpu.VMEM((seq, CONV_W), _BF),
                        pltpu.VMEM((8, CONV_SPAN, CONV_W), _F32)],
        compiler_params=_params(("arbitrary",)),
        name="conv_module_%d" % seq,
    )(uc, dw, vec(dwb), vec(lg), vec(lb), pw)


def _dft_constants(seq):
    half = seq // 2
    j = np.arange(half, dtype=np.int64)
    ang = 2.0 * np.pi * ((j[:, None] * j[None, :]) % half).astype(np.float64) / half
    m = np.concatenate([np.cos(ang), -np.sin(ang)], axis=1) / np.sqrt(seq)
    tw = np.pi * j.astype(np.float64) / half
    cb = np.repeat(np.cos(tw)[:, None], FNET_W, axis=1)
    sb = np.repeat(np.sin(tw)[:, None], FNET_W, axis=1)
    c = np.arange(FNET_GW, dtype=np.int64)
    angc = 2.0 * np.pi * ((c[:, None] * c[None, :]) % FNET_GW).astype(np.float64) / FNET_GW
    eye = np.eye(FNET_GROUPS)
    w1 = np.concatenate([np.kron(eye, np.cos(angc)), np.kron(eye, np.sin(angc))], axis=1) / np.sqrt(FNET_GW)
    return tuple(jnp.asarray(a, dtype=_F32) for a in (m, cb, sb, w1))


def _fnet_kernel(half, group, u_ref, m_ref, cb_ref, sb_ref, w1_ref, fw_ref, o_ref, rhs_ref):
    w1 = w1_ref[...].astype(_BF)
    te = jnp.dot(u_ref[:, :FNET_W], w1, preferred_element_type=_F32)
    to = jnp.dot(u_ref[:, FNET_W:], w1, preferred_element_type=_F32)
    for g in range(group):
        rows = slice(g * half, (g + 1) * half)
        ec, es = te[rows, :FNET_W], te[rows, FNET_W:]
        oc, os_ = to[rows, :FNET_W], to[rows, FNET_W:]
        for col, (top, bot) in enumerate(((ec, es), (oc, os_), (os_, -oc))):
            lanes = slice((3 * g + col) * FNET_W, (3 * g + col + 1) * FNET_W)
            rhs_ref[0:half, lanes] = top.astype(_BF)
            rhs_ref[half:2 * half, lanes] = bot.astype(_BF)
    pqr = jnp.dot(m_ref[...].astype(_BF), rhs_ref[...], preferred_element_type=_F32)
    cb, sb = cb_ref[...], sb_ref[...]
    ys = []
    for g in range(group):
        p, q, r = (pqr[:, (3 * g + col) * FNET_W:(3 * g + col + 1) * FNET_W] for col in range(3))
        rot = cb * q - sb * r
        ys += [(p + rot).astype(_BF), (p - rot).astype(_BF)]
    o_ref[...] = jnp.dot(jnp.concatenate(ys, axis=0), fw_ref[...].astype(_BF),
                         preferred_element_type=_F32).astype(o_ref.dtype)


def _fourier_mix(u_pairs, seq, nbatch, row0, l, fw):
    half = seq // 2
    group = max(1, 512 // half)
    rows = group * half
    m, cb, sb, w1 = _dft_constants(seq)
    const = lambda b: (0, 0)
    return pl.pallas_call(
        functools.partial(_fnet_kernel, half, group),
        grid=(nbatch // group,),
        in_specs=[
            pl.BlockSpec((rows, 2 * FNET_W), lambda b: (row0 // rows + b, 0)),
            pl.BlockSpec((half, 2 * half), const),
            pl.BlockSpec((half, FNET_W), const),
            pl.BlockSpec((half, FNET_W), const),
            pl.BlockSpec((FNET_W, 2 * FNET_W), const),
            pl.BlockSpec((None, FNET_W, FNET_W), lambda b: (l, 0, 0)),
        ],
        out_specs=pl.BlockSpec((2 * rows, FNET_W), lambda b: (b, 0)),
        out_shape=jax.ShapeDtypeStruct((nbatch * seq, FNET_W), _BF),
        scratch_shapes=[pltpu.VMEM((2 * half, 3 * group * FNET_W), _BF)],
        compiler_params=_params(("arbitrary",)),
        name="fourier_mix_%d" % seq,
    )(u_pairs, m, cb, sb, w1, fw)


def _route_top2(r_ref, h, ei_ref, ew_ref):
    lg = lax.dot_general(r_ref[...].astype(_BF), h.astype(_BF), _NT, preferred_element_type=_F32)
    eid = lax.broadcasted_iota(jnp.int32, lg.shape, 0)
    m1 = jnp.max(lg, axis=0, keepdims=True)
    i1 = jnp.min(jnp.where(lg == m1, eid, N_EXPERTS), axis=0, keepdims=True)
    lg2 = jnp.where(eid == i1, -jnp.inf, lg)
    m2 = jnp.max(lg2, axis=0, keepdims=True)
    i2 = jnp.min(jnp.where(lg2 == m2, eid, N_EXPERTS), axis=0, keepdims=True)
    e = jnp.exp(m2 - m1)
    ei_ref[0:1, :] = i1
    ei_ref[1:2, :] = i2
    ew_ref[0:1, :] = 1.0 / (1.0 + e)
    ew_ref[1:2, :] = e / (1.0 + e)


def _outproj_kernel(split, route, *refs):
    branch_refs, refs = refs[:6], refs[6:]
    n_x = 2 if split else 1
    x_refs, refs = refs[:n_x], refs[n_x:]
    mod_ref, w_ref, gf_ref = refs[:3]
    refs = refs[3:]
    if route:
        r_ref, xo_ref, h_ref, ei_ref, ew_ref, wb_ref = refs
    else:
        xo_ref, h_ref, wb_ref = refs
    i = pl.program_id(0)

    @pl.when(i == 0)
    def _():
        wb_ref[...] = w_ref[...].astype(_BF)

    mix = jnp.concatenate([_pick_half(i, ROW_TILE, branch_refs[2 * n], branch_refs[2 * n + 1])
                           for n in range(3)], axis=-1)
    x = _pick_half(i, ROW_TILE, *x_refs) if split else x_refs[0][...]
    x_new = x + mod_ref[2:3, :] * jnp.dot(mix, wb_ref[...], preferred_element_type=_F32)
    xo_ref[...] = x_new
    h = _norm_mod(x_new, gf_ref[...], mod_ref[3:4, :], mod_ref[4:5, :])
    h_ref[...] = h.astype(h_ref.dtype)
    if route:
        _route_top2(r_ref, h, ei_ref, ew_ref)


def _outproj(branches, xs, l, mods, w_all, gf_all, router_t=None):
    split = len(xs) == 2
    route = router_t is not None
    row = lambda i: (i, 0)
    in_specs = []
    args = []
    for pair, width in zip(branches, (ATTN_W, CONV_W, FNET_W)):
        in_specs += _half_specs(width, ROW_TILE)
        args += list(pair)
    in_specs += _half_specs(D_MODEL, ROW_TILE) if split else [pl.BlockSpec((ROW_TILE, D_MODEL), row)]
    args += list(xs)
    mod_spec, gf_spec, w_spec = _layer_specs(l, (D_MODEL, D_MODEL))
    in_specs += [mod_spec, w_spec, gf_spec]
    args += [mods, w_all, gf_all.reshape(DEPTH, 1, D_MODEL)]
    out_specs = [pl.BlockSpec((ROW_TILE, D_MODEL), row)] * 2
    out_shape = [jax.ShapeDtypeStruct((T, D_MODEL), _F32), jax.ShapeDtypeStruct((T, D_MODEL), _BF)]
    if route:
        in_specs.append(pl.BlockSpec((N_EXPERTS, D_MODEL), lambda i: (0, 0)))
        args.append(router_t)
        out_specs += [pl.BlockSpec((TOP_K, ROW_TILE), lambda i: (0, i))] * 2
        out_shape += [jax.ShapeDtypeStruct((TOP_K, T), jnp.int32), jax.ShapeDtypeStruct((TOP_K, T), _F32)]
    return pl.pallas_call(
        functools.partial(_outproj_kernel, split, route),
        grid=(T // ROW_TILE,),
        in_specs=in_specs,
        out_specs=out_specs,
        out_shape=out_shape,
        scratch_shapes=[pltpu.VMEM((D_MODEL, D_MODEL), _BF)],
        compiler_params=_params(("arbitrary",)),
        name="outproj_route" if route else "outproj",
    )(*args)


def _ffn_kernel(nj, n_sub_rows, vis_e, vis_start, vis_cnt, used_sub,
                x_hbm, wg_ref, wu_ref, wd_ref, y_hbm,
                big, xb, wgb, wub, wdb, sem_in, sem_out):
    del vis_e
    v = pl.program_id(0)
    j = pl.program_id(1)
    cnt = vis_cnt[v]
    row0 = vis_start[v] * SUB

    x_is_bf16 = x_hbm.dtype == _BF
    landing = xb if x_is_bf16 else big

    def copy_in(s):
        r = pl.multiple_of(row0 + s * SUB, SUB)
        b = pl.multiple_of(s * SUB, SUB)
        return pltpu.make_async_copy(x_hbm.at[pl.ds(r, SUB)], landing.at[pl.ds(b, SUB)], sem_in.at[s])

    def copy_out(s):
        r = pl.multiple_of(row0 + s * SUB, SUB)
        b = pl.multiple_of(s * SUB, SUB)
        return pltpu.make_async_copy(big.at[pl.ds(b, SUB)], y_hbm.at[pl.ds(r, SUB)], sem_out)

    def for_subs(fn):
        lax.fori_loop(0, cnt, lambda s, c: (fn(s), c)[1], 0)

    @pl.when(jnp.logical_and(j == 0, cnt > 0))
    def _():
        for_subs(lambda s: copy_in(s).start())

    @pl.when(cnt > 0)
    def _():
        wgb[...] = wg_ref[...].astype(_BF)
        wub[...] = wu_ref[...].astype(_BF)
        wdb[...] = wd_ref[...].astype(_BF)

        def block(r0, nrows):
            s0 = r0 // SUB

            @pl.when(j == 0)
            def _():
                for t in range(nrows // SUB):
                    copy_in(s0 + t).wait()
                    rs = pl.ds(pl.multiple_of(r0 + t * SUB, SUB), SUB)
                    if not x_is_bf16:
                        xb[rs, :] = big[rs, :].astype(_BF)
                    big[rs, :] = jnp.zeros((SUB, D_MODEL), _F32)

            rs = pl.ds(r0, nrows)
            x = xb[rs, :]
            a = jnp.dot(x, wgb[...], preferred_element_type=_F32)
            b = jnp.dot(x, wub[...], preferred_element_type=_F32)
            p = (a * _sigmoid(a) * b).astype(_BF)
            big[rs, :] += jnp.dot(p, wdb[...], preferred_element_type=_F32)

            @pl.when(j == nj - 1)
            def _():
                for t in range(nrows // SUB):
                    copy_out(s0 + t).start()

        n_big = cnt // BLK_MIN
        rem = cnt - n_big * BLK_MIN
        room = BLK_SUBS - BLK_MIN
        grow = jnp.minimum(rem, n_big * room)
        left = rem - grow

        def one_block(i, r0):
            size = BLK_MIN + jnp.clip(grow - i * room, 0, room)
            for sz in range(BLK_MIN, BLK_SUBS + 1):
                pl.when(size == sz)(functools.partial(block, pl.multiple_of(r0, SUB), sz * SUB))
            return r0 + size * SUB
        r0 = lax.fori_loop(0, n_big, one_block, 0)
        bit = BLK_MIN // 2
        while bit >= 1:
            take = (left // bit) % 2 == 1
            pl.when(take)(functools.partial(block, pl.multiple_of(r0, SUB), bit * SUB))
            r0 = r0 + jnp.where(take, bit * SUB, 0)
            bit //= 2

    @pl.when(jnp.logical_and(j == nj - 1, cnt > 0))
    def _():
        for_subs(lambda s: copy_out(s).wait())

    if x_hbm.dtype == y_hbm.dtype:
        @pl.when(jnp.logical_and(v == pl.num_programs(0) - 1, j == nj - 1))
        def _():
            def tail(s):
                r = pl.multiple_of(s * SUB, SUB)
                return pltpu.make_async_copy(x_hbm.at[pl.ds(r, SUB)], y_hbm.at[pl.ds(r, SUB)], sem_out)
            lax.fori_loop(used_sub[0], n_sub_rows, lambda s, c: (tail(s).start(), c)[1], 0)
            lax.fori_loop(used_sub[0], n_sub_rows, lambda s, c: (tail(s).wait(), c)[1], 0)


def _grouped_ffn(x, wg, wu, wd, fc, vis_e, vis_start, vis_cnt, used_sub):
    ff = wg.shape[-1]
    nj = ff // fc
    nv = vis_e.shape[0]
    n_rows = x.shape[0]

    def chunk(v, j, vc):
        return jnp.where(vc[v] > 0, j, nj - 1)

    return pl.pallas_call(
        functools.partial(_ffn_kernel, nj, n_rows // SUB),
        grid_spec=pltpu.PrefetchScalarGridSpec(
            num_scalar_prefetch=4,
            grid=(nv, nj),
            in_specs=[
                pl.BlockSpec(memory_space=pl.ANY),
                pl.BlockSpec((None, D_MODEL, fc), lambda v, j, ve, vs, vc, us: (ve[v], 0, chunk(v, j, vc))),
                pl.BlockSpec((None, D_MODEL, fc), lambda v, j, ve, vs, vc, us: (ve[v], 0, chunk(v, j, vc))),
                pl.BlockSpec((None, fc, D_MODEL), lambda v, j, ve, vs, vc, us: (ve[v], chunk(v, j, vc), 0)),
            ],
            out_specs=pl.BlockSpec(memory_space=pl.ANY),
            scratch_shapes=[
                pltpu.VMEM((TMAX, D_MODEL), _F32),
                pltpu.VMEM((TMAX, D_MODEL), _BF),
                pltpu.VMEM((D_MODEL, fc), _BF),
                pltpu.VMEM((D_MODEL, fc), _BF),
                pltpu.VMEM((fc, D_MODEL), _BF),
                pltpu.SemaphoreType.DMA((SUB_MAX,)),
                pltpu.SemaphoreType.DMA(()),
            ],
        ),
        out_shape=jax.ShapeDtypeStruct((n_rows, D_MODEL), _F32),
        compiler_params=_params(("arbitrary", "arbitrary"), FFN_VMEM_LIMIT),
        name="ffn_%d" % ff,
    )(vis_e, vis_start, vis_cnt, used_sub, x, wg, wu, wd)


def _dense_visits():
    nv = T // TMAX
    return (jnp.zeros((nv,), jnp.int32),
            jnp.arange(nv, dtype=jnp.int32) * SUB_MAX,
            jnp.full((nv,), SUB_MAX, jnp.int32),
            jnp.full((1,), T // SUB, jnp.int32))


N_SLOT_SUB = -(-(TOP_K * T + N_TILES * N_EXPERTS * (SEG_ALIGN - 1)) // SUB) + N_EXPERTS
N_SLOT = N_SLOT_SUB * SUB
N_VISIT = -(-N_SLOT_SUB // SUB_MAX) + N_EXPERTS


def _routing_plan(eidx):
    e_loc = eidx.reshape(TOP_K, N_TILES, DISP_TILE).transpose(1, 0, 2).reshape(N_TILES, PAIRS)
    onehot = e_loc[:, :, None] == jnp.arange(N_EXPERTS, dtype=jnp.int32)[None, None, :]
    ch = 128
    oh = onehot.astype(_F32).reshape(N_TILES, PAIRS // ch, ch, N_EXPERTS)
    tri = (jnp.arange(ch)[:, None] >= jnp.arange(ch)[None, :]).astype(_F32)
    within = jnp.einsum("ij,tcjk->tcik", tri, oh)
    tot = within[:, :, -1, :]
    csum = (within + (jnp.cumsum(tot, axis=1) - tot)[:, :, None, :]).reshape(N_TILES, PAIRS, N_EXPERTS)
    csum = csum.astype(jnp.int32)
    n_te = (csum[:, -1, :] + SEG_ALIGN - 1) // SEG_ALIGN * SEG_ALIGN
    src = jnp.cumsum(n_te, axis=1) - n_te
    lpos = jnp.sum(jnp.where(onehot, csum - 1 + src[:, None, :], 0), axis=2).astype(jnp.int32)
    counts = jnp.sum(n_te, axis=0)
    nsub = (counts + SUB - 1) // SUB
    sub_base = jnp.cumsum(nsub) - nsub
    dst = (sub_base * SUB)[None, :] + jnp.cumsum(n_te, axis=0) - n_te
    seg = tuple(a.reshape(-1).astype(jnp.int32) for a in (n_te, src, dst))
    pads = ((sub_base * SUB + counts).astype(jnp.int32), (nsub * SUB - counts).astype(jnp.int32))
    used_sub = jnp.sum(nsub).reshape(1).astype(jnp.int32)
    nvis = (nsub + SUB_MAX - 1) // SUB_MAX
    vend = jnp.cumsum(nvis)
    total = vend[-1]
    vid = jnp.arange(N_VISIT, dtype=jnp.int32)
    ve = jnp.minimum(jnp.sum((vid[:, None] >= vend[None, :]).astype(jnp.int32), axis=1), N_EXPERTS - 1)
    local = vid - (vend - nvis)[ve]
    nv_e = jnp.maximum(nvis[ve], 1)
    q, r = nsub[ve] // nv_e, nsub[ve] % nv_e
    cnt = q + (local < r).astype(jnp.int32)
    start = sub_base[ve] + local * q + jnp.minimum(local, r)
    used = vid < total
    last_e = ve[jnp.maximum(total - 1, 0)]
    vis_e = jnp.where(used, ve, last_e).astype(jnp.int32)
    vis_cnt = jnp.where(used, cnt, 0).astype(jnp.int32)
    vis_start = jnp.where(used, start, 0).astype(jnp.int32)
    return lpos.reshape(N_TILES, TOP_K, DISP_TILE), seg, pads, (vis_e, vis_start, vis_cnt, used_sub)


def _for_pow2_pieces(n, max_piece, fn):
    off = 0
    bit = max_piece
    while bit >= SEG_ALIGN:
        take = (n // bit) % 2 == 1
        pl.when(take)(functools.partial(fn, off, bit))
        off = off + jnp.where(take, bit, 0)
        bit //= 2


def _segment_copies(tile, seg, make_copy, wait=False):
    n_te, src, dst = seg
    for e in range(N_EXPERTS):
        j = tile * N_EXPERTS + e
        s0, d0 = src[j], dst[j]

        def piece(off, size, s0=s0, d0=d0):
            cp = make_copy(pl.multiple_of(s0 + off, SEG_ALIGN), pl.multiple_of(d0 + off, SEG_ALIGN), size)
            cp.wait() if wait else cp.start()
        _for_pow2_pieces(n_te[j], DISP_TILE, piece)


def _dispatch_kernel(n_te, src, dst, pad_start, pad_cnt, used_sub, h_ref, lpos_ref, xs_hbm, ring, ring_sems, sem):
    i = pl.program_id(0)
    last = pl.num_programs(0) - 1
    b = i % 2

    def copies(tile, slot_id, wait):
        _segment_copies(tile, (n_te, src, dst), lambda s, d, size: pltpu.make_async_copy(
            ring.at[slot_id, pl.ds(s, size)], xs_hbm.at[pl.ds(d, size)], ring_sems.at[slot_id]), wait)

    pl.when(i >= 2)(lambda: copies(i - 2, b, True))
    lp = lpos_ref[...]
    srow = lax.broadcasted_iota(jnp.int32, (SORT_ROWS, DISP_TILE), 0)
    perm = jnp.where(lp[0:1, :] == srow, 1.0, jnp.where(lp[1:2, :] == srow, 1.0, 0.0)).astype(_BF)
    ring[b] = jnp.dot(perm, h_ref[...].astype(_BF), preferred_element_type=_F32)
    copies(i, b, False)

    @pl.when(i == last)
    def _():
        copies(i - 1, 1 - b, True)
        copies(i, b, True)
        for e in range(N_EXPERTS):
            p0 = pad_start[e]

            def fill(off, size, p0=p0):
                cp = pltpu.make_async_copy(ring.at[b, pl.ds(0, size)],
                                           xs_hbm.at[pl.ds(pl.multiple_of(p0 + off, SEG_ALIGN), size)], sem)
                cp.start()
                cp.wait()
            _for_pow2_pieces(pad_cnt[e], SUB // 2, fill)

        def tail(s):
            r = pl.multiple_of(s * SUB, SUB)
            return pltpu.make_async_copy(ring.at[b, pl.ds(0, SUB)], xs_hbm.at[pl.ds(r, SUB)], sem)
        lax.fori_loop(used_sub[0], N_SLOT_SUB, lambda s, c: (tail(s).start(), c)[1], 0)
        lax.fori_loop(used_sub[0], N_SLOT_SUB, lambda s, c: (tail(s).wait(), c)[1], 0)


def _dispatch(h, lpos, seg, pads, used_sub):
    return pl.pallas_call(
        _dispatch_kernel,
        grid_spec=pltpu.PrefetchScalarGridSpec(
            num_scalar_prefetch=6,
            grid=(N_TILES,),
            in_specs=[pl.BlockSpec((DISP_TILE, D_MODEL), lambda i, *_: (i, 0)),
                      pl.BlockSpec((None, TOP_K, DISP_TILE), lambda i, *_: (i, 0, 0))],
            out_specs=pl.BlockSpec(memory_space=pl.ANY),
            scratch_shapes=[pltpu.VMEM((2, SORT_ROWS, D_MODEL), _F32),
                            pltpu.SemaphoreType.DMA((2,)), pltpu.SemaphoreType.DMA(())],
        ),
        out_shape=jax.ShapeDtypeStruct((N_SLOT, D_MODEL), _F32),
        compiler_params=_params(("arbitrary",)),
        name="dispatch",
    )(*seg, *pads, used_sub, h, lpos)


def _residual_out(x, gate, f, g_ref, o_refs, i, tile):
    out = x + gate * f
    if g_ref is not None:
        ms = jnp.mean(out * out, axis=-1, keepdims=True)
        out = out * lax.rsqrt(ms + EPS) * g_ref[...]
    if len(o_refs) == 1:
        o_refs[0][...] = out
    else:
        @pl.when(i < T_PROMPT // tile)
        def _():
            o_refs[0][...] = out

        @pl.when(i >= T_PROMPT // tile)
        def _():
            o_refs[1][...] = out


def _out_specs(final, tile):
    if not final:
        return ([pl.BlockSpec((tile, D_MODEL), lambda i, *_: (i, 0))],
                [jax.ShapeDtypeStruct((T, D_MODEL), _F32)])
    return (_half_specs(D_MODEL, tile),
            [jax.ShapeDtypeStruct((T_PROMPT, D_MODEL), _F32), jax.ShapeDtypeStruct((T_SAMPLE, D_MODEL), _F32)])


def _combine_kernel(final, x_ref, mod_ref, y_ref, *rest):
    g_ref, o_refs = (rest[0], rest[1:]) if final else (None, rest)
    _residual_out(x_ref[...], mod_ref[5:6, :], y_ref[...], g_ref, o_refs, pl.program_id(0), ROW_TILE)


def _combine(x, l, mods, y, g_final=None):
    final = g_final is not None
    row = lambda i: (i, 0)
    in_specs = [
        pl.BlockSpec((ROW_TILE, D_MODEL), row),
        _layer_specs(l, ())[0],
        pl.BlockSpec((ROW_TILE, D_MODEL), row),
    ]
    args = [x, mods, y]
    if final:
        in_specs.append(pl.BlockSpec((1, D_MODEL), lambda i: (0, 0)))
        args.append(g_final.reshape(1, D_MODEL))
    out_specs, out_shape = _out_specs(final, ROW_TILE)
    return pl.pallas_call(
        functools.partial(_combine_kernel, final),
        grid=(T // ROW_TILE,),
        in_specs=in_specs,
        out_specs=out_specs,
        out_shape=out_shape,
        compiler_params=_params(("arbitrary",)),
        name="combine%s" % ("_final" if final else ""),
    )(*args)


def _combine_top2_kernel(final, n_out, n_te, src, dst, x_ref, mod_ref, w_ref, lpos_ref, *rest):
    g_ref, rest = (rest[0], rest[1:]) if final else (None, rest)
    ys_hbm, o_refs, (ybuf, sems) = rest[0], rest[1:1 + n_out], rest[1 + n_out:]
    i = pl.program_id(0)

    def fetch(tile, b, wait=False):
        _segment_copies(tile, (n_te, src, dst), lambda s, d, size: pltpu.make_async_copy(
            ys_hbm.at[pl.ds(d, size)], ybuf.at[b, pl.ds(s, size)], sems.at[b]), wait)

    @pl.when(i == 0)
    def _():
        for slot_id in range(2):
            ybuf[slot_id, PAIRS:SORT_ROWS, :] = jnp.zeros((SORT_ROWS - PAIRS, D_MODEL), _F32)
        fetch(0, 0)

    @pl.when(i + 1 < pl.num_programs(0))
    def _():
        fetch(i + 1, (i + 1) % 2)

    b = i % 2
    fetch(i, b, wait=True)
    y = ybuf[b].astype(_BF)
    lp = lpos_ref[...]
    scol = lax.broadcasted_iota(jnp.int32, (DISP_TILE, SORT_ROWS), 1)
    w = w_ref[...]
    pick = jnp.zeros((DISP_TILE, SORT_ROWS), _F32)
    for k in range(TOP_K):
        pick = jnp.where(lp[:, k:k + 1] == scol, w[:, k:k + 1], pick)
    f = jnp.dot(pick.astype(_BF), y, preferred_element_type=_F32)
    _residual_out(x_ref[...], mod_ref[5:6, :], f, g_ref, o_refs, i, DISP_TILE)


def _combine_top2(x, l, mods, ys, lpos_t, seg, w, g_final=None):
    final = g_final is not None
    row = lambda i, *_: (i, 0)
    in_specs = [
        pl.BlockSpec((DISP_TILE, D_MODEL), row),
        pl.BlockSpec((None, None, 6, D_MODEL), lambda i, *_: (l, _cond_of_tile(i, DISP_TILE), 0, 0)),
        pl.BlockSpec((DISP_TILE, TOP_K), row),
        pl.BlockSpec((None, DISP_TILE, TOP_K), lambda i, *_: (i, 0, 0)),
    ]
    args = [x, mods, w, lpos_t]
    if final:
        in_specs.append(pl.BlockSpec((1, D_MODEL), lambda i, *_: (0, 0)))
        args.append(g_final.reshape(1, D_MODEL))
    in_specs.append(pl.BlockSpec(memory_space=pl.ANY))
    args.append(ys)
    out_specs, out_shape = _out_specs(final, DISP_TILE)
    return pl.pallas_call(
        functools.partial(_combine_top2_kernel, final, len(out_specs)),
        grid_spec=pltpu.PrefetchScalarGridSpec(
            num_scalar_prefetch=3,
            grid=(N_TILES,),
            in_specs=in_specs,
            out_specs=out_specs,
            scratch_shapes=[
                pltpu.VMEM((2, SORT_ROWS, D_MODEL), _F32),
                pltpu.SemaphoreType.DMA((2,)),
            ],
        ),
        out_shape=out_shape,
        compiler_params=_params(("arbitrary",)),
        name="combine_top2%s" % ("_final" if final else ""),
    )(*seg, *args)


def kernel(x_prompt, x_sample, cache_k, cache_v, c, c_ctx, w_ada, b_ada, g_norm_mix, g_norm_ffn,
           w_in, w_out, attn_sink, conv_dw, conv_dw_b, conv_ln_g, conv_ln_b, conv_pw, fnet_w,
           ffn_w_gate, ffn_w_up, ffn_w_down, moe_router, moe_w_gate, moe_w_up, moe_w_down, g_final):
    xs = (x_prompt.reshape(T_PROMPT, D_MODEL), x_sample.reshape(T_SAMPLE, D_MODEL))
    cond8 = jnp.concatenate([c_ctx[None, :], c, jnp.zeros((N_COND - 1 - DEC_BATCH, D_MODEL), _F32)], axis=0)
    mods = _modulation(cond8, w_ada, b_ada).reshape(DEPTH, N_COND, 6, D_MODEL)
    cos_t, sin_t = _rope_tables()
    ck_all = cache_k.reshape(DEC_BATCH, DEPTH, PAST_LEN, KV_W)
    cv_all = cache_v.reshape(DEC_BATCH, DEPTH, PAST_LEN, KV_W)
    p_blocks = T_PROMPT // DEC_SEQ
    sinks = attn_sink.reshape(DEPTH * N_HEADS)

    ks, vs = [], []
    resid = None
    for l in range(DEPTH):
        q, kp, vp, kl, vl, uc, uf, *x_new = _inproj(xs, l, mods, g_norm_mix, w_in, cos_t, sin_t, resid=resid)
        if resid is not None:
            xs, resid = tuple(x_new), None
        ks.append(kp)
        vs.append(vp)
        attn = (_ctx_attention(l, sinks, q, kp, vp),
                _lat_attention(l, sinks, q, kl, vl, ck_all, cv_all))
        cargs = (l, conv_dw, conv_dw_b, conv_ln_g, conv_ln_b, conv_pw)
        conv = (_conv_module(uc, SEQ, BATCH, 0, *cargs),
                _conv_module(uc, DEC_SEQ, DEC_BATCH, p_blocks, *cargs))
        four = (_fourier_mix(uf, SEQ, BATCH, 0, l, fnet_w),
                _fourier_mix(uf, DEC_SEQ, DEC_BATCH, T_PROMPT // 2, l, fnet_w))
        last = g_final if l == DEPTH - 1 else None
        i = l // 2
        if l % 2 == 0:
            x, h = _outproj((attn, conv, four), xs, l, mods, w_out, g_norm_ffn)
            y = _grouped_ffn(h, ffn_w_gate[i:i + 1], ffn_w_up[i:i + 1], ffn_w_down[i:i + 1], 256,
                             *_dense_visits())
            if last is None:
                xs, resid = (x,), (y, l)
            else:
                xs = tuple(_combine(x, l, mods, y, g_final=last))
        else:
            x, h, eidx, ew = _outproj((attn, conv, four), xs, l, mods, w_out, g_norm_ffn,
                                      router_t=moe_router[i].T)
            lpos, seg, pads, visits = _routing_plan(eidx)
            xd = _dispatch(h, lpos, seg, pads, visits[-1])
            ys = _grouped_ffn(xd, moe_w_gate[i], moe_w_up[i], moe_w_down[i], 512, *visits)
            xs = tuple(_combine_top2(x, l, mods, ys, lpos.transpose(0, 2, 1), seg, ew.T, g_final=last))

    y_prompt = xs[0].reshape(BATCH, SEQ, D_MODEL)
    y_sample = xs[1].reshape(DEC_BATCH, DEC_SEQ, D_MODEL)
    state_k = jnp.stack([a.reshape(BATCH, SEQ, N_KV_HEADS, HEAD_DIM) for a in ks], axis=1)
    state_v = jnp.stack([a.reshape(BATCH, SEQ, N_KV_HEADS, HEAD_DIM) for a in vs], axis=1)
    return (y_prompt, y_sample, state_k, state_v)
```

```python
import functools

import numpy as np
import jax
import jax.numpy as jnp
from jax import lax
from jax.experimental import pallas as pl
from jax.experimental.pallas import tpu as pltpu

D_MODEL = 1024
BATCH = 16
SEQ = 256
DEPTH = 2
DEC_BATCH = 2
DEC_SEQ = 2048
PAST_LEN = 512
GRID_W = 64
HEAD_DIM = 64
N_HEADS = 8
N_KV_HEADS = 2
GQA_GROUP = N_HEADS // N_KV_HEADS
ATTN_W = N_HEADS * HEAD_DIM
KV_W = N_KV_HEADS * HEAD_DIM
WINDOW = 128
BLOCK = 128
ROPE_THETA = 10000.0
CONV_W = D_MODEL // 4
CONV_K = 31
FNET_GROUPS = 4
FNET_W = D_MODEL // 4
FNET_GW = FNET_W // FNET_GROUPS
IN_WIDTH = ATTN_W + 2 * KV_W + 2 * CONV_W + FNET_W
D_FF = 2816
N_EXPERTS = 8
TOP_K = 2
D_FF_EXPERT = 3584
EPS = 1e-6
NEG = -1e30

T_PROMPT = BATCH * SEQ
T_SAMPLE = DEC_BATCH * DEC_SEQ
T = T_PROMPT + T_SAMPLE
N_COND = 8

ROW_TILE = 512
SUB = 64
SUB_MAX = 64
TMAX = SUB * SUB_MAX
BLK_MIN = 16
BLK_SUBS = 20
DISP_TILE = 512
N_TILES = T // DISP_TILE
PAIRS = TOP_K * DISP_TILE
SEG_ALIGN = 8
SORT_ROWS = 1152
CONV_CHUNK = 64
CONV_PAD = 16
CONV_SPAN = CONV_CHUNK + 8 * ((CONV_PAD + CONV_K // 2) // 8)
VMEM_LIMIT = 48 * 1024 * 1024
FFN_VMEM_LIMIT = 56 * 1024 * 1024

_BF = jnp.bfloat16
_F32 = jnp.float32


def _cond_of_tile(i, tile):
    r = i * tile
    return jnp.where(r < T_PROMPT, 0, 1 + (r - T_PROMPT) // DEC_SEQ)


def _params(sem, vmem=VMEM_LIMIT):
    return pltpu.CompilerParams(dimension_semantics=sem, vmem_limit_bytes=vmem)


def _sigmoid(x):
    return 1.0 / (1.0 + jnp.exp(-x))


def _mod_kernel(cond_ref, w_ref, b_ref, o_ref):
    cnd = cond_ref[...]
    s = (cnd * _sigmoid(cnd)).astype(_BF)
    o_ref[...] = jnp.dot(s, w_ref[...].astype(_BF), preferred_element_type=_F32) + b_ref[...]


def _modulation(cond8, w_ada, b_ada):
    nt = 1536
    return pl.pallas_call(
        _mod_kernel,
        grid=(DEPTH, 6 * D_MODEL // nt),
        in_specs=[
            pl.BlockSpec((N_COND, D_MODEL), lambda l, n: (0, 0)),
            pl.BlockSpec((None, D_MODEL, nt), lambda l, n: (l, 0, n)),
            pl.BlockSpec((None, 1, nt), lambda l, n: (l, 0, n)),
        ],
        out_specs=pl.BlockSpec((None, N_COND, nt), lambda l, n: (l, 0, n)),
        out_shape=jax.ShapeDtypeStruct((DEPTH, N_COND, 6 * D_MODEL), _F32),
        compiler_params=_params(("arbitrary", "arbitrary")),
        name="modulation",
    )(cond8, w_ada, b_ada.reshape(DEPTH, 1, 6 * D_MODEL))


def _norm_mod(x, g, shift, scale):
    ms = jnp.mean(x * x, axis=-1, keepdims=True)
    y = x * lax.rsqrt(ms + EPS) * g
    return y * (1.0 + scale) + shift


def _rope_tables():
    rows = DEC_SEQ // GRID_W
    n_freq = HEAD_DIM // 4
    inv = ROPE_THETA ** (-jnp.arange(n_freq, dtype=_F32) / n_freq)
    gr, gc = jnp.meshgrid(jnp.arange(rows, dtype=_F32), jnp.arange(GRID_W, dtype=_F32), indexing="ij")
    ang_r = gr.reshape(-1)[:, None] * inv
    ang_c = gc.reshape(-1)[:, None] * inv
    cr, sr, cc, sc = jnp.cos(ang_r), jnp.sin(ang_r), jnp.cos(ang_c), jnp.sin(ang_c)
    cos64 = jnp.concatenate([cr, cr, cc, cc], axis=-1)
    sin64 = jnp.concatenate([-sr, sr, -sc, sc], axis=-1)
    return jnp.tile(cos64, (1, 2)), jnp.tile(sin64, (1, 2))


def _rope128(x, cos, sin):
    lane = lax.broadcasted_iota(jnp.int32, x.shape, 1)
    first = (lane % 32) < 16
    partner = jnp.where(first, pltpu.roll(x, 128 - 16, 1), pltpu.roll(x, 16, 1))
    return x * cos + partner * sin


def _half_specs(width, tile):
    pt = T_PROMPT // tile
    return [pl.BlockSpec((tile, width), lambda i, *_: (jnp.minimum(i, pt - 1), 0)),
            pl.BlockSpec((tile, width), lambda i, *_: (jnp.maximum(i - pt, 0), 0))]


def _pick_half(i, tile, a_ref, b_ref):
    return jnp.where(i < T_PROMPT // tile, a_ref[...], b_ref[...])


def _layer_specs(l, w_shape):
    return [
        pl.BlockSpec((None, None, 6, D_MODEL), lambda i, *_: (l, _cond_of_tile(i, ROW_TILE), 0, 0)),
        pl.BlockSpec((None, 1, D_MODEL), lambda i, *_: (l, 0, 0)),
        pl.BlockSpec((None,) + w_shape, lambda i, *_: (l, 0, 0)),
    ]


def _inproj_kernel(split, resid, *refs):
    n_x = 2 if split else 1
    x_refs, refs = refs[:n_x], refs[n_x:]
    if resid:
        y_ref, modp_ref = refs[:2]
        refs = refs[2:]
    mod_ref, g_ref, w_ref, cos_ref, sin_ref = refs[:5]
    q_ref, kp_ref, vp_ref, ks_ref, vs_ref, uc_ref, uf_ref = refs[5:12]
    wb_ref, uf_scr = refs[-2:]
    i = pl.program_id(0)

    @pl.when(i == 0)
    def _():
        wb_ref[...] = w_ref[...].astype(_BF)

    x = _pick_half(i, ROW_TILE, *x_refs) if split else x_refs[0][...]
    if resid:
        x = x + modp_ref[5:6, :] * y_ref[...]
        refs[12][...] = x
    h = _norm_mod(x, g_ref[...], mod_ref[0:1, :], mod_ref[1:2, :]).astype(_BF)
    latent = i >= T_PROMPT // ROW_TILE
    cos = jnp.where(latent, cos_ref[...], 1.0)
    sin = jnp.where(latent, sin_ref[...], 0.0)

    def proj(o, n):
        return jnp.dot(h, wb_ref[:, o:o + n], preferred_element_type=_F32)

    o = 0
    for c in range(ATTN_W // 256):
        qq = proj(o, 256)
        q_ref[:, o:o + 128] = _rope128(qq[:, :128], cos, sin).astype(q_ref.dtype)
        q_ref[:, o + 128:o + 256] = _rope128(qq[:, 128:], cos, sin).astype(q_ref.dtype)
        o += 256
    kv = proj(o, 2 * KV_W)
    k = _rope128(kv[:, :KV_W], cos, sin)

    @pl.when(jnp.logical_not(latent))
    def _():
        kp_ref[...] = k
        vp_ref[...] = kv[:, KV_W:]

    @pl.when(latent)
    def _():
        ks_ref[...] = k
        vs_ref[...] = kv[:, KV_W:]
    o += 2 * KV_W
    uc_ref[...] = proj(o, 2 * CONV_W)
    o += 2 * CONV_W
    uf = proj(o, FNET_W)
    n_lane = FNET_W // 128
    for c in range(n_lane):
        uf_scr[c] = uf[:, c * 128:(c + 1) * 128]
    half = ROW_TILE // 2
    uf_ref[...] = jnp.concatenate([uf_scr[c, pl.ds(par, half, stride=2), :]
                                   for par in range(2) for c in range(n_lane)], axis=-1).astype(uf_ref.dtype)


def _inproj(xs, l, mods, g_all, w_all, cos_t, sin_t, resid=None):
    split = len(xs) == 2
    pt = T_PROMPT // ROW_TILE
    per_seq = DEC_SEQ // ROW_TILE

    def rope_idx(i):
        return (jnp.maximum(i - pt, 0) % per_seq, 0)

    row = lambda i: (i, 0)
    in_specs = _half_specs(D_MODEL, ROW_TILE) if split else [pl.BlockSpec((ROW_TILE, D_MODEL), row)]
    args = list(xs)
    kv_ctx, kv_lat = _half_specs(KV_W, ROW_TILE)
    out_specs = [pl.BlockSpec((ROW_TILE, ATTN_W), row), kv_ctx, kv_ctx, kv_lat, kv_lat,
                 pl.BlockSpec((ROW_TILE, 2 * CONV_W), row), pl.BlockSpec((ROW_TILE // 2, 2 * FNET_W), row)]
    out_shape = [jax.ShapeDtypeStruct((T, ATTN_W), _BF),
                 jax.ShapeDtypeStruct((T_PROMPT, KV_W), _F32), jax.ShapeDtypeStruct((T_PROMPT, KV_W), _F32),
                 jax.ShapeDtypeStruct((T_SAMPLE, KV_W), _F32), jax.ShapeDtypeStruct((T_SAMPLE, KV_W), _F32),
                 jax.ShapeDtypeStruct((T, 2 * CONV_W), _F32), jax.ShapeDtypeStruct((T // 2, 2 * FNET_W), _BF)]
    if resid is not None:
        y, l_prev = resid
        in_specs += [pl.BlockSpec((ROW_TILE, D_MODEL), row), _layer_specs(l_prev, ())[0]]
        args += [y, mods]
        out_specs.append(pl.BlockSpec((ROW_TILE, D_MODEL), row))
        out_shape.append(jax.ShapeDtypeStruct((T, D_MODEL), _F32))
    return pl.pallas_call(
        functools.partial(_inproj_kernel, split, resid is not None),
        grid=(T // ROW_TILE,),
        in_specs=in_specs + _layer_specs(l, (D_MODEL, IN_WIDTH)) + [
            pl.BlockSpec((ROW_TILE, 128), rope_idx),
            pl.BlockSpec((ROW_TILE, 128), rope_idx),
        ],
        out_specs=out_specs,
        out_shape=out_shape,
        scratch_shapes=[pltpu.VMEM((D_MODEL, IN_WIDTH), _BF), pltpu.VMEM((FNET_W // 128, ROW_TILE, 128), _F32)],
        compiler_params=_params(("arbitrary",)),
        name="inproj_resid" if resid is not None else "inproj",
    )(*args, mods, g_all.reshape(DEPTH, 1, D_MODEL), w_all, cos_t, sin_t)


assert 2 * HEAD_DIM == 128 and KV_W == 128 and GQA_GROUP == 4


def _kv_operands(x):
    return x.astype(_BF), pltpu.roll(x, HEAD_DIM, 1).astype(_BF)


def _half_mask(shape, par):
    return (lax.broadcasted_iota(jnp.int32, shape, 1) >= HEAD_DIM) == (par == 1)


def _stack_chunks(q, kh, rows):
    c0 = 2 * kh
    return jnp.concatenate([q[:, c0 * 128:(c0 + 1) * 128], q[:, (c0 + 1) * 128:(c0 + 2) * 128]], axis=0)


def _sink_column(sink_ref, heads, rows):
    r = lax.broadcasted_iota(jnp.int32, (len(heads) * rows, 1), 0)
    col = jnp.full((len(heads) * rows, 1), sink_ref[heads[0]], _F32)
    for g in range(1, len(heads)):
        col = jnp.where(r >= g * rows, sink_ref[heads[g]], col)
    return col


def _stack_heads(q, kh, rows):
    parts = [q[:, (kh * GQA_GROUP + g) * HEAD_DIM:(kh * GQA_GROUP + g + 1) * HEAD_DIM] for g in range(GQA_GROUP)]
    return jnp.concatenate(parts, axis=0)


def _store_heads(o_ref, o, kh, rows):
    for pair in range(GQA_GROUP // 2):
        a = o[(2 * pair) * rows:(2 * pair + 1) * rows]
        b = o[(2 * pair + 1) * rows:(2 * pair + 2) * rows]
        c0 = (kh * GQA_GROUP + 2 * pair) * HEAD_DIM
        o_ref[:, c0:c0 + 2 * HEAD_DIM] = jnp.concatenate([a, b], axis=-1).astype(o_ref.dtype)


def _store_chunks(o_ref, o, kh, rows):
    c0 = 2 * kh
    o_ref[:, c0 * 128:(c0 + 1) * 128] = o[:rows].astype(o_ref.dtype)
    o_ref[:, (c0 + 1) * 128:(c0 + 2) * 128] = o[rows:].astype(o_ref.dtype)


_NT = (((1,), (1,)), ((), ()))


def _ctx_attn_kernel(l, sink_ref, q_ref, k_ref, v_ref, o_ref):
    scale = HEAD_DIM ** -0.5
    q = q_ref[...] * scale
    k_ops = _kv_operands(k_ref[...])
    v_ops = _kv_operands(v_ref[...])
    for kh in range(N_KV_HEADS):
        qs = _stack_chunks(q, kh, SEQ).astype(_BF)
        o = None
        for par in range(2):
            own = _half_mask(qs.shape, par)
            which = 0 if kh == par else 1
            s = lax.dot_general(jnp.where(own, qs, 0), k_ops[which], _NT, preferred_element_type=_F32)
            h0 = l * N_HEADS + kh * GQA_GROUP + par
            sink = _sink_column(sink_ref, (h0, h0 + 2), SEQ)
            m = jnp.maximum(jnp.max(s, axis=-1, keepdims=True), sink)
            p = jnp.exp(s - m)
            den = jnp.sum(p, axis=-1, keepdims=True) + jnp.exp(sink - m)
            t = jnp.where(own, jnp.dot(p.astype(_BF), v_ops[which], preferred_element_type=_F32) / den, 0.0)
            o = t if o is None else o + t
        _store_chunks(o_ref, o, kh, SEQ)


def _ctx_attention(l, sinks, q, k, v):
    row = lambda b, s: (b, 0)
    return pl.pallas_call(
        functools.partial(_ctx_attn_kernel, l),
        grid_spec=pltpu.PrefetchScalarGridSpec(
            num_scalar_prefetch=1,
            grid=(BATCH,),
            in_specs=[
                pl.BlockSpec((SEQ, ATTN_W), row),
                pl.BlockSpec((SEQ, KV_W), row),
                pl.BlockSpec((SEQ, KV_W), row),
            ],
            out_specs=pl.BlockSpec((SEQ, ATTN_W), row),
        ),
        out_shape=jax.ShapeDtypeStruct((T_PROMPT, ATTN_W), _BF),
        compiler_params=_params(("arbitrary",)),
        name="ctx_attention",
    )(sinks, q, k, v)


def _lat_attn_kernel(l, sink_ref, q_ref, k_ref, v_ref, ck_ref, cv_ref, o_ref):
    n = pl.program_id(1)
    scale = HEAD_DIM ** -0.5
    nb = DEC_SEQ // BLOCK
    band = 3 * BLOCK
    start = pl.multiple_of(jnp.clip(n - 1, 0, nb - 3) * BLOCK, BLOCK)
    q = q_ref[...] * scale
    kb = k_ref[pl.ds(start, band), :]
    vb = v_ref[pl.ds(start, band), :]
    dpos = (lax.broadcasted_iota(jnp.int32, (BLOCK, band), 1) - lax.broadcasted_iota(jnp.int32, (BLOCK, band), 0)
            + (start - n * BLOCK))
    in_window = jnp.where(jnp.abs(dpos) <= WINDOW, 0.0, NEG)
    bias = jnp.concatenate([in_window] * GQA_GROUP, axis=0)
    for kh in range(N_KV_HEADS):
        hs = slice(kh * HEAD_DIM, (kh + 1) * HEAD_DIM)
        qs = _stack_heads(q, kh, BLOCK).astype(_BF)
        s_loc = lax.dot_general(qs, kb[:, hs].astype(_BF), _NT, preferred_element_type=_F32)
        s_loc = jnp.where(bias < 0.0, NEG, s_loc)
        s_ctx = lax.dot_general(qs, ck_ref[:, hs].astype(_BF), _NT, preferred_element_type=_F32)
        h0 = l * N_HEADS + kh * GQA_GROUP
        sink = _sink_column(sink_ref, tuple(range(h0, h0 + GQA_GROUP)), BLOCK)
        m = jnp.maximum(jnp.maximum(jnp.max(s_loc, axis=-1, keepdims=True),
                                    jnp.max(s_ctx, axis=-1, keepdims=True)), sink)
        p_loc = jnp.exp(s_loc - m)
        p_ctx = jnp.exp(s_ctx - m)
        den = (jnp.sum(p_loc, axis=-1, keepdims=True) + jnp.sum(p_ctx, axis=-1, keepdims=True)
               + jnp.exp(sink - m))
        o = (jnp.dot(p_loc.astype(_BF), vb[:, hs].astype(_BF), preferred_element_type=_F32)
             + jnp.dot(p_ctx.astype(_BF), cv_ref[:, hs].astype(_BF), preferred_element_type=_F32)) / den
        _store_heads(o_ref, o, kh, BLOCK)


def _lat_attention(l, sinks, q, k, v, ck, cv):
    nb = DEC_SEQ // BLOCK
    q0 = T_PROMPT // BLOCK
    return pl.pallas_call(
        functools.partial(_lat_attn_kernel, l),
        grid_spec=pltpu.PrefetchScalarGridSpec(
            num_scalar_prefetch=1,
            grid=(DEC_BATCH, nb),
            in_specs=[
                pl.BlockSpec((BLOCK, ATTN_W), lambda b, n, s: (q0 + b * nb + n, 0)),
                pl.BlockSpec((DEC_SEQ, KV_W), lambda b, n, s: (b, 0)),
                pl.BlockSpec((DEC_SEQ, KV_W), lambda b, n, s: (b, 0)),
                pl.BlockSpec((None, None, PAST_LEN, KV_W), lambda b, n, s: (b, l, 0, 0)),
                pl.BlockSpec((None, None, PAST_LEN, KV_W), lambda b, n, s: (b, l, 0, 0)),
            ],
            out_specs=pl.BlockSpec((BLOCK, ATTN_W), lambda b, n, s: (b * nb + n, 0)),
        ),
        out_shape=jax.ShapeDtypeStruct((T_SAMPLE, ATTN_W), _BF),
        compiler_params=_params(("arbitrary", "arbitrary")),
        name="lat_attention",
    )(sinks, q, k, v, ck, cv)


def _conv_kernel(seq, u_ref, dw_ref, dwb_ref, lg_ref, lb_ref, pw_ref, o_ref, pad_ref, y_ref, sh_ref):
    u = u_ref[...]
    pad_ref[0:CONV_PAD, :] = jnp.zeros((CONV_PAD, CONV_W), _F32)
    pad_ref[CONV_PAD + seq:2 * CONV_PAD + seq, :] = jnp.zeros((CONV_PAD, CONV_W), _F32)
    pad_ref[CONV_PAD:CONV_PAD + seq, :] = u[:, :CONV_W] * _sigmoid(u[:, CONV_W:])
    off = CONV_PAD - CONV_K // 2
    span = CONV_SPAN

    def chunk(c, carry):
        row = pl.multiple_of(c * CONV_CHUNK, CONV_CHUNK)
        win = pad_ref[pl.ds(row, CONV_CHUNK + 2 * CONV_PAD), :]
        acc = jnp.zeros((CONV_CHUNK, CONV_W), _F32) + dwb_ref[...]
        for phase in range(8):
            sh_ref[phase] = win[phase:phase + span, :]
        for phase in range(8):
            for a in range(span // 8):
                t = 8 * a + phase - off
                if 0 <= t < CONV_K:
                    acc = acc + sh_ref[phase, 8 * a:8 * a + CONV_CHUNK, :] * dw_ref[t:t + 1, :]
        mu = jnp.mean(acc, axis=-1, keepdims=True)
        d = acc - mu
        var = jnp.mean(d * d, axis=-1, keepdims=True)
        y = d * lax.rsqrt(var + EPS) * lg_ref[...] + lb_ref[...]
        y_ref[pl.ds(row, CONV_CHUNK), :] = (y * _sigmoid(y)).astype(_BF)
        return carry

    lax.fori_loop(0, seq // CONV_CHUNK, chunk, 0)
    o_ref[...] = jnp.dot(y_ref[...], pw_ref[...].astype(_BF), preferred_element_type=_F32).astype(o_ref.dtype)


def _conv_module(uc, seq, nbatch, block0, l, dw, dwb, lg, lb, pw):
    vec = lambda a: a.reshape(DEPTH, 1, CONV_W)
    layer = lambda b: (l, 0, 0)
    return pl.pallas_call(
        functools.partial(_conv_kernel, seq),
        grid=(nbatch,),
        in_specs=[
            pl.BlockSpec((seq, 2 * CONV_W), lambda b: (block0 + b, 0)),
            pl.BlockSpec((None, CONV_K, CONV_W), layer),
            pl.BlockSpec((None, 1, CONV_W), layer),
            pl.BlockSpec((None, 1, CONV_W), layer),
            pl.BlockSpec((None, 1, CONV_W), layer),
            pl.BlockSpec((None, CONV_W, CONV_W), layer),
        ],
        out_specs=pl.BlockSpec((seq, CONV_W), lambda b: (b, 0)),
        out_shape=jax.ShapeDtypeStruct((nbatch * seq, CONV_W), _BF),
        scratch_shapes=[pltpu.VMEM((seq + 2 * CONV_PAD, CONV_W), _F32), pltpu.VMEM((seq, CONV_W), _BF),
                        pltpu.VMEM((8, CONV_SPAN, CONV_W), _F32)],
        compiler_params=_params(("arbitrary",)),
        name="conv_module_%d" % seq,
    )(uc, dw, vec(dwb), vec(lg), vec(lb), pw)


def _dft_constants(seq):
    half = seq // 2
    j = np.arange(half, dtype=np.int64)
    ang = 2.0 * np.pi * ((j[:, None] * j[None, :]) % half).astype(np.float64) / half
    m = np.concatenate([np.cos(ang), -np.sin(ang)], axis=1) / np.sqrt(seq)
    tw = np.pi * j.astype(np.float64) / half
    cb = np.repeat(np.cos(tw)[:, None], FNET_W, axis=1)
    sb = np.repeat(np.sin(tw)[:, None], FNET_W, axis=1)
    c = np.arange(FNET_GW, dtype=np.int64)
    angc = 2.0 * np.pi * ((c[:, None] * c[None, :]) % FNET_GW).astype(np.float64) / FNET_GW
    eye = np.eye(FNET_GROUPS)
    w1 = np.concatenate([np.kron(eye, np.cos(angc)), np.kron(eye, np.sin(angc))], axis=1) / np.sqrt(FNET_GW)
    return tuple(jnp.asarray(a, dtype=_F32) for a in (m, cb, sb, w1))


def _fnet_kernel(half, group, u_ref, m_ref, cb_ref, sb_ref, w1_ref, fw_ref, o_ref, rhs_ref):
    w1 = w1_ref[...].astype(_BF)
    te = jnp.dot(u_ref[:, :FNET_W], w1, preferred_element_type=_F32)
    to = jnp.dot(u_ref[:, FNET_W:], w1, preferred_element_type=_F32)
    for g in range(group):
        rows = slice(g * half, (g + 1) * half)
        ec, es = te[rows, :FNET_W], te[rows, FNET_W:]
        oc, os_ = to[rows, :FNET_W], to[rows, FNET_W:]
        for col, (top, bot) in enumerate(((ec, es), (oc, os_), (os_, -oc))):
            lanes = slice((3 * g + col) * FNET_W, (3 * g + col + 1) * FNET_W)
            rhs_ref[0:half, lanes] = top.astype(_BF)
            rhs_ref[half:2 * half, lanes] = bot.astype(_BF)
    pqr = jnp.dot(m_ref[...].astype(_BF), rhs_ref[...], preferred_element_type=_F32)
    cb, sb = cb_ref[...], sb_ref[...]
    ys = []
    for g in range(group):
        p, q, r = (pqr[:, (3 * g + col) * FNET_W:(3 * g + col + 1) * FNET_W] for col in range(3))
        rot = cb * q - sb * r
        ys += [(p + rot).astype(_BF), (p - rot).astype(_BF)]
    o_ref[...] = jnp.dot(jnp.concatenate(ys, axis=0), fw_ref[...].astype(_BF),
                         preferred_element_type=_F32).astype(o_ref.dtype)


def _fourier_mix(u_pairs, seq, nbatch, row0, l, fw):
    half = seq // 2
    group = max(1, 512 // half)
    rows = group * half
    m, cb, sb, w1 = _dft_constants(seq)
    const = lambda b: (0, 0)
    return pl.pallas_call(
        functools.partial(_fnet_kernel, half, group),
        grid=(nbatch // group,),
        in_specs=[
            pl.BlockSpec((rows, 2 * FNET_W), lambda b: (row0 // rows + b, 0)),
            pl.BlockSpec((half, 2 * half), const),
            pl.BlockSpec((half, FNET_W), const),
            pl.BlockSpec((half, FNET_W), const),
            pl.BlockSpec((FNET_W, 2 * FNET_W), const),
            pl.BlockSpec((None, FNET_W, FNET_W), lambda b: (l, 0, 0)),
        ],
        out_specs=pl.BlockSpec((2 * rows, FNET_W), lambda b: (b, 0)),
        out_shape=jax.ShapeDtypeStruct((nbatch * seq, FNET_W), _BF),
        scratch_shapes=[pltpu.VMEM((2 * half, 3 * group * FNET_W), _BF)],
        compiler_params=_params(("arbitrary",)),
        name="fourier_mix_%d" % seq,
    )(u_pairs, m, cb, sb, w1, fw)


def _route_top2(r_ref, h, ei_ref, ew_ref):
    lg = lax.dot_general(r_ref[...].astype(_BF), h.astype(_BF), _NT, preferred_element_type=_F32)
    eid = lax.broadcasted_iota(jnp.int32, lg.shape, 0)
    m1 = jnp.max(lg, axis=0, keepdims=True)
    i1 = jnp.min(jnp.where(lg == m1, eid, N_EXPERTS), axis=0, keepdims=True)
    lg2 = jnp.where(eid == i1, -jnp.inf, lg)
    m2 = jnp.max(lg2, axis=0, keepdims=True)
    i2 = jnp.min(jnp.where(lg2 == m2, eid, N_EXPERTS), axis=0, keepdims=True)
    e = jnp.exp(m2 - m1)
    ei_ref[0:1, :] = i1
    ei_ref[1:2, :] = i2
    ew_ref[0:1, :] = 1.0 / (1.0 + e)
    ew_ref[1:2, :] = e / (1.0 + e)


def _outproj_kernel(split, route, *refs):
    branch_refs, refs = refs[:6], refs[6:]
    n_x = 2 if split else 1
    x_refs, refs = refs[:n_x], refs[n_x:]
    mod_ref, w_ref, gf_ref = refs[:3]
    refs = refs[3:]
    if route:
        r_ref, xo_ref, h_ref, ei_ref, ew_ref, wb_ref = refs
    else:
        xo_ref, h_ref, wb_ref = refs
    i = pl.program_id(0)

    @pl.when(i == 0)
    def _():
        wb_ref[...] = w_ref[...].astype(_BF)

    mix = jnp.concatenate([_pick_half(i, ROW_TILE, branch_refs[2 * n], branch_refs[2 * n + 1])
                           for n in range(3)], axis=-1)
    x = _pick_half(i, ROW_TILE, *x_refs) if split else x_refs[0][...]
    x_new = x + mod_ref[2:3, :] * jnp.dot(mix, wb_ref[...], preferred_element_type=_F32)
    xo_ref[...] = x_new
    h = _norm_mod(x_new, gf_ref[...], mod_ref[3:4, :], mod_ref[4:5, :])
    h_ref[...] = h.astype(h_ref.dtype)
    if route:
        _route_top2(r_ref, h, ei_ref, ew_ref)


def _outproj(branches, xs, l, mods, w_all, gf_all, router_t=None):
    split = len(xs) == 2
    route = router_t is not None
    row = lambda i: (i, 0)
    in_specs = []
    args = []
    for pair, width in zip(branches, (ATTN_W, CONV_W, FNET_W)):
        in_specs += _half_specs(width, ROW_TILE)
        args += list(pair)
    in_specs += _half_specs(D_MODEL, ROW_TILE) if split else [pl.BlockSpec((ROW_TILE, D_MODEL), row)]
    args += list(xs)
    mod_spec, gf_spec, w_spec = _layer_specs(l, (D_MODEL, D_MODEL))
    in_specs += [mod_spec, w_spec, gf_spec]
    args += [mods, w_all, gf_all.reshape(DEPTH, 1, D_MODEL)]
    out_specs = [pl.BlockSpec((ROW_TILE, D_MODEL), row)] * 2
    out_shape = [jax.ShapeDtypeStruct((T, D_MODEL), _F32), jax.ShapeDtypeStruct((T, D_MODEL), _BF)]
    if route:
        in_specs.append(pl.BlockSpec((N_EXPERTS, D_MODEL), lambda i: (0, 0)))
        args.append(router_t)
        out_specs += [pl.BlockSpec((TOP_K, ROW_TILE), lambda i: (0, i))] * 2
        out_shape += [jax.ShapeDtypeStruct((TOP_K, T), jnp.int32), jax.ShapeDtypeStruct((TOP_K, T), _F32)]
    return pl.pallas_call(
        functools.partial(_outproj_kernel, split, route),
        grid=(T // ROW_TILE,),
        in_specs=in_specs,
        out_specs=out_specs,
        out_shape=out_shape,
        scratch_shapes=[pltpu.VMEM((D_MODEL, D_MODEL), _BF)],
        compiler_params=_params(("arbitrary",)),
        name="outproj_route" if route else "outproj",
    )(*args)


def _ffn_kernel(nj, n_sub_rows, vis_e, vis_start, vis_cnt, used_sub,
                x_hbm, wg_ref, wu_ref, wd_ref, y_hbm,
                big, xb, wgb, wub, wdb, sem_in, sem_out):
    del vis_e
    v = pl.program_id(0)
    j = pl.program_id(1)
    cnt = vis_cnt[v]
    row0 = vis_start[v] * SUB

    x_is_bf16 = x_hbm.dtype == _BF
    landing = xb if x_is_bf16 else big

    def copy_in(s):
        r = pl.multiple_of(row0 + s * SUB, SUB)
        b = pl.multiple_of(s * SUB, SUB)
        return pltpu.make_async_copy(x_hbm.at[pl.ds(r, SUB)], landing.at[pl.ds(b, SUB)], sem_in.at[s])

    def copy_out(s):
        r = pl.multiple_of(row0 + s * SUB, SUB)
        b = pl.multiple_of(s * SUB, SUB)
        return pltpu.make_async_copy(big.at[pl.ds(b, SUB)], y_hbm.at[pl.ds(r, SUB)], sem_out)

    def for_subs(fn):
        lax.fori_loop(0, cnt, lambda s, c: (fn(s), c)[1], 0)

    @pl.when(jnp.logical_and(j == 0, cnt > 0))
    def _():
        for_subs(lambda s: copy_in(s).start())

    @pl.when(cnt > 0)
    def _():
        wgb[...] = wg_ref[...].astype(_BF)
        wub[...] = wu_ref[...].astype(_BF)
        wdb[...] = wd_ref[...].astype(_BF)

        def block(r0, nrows):
            s0 = r0 // SUB

            @pl.when(j == 0)
            def _():
                for t in range(nrows // SUB):
                    copy_in(s0 + t).wait()
                    rs = pl.ds(pl.multiple_of(r0 + t * SUB, SUB), SUB)
                    if not x_is_bf16:
                        xb[rs, :] = big[rs, :].astype(_BF)
                    big[rs, :] = jnp.zeros((SUB, D_MODEL), _F32)

            rs = pl.ds(r0, nrows)
            x = xb[rs, :]
            a = jnp.dot(x, wgb[...], preferred_element_type=_F32)
            b = jnp.dot(x, wub[...], preferred_element_type=_F32)
            p = (a * _sigmoid(a) * b).astype(_BF)
            big[rs, :] += jnp.dot(p, wdb[...], preferred_element_type=_F32)

            @pl.when(j == nj - 1)
            def _():
                for t in range(nrows // SUB):
                    copy_out(s0 + t).start()

        n_big = cnt // BLK_MIN
        rem = cnt - n_big * BLK_MIN
        room = BLK_SUBS - BLK_MIN
        grow = jnp.minimum(rem, n_big * room)
        left = rem - grow

        def one_block(i, r0):
            size = BLK_MIN + jnp.clip(grow - i * room, 0, room)
            for sz in range(BLK_MIN, BLK_SUBS + 1):
                pl.when(size == sz)(functools.partial(block, pl.multiple_of(r0, SUB), sz * SUB))
            return r0 + size * SUB
        r0 = lax.fori_loop(0, n_big, one_block, 0)
        bit = BLK_MIN // 2
        while bit >= 1:
            take = (left // bit) % 2 == 1
            pl.when(take)(functools.partial(block, pl.multiple_of(r0, SUB), bit * SUB))
            r0 = r0 + jnp.where(take, bit * SUB, 0)
            bit //= 2

    @pl.when(jnp.logical_and(j == nj - 1, cnt > 0))
    def _():
        for_subs(lambda s: copy_out(s).wait())

    if x_hbm.dtype == y_hbm.dtype:
        @pl.when(jnp.logical_and(v == pl.num_programs(0) - 1, j == nj - 1))
        def _():
            def tail(s):
                r = pl.multiple_of(s * SUB, SUB)
                return pltpu.make_async_copy(x_hbm.at[pl.ds(r, SUB)], y_hbm.at[pl.ds(r, SUB)], sem_out)
            lax.fori_loop(used_sub[0], n_sub_rows, lambda s, c: (tail(s).start(), c)[1], 0)
            lax.fori_loop(used_sub[0], n_sub_rows, lambda s, c: (tail(s).wait(), c)[1], 0)


def _grouped_ffn(x, wg, wu, wd, fc, vis_e, vis_start, vis_cnt, used_sub):
    ff = wg.shape[-1]
    nj = ff // fc
    nv = vis_e.shape[0]
    n_rows = x.shape[0]

    def chunk(v, j, vc):
        return jnp.where(vc[v] > 0, j, nj - 1)

    return pl.pallas_call(
        functools.partial(_ffn_kernel, nj, n_rows // SUB),
        grid_spec=pltpu.PrefetchScalarGridSpec(
            num_scalar_prefetch=4,
            grid=(nv, nj),
            in_specs=[
                pl.BlockSpec(memory_space=pl.ANY),
                pl.BlockSpec((None, D_MODEL, fc), lambda v, j, ve, vs, vc, us: (ve[v], 0, chunk(v, j, vc))),
                pl.BlockSpec((None, D_MODEL, fc), lambda v, j, ve, vs, vc, us: (ve[v], 0, chunk(v, j, vc))),
                pl.BlockSpec((None, fc, D_MODEL), lambda v, j, ve, vs, vc, us: (ve[v], chunk(v, j, vc), 0)),
            ],
            out_specs=pl.BlockSpec(memory_space=pl.ANY),
            scratch_shapes=[
                pltpu.VMEM((TMAX, D_MODEL), _F32),
                pltpu.VMEM((TMAX, D_MODEL), _BF),
                pltpu.VMEM((D_MODEL, fc), _BF),
                pltpu.VMEM((D_MODEL, fc), _BF),
                pltpu.VMEM((fc, D_MODEL), _BF),
                pltpu.SemaphoreType.DMA((SUB_MAX,)),
                pltpu.SemaphoreType.DMA(()),
            ],
        ),
        out_shape=jax.ShapeDtypeStruct((n_rows, D_MODEL), _F32),
        compiler_params=_params(("arbitrary", "arbitrary"), FFN_VMEM_LIMIT),
        name="ffn_%d" % ff,
    )(vis_e, vis_start, vis_cnt, used_sub, x, wg, wu, wd)


def _dense_visits():
    nv = T // TMAX
    return (jnp.zeros((nv,), jnp.int32),
            jnp.arange(nv, dtype=jnp.int32) * SUB_MAX,
            jnp.full((nv,), SUB_MAX, jnp.int32),
            jnp.full((1,), T // SUB, jnp.int32))


N_SLOT_SUB = -(-(TOP_K * T + N_TILES * N_EXPERTS * (SEG_ALIGN - 1)) // SUB) + N_EXPERTS
N_SLOT = N_SLOT_SUB * SUB
N_VISIT = -(-N_SLOT_SUB // SUB_MAX) + N_EXPERTS


def _routing_plan(eidx):
    e_loc = eidx.reshape(TOP_K, N_TILES, DISP_TILE).transpose(1, 0, 2).reshape(N_TILES, PAIRS)
    onehot = e_loc[:, :, None] == jnp.arange(N_EXPERTS, dtype=jnp.int32)[None, None, :]
    ch = 128
    oh = onehot.astype(_F32).reshape(N_TILES, PAIRS // ch, ch, N_EXPERTS)
    tri = (jnp.arange(ch)[:, None] >= jnp.arange(ch)[None, :]).astype(_F32)
    within = jnp.einsum("ij,tcjk->tcik", tri, oh)
    tot = within[:, :, -1, :]
    csum = (within + (jnp.cumsum(tot, axis=1) - tot)[:, :, None, :]).reshape(N_TILES, PAIRS, N_EXPERTS)
    csum = csum.astype(jnp.int32)
    n_te = (csum[:, -1, :] + SEG_ALIGN - 1) // SEG_ALIGN * SEG_ALIGN
    src = jnp.cumsum(n_te, axis=1) - n_te
    lpos = jnp.sum(jnp.where(onehot, csum - 1 + src[:, None, :], 0), axis=2).astype(jnp.int32)
    counts = jnp.sum(n_te, axis=0)
    nsub = (counts + SUB - 1) // SUB
    sub_base = jnp.cumsum(nsub) - nsub
    dst = (sub_base * SUB)[None, :] + jnp.cumsum(n_te, axis=0) - n_te
    seg = tuple(a.reshape(-1).astype(jnp.int32) for a in (n_te, src, dst))
    pads = ((sub_base * SUB + counts).astype(jnp.int32), (nsub * SUB - counts).astype(jnp.int32))
    used_sub = jnp.sum(nsub).reshape(1).astype(jnp.int32)
    nvis = (nsub + SUB_MAX - 1) // SUB_MAX
    vend = jnp.cumsum(nvis)
    total = vend[-1]
    vid = jnp.arange(N_VISIT, dtype=jnp.int32)
    ve = jnp.minimum(jnp.sum((vid[:, None] >= vend[None, :]).astype(jnp.int32), axis=1), N_EXPERTS - 1)
    local = vid - (vend - nvis)[ve]
    nv_e = jnp.maximum(nvis[ve], 1)
    q, r = nsub[ve] // nv_e, nsub[ve] % nv_e
    cnt = q + (local < r).astype(jnp.int32)
    start = sub_base[ve] + local * q + jnp.minimum(local, r)
    used = vid < total
    last_e = ve[jnp.maximum(total - 1, 0)]
    vis_e = jnp.where(used, ve, last_e).astype(jnp.int32)
    vis_cnt = jnp.where(used, cnt, 0).astype(jnp.int32)
    vis_start = jnp.where(used, start, 0).astype(jnp.int32)
    return lpos.reshape(N_TILES, TOP_K, DISP_TILE), seg, pads, (vis_e, vis_start, vis_cnt, used_sub)


def _for_pow2_pieces(n, max_piece, fn):
    off = 0
    bit = max_piece
    while bit >= SEG_ALIGN:
        take = (n // bit) % 2 == 1
        pl.when(take)(functools.partial(fn, off, bit))
        off = off + jnp.where(take, bit, 0)
        bit //= 2


def _segment_copies(tile, seg, make_copy, wait=False):
    n_te, src, dst = seg
    for e in range(N_EXPERTS):
        j = tile * N_EXPERTS + e
        s0, d0 = src[j], dst[j]

        def piece(off, size, s0=s0, d0=d0):
            cp = make_copy(pl.multiple_of(s0 + off, SEG_ALIGN), pl.multiple_of(d0 + off, SEG_ALIGN), size)
            cp.wait() if wait else cp.start()
        _for_pow2_pieces(n_te[j], DISP_TILE, piece)


def _dispatch_kernel(n_te, src, dst, pad_start, pad_cnt, used_sub, h_ref, lpos_ref, xs_hbm, ring, ring_sems, sem):
    i = pl.program_id(0)
    last = pl.num_programs(0) - 1
    b = i % 2

    def copies(tile, slot_id, wait):
        _segment_copies(tile, (n_te, src, dst), lambda s, d, size: pltpu.make_async_copy(
            ring.at[slot_id, pl.ds(s, size)], xs_hbm.at[pl.ds(d, size)], ring_sems.at[slot_id]), wait)

    pl.when(i >= 2)(lambda: copies(i - 2, b, True))
    lp = lpos_ref[...]
    srow = lax.broadcasted_iota(jnp.int32, (SORT_ROWS, DISP_TILE), 0)
    perm = jnp.where(lp[0:1, :] == srow, 1.0, jnp.where(lp[1:2, :] == srow, 1.0, 0.0)).astype(_BF)
    ring[b] = jnp.dot(perm, h_ref[...].astype(_BF), preferred_element_type=_F32)
    copies(i, b, False)

    @pl.when(i == last)
    def _():
        copies(i - 1, 1 - b, True)
        copies(i, b, True)
        for e in range(N_EXPERTS):
            p0 = pad_start[e]

            def fill(off, size, p0=p0):
                cp = pltpu.make_async_copy(ring.at[b, pl.ds(0, size)],
                                           xs_hbm.at[pl.ds(pl.multiple_of(p0 + off, SEG_ALIGN), size)], sem)
                cp.start()
                cp.wait()
            _for_pow2_pieces(pad_cnt[e], SUB // 2, fill)

        def tail(s):
            r = pl.multiple_of(s * SUB, SUB)
            return pltpu.make_async_copy(ring.at[b, pl.ds(0, SUB)], xs_hbm.at[pl.ds(r, SUB)], sem)
        lax.fori_loop(used_sub[0], N_SLOT_SUB, lambda s, c: (tail(s).start(), c)[1], 0)
        lax.fori_loop(used_sub[0], N_SLOT_SUB, lambda s, c: (tail(s).wait(), c)[1], 0)


def _dispatch(h, lpos, seg, pads, used_sub):
    return pl.pallas_call(
        _dispatch_kernel,
        grid_spec=pltpu.PrefetchScalarGridSpec(
            num_scalar_prefetch=6,
            grid=(N_TILES,),
            in_specs=[pl.BlockSpec((DISP_TILE, D_MODEL), lambda i, *_: (i, 0)),
                      pl.BlockSpec((None, TOP_K, DISP_TILE), lambda i, *_: (i, 0, 0))],
            out_specs=pl.BlockSpec(memory_space=pl.ANY),
            scratch_shapes=[pltpu.VMEM((2, SORT_ROWS, D_MODEL), _F32),
                            pltpu.SemaphoreType.DMA((2,)), pltpu.SemaphoreType.DMA(())],
        ),
        out_shape=jax.ShapeDtypeStruct((N_SLOT, D_MODEL), _F32),
        compiler_params=_params(("arbitrary",)),
        name="dispatch",
    )(*seg, *pads, used_sub, h, lpos)


def _residual_out(x, gate, f, g_ref, o_refs, i, tile):
    out = x + gate * f
    if g_ref is not None:
        ms = jnp.mean(out * out, axis=-1, keepdims=True)
        out = out * lax.rsqrt(ms + EPS) * g_ref[...]
    if len(o_refs) == 1:
        o_refs[0][...] = out
    else:
        @pl.when(i < T_PROMPT // tile)
        def _():
            o_refs[0][...] = out

        @pl.when(i >= T_PROMPT // tile)
        def _():
            o_refs[1][...] = out


def _out_specs(final, tile):
    if not final:
        return ([pl.BlockSpec((tile, D_MODEL), lambda i, *_: (i, 0))],
                [jax.ShapeDtypeStruct((T, D_MODEL), _F32)])
    return (_half_specs(D_MODEL, tile),
            [jax.ShapeDtypeStruct((T_PROMPT, D_MODEL), _F32), jax.ShapeDtypeStruct((T_SAMPLE, D_MODEL), _F32)])


def _combine_kernel(final, x_ref, mod_ref, y_ref, *rest):
    g_ref, o_refs = (rest[0], rest[1:]) if final else (None, rest)
    _residual_out(x_ref[...], mod_ref[5:6, :], y_ref[...], g_ref, o_refs, pl.program_id(0), ROW_TILE)


def _combine(x, l, mods, y, g_final=None):
    final = g_final is not None
    row = lambda i: (i, 0)
    in_specs = [
        pl.BlockSpec((ROW_TILE, D_MODEL), row),
        _layer_specs(l, ())[0],
        pl.BlockSpec((ROW_TILE, D_MODEL), row),
    ]
    args = [x, mods, y]
    if final:
        in_specs.append(pl.BlockSpec((1, D_MODEL), lambda i: (0, 0)))
        args.append(g_final.reshape(1, D_MODEL))
    out_specs, out_shape = _out_specs(final, ROW_TILE)
    return pl.pallas_call(
        functools.partial(_combine_kernel, final),
        grid=(T // ROW_TILE,),
        in_specs=in_specs,
        out_specs=out_specs,
        out_shape=out_shape,
        compiler_params=_params(("arbitrary",)),
        name="combine%s" % ("_final" if final else ""),
    )(*args)


def _combine_top2_kernel(final, n_out, n_te, src, dst, x_ref, mod_ref, w_ref, lpos_ref, *rest):
    g_ref, rest = (rest[0], rest[1:]) if final else (None, rest)
    ys_hbm, o_refs, (ybuf, sems) = rest[0], rest[1:1 + n_out], rest[1 + n_out:]
    i = pl.program_id(0)

    def fetch(tile, b, wait=False):
        _segment_copies(tile, (n_te, src, dst), lambda s, d, size: pltpu.make_async_copy(
            ys_hbm.at[pl.ds(d, size)], ybuf.at[b, pl.ds(s, size)], sems.at[b]), wait)

    @pl.when(i == 0)
    def _():
        for slot_id in range(2):
            ybuf[slot_id, PAIRS:SORT_ROWS, :] = jnp.zeros((SORT_ROWS - PAIRS, D_MODEL), _F32)
        fetch(0, 0)

    @pl.when(i + 1 < pl.num_programs(0))
    def _():
        fetch(i + 1, (i + 1) % 2)

    b = i % 2
    fetch(i, b, wait=True)
    y = ybuf[b].astype(_BF)
    lp = lpos_ref[...]
    scol = lax.broadcasted_iota(jnp.int32, (DISP_TILE, SORT_ROWS), 1)
    w = w_ref[...]
    pick = jnp.zeros((DISP_TILE, SORT_ROWS), _F32)
    for k in range(TOP_K):
        pick = jnp.where(lp[:, k:k + 1] == scol, w[:, k:k + 1], pick)
    f = jnp.dot(pick.astype(_BF), y, preferred_element_type=_F32)
    _residual_out(x_ref[...], mod_ref[5:6, :], f, g_ref, o_refs, i, DISP_TILE)


def _combine_top2(x, l, mods, ys, lpos_t, seg, w, g_final=None):
    final = g_final is not None
    row = lambda i, *_: (i, 0)
    in_specs = [
        pl.BlockSpec((DISP_TILE, D_MODEL), row),
        pl.BlockSpec((None, None, 6, D_MODEL), lambda i, *_: (l, _cond_of_tile(i, DISP_TILE), 0, 0)),
        pl.BlockSpec((DISP_TILE, TOP_K), row),
        pl.BlockSpec((None, DISP_TILE, TOP_K), lambda i, *_: (i, 0, 0)),
    ]
    args = [x, mods, w, lpos_t]
    if final:
        in_specs.append(pl.BlockSpec((1, D_MODEL), lambda i, *_: (0, 0)))
        args.append(g_final.reshape(1, D_MODEL))
    in_specs.append(pl.BlockSpec(memory_space=pl.ANY))
    args.append(ys)
    out_specs, out_shape = _out_specs(final, DISP_TILE)
    return pl.pallas_call(
        functools.partial(_combine_top2_kernel, final, len(out_specs)),
        grid_spec=pltpu.PrefetchScalarGridSpec(
            num_scalar_prefetch=3,
            grid=(N_TILES,),
            in_specs=in_specs,
            out_specs=out_specs,
            scratch_shapes=[
                pltpu.VMEM((2, SORT_ROWS, D_MODEL), _F32),
                pltpu.SemaphoreType.DMA((2,)),
            ],
        ),
        out_shape=out_shape,
        compiler_params=_params(("arbitrary",)),
        name="combine_top2%s" % ("_final" if final else ""),
    )(*seg, *args)


def kernel(x_prompt, x_sample, cache_k, cache_v, c, c_ctx, w_ada, b_ada, g_norm_mix, g_norm_ffn,
           w_in, w_out, attn_sink, conv_dw, conv_dw_b, conv_ln_g, conv_ln_b, conv_pw, fnet_w,
           ffn_w_gate, ffn_w_up, ffn_w_down, moe_router, moe_w_gate, moe_w_up, moe_w_down, g_final):
    xs = (x_prompt.reshape(T_PROMPT, D_MODEL), x_sample.reshape(T_SAMPLE, D_MODEL))
    cond8 = jnp.concatenate([c_ctx[None, :], c, jnp.zeros((N_COND - 1 - DEC_BATCH, D_MODEL), _F32)], axis=0)
    mods = _modulation(cond8, w_ada, b_ada).reshape(DEPTH, N_COND, 6, D_MODEL)
    cos_t, sin_t = _rope_tables()
    ck_all = cache_k.reshape(DEC_BATCH, DEPTH, PAST_LEN, KV_W)
    cv_all = cache_v.reshape(DEC_BATCH, DEPTH, PAST_LEN, KV_W)
    p_blocks = T_PROMPT // DEC_SEQ
    sinks = attn_sink.reshape(DEPTH * N_HEADS)

    ks, vs = [], []
    resid = None
    for l in range(DEPTH):
        q, kp, vp, kl, vl, uc, uf, *x_new = _inproj(xs, l, mods, g_norm_mix, w_in, cos_t, sin_t, resid=resid)
        if resid is not None:
            xs, resid = tuple(x_new), None
        ks.append(kp)
        vs.append(vp)
        attn = (_ctx_attention(l, sinks, q, kp, vp),
                _lat_attention(l, sinks, q, kl, vl, ck_all, cv_all))
        cargs = (l, conv_dw, conv_dw_b, conv_ln_g, conv_ln_b, conv_pw)
        conv = (_conv_module(uc, SEQ, BATCH, 0, *cargs),
                _conv_module(uc, DEC_SEQ, DEC_BATCH, p_blocks, *cargs))
        four = (_fourier_mix(uf, SEQ, BATCH, 0, l, fnet_w),
                _fourier_mix(uf, DEC_SEQ, DEC_BATCH, T_PROMPT // 2, l, fnet_w))
        last = g_final if l == DEPTH - 1 else None
        i = l // 2
        if l % 2 == 0:
            x, h = _outproj((attn, conv, four), xs, l, mods, w_out, g_norm_ffn)
            y = _grouped_ffn(h, ffn_w_gate[i:i + 1], ffn_w_up[i:i + 1], ffn_w_down[i:i + 1], 256,
                             *_dense_visits())
            if last is None:
                xs, resid = (x,), (y, l)
            else:
                xs = tuple(_combine(x, l, mods, y, g_final=last))
        else:
            x, h, eidx, ew = _outproj((attn, conv, four), xs, l, mods, w_out, g_norm_ffn,
                                      router_t=moe_router[i].T)
            lpos, seg, pads, visits = _routing_plan(eidx)
            xd = _dispatch(h, lpos, seg, pads, visits[-1])
            ys = _grouped_ffn(xd, moe_w_gate[i], moe_w_up[i], moe_w_down[i], 512, *visits)
            xs = tuple(_combine_top2(x, l, mods, ys, lpos.transpose(0, 2, 1), seg, ew.T, g_final=last))

    y_prompt = xs[0].reshape(BATCH, SEQ, D_MODEL)
    y_sample = xs[1].reshape(DEC_BATCH, DEC_SEQ, D_MODEL)
    state_k = jnp.stack([a.reshape(BATCH, SEQ, N_KV_HEADS, HEAD_DIM) for a in ks], axis=1)
    state_v = jnp.stack([a.reshape(BATCH, SEQ, N_KV_HEADS, HEAD_DIM) for a in vs], axis=1)
    return (y_prompt, y_sample, state_k, state_v)
```

```python
import functools

import numpy as np
import jax
import jax.numpy as jnp
from jax import lax
from jax.experimental import pallas as pl
from jax.experimental.pallas import tpu as pltpu

D_MODEL = 1024
BATCH = 16
SEQ = 256
DEPTH = 2
DEC_BATCH = 2
DEC_SEQ = 2048
PAST_LEN = 512
GRID_W = 64
HEAD_DIM = 64
N_HEADS = 8
N_KV_HEADS = 2
GQA_GROUP = N_HEADS // N_KV_HEADS
ATTN_W = N_HEADS * HEAD_DIM
KV_W = N_KV_HEADS * HEAD_DIM
WINDOW = 128
BLOCK = 128
ROPE_THETA = 10000.0
CONV_W = D_MODEL // 4
CONV_K = 31
FNET_GROUPS = 4
FNET_W = D_MODEL // 4
FNET_GW = FNET_W // FNET_GROUPS
IN_WIDTH = ATTN_W + 2 * KV_W + 2 * CONV_W + FNET_W
D_FF = 2816
N_EXPERTS = 8
TOP_K = 2
D_FF_EXPERT = 3584
EPS = 1e-6
NEG = -1e30

T_PROMPT = BATCH * SEQ
T_SAMPLE = DEC_BATCH * DEC_SEQ
T = T_PROMPT + T_SAMPLE
N_COND = 8

ROW_TILE = 512
SUB = 64
SUB_MAX = 64
TMAX = SUB * SUB_MAX
BLK_MIN = 16
BLK_SUBS = 20
DISP_TILE = 512
N_TILES = T // DISP_TILE
PAIRS = TOP_K * DISP_TILE
SEG_ALIGN = 8
SORT_ROWS = -(-(PAIRS + N_EXPERTS * (SEG_ALIGN - 1)) // 128) * 128
CONV_CHUNK = 256
CONV_PAD = 16
CONV_SPAN = CONV_CHUNK + 8 * ((CONV_PAD + CONV_K // 2) // 8)
VMEM_LIMIT = 48 * 1024 * 1024
FFN_VMEM_LIMIT = 56 * 1024 * 1024

_BF = jnp.bfloat16
_F32 = jnp.float32


def _cond_of_tile(i, tile):
    r = i * tile
    return jnp.where(r < T_PROMPT, 0, 1 + (r - T_PROMPT) // DEC_SEQ)


def _params(sem, vmem=VMEM_LIMIT):
    return pltpu.CompilerParams(dimension_semantics=sem, vmem_limit_bytes=vmem)


def _sigmoid(x):
    return 1.0 / (1.0 + jnp.exp(-x))


def _mod_kernel(cond_ref, w_ref, b_ref, o_ref):
    cnd = cond_ref[...]
    s = (cnd * _sigmoid(cnd)).astype(_BF)
    o_ref[...] = jnp.dot(s, w_ref[...].astype(_BF), preferred_element_type=_F32) + b_ref[...]


def _modulation(cond8, w_ada, b_ada):
    nt = 1536
    return pl.pallas_call(
        _mod_kernel,
        grid=(DEPTH, 6 * D_MODEL // nt),
        in_specs=[
            pl.BlockSpec((N_COND, D_MODEL), lambda l, n: (0, 0)),
            pl.BlockSpec((None, D_MODEL, nt), lambda l, n: (l, 0, n)),
            pl.BlockSpec((None, 1, nt), lambda l, n: (l, 0, n)),
        ],
        out_specs=pl.BlockSpec((None, N_COND, nt), lambda l, n: (l, 0, n)),
        out_shape=jax.ShapeDtypeStruct((DEPTH, N_COND, 6 * D_MODEL), _F32),
        compiler_params=_params(("arbitrary", "arbitrary")),
        name="modulation",
    )(cond8, w_ada, b_ada.reshape(DEPTH, 1, 6 * D_MODEL))


def _norm_mod(x, g, shift, scale):
    ms = jnp.mean(x * x, axis=-1, keepdims=True)
    y = x * lax.rsqrt(ms + EPS) * g
    return y * (1.0 + scale) + shift


def _rope_tables():
    rows = DEC_SEQ // GRID_W
    n_freq = HEAD_DIM // 4
    inv = ROPE_THETA ** (-jnp.arange(n_freq, dtype=_F32) / n_freq)
    gr, gc = jnp.meshgrid(jnp.arange(rows, dtype=_F32), jnp.arange(GRID_W, dtype=_F32), indexing="ij")
    ang_r = gr.reshape(-1)[:, None] * inv
    ang_c = gc.reshape(-1)[:, None] * inv
    cr, sr, cc, sc = jnp.cos(ang_r), jnp.sin(ang_r), jnp.cos(ang_c), jnp.sin(ang_c)
    cos64 = jnp.concatenate([cr, cr, cc, cc], axis=-1)
    sin64 = jnp.concatenate([-sr, sr, -sc, sc], axis=-1)
    return jnp.tile(cos64, (1, 2)), jnp.tile(sin64, (1, 2))


def _rope128(x, cos, sin):
    lane = lax.broadcasted_iota(jnp.int32, x.shape, 1)
    first = (lane % 32) < 16
    partner = jnp.where(first, pltpu.roll(x, 128 - 16, 1), pltpu.roll(x, 16, 1))
    return x * cos + partner * sin


def _half_specs(width, tile):
    pt = T_PROMPT // tile
    return [pl.BlockSpec((tile, width), lambda i, *_: (jnp.minimum(i, pt - 1), 0)),
            pl.BlockSpec((tile, width), lambda i, *_: (jnp.maximum(i - pt, 0), 0))]


def _pick_half(i, tile, a_ref, b_ref):
    return jnp.where(i < T_PROMPT // tile, a_ref[...], b_ref[...])


def _layer_specs(l, w_shape):
    return [
        pl.BlockSpec((None, None, 6, D_MODEL), lambda i, *_: (l, _cond_of_tile(i, ROW_TILE), 0, 0)),
        pl.BlockSpec((None, 1, D_MODEL), lambda i, *_: (l, 0, 0)),
        pl.BlockSpec((None,) + w_shape, lambda i, *_: (l, 0, 0)),
    ]


def _inproj_kernel(split, resid, *refs):
    n_x = 2 if split else 1
    x_refs, refs = refs[:n_x], refs[n_x:]
    if resid:
        y_ref, modp_ref = refs[:2]
        refs = refs[2:]
    mod_ref, g_ref, w_ref, cos_ref, sin_ref = refs[:5]
    q_ref, kp_ref, vp_ref, ks_ref, vs_ref, uc_ref, uf_ref = refs[5:12]
    wb_ref, uf_scr = refs[-2:]
    i = pl.program_id(0)

    @pl.when(i == 0)
    def _():
        wb_ref[...] = w_ref[...].astype(_BF)

    x = _pick_half(i, ROW_TILE, *x_refs) if split else x_refs[0][...]
    if resid:
        x = x + modp_ref[5:6, :] * y_ref[...]
        refs[12][...] = x
    h = _norm_mod(x, g_ref[...], mod_ref[0:1, :], mod_ref[1:2, :]).astype(_BF)
    latent = i >= T_PROMPT // ROW_TILE
    cos = jnp.where(latent, cos_ref[...], 1.0)
    sin = jnp.where(latent, sin_ref[...], 0.0)

    def proj(o, n):
        return jnp.dot(h, wb_ref[:, o:o + n], preferred_element_type=_F32)

    o = 0
    for c in range(ATTN_W // 256):
        qq = proj(o, 256)
        q_ref[:, o:o + 128] = _rope128(qq[:, :128], cos, sin).astype(q_ref.dtype)
        q_ref[:, o + 128:o + 256] = _rope128(qq[:, 128:], cos, sin).astype(q_ref.dtype)
        o += 256
    kv = proj(o, 2 * KV_W)
    k = _rope128(kv[:, :KV_W], cos, sin)

    @pl.when(jnp.logical_not(latent))
    def _():
        kp_ref[...] = k
        vp_ref[...] = kv[:, KV_W:]

    @pl.when(latent)
    def _():
        ks_ref[...] = k
        vs_ref[...] = kv[:, KV_W:]
    o += 2 * KV_W
    uc_ref[...] = proj(o, 2 * CONV_W)
    o += 2 * CONV_W
    uf = proj(o, FNET_W)
    n_lane = FNET_W // 128
    for c in range(n_lane):
        uf_scr[c] = uf[:, c * 128:(c + 1) * 128]
    half = ROW_TILE // 2
    uf_ref[...] = jnp.concatenate([uf_scr[c, pl.ds(par, half, stride=2), :]
                                   for par in range(2) for c in range(n_lane)], axis=-1).astype(uf_ref.dtype)


def _inproj(xs, l, mods, g_all, w_all, cos_t, sin_t, resid=None):
    split = len(xs) == 2
    pt = T_PROMPT // ROW_TILE
    per_seq = DEC_SEQ // ROW_TILE

    def rope_idx(i):
        return (jnp.maximum(i - pt, 0) % per_seq, 0)

    row = lambda i: (i, 0)
    in_specs = _half_specs(D_MODEL, ROW_TILE) if split else [pl.BlockSpec((ROW_TILE, D_MODEL), row)]
    args = list(xs)
    kv_ctx, kv_lat = _half_specs(KV_W, ROW_TILE)
    out_specs = [pl.BlockSpec((ROW_TILE, ATTN_W), row), kv_ctx, kv_ctx, kv_lat, kv_lat,
                 pl.BlockSpec((ROW_TILE, 2 * CONV_W), row), pl.BlockSpec((ROW_TILE // 2, 2 * FNET_W), row)]
    out_shape = [jax.ShapeDtypeStruct((T, ATTN_W), _BF),
                 jax.ShapeDtypeStruct((T_PROMPT, KV_W), _F32), jax.ShapeDtypeStruct((T_PROMPT, KV_W), _F32),
                 jax.ShapeDtypeStruct((T_SAMPLE, KV_W), _F32), jax.ShapeDtypeStruct((T_SAMPLE, KV_W), _F32),
                 jax.ShapeDtypeStruct((T, 2 * CONV_W), _F32), jax.ShapeDtypeStruct((T // 2, 2 * FNET_W), _BF)]
    if resid is not None:
        y, l_prev = resid
        in_specs += [pl.BlockSpec((ROW_TILE, D_MODEL), row), _layer_specs(l_prev, ())[0]]
        args += [y, mods]
        out_specs.append(pl.BlockSpec((ROW_TILE, D_MODEL), row))
        out_shape.append(jax.ShapeDtypeStruct((T, D_MODEL), _F32))
    return pl.pallas_call(
        functools.partial(_inproj_kernel, split, resid is not None),
        grid=(T // ROW_TILE,),
        in_specs=in_specs + _layer_specs(l, (D_MODEL, IN_WIDTH)) + [
            pl.BlockSpec((ROW_TILE, 128), rope_idx),
            pl.BlockSpec((ROW_TILE, 128), rope_idx),
        ],
        out_specs=out_specs,
        out_shape=out_shape,
        scratch_shapes=[pltpu.VMEM((D_MODEL, IN_WIDTH), _BF), pltpu.VMEM((FNET_W // 128, ROW_TILE, 128), _F32)],
        compiler_params=_params(("arbitrary",)),
        name="inproj_resid" if resid is not None else "inproj",
    )(*args, mods, g_all.reshape(DEPTH, 1, D_MODEL), w_all, cos_t, sin_t)


assert 2 * HEAD_DIM == 128 and KV_W == 128 and GQA_GROUP == 4


def _kv_operands(x):
    return x.astype(_BF), pltpu.roll(x, HEAD_DIM, 1).astype(_BF)


def _half_mask(shape, par):
    return (lax.broadcasted_iota(jnp.int32, shape, 1) >= HEAD_DIM) == (par == 1)


def _stack_chunks(q, kh, rows):
    c0 = 2 * kh
    return jnp.concatenate([q[:, c0 * 128:(c0 + 1) * 128], q[:, (c0 + 1) * 128:(c0 + 2) * 128]], axis=0)


def _sink_column(sink_ref, heads, rows):
    r = lax.broadcasted_iota(jnp.int32, (len(heads) * rows, 1), 0)
    col = jnp.full((len(heads) * rows, 1), sink_ref[heads[0]], _F32)
    for g in range(1, len(heads)):
        col = jnp.where(r >= g * rows, sink_ref[heads[g]], col)
    return col


def _stack_heads(q, kh, rows):
    parts = [q[:, (kh * GQA_GROUP + g) * HEAD_DIM:(kh * GQA_GROUP + g + 1) * HEAD_DIM] for g in range(GQA_GROUP)]
    return jnp.concatenate(parts, axis=0)


def _store_heads(o_ref, o, kh, rows):
    for pair in range(GQA_GROUP // 2):
        a = o[(2 * pair) * rows:(2 * pair + 1) * rows]
        b = o[(2 * pair + 1) * rows:(2 * pair + 2) * rows]
        c0 = (kh * GQA_GROUP + 2 * pair) * HEAD_DIM
        o_ref[:, c0:c0 + 2 * HEAD_DIM] = jnp.concatenate([a, b], axis=-1).astype(o_ref.dtype)


def _store_chunks(o_ref, o, kh, rows):
    c0 = 2 * kh
    o_ref[:, c0 * 128:(c0 + 1) * 128] = o[:rows].astype(o_ref.dtype)
    o_ref[:, (c0 + 1) * 128:(c0 + 2) * 128] = o[rows:].astype(o_ref.dtype)


_NT = (((1,), (1,)), ((), ()))


def _ctx_attn_kernel(l, sink_ref, q_ref, k_ref, v_ref, o_ref):
    scale = HEAD_DIM ** -0.5
    q = q_ref[...] * scale
    k_ops = _kv_operands(k_ref[...])
    v_ops = _kv_operands(v_ref[...])
    for kh in range(N_KV_HEADS):
        qs = _stack_chunks(q, kh, SEQ).astype(_BF)
        o = None
        for par in range(2):
            own = _half_mask(qs.shape, par)
            which = 0 if kh == par else 1
            s = lax.dot_general(jnp.where(own, qs, 0), k_ops[which], _NT, preferred_element_type=_F32)
            h0 = l * N_HEADS + kh * GQA_GROUP + par
            sink = _sink_column(sink_ref, (h0, h0 + 2), SEQ)
            m = jnp.maximum(jnp.max(s, axis=-1, keepdims=True), sink)
            p = jnp.exp(s - m)
            den = jnp.sum(p, axis=-1, keepdims=True) + jnp.exp(sink - m)
            t = jnp.where(own, jnp.dot(p.astype(_BF), v_ops[which], preferred_element_type=_F32) / den, 0.0)
            o = t if o is None else o + t
        _store_chunks(o_ref, o, kh, SEQ)


def _ctx_attention(l, sinks, q, k, v):
    row = lambda b, s: (b, 0)
    return pl.pallas_call(
        functools.partial(_ctx_attn_kernel, l),
        grid_spec=pltpu.PrefetchScalarGridSpec(
            num_scalar_prefetch=1,
            grid=(BATCH,),
            in_specs=[
                pl.BlockSpec((SEQ, ATTN_W), row),
                pl.BlockSpec((SEQ, KV_W), row),
                pl.BlockSpec((SEQ, KV_W), row),
            ],
            out_specs=pl.BlockSpec((SEQ, ATTN_W), row),
        ),
        out_shape=jax.ShapeDtypeStruct((T_PROMPT, ATTN_W), _BF),
        compiler_params=_params(("arbitrary",)),
        name="ctx_attention",
    )(sinks, q, k, v)


def _lat_attn_kernel(l, sink_ref, q_ref, k_ref, v_ref, ck_ref, cv_ref, o_ref):
    n = pl.program_id(1)
    scale = HEAD_DIM ** -0.5
    nb = DEC_SEQ // BLOCK
    band = 3 * BLOCK
    start = pl.multiple_of(jnp.clip(n - 1, 0, nb - 3) * BLOCK, BLOCK)
    q = q_ref[...] * scale
    kb = k_ref[pl.ds(start, band), :]
    vb = v_ref[pl.ds(start, band), :]
    dpos = (lax.broadcasted_iota(jnp.int32, (BLOCK, band), 1) - lax.broadcasted_iota(jnp.int32, (BLOCK, band), 0)
            + (start - n * BLOCK))
    in_window = jnp.where(jnp.abs(dpos) <= WINDOW, 0.0, NEG)
    bias = jnp.concatenate([in_window] * GQA_GROUP, axis=0)
    for kh in range(N_KV_HEADS):
        hs = slice(kh * HEAD_DIM, (kh + 1) * HEAD_DIM)
        qs = _stack_heads(q, kh, BLOCK).astype(_BF)
        s_loc = lax.dot_general(qs, kb[:, hs].astype(_BF), _NT, preferred_element_type=_F32)
        s_loc = jnp.where(bias < 0.0, NEG, s_loc)
        s_ctx = lax.dot_general(qs, ck_ref[:, hs].astype(_BF), _NT, preferred_element_type=_F32)
        h0 = l * N_HEADS + kh * GQA_GROUP
        sink = _sink_column(sink_ref, tuple(range(h0, h0 + GQA_GROUP)), BLOCK)
        m = jnp.maximum(jnp.maximum(jnp.max(s_loc, axis=-1, keepdims=True),
                                    jnp.max(s_ctx, axis=-1, keepdims=True)), sink)
        p_loc = jnp.exp(s_loc - m)
        p_ctx = jnp.exp(s_ctx - m)
        den = (jnp.sum(p_loc, axis=-1, keepdims=True) + jnp.sum(p_ctx, axis=-1, keepdims=True)
               + jnp.exp(sink - m))
        o = (jnp.dot(p_loc.astype(_BF), vb[:, hs].astype(_BF), preferred_element_type=_F32)
             + jnp.dot(p_ctx.astype(_BF), cv_ref[:, hs].astype(_BF), preferred_element_type=_F32)) / den
        _store_heads(o_ref, o, kh, BLOCK)


def _lat_attention(l, sinks, q, k, v, ck, cv):
    nb = DEC_SEQ // BLOCK
    q0 = T_PROMPT // BLOCK
    return pl.pallas_call(
        functools.partial(_lat_attn_kernel, l),
        grid_spec=pltpu.PrefetchScalarGridSpec(
            num_scalar_prefetch=1,
            grid=(DEC_BATCH, nb),
            in_specs=[
                pl.BlockSpec((BLOCK, ATTN_W), lambda b, n, s: (q0 + b * nb + n, 0)),
                pl.BlockSpec((DEC_SEQ, KV_W), lambda b, n, s: (b, 0)),
                pl.BlockSpec((DEC_SEQ, KV_W), lambda b, n, s: (b, 0)),
                pl.BlockSpec((None, None, PAST_LEN, KV_W), lambda b, n, s: (b, l, 0, 0)),
                pl.BlockSpec((None, None, PAST_LEN, KV_W), lambda b, n, s: (b, l, 0, 0)),
            ],
            out_specs=pl.BlockSpec((BLOCK, ATTN_W), lambda b, n, s: (b * nb + n, 0)),
        ),
        out_shape=jax.ShapeDtypeStruct((T_SAMPLE, ATTN_W), _BF),
        compiler_params=_params(("arbitrary", "arbitrary")),
        name="lat_attention",
    )(sinks, q, k, v, ck, cv)


def _conv_kernel(seq, u_ref, dw_ref, dwb_ref, lg_ref, lb_ref, pw_ref, o_ref, pad_ref, y_ref, sh_ref):
    u = u_ref[...]
    pad_ref[0:CONV_PAD, :] = jnp.zeros((CONV_PAD, CONV_W), _F32)
    pad_ref[CONV_PAD + seq:2 * CONV_PAD + seq, :] = jnp.zeros((CONV_PAD, CONV_W), _F32)
    pad_ref[CONV_PAD:CONV_PAD + seq, :] = u[:, :CONV_W] * _sigmoid(u[:, CONV_W:])
    off = CONV_PAD - CONV_K // 2
    span = CONV_SPAN

    def chunk(c, carry):
        row = pl.multiple_of(c * CONV_CHUNK, CONV_CHUNK)
        win = pad_ref[pl.ds(row, CONV_CHUNK + 2 * CONV_PAD), :]
        acc = jnp.zeros((CONV_CHUNK, CONV_W), _F32) + dwb_ref[...]
        for phase in range(8):
            sh_ref[phase] = win[phase:phase + span, :]
        for phase in range(8):
            for a in range(span // 8):
                t = 8 * a + phase - off
                if 0 <= t < CONV_K:
                    acc = acc + sh_ref[phase, 8 * a:8 * a + CONV_CHUNK, :] * dw_ref[t:t + 1, :]
        mu = jnp.mean(acc, axis=-1, keepdims=True)
        d = acc - mu
        var = jnp.mean(d * d, axis=-1, keepdims=True)
        y = d * lax.rsqrt(var + EPS) * lg_ref[...] + lb_ref[...]
        y_ref[pl.ds(row, CONV_CHUNK), :] = (y * _sigmoid(y)).astype(_BF)
        return carry

    lax.fori_loop(0, seq // CONV_CHUNK, chunk, 0)
    o_ref[...] = jnp.dot(y_ref[...], pw_ref[...].astype(_BF), preferred_element_type=_F32).astype(o_ref.dtype)


def _conv_module(uc, seq, nbatch, block0, l, dw, dwb, lg, lb, pw):
    vec = lambda a: a.reshape(DEPTH, 1, CONV_W)
    layer = lambda b: (l, 0, 0)
    return pl.pallas_call(
        functools.partial(_conv_kernel, seq),
        grid=(nbatch,),
        in_specs=[
            pl.BlockSpec((seq, 2 * CONV_W), lambda b: (block0 + b, 0)),
            pl.BlockSpec((None, CONV_K, CONV_W), layer),
            pl.BlockSpec((None, 1, CONV_W), layer),
            pl.BlockSpec((None, 1, CONV_W), layer),
            pl.BlockSpec((None, 1, CONV_W), layer),
            pl.BlockSpec((None, CONV_W, CONV_W), layer),
        ],
        out_specs=pl.BlockSpec((seq, CONV_W), lambda b: (b, 0)),
        out_shape=jax.ShapeDtypeStruct((nbatch * seq, CONV_W), _BF),
        scratch_shapes=[pltpu.VMEM((seq + 2 * CONV_PAD, CONV_W), _F32), pltpu.VMEM((seq, CONV_W), _BF),
                        pltpu.VMEM((8, CONV_SPAN, CONV_W), _F32)],
        compiler_params=_params(("arbitrary",)),
        name="conv_module_%d" % seq,
    )(uc, dw, vec(dwb), vec(lg), vec(lb), pw)


def _dft_constants(seq):
    half = seq // 2
    j = np.arange(half, dtype=np.int64)
    ang = 2.0 * np.pi * ((j[:, None] * j[None, :]) % half).astype(np.float64) / half
    m = np.concatenate([np.cos(ang), -np.sin(ang)], axis=1) / np.sqrt(seq)
    tw = np.pi * j.astype(np.float64) / half
    cb = np.repeat(np.cos(tw)[:, None], FNET_W, axis=1)
    sb = np.repeat(np.sin(tw)[:, None], FNET_W, axis=1)
    c = np.arange(FNET_GW, dtype=np.int64)
    angc = 2.0 * np.pi * ((c[:, None] * c[None, :]) % FNET_GW).astype(np.float64) / FNET_GW
    eye = np.eye(FNET_GROUPS)
    w1 = np.concatenate([np.kron(eye, np.cos(angc)), np.kron(eye, np.sin(angc))], axis=1) / np.sqrt(FNET_GW)
    return tuple(jnp.asarray(a, dtype=_F32) for a in (m, cb, sb, w1))


def _fnet_kernel(half, group, u_ref, m_ref, cb_ref, sb_ref, w1_ref, fw_ref, o_ref, rhs_ref):
    w1 = w1_ref[...].astype(_BF)
    te = jnp.dot(u_ref[:, :FNET_W], w1, preferred_element_type=_F32)
    to = jnp.dot(u_ref[:, FNET_W:], w1, preferred_element_type=_F32)
    for g in range(group):
        rows = slice(g * half, (g + 1) * half)
        ec, es = te[rows, :FNET_W], te[rows, FNET_W:]
        oc, os_ = to[rows, :FNET_W], to[rows, FNET_W:]
        for col, (top, bot) in enumerate(((ec, es), (oc, os_), (os_, -oc))):
            lanes = slice((3 * g + col) * FNET_W, (3 * g + col + 1) * FNET_W)
            rhs_ref[0:half, lanes] = top.astype(_BF)
            rhs_ref[half:2 * half, lanes] = bot.astype(_BF)
    pqr = jnp.dot(m_ref[...].astype(_BF), rhs_ref[...], preferred_element_type=_F32)
    cb, sb = cb_ref[...], sb_ref[...]
    ys = []
    for g in range(group):
        p, q, r = (pqr[:, (3 * g + col) * FNET_W:(3 * g + col + 1) * FNET_W] for col in range(3))
        rot = cb * q - sb * r
        ys += [(p + rot).astype(_BF), (p - rot).astype(_BF)]
    o_ref[...] = jnp.dot(jnp.concatenate(ys, axis=0), fw_ref[...].astype(_BF),
                         preferred_element_type=_F32).astype(o_ref.dtype)


def _fourier_mix(u_pairs, seq, nbatch, row0, l, fw):
    half = seq // 2
    group = max(1, 512 // half)
    rows = group * half
    m, cb, sb, w1 = _dft_constants(seq)
    const = lambda b: (0, 0)
    return pl.pallas_call(
        functools.partial(_fnet_kernel, half, group),
        grid=(nbatch // group,),
        in_specs=[
            pl.BlockSpec((rows, 2 * FNET_W), lambda b: (row0 // rows + b, 0)),
            pl.BlockSpec((half, 2 * half), const),
            pl.BlockSpec((half, FNET_W), const),
            pl.BlockSpec((half, FNET_W), const),
            pl.BlockSpec((FNET_W, 2 * FNET_W), const),
            pl.BlockSpec((None, FNET_W, FNET_W), lambda b: (l, 0, 0)),
        ],
        out_specs=pl.BlockSpec((2 * rows, FNET_W), lambda b: (b, 0)),
        out_shape=jax.ShapeDtypeStruct((nbatch * seq, FNET_W), _BF),
        scratch_shapes=[pltpu.VMEM((2 * half, 3 * group * FNET_W), _BF)],
        compiler_params=_params(("arbitrary",)),
        name="fourier_mix_%d" % seq,
    )(u_pairs, m, cb, sb, w1, fw)


def _route_top2(r_ref, h, ei_ref, ew_ref):
    lg = lax.dot_general(r_ref[...].astype(_BF), h.astype(_BF), _NT, preferred_element_type=_F32)
    eid = lax.broadcasted_iota(jnp.int32, lg.shape, 0)
    m1 = jnp.max(lg, axis=0, keepdims=True)
    i1 = jnp.min(jnp.where(lg == m1, eid, N_EXPERTS), axis=0, keepdims=True)
    lg2 = jnp.where(eid == i1, -jnp.inf, lg)
    m2 = jnp.max(lg2, axis=0, keepdims=True)
    i2 = jnp.min(jnp.where(lg2 == m2, eid, N_EXPERTS), axis=0, keepdims=True)
    e = jnp.exp(m2 - m1)
    ei_ref[0:1, :] = i1
    ei_ref[1:2, :] = i2
    ew_ref[0:1, :] = 1.0 / (1.0 + e)
    ew_ref[1:2, :] = e / (1.0 + e)


def _outproj_kernel(split, route, *refs):
    branch_refs, refs = refs[:6], refs[6:]
    n_x = 2 if split else 1
    x_refs, refs = refs[:n_x], refs[n_x:]
    mod_ref, w_ref, gf_ref = refs[:3]
    refs = refs[3:]
    if route:
        r_ref, xo_ref, h_ref, ei_ref, ew_ref, wb_ref = refs
    else:
        xo_ref, h_ref, wb_ref = refs
    i = pl.program_id(0)

    @pl.when(i == 0)
    def _():
        wb_ref[...] = w_ref[...].astype(_BF)

    mix = jnp.concatenate([_pick_half(i, ROW_TILE, branch_refs[2 * n], branch_refs[2 * n + 1])
                           for n in range(3)], axis=-1)
    x = _pick_half(i, ROW_TILE, *x_refs) if split else x_refs[0][...]
    x_new = x + mod_ref[2:3, :] * jnp.dot(mix, wb_ref[...], preferred_element_type=_F32)
    xo_ref[...] = x_new
    h = _norm_mod(x_new, gf_ref[...], mod_ref[3:4, :], mod_ref[4:5, :])
    h_ref[...] = h.astype(h_ref.dtype)
    if route:
        _route_top2(r_ref, h, ei_ref, ew_ref)


def _outproj(branches, xs, l, mods, w_all, gf_all, router_t=None):
    split = len(xs) == 2
    route = router_t is not None
    row = lambda i: (i, 0)
    in_specs = []
    args = []
    for pair, width in zip(branches, (ATTN_W, CONV_W, FNET_W)):
        in_specs += _half_specs(width, ROW_TILE)
        args += list(pair)
    in_specs += _half_specs(D_MODEL, ROW_TILE) if split else [pl.BlockSpec((ROW_TILE, D_MODEL), row)]
    args += list(xs)
    mod_spec, gf_spec, w_spec = _layer_specs(l, (D_MODEL, D_MODEL))
    in_specs += [mod_spec, w_spec, gf_spec]
    args += [mods, w_all, gf_all.reshape(DEPTH, 1, D_MODEL)]
    out_specs = [pl.BlockSpec((ROW_TILE, D_MODEL), row)] * 2
    out_shape = [jax.ShapeDtypeStruct((T, D_MODEL), _F32), jax.ShapeDtypeStruct((T, D_MODEL), _BF)]
    if route:
        in_specs.append(pl.BlockSpec((N_EXPERTS, D_MODEL), lambda i: (0, 0)))
        args.append(router_t)
        out_specs += [pl.BlockSpec((TOP_K, ROW_TILE), lambda i: (0, i))] * 2
        out_shape += [jax.ShapeDtypeStruct((TOP_K, T), jnp.int32), jax.ShapeDtypeStruct((TOP_K, T), _F32)]
    return pl.pallas_call(
        functools.partial(_outproj_kernel, split, route),
        grid=(T // ROW_TILE,),
        in_specs=in_specs,
        out_specs=out_specs,
        out_shape=out_shape,
        scratch_shapes=[pltpu.VMEM((D_MODEL, D_MODEL), _BF)],
        compiler_params=_params(("arbitrary",)),
        name="outproj_route" if route else "outproj",
    )(*args)


def _ffn_kernel(nj, n_sub_rows, vis_e, vis_start, vis_cnt, used_sub,
                x_hbm, wg_ref, wu_ref, wd_ref, y_hbm,
                big, xb, wgb, wub, wdb, sem_in, sem_out):
    del vis_e
    v = pl.program_id(0)
    j = pl.program_id(1)
    cnt = vis_cnt[v]
    row0 = vis_start[v] * SUB

    x_is_bf16 = x_hbm.dtype == _BF
    landing = xb if x_is_bf16 else big

    def copy_in(s):
        r = pl.multiple_of(row0 + s * SUB, SUB)
        b = pl.multiple_of(s * SUB, SUB)
        return pltpu.make_async_copy(x_hbm.at[pl.ds(r, SUB)], landing.at[pl.ds(b, SUB)], sem_in.at[s])

    def copy_out(s):
        r = pl.multiple_of(row0 + s * SUB, SUB)
        b = pl.multiple_of(s * SUB, SUB)
        return pltpu.make_async_copy(big.at[pl.ds(b, SUB)], y_hbm.at[pl.ds(r, SUB)], sem_out)

    def for_subs(fn):
        lax.fori_loop(0, cnt, lambda s, c: (fn(s), c)[1], 0)

    @pl.when(jnp.logical_and(j == 0, cnt > 0))
    def _():
        for_subs(lambda s: copy_in(s).start())

    @pl.when(cnt > 0)
    def _():
        wgb[...] = wg_ref[...].astype(_BF)
        wub[...] = wu_ref[...].astype(_BF)
        wdb[...] = wd_ref[...].astype(_BF)

        def block(r0, nrows):
            s0 = r0 // SUB

            @pl.when(j == 0)
            def _():
                for t in range(nrows // SUB):
                    copy_in(s0 + t).wait()
                    rs = pl.ds(pl.multiple_of(r0 + t * SUB, SUB), SUB)
                    if not x_is_bf16:
                        xb[rs, :] = big[rs, :].astype(_BF)
                    big[rs, :] = jnp.zeros((SUB, D_MODEL), _F32)

            rs = pl.ds(r0, nrows)
            x = xb[rs, :]
            a = jnp.dot(x, wgb[...], preferred_element_type=_F32)
            b = jnp.dot(x, wub[...], preferred_element_type=_F32)
            p = (a * _sigmoid(a) * b).astype(_BF)
            big[rs, :] += jnp.dot(p, wdb[...], preferred_element_type=_F32)

            @pl.when(j == nj - 1)
            def _():
                for t in range(nrows // SUB):
                    copy_out(s0 + t).start()

        n_big = cnt // BLK_MIN
        rem = cnt - n_big * BLK_MIN
        room = BLK_SUBS - BLK_MIN
        grow = jnp.minimum(rem, n_big * room)
        left = rem - grow

        def one_block(i, r0):
            size = BLK_MIN + jnp.clip(grow - i * room, 0, room)
            for sz in range(BLK_MIN, BLK_SUBS + 1):
                pl.when(size == sz)(functools.partial(block, pl.multiple_of(r0, SUB), sz * SUB))
            return r0 + size * SUB
        r0 = lax.fori_loop(0, n_big, one_block, 0)
        bit = BLK_MIN // 2
        while bit >= 1:
            take = (left // bit) % 2 == 1
            pl.when(take)(functools.partial(block, pl.multiple_of(r0, SUB), bit * SUB))
            r0 = r0 + jnp.where(take, bit * SUB, 0)
            bit //= 2

    @pl.when(jnp.logical_and(j == nj - 1, cnt > 0))
    def _():
        for_subs(lambda s: copy_out(s).wait())

    if x_hbm.dtype == y_hbm.dtype:
        @pl.when(jnp.logical_and(v == pl.num_programs(0) - 1, j == nj - 1))
        def _():
            def tail(s):
                r = pl.multiple_of(s * SUB, SUB)
                return pltpu.make_async_copy(x_hbm.at[pl.ds(r, SUB)], y_hbm.at[pl.ds(r, SUB)], sem_out)
            lax.fori_loop(used_sub[0], n_sub_rows, lambda s, c: (tail(s).start(), c)[1], 0)
            lax.fori_loop(used_sub[0], n_sub_rows, lambda s, c: (tail(s).wait(), c)[1], 0)


def _grouped_ffn(x, wg, wu, wd, fc, vis_e, vis_start, vis_cnt, used_sub):
    ff = wg.shape[-1]
    nj = ff // fc
    nv = vis_e.shape[0]
    n_rows = x.shape[0]

    def chunk(v, j, vc):
        return jnp.where(vc[v] > 0, j, nj - 1)

    return pl.pallas_call(
        functools.partial(_ffn_kernel, nj, n_rows // SUB),
        grid_spec=pltpu.PrefetchScalarGridSpec(
            num_scalar_prefetch=4,
            grid=(nv, nj),
            in_specs=[
                pl.BlockSpec(memory_space=pl.ANY),
                pl.BlockSpec((None, D_MODEL, fc), lambda v, j, ve, vs, vc, us: (ve[v], 0, chunk(v, j, vc))),
                pl.BlockSpec((None, D_MODEL, fc), lambda v, j, ve, vs, vc, us: (ve[v], 0, chunk(v, j, vc))),
                pl.BlockSpec((None, fc, D_MODEL), lambda v, j, ve, vs, vc, us: (ve[v], chunk(v, j, vc), 0)),
            ],
            out_specs=pl.BlockSpec(memory_space=pl.ANY),
            scratch_shapes=[
                pltpu.VMEM((TMAX, D_MODEL), _F32),
                pltpu.VMEM((TMAX, D_MODEL), _BF),
                pltpu.VMEM((D_MODEL, fc), _BF),
                pltpu.VMEM((D_MODEL, fc), _BF),
                pltpu.VMEM((fc, D_MODEL), _BF),
                pltpu.SemaphoreType.DMA((SUB_MAX,)),
                pltpu.SemaphoreType.DMA(()),
            ],
        ),
        out_shape=jax.ShapeDtypeStruct((n_rows, D_MODEL), _F32),
        compiler_params=_params(("arbitrary", "arbitrary"), FFN_VMEM_LIMIT),
        name="ffn_%d" % ff,
    )(vis_e, vis_start, vis_cnt, used_sub, x, wg, wu, wd)


def _dense_visits():
    nv = T // TMAX
    return (jnp.zeros((nv,), jnp.int32),
            jnp.arange(nv, dtype=jnp.int32) * SUB_MAX,
            jnp.full((nv,), SUB_MAX, jnp.int32),
            jnp.full((1,), T // SUB, jnp.int32))


N_SLOT_SUB = -(-(TOP_K * T + N_TILES * N_EXPERTS * (SEG_ALIGN - 1)) // SUB) + N_EXPERTS
N_SLOT = N_SLOT_SUB * SUB
N_VISIT = -(-N_SLOT_SUB // SUB_MAX) + N_EXPERTS


def _routing_plan(eidx):
    e_loc = eidx.reshape(TOP_K, N_TILES, DISP_TILE).transpose(1, 0, 2).reshape(N_TILES, PAIRS)
    onehot = e_loc[:, :, None] == jnp.arange(N_EXPERTS, dtype=jnp.int32)[None, None, :]
    ch = 128
    oh = onehot.astype(_F32).reshape(N_TILES, PAIRS // ch, ch, N_EXPERTS)
    tri = (jnp.arange(ch)[:, None] >= jnp.arange(ch)[None, :]).astype(_F32)
    within = jnp.einsum("ij,tcjk->tcik", tri, oh)
    tot = within[:, :, -1, :]
    csum = (within + (jnp.cumsum(tot, axis=1) - tot)[:, :, None, :]).reshape(N_TILES, PAIRS, N_EXPERTS)
    csum = csum.astype(jnp.int32)
    n_te = (csum[:, -1, :] + SEG_ALIGN - 1) // SEG_ALIGN * SEG_ALIGN
    src = jnp.cumsum(n_te, axis=1) - n_te
    lpos = jnp.sum(jnp.where(onehot, csum - 1 + src[:, None, :], 0), axis=2).astype(jnp.int32)
    counts = jnp.sum(n_te, axis=0)
    nsub = (counts + SUB - 1) // SUB
    sub_base = jnp.cumsum(nsub) - nsub
    dst = (sub_base * SUB)[None, :] + jnp.cumsum(n_te, axis=0) - n_te
    seg = tuple(a.reshape(-1).astype(jnp.int32) for a in (n_te, src, dst))
    pads = ((sub_base * SUB + counts).astype(jnp.int32), (nsub * SUB - counts).astype(jnp.int32))
    used_sub = jnp.sum(nsub).reshape(1).astype(jnp.int32)
    nvis = (nsub + SUB_MAX - 1) // SUB_MAX
    vend = jnp.cumsum(nvis)
    total = vend[-1]
    vid = jnp.arange(N_VISIT, dtype=jnp.int32)
    ve = jnp.minimum(jnp.sum((vid[:, None] >= vend[None, :]).astype(jnp.int32), axis=1), N_EXPERTS - 1)
    local = vid - (vend - nvis)[ve]
    nv_e = jnp.maximum(nvis[ve], 1)
    q, r = nsub[ve] // nv_e, nsub[ve] % nv_e
    cnt = q + (local < r).astype(jnp.int32)
    start = sub_base[ve] + local * q + jnp.minimum(local, r)
    used = vid < total
    last_e = ve[jnp.maximum(total - 1, 0)]
    vis_e = jnp.where(used, ve, last_e).astype(jnp.int32)
    vis_cnt = jnp.where(used, cnt, 0).astype(jnp.int32)
    vis_start = jnp.where(used, start, 0).astype(jnp.int32)
    return lpos.reshape(N_TILES, TOP_K, DISP_TILE), seg, pads, (vis_e, vis_start, vis_cnt, used_sub)


def _for_pow2_pieces(n, max_piece, fn):
    off = 0
    bit = max_piece
    while bit >= SEG_ALIGN:
        take = (n // bit) % 2 == 1
        pl.when(take)(functools.partial(fn, off, bit))
        off = off + jnp.where(take, bit, 0)
        bit //= 2


def _segment_copies(tile, seg, make_copy, wait=False):
    n_te, src, dst = seg
    for e in range(N_EXPERTS):
        j = tile * N_EXPERTS + e
        s0, d0 = src[j], dst[j]

        def piece(off, size, s0=s0, d0=d0):
            cp = make_copy(pl.multiple_of(s0 + off, SEG_ALIGN), pl.multiple_of(d0 + off, SEG_ALIGN), size)
            cp.wait() if wait else cp.start()
        _for_pow2_pieces(n_te[j], DISP_TILE, piece)


def _dispatch_kernel(n_te, src, dst, pad_start, pad_cnt, used_sub, h_ref, lpos_ref, xs_hbm, ring, ring_sems, sem):
    i = pl.program_id(0)
    last = pl.num_programs(0) - 1
    b = i % 2

    def copies(tile, slot_id, wait):
        _segment_copies(tile, (n_te, src, dst), lambda s, d, size: pltpu.make_async_copy(
            ring.at[slot_id, pl.ds(s, size)], xs_hbm.at[pl.ds(d, size)], ring_sems.at[slot_id]), wait)

    pl.when(i >= 2)(lambda: copies(i - 2, b, True))
    lp = lpos_ref[...]
    srow = lax.broadcasted_iota(jnp.int32, (SORT_ROWS, DISP_TILE), 0)
    perm = jnp.where(lp[0:1, :] == srow, 1.0, jnp.where(lp[1:2, :] == srow, 1.0, 0.0)).astype(_BF)
    ring[b] = jnp.dot(perm, h_ref[...].astype(_BF), preferred_element_type=_F32)
    copies(i, b, False)

    @pl.when(i == last)
    def _():
        copies(i - 1, 1 - b, True)
        copies(i, b, True)
        for e in range(N_EXPERTS):
            p0 = pad_start[e]

            def fill(off, size, p0=p0):
                cp = pltpu.make_async_copy(ring.at[b, pl.ds(0, size)],
                                           xs_hbm.at[pl.ds(pl.multiple_of(p0 + off, SEG_ALIGN), size)], sem)
                cp.start()
                cp.wait()
            _for_pow2_pieces(pad_cnt[e], SUB // 2, fill)

        def tail(s):
            r = pl.multiple_of(s * SUB, SUB)
            return pltpu.make_async_copy(ring.at[b, pl.ds(0, SUB)], xs_hbm.at[pl.ds(r, SUB)], sem)
        lax.fori_loop(used_sub[0], N_SLOT_SUB, lambda s, c: (tail(s).start(), c)[1], 0)
        lax.fori_loop(used_sub[0], N_SLOT_SUB, lambda s, c: (tail(s).wait(), c)[1], 0)


def _dispatch(h, lpos, seg, pads, used_sub):
    return pl.pallas_call(
        _dispatch_kernel,
        grid_spec=pltpu.PrefetchScalarGridSpec(
            num_scalar_prefetch=6,
            grid=(N_TILES,),
            in_specs=[pl.BlockSpec((DISP_TILE, D_MODEL), lambda i, *_: (i, 0)),
                      pl.BlockSpec((None, TOP_K, DISP_TILE), lambda i, *_: (i, 0, 0))],
            out_specs=pl.BlockSpec(memory_space=pl.ANY),
            scratch_shapes=[pltpu.VMEM((2, SORT_ROWS, D_MODEL), _F32),
                            pltpu.SemaphoreType.DMA((2,)), pltpu.SemaphoreType.DMA(())],
        ),
        out_shape=jax.ShapeDtypeStruct((N_SLOT, D_MODEL), _F32),
        compiler_params=_params(("arbitrary",)),
        name="dispatch",
    )(*seg, *pads, used_sub, h, lpos)


def _residual_out(x, gate, f, g_ref, o_refs, i, tile):
    out = x + gate * f
    if g_ref is not None:
        ms = jnp.mean(out * out, axis=-1, keepdims=True)
        out = out * lax.rsqrt(ms + EPS) * g_ref[...]
    if len(o_refs) == 1:
        o_refs[0][...] = out
    else:
        @pl.when(i < T_PROMPT // tile)
        def _():
            o_refs[0][...] = out

        @pl.when(i >= T_PROMPT // tile)
        def _():
            o_refs[1][...] = out


def _out_specs(final, tile):
    if not final:
        return ([pl.BlockSpec((tile, D_MODEL), lambda i, *_: (i, 0))],
                [jax.ShapeDtypeStruct((T, D_MODEL), _F32)])
    return (_half_specs(D_MODEL, tile),
            [jax.ShapeDtypeStruct((T_PROMPT, D_MODEL), _F32), jax.ShapeDtypeStruct((T_SAMPLE, D_MODEL), _F32)])


def _combine_kernel(final, x_ref, mod_ref, y_ref, *rest):
    g_ref, o_refs = (rest[0], rest[1:]) if final else (None, rest)
    _residual_out(x_ref[...], mod_ref[5:6, :], y_ref[...], g_ref, o_refs, pl.program_id(0), ROW_TILE)


def _combine(x, l, mods, y, g_final=None):
    final = g_final is not None
    row = lambda i: (i, 0)
    in_specs = [
        pl.BlockSpec((ROW_TILE, D_MODEL), row),
        _layer_specs(l, ())[0],
        pl.BlockSpec((ROW_TILE, D_MODEL), row),
    ]
    args = [x, mods, y]
    if final:
        in_specs.append(pl.BlockSpec((1, D_MODEL), lambda i: (0, 0)))
        args.append(g_final.reshape(1, D_MODEL))
    out_specs, out_shape = _out_specs(final, ROW_TILE)
    return pl.pallas_call(
        functools.partial(_combine_kernel, final),
        grid=(T // ROW_TILE,),
        in_specs=in_specs,
        out_specs=out_specs,
        out_shape=out_shape,
        compiler_params=_params(("arbitrary",)),
        name="combine%s" % ("_final" if final else ""),
    )(*args)


def _combine_top2_kernel(final, n_out, n_te, src, dst, x_ref, mod_ref, w_ref, lpos_ref, *rest):
    g_ref, rest = (rest[0], rest[1:]) if final else (None, rest)
    ys_hbm, o_refs, (ybuf, sems) = rest[0], rest[1:1 + n_out], rest[1 + n_out:]
    i = pl.program_id(0)

    def fetch(tile, b, wait=False):
        _segment_copies(tile, (n_te, src, dst), lambda s, d, size: pltpu.make_async_copy(
            ys_hbm.at[pl.ds(d, size)], ybuf.at[b, pl.ds(s, size)], sems.at[b]), wait)

    @pl.when(i == 0)
    def _():
        for slot_id in range(2):
            ybuf[slot_id, PAIRS:SORT_ROWS, :] = jnp.zeros((SORT_ROWS - PAIRS, D_MODEL), _F32)
        fetch(0, 0)

    @pl.when(i + 1 < pl.num_programs(0))
    def _():
        fetch(i + 1, (i + 1) % 2)

    b = i % 2
    fetch(i, b, wait=True)
    y = ybuf[b].astype(_BF)
    lp = lpos_ref[...]
    scol = lax.broadcasted_iota(jnp.int32, (DISP_TILE, SORT_ROWS), 1)
    w = w_ref[...]
    pick = jnp.zeros((DISP_TILE, SORT_ROWS), _F32)
    for k in range(TOP_K):
        pick = jnp.where(lp[:, k:k + 1] == scol, w[:, k:k + 1], pick)
    f = jnp.dot(pick.astype(_BF), y, preferred_element_type=_F32)
    _residual_out(x_ref[...], mod_ref[5:6, :], f, g_ref, o_refs, i, DISP_TILE)


def _combine_top2(x, l, mods, ys, lpos_t, seg, w, g_final=None):
    final = g_final is not None
    row = lambda i, *_: (i, 0)
    in_specs = [
        pl.BlockSpec((DISP_TILE, D_MODEL), row),
        pl.BlockSpec((None, None, 6, D_MODEL), lambda i, *_: (l, _cond_of_tile(i, DISP_TILE), 0, 0)),
        pl.BlockSpec((DISP_TILE, TOP_K), row),
        pl.BlockSpec((None, DISP_TILE, TOP_K), lambda i, *_: (i, 0, 0)),
    ]
    args = [x, mods, w, lpos_t]
    if final:
        in_specs.append(pl.BlockSpec((1, D_MODEL), lambda i, *_: (0, 0)))
        args.append(g_final.reshape(1, D_MODEL))
    in_specs.append(pl.BlockSpec(memory_space=pl.ANY))
    args.append(ys)
    out_specs, out_shape = _out_specs(final, DISP_TILE)
    return pl.pallas_call(
        functools.partial(_combine_top2_kernel, final, len(out_specs)),
        grid_spec=pltpu.PrefetchScalarGridSpec(
            num_scalar_prefetch=3,
            grid=(N_TILES,),
            in_specs=in_specs,
            out_specs=out_specs,
            scratch_shapes=[
                pltpu.VMEM((2, SORT_ROWS, D_MODEL), _F32),
                pltpu.SemaphoreType.DMA((2,)),
            ],
        ),
        out_shape=out_shape,
        compiler_params=_params(("arbitrary",)),
        name="combine_top2%s" % ("_final" if final else ""),
    )(*seg, *args)


def kernel(x_prompt, x_sample, cache_k, cache_v, c, c_ctx, w_ada, b_ada, g_norm_mix, g_norm_ffn,
           w_in, w_out, attn_sink, conv_dw, conv_dw_b, conv_ln_g, conv_ln_b, conv_pw, fnet_w,
           ffn_w_gate, ffn_w_up, ffn_w_down, moe_router, moe_w_gate, moe_w_up, moe_w_down, g_final):
    xs = (x_prompt.reshape(T_PROMPT, D_MODEL), x_sample.reshape(T_SAMPLE, D_MODEL))
    cond8 = jnp.concatenate([c_ctx[None, :], c, jnp.zeros((N_COND - 1 - DEC_BATCH, D_MODEL), _F32)], axis=0)
    mods = _modulation(cond8, w_ada, b_ada).reshape(DEPTH, N_COND, 6, D_MODEL)
    cos_t, sin_t = _rope_tables()
    ck_all = cache_k.reshape(DEC_BATCH, DEPTH, PAST_LEN, KV_W)
    cv_all = cache_v.reshape(DEC_BATCH, DEPTH, PAST_LEN, KV_W)
    p_blocks = T_PROMPT // DEC_SEQ
    sinks = attn_sink.reshape(DEPTH * N_HEADS)

    ks, vs = [], []
    resid = None
    for l in range(DEPTH):
        q, kp, vp, kl, vl, uc, uf, *x_new = _inproj(xs, l, mods, g_norm_mix, w_in, cos_t, sin_t, resid=resid)
        if resid is not None:
            xs, resid = tuple(x_new), None
        ks.append(kp)
        vs.append(vp)
        attn = (_ctx_attention(l, sinks, q, kp, vp),
                _lat_attention(l, sinks, q, kl, vl, ck_all, cv_all))
        cargs = (l, conv_dw, conv_dw_b, conv_ln_g, conv_ln_b, conv_pw)
        conv = (_conv_module(uc, SEQ, BATCH, 0, *cargs),
                _conv_module(uc, DEC_SEQ, DEC_BATCH, p_blocks, *cargs))
        four = (_fourier_mix(uf, SEQ, BATCH, 0, l, fnet_w),
                _fourier_mix(uf, DEC_SEQ, DEC_BATCH, T_PROMPT // 2, l, fnet_w))
        last = g_final if l == DEPTH - 1 else None
        i = l // 2
        if l % 2 == 0:
            x, h = _outproj((attn, conv, four), xs, l, mods, w_out, g_norm_ffn)
            y = _grouped_ffn(h, ffn_w_gate[i:i + 1], ffn_w_up[i:i + 1], ffn_w_down[i:i + 1], 256,
                             *_dense_visits())
            if last is None:
                xs, resid = (x,), (y, l)
            else:
                xs = tuple(_combine(x, l, mods, y, g_final=last))
        else:
            x, h, eidx, ew = _outproj((attn, conv, four), xs, l, mods, w_out, g_norm_ffn,
                                      router_t=moe_router[i].T)
            lpos, seg, pads, visits = _routing_plan(eidx)
            xd = _dispatch(h, lpos, seg, pads, visits[-1])
            ys = _grouped_ffn(xd, moe_w_gate[i], moe_w_up[i], moe_w_down[i], 512, *visits)
            xs = tuple(_combine_top2(x, l, mods, ys, lpos.transpose(0, 2, 1), seg, ew.T, g_final=last))

    y_prompt = xs[0].reshape(BATCH, SEQ, D_MODEL)
    y_sample = xs[1].reshape(DEC_BATCH, DEC_SEQ, D_MODEL)
    state_k = jnp.stack([a.reshape(BATCH, SEQ, N_KV_HEADS, HEAD_DIM) for a in ks], axis=1)
    state_v = jnp.stack([a.reshape(BATCH, SEQ, N_KV_HEADS, HEAD_DIM) for a in vs], axis=1)
    return (y_prompt, y_sample, state_k, state_v)
```

```python
import functools

import numpy as np
import jax
import jax.numpy as jnp
from jax import lax
from jax.experimental import pallas as pl
from jax.experimental.pallas import tpu as pltpu

D_MODEL = 1024
BATCH = 16
SEQ = 256
DEPTH = 2
DEC_BATCH = 2
DEC_SEQ = 2048
PAST_LEN = 512
GRID_W = 64
HEAD_DIM = 64
N_HEADS = 8
N_KV_HEADS = 2
GQA_GROUP = N_HEADS // N_KV_HEADS
ATTN_W = N_HEADS * HEAD_DIM
KV_W = N_KV_HEADS * HEAD_DIM
WINDOW = 128
BLOCK = 128
ROPE_THETA = 10000.0
CONV_W = D_MODEL // 4
CONV_K = 31
FNET_GROUPS = 4
FNET_W = D_MODEL // 4
FNET_GW = FNET_W // FNET_GROUPS
IN_WIDTH = ATTN_W + 2 * KV_W + 2 * CONV_W + FNET_W
D_FF = 2816
N_EXPERTS = 8
TOP_K = 2
D_FF_EXPERT = 3584
EPS = 1e-6
NEG = -1e30

T_PROMPT = BATCH * SEQ
T_SAMPLE = DEC_BATCH * DEC_SEQ
T = T_PROMPT + T_SAMPLE
N_COND = 8

ROW_TILE = 512
SUB = 64
SUB_MAX = 64
TMAX = SUB * SUB_MAX
BLK_MIN = 16
BLK_SUBS = 20
DISP_TILE = 512
N_TILES = T // DISP_TILE
PAIRS = TOP_K * DISP_TILE
SEG_ALIGN = 8
SORT_ROWS = -(-(PAIRS + N_EXPERTS * (SEG_ALIGN - 1)) // 128) * 128
CONV_CHUNK = 256
CONV_PAD = 16
CONV_SPAN = CONV_CHUNK + 8 * ((CONV_PAD + CONV_K // 2) // 8)
VMEM_LIMIT = 48 * 1024 * 1024
FFN_VMEM_LIMIT = 56 * 1024 * 1024

_BF = jnp.bfloat16
_F32 = jnp.float32


def _cond_of_tile(i, tile):
    r = i * tile
    return jnp.where(r < T_PROMPT, 0, 1 + (r - T_PROMPT) // DEC_SEQ)


def _params(sem, vmem=VMEM_LIMIT):
    return pltpu.CompilerParams(dimension_semantics=sem, vmem_limit_bytes=vmem)


def _sigmoid(x):
    return 1.0 / (1.0 + jnp.exp(-x))


def _mod_kernel(cond_ref, w_ref, b_ref, o_ref):
    cnd = cond_ref[...]
    s = (cnd * _sigmoid(cnd)).astype(_BF)
    o_ref[...] = jnp.dot(s, w_ref[...].astype(_BF), preferred_element_type=_F32) + b_ref[...]


def _modulation(cond8, w_ada, b_ada):
    nt = 1536
    return pl.pallas_call(
        _mod_kernel,
        grid=(DEPTH, 6 * D_MODEL // nt),
        in_specs=[
            pl.BlockSpec((N_COND, D_MODEL), lambda l, n: (0, 0)),
            pl.BlockSpec((None, D_MODEL, nt), lambda l, n: (l, 0, n)),
            pl.BlockSpec((None, 1, nt), lambda l, n: (l, 0, n)),
        ],
        out_specs=pl.BlockSpec((None, N_COND, nt), lambda l, n: (l, 0, n)),
        out_shape=jax.ShapeDtypeStruct((DEPTH, N_COND, 6 * D_MODEL), _F32),
        compiler_params=_params(("arbitrary", "arbitrary")),
        name="modulation",
    )(cond8, w_ada, b_ada.reshape(DEPTH, 1, 6 * D_MODEL))


def _norm_mod(x, g, shift, scale):
    ms = jnp.mean(x * x, axis=-1, keepdims=True)
    y = x * lax.rsqrt(ms + EPS) * g
    return y * (1.0 + scale) + shift


def _rope_tables():
    rows = DEC_SEQ // GRID_W
    n_freq = HEAD_DIM // 4
    inv = ROPE_THETA ** (-jnp.arange(n_freq, dtype=_F32) / n_freq)
    gr, gc = jnp.meshgrid(jnp.arange(rows, dtype=_F32), jnp.arange(GRID_W, dtype=_F32), indexing="ij")
    ang_r = gr.reshape(-1)[:, None] * inv
    ang_c = gc.reshape(-1)[:, None] * inv
    cr, sr, cc, sc = jnp.cos(ang_r), jnp.sin(ang_r), jnp.cos(ang_c), jnp.sin(ang_c)
    cos64 = jnp.concatenate([cr, cr, cc, cc], axis=-1)
    sin64 = jnp.concatenate([-sr, sr, -sc, sc], axis=-1)
    return jnp.tile(cos64, (1, 2)), jnp.tile(sin64, (1, 2))


def _rope128(x, cos, sin):
    lane = lax.broadcasted_iota(jnp.int32, x.shape, 1)
    first = (lane % 32) < 16
    partner = jnp.where(first, pltpu.roll(x, 128 - 16, 1), pltpu.roll(x, 16, 1))
    return x * cos + partner * sin


def _half_specs(width, tile):
    pt = T_PROMPT // tile
    return [pl.BlockSpec((tile, width), lambda i, *_: (jnp.minimum(i, pt - 1), 0)),
            pl.BlockSpec((tile, width), lambda i, *_: (jnp.maximum(i - pt, 0), 0))]


def _pick_half(i, tile, a_ref, b_ref):
    return jnp.where(i < T_PROMPT // tile, a_ref[...], b_ref[...])


def _layer_specs(l, w_shape):
    return [
        pl.BlockSpec((None, None, 6, D_MODEL), lambda i, *_: (l, _cond_of_tile(i, ROW_TILE), 0, 0)),
        pl.BlockSpec((None, 1, D_MODEL), lambda i, *_: (l, 0, 0)),
        pl.BlockSpec((None,) + w_shape, lambda i, *_: (l, 0, 0)),
    ]


def _inproj_kernel(split, resid, *refs):
    n_x = 2 if split else 1
    x_refs, refs = refs[:n_x], refs[n_x:]
    if resid:
        y_ref, modp_ref = refs[:2]
        refs = refs[2:]
    mod_ref, g_ref, w_ref, cos_ref, sin_ref = refs[:5]
    q_ref, kp_ref, vp_ref, ks_ref, vs_ref, uc_ref, uf_ref = refs[5:12]
    wb_ref, uf_scr = refs[-2:]
    i = pl.program_id(0)

    @pl.when(i == 0)
    def _():
        wb_ref[...] = w_ref[...].astype(_BF)

    x = _pick_half(i, ROW_TILE, *x_refs) if split else x_refs[0][...]
    if resid:
        x = x + modp_ref[5:6, :] * y_ref[...]
        refs[12][...] = x
    h = _norm_mod(x, g_ref[...], mod_ref[0:1, :], mod_ref[1:2, :]).astype(_BF)
    latent = i >= T_PROMPT // ROW_TILE
    cos = jnp.where(latent, cos_ref[...], 1.0)
    sin = jnp.where(latent, sin_ref[...], 0.0)

    def proj(o, n):
        return jnp.dot(h, wb_ref[:, o:o + n], preferred_element_type=_F32)

    o = 0
    for c in range(ATTN_W // 256):
        qq = proj(o, 256)
        q_ref[:, o:o + 128] = _rope128(qq[:, :128], cos, sin).astype(q_ref.dtype)
        q_ref[:, o + 128:o + 256] = _rope128(qq[:, 128:], cos, sin).astype(q_ref.dtype)
        o += 256
    kv = proj(o, 2 * KV_W)
    k = _rope128(kv[:, :KV_W], cos, sin)
    o += 2 * KV_W
    uc_ref[...] = proj(o, 2 * CONV_W)
    o += 2 * CONV_W
    uf = proj(o, FNET_W)
    n_lane = FNET_W // 128
    for c in range(n_lane):
        uf_scr[c] = uf[:, c * 128:(c + 1) * 128]
    half = ROW_TILE // 2
    uf_ref[...] = jnp.concatenate([uf_scr[c, pl.ds(par, half, stride=2), :]
                                   for par in range(2) for c in range(n_lane)], axis=-1).astype(uf_ref.dtype)

    @pl.when(jnp.logical_not(latent))
    def _():
        kp_ref[...] = k
        vp_ref[...] = kv[:, KV_W:]

    @pl.when(latent)
    def _():
        ks_ref[...] = k
        vs_ref[...] = kv[:, KV_W:]


def _inproj(xs, l, mods, g_all, w_all, cos_t, sin_t, resid=None):
    split = len(xs) == 2
    pt = T_PROMPT // ROW_TILE
    per_seq = DEC_SEQ // ROW_TILE

    def rope_idx(i):
        return (jnp.maximum(i - pt, 0) % per_seq, 0)

    row = lambda i: (i, 0)
    in_specs = _half_specs(D_MODEL, ROW_TILE) if split else [pl.BlockSpec((ROW_TILE, D_MODEL), row)]
    args = list(xs)
    kv_ctx, kv_lat = _half_specs(KV_W, ROW_TILE)
    out_specs = [pl.BlockSpec((ROW_TILE, ATTN_W), row), kv_ctx, kv_ctx, kv_lat, kv_lat,
                 pl.BlockSpec((ROW_TILE, 2 * CONV_W), row), pl.BlockSpec((ROW_TILE // 2, 2 * FNET_W), row)]
    out_shape = [jax.ShapeDtypeStruct((T, ATTN_W), _BF),
                 jax.ShapeDtypeStruct((T_PROMPT, KV_W), _F32), jax.ShapeDtypeStruct((T_PROMPT, KV_W), _F32),
                 jax.ShapeDtypeStruct((T_SAMPLE, KV_W), _F32), jax.ShapeDtypeStruct((T_SAMPLE, KV_W), _F32),
                 jax.ShapeDtypeStruct((T, 2 * CONV_W), _F32), jax.ShapeDtypeStruct((T // 2, 2 * FNET_W), _BF)]
    if resid is not None:
        y, l_prev = resid
        in_specs += [pl.BlockSpec((ROW_TILE, D_MODEL), row), _layer_specs(l_prev, ())[0]]
        args += [y, mods]
        out_specs.append(pl.BlockSpec((ROW_TILE, D_MODEL), row))
        out_shape.append(jax.ShapeDtypeStruct((T, D_MODEL), _F32))
    return pl.pallas_call(
        functools.partial(_inproj_kernel, split, resid is not None),
        grid=(T // ROW_TILE,),
        in_specs=in_specs + _layer_specs(l, (D_MODEL, IN_WIDTH)) + [
            pl.BlockSpec((ROW_TILE, 128), rope_idx),
            pl.BlockSpec((ROW_TILE, 128), rope_idx),
        ],
        out_specs=out_specs,
        out_shape=out_shape,
        scratch_shapes=[pltpu.VMEM((D_MODEL, IN_WIDTH), _BF), pltpu.VMEM((FNET_W // 128, ROW_TILE, 128), _F32)],
        compiler_params=_params(("arbitrary",)),
        name="inproj_resid" if resid is not None else "inproj",
    )(*args, mods, g_all.reshape(DEPTH, 1, D_MODEL), w_all, cos_t, sin_t)


assert 2 * HEAD_DIM == 128 and KV_W == 128 and GQA_GROUP == 4


def _kv_operands(x):
    return x.astype(_BF), pltpu.roll(x, HEAD_DIM, 1).astype(_BF)


def _half_mask(shape, par):
    return (lax.broadcasted_iota(jnp.int32, shape, 1) >= HEAD_DIM) == (par == 1)


def _stack_chunks(q, kh, rows):
    c0 = 2 * kh
    return jnp.concatenate([q[:, c0 * 128:(c0 + 1) * 128], q[:, (c0 + 1) * 128:(c0 + 2) * 128]], axis=0)


def _sink_column(sink_ref, heads, rows):
    r = lax.broadcasted_iota(jnp.int32, (len(heads) * rows, 1), 0)
    col = jnp.full((len(heads) * rows, 1), sink_ref[heads[0]], _F32)
    for g in range(1, len(heads)):
        col = jnp.where(r >= g * rows, sink_ref[heads[g]], col)
    return col


def _stack_heads(q, kh, rows):
    parts = [q[:, (kh * GQA_GROUP + g) * HEAD_DIM:(kh * GQA_GROUP + g + 1) * HEAD_DIM] for g in range(GQA_GROUP)]
    return jnp.concatenate(parts, axis=0)


def _store_heads(o_ref, o, kh, rows):
    for pair in range(GQA_GROUP // 2):
        a = o[(2 * pair) * rows:(2 * pair + 1) * rows]
        b = o[(2 * pair + 1) * rows:(2 * pair + 2) * rows]
        c0 = (kh * GQA_GROUP + 2 * pair) * HEAD_DIM
        o_ref[:, c0:c0 + 2 * HEAD_DIM] = jnp.concatenate([a, b], axis=-1).astype(o_ref.dtype)


def _store_chunks(o_ref, o, kh, rows):
    c0 = 2 * kh
    o_ref[:, c0 * 128:(c0 + 1) * 128] = o[:rows].astype(o_ref.dtype)
    o_ref[:, (c0 + 1) * 128:(c0 + 2) * 128] = o[rows:].astype(o_ref.dtype)


_NT = (((1,), (1,)), ((), ()))


def _ctx_attn_kernel(l, sink_ref, q_ref, k_ref, v_ref, o_ref):
    scale = HEAD_DIM ** -0.5
    q = q_ref[...] * scale
    k_ops = _kv_operands(k_ref[...])
    v_ops = _kv_operands(v_ref[...])
    for kh in range(N_KV_HEADS):
        qs = _stack_chunks(q, kh, SEQ).astype(_BF)
        o = None
        for par in range(2):
            own = _half_mask(qs.shape, par)
            which = 0 if kh == par else 1
            s = lax.dot_general(jnp.where(own, qs, 0), k_ops[which], _NT, preferred_element_type=_F32)
            h0 = l * N_HEADS + kh * GQA_GROUP + par
            sink = _sink_column(sink_ref, (h0, h0 + 2), SEQ)
            m = jnp.maximum(jnp.max(s, axis=-1, keepdims=True), sink)
            p = jnp.exp(s - m)
            den = jnp.sum(p, axis=-1, keepdims=True) + jnp.exp(sink - m)
            t = jnp.where(own, jnp.dot(p.astype(_BF), v_ops[which], preferred_element_type=_F32) / den, 0.0)
            o = t if o is None else o + t
        _store_chunks(o_ref, o, kh, SEQ)


def _ctx_attention(l, sinks, q, k, v):
    row = lambda b, s: (b, 0)
    return pl.pallas_call(
        functools.partial(_ctx_attn_kernel, l),
        grid_spec=pltpu.PrefetchScalarGridSpec(
            num_scalar_prefetch=1,
            grid=(BATCH,),
            in_specs=[
                pl.BlockSpec((SEQ, ATTN_W), row),
                pl.BlockSpec((SEQ, KV_W), row),
                pl.BlockSpec((SEQ, KV_W), row),
            ],
            out_specs=pl.BlockSpec((SEQ, ATTN_W), row),
        ),
        out_shape=jax.ShapeDtypeStruct((T_PROMPT, ATTN_W), _BF),
        compiler_params=_params(("arbitrary",)),
        name="ctx_attention",
    )(sinks, q, k, v)


def _lat_attn_kernel(l, sink_ref, q_ref, k_ref, v_ref, ck_ref, cv_ref, o_ref):
    n = pl.program_id(1)
    scale = HEAD_DIM ** -0.5
    nb = DEC_SEQ // BLOCK
    band = 3 * BLOCK
    start = pl.multiple_of(jnp.clip(n - 1, 0, nb - 3) * BLOCK, BLOCK)
    q = q_ref[...] * scale
    kb = k_ref[pl.ds(start, band), :]
    vb = v_ref[pl.ds(start, band), :]
    dpos = (lax.broadcasted_iota(jnp.int32, (BLOCK, band), 1) - lax.broadcasted_iota(jnp.int32, (BLOCK, band), 0)
            + (start - n * BLOCK))
    in_window = jnp.where(jnp.abs(dpos) <= WINDOW, 0.0, NEG)
    bias = jnp.concatenate([in_window] * GQA_GROUP, axis=0)
    for kh in range(N_KV_HEADS):
        hs = slice(kh * HEAD_DIM, (kh + 1) * HEAD_DIM)
        qs = _stack_heads(q, kh, BLOCK).astype(_BF)
        s_loc = lax.dot_general(qs, kb[:, hs].astype(_BF), _NT, preferred_element_type=_F32)
        s_loc = jnp.where(bias < 0.0, NEG, s_loc)
        s_ctx = lax.dot_general(qs, ck_ref[:, hs].astype(_BF), _NT, preferred_element_type=_F32)
        h0 = l * N_HEADS + kh * GQA_GROUP
        sink = _sink_column(sink_ref, tuple(range(h0, h0 + GQA_GROUP)), BLOCK)
        m = jnp.maximum(jnp.maximum(jnp.max(s_loc, axis=-1, keepdims=True),
                                    jnp.max(s_ctx, axis=-1, keepdims=True)), sink)
        p_loc = jnp.exp(s_loc - m)
        p_ctx = jnp.exp(s_ctx - m)
        den = (jnp.sum(p_loc, axis=-1, keepdims=True) + jnp.sum(p_ctx, axis=-1, keepdims=True)
               + jnp.exp(sink - m))
        o = (jnp.dot(p_loc.astype(_BF), vb[:, hs].astype(_BF), preferred_element_type=_F32)
             + jnp.dot(p_ctx.astype(_BF), cv_ref[:, hs].astype(_BF), preferred_element_type=_F32)) / den
        _store_heads(o_ref, o, kh, BLOCK)


def _lat_attention(l, sinks, q, k, v, ck, cv):
    nb = DEC_SEQ // BLOCK
    q0 = T_PROMPT // BLOCK
    return pl.pallas_call(
        functools.partial(_lat_attn_kernel, l),
        grid_spec=pltpu.PrefetchScalarGridSpec(
            num_scalar_prefetch=1,
            grid=(DEC_BATCH, nb),
            in_specs=[
                pl.BlockSpec((BLOCK, ATTN_W), lambda b, n, s: (q0 + b * nb + n, 0)),
                pl.BlockSpec((DEC_SEQ, KV_W), lambda b, n, s: (b, 0)),
                pl.BlockSpec((DEC_SEQ, KV_W), lambda b, n, s: (b, 0)),
                pl.BlockSpec((None, None, PAST_LEN, KV_W), lambda b, n, s: (b, l, 0, 0)),
                pl.BlockSpec((None, None, PAST_LEN, KV_W), lambda b, n, s: (b, l, 0, 0)),
            ],
            out_specs=pl.BlockSpec((BLOCK, ATTN_W), lambda b, n, s: (b * nb + n, 0)),
        ),
        out_shape=jax.ShapeDtypeStruct((T_SAMPLE, ATTN_W), _BF),
        compiler_params=_params(("arbitrary", "arbitrary")),
        name="lat_attention",
    )(sinks, q, k, v, ck, cv)


def _conv_kernel(seq, u_ref, dw_ref, dwb_ref, lg_ref, lb_ref, pw_ref, o_ref, pad_ref, y_ref, sh_ref):
    u = u_ref[...]
    pad_ref[0:CONV_PAD, :] = jnp.zeros((CONV_PAD, CONV_W), _F32)
    pad_ref[CONV_PAD + seq:2 * CONV_PAD + seq, :] = jnp.zeros((CONV_PAD, CONV_W), _F32)
    pad_ref[CONV_PAD:CONV_PAD + seq, :] = u[:, :CONV_W] * _sigmoid(u[:, CONV_W:])
    off = CONV_PAD - CONV_K // 2
    span = CONV_SPAN

    def chunk(c, carry):
        row = pl.multiple_of(c * CONV_CHUNK, CONV_CHUNK)
        win = pad_ref[pl.ds(row, CONV_CHUNK + 2 * CONV_PAD), :]
        acc = jnp.zeros((CONV_CHUNK, CONV_W), _F32) + dwb_ref[...]
        for phase in range(8):
            sh_ref[phase] = win[phase:phase + span, :]
        for phase in range(8):
            for a in range(span // 8):
                t = 8 * a + phase - off
                if 0 <= t < CONV_K:
                    acc = acc + sh_ref[phase, 8 * a:8 * a + CONV_CHUNK, :] * dw_ref[t:t + 1, :]
        mu = jnp.mean(acc, axis=-1, keepdims=True)
        d = acc - mu
        var = jnp.mean(d * d, axis=-1, keepdims=True)
        y = d * lax.rsqrt(var + EPS) * lg_ref[...] + lb_ref[...]
        y_ref[pl.ds(row, CONV_CHUNK), :] = (y * _sigmoid(y)).astype(_BF)
        return carry

    lax.fori_loop(0, seq // CONV_CHUNK, chunk, 0)
    o_ref[...] = jnp.dot(y_ref[...], pw_ref[...].astype(_BF), preferred_element_type=_F32).astype(o_ref.dtype)


def _conv_module(uc, seq, nbatch, block0, l, dw, dwb, lg, lb, pw):
    vec = lambda a: a.reshape(DEPTH, 1, CONV_W)
    layer = lambda b: (l, 0, 0)
    return pl.pallas_call(
        functools.partial(_conv_kernel, seq),
        grid=(nbatch,),
        in_specs=[
            pl.BlockSpec((seq, 2 * CONV_W), lambda b: (block0 + b, 0)),
            pl.BlockSpec((None, CONV_K, CONV_W), layer),
            pl.BlockSpec((None, 1, CONV_W), layer),
            pl.BlockSpec((None, 1, CONV_W), layer),
            pl.BlockSpec((None, 1, CONV_W), layer),
            pl.BlockSpec((None, CONV_W, CONV_W), layer),
        ],
        out_specs=pl.BlockSpec((seq, CONV_W), lambda b: (b, 0)),
        out_shape=jax.ShapeDtypeStruct((nbatch * seq, CONV_W), _BF),
        scratch_shapes=[pltpu.VMEM((seq + 2 * CONV_PAD, CONV_W), _F32), pltpu.VMEM((seq, CONV_W), _BF),
                        pltpu.VMEM((8, CONV_SPAN, CONV_W), _F32)],
        compiler_params=_params(("arbitrary",)),
        name="conv_module_%d" % seq,
    )(uc, dw, vec(dwb), vec(lg), vec(lb), pw)


def _dft_constants(seq):
    half = seq // 2
    j = np.arange(half, dtype=np.int64)
    ang = 2.0 * np.pi * ((j[:, None] * j[None, :]) % half).astype(np.float64) / half
    m = np.concatenate([np.cos(ang), -np.sin(ang)], axis=1) / np.sqrt(seq)
    tw = np.pi * j.astype(np.float64) / half
    cb = np.repeat(np.cos(tw)[:, None], FNET_W, axis=1)
    sb = np.repeat(np.sin(tw)[:, None], FNET_W, axis=1)
    c = np.arange(FNET_GW, dtype=np.int64)
    angc = 2.0 * np.pi * ((c[:, None] * c[None, :]) % FNET_GW).astype(np.float64) / FNET_GW
    eye = np.eye(FNET_GROUPS)
    w1 = np.concatenate([np.kron(eye, np.cos(angc)), np.kron(eye, np.sin(angc))], axis=1) / np.sqrt(FNET_GW)
    return tuple(jnp.asarray(a, dtype=_F32) for a in (m, cb, sb, w1))


def _fnet_kernel(half, group, u_ref, m_ref, cb_ref, sb_ref, w1_ref, fw_ref, o_ref, rhs_ref):
    w1 = w1_ref[...].astype(_BF)
    te = jnp.dot(u_ref[:, :FNET_W], w1, preferred_element_type=_F32)
    to = jnp.dot(u_ref[:, FNET_W:], w1, preferred_element_type=_F32)
    for g in range(group):
        rows = slice(g * half, (g + 1) * half)
        ec, es = te[rows, :FNET_W], te[rows, FNET_W:]
        oc, os_ = to[rows, :FNET_W], to[rows, FNET_W:]
        for col, (top, bot) in enumerate(((ec, es), (oc, os_), (os_, -oc))):
            lanes = slice((3 * g + col) * FNET_W, (3 * g + col + 1) * FNET_W)
            rhs_ref[0:half, lanes] = top.astype(_BF)
            rhs_ref[half:2 * half, lanes] = bot.astype(_BF)
    pqr = jnp.dot(m_ref[...].astype(_BF), rhs_ref[...], preferred_element_type=_F32)
    cb, sb = cb_ref[...], sb_ref[...]
    ys = []
    for g in range(group):
        p, q, r = (pqr[:, (3 * g + col) * FNET_W:(3 * g + col + 1) * FNET_W] for col in range(3))
        rot = cb * q - sb * r
        ys += [(p + rot).astype(_BF), (p - rot).astype(_BF)]
    o_ref[...] = jnp.dot(jnp.concatenate(ys, axis=0), fw_ref[...].astype(_BF),
                         preferred_element_type=_F32).astype(o_ref.dtype)


def _fourier_mix(u_pairs, seq, nbatch, row0, l, fw):
    half = seq // 2
    group = max(1, 512 // half)
    rows = group * half
    m, cb, sb, w1 = _dft_constants(seq)
    const = lambda b: (0, 0)
    return pl.pallas_call(
        functools.partial(_fnet_kernel, half, group),
        grid=(nbatch // group,),
        in_specs=[
            pl.BlockSpec((rows, 2 * FNET_W), lambda b: (row0 // rows + b, 0)),
            pl.BlockSpec((half, 2 * half), const),
            pl.BlockSpec((half, FNET_W), const),
            pl.BlockSpec((half, FNET_W), const),
            pl.BlockSpec((FNET_W, 2 * FNET_W), const),
            pl.BlockSpec((None, FNET_W, FNET_W), lambda b: (l, 0, 0)),
        ],
        out_specs=pl.BlockSpec((2 * rows, FNET_W), lambda b: (b, 0)),
        out_shape=jax.ShapeDtypeStruct((nbatch * seq, FNET_W), _BF),
        scratch_shapes=[pltpu.VMEM((2 * half, 3 * group * FNET_W), _BF)],
        compiler_params=_params(("arbitrary",)),
        name="fourier_mix_%d" % seq,
    )(u_pairs, m, cb, sb, w1, fw)


def _route_top2(r_ref, h, ei_ref, ew_ref):
    lg = lax.dot_general(r_ref[...].astype(_BF), h.astype(_BF), _NT, preferred_element_type=_F32)
    eid = lax.broadcasted_iota(jnp.int32, lg.shape, 0)
    m1 = jnp.max(lg, axis=0, keepdims=True)
    i1 = jnp.min(jnp.where(lg == m1, eid, N_EXPERTS), axis=0, keepdims=True)
    lg2 = jnp.where(eid == i1, -jnp.inf, lg)
    m2 = jnp.max(lg2, axis=0, keepdims=True)
    i2 = jnp.min(jnp.where(lg2 == m2, eid, N_EXPERTS), axis=0, keepdims=True)
    e = jnp.exp(m2 - m1)
    ei_ref[0:1, :] = i1
    ei_ref[1:2, :] = i2
    ew_ref[0:1, :] = 1.0 / (1.0 + e)
    ew_ref[1:2, :] = e / (1.0 + e)


def _outproj_kernel(split, route, *refs):
    branch_refs, refs = refs[:6], refs[6:]
    n_x = 2 if split else 1
    x_refs, refs = refs[:n_x], refs[n_x:]
    mod_ref, w_ref, gf_ref = refs[:3]
    refs = refs[3:]
    if route:
        r_ref, xo_ref, h_ref, ei_ref, ew_ref, wb_ref = refs
    else:
        xo_ref, h_ref, wb_ref = refs
    i = pl.program_id(0)

    @pl.when(i == 0)
    def _():
        wb_ref[...] = w_ref[...].astype(_BF)

    mix = jnp.concatenate([_pick_half(i, ROW_TILE, branch_refs[2 * n], branch_refs[2 * n + 1])
                           for n in range(3)], axis=-1)
    x = _pick_half(i, ROW_TILE, *x_refs) if split else x_refs[0][...]
    x_new = x + mod_ref[2:3, :] * jnp.dot(mix, wb_ref[...], preferred_element_type=_F32)
    xo_ref[...] = x_new
    h = _norm_mod(x_new, gf_ref[...], mod_ref[3:4, :], mod_ref[4:5, :])
    h_ref[...] = h.astype(h_ref.dtype)
    if route:
        _route_top2(r_ref, h, ei_ref, ew_ref)


def _outproj(branches, xs, l, mods, w_all, gf_all, router_t=None):
    split = len(xs) == 2
    route = router_t is not None
    row = lambda i: (i, 0)
    in_specs = []
    args = []
    for pair, width in zip(branches, (ATTN_W, CONV_W, FNET_W)):
        in_specs += _half_specs(width, ROW_TILE)
        args += list(pair)
    in_specs += _half_specs(D_MODEL, ROW_TILE) if split else [pl.BlockSpec((ROW_TILE, D_MODEL), row)]
    args += list(xs)
    mod_spec, gf_spec, w_spec = _layer_specs(l, (D_MODEL, D_MODEL))
    in_specs += [mod_spec, w_spec, gf_spec]
    args += [mods, w_all, gf_all.reshape(DEPTH, 1, D_MODEL)]
    out_specs = [pl.BlockSpec((ROW_TILE, D_MODEL), row)] * 2
    out_shape = [jax.ShapeDtypeStruct((T, D_MODEL), _F32), jax.ShapeDtypeStruct((T, D_MODEL), _BF)]
    if route:
        in_specs.append(pl.BlockSpec((N_EXPERTS, D_MODEL), lambda i: (0, 0)))
        args.append(router_t)
        out_specs += [pl.BlockSpec((TOP_K, ROW_TILE), lambda i: (0, i))] * 2
        out_shape += [jax.ShapeDtypeStruct((TOP_K, T), jnp.int32), jax.ShapeDtypeStruct((TOP_K, T), _F32)]
    return pl.pallas_call(
        functools.partial(_outproj_kernel, split, route),
        grid=(T // ROW_TILE,),
        in_specs=in_specs,
        out_specs=out_specs,
        out_shape=out_shape,
        scratch_shapes=[pltpu.VMEM((D_MODEL, D_MODEL), _BF)],
        compiler_params=_params(("arbitrary",)),
        name="outproj_route" if route else "outproj",
    )(*args)


def _ffn_kernel(nj, n_sub_rows, vis_e, vis_start, vis_cnt, used_sub,
                x_hbm, wg_ref, wu_ref, wd_ref, y_hbm,
                big, xb, wgb, wub, wdb, sem_in, sem_out):
    del vis_e
    v = pl.program_id(0)
    j = pl.program_id(1)
    cnt = vis_cnt[v]
    row0 = vis_start[v] * SUB

    x_is_bf16 = x_hbm.dtype == _BF
    landing = xb if x_is_bf16 else big

    def copy_in(s):
        r = pl.multiple_of(row0 + s * SUB, SUB)
        b = pl.multiple_of(s * SUB, SUB)
        return pltpu.make_async_copy(x_hbm.at[pl.ds(r, SUB)], landing.at[pl.ds(b, SUB)], sem_in.at[s])

    def copy_out(s):
        r = pl.multiple_of(row0 + s * SUB, SUB)
        b = pl.multiple_of(s * SUB, SUB)
        return pltpu.make_async_copy(big.at[pl.ds(b, SUB)], y_hbm.at[pl.ds(r, SUB)], sem_out)

    def for_subs(fn):
        lax.fori_loop(0, cnt, lambda s, c: (fn(s), c)[1], 0)

    @pl.when(jnp.logical_and(j == 0, cnt > 0))
    def _():
        for_subs(lambda s: copy_in(s).start())

    @pl.when(cnt > 0)
    def _():
        wgb[...] = wg_ref[...].astype(_BF)
        wub[...] = wu_ref[...].astype(_BF)
        wdb[...] = wd_ref[...].astype(_BF)

        def block(r0, nrows):
            s0 = r0 // SUB

            @pl.when(j == 0)
            def _():
                for t in range(nrows // SUB):
                    copy_in(s0 + t).wait()
                    rs = pl.ds(pl.multiple_of(r0 + t * SUB, SUB), SUB)
                    if not x_is_bf16:
                        xb[rs, :] = big[rs, :].astype(_BF)
                    big[rs, :] = jnp.zeros((SUB, D_MODEL), _F32)

            rs = pl.ds(r0, nrows)
            x = xb[rs, :]
            a = jnp.dot(x, wgb[...], preferred_element_type=_F32)
            b = jnp.dot(x, wub[...], preferred_element_type=_F32)
            p = (a * _sigmoid(a) * b).astype(_BF)
            big[rs, :] += jnp.dot(p, wdb[...], preferred_element_type=_F32)

            @pl.when(j == nj - 1)
            def _():
                for t in range(nrows // SUB):
                    copy_out(s0 + t).start()

        n_big = cnt // BLK_MIN
        rem = cnt - n_big * BLK_MIN
        room = BLK_SUBS - BLK_MIN
        grow = jnp.minimum(rem, n_big * room)
        left = rem - grow

        def one_block(i, r0):
            size = BLK_MIN + jnp.clip(grow - i * room, 0, room)
            for sz in range(BLK_MIN, BLK_SUBS + 1):
                pl.when(size == sz)(functools.partial(block, pl.multiple_of(r0, SUB), sz * SUB))
            return r0 + size * SUB
        r0 = lax.fori_loop(0, n_big, one_block, 0)
        bit = BLK_MIN // 2
        while bit >= 1:
            take = (left // bit) % 2 == 1
            pl.when(take)(functools.partial(block, pl.multiple_of(r0, SUB), bit * SUB))
            r0 = r0 + jnp.where(take, bit * SUB, 0)
            bit //= 2

    @pl.when(jnp.logical_and(j == nj - 1, cnt > 0))
    def _():
        for_subs(lambda s: copy_out(s).wait())

    if x_hbm.dtype == y_hbm.dtype:
        @pl.when(jnp.logical_and(v == pl.num_programs(0) - 1, j == nj - 1))
        def _():
            def tail(s):
                r = pl.multiple_of(s * SUB, SUB)
                return pltpu.make_async_copy(x_hbm.at[pl.ds(r, SUB)], y_hbm.at[pl.ds(r, SUB)], sem_out)
            lax.fori_loop(used_sub[0], n_sub_rows, lambda s, c: (tail(s).start(), c)[1], 0)
            lax.fori_loop(used_sub[0], n_sub_rows, lambda s, c: (tail(s).wait(), c)[1], 0)


def _grouped_ffn(x, wg, wu, wd, fc, vis_e, vis_start, vis_cnt, used_sub):
    ff = wg.shape[-1]
    nj = ff // fc
    nv = vis_e.shape[0]
    n_rows = x.shape[0]

    def chunk(v, j, vc):
        return jnp.where(vc[v] > 0, j, nj - 1)

    return pl.pallas_call(
        functools.partial(_ffn_kernel, nj, n_rows // SUB),
        grid_spec=pltpu.PrefetchScalarGridSpec(
            num_scalar_prefetch=4,
            grid=(nv, nj),
            in_specs=[
                pl.BlockSpec(memory_space=pl.ANY),
                pl.BlockSpec((None, D_MODEL, fc), lambda v, j, ve, vs, vc, us: (ve[v], 0, chunk(v, j, vc))),
                pl.BlockSpec((None, D_MODEL, fc), lambda v, j, ve, vs, vc, us: (ve[v], 0, chunk(v, j, vc))),
                pl.BlockSpec((None, fc, D_MODEL), lambda v, j, ve, vs, vc, us: (ve[v], chunk(v, j, vc), 0)),
            ],
            out_specs=pl.BlockSpec(memory_space=pl.ANY),
            scratch_shapes=[
                pltpu.VMEM((TMAX, D_MODEL), _F32),
                pltpu.VMEM((TMAX, D_MODEL), _BF),
                pltpu.VMEM((D_MODEL, fc), _BF),
                pltpu.VMEM((D_MODEL, fc), _BF),
                pltpu.VMEM((fc, D_MODEL), _BF),
                pltpu.SemaphoreType.DMA((SUB_MAX,)),
                pltpu.SemaphoreType.DMA(()),
            ],
        ),
        out_shape=jax.ShapeDtypeStruct((n_rows, D_MODEL), _F32),
        compiler_params=_params(("arbitrary", "arbitrary"), FFN_VMEM_LIMIT),
        name="ffn_%d" % ff,
    )(vis_e, vis_start, vis_cnt, used_sub, x, wg, wu, wd)


def _dense_visits():
    nv = T // TMAX
    return (jnp.zeros((nv,), jnp.int32),
            jnp.arange(nv, dtype=jnp.int32) * SUB_MAX,
            jnp.full((nv,), SUB_MAX, jnp.int32),
            jnp.full((1,), T // SUB, jnp.int32))


N_SLOT_SUB = -(-(TOP_K * T + N_TILES * N_EXPERTS * (SEG_ALIGN - 1)) // SUB) + N_EXPERTS
N_SLOT = N_SLOT_SUB * SUB
N_VISIT = -(-N_SLOT_SUB // SUB_MAX) + N_EXPERTS


def _routing_plan(eidx):
    e_loc = eidx.reshape(TOP_K, N_TILES, DISP_TILE).transpose(1, 0, 2).reshape(N_TILES, PAIRS)
    onehot = e_loc[:, :, None] == jnp.arange(N_EXPERTS, dtype=jnp.int32)[None, None, :]
    ch = 128
    oh = onehot.astype(_F32).reshape(N_TILES, PAIRS // ch, ch, N_EXPERTS)
    tri = (jnp.arange(ch)[:, None] >= jnp.arange(ch)[None, :]).astype(_F32)
    within = jnp.einsum("ij,tcjk->tcik", tri, oh)
    tot = within[:, :, -1, :]
    csum = (within + (jnp.cumsum(tot, axis=1) - tot)[:, :, None, :]).reshape(N_TILES, PAIRS, N_EXPERTS)
    csum = csum.astype(jnp.int32)
    n_te = (csum[:, -1, :] + SEG_ALIGN - 1) // SEG_ALIGN * SEG_ALIGN
    src = jnp.cumsum(n_te, axis=1) - n_te
    lpos = jnp.sum(jnp.where(onehot, csum - 1 + src[:, None, :], 0), axis=2).astype(jnp.int32)
    counts = jnp.sum(n_te, axis=0)
    nsub = (counts + SUB - 1) // SUB
    sub_base = jnp.cumsum(nsub) - nsub
    dst = (sub_base * SUB)[None, :] + jnp.cumsum(n_te, axis=0) - n_te
    seg = tuple(a.reshape(-1).astype(jnp.int32) for a in (n_te, src, dst))
    pads = ((sub_base * SUB + counts).astype(jnp.int32), (nsub * SUB - counts).astype(jnp.int32))
    used_sub = jnp.sum(nsub).reshape(1).astype(jnp.int32)
    nvis = (nsub + SUB_MAX - 1) // SUB_MAX
    vend = jnp.cumsum(nvis)
    total = vend[-1]
    vid = jnp.arange(N_VISIT, dtype=jnp.int32)
    ve = jnp.minimum(jnp.sum((vid[:, None] >= vend[None, :]).astype(jnp.int32), axis=1), N_EXPERTS - 1)
    local = vid - (vend - nvis)[ve]
    nv_e = jnp.maximum(nvis[ve], 1)
    q, r = nsub[ve] // nv_e, nsub[ve] % nv_e
    cnt = q + (local < r).astype(jnp.int32)
    start = sub_base[ve] + local * q + jnp.minimum(local, r)
    used = vid < total
    last_e = ve[jnp.maximum(total - 1, 0)]
    vis_e = jnp.where(used, ve, last_e).astype(jnp.int32)
    vis_cnt = jnp.where(used, cnt, 0).astype(jnp.int32)
    vis_start = jnp.where(used, start, 0).astype(jnp.int32)
    return lpos.reshape(N_TILES, TOP_K, DISP_TILE), seg, pads, (vis_e, vis_start, vis_cnt, used_sub)


def _for_pow2_pieces(n, max_piece, fn):
    off = 0
    bit = max_piece
    while bit >= SEG_ALIGN:
        take = (n // bit) % 2 == 1
        pl.when(take)(functools.partial(fn, off, bit))
        off = off + jnp.where(take, bit, 0)
        bit //= 2


def _segment_copies(tile, seg, make_copy, wait=False):
    n_te, src, dst = seg
    for e in range(N_EXPERTS):
        j = tile * N_EXPERTS + e
        s0, d0 = src[j], dst[j]

        def piece(off, size, s0=s0, d0=d0):
            cp = make_copy(pl.multiple_of(s0 + off, SEG_ALIGN), pl.multiple_of(d0 + off, SEG_ALIGN), size)
            cp.wait() if wait else cp.start()
        _for_pow2_pieces(n_te[j], DISP_TILE, piece)


def _dispatch_kernel(n_te, src, dst, pad_start, pad_cnt, used_sub, h_ref, lpos_ref, xs_hbm, ring, ring_sems, sem):
    i = pl.program_id(0)
    last = pl.num_programs(0) - 1
    b = i % 2

    def copies(tile, slot_id, wait):
        _segment_copies(tile, (n_te, src, dst), lambda s, d, size: pltpu.make_async_copy(
            ring.at[slot_id, pl.ds(s, size)], xs_hbm.at[pl.ds(d, size)], ring_sems.at[slot_id]), wait)

    pl.when(i >= 2)(lambda: copies(i - 2, b, True))
    lp = lpos_ref[...]
    srow = lax.broadcasted_iota(jnp.int32, (SORT_ROWS, DISP_TILE), 0)
    perm = jnp.where(lp[0:1, :] == srow, 1.0, jnp.where(lp[1:2, :] == srow, 1.0, 0.0)).astype(_BF)
    ring[b] = jnp.dot(perm, h_ref[...].astype(_BF), preferred_element_type=_F32)
    copies(i, b, False)

    @pl.when(i == last)
    def _():
        copies(i - 1, 1 - b, True)
        copies(i, b, True)
        for e in range(N_EXPERTS):
            p0 = pad_start[e]

            def fill(off, size, p0=p0):
                cp = pltpu.make_async_copy(ring.at[b, pl.ds(0, size)],
                                           xs_hbm.at[pl.ds(pl.multiple_of(p0 + off, SEG_ALIGN), size)], sem)
                cp.start()
                cp.wait()
            _for_pow2_pieces(pad_cnt[e], SUB // 2, fill)

        def tail(s):
            r = pl.multiple_of(s * SUB, SUB)
            return pltpu.make_async_copy(ring.at[b, pl.ds(0, SUB)], xs_hbm.at[pl.ds(r, SUB)], sem)
        lax.fori_loop(used_sub[0], N_SLOT_SUB, lambda s, c: (tail(s).start(), c)[1], 0)
        lax.fori_loop(used_sub[0], N_SLOT_SUB, lambda s, c: (tail(s).wait(), c)[1], 0)


def _dispatch(h, lpos, seg, pads, used_sub):
    return pl.pallas_call(
        _dispatch_kernel,
        grid_spec=pltpu.PrefetchScalarGridSpec(
            num_scalar_prefetch=6,
            grid=(N_TILES,),
            in_specs=[pl.BlockSpec((DISP_TILE, D_MODEL), lambda i, *_: (i, 0)),
                      pl.BlockSpec((None, TOP_K, DISP_TILE), lambda i, *_: (i, 0, 0))],
            out_specs=pl.BlockSpec(memory_space=pl.ANY),
            scratch_shapes=[pltpu.VMEM((2, SORT_ROWS, D_MODEL), _F32),
                            pltpu.SemaphoreType.DMA((2,)), pltpu.SemaphoreType.DMA(())],
        ),
        out_shape=jax.ShapeDtypeStruct((N_SLOT, D_MODEL), _F32),
        compiler_params=_params(("arbitrary",)),
        name="dispatch",
    )(*seg, *pads, used_sub, h, lpos)


def _residual_out(x, gate, f, g_ref, o_refs, i, tile):
    out = x + gate * f
    if g_ref is not None:
        ms = jnp.mean(out * out, axis=-1, keepdims=True)
        out = out * lax.rsqrt(ms + EPS) * g_ref[...]
    if len(o_refs) == 1:
        o_refs[0][...] = out
    else:
        @pl.when(i < T_PROMPT // tile)
        def _():
            o_refs[0][...] = out

        @pl.when(i >= T_PROMPT // tile)
        def _():
            o_refs[1][...] = out


def _out_specs(final, tile):
    if not final:
        return ([pl.BlockSpec((tile, D_MODEL), lambda i, *_: (i, 0))],
                [jax.ShapeDtypeStruct((T, D_MODEL), _F32)])
    return (_half_specs(D_MODEL, tile),
            [jax.ShapeDtypeStruct((T_PROMPT, D_MODEL), _F32), jax.ShapeDtypeStruct((T_SAMPLE, D_MODEL), _F32)])


def _combine_kernel(final, x_ref, mod_ref, y_ref, *rest):
    g_ref, o_refs = (rest[0], rest[1:]) if final else (None, rest)
    _residual_out(x_ref[...], mod_ref[5:6, :], y_ref[...], g_ref, o_refs, pl.program_id(0), ROW_TILE)


def _combine(x, l, mods, y, g_final=None):
    final = g_final is not None
    row = lambda i: (i, 0)
    in_specs = [
        pl.BlockSpec((ROW_TILE, D_MODEL), row),
        _layer_specs(l, ())[0],
        pl.BlockSpec((ROW_TILE, D_MODEL), row),
    ]
    args = [x, mods, y]
    if final:
        in_specs.append(pl.BlockSpec((1, D_MODEL), lambda i: (0, 0)))
        args.append(g_final.reshape(1, D_MODEL))
    out_specs, out_shape = _out_specs(final, ROW_TILE)
    return pl.pallas_call(
        functools.partial(_combine_kernel, final),
        grid=(T // ROW_TILE,),
        in_specs=in_specs,
        out_specs=out_specs,
        out_shape=out_shape,
        compiler_params=_params(("arbitrary",)),
        name="combine%s" % ("_final" if final else ""),
    )(*args)


def _combine_top2_kernel(final, n_out, n_te, src, dst, x_ref, mod_ref, w_ref, lpos_ref, *rest):
    g_ref, rest = (rest[0], rest[1:]) if final else (None, rest)
    ys_hbm, o_refs, (ybuf, sems) = rest[0], rest[1:1 + n_out], rest[1 + n_out:]
    i = pl.program_id(0)

    def fetch(tile, b, wait=False):
        _segment_copies(tile, (n_te, src, dst), lambda s, d, size: pltpu.make_async_copy(
            ys_hbm.at[pl.ds(d, size)], ybuf.at[b, pl.ds(s, size)], sems.at[b]), wait)

    @pl.when(i == 0)
    def _():
        for slot_id in range(2):
            ybuf[slot_id, PAIRS:SORT_ROWS, :] = jnp.zeros((SORT_ROWS - PAIRS, D_MODEL), _F32)
        fetch(0, 0)

    @pl.when(i + 1 < pl.num_programs(0))
    def _():
        fetch(i + 1, (i + 1) % 2)

    b = i % 2
    fetch(i, b, wait=True)
    y = ybuf[b].astype(_BF)
    lp = lpos_ref[...]
    scol = lax.broadcasted_iota(jnp.int32, (DISP_TILE, SORT_ROWS), 1)
    w = w_ref[...]
    pick = jnp.zeros((DISP_TILE, SORT_ROWS), _F32)
    for k in range(TOP_K):
        pick = jnp.where(lp[:, k:k + 1] == scol, w[:, k:k + 1], pick)
    f = jnp.dot(pick.astype(_BF), y, preferred_element_type=_F32)
    _residual_out(x_ref[...], mod_ref[5:6, :], f, g_ref, o_refs, i, DISP_TILE)


def _combine_top2(x, l, mods, ys, lpos_t, seg, w, g_final=None):
    final = g_final is not None
    row = lambda i, *_: (i, 0)
    in_specs = [
        pl.BlockSpec((DISP_TILE, D_MODEL), row),
        pl.BlockSpec((None, None, 6, D_MODEL), lambda i, *_: (l, _cond_of_tile(i, DISP_TILE), 0, 0)),
        pl.BlockSpec((DISP_TILE, TOP_K), row),
        pl.BlockSpec((None, DISP_TILE, TOP_K), lambda i, *_: (i, 0, 0)),
    ]
    args = [x, mods, w, lpos_t]
    if final:
        in_specs.append(pl.BlockSpec((1, D_MODEL), lambda i, *_: (0, 0)))
        args.append(g_final.reshape(1, D_MODEL))
    in_specs.append(pl.BlockSpec(memory_space=pl.ANY))
    args.append(ys)
    out_specs, out_shape = _out_specs(final, DISP_TILE)
    return pl.pallas_call(
        functools.partial(_combine_top2_kernel, final, len(out_specs)),
        grid_spec=pltpu.PrefetchScalarGridSpec(
            num_scalar_prefetch=3,
            grid=(N_TILES,),
            in_specs=in_specs,
            out_specs=out_specs,
            scratch_shapes=[
                pltpu.VMEM((2, SORT_ROWS, D_MODEL), _F32),
                pltpu.SemaphoreType.DMA((2,)),
            ],
        ),
        out_shape=out_shape,
        compiler_params=_params(("arbitrary",)),
        name="combine_top2%s" % ("_final" if final else ""),
    )(*seg, *args)


def kernel(x_prompt, x_sample, cache_k, cache_v, c, c_ctx, w_ada, b_ada, g_norm_mix, g_norm_ffn,
           w_in, w_out, attn_sink, conv_dw, conv_dw_b, conv_ln_g, conv_ln_b, conv_pw, fnet_w,
           ffn_w_gate, ffn_w_up, ffn_w_down, moe_router, moe_w_gate, moe_w_up, moe_w_down, g_final):
    xs = (x_prompt.reshape(T_PROMPT, D_MODEL), x_sample.reshape(T_SAMPLE, D_MODEL))
    cond8 = jnp.concatenate([c_ctx[None, :], c, jnp.zeros((N_COND - 1 - DEC_BATCH, D_MODEL), _F32)], axis=0)
    mods = _modulation(cond8, w_ada, b_ada).reshape(DEPTH, N_COND, 6, D_MODEL)
    cos_t, sin_t = _rope_tables()
    ck_all = cache_k.reshape(DEC_BATCH, DEPTH, PAST_LEN, KV_W)
    cv_all = cache_v.reshape(DEC_BATCH, DEPTH, PAST_LEN, KV_W)
    p_blocks = T_PROMPT // DEC_SEQ
    sinks = attn_sink.reshape(DEPTH * N_HEADS)

    ks, vs = [], []
    resid = None
    for l in range(DEPTH):
        q, kp, vp, kl, vl, uc, uf, *x_new = _inproj(xs, l, mods, g_norm_mix, w_in, cos_t, sin_t, resid=resid)
        if resid is not None:
            xs, resid = tuple(x_new), None
        ks.append(kp)
        vs.append(vp)
        attn = (_ctx_attention(l, sinks, q, kp, vp),
                _lat_attention(l, sinks, q, kl, vl, ck_all, cv_all))
        cargs = (l, conv_dw, conv_dw_b, conv_ln_g, conv_ln_b, conv_pw)
        conv = (_conv_module(uc, SEQ, BATCH, 0, *cargs),
                _conv_module(uc, DEC_SEQ, DEC_BATCH, p_blocks, *cargs))
        four = (_fourier_mix(uf, SEQ, BATCH, 0, l, fnet_w),
                _fourier_mix(uf, DEC_SEQ, DEC_BATCH, T_PROMPT // 2, l, fnet_w))
        last = g_final if l == DEPTH - 1 else None
        i = l // 2
        if l % 2 == 0:
            x, h = _outproj((attn, conv, four), xs, l, mods, w_out, g_norm_ffn)
            y = _grouped_ffn(h, ffn_w_gate[i:i + 1], ffn_w_up[i:i + 1], ffn_w_down[i:i + 1], 256,
                             *_dense_visits())
            if last is None:
                xs, resid = (x,), (y, l)
            else:
                xs = tuple(_combine(x, l, mods, y, g_final=last))
        else:
            x, h, eidx, ew = _outproj((attn, conv, four), xs, l, mods, w_out, g_norm_ffn,
                                      router_t=moe_router[i].T)
            lpos, seg, pads, visits = _routing_plan(eidx)
            xd = _dispatch(h, lpos, seg, pads, visits[-1])
            ys = _grouped_ffn(xd, moe_w_gate[i], moe_w_up[i], moe_w_down[i], 512, *visits)
            xs = tuple(_combine_top2(x, l, mods, ys, lpos.transpose(0, 2, 1), seg, ew.T, g_final=last))

    y_prompt = xs[0].reshape(BATCH, SEQ, D_MODEL)
    y_sample = xs[1].reshape(DEC_BATCH, DEC_SEQ, D_MODEL)
    state_k = jnp.stack([a.reshape(BATCH, SEQ, N_KV_HEADS, HEAD_DIM) for a in ks], axis=1)
    state_v = jnp.stack([a.reshape(BATCH, SEQ, N_KV_HEADS, HEAD_DIM) for a in vs], axis=1)
    return (y_prompt, y_sample, state_k, state_v)
```

```python
import functools

import numpy as np
import jax
import jax.numpy as jnp
from jax import lax
from jax.experimental import pallas as pl
from jax.experimental.pallas import tpu as pltpu

D_MODEL = 1024
BATCH = 16
SEQ = 256
DEPTH = 2
DEC_BATCH = 2
DEC_SEQ = 2048
PAST_LEN = 512
GRID_W = 64
HEAD_DIM = 64
N_HEADS = 8
N_KV_HEADS = 2
GQA_GROUP = N_HEADS // N_KV_HEADS
ATTN_W = N_HEADS * HEAD_DIM
KV_W = N_KV_HEADS * HEAD_DIM
WINDOW = 128
BLOCK = 128
ROPE_THETA = 10000.0
CONV_W = D_MODEL // 4
CONV_K = 31
FNET_GROUPS = 4
FNET_W = D_MODEL // 4
FNET_GW = FNET_W // FNET_GROUPS
IN_WIDTH = ATTN_W + 2 * KV_W + 2 * CONV_W + FNET_W
D_FF = 2816
N_EXPERTS = 8
TOP_K = 2
D_FF_EXPERT = 3584
EPS = 1e-6
NEG = -1e30

T_PROMPT = BATCH * SEQ
T_SAMPLE = DEC_BATCH * DEC_SEQ
T = T_PROMPT + T_SAMPLE
N_COND = 8

ROW_TILE = 512
SUB = 64
SUB_MAX = 64
TMAX = SUB * SUB_MAX
BLK_MIN = 16
BLK_SUBS = 20
DISP_TILE = 512
N_TILES = T // DISP_TILE
PAIRS = TOP_K * DISP_TILE
SEG_ALIGN = 8
SORT_ROWS = -(-(PAIRS + N_EXPERTS * (SEG_ALIGN - 1)) // 128) * 128
CONV_CHUNK = 512
CONV_PAD = 16
VMEM_LIMIT = 48 * 1024 * 1024
FFN_VMEM_LIMIT = 56 * 1024 * 1024

_BF = jnp.bfloat16
_F32 = jnp.float32


def _cond_of_tile(i, tile):
    r = i * tile
    return jnp.where(r < T_PROMPT, 0, 1 + (r - T_PROMPT) // DEC_SEQ)


def _params(sem, vmem=VMEM_LIMIT):
    return pltpu.CompilerParams(dimension_semantics=sem, vmem_limit_bytes=vmem)


def _sigmoid(x):
    return 1.0 / (1.0 + jnp.exp(-x))


def _mod_kernel(cond_ref, w_ref, b_ref, o_ref):
    cnd = cond_ref[...]
    s = (cnd * _sigmoid(cnd)).astype(_BF)
    o_ref[...] = jnp.dot(s, w_ref[...].astype(_BF), preferred_element_type=_F32) + b_ref[...]


def _modulation(cond8, w_ada, b_ada):
    nt = 1536
    return pl.pallas_call(
        _mod_kernel,
        grid=(DEPTH, 6 * D_MODEL // nt),
        in_specs=[
            pl.BlockSpec((N_COND, D_MODEL), lambda l, n: (0, 0)),
            pl.BlockSpec((None, D_MODEL, nt), lambda l, n: (l, 0, n)),
            pl.BlockSpec((None, 1, nt), lambda l, n: (l, 0, n)),
        ],
        out_specs=pl.BlockSpec((None, N_COND, nt), lambda l, n: (l, 0, n)),
        out_shape=jax.ShapeDtypeStruct((DEPTH, N_COND, 6 * D_MODEL), _F32),
        compiler_params=_params(("arbitrary", "arbitrary")),
        name="modulation",
    )(cond8, w_ada, b_ada.reshape(DEPTH, 1, 6 * D_MODEL))


def _norm_mod(x, g, shift, scale):
    ms = jnp.mean(x * x, axis=-1, keepdims=True)
    y = x * lax.rsqrt(ms + EPS) * g
    return y * (1.0 + scale) + shift


def _rope_tables():
    rows = DEC_SEQ // GRID_W
    n_freq = HEAD_DIM // 4
    inv = ROPE_THETA ** (-jnp.arange(n_freq, dtype=_F32) / n_freq)
    gr, gc = jnp.meshgrid(jnp.arange(rows, dtype=_F32), jnp.arange(GRID_W, dtype=_F32), indexing="ij")
    ang_r = gr.reshape(-1)[:, None] * inv
    ang_c = gc.reshape(-1)[:, None] * inv
    cr, sr, cc, sc = jnp.cos(ang_r), jnp.sin(ang_r), jnp.cos(ang_c), jnp.sin(ang_c)
    cos64 = jnp.concatenate([cr, cr, cc, cc], axis=-1)
    sin64 = jnp.concatenate([-sr, sr, -sc, sc], axis=-1)
    return jnp.tile(cos64, (1, 2)), jnp.tile(sin64, (1, 2))


def _rope128(x, cos, sin):
    lane = lax.broadcasted_iota(jnp.int32, x.shape, 1)
    first = (lane % 32) < 16
    partner = jnp.where(first, pltpu.roll(x, 128 - 16, 1), pltpu.roll(x, 16, 1))
    return x * cos + partner * sin


def _half_specs(width, tile):
    pt = T_PROMPT // tile
    return [pl.BlockSpec((tile, width), lambda i, *_: (jnp.minimum(i, pt - 1), 0)),
            pl.BlockSpec((tile, width), lambda i, *_: (jnp.maximum(i - pt, 0), 0))]


def _pick_half(i, tile, a_ref, b_ref):
    return jnp.where(i < T_PROMPT // tile, a_ref[...], b_ref[...])


def _layer_specs(l, w_shape):
    return [
        pl.BlockSpec((None, None, 6, D_MODEL), lambda i, *_: (l, _cond_of_tile(i, ROW_TILE), 0, 0)),
        pl.BlockSpec((None, 1, D_MODEL), lambda i, *_: (l, 0, 0)),
        pl.BlockSpec((None,) + w_shape, lambda i, *_: (l, 0, 0)),
    ]


def _inproj_kernel(split, resid, *refs):
    n_x = 2 if split else 1
    x_refs, refs = refs[:n_x], refs[n_x:]
    if resid:
        y_ref, modp_ref = refs[:2]
        refs = refs[2:]
    mod_ref, g_ref, w_ref, cos_ref, sin_ref = refs[:5]
    q_ref, kp_ref, vp_ref, ks_ref, vs_ref, uc_ref, uf_ref = refs[5:12]
    wb_ref, uf_scr = refs[-2:]
    i = pl.program_id(0)

    @pl.when(i == 0)
    def _():
        wb_ref[...] = w_ref[...].astype(_BF)

    x = _pick_half(i, ROW_TILE, *x_refs) if split else x_refs[0][...]
    if resid:
        x = x + modp_ref[5:6, :] * y_ref[...]
        refs[12][...] = x
    h = _norm_mod(x, g_ref[...], mod_ref[0:1, :], mod_ref[1:2, :]).astype(_BF)
    latent = i >= T_PROMPT // ROW_TILE
    cos = jnp.where(latent, cos_ref[...], 1.0)
    sin = jnp.where(latent, sin_ref[...], 0.0)

    def proj(o, n):
        return jnp.dot(h, wb_ref[:, o:o + n], preferred_element_type=_F32)

    o = 0
    for c in range(ATTN_W // 256):
        qq = proj(o, 256)
        q_ref[:, o:o + 128] = _rope128(qq[:, :128], cos, sin).astype(q_ref.dtype)
        q_ref[:, o + 128:o + 256] = _rope128(qq[:, 128:], cos, sin).astype(q_ref.dtype)
        o += 256
    kv = proj(o, 2 * KV_W)
    k = _rope128(kv[:, :KV_W], cos, sin)
    o += 2 * KV_W
    uc_ref[...] = proj(o, 2 * CONV_W)
    o += 2 * CONV_W
    uf = proj(o, FNET_W)
    n_lane = FNET_W // 128
    for c in range(n_lane):
        uf_scr[c] = uf[:, c * 128:(c + 1) * 128]
    half = ROW_TILE // 2
    uf_ref[...] = jnp.concatenate([uf_scr[c, pl.ds(par, half, stride=2), :]
                                   for par in range(2) for c in range(n_lane)], axis=-1).astype(uf_ref.dtype)

    @pl.when(jnp.logical_not(latent))
    def _():
        kp_ref[...] = k
        vp_ref[...] = kv[:, KV_W:]

    @pl.when(latent)
    def _():
        ks_ref[...] = k
        vs_ref[...] = kv[:, KV_W:]


def _inproj(xs, l, mods, g_all, w_all, cos_t, sin_t, resid=None):
    split = len(xs) == 2
    pt = T_PROMPT // ROW_TILE
    per_seq = DEC_SEQ // ROW_TILE

    def rope_idx(i):
        return (jnp.maximum(i - pt, 0) % per_seq, 0)

    row = lambda i: (i, 0)
    in_specs = _half_specs(D_MODEL, ROW_TILE) if split else [pl.BlockSpec((ROW_TILE, D_MODEL), row)]
    args = list(xs)
    kv_ctx, kv_lat = _half_specs(KV_W, ROW_TILE)
    out_specs = [pl.BlockSpec((ROW_TILE, ATTN_W), row), kv_ctx, kv_ctx, kv_lat, kv_lat,
                 pl.BlockSpec((ROW_TILE, 2 * CONV_W), row), pl.BlockSpec((ROW_TILE // 2, 2 * FNET_W), row)]
    out_shape = [jax.ShapeDtypeStruct((T, ATTN_W), _BF),
                 jax.ShapeDtypeStruct((T_PROMPT, KV_W), _F32), jax.ShapeDtypeStruct((T_PROMPT, KV_W), _F32),
                 jax.ShapeDtypeStruct((T_SAMPLE, KV_W), _F32), jax.ShapeDtypeStruct((T_SAMPLE, KV_W), _F32),
                 jax.ShapeDtypeStruct((T, 2 * CONV_W), _F32), jax.ShapeDtypeStruct((T // 2, 2 * FNET_W), _BF)]
    if resid is not None:
        y, l_prev = resid
        in_specs += [pl.BlockSpec((ROW_TILE, D_MODEL), row), _layer_specs(l_prev, ())[0]]
        args += [y, mods]
        out_specs.append(pl.BlockSpec((ROW_TILE, D_MODEL), row))
        out_shape.append(jax.ShapeDtypeStruct((T, D_MODEL), _F32))
    return pl.pallas_call(
        functools.partial(_inproj_kernel, split, resid is not None),
        grid=(T // ROW_TILE,),
        in_specs=in_specs + _layer_specs(l, (D_MODEL, IN_WIDTH)) + [
            pl.BlockSpec((ROW_TILE, 128), rope_idx),
            pl.BlockSpec((ROW_TILE, 128), rope_idx),
        ],
        out_specs=out_specs,
        out_shape=out_shape,
        scratch_shapes=[pltpu.VMEM((D_MODEL, IN_WIDTH), _BF), pltpu.VMEM((FNET_W // 128, ROW_TILE, 128), _F32)],
        compiler_params=_params(("arbitrary",)),
        name="inproj_resid" if resid is not None else "inproj",
    )(*args, mods, g_all.reshape(DEPTH, 1, D_MODEL), w_all, cos_t, sin_t)


assert 2 * HEAD_DIM == 128 and KV_W == 128 and GQA_GROUP == 4


def _kv_operands(x):
    return x.astype(_BF), pltpu.roll(x, HEAD_DIM, 1).astype(_BF)


def _half_mask(shape, par):
    return (lax.broadcasted_iota(jnp.int32, shape, 1) >= HEAD_DIM) == (par == 1)


def _stack_chunks(q, kh, rows):
    c0 = 2 * kh
    return jnp.concatenate([q[:, c0 * 128:(c0 + 1) * 128], q[:, (c0 + 1) * 128:(c0 + 2) * 128]], axis=0)


def _sink_column(sink_ref, heads, rows):
    r = lax.broadcasted_iota(jnp.int32, (len(heads) * rows, 1), 0)
    col = jnp.full((len(heads) * rows, 1), sink_ref[heads[0]], _F32)
    for g in range(1, len(heads)):
        col = jnp.where(r >= g * rows, sink_ref[heads[g]], col)
    return col


def _stack_heads(q, kh, rows):
    parts = [q[:, (kh * GQA_GROUP + g) * HEAD_DIM:(kh * GQA_GROUP + g + 1) * HEAD_DIM] for g in range(GQA_GROUP)]
    return jnp.concatenate(parts, axis=0)


def _store_heads(o_ref, o, kh, rows):
    for pair in range(GQA_GROUP // 2):
        a = o[(2 * pair) * rows:(2 * pair + 1) * rows]
        b = o[(2 * pair + 1) * rows:(2 * pair + 2) * rows]
        c0 = (kh * GQA_GROUP + 2 * pair) * HEAD_DIM
        o_ref[:, c0:c0 + 2 * HEAD_DIM] = jnp.concatenate([a, b], axis=-1).astype(o_ref.dtype)


def _store_chunks(o_ref, o, kh, rows):
    c0 = 2 * kh
    o_ref[:, c0 * 128:(c0 + 1) * 128] = o[:rows].astype(o_ref.dtype)
    o_ref[:, (c0 + 1) * 128:(c0 + 2) * 128] = o[rows:].astype(o_ref.dtype)


_NT = (((1,), (1,)), ((), ()))


def _ctx_attn_kernel(l, sink_ref, q_ref, k_ref, v_ref, o_ref):
    scale = HEAD_DIM ** -0.5
    q = q_ref[...] * scale
    k_ops = _kv_operands(k_ref[...])
    v_ops = _kv_operands(v_ref[...])
    for kh in range(N_KV_HEADS):
        qs = _stack_chunks(q, kh, SEQ).astype(_BF)
        o = None
        for par in range(2):
            own = _half_mask(qs.shape, par)
            which = 0 if kh == par else 1
            s = lax.dot_general(jnp.where(own, qs, 0), k_ops[which], _NT, preferred_element_type=_F32)
            h0 = l * N_HEADS + kh * GQA_GROUP + par
            sink = _sink_column(sink_ref, (h0, h0 + 2), SEQ)
            m = jnp.maximum(jnp.max(s, axis=-1, keepdims=True), sink)
            p = jnp.exp(s - m)
            den = jnp.sum(p, axis=-1, keepdims=True) + jnp.exp(sink - m)
            t = jnp.where(own, jnp.dot(p.astype(_BF), v_ops[which], preferred_element_type=_F32) / den, 0.0)
            o = t if o is None else o + t
        _store_chunks(o_ref, o, kh, SEQ)


def _ctx_attention(l, sinks, q, k, v):
    row = lambda b, s: (b, 0)
    return pl.pallas_call(
        functools.partial(_ctx_attn_kernel, l),
        grid_spec=pltpu.PrefetchScalarGridSpec(
            num_scalar_prefetch=1,
            grid=(BATCH,),
            in_specs=[
                pl.BlockSpec((SEQ, ATTN_W), row),
                pl.BlockSpec((SEQ, KV_W), row),
                pl.BlockSpec((SEQ, KV_W), row),
            ],
            out_specs=pl.BlockSpec((SEQ, ATTN_W), row),
        ),
        out_shape=jax.ShapeDtypeStruct((T_PROMPT, ATTN_W), _BF),
        compiler_params=_params(("arbitrary",)),
        name="ctx_attention",
    )(sinks, q, k, v)


def _lat_attn_kernel(l, sink_ref, q_ref, k_ref, v_ref, ck_ref, cv_ref, o_ref):
    n = pl.program_id(1)
    scale = HEAD_DIM ** -0.5
    nb = DEC_SEQ // BLOCK
    band = 3 * BLOCK
    start = pl.multiple_of(jnp.clip(n - 1, 0, nb - 3) * BLOCK, BLOCK)
    q = q_ref[...] * scale
    kb = k_ref[pl.ds(start, band), :]
    vb = v_ref[pl.ds(start, band), :]
    dpos = (lax.broadcasted_iota(jnp.int32, (BLOCK, band), 1) - lax.broadcasted_iota(jnp.int32, (BLOCK, band), 0)
            + (start - n * BLOCK))
    in_window = jnp.where(jnp.abs(dpos) <= WINDOW, 0.0, NEG)
    bias = jnp.concatenate([in_window] * GQA_GROUP, axis=0)
    for kh in range(N_KV_HEADS):
        hs = slice(kh * HEAD_DIM, (kh + 1) * HEAD_DIM)
        qs = _stack_heads(q, kh, BLOCK).astype(_BF)
        s_loc = lax.dot_general(qs, kb[:, hs].astype(_BF), _NT, preferred_element_type=_F32)
        s_loc = jnp.where(bias < 0.0, NEG, s_loc)
        s_ctx = lax.dot_general(qs, ck_ref[:, hs].astype(_BF), _NT, preferred_element_type=_F32)
        h0 = l * N_HEADS + kh * GQA_GROUP
        sink = _sink_column(sink_ref, tuple(range(h0, h0 + GQA_GROUP)), BLOCK)
        m = jnp.maximum(jnp.maximum(jnp.max(s_loc, axis=-1, keepdims=True),
                                    jnp.max(s_ctx, axis=-1, keepdims=True)), sink)
        p_loc = jnp.exp(s_loc - m)
        p_ctx = jnp.exp(s_ctx - m)
        den = (jnp.sum(p_loc, axis=-1, keepdims=True) + jnp.sum(p_ctx, axis=-1, keepdims=True)
               + jnp.exp(sink - m))
        o = (jnp.dot(p_loc.astype(_BF), vb[:, hs].astype(_BF), preferred_element_type=_F32)
             + jnp.dot(p_ctx.astype(_BF), cv_ref[:, hs].astype(_BF), preferred_element_type=_F32)) / den
        _store_heads(o_ref, o, kh, BLOCK)


def _lat_attention(l, sinks, q, k, v, ck, cv):
    nb = DEC_SEQ // BLOCK
    q0 = T_PROMPT // BLOCK
    return pl.pallas_call(
        functools.partial(_lat_attn_kernel, l),
        grid_spec=pltpu.PrefetchScalarGridSpec(
            num_scalar_prefetch=1,
            grid=(DEC_BATCH, nb),
            in_specs=[
                pl.BlockSpec((BLOCK, ATTN_W), lambda b, n, s: (q0 + b * nb + n, 0)),
                pl.BlockSpec((DEC_SEQ, KV_W), lambda b, n, s: (b, 0)),
                pl.BlockSpec((DEC_SEQ, KV_W), lambda b, n, s: (b, 0)),
                pl.BlockSpec((None, None, PAST_LEN, KV_W), lambda b, n, s: (b, l, 0, 0)),
                pl.BlockSpec((None, None, PAST_LEN, KV_W), lambda b, n, s: (b, l, 0, 0)),
            ],
            out_specs=pl.BlockSpec((BLOCK, ATTN_W), lambda b, n, s: (b * nb + n, 0)),
        ),
        out_shape=jax.ShapeDtypeStruct((T_SAMPLE, ATTN_W), _BF),
        compiler_params=_params(("arbitrary", "arbitrary")),
        name="lat_attention",
    )(sinks, q, k, v, ck, cv)


def _conv_rows(seq):
    rows = min(seq, CONV_CHUNK)
    return rows, rows + 8 * ((CONV_PAD + CONV_K // 2) // 8)


def _conv_kernel(seq, u_ref, dw_ref, dwb_ref, lg_ref, lb_ref, pw_ref, o_ref, pad_ref, y_ref, sh_ref):
    u = u_ref[...]
    pad_ref[0:CONV_PAD, :] = jnp.zeros((CONV_PAD, CONV_W), _F32)
    pad_ref[CONV_PAD + seq:2 * CONV_PAD + seq, :] = jnp.zeros((CONV_PAD, CONV_W), _F32)
    pad_ref[CONV_PAD:CONV_PAD + seq, :] = u[:, :CONV_W] * _sigmoid(u[:, CONV_W:])
    off = CONV_PAD - CONV_K // 2
    rows, span = _conv_rows(seq)

    def chunk(c, carry):
        row = pl.multiple_of(c * rows, rows)
        win = pad_ref[pl.ds(row, rows + 2 * CONV_PAD), :]
        acc = jnp.zeros((rows, CONV_W), _F32) + dwb_ref[...]
        for phase in range(8):
            sh_ref[phase] = win[phase:phase + span, :]
        for phase in range(8):
            for a in range(span // 8):
                t = 8 * a + phase - off
                if 0 <= t < CONV_K:
                    acc = acc + sh_ref[phase, 8 * a:8 * a + rows, :] * dw_ref[t:t + 1, :]
        mu = jnp.mean(acc, axis=-1, keepdims=True)
        d = acc - mu
        var = jnp.mean(d * d, axis=-1, keepdims=True)
        y = d * lax.rsqrt(var + EPS) * lg_ref[...] + lb_ref[...]
        y_ref[pl.ds(row, rows), :] = (y * _sigmoid(y)).astype(_BF)
        return carry

    lax.fori_loop(0, seq // rows, chunk, 0)
    o_ref[...] = jnp.dot(y_ref[...], pw_ref[...].astype(_BF), preferred_element_type=_F32).astype(o_ref.dtype)


def _conv_module(uc, seq, nbatch, block0, l, dw, dwb, lg, lb, pw):
    vec = lambda a: a.reshape(DEPTH, 1, CONV_W)
    layer = lambda b: (l, 0, 0)
    return pl.pallas_call(
        functools.partial(_conv_kernel, seq),
        grid=(nbatch,),
        in_specs=[
            pl.BlockSpec((seq, 2 * CONV_W), lambda b: (block0 + b, 0)),
            pl.BlockSpec((None, CONV_K, CONV_W), layer),
            pl.BlockSpec((None, 1, CONV_W), layer),
            pl.BlockSpec((None, 1, CONV_W), layer),
            pl.BlockSpec((None, 1, CONV_W), layer),
            pl.BlockSpec((None, CONV_W, CONV_W), layer),
        ],
        out_specs=pl.BlockSpec((seq, CONV_W), lambda b: (b, 0)),
        out_shape=jax.ShapeDtypeStruct((nbatch * seq, CONV_W), _BF),
        scratch_shapes=[pltpu.VMEM((seq + 2 * CONV_PAD, CONV_W), _F32), pltpu.VMEM((seq, CONV_W), _BF),
                        pltpu.VMEM((8, _conv_rows(seq)[1], CONV_W), _F32)],
        compiler_params=_params(("arbitrary",)),
        name="conv_module_%d" % seq,
    )(uc, dw, vec(dwb), vec(lg), vec(lb), pw)


def _dft_constants(seq):
    half = seq // 2
    j = np.arange(half, dtype=np.int64)
    ang = 2.0 * np.pi * ((j[:, None] * j[None, :]) % half).astype(np.float64) / half
    m = np.concatenate([np.cos(ang), -np.sin(ang)], axis=1) / np.sqrt(seq)
    tw = np.pi * j.astype(np.float64) / half
    cb = np.repeat(np.cos(tw)[:, None], FNET_W, axis=1)
    sb = np.repeat(np.sin(tw)[:, None], FNET_W, axis=1)
    c = np.arange(FNET_GW, dtype=np.int64)
    angc = 2.0 * np.pi * ((c[:, None] * c[None, :]) % FNET_GW).astype(np.float64) / FNET_GW
    eye = np.eye(FNET_GROUPS)
    w1 = np.concatenate([np.kron(eye, np.cos(angc)), np.kron(eye, np.sin(angc))], axis=1) / np.sqrt(FNET_GW)
    return tuple(jnp.asarray(a, dtype=_F32) for a in (m, cb, sb, w1))


def _fnet_kernel(half, group, u_ref, m_ref, cb_ref, sb_ref, w1_ref, fw_ref, o_ref, rhs_ref):
    w1 = w1_ref[...].astype(_BF)
    te = jnp.dot(u_ref[:, :FNET_W], w1, preferred_element_type=_F32)
    to = jnp.dot(u_ref[:, FNET_W:], w1, preferred_element_type=_F32)
    for g in range(group):
        rows = slice(g * half, (g + 1) * half)
        ec, es = te[rows, :FNET_W], te[rows, FNET_W:]
        oc, os_ = to[rows, :FNET_W], to[rows, FNET_W:]
        for col, (top, bot) in enumerate(((ec, es), (oc, os_), (os_, -oc))):
            lanes = slice((3 * g + col) * FNET_W, (3 * g + col + 1) * FNET_W)
            rhs_ref[0:half, lanes] = top.astype(_BF)
            rhs_ref[half:2 * half, lanes] = bot.astype(_BF)
    pqr = jnp.dot(m_ref[...].astype(_BF), rhs_ref[...], preferred_element_type=_F32)
    cb, sb = cb_ref[...], sb_ref[...]
    ys = []
    for g in range(group):
        p, q, r = (pqr[:, (3 * g + col) * FNET_W:(3 * g + col + 1) * FNET_W] for col in range(3))
        rot = cb * q - sb * r
        ys += [(p + rot).astype(_BF), (p - rot).astype(_BF)]
    o_ref[...] = jnp.dot(jnp.concatenate(ys, axis=0), fw_ref[...].astype(_BF),
                         preferred_element_type=_F32).astype(o_ref.dtype)


def _fourier_mix(u_pairs, seq, nbatch, row0, l, fw):
    half = seq // 2
    group = max(1, 512 // half)
    rows = group * half
    m, cb, sb, w1 = _dft_constants(seq)
    const = lambda b: (0, 0)
    return pl.pallas_call(
        functools.partial(_fnet_kernel, half, group),
        grid=(nbatch // group,),
        in_specs=[
            pl.BlockSpec((rows, 2 * FNET_W), lambda b: (row0 // rows + b, 0)),
            pl.BlockSpec((half, 2 * half), const),
            pl.BlockSpec((half, FNET_W), const),
            pl.BlockSpec((half, FNET_W), const),
            pl.BlockSpec((FNET_W, 2 * FNET_W), const),
            pl.BlockSpec((None, FNET_W, FNET_W), lambda b: (l, 0, 0)),
        ],
        out_specs=pl.BlockSpec((2 * rows, FNET_W), lambda b: (b, 0)),
        out_shape=jax.ShapeDtypeStruct((nbatch * seq, FNET_W), _BF),
        scratch_shapes=[pltpu.VMEM((2 * half, 3 * group * FNET_W), _BF)],
        compiler_params=_params(("arbitrary",)),
        name="fourier_mix_%d" % seq,
    )(u_pairs, m, cb, sb, w1, fw)


def _route_top2(r_ref, h, ei_ref, ew_ref):
    lg = lax.dot_general(r_ref[...].astype(_BF), h.astype(_BF), _NT, preferred_element_type=_F32)
    eid = lax.broadcasted_iota(jnp.int32, lg.shape, 0)
    m1 = jnp.max(lg, axis=0, keepdims=True)
    i1 = jnp.min(jnp.where(lg == m1, eid, N_EXPERTS), axis=0, keepdims=True)
    lg2 = jnp.where(eid == i1, -jnp.inf, lg)
    m2 = jnp.max(lg2, axis=0, keepdims=True)
    i2 = jnp.min(jnp.where(lg2 == m2, eid, N_EXPERTS), axis=0, keepdims=True)
    e = jnp.exp(m2 - m1)
    ei_ref[0:1, :] = i1
    ei_ref[1:2, :] = i2
    ew_ref[0:1, :] = 1.0 / (1.0 + e)
    ew_ref[1:2, :] = e / (1.0 + e)


def _outproj_kernel(split, route, *refs):
    branch_refs, refs = refs[:6], refs[6:]
    n_x = 2 if split else 1
    x_refs, refs = refs[:n_x], refs[n_x:]
    mod_ref, w_ref, gf_ref = refs[:3]
    refs = refs[3:]
    if route:
        r_ref, xo_ref, h_ref, ei_ref, ew_ref, wb_ref = refs
    else:
        xo_ref, h_ref, wb_ref = refs
    i = pl.program_id(0)

    @pl.when(i == 0)
    def _():
        wb_ref[...] = w_ref[...].astype(_BF)

    mix = jnp.concatenate([_pick_half(i, ROW_TILE, branch_refs[2 * n], branch_refs[2 * n + 1])
                           for n in range(3)], axis=-1)
    x = _pick_half(i, ROW_TILE, *x_refs) if split else x_refs[0][...]
    x_new = x + mod_ref[2:3, :] * jnp.dot(mix, wb_ref[...], preferred_element_type=_F32)
    xo_ref[...] = x_new
    h = _norm_mod(x_new, gf_ref[...], mod_ref[3:4, :], mod_ref[4:5, :])
    h_ref[...] = h.astype(h_ref.dtype)
    if route:
        _route_top2(r_ref, h, ei_ref, ew_ref)


def _outproj(branches, xs, l, mods, w_all, gf_all, router_t=None):
    split = len(xs) == 2
    route = router_t is not None
    row = lambda i: (i, 0)
    in_specs = []
    args = []
    for pair, width in zip(branches, (ATTN_W, CONV_W, FNET_W)):
        in_specs += _half_specs(width, ROW_TILE)
        args += list(pair)
    in_specs += _half_specs(D_MODEL, ROW_TILE) if split else [pl.BlockSpec((ROW_TILE, D_MODEL), row)]
    args += list(xs)
    mod_spec, gf_spec, w_spec = _layer_specs(l, (D_MODEL, D_MODEL))
    in_specs += [mod_spec, w_spec, gf_spec]
    args += [mods, w_all, gf_all.reshape(DEPTH, 1, D_MODEL)]
    out_specs = [pl.BlockSpec((ROW_TILE, D_MODEL), row)] * 2
    out_shape = [jax.ShapeDtypeStruct((T, D_MODEL), _F32), jax.ShapeDtypeStruct((T, D_MODEL), _BF)]
    if route:
        in_specs.append(pl.BlockSpec((N_EXPERTS, D_MODEL), lambda i: (0, 0)))
        args.append(router_t)
        out_specs += [pl.BlockSpec((TOP_K, ROW_TILE), lambda i: (0, i))] * 2
        out_shape += [jax.ShapeDtypeStruct((TOP_K, T), jnp.int32), jax.ShapeDtypeStruct((TOP_K, T), _F32)]
    return pl.pallas_call(
        functools.partial(_outproj_kernel, split, route),
        grid=(T // ROW_TILE,),
        in_specs=in_specs,
        out_specs=out_specs,
        out_shape=out_shape,
        scratch_shapes=[pltpu.VMEM((D_MODEL, D_MODEL), _BF)],
        compiler_params=_params(("arbitrary",)),
        name="outproj_route" if route else "outproj",
    )(*args)


def _ffn_kernel(nj, n_sub_rows, vis_e, vis_start, vis_cnt, used_sub,
                x_hbm, wg_ref, wu_ref, wd_ref, y_hbm,
                big, xb, sem_in, sem_out):
    del vis_e
    v = pl.program_id(0)
    j = pl.program_id(1)
    cnt = vis_cnt[v]
    row0 = vis_start[v] * SUB

    x_is_bf16 = x_hbm.dtype == _BF
    landing = xb if x_is_bf16 else big

    def copy_in(s):
        r = pl.multiple_of(row0 + s * SUB, SUB)
        b = pl.multiple_of(s * SUB, SUB)
        return pltpu.make_async_copy(x_hbm.at[pl.ds(r, SUB)], landing.at[pl.ds(b, SUB)], sem_in.at[s])

    def copy_out(s):
        r = pl.multiple_of(row0 + s * SUB, SUB)
        b = pl.multiple_of(s * SUB, SUB)
        return pltpu.make_async_copy(big.at[pl.ds(b, SUB)], y_hbm.at[pl.ds(r, SUB)], sem_out)

    def for_subs(fn):
        lax.fori_loop(0, cnt, lambda s, c: (fn(s), c)[1], 0)

    @pl.when(jnp.logical_and(j == 0, cnt > 0))
    def _():
        for_subs(lambda s: copy_in(s).start())

    @pl.when(cnt > 0)
    def _():

        def block(r0, nrows):
            s0 = r0 // SUB

            @pl.when(j == 0)
            def _():
                for t in range(nrows // SUB):
                    copy_in(s0 + t).wait()
                    rs = pl.ds(pl.multiple_of(r0 + t * SUB, SUB), SUB)
                    if not x_is_bf16:
                        xb[rs, :] = big[rs, :].astype(_BF)
                    big[rs, :] = jnp.zeros((SUB, D_MODEL), _F32)

            rs = pl.ds(r0, nrows)
            x = xb[rs, :]
            a = jnp.dot(x, wg_ref[...].astype(_BF), preferred_element_type=_F32)
            b = jnp.dot(x, wu_ref[...].astype(_BF), preferred_element_type=_F32)
            p = (a * _sigmoid(a) * b).astype(_BF)
            big[rs, :] += jnp.dot(p, wd_ref[...].astype(_BF), preferred_element_type=_F32)

            @pl.when(j == nj - 1)
            def _():
                for t in range(nrows // SUB):
                    copy_out(s0 + t).start()

        n_big = cnt // BLK_MIN
        rem = cnt - n_big * BLK_MIN
        room = BLK_SUBS - BLK_MIN
        grow = jnp.minimum(rem, n_big * room)
        left = rem - grow

        def one_block(i, r0):
            size = BLK_MIN + jnp.clip(grow - i * room, 0, room)
            for sz in range(BLK_MIN, BLK_SUBS + 1):
                pl.when(size == sz)(functools.partial(block, pl.multiple_of(r0, SUB), sz * SUB))
            return r0 + size * SUB
        r0 = lax.fori_loop(0, n_big, one_block, 0)
        bit = BLK_MIN // 2
        while bit >= 1:
            take = (left // bit) % 2 == 1
            pl.when(take)(functools.partial(block, pl.multiple_of(r0, SUB), bit * SUB))
            r0 = r0 + jnp.where(take, bit * SUB, 0)
            bit //= 2

    @pl.when(jnp.logical_and(j == nj - 1, cnt > 0))
    def _():
        for_subs(lambda s: copy_out(s).wait())

    if x_hbm.dtype == y_hbm.dtype:
        @pl.when(jnp.logical_and(v == pl.num_programs(0) - 1, j == nj - 1))
        def _():
            def tail(s):
                r = pl.multiple_of(s * SUB, SUB)
                return pltpu.make_async_copy(x_hbm.at[pl.ds(r, SUB)], y_hbm.at[pl.ds(r, SUB)], sem_out)
            lax.fori_loop(used_sub[0], n_sub_rows, lambda s, c: (tail(s).start(), c)[1], 0)
            lax.fori_loop(used_sub[0], n_sub_rows, lambda s, c: (tail(s).wait(), c)[1], 0)


def _grouped_ffn(x, wg, wu, wd, fc, vis_e, vis_start, vis_cnt, used_sub):
    ff = wg.shape[-1]
    nj = ff // fc
    nv = vis_e.shape[0]
    n_rows = x.shape[0]

    def chunk(v, j, vc):
        return jnp.where(vc[v] > 0, j, nj - 1)

    return pl.pallas_call(
        functools.partial(_ffn_kernel, nj, n_rows // SUB),
        grid_spec=pltpu.PrefetchScalarGridSpec(
            num_scalar_prefetch=4,
            grid=(nv, nj),
            in_specs=[
                pl.BlockSpec(memory_space=pl.ANY),
                pl.BlockSpec((None, D_MODEL, fc), lambda v, j, ve, vs, vc, us: (ve[v], 0, chunk(v, j, vc))),
                pl.BlockSpec((None, D_MODEL, fc), lambda v, j, ve, vs, vc, us: (ve[v], 0, chunk(v, j, vc))),
                pl.BlockSpec((None, fc, D_MODEL), lambda v, j, ve, vs, vc, us: (ve[v], chunk(v, j, vc), 0)),
            ],
            out_specs=pl.BlockSpec(memory_space=pl.ANY),
            scratch_shapes=[
                pltpu.VMEM((TMAX, D_MODEL), _F32),
                pltpu.VMEM((TMAX, D_MODEL), _BF),
                pltpu.SemaphoreType.DMA((SUB_MAX,)),
                pltpu.SemaphoreType.DMA(()),
            ],
        ),
        out_shape=jax.ShapeDtypeStruct((n_rows, D_MODEL), _F32),
        compiler_params=_params(("arbitrary", "arbitrary"), FFN_VMEM_LIMIT),
        name="ffn_%d" % ff,
    )(vis_e, vis_start, vis_cnt, used_sub, x, wg, wu, wd)


def _dense_visits():
    nv = T // TMAX
    return (jnp.zeros((nv,), jnp.int32),
            jnp.arange(nv, dtype=jnp.int32) * SUB_MAX,
            jnp.full((nv,), SUB_MAX, jnp.int32),
            jnp.full((1,), T // SUB, jnp.int32))


N_SLOT_SUB = -(-(TOP_K * T + N_TILES * N_EXPERTS * (SEG_ALIGN - 1)) // SUB) + N_EXPERTS
N_SLOT = N_SLOT_SUB * SUB
N_VISIT = N_SLOT_SUB // SUB_MAX + N_EXPERTS


def _routing_plan(eidx):
    e_loc = eidx.reshape(TOP_K, N_TILES, DISP_TILE).transpose(1, 0, 2).reshape(N_TILES, PAIRS)
    onehot = e_loc[:, :, None] == jnp.arange(N_EXPERTS, dtype=jnp.int32)[None, None, :]
    ch = 128
    oh = onehot.astype(_F32).reshape(N_TILES, PAIRS // ch, ch, N_EXPERTS)
    tri = (jnp.arange(ch)[:, None] >= jnp.arange(ch)[None, :]).astype(_F32)
    within = jnp.einsum("ij,tcjk->tcik", tri, oh)
    tot = within[:, :, -1, :]
    csum = (within + (jnp.cumsum(tot, axis=1) - tot)[:, :, None, :]).reshape(N_TILES, PAIRS, N_EXPERTS)
    csum = csum.astype(jnp.int32)
    n_te = (csum[:, -1, :] + SEG_ALIGN - 1) // SEG_ALIGN * SEG_ALIGN
    src = jnp.cumsum(n_te, axis=1) - n_te
    lpos = jnp.sum(jnp.where(onehot, csum - 1 + src[:, None, :], 0), axis=2).astype(jnp.int32)
    counts = jnp.sum(n_te, axis=0)
    nsub = (counts + SUB - 1) // SUB
    sub_base = jnp.cumsum(nsub) - nsub
    dst = (sub_base * SUB)[None, :] + jnp.cumsum(n_te, axis=0) - n_te
    seg = tuple(a.reshape(-1).astype(jnp.int32) for a in (n_te, src, dst))
    pads = ((sub_base * SUB + counts).astype(jnp.int32), (nsub * SUB - counts).astype(jnp.int32))
    used_sub = jnp.sum(nsub).reshape(1).astype(jnp.int32)
    nvis = (nsub + SUB_MAX - 1) // SUB_MAX
    vend = jnp.cumsum(nvis)
    total = vend[-1]
    vid = jnp.arange(N_VISIT, dtype=jnp.int32)
    ve = jnp.minimum(jnp.sum((vid[:, None] >= vend[None, :]).astype(jnp.int32), axis=1), N_EXPERTS - 1)
    local = vid - (vend - nvis)[ve]
    nv_e = jnp.maximum(nvis[ve], 1)
    q, r = nsub[ve] // nv_e, nsub[ve] % nv_e
    cnt = q + (local < r).astype(jnp.int32)
    start = sub_base[ve] + local * q + jnp.minimum(local, r)
    used = vid < total
    last_e = ve[jnp.maximum(total - 1, 0)]
    vis_e = jnp.where(used, ve, last_e).astype(jnp.int32)
    vis_cnt = jnp.where(used, cnt, 0).astype(jnp.int32)
    vis_start = jnp.where(used, start, 0).astype(jnp.int32)
    return lpos.reshape(N_TILES, TOP_K, DISP_TILE), seg, pads, (vis_e, vis_start, vis_cnt, used_sub)


def _for_pow2_pieces(n, max_piece, fn):
    off = 0
    bit = max_piece
    while bit >= SEG_ALIGN:
        take = (n // bit) % 2 == 1
        pl.when(take)(functools.partial(fn, off, bit))
        off = off + jnp.where(take, bit, 0)
        bit //= 2


def _segment_copies(tile, seg, make_copy, wait=False):
    n_te, src, dst = seg
    for e in range(N_EXPERTS):
        j = tile * N_EXPERTS + e
        s0, d0 = src[j], dst[j]

        def piece(off, size, s0=s0, d0=d0):
            cp = make_copy(pl.multiple_of(s0 + off, SEG_ALIGN), pl.multiple_of(d0 + off, SEG_ALIGN), size)
            cp.wait() if wait else cp.start()
        _for_pow2_pieces(n_te[j], DISP_TILE, piece)


def _dispatch_kernel(n_te, src, dst, pad_start, pad_cnt, used_sub, h_ref, lpos_ref, xs_hbm, ring, ring_sems, sem):
    i = pl.program_id(0)
    last = pl.num_programs(0) - 1
    b = i % 2

    def copies(tile, slot_id, wait):
        _segment_copies(tile, (n_te, src, dst), lambda s, d, size: pltpu.make_async_copy(
            ring.at[slot_id, pl.ds(s, size)], xs_hbm.at[pl.ds(d, size)], ring_sems.at[slot_id]), wait)

    pl.when(i >= 2)(lambda: copies(i - 2, b, True))
    lp = lpos_ref[...]
    srow = lax.broadcasted_iota(jnp.int32, (SORT_ROWS, DISP_TILE), 0)
    perm = jnp.where(lp[0:1, :] == srow, 1.0, jnp.where(lp[1:2, :] == srow, 1.0, 0.0)).astype(_BF)
    ring[b] = jnp.dot(perm, h_ref[...].astype(_BF), preferred_element_type=_F32)
    copies(i, b, False)

    @pl.when(i == last)
    def _():
        copies(i - 1, 1 - b, True)
        copies(i, b, True)
        for e in range(N_EXPERTS):
            p0 = pad_start[e]

            def fill(off, size, p0=p0):
                cp = pltpu.make_async_copy(ring.at[b, pl.ds(0, size)],
                                           xs_hbm.at[pl.ds(pl.multiple_of(p0 + off, SEG_ALIGN), size)], sem)
                cp.start()
                cp.wait()
            _for_pow2_pieces(pad_cnt[e], SUB // 2, fill)

        def tail(s):
            r = pl.multiple_of(s * SUB, SUB)
            return pltpu.make_async_copy(ring.at[b, pl.ds(0, SUB)], xs_hbm.at[pl.ds(r, SUB)], sem)
        lax.fori_loop(used_sub[0], N_SLOT_SUB, lambda s, c: (tail(s).start(), c)[1], 0)
        lax.fori_loop(used_sub[0], N_SLOT_SUB, lambda s, c: (tail(s).wait(), c)[1], 0)


def _dispatch(h, lpos, seg, pads, used_sub):
    return pl.pallas_call(
        _dispatch_kernel,
        grid_spec=pltpu.PrefetchScalarGridSpec(
            num_scalar_prefetch=6,
            grid=(N_TILES,),
            in_specs=[pl.BlockSpec((DISP_TILE, D_MODEL), lambda i, *_: (i, 0)),
                      pl.BlockSpec((None, TOP_K, DISP_TILE), lambda i, *_: (i, 0, 0))],
            out_specs=pl.BlockSpec(memory_space=pl.ANY),
            scratch_shapes=[pltpu.VMEM((2, SORT_ROWS, D_MODEL), _F32),
                            pltpu.SemaphoreType.DMA((2,)), pltpu.SemaphoreType.DMA(())],
        ),
        out_shape=jax.ShapeDtypeStruct((N_SLOT, D_MODEL), _F32),
        compiler_params=_params(("arbitrary",)),
        name="dispatch",
    )(*seg, *pads, used_sub, h, lpos)


def _residual_out(x, gate, f, g_ref, o_refs, i, tile):
    out = x + gate * f
    if g_ref is not None:
        ms = jnp.mean(out * out, axis=-1, keepdims=True)
        out = out * lax.rsqrt(ms + EPS) * g_ref[...]
    if len(o_refs) == 1:
        o_refs[0][...] = out
    else:
        @pl.when(i < T_PROMPT // tile)
        def _():
            o_refs[0][...] = out

        @pl.when(i >= T_PROMPT // tile)
        def _():
            o_refs[1][...] = out


def _out_specs(final, tile):
    if not final:
        return ([pl.BlockSpec((tile, D_MODEL), lambda i, *_: (i, 0))],
                [jax.ShapeDtypeStruct((T, D_MODEL), _F32)])
    return (_half_specs(D_MODEL, tile),
            [jax.ShapeDtypeStruct((T_PROMPT, D_MODEL), _F32), jax.ShapeDtypeStruct((T_SAMPLE, D_MODEL), _F32)])


def _combine_kernel(final, x_ref, mod_ref, y_ref, *rest):
    g_ref, o_refs = (rest[0], rest[1:]) if final else (None, rest)
    _residual_out(x_ref[...], mod_ref[5:6, :], y_ref[...], g_ref, o_refs, pl.program_id(0), ROW_TILE)


def _combine(x, l, mods, y, g_final=None):
    final = g_final is not None
    row = lambda i: (i, 0)
    in_specs = [
        pl.BlockSpec((ROW_TILE, D_MODEL), row),
        _layer_specs(l, ())[0],
        pl.BlockSpec((ROW_TILE, D_MODEL), row),
    ]
    args = [x, mods, y]
    if final:
        in_specs.append(pl.BlockSpec((1, D_MODEL), lambda i: (0, 0)))
        args.append(g_final.reshape(1, D_MODEL))
    out_specs, out_shape = _out_specs(final, ROW_TILE)
    return pl.pallas_call(
        functools.partial(_combine_kernel, final),
        grid=(T // ROW_TILE,),
        in_specs=in_specs,
        out_specs=out_specs,
        out_shape=out_shape,
        compiler_params=_params(("arbitrary",)),
        name="combine%s" % ("_final" if final else ""),
    )(*args)


def _combine_top2_kernel(final, n_out, n_te, src, dst, x_ref, mod_ref, w_ref, lpos_ref, *rest):
    g_ref, rest = (rest[0], rest[1:]) if final else (None, rest)
    ys_hbm, o_refs, (ybuf, sems) = rest[0], rest[1:1 + n_out], rest[1 + n_out:]
    i = pl.program_id(0)

    def fetch(tile, b, wait=False):
        _segment_copies(tile, (n_te, src, dst), lambda s, d, size: pltpu.make_async_copy(
            ys_hbm.at[pl.ds(d, size)], ybuf.at[b, pl.ds(s, size)], sems.at[b]), wait)

    @pl.when(i == 0)
    def _():
        for slot_id in range(2):
            ybuf[slot_id, PAIRS:SORT_ROWS, :] = jnp.zeros((SORT_ROWS - PAIRS, D_MODEL), _F32)
        fetch(0, 0)

    @pl.when(i + 1 < pl.num_programs(0))
    def _():
        fetch(i + 1, (i + 1) % 2)

    b = i % 2
    fetch(i, b, wait=True)
    y = ybuf[b].astype(_BF)
    lp = lpos_ref[...]
    scol = lax.broadcasted_iota(jnp.int32, (DISP_TILE, SORT_ROWS), 1)
    w = w_ref[...]
    pick = jnp.zeros((DISP_TILE, SORT_ROWS), _F32)
    for k in range(TOP_K):
        pick = jnp.where(lp[:, k:k + 1] == scol, w[:, k:k + 1], pick)
    f = jnp.dot(pick.astype(_BF), y, preferred_element_type=_F32)
    _residual_out(x_ref[...], mod_ref[5:6, :], f, g_ref, o_refs, i, DISP_TILE)


def _combine_top2(x, l, mods, ys, lpos_t, seg, w, g_final=None):
    final = g_final is not None
    row = lambda i, *_: (i, 0)
    in_specs = [
        pl.BlockSpec((DISP_TILE, D_MODEL), row),
        pl.BlockSpec((None, None, 6, D_MODEL), lambda i, *_: (l, _cond_of_tile(i, DISP_TILE), 0, 0)),
        pl.BlockSpec((DISP_TILE, TOP_K), row),
        pl.BlockSpec((None, DISP_TILE, TOP_K), lambda i, *_: (i, 0, 0)),
    ]
    args = [x, mods, w, lpos_t]
    if final:
        in_specs.append(pl.BlockSpec((1, D_MODEL), lambda i, *_: (0, 0)))
        args.append(g_final.reshape(1, D_MODEL))
    in_specs.append(pl.BlockSpec(memory_space=pl.ANY))
    args.append(ys)
    out_specs, out_shape = _out_specs(final, DISP_TILE)
    return pl.pallas_call(
        functools.partial(_combine_top2_kernel, final, len(out_specs)),
        grid_spec=pltpu.PrefetchScalarGridSpec(
            num_scalar_prefetch=3,
            grid=(N_TILES,),
            in_specs=in_specs,
            out_specs=out_specs,
            scratch_shapes=[
                pltpu.VMEM((2, SORT_ROWS, D_MODEL), _F32),
                pltpu.SemaphoreType.DMA((2,)),
            ],
        ),
        out_shape=out_shape,
        compiler_params=_params(("arbitrary",)),
        name="combine_top2%s" % ("_final" if final else ""),
    )(*seg, *args)


def kernel(x_prompt, x_sample, cache_k, cache_v, c, c_ctx, w_ada, b_ada, g_norm_mix, g_norm_ffn,
           w_in, w_out, attn_sink, conv_dw, conv_dw_b, conv_ln_g, conv_ln_b, conv_pw, fnet_w,
           ffn_w_gate, ffn_w_up, ffn_w_down, moe_router, moe_w_gate, moe_w_up, moe_w_down, g_final):
    xs = (x_prompt.reshape(T_PROMPT, D_MODEL), x_sample.reshape(T_SAMPLE, D_MODEL))
    cond8 = jnp.concatenate([c_ctx[None, :], c, jnp.zeros((N_COND - 1 - DEC_BATCH, D_MODEL), _F32)], axis=0)
    mods = _modulation(cond8, w_ada, b_ada).reshape(DEPTH, N_COND, 6, D_MODEL)
    cos_t, sin_t = _rope_tables()
    ck_all = cache_k.reshape(DEC_BATCH, DEPTH, PAST_LEN, KV_W)
    cv_all = cache_v.reshape(DEC_BATCH, DEPTH, PAST_LEN, KV_W)
    p_blocks = T_PROMPT // DEC_SEQ
    sinks = attn_sink.reshape(DEPTH * N_HEADS)

    ks, vs = [], []
    resid = None
    for l in range(DEPTH):
        q, kp, vp, kl, vl, uc, uf, *x_new = _inproj(xs, l, mods, g_norm_mix, w_in, cos_t, sin_t, resid=resid)
        if resid is not None:
            xs, resid = tuple(x_new), None
        ks.append(kp)
        vs.append(vp)
        attn = (_ctx_attention(l, sinks, q, kp, vp),
                _lat_attention(l, sinks, q, kl, vl, ck_all, cv_all))
        cargs = (l, conv_dw, conv_dw_b, conv_ln_g, conv_ln_b, conv_pw)
        conv = (_conv_module(uc, SEQ, BATCH, 0, *cargs),
                _conv_module(uc, DEC_SEQ, DEC_BATCH, p_blocks, *cargs))
        four = (_fourier_mix(uf, SEQ, BATCH, 0, l, fnet_w),
                _fourier_mix(uf, DEC_SEQ, DEC_BATCH, T_PROMPT // 2, l, fnet_w))
        last = g_final if l == DEPTH - 1 else None
        i = l // 2
        if l % 2 == 0:
            x, h = _outproj((attn, conv, four), xs, l, mods, w_out, g_norm_ffn)
            y = _grouped_ffn(h, ffn_w_gate[i:i + 1], ffn_w_up[i:i + 1], ffn_w_down[i:i + 1], 256,
                             *_dense_visits())
            if last is None:
                xs, resid = (x,), (y, l)
            else:
                xs = tuple(_combine(x, l, mods, y, g_final=last))
        else:
            x, h, eidx, ew = _outproj((attn, conv, four), xs, l, mods, w_out, g_norm_ffn,
                                      router_t=moe_router[i].T)
            lpos, seg, pads, visits = _routing_plan(eidx)
            xd = _dispatch(h, lpos, seg, pads, visits[-1])
            ys = _grouped_ffn(xd, moe_w_gate[i], moe_w_up[i], moe_w_down[i], 512, *visits)
            xs = tuple(_combine_top2(x, l, mods, ys, lpos.transpose(0, 2, 1), seg, ew.T, g_final=last))

    y_prompt = xs[0].reshape(BATCH, SEQ, D_MODEL)
    y_sample = xs[1].reshape(DEC_BATCH, DEC_SEQ, D_MODEL)
    state_k = jnp.stack([a.reshape(BATCH, SEQ, N_KV_HEADS, HEAD_DIM) for a in ks], axis=1)
    state_v = jnp.stack([a.reshape(BATCH, SEQ, N_KV_HEADS, HEAD_DIM) for a in vs], axis=1)
    return (y_prompt, y_sample, state_k, state_v)
```
